```python
import jax
import jax.numpy as jnp
from jax import lax
import numpy as np

D_MODEL = 1024
BATCH = 2
SEQ = 8192
DEPTH = 1

EPS = 1e-6
SSD_HEADS = 8
SSD_HEAD_DIM = 64
SSD_WIDTH = SSD_HEADS * SSD_HEAD_DIM
SSD_GROUPS = 2
SSD_STATE = 128
SSD_CONV = 4
SSD_CHUNK = 128
SSD_CONV_DIM = SSD_WIDTH + 2 * SSD_GROUPS * SSD_STATE
MLA_HEADS = 8
MLA_Q_RANK = 256
MLA_KV_RANK = 128
MLA_NOPE = 64
MLA_ROPE = 32
MLA_V = 64
MLA_QK = MLA_NOPE + MLA_ROPE
MLA_WIDTH = MLA_HEADS * MLA_V
ROPE_THETA = 10000.0
ATTN_BLOCK = 128
MIX_WIDTH = SSD_WIDTH + MLA_WIDTH
IN_SPLITS = (SSD_WIDTH, SSD_CONV_DIM, SSD_HEADS, MLA_Q_RANK, MLA_KV_RANK, MLA_ROPE)
IN_WIDTH = SSD_WIDTH + SSD_CONV_DIM + SSD_HEADS + MLA_Q_RANK + MLA_KV_RANK + MLA_ROPE
N_EXPERTS = 32
TOP_K = 4
D_FF = 1024
SWIGLU_LIMIT = 7.0
SWIGLU_ALPHA = 1.702
MOE_BLOCK = 128

kernel_name = 'hybrid_ssd_mla_moe_layer'


def _split_points(widths):
    return tuple(int(v) for v in np.cumsum(widths)[:-1])


def rms_norm(x, g):
    xf = x.astype(jnp.float32)
    y = xf * lax.rsqrt(jnp.mean(xf * xf, axis=-1, keepdims=True) + EPS)
    return (y * g.astype(jnp.float32)).astype(x.dtype)


def causal_depthwise_conv(u, w, b):
    ch = u.shape[-1]
    out = lax.conv_general_dilated(
        u, w[:, None, :].astype(u.dtype), window_strides=(1,),
        padding=((SSD_CONV - 1, 0),), dimension_numbers=('NWC', 'WIO', 'NWC'),
        feature_group_count=ch)
    return out + b.astype(u.dtype)


def ssd_chunked(x, dt, a, b_in, c_in):
    bsz, seqlen, nh, hp = x.shape
    g, n = b_in.shape[2], b_in.shape[3]
    j = nh // g
    q = SSD_CHUNK
    nc = seqlen // q
    xd = (x * dt[..., None]).reshape(bsz, nc, q, g, j, hp)
    a_dt = (dt * a).reshape(bsz, nc, q, g, j)
    bc = b_in.reshape(bsz, nc, q, g, n)
    cc = c_in.reshape(bsz, nc, q, g, n)
    a_cum = jnp.cumsum(a_dt, axis=2)
    seg = a_cum[:, :, :, None] - a_cum[:, :, None, :]
    causal = jnp.tril(jnp.ones((q, q), dtype=bool))[None, None, :, :, None, None]
    decay = jnp.exp(jnp.where(causal, seg, -jnp.inf))
    scores = jnp.einsum('bclgn,bcsgn->bclsg', cc, bc)
    y_diag = jnp.einsum('bclsgj,bcsgjp->bclgjp', scores[..., None] * decay, xd)
    decay_to_end = jnp.exp(a_cum[:, :, -1:] - a_cum)
    states = jnp.einsum('bcsgn,bcsgj,bcsgjp->bcgjpn', bc, decay_to_end, xd)
    chunk_decay = jnp.exp(a_cum[:, :, -1])

    def step(h, inp):
        s_c, d_c = inp
        return h * d_c[..., None, None] + s_c, h

    h0 = jnp.zeros((bsz, g, j, hp, n), dtype=x.dtype)
    _, prev = lax.scan(step, h0, (jnp.swapaxes(states, 0, 1), jnp.swapaxes(chunk_decay, 0, 1)))
    prev = jnp.swapaxes(prev, 0, 1)
    y_off = jnp.einsum('bclgn,bcgjpn,bclgj->bclgjp', cc, prev, jnp.exp(a_cum))
    return (y_diag + y_off).reshape(bsz, seqlen, nh, hp)


def ssd_mixer(z, xbc, dt_raw, conv_w, conv_b, dt_bias, a_log, d_skip, norm_g):
    bsz, seqlen, _ = z.shape
    xbc = jax.nn.silu(causal_depthwise_conv(xbc, conv_w, conv_b))
    xs, bs, cs = jnp.split(xbc, _split_points((SSD_WIDTH, SSD_GROUPS * SSD_STATE, SSD_GROUPS * SSD_STATE)), axis=-1)
    xh = xs.reshape(bsz, seqlen, SSD_HEADS, SSD_HEAD_DIM).astype(jnp.float32)
    bm = bs.reshape(bsz, seqlen, SSD_GROUPS, SSD_STATE).astype(jnp.float32)
    cm = cs.reshape(bsz, seqlen, SSD_GROUPS, SSD_STATE).astype(jnp.float32)
    dt = jax.nn.softplus(dt_raw.astype(jnp.float32) + dt_bias.astype(jnp.float32))
    a = -jnp.exp(a_log.astype(jnp.float32))
    y = ssd_chunked(xh, dt, a, bm, cm) + d_skip.astype(jnp.float32)[:, None] * xh
    y = y.reshape(bsz, seqlen, SSD_WIDTH) * jax.nn.silu(z.astype(jnp.float32))
    yg = y.reshape(bsz, seqlen, SSD_GROUPS, SSD_WIDTH // SSD_GROUPS)
    yg = yg * lax.rsqrt(jnp.mean(yg * yg, axis=-1, keepdims=True) + EPS)
    y = yg.reshape(bsz, seqlen, SSD_WIDTH) * norm_g.astype(jnp.float32)
    return y.astype(z.dtype)


def rope_tables(positions):
    inv_freq = ROPE_THETA ** (-jnp.arange(0, MLA_ROPE, 2, dtype=jnp.float32) / MLA_ROPE)
    ang = positions.astype(jnp.float32)[..., None] * inv_freq
    return jnp.cos(ang), jnp.sin(ang)


def apply_rope(x, cos, sin):
    x1, x2 = jnp.split(x, 2, axis=-1)
    cos = cos.astype(x.dtype)
    sin = sin.astype(x.dtype)
    return jnp.concatenate([x1 * cos - x2 * sin, x2 * cos + x1 * sin], axis=-1)


def mla_mixer(c_q, c_kv, k_rope, positions, q_norm_g, w_uq, kv_norm_g, w_ukv):
    bsz, seqlen, _ = c_q.shape
    q = (rms_norm(c_q, q_norm_g) @ w_uq).reshape(bsz, seqlen, MLA_HEADS, MLA_QK)
    q_nope, q_rope = jnp.split(q, (MLA_NOPE,), axis=-1)
    kv = (rms_norm(c_kv, kv_norm_g) @ w_ukv).reshape(bsz, seqlen, MLA_HEADS, MLA_NOPE + MLA_V)
    k_nope, v = jnp.split(kv, (MLA_NOPE,), axis=-1)
    cos, sin = rope_tables(positions)
    q_rope = apply_rope(q_rope, cos[:, :, None, :], sin[:, :, None, :])
    k_rope = apply_rope(k_rope, cos, sin)
    scale = MLA_QK ** -0.5
    q_nope = q_nope * scale
    q_rope = q_rope * scale
    key_pos = jnp.arange(seqlen)

    def block(i):
        start = i * ATTN_BLOCK
        qn = lax.dynamic_slice_in_dim(q_nope, start, ATTN_BLOCK, axis=1)
        qr = lax.dynamic_slice_in_dim(q_rope, start, ATTN_BLOCK, axis=1)
        s = (jnp.einsum('bqhd,bkhd->bhqk', qn, k_nope)
             + jnp.einsum('bqhr,bkr->bhqk', qr, k_rope)).astype(jnp.float32)
        q_pos = start + jnp.arange(ATTN_BLOCK)
        mask = key_pos[None, :] <= q_pos[:, None]
        p = jax.nn.softmax(jnp.where(mask, s, -jnp.inf), axis=-1).astype(v.dtype)
        return jnp.einsum('bhqk,bkhd->bqhd', p, v)

    o = lax.map(block, jnp.arange(seqlen // ATTN_BLOCK))
    return jnp.transpose(o, (1, 0, 2, 3, 4)).reshape(bsz, seqlen, MLA_WIDTH)


def moe_ffn(h, w_router, b_router, w_gate_up, b_gate_up, w_down, b_down):
    bsz, seqlen, d = h.shape
    t = bsz * seqlen
    xf = h.reshape(t, d)
    logits = (xf @ w_router + b_router).astype(jnp.float32)
    top_vals, top_idx = lax.top_k(logits, TOP_K)
    gates = jax.nn.softmax(top_vals, axis=-1)
    na = t * TOP_K
    flat_e = top_idx.reshape(na).astype(jnp.int32)
    flat_tok = jnp.repeat(jnp.arange(t, dtype=jnp.int32), TOP_K)
    flat_w = gates.reshape(na)
    order = jnp.argsort(flat_e)
    sorted_e = flat_e[order]
    counts = jnp.zeros((N_EXPERTS,), jnp.int32).at[flat_e].add(1)
    padded = ((counts + MOE_BLOCK - 1) // MOE_BLOCK) * MOE_BLOCK
    start_unpad = jnp.cumsum(counts) - counts
    end_pad = jnp.cumsum(padded)
    start_pad = end_pad - padded
    dest = start_pad[sorted_e] + (jnp.arange(na, dtype=jnp.int32) - start_unpad[sorted_e])
    n_slots = na + N_EXPERTS * MOE_BLOCK
    n_blocks = n_slots // MOE_BLOCK
    slot_tok = jnp.full((n_slots,), t, jnp.int32).at[dest].set(flat_tok[order])
    slot_w = jnp.zeros((n_slots,), jnp.float32).at[dest].set(flat_w[order])
    block_start = jnp.arange(n_blocks, dtype=jnp.int32) * MOE_BLOCK
    block_e = jnp.minimum(jnp.sum(block_start[:, None] >= end_pad[None, :], axis=1), N_EXPERTS - 1)
    x_pad = jnp.concatenate([xf, jnp.zeros((1, d), xf.dtype)], axis=0)
    xs = x_pad[slot_tok].reshape(n_blocks, MOE_BLOCK, d)

    def expert_block(args):
        xb, e = args
        gu = xb @ w_gate_up[e] + b_gate_up[e]
        gate, up = jnp.split(gu, 2, axis=-1)
        gate = jnp.minimum(gate, SWIGLU_LIMIT)
        up = jnp.clip(up, -SWIGLU_LIMIT, SWIGLU_LIMIT)
        glu = gate * jax.nn.sigmoid(SWIGLU_ALPHA * gate)
        return ((up + 1.0) * glu) @ w_down[e] + b_down[e]

    ys = lax.map(expert_block, (xs, block_e)).reshape(n_slots, d)
    ys = ys * slot_w[:, None].astype(ys.dtype)
    out = jnp.zeros((t + 1, d), ys.dtype).at[slot_tok].add(ys)[:t]
    return out.reshape(bsz, seqlen, d)


def setup_inputs(seed: int = 0) -> dict:
    key = jax.random.key(seed)
    ks = jax.random.split(key, 32)
    f32 = jnp.float32

    def nrm(k, shape, scale):
        return jax.random.normal(k, shape, f32) * scale

    def gain(k, shape):
        return 1.0 + 0.02 * jax.random.normal(k, shape, f32)

    x = jax.random.normal(ks[0], (BATCH, SEQ, D_MODEL), f32)
    offsets = jax.random.randint(ks[1], (BATCH, 1), 0, 1024, dtype=jnp.int32)
    positions = jnp.arange(SEQ, dtype=jnp.int32)[None, :] + offsets
    dt0 = jnp.exp(jax.random.uniform(ks[6], (DEPTH, SSD_HEADS), f32, np.log(1e-3), np.log(1e-1)))
    dt_bias = dt0 + jnp.log(-jnp.expm1(-dt0))
    a_log = jnp.log(jax.random.uniform(ks[7], (DEPTH, SSD_HEADS), f32, 1.0, 16.0))
    return {
        'x': x,
        'positions': positions,
        'norm_mix_g': gain(ks[2], (DEPTH, D_MODEL)),
        'w_in': nrm(ks[3], (DEPTH, D_MODEL, IN_WIDTH), D_MODEL ** -0.5),
        'conv_w': nrm(ks[4], (DEPTH, SSD_CONV, SSD_CONV_DIM), SSD_CONV ** -0.5),
        'conv_b': nrm(ks[5], (DEPTH, SSD_CONV_DIM), 0.01),
        'dt_bias': dt_bias,
        'a_log': a_log,
        'd_skip': 1.0 + 0.1 * jax.random.normal(ks[8], (DEPTH, SSD_HEADS), f32),
        'ssd_norm_g': gain(ks[9], (DEPTH, SSD_WIDTH)),
        'q_norm_g': gain(ks[10], (DEPTH, MLA_Q_RANK)),
        'w_uq': nrm(ks[11], (DEPTH, MLA_Q_RANK, MLA_HEADS * MLA_QK), MLA_Q_RANK ** -0.5),
        'kv_norm_g': gain(ks[12], (DEPTH, MLA_KV_RANK)),
        'w_ukv': nrm(ks[13], (DEPTH, MLA_KV_RANK, MLA_HEADS * (MLA_NOPE + MLA_V)), MLA_KV_RANK ** -0.5),
        'w_out': nrm(ks[14], (DEPTH, MIX_WIDTH, D_MODEL), MIX_WIDTH ** -0.5),
        'norm_ffn_g': gain(ks[15], (DEPTH, D_MODEL)),
        'w_router': nrm(ks[16], (DEPTH, D_MODEL, N_EXPERTS), D_MODEL ** -0.5),
        'b_router': nrm(ks[17], (DEPTH, N_EXPERTS), 0.01),
        'w_gate_up': nrm(ks[18], (DEPTH, N_EXPERTS, D_MODEL, 2 * D_FF), D_MODEL ** -0.5),
        'b_gate_up': nrm(ks[19], (DEPTH, N_EXPERTS, 2 * D_FF), 0.01),
        'w_down': nrm(ks[20], (DEPTH, N_EXPERTS, D_FF, D_MODEL), D_FF ** -0.5),
        'b_down': nrm(ks[21], (DEPTH, N_EXPERTS, D_MODEL), 0.01),
        'norm_final_g': gain(ks[22], (D_MODEL,)),
    }


def reference(x, positions, norm_mix_g, w_in, conv_w, conv_b, dt_bias, a_log, d_skip, ssd_norm_g,
              q_norm_g, w_uq, kv_norm_g, w_ukv, w_out, norm_ffn_g, w_router, b_router,
              w_gate_up, b_gate_up, w_down, b_down, norm_final_g):
    for l in range(DEPTH):
        h = rms_norm(x, norm_mix_g[l])
        proj = h @ w_in[l]
        z, xbc, dt_raw, c_q, c_kv, k_rope = jnp.split(proj, _split_points(IN_SPLITS), axis=-1)
        y_ssd = ssd_mixer(z, xbc, dt_raw, conv_w[l], conv_b[l], dt_bias[l], a_log[l], d_skip[l], ssd_norm_g[l])
        y_mla = mla_mixer(c_q, c_kv, k_rope, positions, q_norm_g[l], w_uq[l], kv_norm_g[l], w_ukv[l])
        x = x + jnp.concatenate([y_ssd, y_mla], axis=-1) @ w_out[l]
        h = rms_norm(x, norm_ffn_g[l])
        x = x + moe_ffn(h, w_router[l], b_router[l], w_gate_up[l], b_gate_up[l], w_down[l], b_down[l])
    return rms_norm(x, norm_final_g)
```

```python
import functools

import jax
import jax.numpy as jnp
import numpy as np
from jax import lax
from jax.experimental import pallas as pl
from jax.experimental.pallas import tpu as pltpu

F32 = jnp.float32
BF16 = jnp.bfloat16
I32 = jnp.int32
HIGHEST = lax.Precision.HIGHEST

D_MODEL = 1024
EPS = 1e-6
LANES = 128

SSD_HEADS = 8
SSD_HEAD_DIM = 64
SSD_WIDTH = 512
SSD_STATE = 128
SSD_CONV = 4
SSD_CHUNK = 128
SSD_CONV_DIM = 1024
CONV_PAD = 8

MLA_HEADS = 8
MLA_Q_RANK = 256
MLA_KV_RANK = 128
MLA_NOPE = 64
MLA_ROPE = 32
MLA_V = 64
MLA_QK = MLA_NOPE + MLA_ROPE
MLA_WIDTH = 512
ROPE_THETA = 10000.0

N_EXPERTS = 32
TOP_K = 4
D_FF = 1024
SWIGLU_LIMIT = 7.0
SWIGLU_ALPHA = 1.702

IN_W = 512 + 1024 + 256 + 128 + 128 + 128

TM_PROJ = 512
TQ = 512
TK = 512
MOE_BM = 256
TM_COMB = 512

NT_DIMS = (((1,), (1,)), ((), ()))


def _rms(x):
    return x * lax.rsqrt(jnp.mean(x * x, axis=-1, keepdims=True) + EPS)


def _inproj_kernel(x_ref, pos_ref, g_ref, w1_ref, qg_ref, wq_ref, kvg_ref, wkv_ref, invf_ref, sgn_ref,
                   z_ref, xbc_ref, dtm_ref, q_ref, k_ref, v_ref):
    x = x_ref[...]
    h = (_rms(x) * g_ref[...]).astype(BF16)
    p = jnp.dot(h, w1_ref[...], preferred_element_type=F32)
    z_ref[...] = p[:, 0:512]
    xbc_ref[...] = p[:, 512:1536]
    cq = p[:, 1536:1792]
    ckv = p[:, 1792:1920]
    m1 = p[:, 1920:2048]
    m2 = p[:, 2048:2176]
    dtm_ref[...] = m1

    lane = lax.broadcasted_iota(I32, (1, LANES), 1)
    ang = pos_ref[...].astype(F32) * invf_ref[...]
    rope_lane = (lane >= MLA_NOPE) & (lane < MLA_QK)
    cos_t = jnp.where(rope_lane, jnp.cos(ang), 0.0)
    sin_t = jnp.sin(ang) * sgn_ref[...]
    cosq_t = jnp.where(lane < MLA_NOPE, 1.0, cos_t)
    scale = MLA_QK ** -0.5

    cqn = (_rms(cq) * qg_ref[...]).astype(BF16)
    qq = jnp.dot(cqn, wq_ref[...], preferred_element_type=F32)
    ckvn = (_rms(ckv) * kvg_ref[...]).astype(BF16)
    kv = jnp.dot(ckvn, wkv_ref[...], preferred_element_type=F32)
    krot = m1 * cos_t + m2 * sin_t
    for h_i in range(MLA_HEADS):
        lo = h_i * LANES
        qm = qq[:, lo:lo + LANES]
        qs = qq[:, 1024 + lo:1024 + lo + LANES]
        q_ref[0, h_i] = ((qm * cosq_t + qs * sin_t) * scale).astype(BF16)
        k_ref[0, h_i] = (kv[:, lo:lo + LANES] + krot).astype(BF16)
    for pr in range(MLA_HEADS // 2):
        vp = kv[:, 1024 + pr * LANES:1024 + (pr + 1) * LANES]
        v_ref[0, 2 * pr] = jnp.where(lane < MLA_V, vp, 1.0).astype(BF16)
        v_ref[0, 2 * pr + 1] = jnp.where(lane >= MLA_V, vp, 1.0).astype(BF16)


def _inproj(x2, pos2, g_mix, w1, qg, wq, kvg, wkv, invf, sgn, bsz, seqlen):
    t = x2.shape[0]
    tm = TM_PROJ
    per_b = seqlen // tm
    full = lambda shape: pl.BlockSpec(shape, lambda i: (0,) * len(shape))
    head_spec = pl.BlockSpec((1, MLA_HEADS, tm, LANES), lambda i: (i // per_b, 0, i % per_b, 0))
    head_shape = jax.ShapeDtypeStruct((bsz, MLA_HEADS, seqlen, LANES), BF16)
    return pl.pallas_call(
        _inproj_kernel,
        grid=(t // tm,),
        in_specs=[
            pl.BlockSpec((tm, D_MODEL), lambda i: (i, 0)),
            pl.BlockSpec((tm, 1), lambda i: (i, 0)),
            full((1, D_MODEL)), full((D_MODEL, IN_W)),
            full((1, MLA_Q_RANK)), full((MLA_Q_RANK, 2048)),
            full((1, MLA_KV_RANK)), full((MLA_KV_RANK, 1536)),
            full((1, LANES)), full((1, LANES)),
        ],
        out_specs=[
            pl.BlockSpec((tm, 512), lambda i: (i, 0)),
            pl.BlockSpec((tm, 1024), lambda i: (i, 0)),
            pl.BlockSpec((tm, LANES), lambda i: (i, 0)),
            head_spec, head_spec, head_spec,
        ],
        out_shape=[
            jax.ShapeDtypeStruct((t, 512), F32),
            jax.ShapeDtypeStruct((t, 1024), F32),
            jax.ShapeDtypeStruct((t, LANES), F32),
            head_shape, head_shape, head_shape,
        ],
        compiler_params=pltpu.CompilerParams(
            dimension_semantics=("arbitrary",), vmem_limit_bytes=56 * 1024 * 1024),
        name="inproj",
    )(x2, pos2, g_mix, w1, qg, wq, kvg, wkv, invf, sgn)


def _ssd_kernel(z_ref, xbc_ref, dtm_ref, cw_ref, cb_ref, dtb_ref, alog_ref, dsk_ref, ng_ref,
                y_ref, ext_ref, st_ref):
    q = SSD_CHUNK

    @pl.when(pl.program_id(1) == 0)
    def _():
        ext_ref[0:CONV_PAD, :] = jnp.zeros((CONV_PAD, SSD_CONV_DIM), F32)
        st_ref[...] = jnp.zeros_like(st_ref)

    ext_ref[CONV_PAD:CONV_PAD + q, :] = xbc_ref[...]
    conv = cb_ref[...]
    for kk in range(SSD_CONV):
        off = CONV_PAD - (SSD_CONV - 1) + kk
        conv = conv + cw_ref[kk:kk + 1, :] * ext_ref[off:off + q, :]
    ext_ref[0:CONV_PAD, :] = ext_ref[q:q + CONV_PAD, :]
    u = conv * jax.nn.sigmoid(conv)
    xs = u[:, 0:512]
    bm = u[:, 512:768]
    cm = u[:, 768:1024]

    lane = lax.broadcasted_iota(I32, (1, LANES), 1)
    xdt = dtm_ref[...] + dtb_ref[...]
    dt = jnp.maximum(xdt, 0.0) + jnp.log1p(jnp.exp(-jnp.abs(xdt)))
    a_neg = -jnp.exp(alog_ref[...])
    adt = jnp.where(lane < SSD_HEADS, dt * a_neg, 0.0)
    row = lax.broadcasted_iota(I32, (q, q), 0)
    col = lax.broadcasted_iota(I32, (q, q), 1)
    tril = row >= col
    cum_col = jnp.dot(tril.astype(F32), adt, precision=HIGHEST, preferred_element_type=F32)
    cum_row = cum_col.T

    hid = lax.broadcasted_iota(I32, (1, SSD_WIDTH), 1) // SSD_HEAD_DIM

    def expand(cols):
        out = jnp.zeros((q, SSD_WIDTH), F32)
        for h_i in range(SSD_HEADS):
            out = jnp.where(hid == h_i, cols[:, h_i:h_i + 1], out)
        return out

    dt_e = expand(dt)
    ac_e = expand(cum_col)
    last_e = ac_e[q - 1:q, :]
    xd = xs * dt_e
    w_end = xd * jnp.exp(last_e - ac_e)
    eac = jnp.exp(ac_e)
    cdec = jnp.exp(last_e)

    y_parts = []
    for g in range(2):
        gl = g * 256
        bg = bm[:, g * SSD_STATE:(g + 1) * SSD_STATE]
        cg = cm[:, g * SSD_STATE:(g + 1) * SSD_STATE].astype(BF16)
        scores = lax.dot_general(cg, bg.astype(BF16), NT_DIMS, preferred_element_type=F32)
        bgt = bg.T.astype(BF16)
        sprev = st_ref[g]
        yoff = jnp.dot(cg, sprev.astype(BF16), preferred_element_type=F32)
        st_ref[g] = sprev * cdec[:, gl:gl + 256] + jnp.dot(
            bgt, w_end[:, gl:gl + 256].astype(BF16), preferred_element_type=F32)
        for pr in range(2):
            pl_lo = gl + pr * LANES
            xdp = xd[:, pl_lo:pl_lo + LANES].astype(BF16)
            res = []
            for jj in range(2):
                h_i = g * 4 + pr * 2 + jj
                seg = cum_col[:, h_i:h_i + 1] - cum_row[h_i:h_i + 1, :]
                dec = jnp.exp(jnp.where(tril, seg, -jnp.inf))
                res.append(jnp.dot((scores * dec).astype(BF16), xdp, preferred_element_type=F32))
            ydiag = jnp.where(lane < SSD_HEAD_DIM, res[0], res[1])
            y_parts.append(ydiag + yoff[:, pr * LANES:(pr + 1) * LANES] * eac[:, pl_lo:pl_lo + LANES])
    y = jnp.concatenate(y_parts, axis=1) + dsk_ref[...] * xs
    zz = z_ref[...]
    y = y * (zz * jax.nn.sigmoid(zz))
    outs = []
    for g in range(2):
        yg = y[:, g * 256:(g + 1) * 256]
        outs.append(_rms(yg))
    y_ref[...] = (jnp.concatenate(outs, axis=1) * ng_ref[...]).astype(BF16)


def _ssd(z, xbc, dtm, cw, cb, dtb, alog, dsk, ng, bsz, seqlen):
    t = z.shape[0]
    q = SSD_CHUNK
    nc = seqlen // q
    full = lambda shape: pl.BlockSpec(shape, lambda b, c: (0,) * len(shape))
    row_spec = lambda width: pl.BlockSpec((q, width), lambda b, c: (b * nc + c, 0))
    return pl.pallas_call(
        _ssd_kernel,
        grid=(bsz, nc),
        in_specs=[row_spec(512), row_spec(1024), row_spec(LANES),
                  full((SSD_CONV, SSD_CONV_DIM)), full((1, SSD_CONV_DIM)),
                  full((1, LANES)), full((1, LANES)), full((1, SSD_WIDTH)), full((1, SSD_WIDTH))],
        out_specs=row_spec(512),
        out_shape=jax.ShapeDtypeStruct((t, SSD_WIDTH), BF16),
        scratch_shapes=[pltpu.VMEM((q + CONV_PAD, SSD_CONV_DIM), F32),
                        pltpu.VMEM((2, SSD_STATE, 256), F32)],
        compiler_params=pltpu.CompilerParams(dimension_semantics=("arbitrary", "arbitrary")),
        name="ssd",
    )(z, xbc, dtm, cw, cb, dtb, alog, dsk, ng)


def _attn_kernel(q_ref, k_ref, v_ref, o_ref, acc_ref):
    i = pl.program_id(2)
    lane = lax.broadcasted_iota(I32, (1, LANES), 1)
    acc_ref[...] = jnp.zeros_like(acc_ref)

    def step(j, m_pair, masked):
        start = pl.multiple_of(j * TK, TK)
        new_m = []
        for hh in range(2):
            kb = k_ref[0, hh, pl.ds(start, TK), :]
            vb = v_ref[0, hh, pl.ds(start, TK), :]
            s = lax.dot_general(q_ref[0, hh], kb, NT_DIMS, preferred_element_type=F32)
            if masked:
                r = lax.broadcasted_iota(I32, (TQ, TK), 0)
                c = lax.broadcasted_iota(I32, (TQ, TK), 1)
                s = jnp.where(c <= r, s, -jnp.inf)
            m_old = m_pair[hh]
            m_new = jnp.maximum(m_old, jnp.max(s, axis=-1, keepdims=True))
            alpha = jnp.exp(m_old - m_new)
            p = jnp.exp(s - m_new).astype(BF16)
            acc_ref[hh] = acc_ref[hh] * alpha + jnp.dot(p, vb, preferred_element_type=F32)
            new_m.append(m_new)
        return tuple(new_m)

    m0 = jnp.full((TQ, 1), -jnp.inf, F32)
    m_pair = lax.fori_loop(0, i, lambda j, m: step(j, m, False), (m0, m0))
    step(i, m_pair, True)

    a0 = acc_ref[0]
    a1 = acc_ref[1]
    o0 = a0 / a0[:, MLA_V:MLA_V + 1]
    o1 = a1 / a1[:, 0:1]
    o_ref[0] = jnp.where(lane < MLA_V, o0, o1).astype(BF16)


def _attention(q, k, v, bsz, seqlen):
    nq = seqlen // TQ
    kv_spec = pl.BlockSpec((1, 2, seqlen, LANES), lambda b, p, i: (b, p, 0, 0))
    return pl.pallas_call(
        _attn_kernel,
        grid=(bsz, MLA_HEADS // 2, nq),
        in_specs=[pl.BlockSpec((1, 2, TQ, LANES), lambda b, p, i: (b, p, i, 0)), kv_spec, kv_spec],
        out_specs=pl.BlockSpec((1, TQ, LANES), lambda b, p, i: (b, i, p)),
        out_shape=jax.ShapeDtypeStruct((bsz, seqlen, MLA_WIDTH), BF16),
        scratch_shapes=[pltpu.VMEM((2, TQ, LANES), F32)],
        compiler_params=pltpu.CompilerParams(
            dimension_semantics=("arbitrary", "arbitrary", "arbitrary"),
            vmem_limit_bytes=48 * 1024 * 1024),
        name="attention",
    )(q, k, v)


def _outproj_kernel(x_ref, ys_ref, ym_ref, wo_ref, g_ref, wr_ref, br_ref,
                    x1_ref, h2_ref, route_ref, cnt_ref):
    tm = x_ref.shape[0]

    @pl.when(pl.program_id(0) == 0)
    def _():
        cnt_ref[...] = jnp.zeros_like(cnt_ref)

    mix = (jnp.dot(ys_ref[...], wo_ref[0:512, :], preferred_element_type=F32)
           + jnp.dot(ym_ref[...], wo_ref[512:1024, :], preferred_element_type=F32))
    x1 = x_ref[...] + mix
    x1_ref[...] = x1
    h2 = _rms(x1) * g_ref[...]
    h2_ref[...] = h2.astype(BF16)

    lane = lax.broadcasted_iota(I32, (1, LANES), 1)
    logits = jnp.dot(h2, wr_ref[...], precision=HIGHEST, preferred_element_type=F32) + br_ref[...]
    logits = jnp.where(lane < N_EXPERTS, logits, -jnp.inf)

    vals, idxs, hots = [], [], []
    for _ in range(TOP_K):
        mx = jnp.max(logits, axis=-1, keepdims=True)
        idx = jnp.min(jnp.where(logits == mx, lane, LANES), axis=-1, keepdims=True)
        hot = lane == idx
        logits = jnp.where(hot, -jnp.inf, logits)
        vals.append(mx)
        idxs.append(idx)
        hots.append(hot)
    exps = [jnp.exp(v - vals[0]) for v in vals]
    denom = exps[0] + exps[1] + exps[2] + exps[3]

    multi = (hots[0] | hots[1] | hots[2] | hots[3])
    multi_f = jnp.where(multi, 1.0, 0.0)
    r = lax.broadcasted_iota(I32, (tm, tm), 0)
    c = lax.broadcasted_iota(I32, (tm, tm), 1)
    strict = jnp.where(r > c, 1.0, 0.0).astype(BF16)
    before = jnp.dot(strict, multi_f.astype(BF16), preferred_element_type=F32) + cnt_ref[0:1, :]
    cnt_ref[...] = cnt_ref[...] + jnp.sum(multi_f, axis=0, keepdims=True)

    out = jnp.zeros((tm, LANES), F32)
    for kk in range(TOP_K):
        rank = jnp.sum(jnp.where(hots[kk], before, 0.0), axis=-1, keepdims=True)
        out = jnp.where(lane == kk, idxs[kk].astype(F32), out)
        out = jnp.where(lane == TOP_K + kk, exps[kk] / denom, out)
        out = jnp.where(lane == 2 * TOP_K + kk, rank, out)
    route_ref[...] = out


def _outproj(x2, y_ssd, y_mla, wo, g_ffn, wr, br):
    t = x2.shape[0]
    tm = TM_PROJ
    full = lambda shape: pl.BlockSpec(shape, lambda i: (0,) * len(shape))
    rows = lambda width: pl.BlockSpec((tm, width), lambda i: (i, 0))
    return pl.pallas_call(
        _outproj_kernel,
        grid=(t // tm,),
        in_specs=[rows(D_MODEL), rows(512), rows(512), full((1024, D_MODEL)), full((1, D_MODEL)),
                  full((D_MODEL, LANES)), full((1, LANES))],
        out_specs=[rows(D_MODEL), rows(D_MODEL), rows(LANES), full((8, LANES))],
        out_shape=[jax.ShapeDtypeStruct((t, D_MODEL), F32),
                   jax.ShapeDtypeStruct((t, D_MODEL), BF16),
                   jax.ShapeDtypeStruct((t, LANES), F32),
                   jax.ShapeDtypeStruct((8, LANES), F32)],
        compiler_params=pltpu.CompilerParams(
            dimension_semantics=("arbitrary",), vmem_limit_bytes=40 * 1024 * 1024),
        name="outproj_router",
    )(x2, y_ssd, y_mla, wo, g_ffn, wr, br)


def _ffn_kernel(be_ref, nu_ref, xs_ref, wgu_ref, bgu_ref, wd_ref, bd_ref, ys_ref, wgu_bf, wd_bf):
    i = pl.program_id(0)
    prev = be_ref[jnp.maximum(i - 1, 0)]
    changed = (i == 0) | (be_ref[i] != prev)

    @pl.when(changed)
    def _():
        wgu_bf[...] = wgu_ref[0].astype(BF16)
        wd_bf[...] = wd_ref[0].astype(BF16)

    @pl.when(i < nu_ref[0])
    def _():
        gu = jnp.dot(xs_ref[...], wgu_bf[...], preferred_element_type=F32) + bgu_ref[0]
        gate = jnp.minimum(gu[:, :D_FF], SWIGLU_LIMIT)
        up = jnp.clip(gu[:, D_FF:], -SWIGLU_LIMIT, SWIGLU_LIMIT)
        glu = gate * jax.nn.sigmoid(SWIGLU_ALPHA * gate)
        mid = ((up + 1.0) * glu).astype(BF16)
        ys_ref[...] = (jnp.dot(mid, wd_bf[...], preferred_element_type=F32) + bd_ref[0]).astype(BF16)

    @pl.when(i >= nu_ref[0])
    def _():
        ys_ref[...] = jnp.zeros_like(ys_ref)


def _expert_ffn(block_e, n_used, xs, wgu, bgu, wd, bd):
    n_slots = xs.shape[0]
    bm = MOE_BM
    grid_spec = pltpu.PrefetchScalarGridSpec(
        num_scalar_prefetch=2,
        grid=(n_slots // bm,),
        in_specs=[
            pl.BlockSpec((bm, D_MODEL), lambda i, be, nu: (i, 0)),
            pl.BlockSpec((1, D_MODEL, 2 * D_FF), lambda i, be, nu: (be[i], 0, 0)),
            pl.BlockSpec((1, 1, 2 * D_FF), lambda i, be, nu: (be[i], 0, 0)),
            pl.BlockSpec((1, D_FF, D_MODEL), lambda i, be, nu: (be[i], 0, 0)),
            pl.BlockSpec((1, 1, D_MODEL), lambda i, be, nu: (be[i], 0, 0)),
        ],
        out_specs=pl.BlockSpec((bm, D_MODEL), lambda i, be, nu: (i, 0)),
        scratch_shapes=[pltpu.VMEM((D_MODEL, 2 * D_FF), BF16), pltpu.VMEM((D_FF, D_MODEL), BF16)],
    )
    return pl.pallas_call(
        _ffn_kernel,
        grid_spec=grid_spec,
        out_shape=jax.ShapeDtypeStruct((n_slots, D_MODEL), BF16),
        compiler_params=pltpu.CompilerParams(
            dimension_semantics=("arbitrary",), vmem_limit_bytes=56 * 1024 * 1024),
        name="expert_ffn",
    )(block_e, n_used, xs, wgu, bgu, wd, bd)


def _combine_kernel(x1_ref, yg_ref, route_ref, g_ref, o_ref, *, final_norm):
    acc = x1_ref[...]
    for kk in range(TOP_K):
        gate = route_ref[:, TOP_K + kk:TOP_K + kk + 1]
        acc = acc + gate * yg_ref[kk].astype(F32)
    o_ref[...] = _rms(acc) * g_ref[...] if final_norm else acc


def _combine(x1, yg, route, g_final, final_norm):
    t = x1.shape[0]
    tm = TM_COMB
    return pl.pallas_call(
        functools.partial(_combine_kernel, final_norm=final_norm),
        grid=(t // tm,),
        in_specs=[pl.BlockSpec((tm, D_MODEL), lambda i: (i, 0)),
                  pl.BlockSpec((TOP_K, tm, D_MODEL), lambda i: (0, i, 0)),
                  pl.BlockSpec((tm, LANES), lambda i: (i, 0)),
                  pl.BlockSpec((1, D_MODEL), lambda i: (0, 0))],
        out_specs=pl.BlockSpec((tm, D_MODEL), lambda i: (i, 0)),
        out_shape=jax.ShapeDtypeStruct((t, D_MODEL), F32),
        compiler_params=pltpu.CompilerParams(dimension_semantics=("arbitrary",)),
        name="combine",
    )(x1, yg, route, g_final)


def _prep_in_weights(w_in, w_uq, w_ukv):
    w_z = w_in[:, 0:512]
    w_xbc = w_in[:, 512:1536]
    w_dt = w_in[:, 1536:1544]
    w_cq = w_in[:, 1544:1800]
    w_ckv = w_in[:, 1800:1928]
    w_kr = w_in[:, 1928:1960]
    half = MLA_ROPE // 2
    misc1 = jnp.zeros((D_MODEL, LANES), F32).at[:, 0:SSD_HEADS].set(w_dt).at[:, MLA_NOPE:MLA_QK].set(w_kr)
    misc2 = (jnp.zeros((D_MODEL, LANES), F32)
             .at[:, MLA_NOPE:MLA_NOPE + half].set(w_kr[:, half:])
             .at[:, MLA_NOPE + half:MLA_QK].set(w_kr[:, :half]))
    w1 = jnp.concatenate([w_z, w_xbc, w_cq, w_ckv, misc1, misc2], axis=1).astype(BF16)

    wq3 = w_uq.reshape(MLA_Q_RANK, MLA_HEADS, MLA_QK)
    main = jnp.zeros((MLA_Q_RANK, MLA_HEADS, LANES), F32).at[:, :, 0:MLA_QK].set(wq3)
    swap = (jnp.zeros((MLA_Q_RANK, MLA_HEADS, LANES), F32)
            .at[:, :, MLA_NOPE:MLA_NOPE + half].set(wq3[:, :, MLA_NOPE + half:])
            .at[:, :, MLA_NOPE + half:MLA_QK].set(wq3[:, :, MLA_NOPE:MLA_NOPE + half]))
    wq = jnp.concatenate([main.reshape(MLA_Q_RANK, -1), swap.reshape(MLA_Q_RANK, -1)], axis=1).astype(BF16)

    wkv3 = w_ukv.reshape(MLA_KV_RANK, MLA_HEADS, MLA_NOPE + MLA_V)
    kpart = jnp.zeros((MLA_KV_RANK, MLA_HEADS, LANES), F32).at[:, :, 0:MLA_NOPE].set(wkv3[:, :, :MLA_NOPE])
    vpart = wkv3[:, :, MLA_NOPE:]
    wkv = jnp.concatenate([kpart.reshape(MLA_KV_RANK, -1), vpart.reshape(MLA_KV_RANK, -1)], axis=1).astype(BF16)
    return w1, wq, wkv


def _rope_consts():
    half = MLA_ROPE // 2
    inv_freq = ROPE_THETA ** (-jnp.arange(0, MLA_ROPE, 2, dtype=F32) / MLA_ROPE)
    invf = (jnp.zeros((1, LANES), F32)
            .at[0, MLA_NOPE:MLA_NOPE + half].set(inv_freq)
            .at[0, MLA_NOPE + half:MLA_QK].set(inv_freq))
    sgn = (jnp.zeros((1, LANES), F32)
           .at[0, MLA_NOPE:MLA_NOPE + half].set(-1.0)
           .at[0, MLA_NOPE + half:MLA_QK].set(1.0))
    return invf, sgn


def _pad_lanes(v, fill=0.0):
    return jnp.full((1, LANES), fill, F32).at[0, :v.shape[0]].set(v)


def kernel(x, positions, norm_mix_g, w_in, conv_w, conv_b, dt_bias, a_log, d_skip, ssd_norm_g, q_norm_g, w_uq, kv_norm_g, w_ukv, w_out, norm_ffn_g, w_router, b_router, w_gate_up, b_gate_up, w_down, b_down, norm_final_g):
    bsz, seqlen, d = x.shape
    t = bsz * seqlen
    depth = w_in.shape[0]
    x2 = x.reshape(t, d)
    pos2 = positions.reshape(t, 1).astype(I32)
    invf, sgn = _rope_consts()

    for l in range(depth):
        w1, wq, wkv = _prep_in_weights(w_in[l], w_uq[l], w_ukv[l])
        z, xbc, dtm, q, k, v = _inproj(
            x2, pos2, norm_mix_g[l][None, :], w1, q_norm_g[l][None, :], wq, kv_norm_g[l][None, :], wkv,
            invf, sgn, bsz, seqlen)
        y_ssd = _ssd(z, xbc, dtm, conv_w[l], conv_b[l][None, :], _pad_lanes(dt_bias[l]), _pad_lanes(a_log[l]),
                     jnp.repeat(d_skip[l], SSD_HEAD_DIM)[None, :], ssd_norm_g[l][None, :], bsz, seqlen)
        y_mla = _attention(q, k, v, bsz, seqlen).reshape(t, MLA_WIDTH)

        wr = jnp.zeros((d, LANES), F32).at[:, :N_EXPERTS].set(w_router[l])
        x1, h2, route, cnt = _outproj(x2, y_ssd, y_mla, w_out[l].astype(BF16), norm_ffn_g[l][None, :],
                                      wr, _pad_lanes(b_router[l]))

        counts = cnt[0, :N_EXPERTS].astype(I32)
        padded = ((counts + MOE_BM - 1) // MOE_BM) * MOE_BM
        end_pad = jnp.cumsum(padded)
        start_pad = end_pad - padded
        n_slots = t * TOP_K + N_EXPERTS * MOE_BM
        n_blocks = n_slots // MOE_BM
        block_start = jnp.arange(n_blocks, dtype=I32) * MOE_BM
        block_e = jnp.minimum(jnp.sum(block_start[:, None] >= end_pad[None, :], axis=1), N_EXPERTS - 1).astype(I32)
        n_used = (end_pad[-1:] // MOE_BM).astype(I32)
        top_idx = route[:, 0:TOP_K].astype(I32)
        rank = route[:, 2 * TOP_K:3 * TOP_K].astype(I32)
        dest = start_pad[top_idx] + rank

        slot_tok = jnp.full((n_slots,), t, I32).at[dest.reshape(-1)].set(
            jnp.repeat(jnp.arange(t, dtype=I32), TOP_K))
        h2_pad = jnp.concatenate([h2, jnp.zeros((1, d), BF16)], axis=0)
        xs = h2_pad[slot_tok]
        ys = _expert_ffn(block_e, n_used, xs, w_gate_up[l], b_gate_up[l][:, None, :], w_down[l],
                         b_down[l][:, None, :])
        yg = jnp.transpose(ys[dest], (1, 0, 2))
        x2 = _combine(x1, yg, route, norm_final_g[None, :], l == depth - 1)
    return x2.reshape(bsz, seqlen, d)
```

```python
import functools

import jax
import jax.numpy as jnp
import numpy as np
from jax import lax
from jax.experimental import pallas as pl
from jax.experimental.pallas import tpu as pltpu
from jax.experimental.pallas import tpu_sc as plsc

F32 = jnp.float32
BF16 = jnp.bfloat16
I32 = jnp.int32
HIGHEST = lax.Precision.HIGHEST

D_MODEL = 1024
EPS = 1e-6
LANES = 128

SSD_HEADS = 8
SSD_HEAD_DIM = 64
SSD_WIDTH = 512
SSD_STATE = 128
SSD_CONV = 4
SSD_CHUNK = 128
SSD_CONV_DIM = 1024
CONV_PAD = 8

MLA_HEADS = 8
MLA_Q_RANK = 256
MLA_KV_RANK = 128
MLA_NOPE = 64
MLA_ROPE = 32
MLA_V = 64
MLA_QK = MLA_NOPE + MLA_ROPE
MLA_WIDTH = 512
ROPE_THETA = 10000.0

N_EXPERTS = 32
TOP_K = 4
D_FF = 1024
SWIGLU_LIMIT = 7.0
SWIGLU_ALPHA = 1.702

IN_W = 512 + 1024 + 256 + 128 + 128 + 128

TM_PROJ = 512
TQ = 512
TK = 512
MOE_BM = 256
TM_COMB = 512

NT_DIMS = (((1,), (1,)), ((), ()))


def _rms(x):
    return x * lax.rsqrt(jnp.mean(x * x, axis=-1, keepdims=True) + EPS)


HALF = D_MODEL // 2
HI_MASK = np.int32(-65536)


def _pack_rows(a):
    lo = lax.bitcast_convert_type(a[:, :HALF].astype(BF16).astype(F32), I32)
    hi = lax.bitcast_convert_type(a[:, HALF:].astype(BF16).astype(F32), I32)
    return (hi & HI_MASK) | lax.shift_right_logical(lo, 16)


def _unpack_rows(p):
    lo = lax.bitcast_convert_type(lax.shift_left(p, 16), F32)
    hi = lax.bitcast_convert_type(p & HI_MASK, F32)
    return lo, hi


def _inproj_kernel(x_ref, pos_ref, g_ref, w1_ref, qg_ref, wq_ref, kvg_ref, wkv_ref, invf_ref, sgn_ref,
                   z_ref, xbc_ref, dtm_ref, q_ref, k_ref, v_ref):
    x = x_ref[...]
    h = (_rms(x) * g_ref[...]).astype(BF16)
    p = jnp.dot(h, w1_ref[...], preferred_element_type=F32)
    z_ref[...] = p[:, 0:512]
    xbc_ref[...] = p[:, 512:1536]
    cq = p[:, 1536:1792]
    ckv = p[:, 1792:1920]
    m1 = p[:, 1920:2048]
    m2 = p[:, 2048:2176]
    dtm_ref[...] = m1

    lane = lax.broadcasted_iota(I32, (1, LANES), 1)
    ang = pos_ref[...].astype(F32) * invf_ref[...]
    rope_lane = (lane >= MLA_NOPE) & (lane < MLA_QK)
    cos_t = jnp.where(rope_lane, jnp.cos(ang), 0.0)
    sin_t = jnp.sin(ang) * sgn_ref[...]
    cosq_t = jnp.where(lane < MLA_NOPE, 1.0, cos_t)
    scale = MLA_QK ** -0.5

    cqn = (_rms(cq) * qg_ref[...]).astype(BF16)
    qq = jnp.dot(cqn, wq_ref[...], preferred_element_type=F32)
    ckvn = (_rms(ckv) * kvg_ref[...]).astype(BF16)
    kv = jnp.dot(ckvn, wkv_ref[...], preferred_element_type=F32)
    krot = m1 * cos_t + m2 * sin_t
    for h_i in range(MLA_HEADS):
        lo = h_i * LANES
        qm = qq[:, lo:lo + LANES]
        qs = qq[:, 1024 + lo:1024 + lo + LANES]
        q_ref[0, h_i] = ((qm * cosq_t + qs * sin_t) * scale).astype(BF16)
        k_ref[0, h_i] = (kv[:, lo:lo + LANES] + krot).astype(BF16)
    for pr in range(MLA_HEADS // 2):
        vp = kv[:, 1024 + pr * LANES:1024 + (pr + 1) * LANES]
        v_ref[0, 2 * pr] = jnp.where(lane < MLA_V, vp, 1.0).astype(BF16)
        v_ref[0, 2 * pr + 1] = jnp.where(lane >= MLA_V, vp, 1.0).astype(BF16)


def _inproj(x2, pos2, g_mix, w1, qg, wq, kvg, wkv, invf, sgn, bsz, seqlen):
    t = x2.shape[0]
    tm = TM_PROJ
    per_b = seqlen // tm
    full = lambda shape: pl.BlockSpec(shape, lambda i: (0,) * len(shape))
    head_spec = pl.BlockSpec((1, MLA_HEADS, tm, LANES), lambda i: (i // per_b, 0, i % per_b, 0))
    head_shape = jax.ShapeDtypeStruct((bsz, MLA_HEADS, seqlen, LANES), BF16)
    return pl.pallas_call(
        _inproj_kernel,
        grid=(t // tm,),
        in_specs=[
            pl.BlockSpec((tm, D_MODEL), lambda i: (i, 0)),
            pl.BlockSpec((tm, 1), lambda i: (i, 0)),
            full((1, D_MODEL)), full((D_MODEL, IN_W)),
            full((1, MLA_Q_RANK)), full((MLA_Q_RANK, 2048)),
            full((1, MLA_KV_RANK)), full((MLA_KV_RANK, 1536)),
            full((1, LANES)), full((1, LANES)),
        ],
        out_specs=[
            pl.BlockSpec((tm, 512), lambda i: (i, 0)),
            pl.BlockSpec((tm, 1024), lambda i: (i, 0)),
            pl.BlockSpec((tm, LANES), lambda i: (i, 0)),
            head_spec, head_spec, head_spec,
        ],
        out_shape=[
            jax.ShapeDtypeStruct((t, 512), F32),
            jax.ShapeDtypeStruct((t, 1024), F32),
            jax.ShapeDtypeStruct((t, LANES), F32),
            head_shape, head_shape, head_shape,
        ],
        compiler_params=pltpu.CompilerParams(
            dimension_semantics=("arbitrary",), vmem_limit_bytes=56 * 1024 * 1024),
        name="inproj",
    )(x2, pos2, g_mix, w1, qg, wq, kvg, wkv, invf, sgn)


def _ssd_kernel(z_ref, xbc_ref, dtm_ref, cw_ref, cb_ref, dtb_ref, alog_ref, dsk_ref, ng_ref,
                y_ref, ext_ref, st_ref):
    q = SSD_CHUNK

    @pl.when(pl.program_id(1) == 0)
    def _():
        ext_ref[0:CONV_PAD, :] = jnp.zeros((CONV_PAD, SSD_CONV_DIM), F32)
        st_ref[...] = jnp.zeros_like(st_ref)

    ext_ref[CONV_PAD:CONV_PAD + q, :] = xbc_ref[...]
    conv = cb_ref[...]
    for kk in range(SSD_CONV):
        off = CONV_PAD - (SSD_CONV - 1) + kk
        conv = conv + cw_ref[kk:kk + 1, :] * ext_ref[off:off + q, :]
    ext_ref[0:CONV_PAD, :] = ext_ref[q:q + CONV_PAD, :]
    u = conv * jax.nn.sigmoid(conv)
    xs = u[:, 0:512]
    bm = u[:, 512:768]
    cm = u[:, 768:1024]

    lane = lax.broadcasted_iota(I32, (1, LANES), 1)
    xdt = dtm_ref[...] + dtb_ref[...]
    dt = jnp.maximum(xdt, 0.0) + jnp.log1p(jnp.exp(-jnp.abs(xdt)))
    a_neg = -jnp.exp(alog_ref[...])
    adt = jnp.where(lane < SSD_HEADS, dt * a_neg, 0.0)
    row = lax.broadcasted_iota(I32, (q, q), 0)
    col = lax.broadcasted_iota(I32, (q, q), 1)
    tril = row >= col
    cum_col = jnp.dot(tril.astype(F32), adt, precision=HIGHEST, preferred_element_type=F32)
    cum_row = cum_col.T

    hid = lax.broadcasted_iota(I32, (1, SSD_WIDTH), 1) // SSD_HEAD_DIM

    def expand(cols):
        out = jnp.zeros((q, SSD_WIDTH), F32)
        for h_i in range(SSD_HEADS):
            out = jnp.where(hid == h_i, cols[:, h_i:h_i + 1], out)
        return out

    dt_e = expand(dt)
    ac_e = expand(cum_col)
    last_e = ac_e[q - 1:q, :]
    xd = xs * dt_e
    w_end = xd * jnp.exp(last_e - ac_e)
    eac = jnp.exp(ac_e)
    cdec = jnp.exp(last_e)

    y_parts = []
    for g in range(2):
        gl = g * 256
        bg = bm[:, g * SSD_STATE:(g + 1) * SSD_STATE]
        cg = cm[:, g * SSD_STATE:(g + 1) * SSD_STATE].astype(BF16)
        scores = lax.dot_general(cg, bg.astype(BF16), NT_DIMS, preferred_element_type=F32)
        bgt = bg.T.astype(BF16)
        sprev = st_ref[g]
        yoff = jnp.dot(cg, sprev.astype(BF16), preferred_element_type=F32)
        st_ref[g] = sprev * cdec[:, gl:gl + 256] + jnp.dot(
            bgt, w_end[:, gl:gl + 256].astype(BF16), preferred_element_type=F32)
        for pr in range(2):
            pl_lo = gl + pr * LANES
            xdp = xd[:, pl_lo:pl_lo + LANES].astype(BF16)
            res = []
            for jj in range(2):
                h_i = g * 4 + pr * 2 + jj
                seg = cum_col[:, h_i:h_i + 1] - cum_row[h_i:h_i + 1, :]
                dec = jnp.exp(jnp.where(tril, seg, -jnp.inf))
                res.append(jnp.dot((scores * dec).astype(BF16), xdp, preferred_element_type=F32))
            ydiag = jnp.where(lane < SSD_HEAD_DIM, res[0], res[1])
            y_parts.append(ydiag + yoff[:, pr * LANES:(pr + 1) * LANES] * eac[:, pl_lo:pl_lo + LANES])
    y = jnp.concatenate(y_parts, axis=1) + dsk_ref[...] * xs
    zz = z_ref[...]
    y = y * (zz * jax.nn.sigmoid(zz))
    outs = []
    for g in range(2):
        yg = y[:, g * 256:(g + 1) * 256]
        outs.append(_rms(yg))
    y_ref[...] = (jnp.concatenate(outs, axis=1) * ng_ref[...]).astype(BF16)


def _ssd(z, xbc, dtm, cw, cb, dtb, alog, dsk, ng, bsz, seqlen):
    t = z.shape[0]
    q = SSD_CHUNK
    nc = seqlen // q
    full = lambda shape: pl.BlockSpec(shape, lambda b, c: (0,) * len(shape))
    row_spec = lambda width: pl.BlockSpec((q, width), lambda b, c: (b * nc + c, 0))
    return pl.pallas_call(
        _ssd_kernel,
        grid=(bsz, nc),
        in_specs=[row_spec(512), row_spec(1024), row_spec(LANES),
                  full((SSD_CONV, SSD_CONV_DIM)), full((1, SSD_CONV_DIM)),
                  full((1, LANES)), full((1, LANES)), full((1, SSD_WIDTH)), full((1, SSD_WIDTH))],
        out_specs=row_spec(512),
        out_shape=jax.ShapeDtypeStruct((t, SSD_WIDTH), BF16),
        scratch_shapes=[pltpu.VMEM((q + CONV_PAD, SSD_CONV_DIM), F32),
                        pltpu.VMEM((2, SSD_STATE, 256), F32)],
        compiler_params=pltpu.CompilerParams(dimension_semantics=("arbitrary", "arbitrary")),
        name="ssd",
    )(z, xbc, dtm, cw, cb, dtb, alog, dsk, ng)


def _attn_kernel(q_ref, k_ref, v_ref, o_ref, acc_ref):
    i = pl.program_id(2)
    lane = lax.broadcasted_iota(I32, (1, LANES), 1)
    acc_ref[...] = jnp.zeros_like(acc_ref)

    def step(j, m_pair, masked):
        start = pl.multiple_of(j * TK, TK)
        new_m = []
        for hh in range(2):
            kb = k_ref[0, hh, pl.ds(start, TK), :]
            vb = v_ref[0, hh, pl.ds(start, TK), :]
            s = lax.dot_general(q_ref[0, hh], kb, NT_DIMS, preferred_element_type=F32)
            if masked:
                r = lax.broadcasted_iota(I32, (TQ, TK), 0)
                c = lax.broadcasted_iota(I32, (TQ, TK), 1)
                s = jnp.where(c <= r, s, -jnp.inf)
            m_old = m_pair[hh]
            m_new = jnp.maximum(m_old, jnp.max(s, axis=-1, keepdims=True))
            alpha = jnp.exp(m_old - m_new)
            p = jnp.exp(s - m_new).astype(BF16)
            acc_ref[hh] = acc_ref[hh] * alpha + jnp.dot(p, vb, preferred_element_type=F32)
            new_m.append(m_new)
        return tuple(new_m)

    m0 = jnp.full((TQ, 1), -jnp.inf, F32)
    m_pair = lax.fori_loop(0, i, lambda j, m: step(j, m, False), (m0, m0))
    step(i, m_pair, True)

    a0 = acc_ref[0]
    a1 = acc_ref[1]
    o0 = a0 / a0[:, MLA_V:MLA_V + 1]
    o1 = a1 / a1[:, 0:1]
    o_ref[0] = jnp.where(lane < MLA_V, o0, o1).astype(BF16)


def _attention(q, k, v, bsz, seqlen):
    nq = seqlen // TQ
    kv_spec = pl.BlockSpec((1, 2, seqlen, LANES), lambda b, p, i: (b, p, 0, 0))
    return pl.pallas_call(
        _attn_kernel,
        grid=(bsz, MLA_HEADS // 2, nq),
        in_specs=[pl.BlockSpec((1, 2, TQ, LANES), lambda b, p, i: (b, p, i, 0)), kv_spec, kv_spec],
        out_specs=pl.BlockSpec((1, TQ, LANES), lambda b, p, i: (b, i, p)),
        out_shape=jax.ShapeDtypeStruct((bsz, seqlen, MLA_WIDTH), BF16),
        scratch_shapes=[pltpu.VMEM((2, TQ, LANES), F32)],
        compiler_params=pltpu.CompilerParams(
            dimension_semantics=("arbitrary", "arbitrary", "arbitrary"),
            vmem_limit_bytes=48 * 1024 * 1024),
        name="attention",
    )(q, k, v)


def _outproj_kernel(x_ref, ys_ref, ym_ref, wo_ref, g_ref, wr_ref, br_ref,
                    x1_ref, h2_ref, route_ref, cnt_ref):
    tm = x_ref.shape[0]

    @pl.when(pl.program_id(0) == 0)
    def _():
        cnt_ref[...] = jnp.zeros_like(cnt_ref)

    mix = (jnp.dot(ys_ref[...], wo_ref[0:512, :], preferred_element_type=F32)
           + jnp.dot(ym_ref[...], wo_ref[512:1024, :], preferred_element_type=F32))
    x1 = x_ref[...] + mix
    x1_ref[...] = x1
    h2 = _rms(x1) * g_ref[...]
    h2_ref[...] = _pack_rows(h2)

    lane = lax.broadcasted_iota(I32, (1, LANES), 1)
    h_hi = h2.astype(BF16)
    h_lo = (h2 - h_hi.astype(F32)).astype(BF16)
    hh = jnp.dot(h_hi, wr_ref[...], preferred_element_type=F32)
    lh = jnp.dot(h_lo, wr_ref[:, 0:LANES], preferred_element_type=F32)
    logits = hh[:, 0:LANES] + (hh[:, LANES:2 * LANES] + lh) + br_ref[...]
    logits = jnp.where(lane < N_EXPERTS, logits, -jnp.inf)

    vals, idxs, hots = [], [], []
    for _ in range(TOP_K):
        mx = jnp.max(logits, axis=-1, keepdims=True)
        idx = jnp.min(jnp.where(logits == mx, lane, LANES), axis=-1, keepdims=True)
        hot = lane == idx
        logits = jnp.where(hot, -jnp.inf, logits)
        vals.append(mx)
        idxs.append(idx)
        hots.append(hot)
    exps = [jnp.exp(v - vals[0]) for v in vals]
    denom = exps[0] + exps[1] + exps[2] + exps[3]

    multi = (hots[0] | hots[1] | hots[2] | hots[3])
    multi_f = jnp.where(multi, 1.0, 0.0)
    r = lax.broadcasted_iota(I32, (tm, tm), 0)
    c = lax.broadcasted_iota(I32, (tm, tm), 1)
    strict = jnp.where(r > c, 1.0, 0.0).astype(BF16)
    before = jnp.dot(strict, multi_f.astype(BF16), preferred_element_type=F32) + cnt_ref[0:1, :]
    cnt_ref[...] = cnt_ref[...] + jnp.sum(multi_f, axis=0, keepdims=True)

    out = jnp.zeros((tm, LANES), F32)
    for kk in range(TOP_K):
        rank = jnp.sum(jnp.where(hots[kk], before, 0.0), axis=-1, keepdims=True)
        out = jnp.where(lane == kk, idxs[kk].astype(F32), out)
        out = jnp.where(lane == TOP_K + kk, exps[kk] / denom, out)
        out = jnp.where(lane == 2 * TOP_K + kk, rank, out)
    route_ref[...] = out


def _outproj(x2, y_ssd, y_mla, wo, g_ffn, wr, br):
    t = x2.shape[0]
    tm = TM_PROJ
    full = lambda shape: pl.BlockSpec(shape, lambda i: (0,) * len(shape))
    rows = lambda width: pl.BlockSpec((tm, width), lambda i: (i, 0))
    return pl.pallas_call(
        _outproj_kernel,
        grid=(t // tm,),
        in_specs=[rows(D_MODEL), rows(512), rows(512), full((1024, D_MODEL)), full((1, D_MODEL)),
                  full((D_MODEL, 2 * LANES)), full((1, LANES))],
        out_specs=[rows(D_MODEL), rows(HALF), rows(LANES), full((8, LANES))],
        out_shape=[jax.ShapeDtypeStruct((t, D_MODEL), F32),
                   jax.ShapeDtypeStruct((t, HALF), I32),
                   jax.ShapeDtypeStruct((t, LANES), F32),
                   jax.ShapeDtypeStruct((8, LANES), F32)],
        compiler_params=pltpu.CompilerParams(
            dimension_semantics=("arbitrary",), vmem_limit_bytes=40 * 1024 * 1024),
        name="outproj_router",
    )(x2, y_ssd, y_mla, wo, g_ffn, wr, br)


def _ffn_kernel(be_ref, bv_ref, xs_ref, wgu_ref, bgu_ref, wd_ref, bd_ref, ys_ref, wgu_bf, wd_bf):
    i = pl.program_id(0)
    prev = be_ref[jnp.maximum(i - 1, 0)]
    changed = (i == 0) | (be_ref[i] != prev)
    valid = bv_ref[i]

    @pl.when(changed & (valid > 0))
    def _():
        wgu_bf[...] = wgu_ref[0].astype(BF16)
        wd_bf[...] = wd_ref[0].astype(BF16)

    @pl.when(valid > 0)
    def _():
        row = lax.broadcasted_iota(I32, (MOE_BM, 1), 0)
        x_lo, x_hi = _unpack_rows(jnp.where(row < valid, xs_ref[...], 0))
        gu = (jnp.dot(x_lo.astype(BF16), wgu_bf[0:HALF, :], preferred_element_type=F32)
              + jnp.dot(x_hi.astype(BF16), wgu_bf[HALF:D_MODEL, :], preferred_element_type=F32)
              + bgu_ref[0])
        gate = jnp.minimum(gu[:, :D_FF], SWIGLU_LIMIT)
        up = jnp.clip(gu[:, D_FF:], -SWIGLU_LIMIT, SWIGLU_LIMIT)
        glu = gate * jax.nn.sigmoid(SWIGLU_ALPHA * gate)
        mid = ((up + 1.0) * glu).astype(BF16)
        ys_ref[...] = _pack_rows(jnp.dot(mid, wd_bf[...], preferred_element_type=F32) + bd_ref[0])

    @pl.when(valid == 0)
    def _():
        ys_ref[...] = jnp.zeros_like(ys_ref)


def _expert_ffn(block_e, block_valid, xs, wgu, bgu, wd, bd):
    n_slots = xs.shape[0]
    bm = MOE_BM
    grid_spec = pltpu.PrefetchScalarGridSpec(
        num_scalar_prefetch=2,
        grid=(n_slots // bm,),
        in_specs=[
            pl.BlockSpec((bm, HALF), lambda i, be, bv: (i, 0)),
            pl.BlockSpec((1, D_MODEL, 2 * D_FF), lambda i, be, bv: (be[i], 0, 0)),
            pl.BlockSpec((1, 1, 2 * D_FF), lambda i, be, bv: (be[i], 0, 0)),
            pl.BlockSpec((1, D_FF, D_MODEL), lambda i, be, bv: (be[i], 0, 0)),
            pl.BlockSpec((1, 1, D_MODEL), lambda i, be, bv: (be[i], 0, 0)),
        ],
        out_specs=pl.BlockSpec((bm, HALF), lambda i, be, bv: (i, 0)),
        scratch_shapes=[pltpu.VMEM((D_MODEL, 2 * D_FF), BF16), pltpu.VMEM((D_FF, D_MODEL), BF16)],
    )
    return pl.pallas_call(
        _ffn_kernel,
        grid_spec=grid_spec,
        out_shape=jax.ShapeDtypeStruct((n_slots, HALF), I32),
        compiler_params=pltpu.CompilerParams(
            dimension_semantics=("arbitrary",), vmem_limit_bytes=56 * 1024 * 1024),
        name="expert_ffn",
    )(block_e, block_valid, xs, wgu, bgu, wd, bd)


SC_CHUNK = 32


def _sc_workers():
    info = plsc.get_sparse_core_info()
    return info.num_cores, info.num_cores * info.num_subcores


def _sc_scatter_rows(rows, dest_km, n_out):
    t, w = rows.shape
    n_cores, n_workers = _sc_workers()
    per_w = t // n_workers
    mesh = plsc.VectorSubcoreMesh(core_axis_name="c", subcore_axis_name="s")

    @functools.partial(
        pl.kernel, mesh=mesh, out_type=jax.ShapeDtypeStruct((n_out, w), rows.dtype),
        scratch_types=[pltpu.VMEM((TOP_K, SC_CHUNK), I32), pltpu.VMEM((SC_CHUNK, w), rows.dtype),
                       pltpu.SemaphoreType.DMA],
        name="sc_dispatch_scatter")
    def scatter_kernel(rows_hbm, dest_hbm, out_hbm, idx_v, rows_v, sem):
        wid = lax.axis_index("s") * n_cores + lax.axis_index("c")

        @pl.loop(0, per_w // SC_CHUNK)
        def _(c):
            base = wid * per_w + c * SC_CHUNK
            pltpu.sync_copy(rows_hbm.at[pl.ds(base, SC_CHUNK)], rows_v)
            for kk in range(TOP_K):
                pltpu.sync_copy(dest_hbm.at[kk, pl.ds(base, SC_CHUNK)], idx_v.at[kk])
            copies = [pltpu.async_copy(rows_v, out_hbm.at[idx_v.at[kk]], sem) for kk in range(TOP_K)]
            for cp in copies:
                cp.wait()

    return scatter_kernel(rows, dest_km)


def _sc_gather_rows(table, dest_km):
    _, w = table.shape
    t = dest_km.shape[1]
    n_cores, n_workers = _sc_workers()
    per_w = t // n_workers
    mesh = plsc.VectorSubcoreMesh(core_axis_name="c", subcore_axis_name="s")

    @functools.partial(
        pl.kernel, mesh=mesh, out_type=jax.ShapeDtypeStruct((TOP_K, t, w), table.dtype),
        scratch_types=[pltpu.VMEM((TOP_K, SC_CHUNK), I32), pltpu.VMEM((TOP_K, SC_CHUNK, w), table.dtype),
                       pltpu.SemaphoreType.DMA],
        name="sc_combine_gather")
    def gather_kernel(table_hbm, dest_hbm, out_hbm, idx_v, rows_v, sem):
        wid = lax.axis_index("s") * n_cores + lax.axis_index("c")

        @pl.loop(0, per_w // SC_CHUNK)
        def _(c):
            base = wid * per_w + c * SC_CHUNK
            for kk in range(TOP_K):
                pltpu.sync_copy(dest_hbm.at[kk, pl.ds(base, SC_CHUNK)], idx_v.at[kk])
            copies = [pltpu.async_copy(table_hbm.at[idx_v.at[kk]], rows_v.at[kk], sem) for kk in range(TOP_K)]
            for cp in copies:
                cp.wait()
            for kk in range(TOP_K):
                pltpu.sync_copy(rows_v.at[kk], out_hbm.at[kk, pl.ds(base, SC_CHUNK)])

    return gather_kernel(table, dest_km)


def _combine_kernel(x1_ref, yg_ref, route_ref, g_ref, o_ref, *, final_norm):
    moe_lo = jnp.zeros((x1_ref.shape[0], HALF), F32)
    moe_hi = jnp.zeros((x1_ref.shape[0], HALF), F32)
    for kk in range(TOP_K):
        gate = route_ref[:, TOP_K + kk:TOP_K + kk + 1]
        y_lo, y_hi = _unpack_rows(yg_ref[kk])
        moe_lo = moe_lo + gate * y_lo
        moe_hi = moe_hi + gate * y_hi
    acc = x1_ref[...] + jnp.concatenate([moe_lo, moe_hi], axis=1)
    o_ref[...] = _rms(acc) * g_ref[...] if final_norm else acc


def _combine(x1, yg, route, g_final, final_norm):
    t = x1.shape[0]
    tm = TM_COMB
    return pl.pallas_call(
        functools.partial(_combine_kernel, final_norm=final_norm),
        grid=(t // tm,),
        in_specs=[pl.BlockSpec((tm, D_MODEL), lambda i: (i, 0)),
                  pl.BlockSpec((TOP_K, tm, HALF), lambda i: (0, i, 0)),
                  pl.BlockSpec((tm, LANES), lambda i: (i, 0)),
                  pl.BlockSpec((1, D_MODEL), lambda i: (0, 0))],
        out_specs=pl.BlockSpec((tm, D_MODEL), lambda i: (i, 0)),
        out_shape=jax.ShapeDtypeStruct((t, D_MODEL), F32),
        compiler_params=pltpu.CompilerParams(dimension_semantics=("arbitrary",)),
        name="combine",
    )(x1, yg, route, g_final)


def _prep_in_weights(w_in, w_uq, w_ukv):
    w_z = w_in[:, 0:512]
    w_xbc = w_in[:, 512:1536]
    w_dt = w_in[:, 1536:1544]
    w_cq = w_in[:, 1544:1800]
    w_ckv = w_in[:, 1800:1928]
    w_kr = w_in[:, 1928:1960]
    half = MLA_ROPE // 2
    misc1 = jnp.zeros((D_MODEL, LANES), F32).at[:, 0:SSD_HEADS].set(w_dt).at[:, MLA_NOPE:MLA_QK].set(w_kr)
    misc2 = (jnp.zeros((D_MODEL, LANES), F32)
             .at[:, MLA_NOPE:MLA_NOPE + half].set(w_kr[:, half:])
             .at[:, MLA_NOPE + half:MLA_QK].set(w_kr[:, :half]))
    w1 = jnp.concatenate([w_z, w_xbc, w_cq, w_ckv, misc1, misc2], axis=1).astype(BF16)

    wq3 = w_uq.reshape(MLA_Q_RANK, MLA_HEADS, MLA_QK)
    main = jnp.zeros((MLA_Q_RANK, MLA_HEADS, LANES), F32).at[:, :, 0:MLA_QK].set(wq3)
    swap = (jnp.zeros((MLA_Q_RANK, MLA_HEADS, LANES), F32)
            .at[:, :, MLA_NOPE:MLA_NOPE + half].set(wq3[:, :, MLA_NOPE + half:])
            .at[:, :, MLA_NOPE + half:MLA_QK].set(wq3[:, :, MLA_NOPE:MLA_NOPE + half]))
    wq = jnp.concatenate([main.reshape(MLA_Q_RANK, -1), swap.reshape(MLA_Q_RANK, -1)], axis=1).astype(BF16)

    wkv3 = w_ukv.reshape(MLA_KV_RANK, MLA_HEADS, MLA_NOPE + MLA_V)
    kpart = jnp.zeros((MLA_KV_RANK, MLA_HEADS, LANES), F32).at[:, :, 0:MLA_NOPE].set(wkv3[:, :, :MLA_NOPE])
    vpart = wkv3[:, :, MLA_NOPE:]
    wkv = jnp.concatenate([kpart.reshape(MLA_KV_RANK, -1), vpart.reshape(MLA_KV_RANK, -1)], axis=1).astype(BF16)
    return w1, wq, wkv


def _rope_consts():
    half = MLA_ROPE // 2
    inv_freq = ROPE_THETA ** (-jnp.arange(0, MLA_ROPE, 2, dtype=F32) / MLA_ROPE)
    invf = (jnp.zeros((1, LANES), F32)
            .at[0, MLA_NOPE:MLA_NOPE + half].set(inv_freq)
            .at[0, MLA_NOPE + half:MLA_QK].set(inv_freq))
    sgn = (jnp.zeros((1, LANES), F32)
           .at[0, MLA_NOPE:MLA_NOPE + half].set(-1.0)
           .at[0, MLA_NOPE + half:MLA_QK].set(1.0))
    return invf, sgn


def _pad_lanes(v, fill=0.0):
    return jnp.full((1, LANES), fill, F32).at[0, :v.shape[0]].set(v)


def kernel(x, positions, norm_mix_g, w_in, conv_w, conv_b, dt_bias, a_log, d_skip, ssd_norm_g, q_norm_g, w_uq, kv_norm_g, w_ukv, w_out, norm_ffn_g, w_router, b_router, w_gate_up, b_gate_up, w_down, b_down, norm_final_g):
    bsz, seqlen, d = x.shape
    t = bsz * seqlen
    depth = w_in.shape[0]
    x2 = x.reshape(t, d)
    pos2 = positions.reshape(t, 1).astype(I32)
    invf, sgn = _rope_consts()

    for l in range(depth):
        w1, wq, wkv = _prep_in_weights(w_in[l], w_uq[l], w_ukv[l])
        z, xbc, dtm, q, k, v = _inproj(
            x2, pos2, norm_mix_g[l][None, :], w1, q_norm_g[l][None, :], wq, kv_norm_g[l][None, :], wkv,
            invf, sgn, bsz, seqlen)
        y_ssd = _ssd(z, xbc, dtm, conv_w[l], conv_b[l][None, :], _pad_lanes(dt_bias[l]), _pad_lanes(a_log[l]),
                     jnp.repeat(d_skip[l], SSD_HEAD_DIM)[None, :], ssd_norm_g[l][None, :], bsz, seqlen)
        y_mla = _attention(q, k, v, bsz, seqlen).reshape(t, MLA_WIDTH)

        wr = jnp.zeros((d, LANES), F32).at[:, :N_EXPERTS].set(w_router[l])
        wr_hi = wr.astype(BF16)
        wr_lo = (wr - wr_hi.astype(F32)).astype(BF16)
        x1, h2p, route, cnt = _outproj(x2, y_ssd, y_mla, w_out[l].astype(BF16), norm_ffn_g[l][None, :],
                                       jnp.concatenate([wr_hi, wr_lo], axis=1), _pad_lanes(b_router[l]))

        counts = cnt[0, :N_EXPERTS].astype(I32)
        padded = ((counts + MOE_BM - 1) // MOE_BM) * MOE_BM
        end_pad = jnp.cumsum(padded)
        start_pad = end_pad - padded
        n_slots = t * TOP_K + N_EXPERTS * MOE_BM
        n_blocks = n_slots // MOE_BM
        block_start = jnp.arange(n_blocks, dtype=I32) * MOE_BM
        block_e = jnp.minimum(jnp.sum(block_start[:, None] >= end_pad[None, :], axis=1), N_EXPERTS - 1).astype(I32)
        block_valid = jnp.clip(counts[block_e] - (block_start - start_pad[block_e]), 0, MOE_BM).astype(I32)
        top_idx = route[:, 0:TOP_K].astype(I32)
        rank = route[:, 2 * TOP_K:3 * TOP_K].astype(I32)
        dest_km = (start_pad[top_idx] + rank).T

        xs = _sc_scatter_rows(h2p, dest_km, n_slots)
        ys = _expert_ffn(block_e, block_valid, xs, w_gate_up[l], b_gate_up[l][:, None, :], w_down[l],
                         b_down[l][:, None, :])
        yg = _sc_gather_rows(ys, dest_km)
        x2 = _combine(x1, yg, route, norm_final_g[None, :], l == depth - 1)
    return x2.reshape(bsz, seqlen, d)
```

```python
import functools

import jax
import jax.numpy as jnp
import numpy as np
from jax import lax
from jax.experimental import pallas as pl
from jax.experimental.pallas import tpu as pltpu
from jax.experimental.pallas import tpu_sc as plsc

F32 = jnp.float32
BF16 = jnp.bfloat16
I32 = jnp.int32
HIGHEST = lax.Precision.HIGHEST

D_MODEL = 1024
EPS = 1e-6
LANES = 128

SSD_HEADS = 8
SSD_HEAD_DIM = 64
SSD_WIDTH = 512
SSD_STATE = 128
SSD_CONV = 4
SSD_CHUNK = 128
SSD_CONV_DIM = 1024
CONV_PAD = 8

MLA_HEADS = 8
MLA_Q_RANK = 256
MLA_KV_RANK = 128
MLA_NOPE = 64
MLA_ROPE = 32
MLA_V = 64
MLA_QK = MLA_NOPE + MLA_ROPE
MLA_WIDTH = 512
ROPE_THETA = 10000.0

N_EXPERTS = 32
TOP_K = 4
D_FF = 1024
SWIGLU_LIMIT = 7.0
SWIGLU_ALPHA = 1.702

IN_W = 512 + 1024 + 256 + 128 + 128 + 128

TM_PROJ = 512
TQ = 512
TK_WIDE = 1024
ATT_HPS = 4
MOE_BM = 256
TM_COMB = 512

NT_DIMS = (((1,), (1,)), ((), ()))


def _rms(x):
    return x * lax.rsqrt(jnp.mean(x * x, axis=-1, keepdims=True) + EPS)


HALF = D_MODEL // 2
HI_MASK = np.int32(-65536)


def _pack_rows(a):
    lo = lax.bitcast_convert_type(a[:, :HALF].astype(BF16).astype(F32), I32)
    hi = lax.bitcast_convert_type(a[:, HALF:].astype(BF16).astype(F32), I32)
    return (hi & HI_MASK) | lax.shift_right_logical(lo, 16)


def _unpack_rows(p):
    lo = lax.bitcast_convert_type(lax.shift_left(p, 16), F32)
    hi = lax.bitcast_convert_type(p & HI_MASK, F32)
    return lo, hi


def _inproj_kernel(x_ref, pos_ref, g_ref, w1_ref, qg_ref, wq_ref, kvg_ref, wkv_ref, invf_ref, sgn_ref,
                   z_ref, xbc_ref, dtm_ref, q_ref, k_ref, v_ref):
    x = x_ref[...]
    h = (_rms(x) * g_ref[...]).astype(BF16)
    p = jnp.dot(h, w1_ref[...], preferred_element_type=F32)
    z_ref[...] = p[:, 0:512]
    xbc_ref[...] = p[:, 512:1536]
    cq = p[:, 1536:1792]
    ckv = p[:, 1792:1920]
    m1 = p[:, 1920:2048]
    m2 = p[:, 2048:2176]
    dtm_ref[...] = m1

    lane = lax.broadcasted_iota(I32, (1, LANES), 1)
    ang = pos_ref[...].astype(F32) * invf_ref[...]
    rope_lane = (lane >= MLA_NOPE) & (lane < MLA_QK)
    cos_t = jnp.where(rope_lane, jnp.cos(ang), 0.0)
    sin_t = jnp.sin(ang) * sgn_ref[...]
    cosq_t = jnp.where(lane < MLA_NOPE, 1.0, cos_t)
    scale = MLA_QK ** -0.5

    cqn = (_rms(cq) * qg_ref[...]).astype(BF16)
    qq = jnp.dot(cqn, wq_ref[...], preferred_element_type=F32)
    ckvn = (_rms(ckv) * kvg_ref[...]).astype(BF16)
    kv = jnp.dot(ckvn, wkv_ref[...], preferred_element_type=F32)
    krot = m1 * cos_t + m2 * sin_t
    for h_i in range(MLA_HEADS):
        lo = h_i * LANES
        qm = qq[:, lo:lo + LANES]
        qs = qq[:, 1024 + lo:1024 + lo + LANES]
        q_ref[0, h_i] = ((qm * cosq_t + qs * sin_t) * scale).astype(BF16)
        k_ref[0, h_i] = (kv[:, lo:lo + LANES] + krot).astype(BF16)
    for pr in range(MLA_HEADS // 2):
        vp = kv[:, 1024 + pr * LANES:1024 + (pr + 1) * LANES]
        v_ref[0, 2 * pr] = jnp.where(lane < MLA_V, vp, 1.0).astype(BF16)
        v_ref[0, 2 * pr + 1] = jnp.where(lane >= MLA_V, vp, 1.0).astype(BF16)


def _inproj(x2, pos2, g_mix, w1, qg, wq, kvg, wkv, invf, sgn, bsz, seqlen):
    t = x2.shape[0]
    tm = TM_PROJ
    per_b = seqlen // tm
    full = lambda shape: pl.BlockSpec(shape, lambda i: (0,) * len(shape))
    head_spec = pl.BlockSpec((1, MLA_HEADS, tm, LANES), lambda i: (i // per_b, 0, i % per_b, 0))
    head_shape = jax.ShapeDtypeStruct((bsz, MLA_HEADS, seqlen, LANES), BF16)
    return pl.pallas_call(
        _inproj_kernel,
        grid=(t // tm,),
        in_specs=[
            pl.BlockSpec((tm, D_MODEL), lambda i: (i, 0)),
            pl.BlockSpec((tm, 1), lambda i: (i, 0)),
            full((1, D_MODEL)), full((D_MODEL, IN_W)),
            full((1, MLA_Q_RANK)), full((MLA_Q_RANK, 2048)),
            full((1, MLA_KV_RANK)), full((MLA_KV_RANK, 1536)),
            full((1, LANES)), full((1, LANES)),
        ],
        out_specs=[
            pl.BlockSpec((tm, 512), lambda i: (i, 0)),
            pl.BlockSpec((tm, 1024), lambda i: (i, 0)),
            pl.BlockSpec((tm, LANES), lambda i: (i, 0)),
            head_spec, head_spec, head_spec,
        ],
        out_shape=[
            jax.ShapeDtypeStruct((t, 512), F32),
            jax.ShapeDtypeStruct((t, 1024), F32),
            jax.ShapeDtypeStruct((t, LANES), F32),
            head_shape, head_shape, head_shape,
        ],
        compiler_params=pltpu.CompilerParams(
            dimension_semantics=("arbitrary",), vmem_limit_bytes=56 * 1024 * 1024),
        name="inproj",
    )(x2, pos2, g_mix, w1, qg, wq, kvg, wkv, invf, sgn)


def _ssd_kernel(z_ref, xbc_ref, dtm_ref, cw_ref, cb_ref, dtb_ref, alog_ref, dsk_ref, ng_ref,
                y_ref, ext_ref, st_ref):
    q = SSD_CHUNK

    @pl.when(pl.program_id(1) == 0)
    def _():
        ext_ref[0:CONV_PAD, :] = jnp.zeros((CONV_PAD, SSD_CONV_DIM), F32)
        st_ref[...] = jnp.zeros_like(st_ref)

    ext_ref[CONV_PAD:CONV_PAD + q, :] = xbc_ref[...]
    conv = cb_ref[...]
    for kk in range(SSD_CONV):
        off = CONV_PAD - (SSD_CONV - 1) + kk
        conv = conv + cw_ref[kk:kk + 1, :] * ext_ref[off:off + q, :]
    ext_ref[0:CONV_PAD, :] = ext_ref[q:q + CONV_PAD, :]
    u = conv * jax.nn.sigmoid(conv)
    xs = u[:, 0:512]
    bm = u[:, 512:768]
    cm = u[:, 768:1024]

    lane = lax.broadcasted_iota(I32, (1, LANES), 1)
    xdt = dtm_ref[...] + dtb_ref[...]
    dt = jnp.maximum(xdt, 0.0) + jnp.log1p(jnp.exp(-jnp.abs(xdt)))
    a_neg = -jnp.exp(alog_ref[...])
    adt = jnp.where(lane < SSD_HEADS, dt * a_neg, 0.0)
    row = lax.broadcasted_iota(I32, (q, q), 0)
    col = lax.broadcasted_iota(I32, (q, q), 1)
    tril = row >= col
    cum_col = jnp.dot(tril.astype(F32), adt, precision=HIGHEST, preferred_element_type=F32)
    cum_row = cum_col.T

    hid = lax.broadcasted_iota(I32, (1, SSD_WIDTH), 1) // SSD_HEAD_DIM

    def expand(cols):
        out = jnp.zeros((q, SSD_WIDTH), F32)
        for h_i in range(SSD_HEADS):
            out = jnp.where(hid == h_i, cols[:, h_i:h_i + 1], out)
        return out

    dt_e = expand(dt)
    ac_e = expand(cum_col)
    last_e = ac_e[q - 1:q, :]
    xd = xs * dt_e
    w_end = xd * jnp.exp(last_e - ac_e)
    eac = jnp.exp(ac_e)
    cdec = jnp.exp(last_e)

    y_parts = []
    for g in range(2):
        gl = g * 256
        bg = bm[:, g * SSD_STATE:(g + 1) * SSD_STATE]
        cg = cm[:, g * SSD_STATE:(g + 1) * SSD_STATE].astype(BF16)
        scores = lax.dot_general(cg, bg.astype(BF16), NT_DIMS, preferred_element_type=F32)
        bgt = bg.T.astype(BF16)
        sprev = st_ref[g]
        yoff = jnp.dot(cg, sprev.astype(BF16), preferred_element_type=F32)
        st_ref[g] = sprev * cdec[:, gl:gl + 256] + jnp.dot(
            bgt, w_end[:, gl:gl + 256].astype(BF16), preferred_element_type=F32)
        for pr in range(2):
            pl_lo = gl + pr * LANES
            xdp = xd[:, pl_lo:pl_lo + LANES].astype(BF16)
            res = []
            for jj in range(2):
                h_i = g * 4 + pr * 2 + jj
                seg = cum_col[:, h_i:h_i + 1] - cum_row[h_i:h_i + 1, :]
                dec = jnp.exp(jnp.where(tril, seg, -jnp.inf))
                res.append(jnp.dot((scores * dec).astype(BF16), xdp, preferred_element_type=F32))
            ydiag = jnp.where(lane < SSD_HEAD_DIM, res[0], res[1])
            y_parts.append(ydiag + yoff[:, pr * LANES:(pr + 1) * LANES] * eac[:, pl_lo:pl_lo + LANES])
    y = jnp.concatenate(y_parts, axis=1) + dsk_ref[...] * xs
    zz = z_ref[...]
    y = y * (zz * jax.nn.sigmoid(zz))
    outs = []
    for g in range(2):
        yg = y[:, g * 256:(g + 1) * 256]
        outs.append(_rms(yg))
    y_ref[...] = (jnp.concatenate(outs, axis=1) * ng_ref[...]).astype(BF16)


def _ssd(z, xbc, dtm, cw, cb, dtb, alog, dsk, ng, bsz, seqlen):
    t = z.shape[0]
    q = SSD_CHUNK
    nc = seqlen // q
    full = lambda shape: pl.BlockSpec(shape, lambda b, c: (0,) * len(shape))
    row_spec = lambda width: pl.BlockSpec((q, width), lambda b, c: (b * nc + c, 0))
    return pl.pallas_call(
        _ssd_kernel,
        grid=(bsz, nc),
        in_specs=[row_spec(512), row_spec(1024), row_spec(LANES),
                  full((SSD_CONV, SSD_CONV_DIM)), full((1, SSD_CONV_DIM)),
                  full((1, LANES)), full((1, LANES)), full((1, SSD_WIDTH)), full((1, SSD_WIDTH))],
        out_specs=row_spec(512),
        out_shape=jax.ShapeDtypeStruct((t, SSD_WIDTH), BF16),
        scratch_shapes=[pltpu.VMEM((q + CONV_PAD, SSD_CONV_DIM), F32),
                        pltpu.VMEM((2, SSD_STATE, 256), F32)],
        compiler_params=pltpu.CompilerParams(dimension_semantics=("arbitrary", "arbitrary")),
        name="ssd",
    )(z, xbc, dtm, cw, cb, dtb, alog, dsk, ng)


def _attn_kernel(q_ref, k_ref, v_ref, o_ref, acc_ref):
    i = pl.program_id(2)
    lane = lax.broadcasted_iota(I32, (1, LANES), 1)
    acc_ref[...] = jnp.zeros_like(acc_ref)

    def step(start, width, m_all, masked):
        new_m = []
        for hh in range(ATT_HPS):
            kb = k_ref[0, hh, pl.ds(start, width), :]
            vb = v_ref[0, hh, pl.ds(start, width), :]
            s = lax.dot_general(q_ref[0, hh], kb, NT_DIMS, preferred_element_type=F32)
            if masked:
                r = lax.broadcasted_iota(I32, (TQ, width), 0)
                c = lax.broadcasted_iota(I32, (TQ, width), 1)
                s = jnp.where(c <= r, s, -jnp.inf)
            m_old = m_all[hh]
            m_new = jnp.maximum(m_old, jnp.max(s, axis=-1, keepdims=True))
            alpha = jnp.exp(m_old - m_new)
            p = jnp.exp(s - m_new).astype(BF16)
            acc_ref[hh] = acc_ref[hh] * alpha + jnp.dot(p, vb, preferred_element_type=F32)
            new_m.append(m_new)
        return tuple(new_m)

    m0 = jnp.full((TQ, 1), -jnp.inf, F32)
    per_wide = TK_WIDE // TQ
    n_wide = i // per_wide
    m_all = lax.fori_loop(
        0, n_wide, lambda j, m: step(pl.multiple_of(j * TK_WIDE, TK_WIDE), TK_WIDE, m, False), (m0,) * ATT_HPS)
    m_all = lax.fori_loop(
        n_wide * per_wide, i, lambda j, m: step(pl.multiple_of(j * TQ, TQ), TQ, m, False), m_all)
    step(pl.multiple_of(i * TQ, TQ), TQ, m_all, True)

    outs = []
    for pr in range(ATT_HPS // 2):
        a0 = acc_ref[2 * pr]
        a1 = acc_ref[2 * pr + 1]
        o0 = a0 / a0[:, MLA_V:MLA_V + 1]
        o1 = a1 / a1[:, 0:1]
        outs.append(jnp.where(lane < MLA_V, o0, o1))
    o_ref[0] = jnp.concatenate(outs, axis=1).astype(BF16)


def _attention(q, k, v, bsz, seqlen):
    nq = seqlen // TQ
    hps = ATT_HPS
    kv_spec = pl.BlockSpec((1, hps, seqlen, LANES), lambda b, p, i: (b, p, 0, 0))
    return pl.pallas_call(
        _attn_kernel,
        grid=(bsz, MLA_HEADS // hps, nq),
        in_specs=[pl.BlockSpec((1, hps, TQ, LANES), lambda b, p, i: (b, p, i, 0)), kv_spec, kv_spec],
        out_specs=pl.BlockSpec((1, TQ, hps * MLA_V), lambda b, p, i: (b, i, p)),
        out_shape=jax.ShapeDtypeStruct((bsz, seqlen, MLA_WIDTH), BF16),
        scratch_shapes=[pltpu.VMEM((hps, TQ, LANES), F32)],
        compiler_params=pltpu.CompilerParams(
            dimension_semantics=("arbitrary", "arbitrary", "arbitrary"),
            vmem_limit_bytes=56 * 1024 * 1024),
        name="attention",
    )(q, k, v)


def _outproj_kernel(x_ref, ys_ref, ym_ref, wo_ref, g_ref, wr_ref, br_ref,
                    x1_ref, h2_ref, route_ref, cnt_ref):
    tm = x_ref.shape[0]

    @pl.when(pl.program_id(0) == 0)
    def _():
        cnt_ref[...] = jnp.zeros_like(cnt_ref)

    mix = (jnp.dot(ys_ref[...], wo_ref[0:512, :], preferred_element_type=F32)
           + jnp.dot(ym_ref[...], wo_ref[512:1024, :], preferred_element_type=F32))
    x1 = x_ref[...] + mix
    x1_ref[...] = x1
    h2 = _rms(x1) * g_ref[...]
    h2_ref[...] = _pack_rows(h2)

    lane = lax.broadcasted_iota(I32, (1, LANES), 1)
    h_hi = h2.astype(BF16)
    h_lo = (h2 - h_hi.astype(F32)).astype(BF16)
    hh = jnp.dot(h_hi, wr_ref[...], preferred_element_type=F32)
    lh = jnp.dot(h_lo, wr_ref[:, 0:LANES], preferred_element_type=F32)
    logits = hh[:, 0:LANES] + (hh[:, LANES:2 * LANES] + lh) + br_ref[...]
    logits = jnp.where(lane < N_EXPERTS, logits, -jnp.inf)

    vals, idxs, hots = [], [], []
    for _ in range(TOP_K):
        mx = jnp.max(logits, axis=-1, keepdims=True)
        idx = jnp.min(jnp.where(logits == mx, lane, LANES), axis=-1, keepdims=True)
        hot = lane == idx
        logits = jnp.where(hot, -jnp.inf, logits)
        vals.append(mx)
        idxs.append(idx)
        hots.append(hot)
    exps = [jnp.exp(v - vals[0]) for v in vals]
    denom = exps[0] + exps[1] + exps[2] + exps[3]

    multi = (hots[0] | hots[1] | hots[2] | hots[3])
    multi_f = jnp.where(multi, 1.0, 0.0)
    r = lax.broadcasted_iota(I32, (tm, tm), 0)
    c = lax.broadcasted_iota(I32, (tm, tm), 1)
    strict = jnp.where(r > c, 1.0, 0.0).astype(BF16)
    before = jnp.dot(strict, multi_f.astype(BF16), preferred_element_type=F32) + cnt_ref[0:1, :]
    cnt_ref[...] = cnt_ref[...] + jnp.sum(multi_f, axis=0, keepdims=True)

    out = jnp.zeros((tm, LANES), F32)
    for kk in range(TOP_K):
        rank = jnp.sum(jnp.where(hots[kk], before, 0.0), axis=-1, keepdims=True)
        out = jnp.where(lane == kk, idxs[kk].astype(F32), out)
        out = jnp.where(lane == TOP_K + kk, exps[kk] / denom, out)
        out = jnp.where(lane == 2 * TOP_K + kk, rank, out)
    route_ref[...] = out


def _outproj(x2, y_ssd, y_mla, wo, g_ffn, wr, br):
    t = x2.shape[0]
    tm = TM_PROJ
    full = lambda shape: pl.BlockSpec(shape, lambda i: (0,) * len(shape))
    rows = lambda width: pl.BlockSpec((tm, width), lambda i: (i, 0))
    return pl.pallas_call(
        _outproj_kernel,
        grid=(t // tm,),
        in_specs=[rows(D_MODEL), rows(512), rows(512), full((1024, D_MODEL)), full((1, D_MODEL)),
                  full((D_MODEL, 2 * LANES)), full((1, LANES))],
        out_specs=[rows(D_MODEL), rows(HALF), rows(LANES), full((8, LANES))],
        out_shape=[jax.ShapeDtypeStruct((t, D_MODEL), F32),
                   jax.ShapeDtypeStruct((t, HALF), I32),
                   jax.ShapeDtypeStruct((t, LANES), F32),
                   jax.ShapeDtypeStruct((8, LANES), F32)],
        compiler_params=pltpu.CompilerParams(
            dimension_semantics=("arbitrary",), vmem_limit_bytes=40 * 1024 * 1024),
        name="outproj_router",
    )(x2, y_ssd, y_mla, wo, g_ffn, wr, br)


def _ffn_kernel(be_ref, bv_ref, xs_ref, wgu_ref, bgu_ref, wd_ref, bd_ref, ys_ref, wgu_bf, wd_bf):
    i = pl.program_id(0)
    prev = be_ref[jnp.maximum(i - 1, 0)]
    changed = (i == 0) | (be_ref[i] != prev)
    valid = bv_ref[i]

    @pl.when(changed & (valid > 0))
    def _():
        wgu_bf[...] = wgu_ref[0].astype(BF16)
        wd_bf[...] = wd_ref[0].astype(BF16)

    @pl.when(valid > 0)
    def _():
        row = lax.broadcasted_iota(I32, (MOE_BM, 1), 0)
        x_lo, x_hi = _unpack_rows(jnp.where(row < valid, xs_ref[...], 0))
        gu = (jnp.dot(x_lo.astype(BF16), wgu_bf[0:HALF, :], preferred_element_type=F32)
              + jnp.dot(x_hi.astype(BF16), wgu_bf[HALF:D_MODEL, :], preferred_element_type=F32)
              + bgu_ref[0])
        gate = jnp.minimum(gu[:, :D_FF], SWIGLU_LIMIT)
        up = jnp.clip(gu[:, D_FF:], -SWIGLU_LIMIT, SWIGLU_LIMIT)
        glu = gate * jax.nn.sigmoid(SWIGLU_ALPHA * gate)
        mid = ((up + 1.0) * glu).astype(BF16)
        ys_ref[...] = _pack_rows(jnp.dot(mid, wd_bf[...], preferred_element_type=F32) + bd_ref[0])

    @pl.when(valid == 0)
    def _():
        ys_ref[...] = jnp.zeros_like(ys_ref)


def _expert_ffn(block_e, block_valid, xs, wgu, bgu, wd, bd):
    n_slots = xs.shape[0]
    bm = MOE_BM
    grid_spec = pltpu.PrefetchScalarGridSpec(
        num_scalar_prefetch=2,
        grid=(n_slots // bm,),
        in_specs=[
            pl.BlockSpec((bm, HALF), lambda i, be, bv: (i, 0)),
            pl.BlockSpec((1, D_MODEL, 2 * D_FF), lambda i, be, bv: (be[i], 0, 0)),
            pl.BlockSpec((1, 1, 2 * D_FF), lambda i, be, bv: (be[i], 0, 0)),
            pl.BlockSpec((1, D_FF, D_MODEL), lambda i, be, bv: (be[i], 0, 0)),
            pl.BlockSpec((1, 1, D_MODEL), lambda i, be, bv: (be[i], 0, 0)),
        ],
        out_specs=pl.BlockSpec((bm, HALF), lambda i, be, bv: (i, 0)),
        scratch_shapes=[pltpu.VMEM((D_MODEL, 2 * D_FF), BF16), pltpu.VMEM((D_FF, D_MODEL), BF16)],
    )
    return pl.pallas_call(
        _ffn_kernel,
        grid_spec=grid_spec,
        out_shape=jax.ShapeDtypeStruct((n_slots, HALF), I32),
        compiler_params=pltpu.CompilerParams(
            dimension_semantics=("arbitrary",), vmem_limit_bytes=56 * 1024 * 1024),
        name="expert_ffn",
    )(block_e, block_valid, xs, wgu, bgu, wd, bd)


SC_CHUNK = 32


def _sc_workers():
    info = plsc.get_sparse_core_info()
    return info.num_cores, info.num_cores * info.num_subcores


def _sc_scatter_rows(rows, dest_km, n_out):
    t, w = rows.shape
    n_cores, n_workers = _sc_workers()
    per_w = t // n_workers
    mesh = plsc.VectorSubcoreMesh(core_axis_name="c", subcore_axis_name="s")

    @functools.partial(
        pl.kernel, mesh=mesh, out_type=jax.ShapeDtypeStruct((n_out, w), rows.dtype),
        scratch_types=[pltpu.VMEM((TOP_K, SC_CHUNK), I32), pltpu.VMEM((SC_CHUNK, w), rows.dtype),
                       pltpu.SemaphoreType.DMA],
        name="sc_dispatch_scatter")
    def scatter_kernel(rows_hbm, dest_hbm, out_hbm, idx_v, rows_v, sem):
        wid = lax.axis_index("s") * n_cores + lax.axis_index("c")

        @pl.loop(0, per_w // SC_CHUNK)
        def _(c):
            base = wid * per_w + c * SC_CHUNK
            pltpu.sync_copy(rows_hbm.at[pl.ds(base, SC_CHUNK)], rows_v)
            for kk in range(TOP_K):
                pltpu.sync_copy(dest_hbm.at[kk, pl.ds(base, SC_CHUNK)], idx_v.at[kk])
            copies = [pltpu.async_copy(rows_v, out_hbm.at[idx_v.at[kk]], sem) for kk in range(TOP_K)]
            for cp in copies:
                cp.wait()

    return scatter_kernel(rows, dest_km)


def _sc_gather_rows(table, dest_km):
    _, w = table.shape
    t = dest_km.shape[1]
    n_cores, n_workers = _sc_workers()
    per_w = t // n_workers
    mesh = plsc.VectorSubcoreMesh(core_axis_name="c", subcore_axis_name="s")

    @functools.partial(
        pl.kernel, mesh=mesh, out_type=jax.ShapeDtypeStruct((TOP_K, t, w), table.dtype),
        scratch_types=[pltpu.VMEM((TOP_K, SC_CHUNK), I32), pltpu.VMEM((TOP_K, SC_CHUNK, w), table.dtype),
                       pltpu.SemaphoreType.DMA],
        name="sc_combine_gather")
    def gather_kernel(table_hbm, dest_hbm, out_hbm, idx_v, rows_v, sem):
        wid = lax.axis_index("s") * n_cores + lax.axis_index("c")

        @pl.loop(0, per_w // SC_CHUNK)
        def _(c):
            base = wid * per_w + c * SC_CHUNK
            for kk in range(TOP_K):
                pltpu.sync_copy(dest_hbm.at[kk, pl.ds(base, SC_CHUNK)], idx_v.at[kk])
            copies = [pltpu.async_copy(table_hbm.at[idx_v.at[kk]], rows_v.at[kk], sem) for kk in range(TOP_K)]
            for cp in copies:
                cp.wait()
            for kk in range(TOP_K):
                pltpu.sync_copy(rows_v.at[kk], out_hbm.at[kk, pl.ds(base, SC_CHUNK)])

    return gather_kernel(table, dest_km)


def _combine_kernel(x1_ref, yg_ref, route_ref, g_ref, o_ref, *, final_norm):
    moe_lo = jnp.zeros((x1_ref.shape[0], HALF), F32)
    moe_hi = jnp.zeros((x1_ref.shape[0], HALF), F32)
    for kk in range(TOP_K):
        gate = route_ref[:, TOP_K + kk:TOP_K + kk + 1]
        y_lo, y_hi = _unpack_rows(yg_ref[kk])
        moe_lo = moe_lo + gate * y_lo
        moe_hi = moe_hi + gate * y_hi
    acc = x1_ref[...] + jnp.concatenate([moe_lo, moe_hi], axis=1)
    o_ref[...] = _rms(acc) * g_ref[...] if final_norm else acc


def _combine(x1, yg, route, g_final, final_norm):
    t = x1.shape[0]
    tm = TM_COMB
    return pl.pallas_call(
        functools.partial(_combine_kernel, final_norm=final_norm),
        grid=(t // tm,),
        in_specs=[pl.BlockSpec((tm, D_MODEL), lambda i: (i, 0)),
                  pl.BlockSpec((TOP_K, tm, HALF), lambda i: (0, i, 0)),
                  pl.BlockSpec((tm, LANES), lambda i: (i, 0)),
                  pl.BlockSpec((1, D_MODEL), lambda i: (0, 0))],
        out_specs=pl.BlockSpec((tm, D_MODEL), lambda i: (i, 0)),
        out_shape=jax.ShapeDtypeStruct((t, D_MODEL), F32),
        compiler_params=pltpu.CompilerParams(dimension_semantics=("arbitrary",)),
        name="combine",
    )(x1, yg, route, g_final)


def _prep_in_weights(w_in, w_uq, w_ukv):
    w_z = w_in[:, 0:512]
    w_xbc = w_in[:, 512:1536]
    w_dt = w_in[:, 1536:1544]
    w_cq = w_in[:, 1544:1800]
    w_ckv = w_in[:, 1800:1928]
    w_kr = w_in[:, 1928:1960]
    half = MLA_ROPE // 2
    misc1 = jnp.zeros((D_MODEL, LANES), F32).at[:, 0:SSD_HEADS].set(w_dt).at[:, MLA_NOPE:MLA_QK].set(w_kr)
    misc2 = (jnp.zeros((D_MODEL, LANES), F32)
             .at[:, MLA_NOPE:MLA_NOPE + half].set(w_kr[:, half:])
             .at[:, MLA_NOPE + half:MLA_QK].set(w_kr[:, :half]))
    w1 = jnp.concatenate([w_z, w_xbc, w_cq, w_ckv, misc1, misc2], axis=1).astype(BF16)

    wq3 = w_uq.reshape(MLA_Q_RANK, MLA_HEADS, MLA_QK)
    main = jnp.zeros((MLA_Q_RANK, MLA_HEADS, LANES), F32).at[:, :, 0:MLA_QK].set(wq3)
    swap = (jnp.zeros((MLA_Q_RANK, MLA_HEADS, LANES), F32)
            .at[:, :, MLA_NOPE:MLA_NOPE + half].set(wq3[:, :, MLA_NOPE + half:])
            .at[:, :, MLA_NOPE + half:MLA_QK].set(wq3[:, :, MLA_NOPE:MLA_NOPE + half]))
    wq = jnp.concatenate([main.reshape(MLA_Q_RANK, -1), swap.reshape(MLA_Q_RANK, -1)], axis=1).astype(BF16)

    wkv3 = w_ukv.reshape(MLA_KV_RANK, MLA_HEADS, MLA_NOPE + MLA_V)
    kpart = jnp.zeros((MLA_KV_RANK, MLA_HEADS, LANES), F32).at[:, :, 0:MLA_NOPE].set(wkv3[:, :, :MLA_NOPE])
    vpart = wkv3[:, :, MLA_NOPE:]
    wkv = jnp.concatenate([kpart.reshape(MLA_KV_RANK, -1), vpart.reshape(MLA_KV_RANK, -1)], axis=1).astype(BF16)
    return w1, wq, wkv


def _rope_consts():
    half = MLA_ROPE // 2
    inv_freq = ROPE_THETA ** (-jnp.arange(0, MLA_ROPE, 2, dtype=F32) / MLA_ROPE)
    invf = (jnp.zeros((1, LANES), F32)
            .at[0, MLA_NOPE:MLA_NOPE + half].set(inv_freq)
            .at[0, MLA_NOPE + half:MLA_QK].set(inv_freq))
    sgn = (jnp.zeros((1, LANES), F32)
           .at[0, MLA_NOPE:MLA_NOPE + half].set(-1.0)
           .at[0, MLA_NOPE + half:MLA_QK].set(1.0))
    return invf, sgn


def _pad_lanes(v, fill=0.0):
    return jnp.full((1, LANES), fill, F32).at[0, :v.shape[0]].set(v)


def kernel(x, positions, norm_mix_g, w_in, conv_w, conv_b, dt_bias, a_log, d_skip, ssd_norm_g, q_norm_g, w_uq, kv_norm_g, w_ukv, w_out, norm_ffn_g, w_router, b_router, w_gate_up, b_gate_up, w_down, b_down, norm_final_g):
    bsz, seqlen, d = x.shape
    t = bsz * seqlen
    depth = w_in.shape[0]
    x2 = x.reshape(t, d)
    pos2 = positions.reshape(t, 1).astype(I32)
    invf, sgn = _rope_consts()

    for l in range(depth):
        w1, wq, wkv = _prep_in_weights(w_in[l], w_uq[l], w_ukv[l])
        z, xbc, dtm, q, k, v = _inproj(
            x2, pos2, norm_mix_g[l][None, :], w1, q_norm_g[l][None, :], wq, kv_norm_g[l][None, :], wkv,
            invf, sgn, bsz, seqlen)
        y_ssd = _ssd(z, xbc, dtm, conv_w[l], conv_b[l][None, :], _pad_lanes(dt_bias[l]), _pad_lanes(a_log[l]),
                     jnp.repeat(d_skip[l], SSD_HEAD_DIM)[None, :], ssd_norm_g[l][None, :], bsz, seqlen)
        y_mla = _attention(q, k, v, bsz, seqlen).reshape(t, MLA_WIDTH)

        wr = jnp.zeros((d, LANES), F32).at[:, :N_EXPERTS].set(w_router[l])
        wr_hi = wr.astype(BF16)
        wr_lo = (wr - wr_hi.astype(F32)).astype(BF16)
        x1, h2p, route, cnt = _outproj(x2, y_ssd, y_mla, w_out[l].astype(BF16), norm_ffn_g[l][None, :],
                                       jnp.concatenate([wr_hi, wr_lo], axis=1), _pad_lanes(b_router[l]))

        counts = cnt[0, :N_EXPERTS].astype(I32)
        padded = ((counts + MOE_BM - 1) // MOE_BM) * MOE_BM
        end_pad = jnp.cumsum(padded)
        start_pad = end_pad - padded
        n_slots = t * TOP_K + N_EXPERTS * MOE_BM
        n_blocks = n_slots // MOE_BM
        block_start = jnp.arange(n_blocks, dtype=I32) * MOE_BM
        block_e = jnp.minimum(jnp.sum(block_start[:, None] >= end_pad[None, :], axis=1), N_EXPERTS - 1).astype(I32)
        block_valid = jnp.clip(counts[block_e] - (block_start - start_pad[block_e]), 0, MOE_BM).astype(I32)
        top_idx = route[:, 0:TOP_K].astype(I32)
        rank = route[:, 2 * TOP_K:3 * TOP_K].astype(I32)
        dest_km = (start_pad[top_idx] + rank).T

        xs = _sc_scatter_rows(h2p, dest_km, n_slots)
        ys = _expert_ffn(block_e, block_valid, xs, w_gate_up[l], b_gate_up[l][:, None, :], w_down[l],
                         b_down[l][:, None, :])
        yg = _sc_gather_rows(ys, dest_km)
        x2 = _combine(x1, yg, route, norm_final_g[None, :], l == depth - 1)
    return x2.reshape(bsz, seqlen, d)
```

```python
import functools

import jax
import jax.numpy as jnp
import numpy as np
from jax import lax
from jax.experimental import pallas as pl
from jax.experimental.pallas import tpu as pltpu
from jax.experimental.pallas import tpu_sc as plsc

F32 = jnp.float32
BF16 = jnp.bfloat16
I32 = jnp.int32
HIGHEST = lax.Precision.HIGHEST

D_MODEL = 1024
EPS = 1e-6
LANES = 128

SSD_HEADS = 8
SSD_HEAD_DIM = 64
SSD_WIDTH = 512
SSD_STATE = 128
SSD_CONV = 4
SSD_CHUNK = 128
SSD_CONV_DIM = 1024
CONV_PAD = 8

MLA_HEADS = 8
MLA_Q_RANK = 256
MLA_KV_RANK = 128
MLA_NOPE = 64
MLA_ROPE = 32
MLA_V = 64
MLA_QK = MLA_NOPE + MLA_ROPE
MLA_WIDTH = 512
ROPE_THETA = 10000.0

N_EXPERTS = 32
TOP_K = 4
D_FF = 1024
SWIGLU_LIMIT = 7.0
SWIGLU_ALPHA = 1.702

IN_W = 512 + 1024 + 256 + 128 + 128 + 128

TM_PROJ = 512
TQ = 512
TK_WIDE = 1024
ATT_HPS = 4
MOE_BM = 256
TM_COMB = 512
TM_PLAN = 2048

NT_DIMS = (((1,), (1,)), ((), ()))


def _rms(x):
    return x * lax.rsqrt(jnp.mean(x * x, axis=-1, keepdims=True) + EPS)


HALF = D_MODEL // 2
HI_MASK = np.int32(-65536)


def _pack_rows(a):
    lo = lax.bitcast_convert_type(a[:, :HALF].astype(BF16).astype(F32), I32)
    hi = lax.bitcast_convert_type(a[:, HALF:].astype(BF16).astype(F32), I32)
    return (hi & HI_MASK) | lax.shift_right_logical(lo, 16)


def _unpack_rows(p):
    lo = lax.bitcast_convert_type(lax.shift_left(p, 16), F32)
    hi = lax.bitcast_convert_type(p & HI_MASK, F32)
    return lo, hi


def _inproj_kernel(x_ref, pos_ref, g_ref, w1_ref, qg_ref, wq_ref, kvg_ref, wkv_ref, invf_ref, sgn_ref,
                   z_ref, xbc_ref, dtm_ref, q_ref, k_ref, v_ref):
    x = x_ref[...]
    h = (_rms(x) * g_ref[...]).astype(BF16)
    p = jnp.dot(h, w1_ref[...], preferred_element_type=F32)
    z_ref[...] = p[:, 0:512]
    xbc_ref[...] = p[:, 512:1536]
    cq = p[:, 1536:1792]
    ckv = p[:, 1792:1920]
    m1 = p[:, 1920:2048]
    m2 = p[:, 2048:2176]
    dtm_ref[...] = m1

    lane = lax.broadcasted_iota(I32, (1, LANES), 1)
    ang = pos_ref[...].astype(F32) * invf_ref[...]
    rope_lane = (lane >= MLA_NOPE) & (lane < MLA_QK)
    cos_t = jnp.where(rope_lane, jnp.cos(ang), 0.0)
    sin_t = jnp.sin(ang) * sgn_ref[...]
    cosq_t = jnp.where(lane < MLA_NOPE, 1.0, cos_t)
    scale = MLA_QK ** -0.5

    cqn = (_rms(cq) * qg_ref[...]).astype(BF16)
    qq = jnp.dot(cqn, wq_ref[...], preferred_element_type=F32)
    ckvn = (_rms(ckv) * kvg_ref[...]).astype(BF16)
    kv = jnp.dot(ckvn, wkv_ref[...], preferred_element_type=F32)
    krot = m1 * cos_t + m2 * sin_t
    for h_i in range(MLA_HEADS):
        lo = h_i * LANES
        qm = qq[:, lo:lo + LANES]
        qs = qq[:, 1024 + lo:1024 + lo + LANES]
        q_ref[0, h_i] = ((qm * cosq_t + qs * sin_t) * scale).astype(BF16)
        k_ref[0, h_i] = (kv[:, lo:lo + LANES] + krot).astype(BF16)
    for pr in range(MLA_HEADS // 2):
        vp = kv[:, 1024 + pr * LANES:1024 + (pr + 1) * LANES]
        v_ref[0, 2 * pr] = jnp.where(lane < MLA_V, vp, 1.0).astype(BF16)
        v_ref[0, 2 * pr + 1] = jnp.where(lane >= MLA_V, vp, 1.0).astype(BF16)


def _inproj(x2, pos2, g_mix, w1, qg, wq, kvg, wkv, invf, sgn, bsz, seqlen):
    t = x2.shape[0]
    tm = TM_PROJ
    per_b = seqlen // tm
    full = lambda shape: pl.BlockSpec(shape, lambda i: (0,) * len(shape))
    head_spec = pl.BlockSpec((1, MLA_HEADS, tm, LANES), lambda i: (i // per_b, 0, i % per_b, 0))
    head_shape = jax.ShapeDtypeStruct((bsz, MLA_HEADS, seqlen, LANES), BF16)
    return pl.pallas_call(
        _inproj_kernel,
        grid=(t // tm,),
        in_specs=[
            pl.BlockSpec((tm, D_MODEL), lambda i: (i, 0)),
            pl.BlockSpec((tm, 1), lambda i: (i, 0)),
            full((1, D_MODEL)), full((D_MODEL, IN_W)),
            full((1, MLA_Q_RANK)), full((MLA_Q_RANK, 2048)),
            full((1, MLA_KV_RANK)), full((MLA_KV_RANK, 1536)),
            full((1, LANES)), full((1, LANES)),
        ],
        out_specs=[
            pl.BlockSpec((tm, 512), lambda i: (i, 0)),
            pl.BlockSpec((tm, 1024), lambda i: (i, 0)),
            pl.BlockSpec((tm, LANES), lambda i: (i, 0)),
            head_spec, head_spec, head_spec,
        ],
        out_shape=[
            jax.ShapeDtypeStruct((t, 512), F32),
            jax.ShapeDtypeStruct((t, 1024), F32),
            jax.ShapeDtypeStruct((t, LANES), F32),
            head_shape, head_shape, head_shape,
        ],
        compiler_params=pltpu.CompilerParams(
            dimension_semantics=("arbitrary",), vmem_limit_bytes=56 * 1024 * 1024),
        name="inproj",
    )(x2, pos2, g_mix, w1, qg, wq, kvg, wkv, invf, sgn)


def _ssd_kernel(z_ref, xbc_ref, dtm_ref, cw_ref, cb_ref, dtb_ref, alog_ref, dsk_ref, ng_ref,
                y_ref, ext_ref, st_ref):
    q = SSD_CHUNK

    @pl.when(pl.program_id(1) == 0)
    def _():
        ext_ref[0:CONV_PAD, :] = jnp.zeros((CONV_PAD, SSD_CONV_DIM), F32)
        st_ref[...] = jnp.zeros_like(st_ref)

    ext_ref[CONV_PAD:CONV_PAD + q, :] = xbc_ref[...]
    conv = cb_ref[...]
    for kk in range(SSD_CONV):
        off = CONV_PAD - (SSD_CONV - 1) + kk
        conv = conv + cw_ref[kk:kk + 1, :] * ext_ref[off:off + q, :]
    ext_ref[0:CONV_PAD, :] = ext_ref[q:q + CONV_PAD, :]
    u = conv * jax.nn.sigmoid(conv)
    xs = u[:, 0:512]
    bm = u[:, 512:768]
    cm = u[:, 768:1024]

    lane = lax.broadcasted_iota(I32, (1, LANES), 1)
    xdt = dtm_ref[...] + dtb_ref[...]
    dt = jnp.maximum(xdt, 0.0) + jnp.log1p(jnp.exp(-jnp.abs(xdt)))
    a_neg = -jnp.exp(alog_ref[...])
    adt = jnp.where(lane < SSD_HEADS, dt * a_neg, 0.0)
    row = lax.broadcasted_iota(I32, (q, q), 0)
    col = lax.broadcasted_iota(I32, (q, q), 1)
    tril = row >= col
    cum_col = jnp.dot(tril.astype(F32), adt, precision=HIGHEST, preferred_element_type=F32)
    cum_row = cum_col.T

    hid = lax.broadcasted_iota(I32, (1, SSD_WIDTH), 1) // SSD_HEAD_DIM

    def expand(cols):
        out = jnp.zeros((q, SSD_WIDTH), F32)
        for h_i in range(SSD_HEADS):
            out = jnp.where(hid == h_i, cols[:, h_i:h_i + 1], out)
        return out

    dt_e = expand(dt)
    ac_e = expand(cum_col)
    last_e = ac_e[q - 1:q, :]
    xd = xs * dt_e
    w_end = xd * jnp.exp(last_e - ac_e)
    eac = jnp.exp(ac_e)
    cdec = jnp.exp(last_e)

    y_parts = []
    for g in range(2):
        gl = g * 256
        bg = bm[:, g * SSD_STATE:(g + 1) * SSD_STATE]
        cg = cm[:, g * SSD_STATE:(g + 1) * SSD_STATE].astype(BF16)
        scores = lax.dot_general(cg, bg.astype(BF16), NT_DIMS, preferred_element_type=F32)
        bgt = bg.T.astype(BF16)
        sprev = st_ref[g]
        yoff = jnp.dot(cg, sprev.astype(BF16), preferred_element_type=F32)
        st_ref[g] = sprev * cdec[:, gl:gl + 256] + jnp.dot(
            bgt, w_end[:, gl:gl + 256].astype(BF16), preferred_element_type=F32)
        for pr in range(2):
            pl_lo = gl + pr * LANES
            xdp = xd[:, pl_lo:pl_lo + LANES].astype(BF16)
            res = []
            for jj in range(2):
                h_i = g * 4 + pr * 2 + jj
                seg = cum_col[:, h_i:h_i + 1] - cum_row[h_i:h_i + 1, :]
                dec = jnp.exp(jnp.where(tril, seg, -jnp.inf))
                res.append(jnp.dot((scores * dec).astype(BF16), xdp, preferred_element_type=F32))
            ydiag = jnp.where(lane < SSD_HEAD_DIM, res[0], res[1])
            y_parts.append(ydiag + yoff[:, pr * LANES:(pr + 1) * LANES] * eac[:, pl_lo:pl_lo + LANES])
    y = jnp.concatenate(y_parts, axis=1) + dsk_ref[...] * xs
    zz = z_ref[...]
    y = y * (zz * jax.nn.sigmoid(zz))
    outs = []
    for g in range(2):
        yg = y[:, g * 256:(g + 1) * 256]
        outs.append(_rms(yg))
    y_ref[...] = (jnp.concatenate(outs, axis=1) * ng_ref[...]).astype(BF16)


def _ssd(z, xbc, dtm, cw, cb, dtb, alog, dsk, ng, bsz, seqlen):
    t = z.shape[0]
    q = SSD_CHUNK
    nc = seqlen // q
    full = lambda shape: pl.BlockSpec(shape, lambda b, c: (0,) * len(shape))
    row_spec = lambda width: pl.BlockSpec((q, width), lambda b, c: (b * nc + c, 0))
    return pl.pallas_call(
        _ssd_kernel,
        grid=(bsz, nc),
        in_specs=[row_spec(512), row_spec(1024), row_spec(LANES),
                  full((SSD_CONV, SSD_CONV_DIM)), full((1, SSD_CONV_DIM)),
                  full((1, LANES)), full((1, LANES)), full((1, SSD_WIDTH)), full((1, SSD_WIDTH))],
        out_specs=row_spec(512),
        out_shape=jax.ShapeDtypeStruct((t, SSD_WIDTH), BF16),
        scratch_shapes=[pltpu.VMEM((q + CONV_PAD, SSD_CONV_DIM), F32),
                        pltpu.VMEM((2, SSD_STATE, 256), F32)],
        compiler_params=pltpu.CompilerParams(dimension_semantics=("arbitrary", "arbitrary")),
        name="ssd",
    )(z, xbc, dtm, cw, cb, dtb, alog, dsk, ng)


def _attn_kernel(q_ref, k_ref, v_ref, o_ref, acc_ref):
    i = pl.program_id(2)
    lane = lax.broadcasted_iota(I32, (1, LANES), 1)
    acc_ref[...] = jnp.zeros_like(acc_ref)

    def step(start, width, m_all, masked):
        new_m = []
        for hh in range(ATT_HPS):
            kb = k_ref[0, hh, pl.ds(start, width), :]
            vb = v_ref[0, hh, pl.ds(start, width), :]
            s = lax.dot_general(q_ref[0, hh], kb, NT_DIMS, preferred_element_type=F32)
            if masked:
                r = lax.broadcasted_iota(I32, (TQ, width), 0)
                c = lax.broadcasted_iota(I32, (TQ, width), 1)
                s = jnp.where(c <= r, s, -jnp.inf)
            m_old = m_all[hh]
            m_new = jnp.maximum(m_old, jnp.max(s, axis=-1, keepdims=True))
            alpha = jnp.exp(m_old - m_new)
            p = jnp.exp(s - m_new).astype(BF16)
            acc_ref[hh] = acc_ref[hh] * alpha + jnp.dot(p, vb, preferred_element_type=F32)
            new_m.append(m_new)
        return tuple(new_m)

    m0 = jnp.full((TQ, 1), -jnp.inf, F32)
    per_wide = TK_WIDE // TQ
    n_wide = i // per_wide
    m_all = lax.fori_loop(
        0, n_wide, lambda j, m: step(pl.multiple_of(j * TK_WIDE, TK_WIDE), TK_WIDE, m, False), (m0,) * ATT_HPS)
    m_all = lax.fori_loop(
        n_wide * per_wide, i, lambda j, m: step(pl.multiple_of(j * TQ, TQ), TQ, m, False), m_all)
    step(pl.multiple_of(i * TQ, TQ), TQ, m_all, True)

    outs = []
    for pr in range(ATT_HPS // 2):
        a0 = acc_ref[2 * pr]
        a1 = acc_ref[2 * pr + 1]
        o0 = a0 / a0[:, MLA_V:MLA_V + 1]
        o1 = a1 / a1[:, 0:1]
        outs.append(jnp.where(lane < MLA_V, o0, o1))
    o_ref[0] = jnp.concatenate(outs, axis=1).astype(BF16)


def _attention(q, k, v, bsz, seqlen):
    nq = seqlen // TQ
    hps = ATT_HPS
    kv_spec = pl.BlockSpec((1, hps, seqlen, LANES), lambda b, p, i: (b, p, 0, 0))
    return pl.pallas_call(
        _attn_kernel,
        grid=(bsz, MLA_HEADS // hps, nq),
        in_specs=[pl.BlockSpec((1, hps, TQ, LANES), lambda b, p, i: (b, p, i, 0)), kv_spec, kv_spec],
        out_specs=pl.BlockSpec((1, TQ, hps * MLA_V), lambda b, p, i: (b, i, p)),
        out_shape=jax.ShapeDtypeStruct((bsz, seqlen, MLA_WIDTH), BF16),
        scratch_shapes=[pltpu.VMEM((hps, TQ, LANES), F32)],
        compiler_params=pltpu.CompilerParams(
            dimension_semantics=("arbitrary", "arbitrary", "arbitrary"),
            vmem_limit_bytes=56 * 1024 * 1024),
        name="attention",
    )(q, k, v)


def _outproj_kernel(x_ref, ys_ref, ym_ref, wo_ref, g_ref, wr_ref, br_ref,
                    x1_ref, h2_ref, route_ref, cnt_ref):
    tm = x_ref.shape[0]

    @pl.when(pl.program_id(0) == 0)
    def _():
        cnt_ref[...] = jnp.zeros_like(cnt_ref)

    mix = (jnp.dot(ys_ref[...], wo_ref[0:512, :], preferred_element_type=F32)
           + jnp.dot(ym_ref[...], wo_ref[512:1024, :], preferred_element_type=F32))
    x1 = x_ref[...] + mix
    x1_ref[...] = x1
    h2 = _rms(x1) * g_ref[...]
    h2_ref[...] = _pack_rows(h2)

    lane = lax.broadcasted_iota(I32, (1, LANES), 1)
    h_hi = h2.astype(BF16)
    h_lo = (h2 - h_hi.astype(F32)).astype(BF16)
    hh = jnp.dot(h_hi, wr_ref[...], preferred_element_type=F32)
    lh = jnp.dot(h_lo, wr_ref[:, 0:LANES], preferred_element_type=F32)
    logits = hh[:, 0:LANES] + (hh[:, LANES:2 * LANES] + lh) + br_ref[...]
    logits = jnp.where(lane < N_EXPERTS, logits, -jnp.inf)

    vals, idxs, hots = [], [], []
    for _ in range(TOP_K):
        mx = jnp.max(logits, axis=-1, keepdims=True)
        idx = jnp.min(jnp.where(logits == mx, lane, LANES), axis=-1, keepdims=True)
        hot = lane == idx
        logits = jnp.where(hot, -jnp.inf, logits)
        vals.append(mx)
        idxs.append(idx)
        hots.append(hot)
    exps = [jnp.exp(v - vals[0]) for v in vals]
    denom = exps[0] + exps[1] + exps[2] + exps[3]

    multi = (hots[0] | hots[1] | hots[2] | hots[3])
    multi_f = jnp.where(multi, 1.0, 0.0)
    r = lax.broadcasted_iota(I32, (tm, tm), 0)
    c = lax.broadcasted_iota(I32, (tm, tm), 1)
    strict = jnp.where(r > c, 1.0, 0.0).astype(BF16)
    before = jnp.dot(strict, multi_f.astype(BF16), preferred_element_type=F32) + cnt_ref[0:1, :]
    cnt_ref[...] = cnt_ref[...] + jnp.sum(multi_f, axis=0, keepdims=True)

    out = jnp.zeros((tm, LANES), F32)
    for kk in range(TOP_K):
        rank = jnp.sum(jnp.where(hots[kk], before, 0.0), axis=-1, keepdims=True)
        out = jnp.where(lane == kk, idxs[kk].astype(F32), out)
        out = jnp.where(lane == TOP_K + kk, exps[kk] / denom, out)
        out = jnp.where(lane == 2 * TOP_K + kk, rank, out)
    route_ref[...] = out


def _outproj(x2, y_ssd, y_mla, wo, g_ffn, wr, br):
    t = x2.shape[0]
    tm = TM_PROJ
    full = lambda shape: pl.BlockSpec(shape, lambda i: (0,) * len(shape))
    rows = lambda width: pl.BlockSpec((tm, width), lambda i: (i, 0))
    return pl.pallas_call(
        _outproj_kernel,
        grid=(t // tm,),
        in_specs=[rows(D_MODEL), rows(512), rows(512), full((1024, D_MODEL)), full((1, D_MODEL)),
                  full((D_MODEL, 2 * LANES)), full((1, LANES))],
        out_specs=[rows(D_MODEL), rows(HALF), rows(LANES), full((8, LANES))],
        out_shape=[jax.ShapeDtypeStruct((t, D_MODEL), F32),
                   jax.ShapeDtypeStruct((t, HALF), I32),
                   jax.ShapeDtypeStruct((t, LANES), F32),
                   jax.ShapeDtypeStruct((8, LANES), F32)],
        compiler_params=pltpu.CompilerParams(
            dimension_semantics=("arbitrary",), vmem_limit_bytes=40 * 1024 * 1024),
        name="outproj_router",
    )(x2, y_ssd, y_mla, wo, g_ffn, wr, br)


def _ffn_kernel(be_ref, bv_ref, nx_ref, sl_ref, xs_ref, wgu_hbm, bgu_ref, wd_hbm, bd_ref, ys_ref,
                wgu_st, wd_st, wgu_bf, wd_bf, sem):
    i = pl.program_id(0)
    e = be_ref[i]
    valid = bv_ref[i]
    slot = sl_ref[i]
    first = ((i == 0) | (e != be_ref[jnp.maximum(i - 1, 0)])) & (valid > 0)

    def weight_copies(expert, dst_slot):
        return (pltpu.make_async_copy(wgu_hbm.at[expert], wgu_st.at[dst_slot], sem.at[0, dst_slot]),
                pltpu.make_async_copy(wd_hbm.at[expert], wd_st.at[dst_slot], sem.at[1, dst_slot]))

    @pl.when(i == 0)
    def _():
        for cp in weight_copies(e, slot):
            cp.start()

    @pl.when(first)
    def _():
        for cp in weight_copies(e, slot):
            cp.wait()

        @pl.when(nx_ref[i] >= 0)
        def _():
            for cp in weight_copies(nx_ref[i], 1 - slot):
                cp.start()

        wgu_bf[...] = wgu_st[slot].astype(BF16)
        wd_bf[...] = wd_st[slot].astype(BF16)

    @pl.when(valid > 0)
    def _():
        row = lax.broadcasted_iota(I32, (MOE_BM, 1), 0)
        x_lo, x_hi = _unpack_rows(jnp.where(row < valid, xs_ref[...], 0))
        gu = (jnp.dot(x_lo.astype(BF16), wgu_bf[0:HALF, :], preferred_element_type=F32)
              + jnp.dot(x_hi.astype(BF16), wgu_bf[HALF:D_MODEL, :], preferred_element_type=F32)
              + bgu_ref[0])
        gate = jnp.minimum(gu[:, :D_FF], SWIGLU_LIMIT)
        up = jnp.clip(gu[:, D_FF:], -SWIGLU_LIMIT, SWIGLU_LIMIT)
        glu = gate * jax.nn.sigmoid(SWIGLU_ALPHA * gate)
        mid = ((up + 1.0) * glu).astype(BF16)
        ys_ref[...] = _pack_rows(jnp.dot(mid, wd_bf[...], preferred_element_type=F32) + bd_ref[0])

    @pl.when(valid == 0)
    def _():
        ys_ref[...] = jnp.zeros_like(ys_ref)


def _expert_ffn(block_e, block_valid, block_next, block_slot, xs, wgu, bgu, wd, bd):
    n_slots = xs.shape[0]
    bm = MOE_BM
    grid_spec = pltpu.PrefetchScalarGridSpec(
        num_scalar_prefetch=4,
        grid=(n_slots // bm,),
        in_specs=[
            pl.BlockSpec((bm, HALF), lambda i, be, bv, nx, sl: (i, 0)),
            pl.BlockSpec(memory_space=pl.ANY),
            pl.BlockSpec((1, 1, 2 * D_FF), lambda i, be, bv, nx, sl: (be[i], 0, 0)),
            pl.BlockSpec(memory_space=pl.ANY),
            pl.BlockSpec((1, 1, D_MODEL), lambda i, be, bv, nx, sl: (be[i], 0, 0)),
        ],
        out_specs=pl.BlockSpec((bm, HALF), lambda i, be, bv, nx, sl: (i, 0)),
        scratch_shapes=[pltpu.VMEM((2, D_MODEL, 2 * D_FF), F32), pltpu.VMEM((2, D_FF, D_MODEL), F32),
                        pltpu.VMEM((D_MODEL, 2 * D_FF), BF16), pltpu.VMEM((D_FF, D_MODEL), BF16),
                        pltpu.SemaphoreType.DMA((2, 2))],
    )
    return pl.pallas_call(
        _ffn_kernel,
        grid_spec=grid_spec,
        out_shape=jax.ShapeDtypeStruct((n_slots, HALF), I32),
        compiler_params=pltpu.CompilerParams(
            dimension_semantics=("arbitrary",), vmem_limit_bytes=56 * 1024 * 1024),
        name="expert_ffn",
    )(block_e, block_valid, block_next, block_slot, xs, wgu, bgu, wd, bd)


def _plan_kernel(route_ref, sp_ref, dest_ref):
    lane = lax.broadcasted_iota(I32, (1, LANES), 1)
    lane_f = lane.astype(F32)
    r = route_ref[...]
    out = jnp.zeros(r.shape, F32)
    for kk in range(TOP_K):
        idx = r[:, kk:kk + 1]
        rank = r[:, 2 * TOP_K + kk:2 * TOP_K + kk + 1]
        start = jnp.sum(jnp.where(lane_f == idx, sp_ref[...], 0.0), axis=-1, keepdims=True)
        out = jnp.where(lane == kk, start + rank, out)
    dest_ref[...] = out.T[0:8, :].astype(I32)


def _slot_plan(route, start_pad_row):
    t = route.shape[0]
    tm = TM_PLAN
    return pl.pallas_call(
        _plan_kernel,
        grid=(t // tm,),
        in_specs=[pl.BlockSpec((tm, LANES), lambda i: (i, 0)), pl.BlockSpec((1, LANES), lambda i: (0, 0))],
        out_specs=pl.BlockSpec((8, tm), lambda i: (0, i)),
        out_shape=jax.ShapeDtypeStruct((8, t), I32),
        compiler_params=pltpu.CompilerParams(dimension_semantics=("arbitrary",)),
        name="slot_plan",
    )(route, start_pad_row)


SC_CHUNK = 32


def _sc_workers():
    info = plsc.get_sparse_core_info()
    return info.num_cores, info.num_cores * info.num_subcores


def _sc_scatter_rows(rows, dest_km, n_out):
    t, w = rows.shape
    n_cores, n_workers = _sc_workers()
    per_w = t // n_workers
    mesh = plsc.VectorSubcoreMesh(core_axis_name="c", subcore_axis_name="s")

    @functools.partial(
        pl.kernel, mesh=mesh, out_type=jax.ShapeDtypeStruct((n_out, w), rows.dtype),
        scratch_types=[pltpu.VMEM((TOP_K, SC_CHUNK), I32), pltpu.VMEM((SC_CHUNK, w), rows.dtype),
                       pltpu.SemaphoreType.DMA],
        name="sc_dispatch_scatter")
    def scatter_kernel(rows_hbm, dest_hbm, out_hbm, idx_v, rows_v, sem):
        wid = lax.axis_index("s") * n_cores + lax.axis_index("c")

        @pl.loop(0, per_w // SC_CHUNK)
        def _(c):
            base = wid * per_w + c * SC_CHUNK
            pltpu.sync_copy(rows_hbm.at[pl.ds(base, SC_CHUNK)], rows_v)
            for kk in range(TOP_K):
                pltpu.sync_copy(dest_hbm.at[kk, pl.ds(base, SC_CHUNK)], idx_v.at[kk])
            copies = [pltpu.async_copy(rows_v, out_hbm.at[idx_v.at[kk]], sem) for kk in range(TOP_K)]
            for cp in copies:
                cp.wait()

    return scatter_kernel(rows, dest_km)


def _sc_gather_rows(table, dest_km):
    _, w = table.shape
    t = dest_km.shape[1]
    n_cores, n_workers = _sc_workers()
    per_w = t // n_workers
    mesh = plsc.VectorSubcoreMesh(core_axis_name="c", subcore_axis_name="s")

    @functools.partial(
        pl.kernel, mesh=mesh, out_type=jax.ShapeDtypeStruct((TOP_K, t, w), table.dtype),
        scratch_types=[pltpu.VMEM((TOP_K, SC_CHUNK), I32), pltpu.VMEM((TOP_K, SC_CHUNK, w), table.dtype),
                       pltpu.SemaphoreType.DMA],
        name="sc_combine_gather")
    def gather_kernel(table_hbm, dest_hbm, out_hbm, idx_v, rows_v, sem):
        wid = lax.axis_index("s") * n_cores + lax.axis_index("c")

        @pl.loop(0, per_w // SC_CHUNK)
        def _(c):
            base = wid * per_w + c * SC_CHUNK
            for kk in range(TOP_K):
                pltpu.sync_copy(dest_hbm.at[kk, pl.ds(base, SC_CHUNK)], idx_v.at[kk])
            copies = [pltpu.async_copy(table_hbm.at[idx_v.at[kk]], rows_v.at[kk], sem) for kk in range(TOP_K)]
            for cp in copies:
                cp.wait()
            for kk in range(TOP_K):
                pltpu.sync_copy(rows_v.at[kk], out_hbm.at[kk, pl.ds(base, SC_CHUNK)])

    return gather_kernel(table, dest_km)


def _combine_kernel(x1_ref, yg_ref, route_ref, g_ref, o_ref, *, final_norm):
    moe_lo = jnp.zeros((x1_ref.shape[0], HALF), F32)
    moe_hi = jnp.zeros((x1_ref.shape[0], HALF), F32)
    for kk in range(TOP_K):
        gate = route_ref[:, TOP_K + kk:TOP_K + kk + 1]
        y_lo, y_hi = _unpack_rows(yg_ref[kk])
        moe_lo = moe_lo + gate * y_lo
        moe_hi = moe_hi + gate * y_hi
    acc = x1_ref[...] + jnp.concatenate([moe_lo, moe_hi], axis=1)
    o_ref[...] = _rms(acc) * g_ref[...] if final_norm else acc


def _combine(x1, yg, route, g_final, final_norm):
    t = x1.shape[0]
    tm = TM_COMB
    return pl.pallas_call(
        functools.partial(_combine_kernel, final_norm=final_norm),
        grid=(t // tm,),
        in_specs=[pl.BlockSpec((tm, D_MODEL), lambda i: (i, 0)),
                  pl.BlockSpec((TOP_K, tm, HALF), lambda i: (0, i, 0)),
                  pl.BlockSpec((tm, LANES), lambda i: (i, 0)),
                  pl.BlockSpec((1, D_MODEL), lambda i: (0, 0))],
        out_specs=pl.BlockSpec((tm, D_MODEL), lambda i: (i, 0)),
        out_shape=jax.ShapeDtypeStruct((t, D_MODEL), F32),
        compiler_params=pltpu.CompilerParams(dimension_semantics=("arbitrary",)),
        name="combine",
    )(x1, yg, route, g_final)


def _prep_in_weights(w_in, w_uq, w_ukv):
    w_z = w_in[:, 0:512]
    w_xbc = w_in[:, 512:1536]
    w_dt = w_in[:, 1536:1544]
    w_cq = w_in[:, 1544:1800]
    w_ckv = w_in[:, 1800:1928]
    w_kr = w_in[:, 1928:1960]
    half = MLA_ROPE // 2
    zeros = lambda rows, width: jnp.zeros(rows + (width,), BF16)
    cat = lambda parts: jnp.concatenate([p.astype(BF16) for p in parts], axis=-1)
    d = (D_MODEL,)
    misc1 = [w_dt, zeros(d, MLA_NOPE - SSD_HEADS), w_kr, zeros(d, LANES - MLA_QK)]
    misc2 = [zeros(d, MLA_NOPE), w_kr[:, half:], w_kr[:, :half], zeros(d, LANES - MLA_QK)]
    w1 = cat([w_z, w_xbc, w_cq, w_ckv] + misc1 + misc2)

    wq3 = w_uq.reshape(MLA_Q_RANK, MLA_HEADS, MLA_QK)
    qh = (MLA_Q_RANK, MLA_HEADS)
    main = cat([wq3, zeros(qh, LANES - MLA_QK)])
    swap = cat([zeros(qh, MLA_NOPE), wq3[:, :, MLA_NOPE + half:], wq3[:, :, MLA_NOPE:MLA_NOPE + half],
                zeros(qh, LANES - MLA_QK)])
    wq = jnp.concatenate([main.reshape(MLA_Q_RANK, -1), swap.reshape(MLA_Q_RANK, -1)], axis=1)

    wkv3 = w_ukv.reshape(MLA_KV_RANK, MLA_HEADS, MLA_NOPE + MLA_V)
    kh = (MLA_KV_RANK, MLA_HEADS)
    kpart = cat([wkv3[:, :, :MLA_NOPE], zeros(kh, LANES - MLA_NOPE)])
    vpart = wkv3[:, :, MLA_NOPE:].astype(BF16)
    wkv = jnp.concatenate([kpart.reshape(MLA_KV_RANK, -1), vpart.reshape(MLA_KV_RANK, -1)], axis=1)
    return w1, wq, wkv


def _rope_consts():
    half = MLA_ROPE // 2
    inv_freq = ROPE_THETA ** (-jnp.arange(0, MLA_ROPE, 2, dtype=F32) / MLA_ROPE)
    invf = (jnp.zeros((1, LANES), F32)
            .at[0, MLA_NOPE:MLA_NOPE + half].set(inv_freq)
            .at[0, MLA_NOPE + half:MLA_QK].set(inv_freq))
    sgn = (jnp.zeros((1, LANES), F32)
           .at[0, MLA_NOPE:MLA_NOPE + half].set(-1.0)
           .at[0, MLA_NOPE + half:MLA_QK].set(1.0))
    return invf, sgn


def _pad_lanes(v, fill=0.0):
    return jnp.full((1, LANES), fill, F32).at[0, :v.shape[0]].set(v)


def kernel(x, positions, norm_mix_g, w_in, conv_w, conv_b, dt_bias, a_log, d_skip, ssd_norm_g, q_norm_g, w_uq, kv_norm_g, w_ukv, w_out, norm_ffn_g, w_router, b_router, w_gate_up, b_gate_up, w_down, b_down, norm_final_g):
    bsz, seqlen, d = x.shape
    t = bsz * seqlen
    depth = w_in.shape[0]
    x2 = x.reshape(t, d)
    pos2 = positions.reshape(t, 1).astype(I32)
    invf, sgn = _rope_consts()

    for l in range(depth):
        w1, wq, wkv = _prep_in_weights(w_in[l], w_uq[l], w_ukv[l])
        z, xbc, dtm, q, k, v = _inproj(
            x2, pos2, norm_mix_g[l][None, :], w1, q_norm_g[l][None, :], wq, kv_norm_g[l][None, :], wkv,
            invf, sgn, bsz, seqlen)
        y_ssd = _ssd(z, xbc, dtm, conv_w[l], conv_b[l][None, :], _pad_lanes(dt_bias[l]), _pad_lanes(a_log[l]),
                     jnp.repeat(d_skip[l], SSD_HEAD_DIM)[None, :], ssd_norm_g[l][None, :], bsz, seqlen)
        y_mla = _attention(q, k, v, bsz, seqlen).reshape(t, MLA_WIDTH)

        wr = jnp.zeros((d, LANES), F32).at[:, :N_EXPERTS].set(w_router[l])
        wr_hi = wr.astype(BF16)
        wr_lo = (wr - wr_hi.astype(F32)).astype(BF16)
        x1, h2p, route, cnt = _outproj(x2, y_ssd, y_mla, w_out[l].astype(BF16), norm_ffn_g[l][None, :],
                                       jnp.concatenate([wr_hi, wr_lo], axis=1), _pad_lanes(b_router[l]))

        counts = cnt[0, :N_EXPERTS].astype(I32)
        padded = ((counts + MOE_BM - 1) // MOE_BM) * MOE_BM
        end_pad = jnp.cumsum(padded)
        start_pad = end_pad - padded
        n_slots = t * TOP_K + N_EXPERTS * MOE_BM
        n_blocks = n_slots // MOE_BM
        block_start = jnp.arange(n_blocks, dtype=I32) * MOE_BM
        block_e = jnp.minimum(jnp.sum(block_start[:, None] >= end_pad[None, :], axis=1), N_EXPERTS - 1).astype(I32)
        eids = jnp.arange(N_EXPERTS, dtype=I32)
        block_hot = block_e[:, None] == eids[None, :]
        per_block = lambda table: jnp.sum(jnp.where(block_hot, table[None, :], 0), axis=1).astype(I32)
        block_valid = jnp.clip(per_block(counts) - (block_start - per_block(start_pad)), 0, MOE_BM).astype(I32)
        used = counts > 0
        later_used = jnp.where((eids[None, :] > eids[:, None]) & used[None, :], eids[None, :], N_EXPERTS)
        next_used = jnp.min(later_used, axis=1)
        next_used = jnp.where(next_used < N_EXPERTS, next_used, -1).astype(I32)
        stage_slot = ((jnp.cumsum(used.astype(I32)) - 1) & 1).astype(I32)
        dest_km = _slot_plan(route, _pad_lanes(start_pad.astype(F32)))

        xs = _sc_scatter_rows(h2p, dest_km, n_slots)
        ys = _expert_ffn(block_e, block_valid, per_block(next_used), per_block(stage_slot), xs,
                         w_gate_up[l], b_gate_up[l][:, None, :], w_down[l], b_down[l][:, None, :])
        yg = _sc_gather_rows(ys, dest_km)
        x2 = _combine(x1, yg, route, norm_final_g[None, :], l == depth - 1)
    return x2.reshape(bsz, seqlen, d)
```

```python
import functools

import jax
import jax.numpy as jnp
import numpy as np
from jax import lax
from jax.experimental import pallas as pl
from jax.experimental.pallas import tpu as pltpu
from jax.experimental.pallas import tpu_sc as plsc

F32 = jnp.float32
BF16 = jnp.bfloat16
I32 = jnp.int32
HIGHEST = lax.Precision.HIGHEST

D_MODEL = 1024
EPS = 1e-6
LANES = 128

SSD_HEADS = 8
SSD_HEAD_DIM = 64
SSD_WIDTH = 512
SSD_STATE = 128
SSD_CONV = 4
SSD_CHUNK = 128
SSD_CONV_DIM = 1024
CONV_PAD = 8
SSD_CPS = 2

MLA_HEADS = 8
MLA_Q_RANK = 256
MLA_KV_RANK = 128
MLA_NOPE = 64
MLA_ROPE = 32
MLA_V = 64
MLA_QK = MLA_NOPE + MLA_ROPE
MLA_WIDTH = 512
ROPE_THETA = 10000.0

N_EXPERTS = 32
TOP_K = 4
D_FF = 1024
SWIGLU_LIMIT = 7.0
SWIGLU_ALPHA = 1.702

IN_W = 512 + 1024 + 256 + 128 + 128 + 128

TM_PROJ = 512
TQ = 512
TK_WIDE = 1024
ATT_HPS = 4
MOE_BM = 512
TM_COMB = 512
TM_PLAN = 2048

NT_DIMS = (((1,), (1,)), ((), ()))


def _rms(x):
    return x * lax.rsqrt(jnp.mean(x * x, axis=-1, keepdims=True) + EPS)


HALF = D_MODEL // 2
HI_MASK = np.int32(-65536)


def _pack_rows(a):
    lo = lax.bitcast_convert_type(a[:, :HALF].astype(BF16).astype(F32), I32)
    hi = lax.bitcast_convert_type(a[:, HALF:].astype(BF16).astype(F32), I32)
    return (hi & HI_MASK) | lax.shift_right_logical(lo, 16)


def _unpack_rows(p):
    lo = lax.bitcast_convert_type(lax.shift_left(p, 16), F32)
    hi = lax.bitcast_convert_type(p & HI_MASK, F32)
    return lo, hi


def _inproj_kernel(x_ref, pos_ref, g_ref, w1_ref, qg_ref, wq_ref, kvg_ref, wkv_ref, invf_ref, sgn_ref,
                   z_ref, xbc_ref, dtm_ref, q_ref, k_ref, v_ref):
    x = x_ref[...]
    h = (_rms(x) * g_ref[...]).astype(BF16)
    p = jnp.dot(h, w1_ref[...], preferred_element_type=F32)
    z_ref[...] = p[:, 0:512]
    xbc_ref[...] = p[:, 512:1536]
    cq = p[:, 1536:1792]
    ckv = p[:, 1792:1920]
    m1 = p[:, 1920:2048]
    m2 = p[:, 2048:2176]
    dtm_ref[...] = m1

    lane = lax.broadcasted_iota(I32, (1, LANES), 1)
    ang = pos_ref[...].astype(F32) * invf_ref[...]
    rope_lane = (lane >= MLA_NOPE) & (lane < MLA_QK)
    cos_t = jnp.where(rope_lane, jnp.cos(ang), 0.0)
    sin_t = jnp.sin(ang) * sgn_ref[...]
    cosq_t = jnp.where(lane < MLA_NOPE, 1.0, cos_t)
    scale = MLA_QK ** -0.5

    cqn = (_rms(cq) * qg_ref[...]).astype(BF16)
    qq = jnp.dot(cqn, wq_ref[...], preferred_element_type=F32)
    ckvn = (_rms(ckv) * kvg_ref[...]).astype(BF16)
    kv = jnp.dot(ckvn, wkv_ref[...], preferred_element_type=F32)
    krot = m1 * cos_t + m2 * sin_t
    for h_i in range(MLA_HEADS):
        lo = h_i * LANES
        qm = qq[:, lo:lo + LANES]
        qs = qq[:, 1024 + lo:1024 + lo + LANES]
        q_ref[0, h_i] = ((qm * cosq_t + qs * sin_t) * scale).astype(BF16)
        k_ref[0, h_i] = (kv[:, lo:lo + LANES] + krot).astype(BF16)
    for pr in range(MLA_HEADS // 2):
        vp = kv[:, 1024 + pr * LANES:1024 + (pr + 1) * LANES]
        v_ref[0, 2 * pr] = jnp.where(lane < MLA_V, vp, 1.0).astype(BF16)
        v_ref[0, 2 * pr + 1] = jnp.where(lane >= MLA_V, vp, 1.0).astype(BF16)


def _inproj(x2, pos2, g_mix, w1, qg, wq, kvg, wkv, invf, sgn, bsz, seqlen):
    t = x2.shape[0]
    tm = TM_PROJ
    per_b = seqlen // tm
    full = lambda shape: pl.BlockSpec(shape, lambda i: (0,) * len(shape))
    head_spec = pl.BlockSpec((1, MLA_HEADS, tm, LANES), lambda i: (i // per_b, 0, i % per_b, 0))
    head_shape = jax.ShapeDtypeStruct((bsz, MLA_HEADS, seqlen, LANES), BF16)
    return pl.pallas_call(
        _inproj_kernel,
        grid=(t // tm,),
        in_specs=[
            pl.BlockSpec((tm, D_MODEL), lambda i: (i, 0)),
            pl.BlockSpec((tm, 1), lambda i: (i, 0)),
            full((1, D_MODEL)), full((D_MODEL, IN_W)),
            full((1, MLA_Q_RANK)), full((MLA_Q_RANK, 2048)),
            full((1, MLA_KV_RANK)), full((MLA_KV_RANK, 1536)),
            full((1, LANES)), full((1, LANES)),
        ],
        out_specs=[
            pl.BlockSpec((tm, 512), lambda i: (i, 0)),
            pl.BlockSpec((tm, 1024), lambda i: (i, 0)),
            pl.BlockSpec((tm, LANES), lambda i: (i, 0)),
            head_spec, head_spec, head_spec,
        ],
        out_shape=[
            jax.ShapeDtypeStruct((t, 512), F32),
            jax.ShapeDtypeStruct((t, 1024), F32),
            jax.ShapeDtypeStruct((t, LANES), F32),
            head_shape, head_shape, head_shape,
        ],
        compiler_params=pltpu.CompilerParams(
            dimension_semantics=("arbitrary",), vmem_limit_bytes=56 * 1024 * 1024),
        name="inproj",
    )(x2, pos2, g_mix, w1, qg, wq, kvg, wkv, invf, sgn)


def _ssd_kernel(z_ref, xbc_ref, dtm_ref, cw_ref, cb_ref, dtb_ref, alog_ref, dsk_ref, ng_ref,
                y_ref, ext_ref, st_ref):
    q = SSD_CHUNK
    rows = SSD_CPS * q

    @pl.when(pl.program_id(1) == 0)
    def _():
        ext_ref[0:CONV_PAD, :] = jnp.zeros((CONV_PAD, SSD_CONV_DIM), F32)
        st_ref[...] = jnp.zeros_like(st_ref)

    ext_ref[CONV_PAD:CONV_PAD + rows, :] = xbc_ref[...]

    lane = lax.broadcasted_iota(I32, (1, LANES), 1)
    row = lax.broadcasted_iota(I32, (q, q), 0)
    col = lax.broadcasted_iota(I32, (q, q), 1)
    tril = row >= col
    tril_b = jnp.where(tril, 1.0, 0.0).astype(BF16)
    spread = jnp.where(
        lax.broadcasted_iota(I32, (LANES, SSD_WIDTH), 0)
        == lax.broadcasted_iota(I32, (LANES, SSD_WIDTH), 1) // SSD_HEAD_DIM, 1.0, 0.0).astype(BF16)
    a_neg = -jnp.exp(alog_ref[...])

    def split3(v):
        hi = v.astype(BF16)
        r1 = v - hi.astype(F32)
        mid = r1.astype(BF16)
        return hi, mid, (r1 - mid.astype(F32)).astype(BF16)

    def dot3_right(parts, m):
        return sum(jnp.dot(p, m, preferred_element_type=F32) for p in parts)

    def expand(cols):
        return dot3_right(split3(cols), spread)

    for ci in range(SSD_CPS):
        lo = ci * q
        conv = cb_ref[...]
        for kk in range(SSD_CONV):
            off = CONV_PAD + lo - (SSD_CONV - 1) + kk
            conv = conv + cw_ref[kk:kk + 1, :] * ext_ref[off:off + q, :]
        u = conv * jax.nn.sigmoid(conv)
        xs = u[:, 0:512]
        bm = u[:, 512:768]
        cm = u[:, 768:1024]

        xdt = dtm_ref[lo:lo + q, :] + dtb_ref[...]
        dt = jnp.maximum(xdt, 0.0) + jnp.log1p(jnp.exp(-jnp.abs(xdt)))
        adt = jnp.where(lane < SSD_HEADS, dt * a_neg, 0.0)
        cum_col = sum(jnp.dot(tril_b, p, preferred_element_type=F32) for p in split3(adt))
        cum_row = cum_col.T

        dt_e = expand(dt)
        ac_e = expand(cum_col)
        last_e = ac_e[q - 1:q, :]
        xd = xs * dt_e
        w_end = xd * jnp.exp(last_e - ac_e)
        eac = jnp.exp(ac_e)
        cdec = jnp.exp(last_e)

        y_parts = []
        for g in range(2):
            gl = g * 256
            bg = bm[:, g * SSD_STATE:(g + 1) * SSD_STATE]
            cg = cm[:, g * SSD_STATE:(g + 1) * SSD_STATE].astype(BF16)
            scores = lax.dot_general(cg, bg.astype(BF16), NT_DIMS, preferred_element_type=F32)
            bgt = bg.T.astype(BF16)
            sprev = st_ref[g]
            yoff = jnp.dot(cg, sprev.astype(BF16), preferred_element_type=F32)
            st_ref[g] = sprev * cdec[:, gl:gl + 256] + jnp.dot(
                bgt, w_end[:, gl:gl + 256].astype(BF16), preferred_element_type=F32)
            for pr in range(2):
                pl_lo = gl + pr * LANES
                xdp = xd[:, pl_lo:pl_lo + LANES].astype(BF16)
                res = []
                for jj in range(2):
                    h_i = g * 4 + pr * 2 + jj
                    seg = cum_col[:, h_i:h_i + 1] - cum_row[h_i:h_i + 1, :]
                    dec = jnp.exp(jnp.where(tril, seg, -jnp.inf))
                    res.append(jnp.dot((scores * dec).astype(BF16), xdp, preferred_element_type=F32))
                ydiag = jnp.where(lane < SSD_HEAD_DIM, res[0], res[1])
                y_parts.append(ydiag + yoff[:, pr * LANES:(pr + 1) * LANES] * eac[:, pl_lo:pl_lo + LANES])
        y = jnp.concatenate(y_parts, axis=1) + dsk_ref[...] * xs
        zz = z_ref[lo:lo + q, :]
        y = y * (zz * jax.nn.sigmoid(zz))
        outs = []
        for g in range(2):
            yg = y[:, g * 256:(g + 1) * 256]
            outs.append(_rms(yg))
        y_ref[lo:lo + q, :] = (jnp.concatenate(outs, axis=1) * ng_ref[...]).astype(BF16)

    ext_ref[0:CONV_PAD, :] = ext_ref[rows:rows + CONV_PAD, :]


def _ssd(z, xbc, dtm, cw, cb, dtb, alog, dsk, ng, bsz, seqlen):
    t = z.shape[0]
    q = SSD_CHUNK
    rows = SSD_CPS * q
    nc = seqlen // rows
    full = lambda shape: pl.BlockSpec(shape, lambda b, c: (0,) * len(shape))
    row_spec = lambda width: pl.BlockSpec((rows, width), lambda b, c: (b * nc + c, 0))
    return pl.pallas_call(
        _ssd_kernel,
        grid=(bsz, nc),
        in_specs=[row_spec(512), row_spec(1024), row_spec(LANES),
                  full((SSD_CONV, SSD_CONV_DIM)), full((1, SSD_CONV_DIM)),
                  full((1, LANES)), full((1, LANES)), full((1, SSD_WIDTH)), full((1, SSD_WIDTH))],
        out_specs=row_spec(512),
        out_shape=jax.ShapeDtypeStruct((t, SSD_WIDTH), BF16),
        scratch_shapes=[pltpu.VMEM((rows + CONV_PAD, SSD_CONV_DIM), F32),
                        pltpu.VMEM((2, SSD_STATE, 256), F32)],
        compiler_params=pltpu.CompilerParams(dimension_semantics=("arbitrary", "arbitrary")),
        name="ssd",
    )(z, xbc, dtm, cw, cb, dtb, alog, dsk, ng)


def _attn_kernel(q_ref, k_ref, v_ref, o_ref, acc_ref):
    i = pl.program_id(2)
    lane = lax.broadcasted_iota(I32, (1, LANES), 1)
    acc_ref[...] = jnp.zeros_like(acc_ref)

    def step(start, width, m_all, masked):
        new_m = []
        for hh in range(ATT_HPS):
            kb = k_ref[0, hh, pl.ds(start, width), :]
            vb = v_ref[0, hh, pl.ds(start, width), :]
            s = lax.dot_general(q_ref[0, hh], kb, NT_DIMS, preferred_element_type=F32)
            if masked:
                r = lax.broadcasted_iota(I32, (TQ, width), 0)
                c = lax.broadcasted_iota(I32, (TQ, width), 1)
                s = jnp.where(c <= r, s, -jnp.inf)
            m_old = m_all[hh]
            m_new = jnp.maximum(m_old, jnp.max(s, axis=-1, keepdims=True))
            alpha = jnp.exp(m_old - m_new)
            p = jnp.exp(s - m_new).astype(BF16)
            acc_ref[hh] = acc_ref[hh] * alpha + jnp.dot(p, vb, preferred_element_type=F32)
            new_m.append(m_new)
        return tuple(new_m)

    m0 = jnp.full((TQ, 1), -jnp.inf, F32)
    per_wide = TK_WIDE // TQ
    n_wide = i // per_wide
    m_all = lax.fori_loop(
        0, n_wide, lambda j, m: step(pl.multiple_of(j * TK_WIDE, TK_WIDE), TK_WIDE, m, False), (m0,) * ATT_HPS)
    m_all = lax.fori_loop(
        n_wide * per_wide, i, lambda j, m: step(pl.multiple_of(j * TQ, TQ), TQ, m, False), m_all)
    step(pl.multiple_of(i * TQ, TQ), TQ, m_all, True)

    outs = []
    for pr in range(ATT_HPS // 2):
        a0 = acc_ref[2 * pr]
        a1 = acc_ref[2 * pr + 1]
        o0 = a0 / a0[:, MLA_V:MLA_V + 1]
        o1 = a1 / a1[:, 0:1]
        outs.append(jnp.where(lane < MLA_V, o0, o1))
    o_ref[0] = jnp.concatenate(outs, axis=1).astype(BF16)


def _attention(q, k, v, bsz, seqlen):
    nq = seqlen // TQ
    hps = ATT_HPS
    kv_spec = pl.BlockSpec((1, hps, seqlen, LANES), lambda b, p, i: (b, p, 0, 0))
    return pl.pallas_call(
        _attn_kernel,
        grid=(bsz, MLA_HEADS // hps, nq),
        in_specs=[pl.BlockSpec((1, hps, TQ, LANES), lambda b, p, i: (b, p, i, 0)), kv_spec, kv_spec],
        out_specs=pl.BlockSpec((1, TQ, hps * MLA_V), lambda b, p, i: (b, i, p)),
        out_shape=jax.ShapeDtypeStruct((bsz, seqlen, MLA_WIDTH), BF16),
        scratch_shapes=[pltpu.VMEM((hps, TQ, LANES), F32)],
        compiler_params=pltpu.CompilerParams(
            dimension_semantics=("arbitrary", "arbitrary", "arbitrary"),
            vmem_limit_bytes=56 * 1024 * 1024),
        name="attention",
    )(q, k, v)


def _outproj_kernel(x_ref, ys_ref, ym_ref, wo_ref, g_ref, wr_ref, br_ref,
                    x1_ref, h2_ref, route_ref, cnt_ref):
    tm = x_ref.shape[0]

    @pl.when(pl.program_id(0) == 0)
    def _():
        cnt_ref[...] = jnp.zeros_like(cnt_ref)

    mix = (jnp.dot(ys_ref[...], wo_ref[0:512, :], preferred_element_type=F32)
           + jnp.dot(ym_ref[...], wo_ref[512:1024, :], preferred_element_type=F32))
    x1 = x_ref[...] + mix
    x1_ref[...] = x1
    h2 = _rms(x1) * g_ref[...]
    h2_ref[...] = _pack_rows(h2)

    lane = lax.broadcasted_iota(I32, (1, LANES), 1)
    lane_f = lane.astype(F32)
    h_hi = h2.astype(BF16)
    h_lo = (h2 - h_hi.astype(F32)).astype(BF16)
    hh = jnp.dot(h_hi, wr_ref[...], preferred_element_type=F32)
    lh = jnp.dot(h_lo, wr_ref[:, 0:LANES], preferred_element_type=F32)
    logits = hh[:, 0:LANES] + (hh[:, LANES:2 * LANES] + lh) + br_ref[...]
    logits = jnp.where(lane < N_EXPERTS, logits, -jnp.inf)

    vals, idxs, hots = [], [], []
    for _ in range(TOP_K):
        mx = jnp.max(logits, axis=-1, keepdims=True)
        idx = jnp.min(jnp.where(logits == mx, lane_f, float(LANES)), axis=-1, keepdims=True)
        hot = lane_f == idx
        logits = jnp.where(hot, -jnp.inf, logits)
        vals.append(mx)
        idxs.append(idx)
        hots.append(hot)
    exps = [jnp.exp(v - vals[0]) for v in vals]
    denom = exps[0] + exps[1] + exps[2] + exps[3]

    multi = (hots[0] | hots[1] | hots[2] | hots[3])
    multi_f = jnp.where(multi, 1.0, 0.0)
    r = lax.broadcasted_iota(I32, (tm, tm), 0)
    c = lax.broadcasted_iota(I32, (tm, tm), 1)
    strict = jnp.where(r > c, 1.0, 0.0).astype(BF16)
    before = jnp.dot(strict, multi_f.astype(BF16), preferred_element_type=F32) + cnt_ref[0:1, :]
    cnt_ref[...] = cnt_ref[...] + jnp.sum(multi_f, axis=0, keepdims=True)

    out = jnp.zeros((tm, LANES), F32)
    for kk in range(TOP_K):
        rank = jnp.sum(jnp.where(hots[kk], before, 0.0), axis=-1, keepdims=True)
        out = jnp.where(lane == kk, idxs[kk], out)
        out = jnp.where(lane == TOP_K + kk, exps[kk] / denom, out)
        out = jnp.where(lane == 2 * TOP_K + kk, rank, out)
    route_ref[...] = out


def _outproj(x2, y_ssd, y_mla, wo, g_ffn, wr, br):
    t = x2.shape[0]
    tm = TM_PROJ
    full = lambda shape: pl.BlockSpec(shape, lambda i: (0,) * len(shape))
    rows = lambda width: pl.BlockSpec((tm, width), lambda i: (i, 0))
    return pl.pallas_call(
        _outproj_kernel,
        grid=(t // tm,),
        in_specs=[rows(D_MODEL), rows(512), rows(512), full((1024, D_MODEL)), full((1, D_MODEL)),
                  full((D_MODEL, 2 * LANES)), full((1, LANES))],
        out_specs=[rows(D_MODEL), rows(HALF), rows(LANES), full((8, LANES))],
        out_shape=[jax.ShapeDtypeStruct((t, D_MODEL), F32),
                   jax.ShapeDtypeStruct((t, HALF), I32),
                   jax.ShapeDtypeStruct((t, LANES), F32),
                   jax.ShapeDtypeStruct((8, LANES), F32)],
        compiler_params=pltpu.CompilerParams(
            dimension_semantics=("arbitrary",), vmem_limit_bytes=40 * 1024 * 1024),
        name="outproj_router",
    )(x2, y_ssd, y_mla, wo, g_ffn, wr, br)


def _ffn_kernel(be_ref, bv_ref, nx_ref, sl_ref, xs_ref, wgu_hbm, bgu_ref, wd_hbm, bd_ref, ys_ref,
                wgu_st, wd_st, wgu_bf, wd_bf, sem):
    i = pl.program_id(0)
    e = be_ref[i]
    valid = bv_ref[i]
    slot = sl_ref[i]
    first = ((i == 0) | (e != be_ref[jnp.maximum(i - 1, 0)])) & (valid > 0)

    def weight_copies(expert, dst_slot):
        return (pltpu.make_async_copy(wgu_hbm.at[expert], wgu_st.at[dst_slot], sem.at[0, dst_slot]),
                pltpu.make_async_copy(wd_hbm.at[expert], wd_st.at[dst_slot], sem.at[1, dst_slot]))

    @pl.when(i == 0)
    def _():
        for cp in weight_copies(e, slot):
            cp.start()

    @pl.when(first)
    def _():
        for cp in weight_copies(e, slot):
            cp.wait()

        @pl.when(nx_ref[i] >= 0)
        def _():
            for cp in weight_copies(nx_ref[i], 1 - slot):
                cp.start()

        wgu_bf[...] = wgu_st[slot].astype(BF16)
        wd_bf[...] = wd_st[slot].astype(BF16)

    @pl.when(valid > 0)
    def _():
        row = lax.broadcasted_iota(I32, (MOE_BM, 1), 0)
        x_lo, x_hi = _unpack_rows(jnp.where(row < valid, xs_ref[...], 0))
        gu = (jnp.dot(x_lo.astype(BF16), wgu_bf[0:HALF, :], preferred_element_type=F32)
              + jnp.dot(x_hi.astype(BF16), wgu_bf[HALF:D_MODEL, :], preferred_element_type=F32)
              + bgu_ref[0])
        gate = jnp.minimum(gu[:, :D_FF], SWIGLU_LIMIT)
        up = jnp.clip(gu[:, D_FF:], -SWIGLU_LIMIT, SWIGLU_LIMIT)
        glu = gate * jax.nn.sigmoid(SWIGLU_ALPHA * gate)
        mid = ((up + 1.0) * glu).astype(BF16)
        ys_ref[...] = _pack_rows(jnp.dot(mid, wd_bf[...], preferred_element_type=F32) + bd_ref[0])

    @pl.when(valid == 0)
    def _():
        ys_ref[...] = jnp.zeros_like(ys_ref)


def _expert_ffn(block_e, block_valid, block_next, block_slot, xs, wgu, bgu, wd, bd):
    n_slots = xs.shape[0]
    bm = MOE_BM
    grid_spec = pltpu.PrefetchScalarGridSpec(
        num_scalar_prefetch=4,
        grid=(n_slots // bm,),
        in_specs=[
            pl.BlockSpec((bm, HALF), lambda i, be, bv, nx, sl: (i, 0)),
            pl.BlockSpec(memory_space=pl.ANY),
            pl.BlockSpec((1, 1, 2 * D_FF), lambda i, be, bv, nx, sl: (be[i], 0, 0)),
            pl.BlockSpec(memory_space=pl.ANY),
            pl.BlockSpec((1, 1, D_MODEL), lambda i, be, bv, nx, sl: (be[i], 0, 0)),
        ],
        out_specs=pl.BlockSpec((bm, HALF), lambda i, be, bv, nx, sl: (i, 0)),
        scratch_shapes=[pltpu.VMEM((2, D_MODEL, 2 * D_FF), F32), pltpu.VMEM((2, D_FF, D_MODEL), F32),
                        pltpu.VMEM((D_MODEL, 2 * D_FF), BF16), pltpu.VMEM((D_FF, D_MODEL), BF16),
                        pltpu.SemaphoreType.DMA((2, 2))],
    )
    return pl.pallas_call(
        _ffn_kernel,
        grid_spec=grid_spec,
        out_shape=jax.ShapeDtypeStruct((n_slots, HALF), I32),
        compiler_params=pltpu.CompilerParams(
            dimension_semantics=("arbitrary",), vmem_limit_bytes=56 * 1024 * 1024),
        name="expert_ffn",
    )(block_e, block_valid, block_next, block_slot, xs, wgu, bgu, wd, bd)


def _plan_kernel(sp_ref, route_ref, dest_ref):
    rt = route_ref[...].T
    idx = rt[0:TOP_K, :]
    rank = rt[2 * TOP_K:3 * TOP_K, :]
    start = jnp.zeros(idx.shape, F32)
    for e_i in range(N_EXPERTS):
        start = jnp.where(idx == float(e_i), sp_ref[e_i].astype(F32), start)
    dest_ref[...] = (start + rank).astype(I32)


def _slot_plan(route, start_pad):
    t = route.shape[0]
    tm = TM_PLAN
    grid_spec = pltpu.PrefetchScalarGridSpec(
        num_scalar_prefetch=1,
        grid=(t // tm,),
        in_specs=[pl.BlockSpec((tm, LANES), lambda i, sp: (i, 0))],
        out_specs=pl.BlockSpec((TOP_K, tm), lambda i, sp: (0, i)),
    )
    return pl.pallas_call(
        _plan_kernel,
        grid_spec=grid_spec,
        out_shape=jax.ShapeDtypeStruct((TOP_K, t), I32),
        compiler_params=pltpu.CompilerParams(dimension_semantics=("arbitrary",)),
        name="slot_plan",
    )(start_pad, route)


SC_CHUNK = 32


def _sc_workers():
    info = plsc.get_sparse_core_info()
    return info.num_cores, info.num_cores * info.num_subcores


def _sc_scatter_rows(rows, dest_km, n_out):
    t, w = rows.shape
    n_cores, n_workers = _sc_workers()
    per_w = t // n_workers
    mesh = plsc.VectorSubcoreMesh(core_axis_name="c", subcore_axis_name="s")

    @functools.partial(
        pl.kernel, mesh=mesh, out_type=jax.ShapeDtypeStruct((n_out, w), rows.dtype),
        scratch_types=[pltpu.VMEM((TOP_K, SC_CHUNK), I32), pltpu.VMEM((SC_CHUNK, w), rows.dtype),
                       pltpu.SemaphoreType.DMA],
        name="sc_dispatch_scatter")
    def scatter_kernel(rows_hbm, dest_hbm, out_hbm, idx_v, rows_v, sem):
        wid = lax.axis_index("s") * n_cores + lax.axis_index("c")

        @pl.loop(0, per_w // SC_CHUNK)
        def _(c):
            base = wid * per_w + c * SC_CHUNK
            pltpu.sync_copy(rows_hbm.at[pl.ds(base, SC_CHUNK)], rows_v)
            for kk in range(TOP_K):
                pltpu.sync_copy(dest_hbm.at[kk, pl.ds(base, SC_CHUNK)], idx_v.at[kk])
            copies = [pltpu.async_copy(rows_v, out_hbm.at[idx_v.at[kk]], sem) for kk in range(TOP_K)]
            for cp in copies:
                cp.wait()

    return scatter_kernel(rows, dest_km)


def _sc_gather_rows(table, dest_km):
    _, w = table.shape
    t = dest_km.shape[1]
    n_cores, n_workers = _sc_workers()
    per_w = t // n_workers
    mesh = plsc.VectorSubcoreMesh(core_axis_name="c", subcore_axis_name="s")

    @functools.partial(
        pl.kernel, mesh=mesh, out_type=jax.ShapeDtypeStruct((TOP_K, t, w), table.dtype),
        scratch_types=[pltpu.VMEM((TOP_K, SC_CHUNK), I32), pltpu.VMEM((TOP_K, SC_CHUNK, w), table.dtype),
                       pltpu.SemaphoreType.DMA],
        name="sc_combine_gather")
    def gather_kernel(table_hbm, dest_hbm, out_hbm, idx_v, rows_v, sem):
        wid = lax.axis_index("s") * n_cores + lax.axis_index("c")

        @pl.loop(0, per_w // SC_CHUNK)
        def _(c):
            base = wid * per_w + c * SC_CHUNK
            for kk in range(TOP_K):
                pltpu.sync_copy(dest_hbm.at[kk, pl.ds(base, SC_CHUNK)], idx_v.at[kk])
            copies = [pltpu.async_copy(table_hbm.at[idx_v.at[kk]], rows_v.at[kk], sem) for kk in range(TOP_K)]
            for cp in copies:
                cp.wait()
            for kk in range(TOP_K):
                pltpu.sync_copy(rows_v.at[kk], out_hbm.at[kk, pl.ds(base, SC_CHUNK)])

    return gather_kernel(table, dest_km)


def _combine_kernel(x1_ref, yg_ref, route_ref, g_ref, o_ref, *, final_norm):
    moe_lo = jnp.zeros((x1_ref.shape[0], HALF), F32)
    moe_hi = jnp.zeros((x1_ref.shape[0], HALF), F32)
    for kk in range(TOP_K):
        gate = route_ref[:, TOP_K + kk:TOP_K + kk + 1]
        y_lo, y_hi = _unpack_rows(yg_ref[kk])
        moe_lo = moe_lo + gate * y_lo
        moe_hi = moe_hi + gate * y_hi
    acc = x1_ref[...] + jnp.concatenate([moe_lo, moe_hi], axis=1)
    o_ref[...] = _rms(acc) * g_ref[...] if final_norm else acc


def _combine(x1, yg, route, g_final, final_norm):
    t = x1.shape[0]
    tm = TM_COMB
    return pl.pallas_call(
        functools.partial(_combine_kernel, final_norm=final_norm),
        grid=(t // tm,),
        in_specs=[pl.BlockSpec((tm, D_MODEL), lambda i: (i, 0)),
                  pl.BlockSpec((TOP_K, tm, HALF), lambda i: (0, i, 0)),
                  pl.BlockSpec((tm, LANES), lambda i: (i, 0)),
                  pl.BlockSpec((1, D_MODEL), lambda i: (0, 0))],
        out_specs=pl.BlockSpec((tm, D_MODEL), lambda i: (i, 0)),
        out_shape=jax.ShapeDtypeStruct((t, D_MODEL), F32),
        compiler_params=pltpu.CompilerParams(dimension_semantics=("arbitrary",)),
        name="combine",
    )(x1, yg, route, g_final)


def _prep_in_weights(w_in, w_uq, w_ukv):
    w_z = w_in[:, 0:512]
    w_xbc = w_in[:, 512:1536]
    w_dt = w_in[:, 1536:1544]
    w_cq = w_in[:, 1544:1800]
    w_ckv = w_in[:, 1800:1928]
    w_kr = w_in[:, 1928:1960]
    half = MLA_ROPE // 2
    zeros = lambda rows, width: jnp.zeros(rows + (width,), BF16)
    cat = lambda parts: jnp.concatenate([p.astype(BF16) for p in parts], axis=-1)
    d = (D_MODEL,)
    misc1 = [w_dt, zeros(d, MLA_NOPE - SSD_HEADS), w_kr, zeros(d, LANES - MLA_QK)]
    misc2 = [zeros(d, MLA_NOPE), w_kr[:, half:], w_kr[:, :half], zeros(d, LANES - MLA_QK)]
    w1 = cat([w_z, w_xbc, w_cq, w_ckv] + misc1 + misc2)

    wq3 = w_uq.reshape(MLA_Q_RANK, MLA_HEADS, MLA_QK)
    qh = (MLA_Q_RANK, MLA_HEADS)
    main = cat([wq3, zeros(qh, LANES - MLA_QK)])
    swap = cat([zeros(qh, MLA_NOPE), wq3[:, :, MLA_NOPE + half:], wq3[:, :, MLA_NOPE:MLA_NOPE + half],
                zeros(qh, LANES - MLA_QK)])
    wq = jnp.concatenate([main.reshape(MLA_Q_RANK, -1), swap.reshape(MLA_Q_RANK, -1)], axis=1)

    wkv3 = w_ukv.reshape(MLA_KV_RANK, MLA_HEADS, MLA_NOPE + MLA_V)
    kh = (MLA_KV_RANK, MLA_HEADS)
    kpart = cat([wkv3[:, :, :MLA_NOPE], zeros(kh, LANES - MLA_NOPE)])
    vpart = wkv3[:, :, MLA_NOPE:].astype(BF16)
    wkv = jnp.concatenate([kpart.reshape(MLA_KV_RANK, -1), vpart.reshape(MLA_KV_RANK, -1)], axis=1)
    return w1, wq, wkv


def _rope_consts():
    half = MLA_ROPE // 2
    inv_freq = ROPE_THETA ** (-jnp.arange(0, MLA_ROPE, 2, dtype=F32) / MLA_ROPE)
    invf = (jnp.zeros((1, LANES), F32)
            .at[0, MLA_NOPE:MLA_NOPE + half].set(inv_freq)
            .at[0, MLA_NOPE + half:MLA_QK].set(inv_freq))
    sgn = (jnp.zeros((1, LANES), F32)
           .at[0, MLA_NOPE:MLA_NOPE + half].set(-1.0)
           .at[0, MLA_NOPE + half:MLA_QK].set(1.0))
    return invf, sgn


def _pad_lanes(v, fill=0.0):
    return jnp.full((1, LANES), fill, F32).at[0, :v.shape[0]].set(v)


def kernel(x, positions, norm_mix_g, w_in, conv_w, conv_b, dt_bias, a_log, d_skip, ssd_norm_g, q_norm_g, w_uq, kv_norm_g, w_ukv, w_out, norm_ffn_g, w_router, b_router, w_gate_up, b_gate_up, w_down, b_down, norm_final_g):
    bsz, seqlen, d = x.shape
    t = bsz * seqlen
    depth = w_in.shape[0]
    x2 = x.reshape(t, d)
    pos2 = positions.reshape(t, 1).astype(I32)
    invf, sgn = _rope_consts()

    for l in range(depth):
        w1, wq, wkv = _prep_in_weights(w_in[l], w_uq[l], w_ukv[l])
        z, xbc, dtm, q, k, v = _inproj(
            x2, pos2, norm_mix_g[l][None, :], w1, q_norm_g[l][None, :], wq, kv_norm_g[l][None, :], wkv,
            invf, sgn, bsz, seqlen)
        y_ssd = _ssd(z, xbc, dtm, conv_w[l], conv_b[l][None, :], _pad_lanes(dt_bias[l]), _pad_lanes(a_log[l]),
                     jnp.repeat(d_skip[l], SSD_HEAD_DIM)[None, :], ssd_norm_g[l][None, :], bsz, seqlen)
        y_mla = _attention(q, k, v, bsz, seqlen).reshape(t, MLA_WIDTH)

        wr = jnp.zeros((d, LANES), F32).at[:, :N_EXPERTS].set(w_router[l])
        wr_hi = wr.astype(BF16)
        wr_lo = (wr - wr_hi.astype(F32)).astype(BF16)
        x1, h2p, route, cnt = _outproj(x2, y_ssd, y_mla, w_out[l].astype(BF16), norm_ffn_g[l][None, :],
                                       jnp.concatenate([wr_hi, wr_lo], axis=1), _pad_lanes(b_router[l]))

        counts = cnt[0, :N_EXPERTS].astype(I32)
        padded = ((counts + MOE_BM - 1) // MOE_BM) * MOE_BM
        end_pad = jnp.cumsum(padded)
        start_pad = end_pad - padded
        n_slots = t * TOP_K + N_EXPERTS * MOE_BM
        n_blocks = n_slots // MOE_BM
        block_start = jnp.arange(n_blocks, dtype=I32) * MOE_BM
        block_e = jnp.minimum(jnp.sum(block_start[:, None] >= end_pad[None, :], axis=1), N_EXPERTS - 1).astype(I32)
        eids = jnp.arange(N_EXPERTS, dtype=I32)
        block_hot = block_e[:, None] == eids[None, :]
        per_block = lambda table: jnp.sum(jnp.where(block_hot, table[None, :], 0), axis=1).astype(I32)
        block_valid = jnp.clip(per_block(counts) - (block_start - per_block(start_pad)), 0, MOE_BM).astype(I32)
        used = counts > 0
        later_used = jnp.where((eids[None, :] > eids[:, None]) & used[None, :], eids[None, :], N_EXPERTS)
        next_used = jnp.min(later_used, axis=1)
        next_used = jnp.where(next_used < N_EXPERTS, next_used, -1).astype(I32)
        stage_slot = ((jnp.cumsum(used.astype(I32)) - 1) & 1).astype(I32)
        dest_km = _slot_plan(route, start_pad.astype(I32))

        xs = _sc_scatter_rows(h2p, dest_km, n_slots)
        ys = _expert_ffn(block_e, block_valid, per_block(next_used), per_block(stage_slot), xs,
                         w_gate_up[l], b_gate_up[l][:, None, :], w_down[l], b_down[l][:, None, :])
        yg = _sc_gather_rows(ys, dest_km)
        x2 = _combine(x1, yg, route, norm_final_g[None, :], l == depth - 1)
    return x2.reshape(bsz, seqlen, d)
```

```python
import functools

import jax
import jax.numpy as jnp
import numpy as np
from jax import lax
from jax.experimental import pallas as pl
from jax.experimental.pallas import tpu as pltpu
from jax.experimental.pallas import tpu_sc as plsc

F32 = jnp.float32
BF16 = jnp.bfloat16
I32 = jnp.int32
HIGHEST = lax.Precision.HIGHEST

D_MODEL = 1024
EPS = 1e-6
LANES = 128

SSD_HEADS = 8
SSD_HEAD_DIM = 64
SSD_WIDTH = 512
SSD_STATE = 128
SSD_CONV = 4
SSD_CHUNK = 128
SSD_CONV_DIM = 1024
CONV_PAD = 8
SSD_CPS = 2

MLA_HEADS = 8
MLA_Q_RANK = 256
MLA_KV_RANK = 128
MLA_NOPE = 64
MLA_ROPE = 32
MLA_V = 64
MLA_QK = MLA_NOPE + MLA_ROPE
MLA_WIDTH = 512
ROPE_THETA = 10000.0

N_EXPERTS = 32
TOP_K = 4
D_FF = 1024
SWIGLU_LIMIT = 7.0
SWIGLU_ALPHA = 1.702

IN_W = 512 + 1024 + 256 + 128 + 128 + 128

TM_PROJ = 512
TQ = 512
TK_WIDE = 1024
ATT_HPS = 4
MOE_BM = 256
TM_COMB = 512
TM_PLAN = 2048

NT_DIMS = (((1,), (1,)), ((), ()))


def _rms(x):
    return x * lax.rsqrt(jnp.mean(x * x, axis=-1, keepdims=True) + EPS)


HALF = D_MODEL // 2
HI_MASK = np.int32(-65536)


def _pack_rows(a):
    lo = lax.bitcast_convert_type(a[:, :HALF].astype(BF16).astype(F32), I32)
    hi = lax.bitcast_convert_type(a[:, HALF:].astype(BF16).astype(F32), I32)
    return (hi & HI_MASK) | lax.shift_right_logical(lo, 16)


def _unpack_rows(p):
    lo = lax.bitcast_convert_type(lax.shift_left(p, 16), F32)
    hi = lax.bitcast_convert_type(p & HI_MASK, F32)
    return lo, hi


def _inproj_kernel(x_ref, pos_ref, g_ref, w1_ref, qg_ref, wq_ref, kvg_ref, wkv_ref, invf_ref, sgn_ref,
                   z_ref, xbc_ref, dtm_ref, q_ref, k_ref, v_ref):
    x = x_ref[...]
    h = (_rms(x) * g_ref[...]).astype(BF16)
    p = jnp.dot(h, w1_ref[...], preferred_element_type=F32)
    z_ref[...] = p[:, 0:512]
    xbc_ref[...] = p[:, 512:1536]
    cq = p[:, 1536:1792]
    ckv = p[:, 1792:1920]
    m1 = p[:, 1920:2048]
    m2 = p[:, 2048:2176]
    dtm_ref[...] = m1

    lane = lax.broadcasted_iota(I32, (1, LANES), 1)
    ang = pos_ref[...].astype(F32) * invf_ref[...]
    rope_lane = (lane >= MLA_NOPE) & (lane < MLA_QK)
    cos_t = jnp.where(rope_lane, jnp.cos(ang), 0.0)
    sin_t = jnp.sin(ang) * sgn_ref[...]
    cosq_t = jnp.where(lane < MLA_NOPE, 1.0, cos_t)
    scale = MLA_QK ** -0.5

    cqn = (_rms(cq) * qg_ref[...]).astype(BF16)
    qq = jnp.dot(cqn, wq_ref[...], preferred_element_type=F32)
    ckvn = (_rms(ckv) * kvg_ref[...]).astype(BF16)
    kv = jnp.dot(ckvn, wkv_ref[...], preferred_element_type=F32)
    krot = m1 * cos_t + m2 * sin_t
    for h_i in range(MLA_HEADS):
        lo = h_i * LANES
        qm = qq[:, lo:lo + LANES]
        qs = qq[:, 1024 + lo:1024 + lo + LANES]
        q_ref[0, h_i] = ((qm * cosq_t + qs * sin_t) * scale).astype(BF16)
        k_ref[0, h_i] = (kv[:, lo:lo + LANES] + krot).astype(BF16)
    for pr in range(MLA_HEADS // 2):
        vp = kv[:, 1024 + pr * LANES:1024 + (pr + 1) * LANES]
        v_ref[0, 2 * pr] = jnp.where(lane < MLA_V, vp, 1.0).astype(BF16)
        v_ref[0, 2 * pr + 1] = jnp.where(lane >= MLA_V, vp, 1.0).astype(BF16)


def _inproj(x2, pos2, g_mix, w1, qg, wq, kvg, wkv, invf, sgn, bsz, seqlen):
    t = x2.shape[0]
    tm = TM_PROJ
    per_b = seqlen // tm
    full = lambda shape: pl.BlockSpec(shape, lambda i: (0,) * len(shape))
    head_spec = pl.BlockSpec((1, MLA_HEADS, tm, LANES), lambda i: (i // per_b, 0, i % per_b, 0))
    head_shape = jax.ShapeDtypeStruct((bsz, MLA_HEADS, seqlen, LANES), BF16)
    return pl.pallas_call(
        _inproj_kernel,
        grid=(t // tm,),
        in_specs=[
            pl.BlockSpec((tm, D_MODEL), lambda i: (i, 0)),
            pl.BlockSpec((tm, 1), lambda i: (i, 0)),
            full((1, D_MODEL)), full((D_MODEL, IN_W)),
            full((1, MLA_Q_RANK)), full((MLA_Q_RANK, 2048)),
            full((1, MLA_KV_RANK)), full((MLA_KV_RANK, 1536)),
            full((1, LANES)), full((1, LANES)),
        ],
        out_specs=[
            pl.BlockSpec((tm, 512), lambda i: (i, 0)),
            pl.BlockSpec((tm, 1024), lambda i: (i, 0)),
            pl.BlockSpec((tm, LANES), lambda i: (i, 0)),
            head_spec, head_spec, head_spec,
        ],
        out_shape=[
            jax.ShapeDtypeStruct((t, 512), F32),
            jax.ShapeDtypeStruct((t, 1024), F32),
            jax.ShapeDtypeStruct((t, LANES), F32),
            head_shape, head_shape, head_shape,
        ],
        compiler_params=pltpu.CompilerParams(
            dimension_semantics=("arbitrary",), vmem_limit_bytes=56 * 1024 * 1024),
        name="inproj",
    )(x2, pos2, g_mix, w1, qg, wq, kvg, wkv, invf, sgn)


def _ssd_kernel(z_ref, xbc_ref, dtm_ref, cw_ref, cb_ref, dtb_ref, alog_ref, dsk_ref, ng_ref,
                y_ref, ext_ref, st_ref):
    q = SSD_CHUNK
    rows = SSD_CPS * q

    @pl.when(pl.program_id(1) == 0)
    def _():
        ext_ref[0:CONV_PAD, :] = jnp.zeros((CONV_PAD, SSD_CONV_DIM), F32)
        st_ref[...] = jnp.zeros_like(st_ref)

    ext_ref[CONV_PAD:CONV_PAD + rows, :] = xbc_ref[...]

    lane = lax.broadcasted_iota(I32, (1, LANES), 1)
    row = lax.broadcasted_iota(I32, (q, q), 0)
    col = lax.broadcasted_iota(I32, (q, q), 1)
    tril = row >= col
    tril_b = jnp.where(tril, 1.0, 0.0).astype(BF16)
    spread = jnp.where(
        lax.broadcasted_iota(I32, (LANES, SSD_WIDTH), 0)
        == lax.broadcasted_iota(I32, (LANES, SSD_WIDTH), 1) // SSD_HEAD_DIM, 1.0, 0.0).astype(BF16)
    a_neg = -jnp.exp(alog_ref[...])

    def split3(v):
        hi = v.astype(BF16)
        r1 = v - hi.astype(F32)
        mid = r1.astype(BF16)
        return hi, mid, (r1 - mid.astype(F32)).astype(BF16)

    def dot3_right(parts, m):
        return sum(jnp.dot(p, m, preferred_element_type=F32) for p in parts)

    def expand(cols):
        return dot3_right(split3(cols), spread)

    for ci in range(SSD_CPS):
        lo = ci * q
        conv = cb_ref[...]
        for kk in range(SSD_CONV):
            off = CONV_PAD + lo - (SSD_CONV - 1) + kk
            conv = conv + cw_ref[kk:kk + 1, :] * ext_ref[off:off + q, :]
        u = conv * jax.nn.sigmoid(conv)
        xs = u[:, 0:512]
        bm = u[:, 512:768]
        cm = u[:, 768:1024]

        xdt = dtm_ref[lo:lo + q, :] + dtb_ref[...]
        dt = jnp.maximum(xdt, 0.0) + jnp.log1p(jnp.exp(-jnp.abs(xdt)))
        adt = jnp.where(lane < SSD_HEADS, dt * a_neg, 0.0)
        cum_col = sum(jnp.dot(tril_b, p, preferred_element_type=F32) for p in split3(adt))
        cum_row = cum_col.T

        dt_e = expand(dt)
        ac_e = expand(cum_col)
        last_e = ac_e[q - 1:q, :]
        xd = xs * dt_e
        w_end = xd * jnp.exp(last_e - ac_e)
        eac = jnp.exp(ac_e)
        cdec = jnp.exp(last_e)

        y_parts = []
        for g in range(2):
            gl = g * 256
            bg = bm[:, g * SSD_STATE:(g + 1) * SSD_STATE]
            cg = cm[:, g * SSD_STATE:(g + 1) * SSD_STATE].astype(BF16)
            scores = lax.dot_general(cg, bg.astype(BF16), NT_DIMS, preferred_element_type=F32)
            bgt = bg.T.astype(BF16)
            sprev = st_ref[g]
            yoff = jnp.dot(cg, sprev.astype(BF16), preferred_element_type=F32)
            st_ref[g] = sprev * cdec[:, gl:gl + 256] + jnp.dot(
                bgt, w_end[:, gl:gl + 256].astype(BF16), preferred_element_type=F32)
            for pr in range(2):
                pl_lo = gl + pr * LANES
                xdp = xd[:, pl_lo:pl_lo + LANES].astype(BF16)
                res = []
                for jj in range(2):
                    h_i = g * 4 + pr * 2 + jj
                    seg = cum_col[:, h_i:h_i + 1] - cum_row[h_i:h_i + 1, :]
                    dec = jnp.exp(jnp.where(tril, seg, -jnp.inf))
                    res.append(jnp.dot((scores * dec).astype(BF16), xdp, preferred_element_type=F32))
                ydiag = jnp.where(lane < SSD_HEAD_DIM, res[0], res[1])
                y_parts.append(ydiag + yoff[:, pr * LANES:(pr + 1) * LANES] * eac[:, pl_lo:pl_lo + LANES])
        y = jnp.concatenate(y_parts, axis=1) + dsk_ref[...] * xs
        zz = z_ref[lo:lo + q, :]
        y = y * (zz * jax.nn.sigmoid(zz))
        outs = []
        for g in range(2):
            yg = y[:, g * 256:(g + 1) * 256]
            outs.append(_rms(yg))
        y_ref[lo:lo + q, :] = (jnp.concatenate(outs, axis=1) * ng_ref[...]).astype(BF16)

    ext_ref[0:CONV_PAD, :] = ext_ref[rows:rows + CONV_PAD, :]


def _ssd(z, xbc, dtm, cw, cb, dtb, alog, dsk, ng, bsz, seqlen):
    t = z.shape[0]
    q = SSD_CHUNK
    rows = SSD_CPS * q
    nc = seqlen // rows
    full = lambda shape: pl.BlockSpec(shape, lambda b, c: (0,) * len(shape))
    row_spec = lambda width: pl.BlockSpec((rows, width), lambda b, c: (b * nc + c, 0))
    return pl.pallas_call(
        _ssd_kernel,
        grid=(bsz, nc),
        in_specs=[row_spec(512), row_spec(1024), row_spec(LANES),
                  full((SSD_CONV, SSD_CONV_DIM)), full((1, SSD_CONV_DIM)),
                  full((1, LANES)), full((1, LANES)), full((1, SSD_WIDTH)), full((1, SSD_WIDTH))],
        out_specs=row_spec(512),
        out_shape=jax.ShapeDtypeStruct((t, SSD_WIDTH), BF16),
        scratch_shapes=[pltpu.VMEM((rows + CONV_PAD, SSD_CONV_DIM), F32),
                        pltpu.VMEM((2, SSD_STATE, 256), F32)],
        compiler_params=pltpu.CompilerParams(dimension_semantics=("arbitrary", "arbitrary")),
        name="ssd",
    )(z, xbc, dtm, cw, cb, dtb, alog, dsk, ng)


def _attn_kernel(q_ref, k_ref, v_ref, o_ref, acc_ref):
    i = pl.program_id(2)
    lane = lax.broadcasted_iota(I32, (1, LANES), 1)
    acc_ref[...] = jnp.zeros_like(acc_ref)

    def step(start, width, m_all, masked):
        new_m = []
        for hh in range(ATT_HPS):
            kb = k_ref[0, hh, pl.ds(start, width), :]
            vb = v_ref[0, hh, pl.ds(start, width), :]
            s = lax.dot_general(q_ref[0, hh], kb, NT_DIMS, preferred_element_type=F32)
            if masked:
                r = lax.broadcasted_iota(I32, (TQ, width), 0)
                c = lax.broadcasted_iota(I32, (TQ, width), 1)
                s = jnp.where(c <= r, s, -jnp.inf)
            m_old = m_all[hh]
            m_new = jnp.maximum(m_old, jnp.max(s, axis=-1, keepdims=True))
            alpha = jnp.exp(m_old - m_new)
            p = jnp.exp(s - m_new).astype(BF16)
            acc_ref[hh] = acc_ref[hh] * alpha + jnp.dot(p, vb, preferred_element_type=F32)
            new_m.append(m_new)
        return tuple(new_m)

    m0 = jnp.full((TQ, 1), -jnp.inf, F32)
    per_wide = TK_WIDE // TQ
    n_wide = i // per_wide
    m_all = lax.fori_loop(
        0, n_wide, lambda j, m: step(pl.multiple_of(j * TK_WIDE, TK_WIDE), TK_WIDE, m, False), (m0,) * ATT_HPS)
    m_all = lax.fori_loop(
        n_wide * per_wide, i, lambda j, m: step(pl.multiple_of(j * TQ, TQ), TQ, m, False), m_all)
    step(pl.multiple_of(i * TQ, TQ), TQ, m_all, True)

    outs = []
    for pr in range(ATT_HPS // 2):
        a0 = acc_ref[2 * pr]
        a1 = acc_ref[2 * pr + 1]
        o0 = a0 / a0[:, MLA_V:MLA_V + 1]
        o1 = a1 / a1[:, 0:1]
        outs.append(jnp.where(lane < MLA_V, o0, o1))
    o_ref[0] = jnp.concatenate(outs, axis=1).astype(BF16)


def _attention(q, k, v, bsz, seqlen):
    nq = seqlen // TQ
    hps = ATT_HPS
    kv_spec = pl.BlockSpec((1, hps, seqlen, LANES), lambda b, p, i: (b, p, 0, 0))
    return pl.pallas_call(
        _attn_kernel,
        grid=(bsz, MLA_HEADS // hps, nq),
        in_specs=[pl.BlockSpec((1, hps, TQ, LANES), lambda b, p, i: (b, p, i, 0)), kv_spec, kv_spec],
        out_specs=pl.BlockSpec((1, TQ, hps * MLA_V), lambda b, p, i: (b, i, p)),
        out_shape=jax.ShapeDtypeStruct((bsz, seqlen, MLA_WIDTH), BF16),
        scratch_shapes=[pltpu.VMEM((hps, TQ, LANES), F32)],
        compiler_params=pltpu.CompilerParams(
            dimension_semantics=("arbitrary", "arbitrary", "arbitrary"),
            vmem_limit_bytes=56 * 1024 * 1024),
        name="attention",
    )(q, k, v)


def _outproj_kernel(x_ref, ys_ref, ym_ref, wo_ref, g_ref, wr_ref, br_ref,
                    x1_ref, h2_ref, route_ref, cnt_ref):
    tm = x_ref.shape[0]

    @pl.when(pl.program_id(0) == 0)
    def _():
        cnt_ref[...] = jnp.zeros_like(cnt_ref)

    mix = (jnp.dot(ys_ref[...], wo_ref[0:512, :], preferred_element_type=F32)
           + jnp.dot(ym_ref[...], wo_ref[512:1024, :], preferred_element_type=F32))
    x1 = x_ref[...] + mix
    x1_ref[...] = x1
    h2 = _rms(x1) * g_ref[...]
    h2_ref[...] = _pack_rows(h2)

    lane = lax.broadcasted_iota(I32, (1, LANES), 1)
    lane_f = lane.astype(F32)
    h_hi = h2.astype(BF16)
    h_lo = (h2 - h_hi.astype(F32)).astype(BF16)
    hh = jnp.dot(h_hi, wr_ref[...], preferred_element_type=F32)
    lh = jnp.dot(h_lo, wr_ref[:, 0:LANES], preferred_element_type=F32)
    logits = hh[:, 0:LANES] + (hh[:, LANES:2 * LANES] + lh) + br_ref[...]
    logits = jnp.where(lane < N_EXPERTS, logits, -jnp.inf)

    vals, idxs, hots = [], [], []
    for _ in range(TOP_K):
        mx = jnp.max(logits, axis=-1, keepdims=True)
        idx = jnp.min(jnp.where(logits == mx, lane_f, float(LANES)), axis=-1, keepdims=True)
        hot = lane_f == idx
        logits = jnp.where(hot, -jnp.inf, logits)
        vals.append(mx)
        idxs.append(idx)
        hots.append(hot)
    exps = [jnp.exp(v - vals[0]) for v in vals]
    denom = exps[0] + exps[1] + exps[2] + exps[3]

    multi = (hots[0] | hots[1] | hots[2] | hots[3])
    multi_f = jnp.where(multi, 1.0, 0.0)
    r = lax.broadcasted_iota(I32, (tm, tm), 0)
    c = lax.broadcasted_iota(I32, (tm, tm), 1)
    strict = jnp.where(r > c, 1.0, 0.0).astype(BF16)
    before = jnp.dot(strict, multi_f.astype(BF16), preferred_element_type=F32) + cnt_ref[0:1, :]
    cnt_ref[...] = cnt_ref[...] + jnp.sum(multi_f, axis=0, keepdims=True)

    out = jnp.zeros((tm, LANES), F32)
    for kk in range(TOP_K):
        rank = jnp.sum(jnp.where(hots[kk], before, 0.0), axis=-1, keepdims=True)
        out = jnp.where(lane == kk, idxs[kk], out)
        out = jnp.where(lane == TOP_K + kk, exps[kk] / denom, out)
        out = jnp.where(lane == 2 * TOP_K + kk, rank, out)
    route_ref[...] = out


def _outproj(x2, y_ssd, y_mla, wo, g_ffn, wr, br):
    t = x2.shape[0]
    tm = TM_PROJ
    full = lambda shape: pl.BlockSpec(shape, lambda i: (0,) * len(shape))
    rows = lambda width: pl.BlockSpec((tm, width), lambda i: (i, 0))
    return pl.pallas_call(
        _outproj_kernel,
        grid=(t // tm,),
        in_specs=[rows(D_MODEL), rows(512), rows(512), full((1024, D_MODEL)), full((1, D_MODEL)),
                  full((D_MODEL, 2 * LANES)), full((1, LANES))],
        out_specs=[rows(D_MODEL), rows(HALF), rows(LANES), full((8, LANES))],
        out_shape=[jax.ShapeDtypeStruct((t, D_MODEL), F32),
                   jax.ShapeDtypeStruct((t, HALF), I32),
                   jax.ShapeDtypeStruct((t, LANES), F32),
                   jax.ShapeDtypeStruct((8, LANES), F32)],
        compiler_params=pltpu.CompilerParams(
            dimension_semantics=("arbitrary",), vmem_limit_bytes=40 * 1024 * 1024),
        name="outproj_router",
    )(x2, y_ssd, y_mla, wo, g_ffn, wr, br)


def _ffn_kernel(be_ref, bv_ref, nx_ref, sl_ref, xs_ref, wgu_hbm, bgu_ref, wd_hbm, bd_ref, ys_ref,
                wgu_st, wd_st, wgu_bf, wd_bf, sem):
    i = pl.program_id(0)
    e = be_ref[i]
    valid = bv_ref[i]
    slot = sl_ref[i]
    first = ((i == 0) | (e != be_ref[jnp.maximum(i - 1, 0)])) & (valid > 0)

    def weight_copies(expert, dst_slot):
        return (pltpu.make_async_copy(wgu_hbm.at[expert], wgu_st.at[dst_slot], sem.at[0, dst_slot]),
                pltpu.make_async_copy(wd_hbm.at[expert], wd_st.at[dst_slot], sem.at[1, dst_slot]))

    @pl.when(i == 0)
    def _():
        for cp in weight_copies(e, slot):
            cp.start()

    @pl.when(first)
    def _():
        for cp in weight_copies(e, slot):
            cp.wait()

        @pl.when(nx_ref[i] >= 0)
        def _():
            for cp in weight_copies(nx_ref[i], 1 - slot):
                cp.start()

        wgu_bf[...] = wgu_st[slot].astype(BF16)
        wd_bf[...] = wd_st[slot].astype(BF16)

    @pl.when(valid > 0)
    def _():
        row = lax.broadcasted_iota(I32, (MOE_BM, 1), 0)
        x_lo, x_hi = _unpack_rows(jnp.where(row < valid, xs_ref[...], 0))
        gu = (jnp.dot(x_lo.astype(BF16), wgu_bf[0:HALF, :], preferred_element_type=F32)
              + jnp.dot(x_hi.astype(BF16), wgu_bf[HALF:D_MODEL, :], preferred_element_type=F32)
              + bgu_ref[0])
        gate = jnp.minimum(gu[:, :D_FF], SWIGLU_LIMIT)
        up = jnp.clip(gu[:, D_FF:], -SWIGLU_LIMIT, SWIGLU_LIMIT)
        glu = gate * jax.nn.sigmoid(SWIGLU_ALPHA * gate)
        mid = ((up + 1.0) * glu).astype(BF16)
        ys_ref[...] = _pack_rows(jnp.dot(mid, wd_bf[...], preferred_element_type=F32) + bd_ref[0])

    @pl.when(valid == 0)
    def _():
        ys_ref[...] = jnp.zeros_like(ys_ref)


def _expert_ffn(block_e, block_valid, block_next, block_slot, xs, wgu, bgu, wd, bd):
    n_slots = xs.shape[0]
    bm = MOE_BM
    grid_spec = pltpu.PrefetchScalarGridSpec(
        num_scalar_prefetch=4,
        grid=(n_slots // bm,),
        in_specs=[
            pl.BlockSpec((bm, HALF), lambda i, be, bv, nx, sl: (i, 0)),
            pl.BlockSpec(memory_space=pl.ANY),
            pl.BlockSpec((1, 1, 2 * D_FF), lambda i, be, bv, nx, sl: (be[i], 0, 0)),
            pl.BlockSpec(memory_space=pl.ANY),
            pl.BlockSpec((1, 1, D_MODEL), lambda i, be, bv, nx, sl: (be[i], 0, 0)),
        ],
        out_specs=pl.BlockSpec((bm, HALF), lambda i, be, bv, nx, sl: (i, 0)),
        scratch_shapes=[pltpu.VMEM((2, D_MODEL, 2 * D_FF), F32), pltpu.VMEM((2, D_FF, D_MODEL), F32),
                        pltpu.VMEM((D_MODEL, 2 * D_FF), BF16), pltpu.VMEM((D_FF, D_MODEL), BF16),
                        pltpu.SemaphoreType.DMA((2, 2))],
    )
    return pl.pallas_call(
        _ffn_kernel,
        grid_spec=grid_spec,
        out_shape=jax.ShapeDtypeStruct((n_slots, HALF), I32),
        compiler_params=pltpu.CompilerParams(
            dimension_semantics=("arbitrary",), vmem_limit_bytes=56 * 1024 * 1024),
        name="expert_ffn",
    )(block_e, block_valid, block_next, block_slot, xs, wgu, bgu, wd, bd)


def _plan_kernel(sp_ref, route_ref, dest_ref):
    rt = route_ref[...].T
    idx = rt[0:TOP_K, :]
    rank = rt[2 * TOP_K:3 * TOP_K, :]
    start = jnp.zeros(idx.shape, F32)
    for e_i in range(N_EXPERTS):
        start = jnp.where(idx == float(e_i), sp_ref[e_i].astype(F32), start)
    dest_ref[...] = (start + rank).astype(I32)


def _slot_plan(route, start_pad):
    t = route.shape[0]
    tm = TM_PLAN
    grid_spec = pltpu.PrefetchScalarGridSpec(
        num_scalar_prefetch=1,
        grid=(t // tm,),
        in_specs=[pl.BlockSpec((tm, LANES), lambda i, sp: (i, 0))],
        out_specs=pl.BlockSpec((TOP_K, tm), lambda i, sp: (0, i)),
    )
    return pl.pallas_call(
        _plan_kernel,
        grid_spec=grid_spec,
        out_shape=jax.ShapeDtypeStruct((TOP_K, t), I32),
        compiler_params=pltpu.CompilerParams(dimension_semantics=("arbitrary",)),
        name="slot_plan",
    )(start_pad, route)


SC_SCATTER_CHUNK = 32
SC_GATHER_CHUNK = 16


def _sc_workers():
    info = plsc.get_sparse_core_info()
    return info.num_cores, info.num_cores * info.num_subcores


def _sc_scatter_rows(rows, dest_km, n_out):
    t, w = rows.shape
    ch = SC_SCATTER_CHUNK
    n_cores, n_workers = _sc_workers()
    n_chunks = t // n_workers // ch
    mesh = plsc.VectorSubcoreMesh(core_axis_name="c", subcore_axis_name="s")

    @functools.partial(
        pl.kernel, mesh=mesh, out_type=jax.ShapeDtypeStruct((n_out, w), rows.dtype),
        scratch_types=[pltpu.VMEM((TOP_K, n_chunks, ch), I32), pltpu.VMEM((2, ch, w), rows.dtype),
                       pltpu.SemaphoreType.DMA((2,)), pltpu.SemaphoreType.DMA((2,))],
        name="sc_dispatch_scatter")
    def scatter_kernel(rows_hbm, dest_hbm, out_hbm, idx_v, rows_v, sem_in, sem_out):
        wid = lax.axis_index("s") * n_cores + lax.axis_index("c")
        first = wid * n_chunks
        for kk in range(TOP_K):
            pltpu.sync_copy(dest_hbm.at[kk, pl.ds(first, n_chunks)], idx_v.at[kk])

        def load(cc, b):
            return pltpu.make_async_copy(rows_hbm.at[pl.ds((first + cc) * ch, ch)], rows_v.at[b], sem_in.at[b])

        def scatters(cc, b):
            return [pltpu.make_async_copy(rows_v.at[b], out_hbm.at[idx_v.at[kk, cc]], sem_out.at[b])
                    for kk in range(TOP_K)]

        load(0, 0).start()
        load(1, 1).start()

        @pl.loop(0, n_chunks, step=2)
        def _(c):
            for b in range(2):
                load(c + b, b).wait()
                for cp in scatters(c + b, b):
                    cp.start()
            for b in range(2):
                for cp in scatters(c + b, b):
                    cp.wait()

                @pl.when(c + 2 + b < n_chunks)
                def _():
                    load(c + 2 + b, b).start()

    return scatter_kernel(rows, dest_km.reshape(TOP_K, t // ch, ch))


def _sc_gather_rows(table, dest_km):
    _, w = table.shape
    t = dest_km.shape[1]
    ch = SC_GATHER_CHUNK
    n_cores, n_workers = _sc_workers()
    n_chunks = t // n_workers // ch
    mesh = plsc.VectorSubcoreMesh(core_axis_name="c", subcore_axis_name="s")

    @functools.partial(
        pl.kernel, mesh=mesh, out_type=jax.ShapeDtypeStruct((TOP_K, t, w), table.dtype),
        scratch_types=[pltpu.VMEM((TOP_K, n_chunks, ch), I32), pltpu.VMEM((2, TOP_K, ch, w), table.dtype),
                       pltpu.SemaphoreType.DMA((2,)), pltpu.SemaphoreType.DMA((2,))],
        name="sc_combine_gather")
    def gather_kernel(table_hbm, dest_hbm, out_hbm, idx_v, rows_v, sem_in, sem_out):
        wid = lax.axis_index("s") * n_cores + lax.axis_index("c")
        first = wid * n_chunks
        for kk in range(TOP_K):
            pltpu.sync_copy(dest_hbm.at[kk, pl.ds(first, n_chunks)], idx_v.at[kk])

        def gathers(cc, b):
            return [pltpu.make_async_copy(table_hbm.at[idx_v.at[kk, cc]], rows_v.at[b, kk], sem_in.at[b])
                    for kk in range(TOP_K)]

        def stores(cc, b):
            return [pltpu.make_async_copy(rows_v.at[b, kk], out_hbm.at[kk, pl.ds((first + cc) * ch, ch)],
                                          sem_out.at[b]) for kk in range(TOP_K)]

        for b in range(2):
            for cp in gathers(b, b):
                cp.start()

        @pl.loop(0, n_chunks, step=2)
        def _(c):
            for b in range(2):
                for cp in gathers(c + b, b):
                    cp.wait()
                for cp in stores(c + b, b):
                    cp.start()
            for b in range(2):
                for cp in stores(c + b, b):
                    cp.wait()

                @pl.when(c + 2 + b < n_chunks)
                def _():
                    for cp in gathers(c + 2 + b, b):
                        cp.start()

    return gather_kernel(table, dest_km.reshape(TOP_K, t // ch, ch))


def _combine_kernel(x1_ref, yg_ref, route_ref, g_ref, o_ref, *, final_norm):
    moe_lo = jnp.zeros((x1_ref.shape[0], HALF), F32)
    moe_hi = jnp.zeros((x1_ref.shape[0], HALF), F32)
    for kk in range(TOP_K):
        gate = route_ref[:, TOP_K + kk:TOP_K + kk + 1]
        y_lo, y_hi = _unpack_rows(yg_ref[kk])
        moe_lo = moe_lo + gate * y_lo
        moe_hi = moe_hi + gate * y_hi
    acc = x1_ref[...] + jnp.concatenate([moe_lo, moe_hi], axis=1)
    o_ref[...] = _rms(acc) * g_ref[...] if final_norm else acc


def _combine(x1, yg, route, g_final, final_norm):
    t = x1.shape[0]
    tm = TM_COMB
    return pl.pallas_call(
        functools.partial(_combine_kernel, final_norm=final_norm),
        grid=(t // tm,),
        in_specs=[pl.BlockSpec((tm, D_MODEL), lambda i: (i, 0)),
                  pl.BlockSpec((TOP_K, tm, HALF), lambda i: (0, i, 0)),
                  pl.BlockSpec((tm, LANES), lambda i: (i, 0)),
                  pl.BlockSpec((1, D_MODEL), lambda i: (0, 0))],
        out_specs=pl.BlockSpec((tm, D_MODEL), lambda i: (i, 0)),
        out_shape=jax.ShapeDtypeStruct((t, D_MODEL), F32),
        compiler_params=pltpu.CompilerParams(dimension_semantics=("arbitrary",)),
        name="combine",
    )(x1, yg, route, g_final)


def _prep_in_weights(w_in, w_uq, w_ukv):
    w_z = w_in[:, 0:512]
    w_xbc = w_in[:, 512:1536]
    w_dt = w_in[:, 1536:1544]
    w_cq = w_in[:, 1544:1800]
    w_ckv = w_in[:, 1800:1928]
    w_kr = w_in[:, 1928:1960]
    half = MLA_ROPE // 2
    zeros = lambda rows, width: jnp.zeros(rows + (width,), BF16)
    cat = lambda parts: jnp.concatenate([p.astype(BF16) for p in parts], axis=-1)
    d = (D_MODEL,)
    misc1 = [w_dt, zeros(d, MLA_NOPE - SSD_HEADS), w_kr, zeros(d, LANES - MLA_QK)]
    misc2 = [zeros(d, MLA_NOPE), w_kr[:, half:], w_kr[:, :half], zeros(d, LANES - MLA_QK)]
    w1 = cat([w_z, w_xbc, w_cq, w_ckv] + misc1 + misc2)

    wq3 = w_uq.reshape(MLA_Q_RANK, MLA_HEADS, MLA_QK)
    qh = (MLA_Q_RANK, MLA_HEADS)
    main = cat([wq3, zeros(qh, LANES - MLA_QK)])
    swap = cat([zeros(qh, MLA_NOPE), wq3[:, :, MLA_NOPE + half:], wq3[:, :, MLA_NOPE:MLA_NOPE + half],
                zeros(qh, LANES - MLA_QK)])
    wq = jnp.concatenate([main.reshape(MLA_Q_RANK, -1), swap.reshape(MLA_Q_RANK, -1)], axis=1)

    wkv3 = w_ukv.reshape(MLA_KV_RANK, MLA_HEADS, MLA_NOPE + MLA_V)
    kh = (MLA_KV_RANK, MLA_HEADS)
    kpart = cat([wkv3[:, :, :MLA_NOPE], zeros(kh, LANES - MLA_NOPE)])
    vpart = wkv3[:, :, MLA_NOPE:].astype(BF16)
    wkv = jnp.concatenate([kpart.reshape(MLA_KV_RANK, -1), vpart.reshape(MLA_KV_RANK, -1)], axis=1)
    return w1, wq, wkv


def _rope_consts():
    half = MLA_ROPE // 2
    inv_freq = ROPE_THETA ** (-jnp.arange(0, MLA_ROPE, 2, dtype=F32) / MLA_ROPE)
    invf = (jnp.zeros((1, LANES), F32)
            .at[0, MLA_NOPE:MLA_NOPE + half].set(inv_freq)
            .at[0, MLA_NOPE + half:MLA_QK].set(inv_freq))
    sgn = (jnp.zeros((1, LANES), F32)
           .at[0, MLA_NOPE:MLA_NOPE + half].set(-1.0)
           .at[0, MLA_NOPE + half:MLA_QK].set(1.0))
    return invf, sgn


def _pad_lanes(v, fill=0.0):
    return jnp.full((1, LANES), fill, F32).at[0, :v.shape[0]].set(v)


def kernel(x, positions, norm_mix_g, w_in, conv_w, conv_b, dt_bias, a_log, d_skip, ssd_norm_g, q_norm_g, w_uq, kv_norm_g, w_ukv, w_out, norm_ffn_g, w_router, b_router, w_gate_up, b_gate_up, w_down, b_down, norm_final_g):
    bsz, seqlen, d = x.shape
    t = bsz * seqlen
    depth = w_in.shape[0]
    x2 = x.reshape(t, d)
    pos2 = positions.reshape(t, 1).astype(I32)
    invf, sgn = _rope_consts()

    for l in range(depth):
        w1, wq, wkv = _prep_in_weights(w_in[l], w_uq[l], w_ukv[l])
        z, xbc, dtm, q, k, v = _inproj(
            x2, pos2, norm_mix_g[l][None, :], w1, q_norm_g[l][None, :], wq, kv_norm_g[l][None, :], wkv,
            invf, sgn, bsz, seqlen)
        y_ssd = _ssd(z, xbc, dtm, conv_w[l], conv_b[l][None, :], _pad_lanes(dt_bias[l]), _pad_lanes(a_log[l]),
                     jnp.repeat(d_skip[l], SSD_HEAD_DIM)[None, :], ssd_norm_g[l][None, :], bsz, seqlen)
        y_mla = _attention(q, k, v, bsz, seqlen).reshape(t, MLA_WIDTH)

        wr = jnp.zeros((d, LANES), F32).at[:, :N_EXPERTS].set(w_router[l])
        wr_hi = wr.astype(BF16)
        wr_lo = (wr - wr_hi.astype(F32)).astype(BF16)
        x1, h2p, route, cnt = _outproj(x2, y_ssd, y_mla, w_out[l].astype(BF16), norm_ffn_g[l][None, :],
                                       jnp.concatenate([wr_hi, wr_lo], axis=1), _pad_lanes(b_router[l]))

        counts = cnt[0, :N_EXPERTS].astype(I32)
        padded = ((counts + MOE_BM - 1) // MOE_BM) * MOE_BM
        end_pad = jnp.cumsum(padded)
        start_pad = end_pad - padded
        n_slots = t * TOP_K + N_EXPERTS * MOE_BM
        n_blocks = n_slots // MOE_BM
        block_start = jnp.arange(n_blocks, dtype=I32) * MOE_BM
        block_e = jnp.minimum(jnp.sum(block_start[:, None] >= end_pad[None, :], axis=1), N_EXPERTS - 1).astype(I32)
        eids = jnp.arange(N_EXPERTS, dtype=I32)
        block_hot = block_e[:, None] == eids[None, :]
        per_block = lambda table: jnp.sum(jnp.where(block_hot, table[None, :], 0), axis=1).astype(I32)
        block_valid = jnp.clip(per_block(counts) - (block_start - per_block(start_pad)), 0, MOE_BM).astype(I32)
        used = counts > 0
        later_used = jnp.where((eids[None, :] > eids[:, None]) & used[None, :], eids[None, :], N_EXPERTS)
        next_used = jnp.min(later_used, axis=1)
        next_used = jnp.where(next_used < N_EXPERTS, next_used, -1).astype(I32)
        stage_slot = ((jnp.cumsum(used.astype(I32)) - 1) & 1).astype(I32)
        dest_km = _slot_plan(route, start_pad.astype(I32))

        xs = _sc_scatter_rows(h2p, dest_km, n_slots)
        ys = _expert_ffn(block_e, block_valid, per_block(next_used), per_block(stage_slot), xs,
                         w_gate_up[l], b_gate_up[l][:, None, :], w_down[l], b_down[l][:, None, :])
        yg = _sc_gather_rows(ys, dest_km)
        x2 = _combine(x1, yg, route, norm_final_g[None, :], l == depth - 1)
    return x2.reshape(bsz, seqlen, d)
```

```python
import functools

import jax
import jax.numpy as jnp
import numpy as np
from jax import lax
from jax.experimental import pallas as pl
from jax.experimental.pallas import tpu as pltpu
from jax.experimental.pallas import tpu_sc as plsc

F32 = jnp.float32
BF16 = jnp.bfloat16
I32 = jnp.int32
HIGHEST = lax.Precision.HIGHEST

D_MODEL = 1024
EPS = 1e-6
LANES = 128

SSD_HEADS = 8
SSD_HEAD_DIM = 64
SSD_WIDTH = 512
SSD_STATE = 128
SSD_CONV = 4
SSD_CHUNK = 128
SSD_CONV_DIM = 1024
CONV_PAD = 8
SSD_CPS = 2

MLA_HEADS = 8
MLA_Q_RANK = 256
MLA_KV_RANK = 128
MLA_NOPE = 64
MLA_ROPE = 32
MLA_V = 64
MLA_QK = MLA_NOPE + MLA_ROPE
MLA_WIDTH = 512
ROPE_THETA = 10000.0
LOG2_E = 1.4426950408889634

N_EXPERTS = 32
TOP_K = 4
D_FF = 1024
SWIGLU_LIMIT = 7.0
SWIGLU_ALPHA = 1.702

IN_W = 512 + 1024 + 256 + 128 + 128 + 128

TM_PROJ = 512
TQ = 512
TK = 512
ATT_HPS = 4
MOE_BM = 256
TM_COMB = 512
TM_PLAN = 2048

NT_DIMS = (((1,), (1,)), ((), ()))


def _rms(x):
    return x * lax.rsqrt(jnp.mean(x * x, axis=-1, keepdims=True) + EPS)


HALF = D_MODEL // 2
HI_MASK = np.int32(-65536)


def _pack_rows(a):
    lo = lax.bitcast_convert_type(a[:, :HALF].astype(BF16).astype(F32), I32)
    hi = lax.bitcast_convert_type(a[:, HALF:].astype(BF16).astype(F32), I32)
    return (hi & HI_MASK) | lax.shift_right_logical(lo, 16)


def _unpack_rows(p):
    lo = lax.bitcast_convert_type(lax.shift_left(p, 16), F32)
    hi = lax.bitcast_convert_type(p & HI_MASK, F32)
    return lo, hi


def _inproj_kernel(x_ref, pos_ref, g_ref, w1_ref, qg_ref, wq_ref, kvg_ref, wkv_ref, invf_ref, sgn_ref,
                   z_ref, xbc_ref, dtm_ref, q_ref, k_ref, v_ref):
    x = x_ref[...]
    h = (_rms(x) * g_ref[...]).astype(BF16)
    p = jnp.dot(h, w1_ref[...], preferred_element_type=F32)
    z_ref[...] = p[:, 0:512]
    xbc_ref[...] = p[:, 512:1536]
    cq = p[:, 1536:1792]
    ckv = p[:, 1792:1920]
    m1 = p[:, 1920:2048]
    m2 = p[:, 2048:2176]
    dtm_ref[...] = m1

    lane = lax.broadcasted_iota(I32, (1, LANES), 1)
    ang = pos_ref[...].astype(F32) * invf_ref[...]
    rope_lane = (lane >= MLA_NOPE) & (lane < MLA_QK)
    cos_t = jnp.where(rope_lane, jnp.cos(ang), 0.0)
    sin_t = jnp.sin(ang) * sgn_ref[...]
    cosq_t = jnp.where(lane < MLA_NOPE, 1.0, cos_t)
    scale = MLA_QK ** -0.5 * LOG2_E

    cqn = (_rms(cq) * qg_ref[...]).astype(BF16)
    qq = jnp.dot(cqn, wq_ref[...], preferred_element_type=F32)
    ckvn = (_rms(ckv) * kvg_ref[...]).astype(BF16)
    kv = jnp.dot(ckvn, wkv_ref[...], preferred_element_type=F32)
    krot = m1 * cos_t + m2 * sin_t
    for h_i in range(MLA_HEADS):
        lo = h_i * LANES
        qm = qq[:, lo:lo + LANES]
        qs = qq[:, 1024 + lo:1024 + lo + LANES]
        q_ref[0, h_i] = ((qm * cosq_t + qs * sin_t) * scale).astype(BF16)
        k_ref[0, h_i] = (kv[:, lo:lo + LANES] + krot).astype(BF16)
    for pr in range(MLA_HEADS // 2):
        vp = kv[:, 1024 + pr * LANES:1024 + (pr + 1) * LANES]
        v_ref[0, 2 * pr] = jnp.where(lane < MLA_V, vp, 1.0).astype(BF16)
        v_ref[0, 2 * pr + 1] = jnp.where(lane >= MLA_V, vp, 1.0).astype(BF16)


def _inproj(x2, pos2, g_mix, w1, qg, wq, kvg, wkv, invf, sgn, bsz, seqlen):
    t = x2.shape[0]
    tm = TM_PROJ
    per_b = seqlen // tm
    full = lambda shape: pl.BlockSpec(shape, lambda i: (0,) * len(shape))
    head_spec = pl.BlockSpec((1, MLA_HEADS, tm, LANES), lambda i: (i // per_b, 0, i % per_b, 0))
    head_shape = jax.ShapeDtypeStruct((bsz, MLA_HEADS, seqlen, LANES), BF16)
    return pl.pallas_call(
        _inproj_kernel,
        grid=(t // tm,),
        in_specs=[
            pl.BlockSpec((tm, D_MODEL), lambda i: (i, 0)),
            pl.BlockSpec((tm, 1), lambda i: (i, 0)),
            full((1, D_MODEL)), full((D_MODEL, IN_W)),
            full((1, MLA_Q_RANK)), full((MLA_Q_RANK, 2048)),
            full((1, MLA_KV_RANK)), full((MLA_KV_RANK, 1536)),
            full((1, LANES)), full((1, LANES)),
        ],
        out_specs=[
            pl.BlockSpec((tm, 512), lambda i: (i, 0)),
            pl.BlockSpec((tm, 1024), lambda i: (i, 0)),
            pl.BlockSpec((tm, LANES), lambda i: (i, 0)),
            head_spec, head_spec, head_spec,
        ],
        out_shape=[
            jax.ShapeDtypeStruct((t, 512), F32),
            jax.ShapeDtypeStruct((t, 1024), F32),
            jax.ShapeDtypeStruct((t, LANES), F32),
            head_shape, head_shape, head_shape,
        ],
        compiler_params=pltpu.CompilerParams(
            dimension_semantics=("arbitrary",), vmem_limit_bytes=56 * 1024 * 1024),
        name="inproj",
    )(x2, pos2, g_mix, w1, qg, wq, kvg, wkv, invf, sgn)


def _ssd_kernel(z_ref, xbc_ref, dtm_ref, cw_ref, cb_ref, dtb_ref, alog_ref, dsk_ref, ng_ref,
                y_ref, ext_ref, st_ref):
    q = SSD_CHUNK
    rows = SSD_CPS * q

    @pl.when(pl.program_id(1) == 0)
    def _():
        ext_ref[0:CONV_PAD, :] = jnp.zeros((CONV_PAD, SSD_CONV_DIM), F32)
        st_ref[...] = jnp.zeros_like(st_ref)

    ext_ref[CONV_PAD:CONV_PAD + rows, :] = xbc_ref[...]

    lane = lax.broadcasted_iota(I32, (1, LANES), 1)
    row = lax.broadcasted_iota(I32, (q, q), 0)
    col = lax.broadcasted_iota(I32, (q, q), 1)
    tril = row >= col
    tril_b = jnp.where(tril, 1.0, 0.0).astype(BF16)
    spread = jnp.where(
        lax.broadcasted_iota(I32, (LANES, SSD_WIDTH), 0)
        == lax.broadcasted_iota(I32, (LANES, SSD_WIDTH), 1) // SSD_HEAD_DIM, 1.0, 0.0).astype(BF16)
    a_neg = -jnp.exp(alog_ref[...])

    def split3(v):
        hi = v.astype(BF16)
        r1 = v - hi.astype(F32)
        mid = r1.astype(BF16)
        return hi, mid, (r1 - mid.astype(F32)).astype(BF16)

    def dot3_right(parts, m):
        return sum(jnp.dot(p, m, preferred_element_type=F32) for p in parts)

    def expand(cols):
        return dot3_right(split3(cols), spread)

    for ci in range(SSD_CPS):
        lo = ci * q
        conv = cb_ref[...]
        for kk in range(SSD_CONV):
            off = CONV_PAD + lo - (SSD_CONV - 1) + kk
            conv = conv + cw_ref[kk:kk + 1, :] * ext_ref[off:off + q, :]
        u = conv * jax.nn.sigmoid(conv)
        xs = u[:, 0:512]
        bm = u[:, 512:768]
        cm = u[:, 768:1024]

        xdt = dtm_ref[lo:lo + q, :] + dtb_ref[...]
        dt = jnp.maximum(xdt, 0.0) + jnp.log1p(jnp.exp(-jnp.abs(xdt)))
        adt = jnp.where(lane < SSD_HEADS, dt * a_neg, 0.0)
        cum_col = sum(jnp.dot(tril_b, p, preferred_element_type=F32) for p in split3(adt))
        cum_row = cum_col.T

        dt_e = expand(dt)
        ac_e = expand(cum_col)
        last_e = ac_e[q - 1:q, :]
        xd = xs * dt_e
        w_end = xd * jnp.exp(last_e - ac_e)
        eac = jnp.exp(ac_e)
        cdec = jnp.exp(last_e)

        y_parts = []
        for g in range(2):
            gl = g * 256
            bg = bm[:, g * SSD_STATE:(g + 1) * SSD_STATE]
            cg = cm[:, g * SSD_STATE:(g + 1) * SSD_STATE].astype(BF16)
            scores = lax.dot_general(cg, bg.astype(BF16), NT_DIMS, preferred_element_type=F32)
            bgt = bg.T.astype(BF16)
            sprev = st_ref[g]
            yoff = jnp.dot(cg, sprev.astype(BF16), preferred_element_type=F32)
            st_ref[g] = sprev * cdec[:, gl:gl + 256] + jnp.dot(
                bgt, w_end[:, gl:gl + 256].astype(BF16), preferred_element_type=F32)
            for pr in range(2):
                pl_lo = gl + pr * LANES
                xdp = xd[:, pl_lo:pl_lo + LANES].astype(BF16)
                res = []
                for jj in range(2):
                    h_i = g * 4 + pr * 2 + jj
                    seg = cum_col[:, h_i:h_i + 1] - cum_row[h_i:h_i + 1, :]
                    dec = jnp.exp(jnp.where(tril, seg, -jnp.inf))
                    res.append(jnp.dot((scores * dec).astype(BF16), xdp, preferred_element_type=F32))
                ydiag = jnp.where(lane < SSD_HEAD_DIM, res[0], res[1])
                y_parts.append(ydiag + yoff[:, pr * LANES:(pr + 1) * LANES] * eac[:, pl_lo:pl_lo + LANES])
        y = jnp.concatenate(y_parts, axis=1) + dsk_ref[...] * xs
        zz = z_ref[lo:lo + q, :]
        y = y * (zz * jax.nn.sigmoid(zz))
        outs = []
        for g in range(2):
            yg = y[:, g * 256:(g + 1) * 256]
            outs.append(_rms(yg))
        y_ref[lo:lo + q, :] = (jnp.concatenate(outs, axis=1) * ng_ref[...]).astype(BF16)

    ext_ref[0:CONV_PAD, :] = ext_ref[rows:rows + CONV_PAD, :]


def _ssd(z, xbc, dtm, cw, cb, dtb, alog, dsk, ng, bsz, seqlen):
    t = z.shape[0]
    q = SSD_CHUNK
    rows = SSD_CPS * q
    nc = seqlen // rows
    full = lambda shape: pl.BlockSpec(shape, lambda b, c: (0,) * len(shape))
    row_spec = lambda width: pl.BlockSpec((rows, width), lambda b, c: (b * nc + c, 0))
    return pl.pallas_call(
        _ssd_kernel,
        grid=(bsz, nc),
        in_specs=[row_spec(512), row_spec(1024), row_spec(LANES),
                  full((SSD_CONV, SSD_CONV_DIM)), full((1, SSD_CONV_DIM)),
                  full((1, LANES)), full((1, LANES)), full((1, SSD_WIDTH)), full((1, SSD_WIDTH))],
        out_specs=row_spec(512),
        out_shape=jax.ShapeDtypeStruct((t, SSD_WIDTH), BF16),
        scratch_shapes=[pltpu.VMEM((rows + CONV_PAD, SSD_CONV_DIM), F32),
                        pltpu.VMEM((2, SSD_STATE, 256), F32)],
        compiler_params=pltpu.CompilerParams(dimension_semantics=("arbitrary", "arbitrary")),
        name="ssd",
    )(z, xbc, dtm, cw, cb, dtb, alog, dsk, ng)


def _attn_kernel(q_ref, k_ref, v_ref, o_ref, acc_ref, s_ref, bmax_ref):
    i = pl.program_id(2)
    lane = lax.broadcasted_iota(I32, (1, LANES), 1)
    acc_ref[...] = jnp.zeros_like(acc_ref)

    def scores(j, slot):
        start = pl.multiple_of(j * TK, TK)
        for hh in range(ATT_HPS):
            kb = k_ref[0, hh, pl.ds(start, TK), :]
            s = lax.dot_general(q_ref[0, hh], kb, NT_DIMS, preferred_element_type=F32)
            s_ref[slot, hh] = s
            bmax_ref[slot, hh] = jnp.broadcast_to(jnp.max(s, axis=-1, keepdims=True), (TQ, LANES))

    def consume(j, slot, m_all, masked):
        start = pl.multiple_of(j * TK, TK)
        new_m = []
        for hh in range(ATT_HPS):
            vb = v_ref[0, hh, pl.ds(start, TK), :]
            s = s_ref[slot, hh]
            if masked:
                r = lax.broadcasted_iota(I32, (TQ, TK), 0)
                c = lax.broadcasted_iota(I32, (TQ, TK), 1)
                s = jnp.where(c <= r, s, -jnp.inf)
                block_max = jnp.broadcast_to(jnp.max(s, axis=-1, keepdims=True), (TQ, LANES))
            else:
                block_max = bmax_ref[slot, hh]
            m_old = m_all[hh]
            m_new = jnp.maximum(m_old, block_max)
            alpha = jnp.exp2(m_old - m_new)
            p = jnp.exp2(s - jnp.tile(m_new, (1, TK // LANES))).astype(BF16)
            acc_ref[hh] = acc_ref[hh] * alpha + jnp.dot(p, vb, preferred_element_type=F32)
            new_m.append(m_new)
        return tuple(new_m)

    def pair(p, m_all):
        scores(2 * p + 1, 1)
        m_all = consume(2 * p, 0, m_all, False)
        scores(2 * p + 2, 0)
        return consume(2 * p + 1, 1, m_all, False)

    def odd_tail(_, m_all):
        scores(i, 1)
        return consume(i - 1, 0, m_all, False)

    m0 = jnp.full((TQ, LANES), -jnp.inf, F32)
    scores(0, 0)
    m_all = lax.fori_loop(0, i // 2, pair, (m0,) * ATT_HPS)
    m_all = lax.fori_loop(0, i & 1, odd_tail, m_all)

    @pl.when((i & 1) == 0)
    def _():
        consume(i, 0, m_all, True)

    @pl.when((i & 1) == 1)
    def _():
        consume(i, 1, m_all, True)

    outs = []
    for pr in range(ATT_HPS // 2):
        a0 = acc_ref[2 * pr]
        a1 = acc_ref[2 * pr + 1]
        o0 = a0 / a0[:, MLA_V:MLA_V + 1]
        o1 = a1 / a1[:, 0:1]
        outs.append(jnp.where(lane < MLA_V, o0, o1))
    o_ref[0] = jnp.concatenate(outs, axis=1).astype(BF16)


def _attention(q, k, v, bsz, seqlen):
    nq = seqlen // TQ
    hps = ATT_HPS
    kv_spec = pl.BlockSpec((1, hps, seqlen, LANES), lambda b, p, i: (b, p, 0, 0))
    return pl.pallas_call(
        _attn_kernel,
        grid=(bsz, MLA_HEADS // hps, nq),
        in_specs=[pl.BlockSpec((1, hps, TQ, LANES), lambda b, p, i: (b, p, i, 0)), kv_spec, kv_spec],
        out_specs=pl.BlockSpec((1, TQ, hps * MLA_V), lambda b, p, i: (b, i, p)),
        out_shape=jax.ShapeDtypeStruct((bsz, seqlen, MLA_WIDTH), BF16),
        scratch_shapes=[pltpu.VMEM((hps, TQ, LANES), F32), pltpu.VMEM((2, hps, TQ, TK), F32),
                        pltpu.VMEM((2, hps, TQ, LANES), F32)],
        compiler_params=pltpu.CompilerParams(
            dimension_semantics=("arbitrary", "arbitrary", "arbitrary"),
            vmem_limit_bytes=56 * 1024 * 1024),
        name="attention",
    )(q, k, v)


def _outproj_kernel(x_ref, ys_ref, ym_ref, wo_ref, g_ref, wr_ref, br_ref,
                    x1_ref, h2_ref, route_ref, cnt_ref):
    tm = x_ref.shape[0]

    @pl.when(pl.program_id(0) == 0)
    def _():
        cnt_ref[...] = jnp.zeros_like(cnt_ref)

    mix = (jnp.dot(ys_ref[...], wo_ref[0:512, :], preferred_element_type=F32)
           + jnp.dot(ym_ref[...], wo_ref[512:1024, :], preferred_element_type=F32))
    x1 = x_ref[...] + mix
    x1_ref[...] = x1
    h2 = _rms(x1) * g_ref[...]
    h2_ref[...] = _pack_rows(h2)

    lane = lax.broadcasted_iota(I32, (1, LANES), 1)
    lane_f = lane.astype(F32)
    h_hi = h2.astype(BF16)
    h_lo = (h2 - h_hi.astype(F32)).astype(BF16)
    hh = jnp.dot(h_hi, wr_ref[...], preferred_element_type=F32)
    lh = jnp.dot(h_lo, wr_ref[:, 0:LANES], preferred_element_type=F32)
    logits = hh[:, 0:LANES] + (hh[:, LANES:2 * LANES] + lh) + br_ref[...]
    logits = jnp.where(lane < N_EXPERTS, logits, -jnp.inf)

    vals, idxs, hots = [], [], []
    for _ in range(TOP_K):
        mx = jnp.max(logits, axis=-1, keepdims=True)
        idx = jnp.min(jnp.where(logits == mx, lane_f, float(LANES)), axis=-1, keepdims=True)
        hot = lane_f == idx
        logits = jnp.where(hot, -jnp.inf, logits)
        vals.append(mx)
        idxs.append(idx)
        hots.append(hot)
    exps = [jnp.exp(v - vals[0]) for v in vals]
    denom = exps[0] + exps[1] + exps[2] + exps[3]

    multi = (hots[0] | hots[1] | hots[2] | hots[3])
    multi_f = jnp.where(multi, 1.0, 0.0)
    r = lax.broadcasted_iota(I32, (tm, tm), 0)
    c = lax.broadcasted_iota(I32, (tm, tm), 1)
    strict = jnp.where(r > c, 1.0, 0.0).astype(BF16)
    before = jnp.dot(strict, multi_f.astype(BF16), preferred_element_type=F32) + cnt_ref[0:1, :]
    cnt_ref[...] = cnt_ref[...] + jnp.sum(multi_f, axis=0, keepdims=True)

    out = jnp.zeros((tm, LANES), F32)
    for kk in range(TOP_K):
        rank = jnp.sum(jnp.where(hots[kk], before, 0.0), axis=-1, keepdims=True)
        out = jnp.where(lane == kk, idxs[kk], out)
        out = jnp.where(lane == TOP_K + kk, exps[kk] / denom, out)
        out = jnp.where(lane == 2 * TOP_K + kk, rank, out)
    route_ref[...] = out


def _outproj(x2, y_ssd, y_mla, wo, g_ffn, wr, br):
    t = x2.shape[0]
    tm = TM_PROJ
    full = lambda shape: pl.BlockSpec(shape, lambda i: (0,) * len(shape))
    rows = lambda width: pl.BlockSpec((tm, width), lambda i: (i, 0))
    return pl.pallas_call(
        _outproj_kernel,
        grid=(t // tm,),
        in_specs=[rows(D_MODEL), rows(512), rows(512), full((1024, D_MODEL)), full((1, D_MODEL)),
                  full((D_MODEL, 2 * LANES)), full((1, LANES))],
        out_specs=[rows(D_MODEL), rows(HALF), rows(LANES), full((8, LANES))],
        out_shape=[jax.ShapeDtypeStruct((t, D_MODEL), F32),
                   jax.ShapeDtypeStruct((t, HALF), I32),
                   jax.ShapeDtypeStruct((t, LANES), F32),
                   jax.ShapeDtypeStruct((8, LANES), F32)],
        compiler_params=pltpu.CompilerParams(
            dimension_semantics=("arbitrary",), vmem_limit_bytes=40 * 1024 * 1024),
        name="outproj_router",
    )(x2, y_ssd, y_mla, wo, g_ffn, wr, br)


def _ffn_kernel(be_ref, bv_ref, nx_ref, sl_ref, xs_ref, wgu_hbm, bgu_ref, wd_hbm, bd_ref, ys_ref,
                wgu_st, wd_st, wgu_bf, wd_bf, sem):
    i = pl.program_id(0)
    e = be_ref[i]
    valid = bv_ref[i]
    slot = sl_ref[i]
    first = ((i == 0) | (e != be_ref[jnp.maximum(i - 1, 0)])) & (valid > 0)

    def weight_copies(expert, dst_slot):
        return (pltpu.make_async_copy(wgu_hbm.at[expert], wgu_st.at[dst_slot], sem.at[0, dst_slot]),
                pltpu.make_async_copy(wd_hbm.at[expert], wd_st.at[dst_slot], sem.at[1, dst_slot]))

    @pl.when(i == 0)
    def _():
        for cp in weight_copies(e, slot):
            cp.start()

    @pl.when(first)
    def _():
        for cp in weight_copies(e, slot):
            cp.wait()

        @pl.when(nx_ref[i] >= 0)
        def _():
            for cp in weight_copies(nx_ref[i], 1 - slot):
                cp.start()

        wgu_bf[...] = wgu_st[slot].astype(BF16)
        wd_bf[...] = wd_st[slot].astype(BF16)

    @pl.when(valid > 0)
    def _():
        row = lax.broadcasted_iota(I32, (MOE_BM, 1), 0)
        x_lo, x_hi = _unpack_rows(jnp.where(row < valid, xs_ref[...], 0))
        gu = (jnp.dot(x_lo.astype(BF16), wgu_bf[0:HALF, :], preferred_element_type=F32)
              + jnp.dot(x_hi.astype(BF16), wgu_bf[HALF:D_MODEL, :], preferred_element_type=F32)
              + bgu_ref[0])
        gate = jnp.minimum(gu[:, :D_FF], SWIGLU_LIMIT)
        up = jnp.clip(gu[:, D_FF:], -SWIGLU_LIMIT, SWIGLU_LIMIT)
        glu = gate * jax.nn.sigmoid(SWIGLU_ALPHA * gate)
        mid = ((up + 1.0) * glu).astype(BF16)
        ys_ref[...] = _pack_rows(jnp.dot(mid, wd_bf[...], preferred_element_type=F32) + bd_ref[0])

    @pl.when(valid == 0)
    def _():
        ys_ref[...] = jnp.zeros_like(ys_ref)


def _expert_ffn(block_e, block_valid, block_next, block_slot, xs, wgu, bgu, wd, bd):
    n_slots = xs.shape[0]
    bm = MOE_BM
    grid_spec = pltpu.PrefetchScalarGridSpec(
        num_scalar_prefetch=4,
        grid=(n_slots // bm,),
        in_specs=[
            pl.BlockSpec((bm, HALF), lambda i, be, bv, nx, sl: (i, 0)),
            pl.BlockSpec(memory_space=pl.ANY),
            pl.BlockSpec((1, 1, 2 * D_FF), lambda i, be, bv, nx, sl: (be[i], 0, 0)),
            pl.BlockSpec(memory_space=pl.ANY),
            pl.BlockSpec((1, 1, D_MODEL), lambda i, be, bv, nx, sl: (be[i], 0, 0)),
        ],
        out_specs=pl.BlockSpec((bm, HALF), lambda i, be, bv, nx, sl: (i, 0)),
        scratch_shapes=[pltpu.VMEM((2, D_MODEL, 2 * D_FF), F32), pltpu.VMEM((2, D_FF, D_MODEL), F32),
                        pltpu.VMEM((D_MODEL, 2 * D_FF), BF16), pltpu.VMEM((D_FF, D_MODEL), BF16),
                        pltpu.SemaphoreType.DMA((2, 2))],
    )
    return pl.pallas_call(
        _ffn_kernel,
        grid_spec=grid_spec,
        out_shape=jax.ShapeDtypeStruct((n_slots, HALF), I32),
        compiler_params=pltpu.CompilerParams(
            dimension_semantics=("arbitrary",), vmem_limit_bytes=56 * 1024 * 1024),
        name="expert_ffn",
    )(block_e, block_valid, block_next, block_slot, xs, wgu, bgu, wd, bd)


def _plan_kernel(sp_ref, route_ref, dest_ref):
    rt = route_ref[...].T
    idx = rt[0:TOP_K, :]
    rank = rt[2 * TOP_K:3 * TOP_K, :]
    start = jnp.zeros(idx.shape, F32)
    for e_i in range(N_EXPERTS):
        start = jnp.where(idx == float(e_i), sp_ref[e_i].astype(F32), start)
    dest_ref[...] = (start + rank).astype(I32)


def _slot_plan(route, start_pad):
    t = route.shape[0]
    tm = TM_PLAN
    grid_spec = pltpu.PrefetchScalarGridSpec(
        num_scalar_prefetch=1,
        grid=(t // tm,),
        in_specs=[pl.BlockSpec((tm, LANES), lambda i, sp: (i, 0))],
        out_specs=pl.BlockSpec((TOP_K, tm), lambda i, sp: (0, i)),
    )
    return pl.pallas_call(
        _plan_kernel,
        grid_spec=grid_spec,
        out_shape=jax.ShapeDtypeStruct((TOP_K, t), I32),
        compiler_params=pltpu.CompilerParams(dimension_semantics=("arbitrary",)),
        name="slot_plan",
    )(start_pad, route)


SC_SCATTER_CHUNK = 32
SC_GATHER_CHUNK = 16


def _sc_workers():
    info = plsc.get_sparse_core_info()
    return info.num_cores, info.num_cores * info.num_subcores


def _sc_scatter_rows(rows, dest_km, n_out):
    t, w = rows.shape
    ch = SC_SCATTER_CHUNK
    n_cores, n_workers = _sc_workers()
    n_chunks = t // n_workers // ch
    mesh = plsc.VectorSubcoreMesh(core_axis_name="c", subcore_axis_name="s")

    @functools.partial(
        pl.kernel, mesh=mesh, out_type=jax.ShapeDtypeStruct((n_out, w), rows.dtype),
        scratch_types=[pltpu.VMEM((TOP_K, n_chunks, ch), I32), pltpu.VMEM((2, ch, w), rows.dtype),
                       pltpu.SemaphoreType.DMA((2,)), pltpu.SemaphoreType.DMA((2,))],
        name="sc_dispatch_scatter")
    def scatter_kernel(rows_hbm, dest_hbm, out_hbm, idx_v, rows_v, sem_in, sem_out):
        wid = lax.axis_index("s") * n_cores + lax.axis_index("c")
        first = wid * n_chunks
        for kk in range(TOP_K):
            pltpu.sync_copy(dest_hbm.at[kk, pl.ds(first, n_chunks)], idx_v.at[kk])

        def load(cc, b):
            return pltpu.make_async_copy(rows_hbm.at[pl.ds((first + cc) * ch, ch)], rows_v.at[b], sem_in.at[b])

        def scatters(cc, b):
            return [pltpu.make_async_copy(rows_v.at[b], out_hbm.at[idx_v.at[kk, cc]], sem_out.at[b])
                    for kk in range(TOP_K)]

        load(0, 0).start()
        load(1, 1).start()

        @pl.loop(0, n_chunks, step=2)
        def _(c):
            for b in range(2):
                load(c + b, b).wait()
                for cp in scatters(c + b, b):
                    cp.start()
            for b in range(2):
                for cp in scatters(c + b, b):
                    cp.wait()

                @pl.when(c + 2 + b < n_chunks)
                def _():
                    load(c + 2 + b, b).start()

    return scatter_kernel(rows, dest_km.reshape(TOP_K, t // ch, ch))


def _sc_gather_rows(table, dest_km):
    _, w = table.shape
    t = dest_km.shape[1]
    ch = SC_GATHER_CHUNK
    n_cores, n_workers = _sc_workers()
    n_chunks = t // n_workers // ch
    mesh = plsc.VectorSubcoreMesh(core_axis_name="c", subcore_axis_name="s")

    @functools.partial(
        pl.kernel, mesh=mesh, out_type=jax.ShapeDtypeStruct((TOP_K, t, w), table.dtype),
        scratch_types=[pltpu.VMEM((TOP_K, n_chunks, ch), I32), pltpu.VMEM((2, TOP_K, ch, w), table.dtype),
                       pltpu.SemaphoreType.DMA((2,)), pltpu.SemaphoreType.DMA((2,))],
        name="sc_combine_gather")
    def gather_kernel(table_hbm, dest_hbm, out_hbm, idx_v, rows_v, sem_in, sem_out):
        wid = lax.axis_index("s") * n_cores + lax.axis_index("c")
        first = wid * n_chunks
        for kk in range(TOP_K):
            pltpu.sync_copy(dest_hbm.at[kk, pl.ds(first, n_chunks)], idx_v.at[kk])

        def gathers(cc, b):
            return [pltpu.make_async_copy(table_hbm.at[idx_v.at[kk, cc]], rows_v.at[b, kk], sem_in.at[b])
                    for kk in range(TOP_K)]

        def stores(cc, b):
            return [pltpu.make_async_copy(rows_v.at[b, kk], out_hbm.at[kk, pl.ds((first + cc) * ch, ch)],
                                          sem_out.at[b]) for kk in range(TOP_K)]

        for b in range(2):
            for cp in gathers(b, b):
                cp.start()

        @pl.loop(0, n_chunks, step=2)
        def _(c):
            for b in range(2):
                for cp in gathers(c + b, b):
                    cp.wait()
                for cp in stores(c + b, b):
                    cp.start()
            for b in range(2):
                for cp in stores(c + b, b):
                    cp.wait()

                @pl.when(c + 2 + b < n_chunks)
                def _():
                    for cp in gathers(c + 2 + b, b):
                        cp.start()

    return gather_kernel(table, dest_km.reshape(TOP_K, t // ch, ch))


def _combine_kernel(x1_ref, yg_ref, route_ref, g_ref, o_ref, *, final_norm):
    moe_lo = jnp.zeros((x1_ref.shape[0], HALF), F32)
    moe_hi = jnp.zeros((x1_ref.shape[0], HALF), F32)
    for kk in range(TOP_K):
        gate = route_ref[:, TOP_K + kk:TOP_K + kk + 1]
        y_lo, y_hi = _unpack_rows(yg_ref[kk])
        moe_lo = moe_lo + gate * y_lo
        moe_hi = moe_hi + gate * y_hi
    acc = x1_ref[...] + jnp.concatenate([moe_lo, moe_hi], axis=1)
    o_ref[...] = _rms(acc) * g_ref[...] if final_norm else acc


def _combine(x1, yg, route, g_final, final_norm):
    t = x1.shape[0]
    tm = TM_COMB
    return pl.pallas_call(
        functools.partial(_combine_kernel, final_norm=final_norm),
        grid=(t // tm,),
        in_specs=[pl.BlockSpec((tm, D_MODEL), lambda i: (i, 0)),
                  pl.BlockSpec((TOP_K, tm, HALF), lambda i: (0, i, 0)),
                  pl.BlockSpec((tm, LANES), lambda i: (i, 0)),
                  pl.BlockSpec((1, D_MODEL), lambda i: (0, 0))],
        out_specs=pl.BlockSpec((tm, D_MODEL), lambda i: (i, 0)),
        out_shape=jax.ShapeDtypeStruct((t, D_MODEL), F32),
        compiler_params=pltpu.CompilerParams(dimension_semantics=("arbitrary",)),
        name="combine",
    )(x1, yg, route, g_final)


def _prep_in_weights(w_in, w_uq, w_ukv):
    w_z = w_in[:, 0:512]
    w_xbc = w_in[:, 512:1536]
    w_dt = w_in[:, 1536:1544]
    w_cq = w_in[:, 1544:1800]
    w_ckv = w_in[:, 1800:1928]
    w_kr = w_in[:, 1928:1960]
    half = MLA_ROPE // 2
    zeros = lambda rows, width: jnp.zeros(rows + (width,), BF16)
    cat = lambda parts: jnp.concatenate([p.astype(BF16) for p in parts], axis=-1)
    d = (D_MODEL,)
    misc1 = [w_dt, zeros(d, MLA_NOPE - SSD_HEADS), w_kr, zeros(d, LANES - MLA_QK)]
    misc2 = [zeros(d, MLA_NOPE), w_kr[:, half:], w_kr[:, :half], zeros(d, LANES - MLA_QK)]
    w1 = cat([w_z, w_xbc, w_cq, w_ckv] + misc1 + misc2)

    wq3 = w_uq.reshape(MLA_Q_RANK, MLA_HEADS, MLA_QK)
    qh = (MLA_Q_RANK, MLA_HEADS)
    main = cat([wq3, zeros(qh, LANES - MLA_QK)])
    swap = cat([zeros(qh, MLA_NOPE), wq3[:, :, MLA_NOPE + half:], wq3[:, :, MLA_NOPE:MLA_NOPE + half],
                zeros(qh, LANES - MLA_QK)])
    wq = jnp.concatenate([main.reshape(MLA_Q_RANK, -1), swap.reshape(MLA_Q_RANK, -1)], axis=1)

    wkv3 = w_ukv.reshape(MLA_KV_RANK, MLA_HEADS, MLA_NOPE + MLA_V)
    kh = (MLA_KV_RANK, MLA_HEADS)
    kpart = cat([wkv3[:, :, :MLA_NOPE], zeros(kh, LANES - MLA_NOPE)])
    vpart = wkv3[:, :, MLA_NOPE:].astype(BF16)
    wkv = jnp.concatenate([kpart.reshape(MLA_KV_RANK, -1), vpart.reshape(MLA_KV_RANK, -1)], axis=1)
    return w1, wq, wkv


def _rope_consts():
    half = MLA_ROPE // 2
    inv_freq = ROPE_THETA ** (-jnp.arange(0, MLA_ROPE, 2, dtype=F32) / MLA_ROPE)
    invf = (jnp.zeros((1, LANES), F32)
            .at[0, MLA_NOPE:MLA_NOPE + half].set(inv_freq)
            .at[0, MLA_NOPE + half:MLA_QK].set(inv_freq))
    sgn = (jnp.zeros((1, LANES), F32)
           .at[0, MLA_NOPE:MLA_NOPE + half].set(-1.0)
           .at[0, MLA_NOPE + half:MLA_QK].set(1.0))
    return invf, sgn


def _pad_lanes(v, fill=0.0):
    return jnp.full((1, LANES), fill, F32).at[0, :v.shape[0]].set(v)


def kernel(x, positions, norm_mix_g, w_in, conv_w, conv_b, dt_bias, a_log, d_skip, ssd_norm_g, q_norm_g, w_uq, kv_norm_g, w_ukv, w_out, norm_ffn_g, w_router, b_router, w_gate_up, b_gate_up, w_down, b_down, norm_final_g):
    bsz, seqlen, d = x.shape
    t = bsz * seqlen
    depth = w_in.shape[0]
    x2 = x.reshape(t, d)
    pos2 = positions.reshape(t, 1).astype(I32)
    invf, sgn = _rope_consts()

    for l in range(depth):
        w1, wq, wkv = _prep_in_weights(w_in[l], w_uq[l], w_ukv[l])
        z, xbc, dtm, q, k, v = _inproj(
            x2, pos2, norm_mix_g[l][None, :], w1, q_norm_g[l][None, :], wq, kv_norm_g[l][None, :], wkv,
            invf, sgn, bsz, seqlen)
        y_ssd = _ssd(z, xbc, dtm, conv_w[l], conv_b[l][None, :], _pad_lanes(dt_bias[l]), _pad_lanes(a_log[l]),
                     jnp.repeat(d_skip[l], SSD_HEAD_DIM)[None, :], ssd_norm_g[l][None, :], bsz, seqlen)
        y_mla = _attention(q, k, v, bsz, seqlen).reshape(t, MLA_WIDTH)

        wr = jnp.zeros((d, LANES), F32).at[:, :N_EXPERTS].set(w_router[l])
        wr_hi = wr.astype(BF16)
        wr_lo = (wr - wr_hi.astype(F32)).astype(BF16)
        x1, h2p, route, cnt = _outproj(x2, y_ssd, y_mla, w_out[l].astype(BF16), norm_ffn_g[l][None, :],
                                       jnp.concatenate([wr_hi, wr_lo], axis=1), _pad_lanes(b_router[l]))

        counts = cnt[0, :N_EXPERTS].astype(I32)
        padded = ((counts + MOE_BM - 1) // MOE_BM) * MOE_BM
        end_pad = jnp.cumsum(padded)
        start_pad = end_pad - padded
        n_slots = t * TOP_K + N_EXPERTS * MOE_BM
        n_blocks = n_slots // MOE_BM
        block_start = jnp.arange(n_blocks, dtype=I32) * MOE_BM
        block_e = jnp.minimum(jnp.sum(block_start[:, None] >= end_pad[None, :], axis=1), N_EXPERTS - 1).astype(I32)
        eids = jnp.arange(N_EXPERTS, dtype=I32)
        block_hot = block_e[:, None] == eids[None, :]
        per_block = lambda table: jnp.sum(jnp.where(block_hot, table[None, :], 0), axis=1).astype(I32)
        block_valid = jnp.clip(per_block(counts) - (block_start - per_block(start_pad)), 0, MOE_BM).astype(I32)
        used = counts > 0
        later_used = jnp.where((eids[None, :] > eids[:, None]) & used[None, :], eids[None, :], N_EXPERTS)
        next_used = jnp.min(later_used, axis=1)
        next_used = jnp.where(next_used < N_EXPERTS, next_used, -1).astype(I32)
        stage_slot = ((jnp.cumsum(used.astype(I32)) - 1) & 1).astype(I32)
        dest_km = _slot_plan(route, start_pad.astype(I32))

        xs = _sc_scatter_rows(h2p, dest_km, n_slots)
        ys = _expert_ffn(block_e, block_valid, per_block(next_used), per_block(stage_slot), xs,
                         w_gate_up[l], b_gate_up[l][:, None, :], w_down[l], b_down[l][:, None, :])
        yg = _sc_gather_rows(ys, dest_km)
        x2 = _combine(x1, yg, route, norm_final_g[None, :], l == depth - 1)
    return x2.reshape(bsz, seqlen, d)
```

```python
import functools

import jax
import jax.numpy as jnp
import numpy as np
from jax import lax
from jax.experimental import pallas as pl
from jax.experimental.pallas import tpu as pltpu
from jax.experimental.pallas import tpu_sc as plsc

F32 = jnp.float32
BF16 = jnp.bfloat16
I32 = jnp.int32
HIGHEST = lax.Precision.HIGHEST

D_MODEL = 1024
EPS = 1e-6
LANES = 128

SSD_HEADS = 8
SSD_HEAD_DIM = 64
SSD_WIDTH = 512
SSD_STATE = 128
SSD_CONV = 4
SSD_CHUNK = 128
SSD_CONV_DIM = 1024
CONV_PAD = 8
SSD_CPS = 2

MLA_HEADS = 8
MLA_Q_RANK = 256
MLA_KV_RANK = 128
MLA_NOPE = 64
MLA_ROPE = 32
MLA_V = 64
MLA_QK = MLA_NOPE + MLA_ROPE
MLA_WIDTH = 512
ROPE_THETA = 10000.0
LOG2_E = 1.4426950408889634

N_EXPERTS = 32
TOP_K = 4
D_FF = 1024
SWIGLU_LIMIT = 7.0
SWIGLU_ALPHA = 1.702

IN_W = 512 + 1024 + 256 + 128 + 128 + 128

TM_PROJ = 512
TQ = 512
TK = 512
ATT_HPS = 4
MOE_BM = 256
TM_COMB = 512
TM_PLAN = 2048
ROUTE_ROWS = 16

NT_DIMS = (((1,), (1,)), ((), ()))


def _rms(x):
    return x * lax.rsqrt(jnp.mean(x * x, axis=-1, keepdims=True) + EPS)


HALF = D_MODEL // 2
HI_MASK = np.int32(-65536)


def _pack_rows(a):
    lo = lax.bitcast_convert_type(a[:, :HALF].astype(BF16).astype(F32), I32)
    hi = lax.bitcast_convert_type(a[:, HALF:].astype(BF16).astype(F32), I32)
    return (hi & HI_MASK) | lax.shift_right_logical(lo, 16)


def _unpack_rows(p):
    lo = lax.bitcast_convert_type(lax.shift_left(p, 16), F32)
    hi = lax.bitcast_convert_type(p & HI_MASK, F32)
    return lo, hi


def _inproj_kernel(x_ref, pos_ref, g_ref, w1_ref, qg_ref, wq_ref, kvg_ref, wkv_ref, invf_ref, sgn_ref,
                   z_ref, xbc_ref, dtm_ref, q_ref, k_ref, v_ref):
    x = x_ref[...]
    h = (_rms(x) * g_ref[...]).astype(BF16)
    p = jnp.dot(h, w1_ref[...], preferred_element_type=F32)
    z_ref[...] = p[:, 0:512]
    xbc_ref[...] = p[:, 512:1536]
    cq = p[:, 1536:1792]
    ckv = p[:, 1792:1920]
    m1 = p[:, 1920:2048]
    m2 = p[:, 2048:2176]
    dtm_ref[...] = m1

    lane = lax.broadcasted_iota(I32, (1, LANES), 1)
    ang = pos_ref[...].astype(F32) * invf_ref[...]
    rope_lane = (lane >= MLA_NOPE) & (lane < MLA_QK)
    cos_t = jnp.where(rope_lane, jnp.cos(ang), 0.0)
    sin_t = jnp.sin(ang) * sgn_ref[...]
    cosq_t = jnp.where(lane < MLA_NOPE, 1.0, cos_t)
    scale = MLA_QK ** -0.5 * LOG2_E

    cqn = (_rms(cq) * qg_ref[...]).astype(BF16)
    qq = jnp.dot(cqn, wq_ref[...], preferred_element_type=F32)
    ckvn = (_rms(ckv) * kvg_ref[...]).astype(BF16)
    kv = jnp.dot(ckvn, wkv_ref[...], preferred_element_type=F32)
    krot = m1 * cos_t + m2 * sin_t
    for h_i in range(MLA_HEADS):
        lo = h_i * LANES
        qm = qq[:, lo:lo + LANES]
        qs = qq[:, 1024 + lo:1024 + lo + LANES]
        q_ref[0, h_i] = ((qm * cosq_t + qs * sin_t) * scale).astype(BF16)
        k_ref[0, h_i] = (kv[:, lo:lo + LANES] + krot).astype(BF16)
    for pr in range(MLA_HEADS // 2):
        vp = kv[:, 1024 + pr * LANES:1024 + (pr + 1) * LANES]
        v_ref[0, 2 * pr] = jnp.where(lane < MLA_V, vp, 1.0).astype(BF16)
        v_ref[0, 2 * pr + 1] = jnp.where(lane >= MLA_V, vp, 1.0).astype(BF16)


def _inproj(x2, pos2, g_mix, w1, qg, wq, kvg, wkv, invf, sgn, bsz, seqlen):
    t = x2.shape[0]
    tm = TM_PROJ
    per_b = seqlen // tm
    full = lambda shape: pl.BlockSpec(shape, lambda i: (0,) * len(shape))
    head_spec = pl.BlockSpec((1, MLA_HEADS, tm, LANES), lambda i: (i // per_b, 0, i % per_b, 0))
    head_shape = jax.ShapeDtypeStruct((bsz, MLA_HEADS, seqlen, LANES), BF16)
    return pl.pallas_call(
        _inproj_kernel,
        grid=(t // tm,),
        in_specs=[
            pl.BlockSpec((tm, D_MODEL), lambda i: (i, 0)),
            pl.BlockSpec((tm, 1), lambda i: (i, 0)),
            full((1, D_MODEL)), full((D_MODEL, IN_W)),
            full((1, MLA_Q_RANK)), full((MLA_Q_RANK, 2048)),
            full((1, MLA_KV_RANK)), full((MLA_KV_RANK, 1536)),
            full((1, LANES)), full((1, LANES)),
        ],
        out_specs=[
            pl.BlockSpec((tm, 512), lambda i: (i, 0)),
            pl.BlockSpec((tm, 1024), lambda i: (i, 0)),
            pl.BlockSpec((tm, LANES), lambda i: (i, 0)),
            head_spec, head_spec, head_spec,
        ],
        out_shape=[
            jax.ShapeDtypeStruct((t, 512), F32),
            jax.ShapeDtypeStruct((t, 1024), F32),
            jax.ShapeDtypeStruct((t, LANES), F32),
            head_shape, head_shape, head_shape,
        ],
        compiler_params=pltpu.CompilerParams(
            dimension_semantics=("arbitrary",), vmem_limit_bytes=56 * 1024 * 1024),
        name="inproj",
    )(x2, pos2, g_mix, w1, qg, wq, kvg, wkv, invf, sgn)


def _ssd_kernel(z_ref, xbc_ref, dtm_ref, cw_ref, cb_ref, dtb_ref, alog_ref, dsk_ref, ng_ref,
                y_ref, ext_ref, st_ref):
    q = SSD_CHUNK
    rows = SSD_CPS * q

    @pl.when(pl.program_id(1) == 0)
    def _():
        ext_ref[0:CONV_PAD, :] = jnp.zeros((CONV_PAD, SSD_CONV_DIM), F32)
        st_ref[...] = jnp.zeros_like(st_ref)

    ext_ref[CONV_PAD:CONV_PAD + rows, :] = xbc_ref[...]

    lane = lax.broadcasted_iota(I32, (1, LANES), 1)
    row = lax.broadcasted_iota(I32, (q, q), 0)
    col = lax.broadcasted_iota(I32, (q, q), 1)
    tril = row >= col
    tril_b = jnp.where(tril, 1.0, 0.0).astype(BF16)
    spread = jnp.where(
        lax.broadcasted_iota(I32, (LANES, SSD_WIDTH), 0)
        == lax.broadcasted_iota(I32, (LANES, SSD_WIDTH), 1) // SSD_HEAD_DIM, 1.0, 0.0).astype(BF16)
    a_neg = -jnp.exp(alog_ref[...])

    def split3(v):
        hi = v.astype(BF16)
        r1 = v - hi.astype(F32)
        mid = r1.astype(BF16)
        return hi, mid, (r1 - mid.astype(F32)).astype(BF16)

    def dot3_right(parts, m):
        return sum(jnp.dot(p, m, preferred_element_type=F32) for p in parts)

    def expand(cols):
        return dot3_right(split3(cols), spread)

    for ci in range(SSD_CPS):
        lo = ci * q
        conv = cb_ref[...]
        for kk in range(SSD_CONV):
            off = CONV_PAD + lo - (SSD_CONV - 1) + kk
            conv = conv + cw_ref[kk:kk + 1, :] * ext_ref[off:off + q, :]
        u = conv * jax.nn.sigmoid(conv)
        xs = u[:, 0:512]
        bm = u[:, 512:768]
        cm = u[:, 768:1024]

        xdt = dtm_ref[lo:lo + q, :] + dtb_ref[...]
        dt = jnp.maximum(xdt, 0.0) + jnp.log1p(jnp.exp(-jnp.abs(xdt)))
        adt = jnp.where(lane < SSD_HEADS, dt * a_neg, 0.0)
        cum_col = sum(jnp.dot(tril_b, p, preferred_element_type=F32) for p in split3(adt))
        cum_row = cum_col.T

        dt_e = expand(dt)
        ac_e = expand(cum_col)
        last_e = ac_e[q - 1:q, :]
        xd = xs * dt_e
        w_end = xd * jnp.exp(last_e - ac_e)
        eac = jnp.exp(ac_e)
        cdec = jnp.exp(last_e)

        y_parts = []
        for g in range(2):
            gl = g * 256
            bg = bm[:, g * SSD_STATE:(g + 1) * SSD_STATE]
            cg = cm[:, g * SSD_STATE:(g + 1) * SSD_STATE].astype(BF16)
            scores = lax.dot_general(cg, bg.astype(BF16), NT_DIMS, preferred_element_type=F32)
            bgt = bg.T.astype(BF16)
            sprev = st_ref[g]
            yoff = jnp.dot(cg, sprev.astype(BF16), preferred_element_type=F32)
            st_ref[g] = sprev * cdec[:, gl:gl + 256] + jnp.dot(
                bgt, w_end[:, gl:gl + 256].astype(BF16), preferred_element_type=F32)
            for pr in range(2):
                pl_lo = gl + pr * LANES
                xdp = xd[:, pl_lo:pl_lo + LANES].astype(BF16)
                res = []
                for jj in range(2):
                    h_i = g * 4 + pr * 2 + jj
                    seg = cum_col[:, h_i:h_i + 1] - cum_row[h_i:h_i + 1, :]
                    dec = jnp.exp(jnp.where(tril, seg, -jnp.inf))
                    res.append(jnp.dot((scores * dec).astype(BF16), xdp, preferred_element_type=F32))
                ydiag = jnp.where(lane < SSD_HEAD_DIM, res[0], res[1])
                y_parts.append(ydiag + yoff[:, pr * LANES:(pr + 1) * LANES] * eac[:, pl_lo:pl_lo + LANES])
        y = jnp.concatenate(y_parts, axis=1) + dsk_ref[...] * xs
        zz = z_ref[lo:lo + q, :]
        y = y * (zz * jax.nn.sigmoid(zz))
        outs = []
        for g in range(2):
            yg = y[:, g * 256:(g + 1) * 256]
            outs.append(_rms(yg))
        y_ref[lo:lo + q, :] = (jnp.concatenate(outs, axis=1) * ng_ref[...]).astype(BF16)

    ext_ref[0:CONV_PAD, :] = ext_ref[rows:rows + CONV_PAD, :]


def _ssd(z, xbc, dtm, cw, cb, dtb, alog, dsk, ng, bsz, seqlen):
    t = z.shape[0]
    q = SSD_CHUNK
    rows = SSD_CPS * q
    nc = seqlen // rows
    full = lambda shape: pl.BlockSpec(shape, lambda b, c: (0,) * len(shape))
    row_spec = lambda width: pl.BlockSpec((rows, width), lambda b, c: (b * nc + c, 0))
    return pl.pallas_call(
        _ssd_kernel,
        grid=(bsz, nc),
        in_specs=[row_spec(512), row_spec(1024), row_spec(LANES),
                  full((SSD_CONV, SSD_CONV_DIM)), full((1, SSD_CONV_DIM)),
                  full((1, LANES)), full((1, LANES)), full((1, SSD_WIDTH)), full((1, SSD_WIDTH))],
        out_specs=row_spec(512),
        out_shape=jax.ShapeDtypeStruct((t, SSD_WIDTH), BF16),
        scratch_shapes=[pltpu.VMEM((rows + CONV_PAD, SSD_CONV_DIM), F32),
                        pltpu.VMEM((2, SSD_STATE, 256), F32)],
        compiler_params=pltpu.CompilerParams(dimension_semantics=("arbitrary", "arbitrary")),
        name="ssd",
    )(z, xbc, dtm, cw, cb, dtb, alog, dsk, ng)


def _attn_kernel(q_ref, k_ref, v_ref, o_ref, acc_ref, s_ref, bmax_ref):
    i = pl.program_id(2)
    lane = lax.broadcasted_iota(I32, (1, LANES), 1)
    acc_ref[...] = jnp.zeros_like(acc_ref)

    def scores(j, slot):
        start = pl.multiple_of(j * TK, TK)
        for hh in range(ATT_HPS):
            kb = k_ref[0, hh, pl.ds(start, TK), :]
            s = lax.dot_general(q_ref[0, hh], kb, NT_DIMS, preferred_element_type=F32)
            s_ref[slot, hh] = s
            bmax_ref[slot, hh] = jnp.broadcast_to(jnp.max(s, axis=-1, keepdims=True), (TQ, LANES))

    def consume(j, slot, m_all, masked):
        start = pl.multiple_of(j * TK, TK)
        new_m = []
        for hh in range(ATT_HPS):
            vb = v_ref[0, hh, pl.ds(start, TK), :]
            s = s_ref[slot, hh]
            if masked:
                r = lax.broadcasted_iota(I32, (TQ, TK), 0)
                c = lax.broadcasted_iota(I32, (TQ, TK), 1)
                s = jnp.where(c <= r, s, -jnp.inf)
                block_max = jnp.broadcast_to(jnp.max(s, axis=-1, keepdims=True), (TQ, LANES))
            else:
                block_max = bmax_ref[slot, hh]
            m_old = m_all[hh]
            m_new = jnp.maximum(m_old, block_max)
            alpha = jnp.exp2(m_old - m_new)
            p = jnp.exp2(s - jnp.tile(m_new, (1, TK // LANES))).astype(BF16)
            acc_ref[hh] = acc_ref[hh] * alpha + jnp.dot(p, vb, preferred_element_type=F32)
            new_m.append(m_new)
        return tuple(new_m)

    def pair(p, m_all):
        scores(2 * p + 1, 1)
        m_all = consume(2 * p, 0, m_all, False)
        scores(2 * p + 2, 0)
        return consume(2 * p + 1, 1, m_all, False)

    def odd_tail(_, m_all):
        scores(i, 1)
        return consume(i - 1, 0, m_all, False)

    m0 = jnp.full((TQ, LANES), -jnp.inf, F32)
    scores(0, 0)
    m_all = lax.fori_loop(0, i // 2, pair, (m0,) * ATT_HPS)
    m_all = lax.fori_loop(0, i & 1, odd_tail, m_all)

    @pl.when((i & 1) == 0)
    def _():
        consume(i, 0, m_all, True)

    @pl.when((i & 1) == 1)
    def _():
        consume(i, 1, m_all, True)

    outs = []
    for pr in range(ATT_HPS // 2):
        a0 = acc_ref[2 * pr]
        a1 = acc_ref[2 * pr + 1]
        o0 = a0 / a0[:, MLA_V:MLA_V + 1]
        o1 = a1 / a1[:, 0:1]
        outs.append(jnp.where(lane < MLA_V, o0, o1))
    o_ref[0] = jnp.concatenate(outs, axis=1).astype(BF16)


def _attention(q, k, v, bsz, seqlen):
    nq = seqlen // TQ
    hps = ATT_HPS
    kv_spec = pl.BlockSpec((1, hps, seqlen, LANES), lambda b, p, i: (b, p, 0, 0))
    return pl.pallas_call(
        _attn_kernel,
        grid=(bsz, MLA_HEADS // hps, nq),
        in_specs=[pl.BlockSpec((1, hps, TQ, LANES), lambda b, p, i: (b, p, i, 0)), kv_spec, kv_spec],
        out_specs=pl.BlockSpec((1, TQ, hps * MLA_V), lambda b, p, i: (b, i, p)),
        out_shape=jax.ShapeDtypeStruct((bsz, seqlen, MLA_WIDTH), BF16),
        scratch_shapes=[pltpu.VMEM((hps, TQ, LANES), F32), pltpu.VMEM((2, hps, TQ, TK), F32),
                        pltpu.VMEM((2, hps, TQ, LANES), F32)],
        compiler_params=pltpu.CompilerParams(
            dimension_semantics=("arbitrary", "arbitrary", "arbitrary"),
            vmem_limit_bytes=56 * 1024 * 1024),
        name="attention",
    )(q, k, v)


def _outproj_kernel(x_ref, ys_ref, ym_ref, wo_ref, g_ref, wr_ref, br_ref,
                    x1_ref, h2_ref, route_ref, cnt_ref):
    tm = x_ref.shape[0]

    @pl.when(pl.program_id(0) == 0)
    def _():
        cnt_ref[...] = jnp.zeros_like(cnt_ref)

    mix = (jnp.dot(ys_ref[...], wo_ref[0:512, :], preferred_element_type=F32)
           + jnp.dot(ym_ref[...], wo_ref[512:1024, :], preferred_element_type=F32))
    x1 = x_ref[...] + mix
    x1_ref[...] = x1
    h2 = _rms(x1) * g_ref[...]
    h2_ref[...] = _pack_rows(h2)

    h_hi = h2.astype(BF16)
    h_lo = (h2 - h_hi.astype(F32)).astype(BF16)
    hh = jnp.dot(h_hi, wr_ref[...], preferred_element_type=F32)
    lh = jnp.dot(h_lo, wr_ref[:, 0:LANES], preferred_element_type=F32)
    logits = hh[:, 0:LANES] + (hh[:, LANES:2 * LANES] + lh) + br_ref[...]
    lt = logits.T[0:N_EXPERTS, :]
    eid = lax.broadcasted_iota(I32, (N_EXPERTS, 1), 0).astype(F32)

    vals, idxs, hots = [], [], []
    for _ in range(TOP_K):
        mx = jnp.max(lt, axis=0, keepdims=True)
        idx = jnp.min(jnp.where(lt == mx, eid, float(N_EXPERTS)), axis=0, keepdims=True)
        hot = eid == idx
        lt = jnp.where(hot, -jnp.inf, lt)
        vals.append(mx)
        idxs.append(idx)
        hots.append(hot)
    exps = [jnp.exp(v - vals[0]) for v in vals]
    denom = exps[0] + exps[1] + exps[2] + exps[3]

    multi_f = jnp.where(hots[0] | hots[1] | hots[2] | hots[3], 1.0, 0.0)
    r = lax.broadcasted_iota(I32, (tm, tm), 0)
    c = lax.broadcasted_iota(I32, (tm, tm), 1)
    earlier = jnp.where(r < c, 1.0, 0.0).astype(BF16)
    before = jnp.dot(multi_f.astype(BF16), earlier, preferred_element_type=F32) + cnt_ref[:, 0:1]
    cnt_ref[...] = cnt_ref[...] + jnp.sum(multi_f, axis=1, keepdims=True)

    ranks = [jnp.sum(jnp.where(hots[kk], before, 0.0), axis=0, keepdims=True) for kk in range(TOP_K)]
    gates = [e / denom for e in exps]
    route_ref[...] = jnp.concatenate(idxs + gates + ranks + [jnp.zeros((ROUTE_ROWS - 3 * TOP_K, tm), F32)], axis=0)


def _outproj(x2, y_ssd, y_mla, wo, g_ffn, wr, br):
    t = x2.shape[0]
    tm = TM_PROJ
    full = lambda shape: pl.BlockSpec(shape, lambda i: (0,) * len(shape))
    rows = lambda width: pl.BlockSpec((tm, width), lambda i: (i, 0))
    return pl.pallas_call(
        _outproj_kernel,
        grid=(t // tm,),
        in_specs=[rows(D_MODEL), rows(512), rows(512), full((1024, D_MODEL)), full((1, D_MODEL)),
                  full((D_MODEL, 2 * LANES)), full((1, LANES))],
        out_specs=[rows(D_MODEL), rows(HALF), pl.BlockSpec((ROUTE_ROWS, tm), lambda i: (0, i)),
                   full((N_EXPERTS, LANES))],
        out_shape=[jax.ShapeDtypeStruct((t, D_MODEL), F32),
                   jax.ShapeDtypeStruct((t, HALF), I32),
                   jax.ShapeDtypeStruct((ROUTE_ROWS, t), F32),
                   jax.ShapeDtypeStruct((N_EXPERTS, LANES), F32)],
        compiler_params=pltpu.CompilerParams(
            dimension_semantics=("arbitrary",), vmem_limit_bytes=40 * 1024 * 1024),
        name="outproj_router",
    )(x2, y_ssd, y_mla, wo, g_ffn, wr, br)


def _ffn_kernel(be_ref, bv_ref, nx_ref, sl_ref, xs_ref, wgu_hbm, bgu_ref, wd_hbm, bd_ref, ys_ref,
                wgu_st, wd_st, wgu_bf, wd_bf, sem):
    i = pl.program_id(0)
    e = be_ref[i]
    valid = bv_ref[i]
    slot = sl_ref[i]
    first = ((i == 0) | (e != be_ref[jnp.maximum(i - 1, 0)])) & (valid > 0)

    def weight_copies(expert, dst_slot):
        return (pltpu.make_async_copy(wgu_hbm.at[expert], wgu_st.at[dst_slot], sem.at[0, dst_slot]),
                pltpu.make_async_copy(wd_hbm.at[expert], wd_st.at[dst_slot], sem.at[1, dst_slot]))

    @pl.when(i == 0)
    def _():
        for cp in weight_copies(e, slot):
            cp.start()

    @pl.when(first)
    def _():
        for cp in weight_copies(e, slot):
            cp.wait()

        @pl.when(nx_ref[i] >= 0)
        def _():
            for cp in weight_copies(nx_ref[i], 1 - slot):
                cp.start(priority=1)

        wgu_bf[...] = wgu_st[slot].astype(BF16)
        wd_bf[...] = wd_st[slot].astype(BF16)

    @pl.when(valid > 0)
    def _():
        row = lax.broadcasted_iota(I32, (MOE_BM, 1), 0)
        x_lo, x_hi = _unpack_rows(jnp.where(row < valid, xs_ref[...], 0))
        gu = (jnp.dot(x_lo.astype(BF16), wgu_bf[0:HALF, :], preferred_element_type=F32)
              + jnp.dot(x_hi.astype(BF16), wgu_bf[HALF:D_MODEL, :], preferred_element_type=F32)
              + bgu_ref[0])
        gate = jnp.minimum(gu[:, :D_FF], SWIGLU_LIMIT)
        up = jnp.clip(gu[:, D_FF:], -SWIGLU_LIMIT, SWIGLU_LIMIT)
        glu = gate * jax.nn.sigmoid(SWIGLU_ALPHA * gate)
        mid = ((up + 1.0) * glu).astype(BF16)
        ys_ref[...] = _pack_rows(jnp.dot(mid, wd_bf[...], preferred_element_type=F32) + bd_ref[0])

    @pl.when(valid == 0)
    def _():
        ys_ref[...] = jnp.zeros_like(ys_ref)


def _expert_ffn(block_e, block_valid, block_next, block_slot, xs, wgu, bgu, wd, bd):
    n_slots = xs.shape[0]
    bm = MOE_BM
    grid_spec = pltpu.PrefetchScalarGridSpec(
        num_scalar_prefetch=4,
        grid=(n_slots // bm,),
        in_specs=[
            pl.BlockSpec((bm, HALF), lambda i, be, bv, nx, sl: (i, 0)),
            pl.BlockSpec(memory_space=pl.ANY),
            pl.BlockSpec((1, 1, 2 * D_FF), lambda i, be, bv, nx, sl: (be[i], 0, 0)),
            pl.BlockSpec(memory_space=pl.ANY),
            pl.BlockSpec((1, 1, D_MODEL), lambda i, be, bv, nx, sl: (be[i], 0, 0)),
        ],
        out_specs=pl.BlockSpec((bm, HALF), lambda i, be, bv, nx, sl: (i, 0)),
        scratch_shapes=[pltpu.VMEM((2, D_MODEL, 2 * D_FF), F32), pltpu.VMEM((2, D_FF, D_MODEL), F32),
                        pltpu.VMEM((D_MODEL, 2 * D_FF), BF16), pltpu.VMEM((D_FF, D_MODEL), BF16),
                        pltpu.SemaphoreType.DMA((2, 2))],
    )
    return pl.pallas_call(
        _ffn_kernel,
        grid_spec=grid_spec,
        out_shape=jax.ShapeDtypeStruct((n_slots, HALF), I32),
        compiler_params=pltpu.CompilerParams(
            dimension_semantics=("arbitrary",), vmem_limit_bytes=56 * 1024 * 1024),
        name="expert_ffn",
    )(block_e, block_valid, block_next, block_slot, xs, wgu, bgu, wd, bd)


def _plan_kernel(sp_ref, route_ref, dest_ref):
    idx = route_ref[0:TOP_K, :]
    rank = route_ref[2 * TOP_K:3 * TOP_K, :]
    start = jnp.zeros(idx.shape, F32)
    for e_i in range(N_EXPERTS):
        start = jnp.where(idx == float(e_i), sp_ref[e_i].astype(F32), start)
    dest_ref[...] = (start + rank).astype(I32)


def _slot_plan(route, start_pad):
    t = route.shape[1]
    tm = TM_PLAN
    grid_spec = pltpu.PrefetchScalarGridSpec(
        num_scalar_prefetch=1,
        grid=(t // tm,),
        in_specs=[pl.BlockSpec((ROUTE_ROWS, tm), lambda i, sp: (0, i))],
        out_specs=pl.BlockSpec((TOP_K, tm), lambda i, sp: (0, i)),
    )
    return pl.pallas_call(
        _plan_kernel,
        grid_spec=grid_spec,
        out_shape=jax.ShapeDtypeStruct((TOP_K, t), I32),
        compiler_params=pltpu.CompilerParams(dimension_semantics=("arbitrary",)),
        name="slot_plan",
    )(start_pad, route)


SC_SCATTER_CHUNK = 32
SC_GATHER_CHUNK = 16


def _sc_workers():
    info = plsc.get_sparse_core_info()
    return info.num_cores, info.num_cores * info.num_subcores


def _sc_scatter_rows(rows, dest_km, n_out):
    t, w = rows.shape
    ch = SC_SCATTER_CHUNK
    n_cores, n_workers = _sc_workers()
    n_chunks = t // n_workers // ch
    mesh = plsc.VectorSubcoreMesh(core_axis_name="c", subcore_axis_name="s")

    @functools.partial(
        pl.kernel, mesh=mesh, out_type=jax.ShapeDtypeStruct((n_out, w), rows.dtype),
        scratch_types=[pltpu.VMEM((TOP_K, n_chunks, ch), I32), pltpu.VMEM((2, ch, w), rows.dtype),
                       pltpu.SemaphoreType.DMA((2,)), pltpu.SemaphoreType.DMA((2,))],
        name="sc_dispatch_scatter")
    def scatter_kernel(rows_hbm, dest_hbm, out_hbm, idx_v, rows_v, sem_in, sem_out):
        wid = lax.axis_index("s") * n_cores + lax.axis_index("c")
        first = wid * n_chunks
        for kk in range(TOP_K):
            pltpu.sync_copy(dest_hbm.at[kk, pl.ds(first, n_chunks)], idx_v.at[kk])

        def load(cc, b):
            return pltpu.make_async_copy(rows_hbm.at[pl.ds((first + cc) * ch, ch)], rows_v.at[b], sem_in.at[b])

        def scatters(cc, b):
            return [pltpu.make_async_copy(rows_v.at[b], out_hbm.at[idx_v.at[kk, cc]], sem_out.at[b])
                    for kk in range(TOP_K)]

        load(0, 0).start()
        load(1, 1).start()

        @pl.loop(0, n_chunks, step=2)
        def _(c):
            for b in range(2):
                load(c + b, b).wait()
                for cp in scatters(c + b, b):
                    cp.start()
            for b in range(2):
                for cp in scatters(c + b, b):
                    cp.wait()

                @pl.when(c + 2 + b < n_chunks)
                def _():
                    load(c + 2 + b, b).start()

    return scatter_kernel(rows, dest_km.reshape(TOP_K, t // ch, ch))


def _sc_gather_rows(table, dest_km):
    _, w = table.shape
    t = dest_km.shape[1]
    ch = SC_GATHER_CHUNK
    n_cores, n_workers = _sc_workers()
    n_chunks = t // n_workers // ch
    mesh = plsc.VectorSubcoreMesh(core_axis_name="c", subcore_axis_name="s")

    @functools.partial(
        pl.kernel, mesh=mesh, out_type=jax.ShapeDtypeStruct((TOP_K, t, w), table.dtype),
        scratch_types=[pltpu.VMEM((TOP_K, n_chunks, ch), I32), pltpu.VMEM((2, TOP_K, ch, w), table.dtype),
                       pltpu.SemaphoreType.DMA((2,)), pltpu.SemaphoreType.DMA((2,))],
        name="sc_combine_gather")
    def gather_kernel(table_hbm, dest_hbm, out_hbm, idx_v, rows_v, sem_in, sem_out):
        wid = lax.axis_index("s") * n_cores + lax.axis_index("c")
        first = wid * n_chunks
        for kk in range(TOP_K):
            pltpu.sync_copy(dest_hbm.at[kk, pl.ds(first, n_chunks)], idx_v.at[kk])

        def gathers(cc, b):
            return [pltpu.make_async_copy(table_hbm.at[idx_v.at[kk, cc]], rows_v.at[b, kk], sem_in.at[b])
                    for kk in range(TOP_K)]

        def stores(cc, b):
            return [pltpu.make_async_copy(rows_v.at[b, kk], out_hbm.at[kk, pl.ds((first + cc) * ch, ch)],
                                          sem_out.at[b]) for kk in range(TOP_K)]

        for b in range(2):
            for cp in gathers(b, b):
                cp.start()

        @pl.loop(0, n_chunks, step=2)
        def _(c):
            for b in range(2):
                for cp in gathers(c + b, b):
                    cp.wait()
                for cp in stores(c + b, b):
                    cp.start()
            for b in range(2):
                for cp in stores(c + b, b):
                    cp.wait()

                @pl.when(c + 2 + b < n_chunks)
                def _():
                    for cp in gathers(c + 2 + b, b):
                        cp.start()

    return gather_kernel(table, dest_km.reshape(TOP_K, t // ch, ch))


def _combine_kernel(x1_ref, yg_ref, route_ref, g_ref, o_ref, *, final_norm):
    tm = x1_ref.shape[0]
    moe_lo = jnp.zeros((tm, HALF), F32)
    moe_hi = jnp.zeros((tm, HALF), F32)
    route_t = jnp.concatenate([route_ref[...], jnp.zeros((LANES - ROUTE_ROWS, tm), F32)], axis=0).T
    for kk in range(TOP_K):
        gate = route_t[:, TOP_K + kk:TOP_K + kk + 1]
        y_lo, y_hi = _unpack_rows(yg_ref[kk])
        moe_lo = moe_lo + gate * y_lo
        moe_hi = moe_hi + gate * y_hi
    acc = x1_ref[...] + jnp.concatenate([moe_lo, moe_hi], axis=1)
    o_ref[...] = _rms(acc) * g_ref[...] if final_norm else acc


def _combine(x1, yg, route, g_final, final_norm):
    t = x1.shape[0]
    tm = TM_COMB
    return pl.pallas_call(
        functools.partial(_combine_kernel, final_norm=final_norm),
        grid=(t // tm,),
        in_specs=[pl.BlockSpec((tm, D_MODEL), lambda i: (i, 0)),
                  pl.BlockSpec((TOP_K, tm, HALF), lambda i: (0, i, 0)),
                  pl.BlockSpec((ROUTE_ROWS, tm), lambda i: (0, i)),
                  pl.BlockSpec((1, D_MODEL), lambda i: (0, 0))],
        out_specs=pl.BlockSpec((tm, D_MODEL), lambda i: (i, 0)),
        out_shape=jax.ShapeDtypeStruct((t, D_MODEL), F32),
        compiler_params=pltpu.CompilerParams(dimension_semantics=("arbitrary",)),
        name="combine",
    )(x1, yg, route, g_final)


def _prep_in_weights(w_in, w_uq, w_ukv):
    w_z = w_in[:, 0:512]
    w_xbc = w_in[:, 512:1536]
    w_dt = w_in[:, 1536:1544]
    w_cq = w_in[:, 1544:1800]
    w_ckv = w_in[:, 1800:1928]
    w_kr = w_in[:, 1928:1960]
    half = MLA_ROPE // 2
    zeros = lambda rows, width: jnp.zeros(rows + (width,), BF16)
    cat = lambda parts: jnp.concatenate([p.astype(BF16) for p in parts], axis=-1)
    d = (D_MODEL,)
    misc1 = [w_dt, zeros(d, MLA_NOPE - SSD_HEADS), w_kr, zeros(d, LANES - MLA_QK)]
    misc2 = [zeros(d, MLA_NOPE), w_kr[:, half:], w_kr[:, :half], zeros(d, LANES - MLA_QK)]
    w1 = cat([w_z, w_xbc, w_cq, w_ckv] + misc1 + misc2)

    wq3 = w_uq.reshape(MLA_Q_RANK, MLA_HEADS, MLA_QK)
    qh = (MLA_Q_RANK, MLA_HEADS)
    main = cat([wq3, zeros(qh, LANES - MLA_QK)])
    swap = cat([zeros(qh, MLA_NOPE), wq3[:, :, MLA_NOPE + half:], wq3[:, :, MLA_NOPE:MLA_NOPE + half],
                zeros(qh, LANES - MLA_QK)])
    wq = jnp.concatenate([main.reshape(MLA_Q_RANK, -1), swap.reshape(MLA_Q_RANK, -1)], axis=1)

    wkv3 = w_ukv.reshape(MLA_KV_RANK, MLA_HEADS, MLA_NOPE + MLA_V)
    kh = (MLA_KV_RANK, MLA_HEADS)
    kpart = cat([wkv3[:, :, :MLA_NOPE], zeros(kh, LANES - MLA_NOPE)])
    vpart = wkv3[:, :, MLA_NOPE:].astype(BF16)
    wkv = jnp.concatenate([kpart.reshape(MLA_KV_RANK, -1), vpart.reshape(MLA_KV_RANK, -1)], axis=1)
    return w1, wq, wkv


def _rope_consts():
    half = MLA_ROPE // 2
    inv_freq = ROPE_THETA ** (-jnp.arange(0, MLA_ROPE, 2, dtype=F32) / MLA_ROPE)
    invf = (jnp.zeros((1, LANES), F32)
            .at[0, MLA_NOPE:MLA_NOPE + half].set(inv_freq)
            .at[0, MLA_NOPE + half:MLA_QK].set(inv_freq))
    sgn = (jnp.zeros((1, LANES), F32)
           .at[0, MLA_NOPE:MLA_NOPE + half].set(-1.0)
           .at[0, MLA_NOPE + half:MLA_QK].set(1.0))
    return invf, sgn


def _pad_lanes(v, fill=0.0):
    return jnp.full((1, LANES), fill, F32).at[0, :v.shape[0]].set(v)


def kernel(x, positions, norm_mix_g, w_in, conv_w, conv_b, dt_bias, a_log, d_skip, ssd_norm_g, q_norm_g, w_uq, kv_norm_g, w_ukv, w_out, norm_ffn_g, w_router, b_router, w_gate_up, b_gate_up, w_down, b_down, norm_final_g):
    bsz, seqlen, d = x.shape
    t = bsz * seqlen
    depth = w_in.shape[0]
    x2 = x.reshape(t, d)
    pos2 = positions.reshape(t, 1).astype(I32)
    invf, sgn = _rope_consts()

    for l in range(depth):
        w1, wq, wkv = _prep_in_weights(w_in[l], w_uq[l], w_ukv[l])
        z, xbc, dtm, q, k, v = _inproj(
            x2, pos2, norm_mix_g[l][None, :], w1, q_norm_g[l][None, :], wq, kv_norm_g[l][None, :], wkv,
            invf, sgn, bsz, seqlen)
        y_ssd = _ssd(z, xbc, dtm, conv_w[l], conv_b[l][None, :], _pad_lanes(dt_bias[l]), _pad_lanes(a_log[l]),
                     jnp.repeat(d_skip[l], SSD_HEAD_DIM)[None, :], ssd_norm_g[l][None, :], bsz, seqlen)
        y_mla = _attention(q, k, v, bsz, seqlen).reshape(t, MLA_WIDTH)

        wr = jnp.zeros((d, LANES), F32).at[:, :N_EXPERTS].set(w_router[l])
        wr_hi = wr.astype(BF16)
        wr_lo = (wr - wr_hi.astype(F32)).astype(BF16)
        x1, h2p, route, cnt = _outproj(x2, y_ssd, y_mla, w_out[l].astype(BF16), norm_ffn_g[l][None, :],
                                       jnp.concatenate([wr_hi, wr_lo], axis=1), _pad_lanes(b_router[l]))

        counts = cnt[:, 0].astype(I32)
        padded = ((counts + MOE_BM - 1) // MOE_BM) * MOE_BM
        end_pad = jnp.cumsum(padded)
        start_pad = end_pad - padded
        n_slots = t * TOP_K + N_EXPERTS * MOE_BM
        n_blocks = n_slots // MOE_BM
        block_start = jnp.arange(n_blocks, dtype=I32) * MOE_BM
        block_e = jnp.minimum(jnp.sum(block_start[:, None] >= end_pad[None, :], axis=1), N_EXPERTS - 1).astype(I32)
        eids = jnp.arange(N_EXPERTS, dtype=I32)
        block_hot = block_e[:, None] == eids[None, :]
        per_block = lambda table: jnp.sum(jnp.where(block_hot, table[None, :], 0), axis=1).astype(I32)
        block_valid = jnp.clip(per_block(counts) - (block_start - per_block(start_pad)), 0, MOE_BM).astype(I32)
        used = counts > 0
        later_used = jnp.where((eids[None, :] > eids[:, None]) & used[None, :], eids[None, :], N_EXPERTS)
        next_used = jnp.min(later_used, axis=1)
        next_used = jnp.where(next_used < N_EXPERTS, next_used, -1).astype(I32)
        stage_slot = ((jnp.cumsum(used.astype(I32)) - 1) & 1).astype(I32)
        dest_km = _slot_plan(route, start_pad.astype(I32))

        xs = _sc_scatter_rows(h2p, dest_km, n_slots)
        ys = _expert_ffn(block_e, block_valid, per_block(next_used), per_block(stage_slot), xs,
                         w_gate_up[l], b_gate_up[l][:, None, :], w_down[l], b_down[l][:, None, :])
        yg = _sc_gather_rows(ys, dest_km)
        x2 = _combine(x1, yg, route, norm_final_g[None, :], l == depth - 1)
    return x2.reshape(bsz, seqlen, d)
```

```python
import functools

import jax
import jax.numpy as jnp
import numpy as np
from jax import lax
from jax.experimental import pallas as pl
from jax.experimental.pallas import tpu as pltpu
from jax.experimental.pallas import tpu_sc as plsc

F32 = jnp.float32
BF16 = jnp.bfloat16
I32 = jnp.int32
HIGHEST = lax.Precision.HIGHEST

D_MODEL = 1024
EPS = 1e-6
LANES = 128

SSD_HEADS = 8
SSD_HEAD_DIM = 64
SSD_WIDTH = 512
SSD_STATE = 128
SSD_CONV = 4
SSD_CHUNK = 128
SSD_CONV_DIM = 1024
CONV_PAD = 8
SSD_CPS = 2

MLA_HEADS = 8
MLA_Q_RANK = 256
MLA_KV_RANK = 128
MLA_NOPE = 64
MLA_ROPE = 32
MLA_V = 64
MLA_QK = MLA_NOPE + MLA_ROPE
MLA_WIDTH = 512
ROPE_THETA = 10000.0
LOG2_E = 1.4426950408889634

N_EXPERTS = 32
TOP_K = 4
D_FF = 1024
SWIGLU_LIMIT = 7.0
SWIGLU_ALPHA = 1.702

IN_W = 512 + 1024 + 256 + 128 + 128 + 128

TM_PROJ = 512
TQ = 512
TK = 512
ATT_HPS = 4
MOE_BM = 256
MOE_STEP = 1024
TM_COMB = 512
TM_PLAN = 2048
ROUTE_ROWS = 16

NT_DIMS = (((1,), (1,)), ((), ()))


def _rms(x):
    return x * lax.rsqrt(jnp.mean(x * x, axis=-1, keepdims=True) + EPS)


HALF = D_MODEL // 2
HI_MASK = np.int32(-65536)


def _pack_rows(a):
    lo = lax.bitcast_convert_type(a[:, :HALF].astype(BF16).astype(F32), I32)
    hi = lax.bitcast_convert_type(a[:, HALF:].astype(BF16).astype(F32), I32)
    return (hi & HI_MASK) | lax.shift_right_logical(lo, 16)


def _unpack_rows(p):
    lo = lax.bitcast_convert_type(lax.shift_left(p, 16), F32)
    hi = lax.bitcast_convert_type(p & HI_MASK, F32)
    return lo, hi


def _inproj_kernel(x_ref, pos_ref, g_ref, w1_ref, qg_ref, wq_ref, kvg_ref, wkv_ref, invf_ref, sgn_ref,
                   z_ref, xbc_ref, dtm_ref, q_ref, k_ref, v_ref):
    x = x_ref[...]
    h = (_rms(x) * g_ref[...]).astype(BF16)
    p = jnp.dot(h, w1_ref[...], preferred_element_type=F32)
    z_ref[...] = p[:, 0:512]
    xbc_ref[...] = p[:, 512:1536]
    cq = p[:, 1536:1792]
    ckv = p[:, 1792:1920]
    m1 = p[:, 1920:2048]
    m2 = p[:, 2048:2176]
    dtm_ref[...] = m1

    lane = lax.broadcasted_iota(I32, (1, LANES), 1)
    ang = pos_ref[...].astype(F32) * invf_ref[...]
    rope_lane = (lane >= MLA_NOPE) & (lane < MLA_QK)
    cos_t = jnp.where(rope_lane, jnp.cos(ang), 0.0)
    sin_t = jnp.sin(ang) * sgn_ref[...]
    cosq_t = jnp.where(lane < MLA_NOPE, 1.0, cos_t)
    scale = MLA_QK ** -0.5 * LOG2_E

    cqn = (_rms(cq) * qg_ref[...]).astype(BF16)
    qq = jnp.dot(cqn, wq_ref[...], preferred_element_type=F32)
    ckvn = (_rms(ckv) * kvg_ref[...]).astype(BF16)
    kv = jnp.dot(ckvn, wkv_ref[...], preferred_element_type=F32)
    krot = m1 * cos_t + m2 * sin_t
    for h_i in range(MLA_HEADS):
        lo = h_i * LANES
        qm = qq[:, lo:lo + LANES]
        qs = qq[:, 1024 + lo:1024 + lo + LANES]
        q_ref[0, h_i] = ((qm * cosq_t + qs * sin_t) * scale).astype(BF16)
        k_ref[0, h_i] = (kv[:, lo:lo + LANES] + krot).astype(BF16)
    for pr in range(MLA_HEADS // 2):
        vp = kv[:, 1024 + pr * LANES:1024 + (pr + 1) * LANES]
        v_ref[0, 2 * pr] = jnp.where(lane < MLA_V, vp, 1.0).astype(BF16)
        v_ref[0, 2 * pr + 1] = jnp.where(lane >= MLA_V, vp, 1.0).astype(BF16)


def _inproj(x2, pos2, g_mix, w1, qg, wq, kvg, wkv, invf, sgn, bsz, seqlen):
    t = x2.shape[0]
    tm = TM_PROJ
    per_b = seqlen // tm
    full = lambda shape: pl.BlockSpec(shape, lambda i: (0,) * len(shape))
    head_spec = pl.BlockSpec((1, MLA_HEADS, tm, LANES), lambda i: (i // per_b, 0, i % per_b, 0))
    head_shape = jax.ShapeDtypeStruct((bsz, MLA_HEADS, seqlen, LANES), BF16)
    return pl.pallas_call(
        _inproj_kernel,
        grid=(t // tm,),
        in_specs=[
            pl.BlockSpec((tm, D_MODEL), lambda i: (i, 0)),
            pl.BlockSpec((tm, 1), lambda i: (i, 0)),
            full((1, D_MODEL)), full((D_MODEL, IN_W)),
            full((1, MLA_Q_RANK)), full((MLA_Q_RANK, 2048)),
            full((1, MLA_KV_RANK)), full((MLA_KV_RANK, 1536)),
            full((1, LANES)), full((1, LANES)),
        ],
        out_specs=[
            pl.BlockSpec((tm, 512), lambda i: (i, 0)),
            pl.BlockSpec((tm, 1024), lambda i: (i, 0)),
            pl.BlockSpec((tm, LANES), lambda i: (i, 0)),
            head_spec, head_spec, head_spec,
        ],
        out_shape=[
            jax.ShapeDtypeStruct((t, 512), F32),
            jax.ShapeDtypeStruct((t, 1024), F32),
            jax.ShapeDtypeStruct((t, LANES), F32),
            head_shape, head_shape, head_shape,
        ],
        compiler_params=pltpu.CompilerParams(
            dimension_semantics=("arbitrary",), vmem_limit_bytes=56 * 1024 * 1024),
        name="inproj",
    )(x2, pos2, g_mix, w1, qg, wq, kvg, wkv, invf, sgn)


def _ssd_kernel(z_ref, xbc_ref, dtm_ref, cw_ref, cb_ref, dtb_ref, alog_ref, dsk_ref, ng_ref,
                y_ref, ext_ref, st_ref):
    q = SSD_CHUNK
    rows = SSD_CPS * q

    @pl.when(pl.program_id(1) == 0)
    def _():
        ext_ref[0:CONV_PAD, :] = jnp.zeros((CONV_PAD, SSD_CONV_DIM), F32)
        st_ref[...] = jnp.zeros_like(st_ref)

    ext_ref[CONV_PAD:CONV_PAD + rows, :] = xbc_ref[...]

    lane = lax.broadcasted_iota(I32, (1, LANES), 1)
    row = lax.broadcasted_iota(I32, (q, q), 0)
    col = lax.broadcasted_iota(I32, (q, q), 1)
    tril = row >= col
    tril_b = jnp.where(tril, 1.0, 0.0).astype(BF16)
    spread = jnp.where(
        lax.broadcasted_iota(I32, (LANES, SSD_WIDTH), 0)
        == lax.broadcasted_iota(I32, (LANES, SSD_WIDTH), 1) // SSD_HEAD_DIM, 1.0, 0.0).astype(BF16)
    a_neg = -jnp.exp(alog_ref[...])

    def split3(v):
        hi = v.astype(BF16)
        r1 = v - hi.astype(F32)
        mid = r1.astype(BF16)
        return hi, mid, (r1 - mid.astype(F32)).astype(BF16)

    def dot3_right(parts, m):
        return sum(jnp.dot(p, m, preferred_element_type=F32) for p in parts)

    def expand(cols):
        return dot3_right(split3(cols), spread)

    for ci in range(SSD_CPS):
        lo = ci * q
        conv = cb_ref[...]
        for kk in range(SSD_CONV):
            off = CONV_PAD + lo - (SSD_CONV - 1) + kk
            conv = conv + cw_ref[kk:kk + 1, :] * ext_ref[off:off + q, :]
        u = conv * jax.nn.sigmoid(conv)
        xs = u[:, 0:512]
        bm = u[:, 512:768]
        cm = u[:, 768:1024]

        xdt = dtm_ref[lo:lo + q, :] + dtb_ref[...]
        dt = jnp.maximum(xdt, 0.0) + jnp.log1p(jnp.exp(-jnp.abs(xdt)))
        adt = jnp.where(lane < SSD_HEADS, dt * a_neg, 0.0)
        cum_col = sum(jnp.dot(tril_b, p, preferred_element_type=F32) for p in split3(adt))
        cum_row = cum_col.T

        dt_e = expand(dt)
        ac_e = expand(cum_col)
        last_e = ac_e[q - 1:q, :]
        xd = xs * dt_e
        w_end = xd * jnp.exp(last_e - ac_e)
        eac = jnp.exp(ac_e)
        cdec = jnp.exp(last_e)

        y_parts = []
        for g in range(2):
            gl = g * 256
            bg = bm[:, g * SSD_STATE:(g + 1) * SSD_STATE]
            cg = cm[:, g * SSD_STATE:(g + 1) * SSD_STATE].astype(BF16)
            scores = lax.dot_general(cg, bg.astype(BF16), NT_DIMS, preferred_element_type=F32)
            bgt = bg.T.astype(BF16)
            sprev = st_ref[g]
            yoff = jnp.dot(cg, sprev.astype(BF16), preferred_element_type=F32)
            st_ref[g] = sprev * cdec[:, gl:gl + 256] + jnp.dot(
                bgt, w_end[:, gl:gl + 256].astype(BF16), preferred_element_type=F32)
            for pr in range(2):
                pl_lo = gl + pr * LANES
                xdp = xd[:, pl_lo:pl_lo + LANES].astype(BF16)
                res = []
                for jj in range(2):
                    h_i = g * 4 + pr * 2 + jj
                    seg = cum_col[:, h_i:h_i + 1] - cum_row[h_i:h_i + 1, :]
                    dec = jnp.exp(jnp.where(tril, seg, -jnp.inf))
                    res.append(jnp.dot((scores * dec).astype(BF16), xdp, preferred_element_type=F32))
                ydiag = jnp.where(lane < SSD_HEAD_DIM, res[0], res[1])
                y_parts.append(ydiag + yoff[:, pr * LANES:(pr + 1) * LANES] * eac[:, pl_lo:pl_lo + LANES])
        y = jnp.concatenate(y_parts, axis=1) + dsk_ref[...] * xs
        zz = z_ref[lo:lo + q, :]
        y = y * (zz * jax.nn.sigmoid(zz))
        outs = []
        for g in range(2):
            yg = y[:, g * 256:(g + 1) * 256]
            outs.append(_rms(yg))
        y_ref[lo:lo + q, :] = (jnp.concatenate(outs, axis=1) * ng_ref[...]).astype(BF16)

    ext_ref[0:CONV_PAD, :] = ext_ref[rows:rows + CONV_PAD, :]


def _ssd(z, xbc, dtm, cw, cb, dtb, alog, dsk, ng, bsz, seqlen):
    t = z.shape[0]
    q = SSD_CHUNK
    rows = SSD_CPS * q
    nc = seqlen // rows
    full = lambda shape: pl.BlockSpec(shape, lambda b, c: (0,) * len(shape))
    row_spec = lambda width: pl.BlockSpec((rows, width), lambda b, c: (b * nc + c, 0))
    return pl.pallas_call(
        _ssd_kernel,
        grid=(bsz, nc),
        in_specs=[row_spec(512), row_spec(1024), row_spec(LANES),
                  full((SSD_CONV, SSD_CONV_DIM)), full((1, SSD_CONV_DIM)),
                  full((1, LANES)), full((1, LANES)), full((1, SSD_WIDTH)), full((1, SSD_WIDTH))],
        out_specs=row_spec(512),
        out_shape=jax.ShapeDtypeStruct((t, SSD_WIDTH), BF16),
        scratch_shapes=[pltpu.VMEM((rows + CONV_PAD, SSD_CONV_DIM), F32),
                        pltpu.VMEM((2, SSD_STATE, 256), F32)],
        compiler_params=pltpu.CompilerParams(dimension_semantics=("arbitrary", "arbitrary")),
        name="ssd",
    )(z, xbc, dtm, cw, cb, dtb, alog, dsk, ng)


def _attn_kernel(q_ref, k_ref, v_ref, o_ref, acc_ref, s_ref, bmax_ref):
    i = pl.program_id(2)
    lane = lax.broadcasted_iota(I32, (1, LANES), 1)
    acc_ref[...] = jnp.zeros_like(acc_ref)

    def scores(j, slot):
        start = pl.multiple_of(j * TK, TK)
        for hh in range(ATT_HPS):
            kb = k_ref[0, hh, pl.ds(start, TK), :]
            s = lax.dot_general(q_ref[0, hh], kb, NT_DIMS, preferred_element_type=F32)
            s_ref[slot, hh] = s
            bmax_ref[slot, hh] = jnp.broadcast_to(jnp.max(s, axis=-1, keepdims=True), (TQ, LANES))

    def consume(j, slot, m_all, masked):
        start = pl.multiple_of(j * TK, TK)
        new_m = []
        for hh in range(ATT_HPS):
            vb = v_ref[0, hh, pl.ds(start, TK), :]
            s = s_ref[slot, hh]
            if masked:
                r = lax.broadcasted_iota(I32, (TQ, TK), 0)
                c = lax.broadcasted_iota(I32, (TQ, TK), 1)
                s = jnp.where(c <= r, s, -jnp.inf)
                block_max = jnp.broadcast_to(jnp.max(s, axis=-1, keepdims=True), (TQ, LANES))
            else:
                block_max = bmax_ref[slot, hh]
            m_old = m_all[hh]
            m_new = jnp.maximum(m_old, block_max)
            alpha = jnp.exp2(m_old - m_new)
            p = jnp.exp2(s - jnp.tile(m_new, (1, TK // LANES))).astype(BF16)
            acc_ref[hh] = acc_ref[hh] * alpha + jnp.dot(p, vb, preferred_element_type=F32)
            new_m.append(m_new)
        return tuple(new_m)

    def pair(p, m_all):
        scores(2 * p + 1, 1)
        m_all = consume(2 * p, 0, m_all, False)
        scores(2 * p + 2, 0)
        return consume(2 * p + 1, 1, m_all, False)

    def odd_tail(_, m_all):
        scores(i, 1)
        return consume(i - 1, 0, m_all, False)

    m0 = jnp.full((TQ, LANES), -jnp.inf, F32)
    scores(0, 0)
    m_all = lax.fori_loop(0, i // 2, pair, (m0,) * ATT_HPS)
    m_all = lax.fori_loop(0, i & 1, odd_tail, m_all)

    @pl.when((i & 1) == 0)
    def _():
        consume(i, 0, m_all, True)

    @pl.when((i & 1) == 1)
    def _():
        consume(i, 1, m_all, True)

    outs = []
    for pr in range(ATT_HPS // 2):
        a0 = acc_ref[2 * pr]
        a1 = acc_ref[2 * pr + 1]
        o0 = a0 / a0[:, MLA_V:MLA_V + 1]
        o1 = a1 / a1[:, 0:1]
        outs.append(jnp.where(lane < MLA_V, o0, o1))
    o_ref[0] = jnp.concatenate(outs, axis=1).astype(BF16)


def _attention(q, k, v, bsz, seqlen):
    nq = seqlen // TQ
    hps = ATT_HPS
    kv_spec = pl.BlockSpec((1, hps, seqlen, LANES), lambda b, p, i: (b, p, 0, 0))
    return pl.pallas_call(
        _attn_kernel,
        grid=(bsz, MLA_HEADS // hps, nq),
        in_specs=[pl.BlockSpec((1, hps, TQ, LANES), lambda b, p, i: (b, p, i, 0)), kv_spec, kv_spec],
        out_specs=pl.BlockSpec((1, TQ, hps * MLA_V), lambda b, p, i: (b, i, p)),
        out_shape=jax.ShapeDtypeStruct((bsz, seqlen, MLA_WIDTH), BF16),
        scratch_shapes=[pltpu.VMEM((hps, TQ, LANES), F32), pltpu.VMEM((2, hps, TQ, TK), F32),
                        pltpu.VMEM((2, hps, TQ, LANES), F32)],
        compiler_params=pltpu.CompilerParams(
            dimension_semantics=("arbitrary", "arbitrary", "arbitrary"),
            vmem_limit_bytes=56 * 1024 * 1024),
        name="attention",
    )(q, k, v)


def _outproj_kernel(x_ref, ys_ref, ym_ref, wo_ref, g_ref, wr_ref, br_ref,
                    x1_ref, h2_ref, route_ref, cnt_ref):
    tm = x_ref.shape[0]

    @pl.when(pl.program_id(0) == 0)
    def _():
        cnt_ref[...] = jnp.zeros_like(cnt_ref)

    mix = (jnp.dot(ys_ref[...], wo_ref[0:512, :], preferred_element_type=F32)
           + jnp.dot(ym_ref[...], wo_ref[512:1024, :], preferred_element_type=F32))
    x1 = x_ref[...] + mix
    x1_ref[...] = x1
    h2 = _rms(x1) * g_ref[...]
    h2_ref[...] = _pack_rows(h2)

    h_hi = h2.astype(BF16)
    h_lo = (h2 - h_hi.astype(F32)).astype(BF16)
    hh = jnp.dot(h_hi, wr_ref[...], preferred_element_type=F32)
    lh = jnp.dot(h_lo, wr_ref[:, 0:LANES], preferred_element_type=F32)
    logits = hh[:, 0:LANES] + (hh[:, LANES:2 * LANES] + lh) + br_ref[...]
    lt = logits.T[0:N_EXPERTS, :]
    eid = lax.broadcasted_iota(I32, (N_EXPERTS, 1), 0).astype(F32)

    vals, idxs, hots = [], [], []
    for _ in range(TOP_K):
        mx = jnp.max(lt, axis=0, keepdims=True)
        idx = jnp.min(jnp.where(lt == mx, eid, float(N_EXPERTS)), axis=0, keepdims=True)
        hot = eid == idx
        lt = jnp.where(hot, -jnp.inf, lt)
        vals.append(mx)
        idxs.append(idx)
        hots.append(hot)
    exps = [jnp.exp(v - vals[0]) for v in vals]
    denom = exps[0] + exps[1] + exps[2] + exps[3]

    multi_f = jnp.where(hots[0] | hots[1] | hots[2] | hots[3], 1.0, 0.0)
    r = lax.broadcasted_iota(I32, (tm, tm), 0)
    c = lax.broadcasted_iota(I32, (tm, tm), 1)
    earlier = jnp.where(r < c, 1.0, 0.0).astype(BF16)
    before = jnp.dot(multi_f.astype(BF16), earlier, preferred_element_type=F32) + cnt_ref[:, 0:1]
    cnt_ref[...] = cnt_ref[...] + jnp.sum(multi_f, axis=1, keepdims=True)

    ranks = [jnp.sum(jnp.where(hots[kk], before, 0.0), axis=0, keepdims=True) for kk in range(TOP_K)]
    gates = [e / denom for e in exps]
    route_ref[...] = jnp.concatenate(idxs + gates + ranks + [jnp.zeros((ROUTE_ROWS - 3 * TOP_K, tm), F32)], axis=0)


def _outproj(x2, y_ssd, y_mla, wo, g_ffn, wr, br):
    t = x2.shape[0]
    tm = TM_PROJ
    full = lambda shape: pl.BlockSpec(shape, lambda i: (0,) * len(shape))
    rows = lambda width: pl.BlockSpec((tm, width), lambda i: (i, 0))
    return pl.pallas_call(
        _outproj_kernel,
        grid=(t // tm,),
        in_specs=[rows(D_MODEL), rows(512), rows(512), full((1024, D_MODEL)), full((1, D_MODEL)),
                  full((D_MODEL, 2 * LANES)), full((1, LANES))],
        out_specs=[rows(D_MODEL), rows(HALF), pl.BlockSpec((ROUTE_ROWS, tm), lambda i: (0, i)),
                   full((N_EXPERTS, LANES))],
        out_shape=[jax.ShapeDtypeStruct((t, D_MODEL), F32),
                   jax.ShapeDtypeStruct((t, HALF), I32),
                   jax.ShapeDtypeStruct((ROUTE_ROWS, t), F32),
                   jax.ShapeDtypeStruct((N_EXPERTS, LANES), F32)],
        compiler_params=pltpu.CompilerParams(
            dimension_semantics=("arbitrary",), vmem_limit_bytes=40 * 1024 * 1024),
        name="outproj_router",
    )(x2, y_ssd, y_mla, wo, g_ffn, wr, br)


def _ffn_kernel(be_ref, bv_ref, nx_ref, sl_ref, xs_ref, wgu_hbm, bgu_ref, wd_hbm, bd_ref, ys_ref,
                wgu_st, wd_st, wgu_bf, wd_bf, sem):
    i = pl.program_id(0)
    e = be_ref[i]
    valid = bv_ref[i]
    slot = sl_ref[i]
    first = ((i == 0) | (e != be_ref[jnp.maximum(i - 1, 0)])) & (valid > 0)

    def weight_copies(expert, dst_slot):
        return (pltpu.make_async_copy(wgu_hbm.at[expert], wgu_st.at[dst_slot], sem.at[0, dst_slot]),
                pltpu.make_async_copy(wd_hbm.at[expert], wd_st.at[dst_slot], sem.at[1, dst_slot]))

    @pl.when(i == 0)
    def _():
        for cp in weight_copies(e, slot):
            cp.start()

    @pl.when(first)
    def _():
        for cp in weight_copies(e, slot):
            cp.wait()

        @pl.when(nx_ref[i] >= 0)
        def _():
            for cp in weight_copies(nx_ref[i], 1 - slot):
                cp.start(priority=1)

        wgu_bf[...] = wgu_st[slot].astype(BF16)
        wd_bf[...] = wd_st[slot].astype(BF16)

    for r0 in range(0, MOE_STEP, MOE_BM):
        rows = pl.ds(r0, MOE_BM)

        @pl.when(valid > r0)
        def _():
            row = r0 + lax.broadcasted_iota(I32, (MOE_BM, 1), 0)
            x_lo, x_hi = _unpack_rows(jnp.where(row < valid, xs_ref[rows, :], 0))
            gu = (jnp.dot(x_lo.astype(BF16), wgu_bf[0:HALF, :], preferred_element_type=F32)
                  + jnp.dot(x_hi.astype(BF16), wgu_bf[HALF:D_MODEL, :], preferred_element_type=F32)
                  + bgu_ref[0])
            gate = jnp.minimum(gu[:, :D_FF], SWIGLU_LIMIT)
            up = jnp.clip(gu[:, D_FF:], -SWIGLU_LIMIT, SWIGLU_LIMIT)
            glu = gate * jax.nn.sigmoid(SWIGLU_ALPHA * gate)
            mid = ((up + 1.0) * glu).astype(BF16)
            ys_ref[rows, :] = _pack_rows(jnp.dot(mid, wd_bf[...], preferred_element_type=F32) + bd_ref[0])

        @pl.when(valid <= r0)
        def _():
            ys_ref[rows, :] = jnp.zeros((MOE_BM, HALF), I32)


def _expert_ffn(block_e, block_valid, block_next, block_slot, xs, wgu, bgu, wd, bd):
    n_slots = xs.shape[0]
    bm = MOE_STEP
    grid_spec = pltpu.PrefetchScalarGridSpec(
        num_scalar_prefetch=4,
        grid=(n_slots // bm,),
        in_specs=[
            pl.BlockSpec((bm, HALF), lambda i, be, bv, nx, sl: (i, 0)),
            pl.BlockSpec(memory_space=pl.ANY),
            pl.BlockSpec((1, 1, 2 * D_FF), lambda i, be, bv, nx, sl: (be[i], 0, 0)),
            pl.BlockSpec(memory_space=pl.ANY),
            pl.BlockSpec((1, 1, D_MODEL), lambda i, be, bv, nx, sl: (be[i], 0, 0)),
        ],
        out_specs=pl.BlockSpec((bm, HALF), lambda i, be, bv, nx, sl: (i, 0)),
        scratch_shapes=[pltpu.VMEM((2, D_MODEL, 2 * D_FF), F32), pltpu.VMEM((2, D_FF, D_MODEL), F32),
                        pltpu.VMEM((D_MODEL, 2 * D_FF), BF16), pltpu.VMEM((D_FF, D_MODEL), BF16),
                        pltpu.SemaphoreType.DMA((2, 2))],
    )
    return pl.pallas_call(
        _ffn_kernel,
        grid_spec=grid_spec,
        out_shape=jax.ShapeDtypeStruct((n_slots, HALF), I32),
        compiler_params=pltpu.CompilerParams(
            dimension_semantics=("arbitrary",), vmem_limit_bytes=56 * 1024 * 1024),
        name="expert_ffn",
    )(block_e, block_valid, block_next, block_slot, xs, wgu, bgu, wd, bd)


def _plan_kernel(sp_ref, route_ref, dest_ref):
    idx = route_ref[0:TOP_K, :]
    rank = route_ref[2 * TOP_K:3 * TOP_K, :]
    start = jnp.zeros(idx.shape, F32)
    for e_i in range(N_EXPERTS):
        start = jnp.where(idx == float(e_i), sp_ref[e_i].astype(F32), start)
    dest_ref[...] = (start + rank).astype(I32)


def _slot_plan(route, start_pad):
    t = route.shape[1]
    tm = TM_PLAN
    grid_spec = pltpu.PrefetchScalarGridSpec(
        num_scalar_prefetch=1,
        grid=(t // tm,),
        in_specs=[pl.BlockSpec((ROUTE_ROWS, tm), lambda i, sp: (0, i))],
        out_specs=pl.BlockSpec((TOP_K, tm), lambda i, sp: (0, i)),
    )
    return pl.pallas_call(
        _plan_kernel,
        grid_spec=grid_spec,
        out_shape=jax.ShapeDtypeStruct((TOP_K, t), I32),
        compiler_params=pltpu.CompilerParams(dimension_semantics=("arbitrary",)),
        name="slot_plan",
    )(start_pad, route)


SC_SCATTER_CHUNK = 32
SC_GATHER_CHUNK = 16


def _sc_workers():
    info = plsc.get_sparse_core_info()
    return info.num_cores, info.num_cores * info.num_subcores


def _sc_scatter_rows(rows, dest_km, n_out):
    t, w = rows.shape
    ch = SC_SCATTER_CHUNK
    n_cores, n_workers = _sc_workers()
    n_chunks = t // n_workers // ch
    mesh = plsc.VectorSubcoreMesh(core_axis_name="c", subcore_axis_name="s")

    @functools.partial(
        pl.kernel, mesh=mesh, out_type=jax.ShapeDtypeStruct((n_out, w), rows.dtype),
        scratch_types=[pltpu.VMEM((TOP_K, n_chunks, ch), I32), pltpu.VMEM((2, ch, w), rows.dtype),
                       pltpu.SemaphoreType.DMA((2,)), pltpu.SemaphoreType.DMA((2,))],
        name="sc_dispatch_scatter")
    def scatter_kernel(rows_hbm, dest_hbm, out_hbm, idx_v, rows_v, sem_in, sem_out):
        wid = lax.axis_index("s") * n_cores + lax.axis_index("c")
        first = wid * n_chunks
        for kk in range(TOP_K):
            pltpu.sync_copy(dest_hbm.at[kk, pl.ds(first, n_chunks)], idx_v.at[kk])

        def load(cc, b):
            return pltpu.make_async_copy(rows_hbm.at[pl.ds((first + cc) * ch, ch)], rows_v.at[b], sem_in.at[b])

        def scatters(cc, b):
            return [pltpu.make_async_copy(rows_v.at[b], out_hbm.at[idx_v.at[kk, cc]], sem_out.at[b])
                    for kk in range(TOP_K)]

        load(0, 0).start()
        load(1, 1).start()

        @pl.loop(0, n_chunks, step=2)
        def _(c):
            for b in range(2):
                load(c + b, b).wait()
                for cp in scatters(c + b, b):
                    cp.start()
            for b in range(2):
                for cp in scatters(c + b, b):
                    cp.wait()

                @pl.when(c + 2 + b < n_chunks)
                def _():
                    load(c + 2 + b, b).start()

    return scatter_kernel(rows, dest_km.reshape(TOP_K, t // ch, ch))


def _sc_gather_rows(table, dest_km):
    _, w = table.shape
    t = dest_km.shape[1]
    ch = SC_GATHER_CHUNK
    n_cores, n_workers = _sc_workers()
    n_chunks = t // n_workers // ch
    mesh = plsc.VectorSubcoreMesh(core_axis_name="c", subcore_axis_name="s")

    @functools.partial(
        pl.kernel, mesh=mesh, out_type=jax.ShapeDtypeStruct((TOP_K, t, w), table.dtype),
        scratch_types=[pltpu.VMEM((TOP_K, n_chunks, ch), I32), pltpu.VMEM((2, TOP_K, ch, w), table.dtype),
                       pltpu.SemaphoreType.DMA((2,)), pltpu.SemaphoreType.DMA((2,))],
        name="sc_combine_gather")
    def gather_kernel(table_hbm, dest_hbm, out_hbm, idx_v, rows_v, sem_in, sem_out):
        wid = lax.axis_index("s") * n_cores + lax.axis_index("c")
        first = wid * n_chunks
        for kk in range(TOP_K):
            pltpu.sync_copy(dest_hbm.at[kk, pl.ds(first, n_chunks)], idx_v.at[kk])

        def gathers(cc, b):
            return [pltpu.make_async_copy(table_hbm.at[idx_v.at[kk, cc]], rows_v.at[b, kk], sem_in.at[b])
                    for kk in range(TOP_K)]

        def stores(cc, b):
            return [pltpu.make_async_copy(rows_v.at[b, kk], out_hbm.at[kk, pl.ds((first + cc) * ch, ch)],
                                          sem_out.at[b]) for kk in range(TOP_K)]

        for b in range(2):
            for cp in gathers(b, b):
                cp.start()

        @pl.loop(0, n_chunks, step=2)
        def _(c):
            for b in range(2):
                for cp in gathers(c + b, b):
                    cp.wait()
                for cp in stores(c + b, b):
                    cp.start()
            for b in range(2):
                for cp in stores(c + b, b):
                    cp.wait()

                @pl.when(c + 2 + b < n_chunks)
                def _():
                    for cp in gathers(c + 2 + b, b):
                        cp.start()

    return gather_kernel(table, dest_km.reshape(TOP_K, t // ch, ch))


def _combine_kernel(x1_ref, yg_ref, route_ref, g_ref, o_ref, *, final_norm):
    tm = x1_ref.shape[0]
    moe_lo = jnp.zeros((tm, HALF), F32)
    moe_hi = jnp.zeros((tm, HALF), F32)
    route_t = jnp.concatenate([route_ref[...], jnp.zeros((LANES - ROUTE_ROWS, tm), F32)], axis=0).T
    for kk in range(TOP_K):
        gate = route_t[:, TOP_K + kk:TOP_K + kk + 1]
        y_lo, y_hi = _unpack_rows(yg_ref[kk])
        moe_lo = moe_lo + gate * y_lo
        moe_hi = moe_hi + gate * y_hi
    acc = x1_ref[...] + jnp.concatenate([moe_lo, moe_hi], axis=1)
    o_ref[...] = _rms(acc) * g_ref[...] if final_norm else acc


def _combine(x1, yg, route, g_final, final_norm):
    t = x1.shape[0]
    tm = TM_COMB
    return pl.pallas_call(
        functools.partial(_combine_kernel, final_norm=final_norm),
        grid=(t // tm,),
        in_specs=[pl.BlockSpec((tm, D_MODEL), lambda i: (i, 0)),
                  pl.BlockSpec((TOP_K, tm, HALF), lambda i: (0, i, 0)),
                  pl.BlockSpec((ROUTE_ROWS, tm), lambda i: (0, i)),
                  pl.BlockSpec((1, D_MODEL), lambda i: (0, 0))],
        out_specs=pl.BlockSpec((tm, D_MODEL), lambda i: (i, 0)),
        out_shape=jax.ShapeDtypeStruct((t, D_MODEL), F32),
        compiler_params=pltpu.CompilerParams(dimension_semantics=("arbitrary",)),
        name="combine",
    )(x1, yg, route, g_final)


def _prep_in_weights(w_in, w_uq, w_ukv):
    w_z = w_in[:, 0:512]
    w_xbc = w_in[:, 512:1536]
    w_dt = w_in[:, 1536:1544]
    w_cq = w_in[:, 1544:1800]
    w_ckv = w_in[:, 1800:1928]
    w_kr = w_in[:, 1928:1960]
    half = MLA_ROPE // 2
    zeros = lambda rows, width: jnp.zeros(rows + (width,), BF16)
    cat = lambda parts: jnp.concatenate([p.astype(BF16) for p in parts], axis=-1)
    d = (D_MODEL,)
    misc1 = [w_dt, zeros(d, MLA_NOPE - SSD_HEADS), w_kr, zeros(d, LANES - MLA_QK)]
    misc2 = [zeros(d, MLA_NOPE), w_kr[:, half:], w_kr[:, :half], zeros(d, LANES - MLA_QK)]
    w1 = cat([w_z, w_xbc, w_cq, w_ckv] + misc1 + misc2)

    wq3 = w_uq.reshape(MLA_Q_RANK, MLA_HEADS, MLA_QK)
    qh = (MLA_Q_RANK, MLA_HEADS)
    main = cat([wq3, zeros(qh, LANES - MLA_QK)])
    swap = cat([zeros(qh, MLA_NOPE), wq3[:, :, MLA_NOPE + half:], wq3[:, :, MLA_NOPE:MLA_NOPE + half],
                zeros(qh, LANES - MLA_QK)])
    wq = jnp.concatenate([main.reshape(MLA_Q_RANK, -1), swap.reshape(MLA_Q_RANK, -1)], axis=1)

    wkv3 = w_ukv.reshape(MLA_KV_RANK, MLA_HEADS, MLA_NOPE + MLA_V)
    kh = (MLA_KV_RANK, MLA_HEADS)
    kpart = cat([wkv3[:, :, :MLA_NOPE], zeros(kh, LANES - MLA_NOPE)])
    vpart = wkv3[:, :, MLA_NOPE:].astype(BF16)
    wkv = jnp.concatenate([kpart.reshape(MLA_KV_RANK, -1), vpart.reshape(MLA_KV_RANK, -1)], axis=1)
    return w1, wq, wkv


def _rope_consts():
    half = MLA_ROPE // 2
    inv_freq = ROPE_THETA ** (-jnp.arange(0, MLA_ROPE, 2, dtype=F32) / MLA_ROPE)
    invf = (jnp.zeros((1, LANES), F32)
            .at[0, MLA_NOPE:MLA_NOPE + half].set(inv_freq)
            .at[0, MLA_NOPE + half:MLA_QK].set(inv_freq))
    sgn = (jnp.zeros((1, LANES), F32)
           .at[0, MLA_NOPE:MLA_NOPE + half].set(-1.0)
           .at[0, MLA_NOPE + half:MLA_QK].set(1.0))
    return invf, sgn


def _pad_lanes(v, fill=0.0):
    return jnp.full((1, LANES), fill, F32).at[0, :v.shape[0]].set(v)


def kernel(x, positions, norm_mix_g, w_in, conv_w, conv_b, dt_bias, a_log, d_skip, ssd_norm_g, q_norm_g, w_uq, kv_norm_g, w_ukv, w_out, norm_ffn_g, w_router, b_router, w_gate_up, b_gate_up, w_down, b_down, norm_final_g):
    bsz, seqlen, d = x.shape
    t = bsz * seqlen
    depth = w_in.shape[0]
    x2 = x.reshape(t, d)
    pos2 = positions.reshape(t, 1).astype(I32)
    invf, sgn = _rope_consts()

    for l in range(depth):
        w1, wq, wkv = _prep_in_weights(w_in[l], w_uq[l], w_ukv[l])
        z, xbc, dtm, q, k, v = _inproj(
            x2, pos2, norm_mix_g[l][None, :], w1, q_norm_g[l][None, :], wq, kv_norm_g[l][None, :], wkv,
            invf, sgn, bsz, seqlen)
        y_ssd = _ssd(z, xbc, dtm, conv_w[l], conv_b[l][None, :], _pad_lanes(dt_bias[l]), _pad_lanes(a_log[l]),
                     jnp.repeat(d_skip[l], SSD_HEAD_DIM)[None, :], ssd_norm_g[l][None, :], bsz, seqlen)
        y_mla = _attention(q, k, v, bsz, seqlen).reshape(t, MLA_WIDTH)

        wr = jnp.zeros((d, LANES), F32).at[:, :N_EXPERTS].set(w_router[l])
        wr_hi = wr.astype(BF16)
        wr_lo = (wr - wr_hi.astype(F32)).astype(BF16)
        x1, h2p, route, cnt = _outproj(x2, y_ssd, y_mla, w_out[l].astype(BF16), norm_ffn_g[l][None, :],
                                       jnp.concatenate([wr_hi, wr_lo], axis=1), _pad_lanes(b_router[l]))

        counts = cnt[:, 0].astype(I32)
        padded = ((counts + MOE_STEP - 1) // MOE_STEP) * MOE_STEP
        end_pad = jnp.cumsum(padded)
        start_pad = end_pad - padded
        n_slots = t * TOP_K + N_EXPERTS * MOE_STEP
        n_blocks = n_slots // MOE_STEP
        block_start = jnp.arange(n_blocks, dtype=I32) * MOE_STEP
        block_e = jnp.minimum(jnp.sum(block_start[:, None] >= end_pad[None, :], axis=1), N_EXPERTS - 1).astype(I32)
        eids = jnp.arange(N_EXPERTS, dtype=I32)
        block_hot = block_e[:, None] == eids[None, :]
        per_block = lambda table: jnp.sum(jnp.where(block_hot, table[None, :], 0), axis=1).astype(I32)
        block_valid = jnp.clip(per_block(counts) - (block_start - per_block(start_pad)), 0, MOE_STEP).astype(I32)
        used = counts > 0
        later_used = jnp.where((eids[None, :] > eids[:, None]) & used[None, :], eids[None, :], N_EXPERTS)
        next_used = jnp.min(later_used, axis=1)
        next_used = jnp.where(next_used < N_EXPERTS, next_used, -1).astype(I32)
        stage_slot = ((jnp.cumsum(used.astype(I32)) - 1) & 1).astype(I32)
        dest_km = _slot_plan(route, start_pad.astype(I32))

        xs = _sc_scatter_rows(h2p, dest_km, n_slots)
        ys = _expert_ffn(block_e, block_valid, per_block(next_used), per_block(stage_slot), xs,
                         w_gate_up[l], b_gate_up[l][:, None, :], w_down[l], b_down[l][:, None, :])
        yg = _sc_gather_rows(ys, dest_km)
        x2 = _combine(x1, yg, route, norm_final_g[None, :], l == depth - 1)
    return x2.reshape(bsz, seqlen, d)
```

```python
import functools

import jax
import jax.numpy as jnp
import numpy as np
from jax import lax
from jax.experimental import pallas as pl
from jax.experimental.pallas import tpu as pltpu
from jax.experimental.pallas import tpu_sc as plsc

F32 = jnp.float32
BF16 = jnp.bfloat16
I32 = jnp.int32
HIGHEST = lax.Precision.HIGHEST

D_MODEL = 1024
EPS = 1e-6
LANES = 128

SSD_HEADS = 8
SSD_HEAD_DIM = 64
SSD_WIDTH = 512
SSD_STATE = 128
SSD_CONV = 4
SSD_CHUNK = 128
SSD_CONV_DIM = 1024
CONV_PAD = 8
SSD_CPS = 2

MLA_HEADS = 8
MLA_Q_RANK = 256
MLA_KV_RANK = 128
MLA_NOPE = 64
MLA_ROPE = 32
MLA_V = 64
MLA_QK = MLA_NOPE + MLA_ROPE
MLA_WIDTH = 512
ROPE_THETA = 10000.0
LOG2_E = 1.4426950408889634

N_EXPERTS = 32
TOP_K = 4
D_FF = 1024
SWIGLU_LIMIT = 7.0
SWIGLU_ALPHA = 1.702

IN_W = 512 + 1024 + 256 + 128 + 128 + 128

TM_PROJ = 512
TQ = 512
TK = 512
ATT_HPS = 4
MOE_BM = 256
MOE_STEP = 1024
TM_COMB = 512
TM_PLAN = 2048
ROUTE_ROWS = 16

NT_DIMS = (((1,), (1,)), ((), ()))


def _rms(x):
    return x * lax.rsqrt(jnp.mean(x * x, axis=-1, keepdims=True) + EPS)


HALF = D_MODEL // 2
HI_MASK = np.int32(-65536)


def _pack_rows(a):
    lo = lax.bitcast_convert_type(a[:, :HALF].astype(BF16).astype(F32), I32)
    hi = lax.bitcast_convert_type(a[:, HALF:].astype(BF16).astype(F32), I32)
    return (hi & HI_MASK) | lax.shift_right_logical(lo, 16)


def _unpack_rows(p):
    lo = lax.bitcast_convert_type(lax.shift_left(p, 16), F32)
    hi = lax.bitcast_convert_type(p & HI_MASK, F32)
    return lo, hi


def _inproj_kernel(x_ref, pos_ref, g_ref, w1_ref, qg_ref, wq_ref, kvg_ref, wkv_ref, invf_ref,
                   z_ref, xbc_ref, dtm_ref, q_ref, k_ref, v_ref):
    x = x_ref[...]
    h = (_rms(x) * g_ref[...]).astype(BF16)
    p = jnp.dot(h, w1_ref[...], preferred_element_type=F32)
    z_ref[...] = p[:, 0:512]
    xbc_ref[...] = p[:, 512:1536]
    cq = p[:, 1536:1792]
    ckv = p[:, 1792:1920]
    m1 = p[:, 1920:2048]
    m2 = p[:, 2048:2176]
    dtm_ref[...] = m1

    lane = lax.broadcasted_iota(I32, (1, LANES), 1)
    tm = x.shape[0]
    ang = invf_ref[...] * pos_ref[0].astype(F32)
    cos_c = jnp.cos(ang)
    sin_c = jnp.sin(ang)
    z_lo = jnp.zeros((MLA_NOPE, tm), F32)
    z_hi = jnp.zeros((LANES - MLA_QK, tm), F32)
    cos_t = jnp.concatenate([z_lo, cos_c, cos_c, z_hi], axis=0).T
    sin_t = jnp.concatenate([z_lo, -sin_c, sin_c, z_hi], axis=0).T
    cosq_t = jnp.where(lane < MLA_NOPE, 1.0, cos_t)
    scale = MLA_QK ** -0.5 * LOG2_E

    cqn = (_rms(cq) * qg_ref[...]).astype(BF16)
    qq = jnp.dot(cqn, wq_ref[...], preferred_element_type=F32)
    ckvn = (_rms(ckv) * kvg_ref[...]).astype(BF16)
    kv = jnp.dot(ckvn, wkv_ref[...], preferred_element_type=F32)
    krot = m1 * cos_t + m2 * sin_t
    for h_i in range(MLA_HEADS):
        lo = h_i * LANES
        qm = qq[:, lo:lo + LANES]
        qs = qq[:, 1024 + lo:1024 + lo + LANES]
        q_ref[0, h_i] = ((qm * cosq_t + qs * sin_t) * scale).astype(BF16)
        k_ref[0, h_i] = (kv[:, lo:lo + LANES] + krot).astype(BF16)
    for pr in range(MLA_HEADS // 2):
        vp = kv[:, 1024 + pr * LANES:1024 + (pr + 1) * LANES]
        v_ref[0, 2 * pr] = jnp.where(lane < MLA_V, vp, 1.0).astype(BF16)
        v_ref[0, 2 * pr + 1] = jnp.where(lane >= MLA_V, vp, 1.0).astype(BF16)


def _inproj(x2, pos_rows, g_mix, w1, qg, wq, kvg, wkv, invf, bsz, seqlen):
    t = x2.shape[0]
    tm = TM_PROJ
    per_b = seqlen // tm
    full = lambda shape: pl.BlockSpec(shape, lambda i: (0,) * len(shape))
    head_spec = pl.BlockSpec((1, MLA_HEADS, tm, LANES), lambda i: (i // per_b, 0, i % per_b, 0))
    head_shape = jax.ShapeDtypeStruct((bsz, MLA_HEADS, seqlen, LANES), BF16)
    return pl.pallas_call(
        _inproj_kernel,
        grid=(t // tm,),
        in_specs=[
            pl.BlockSpec((tm, D_MODEL), lambda i: (i, 0)),
            pl.BlockSpec((1, 1, tm), lambda i: (i, 0, 0)),
            full((1, D_MODEL)), full((D_MODEL, IN_W)),
            full((1, MLA_Q_RANK)), full((MLA_Q_RANK, 2048)),
            full((1, MLA_KV_RANK)), full((MLA_KV_RANK, 1536)),
            full((MLA_ROPE // 2, 1)),
        ],
        out_specs=[
            pl.BlockSpec((tm, 512), lambda i: (i, 0)),
            pl.BlockSpec((tm, 1024), lambda i: (i, 0)),
            pl.BlockSpec((tm, LANES), lambda i: (i, 0)),
            head_spec, head_spec, head_spec,
        ],
        out_shape=[
            jax.ShapeDtypeStruct((t, 512), F32),
            jax.ShapeDtypeStruct((t, 1024), F32),
            jax.ShapeDtypeStruct((t, LANES), F32),
            head_shape, head_shape, head_shape,
        ],
        compiler_params=pltpu.CompilerParams(
            dimension_semantics=("arbitrary",), vmem_limit_bytes=56 * 1024 * 1024),
        name="inproj",
    )(x2, pos_rows, g_mix, w1, qg, wq, kvg, wkv, invf)


def _ssd_kernel(z_ref, xbc_ref, dtm_ref, cw_ref, cb_ref, dtb_ref, alog_ref, dsk_ref, ng_ref,
                y_ref, ext_ref, st_ref):
    q = SSD_CHUNK
    rows = SSD_CPS * q

    @pl.when(pl.program_id(1) == 0)
    def _():
        ext_ref[0:CONV_PAD, :] = jnp.zeros((CONV_PAD, SSD_CONV_DIM), F32)
        st_ref[...] = jnp.zeros_like(st_ref)

    ext_ref[CONV_PAD:CONV_PAD + rows, :] = xbc_ref[...]

    lane = lax.broadcasted_iota(I32, (1, LANES), 1)
    row = lax.broadcasted_iota(I32, (q, q), 0)
    col = lax.broadcasted_iota(I32, (q, q), 1)
    tril = row >= col
    tril_b = jnp.where(tril, 1.0, 0.0).astype(BF16)
    spread = jnp.where(
        lax.broadcasted_iota(I32, (LANES, SSD_WIDTH), 0)
        == lax.broadcasted_iota(I32, (LANES, SSD_WIDTH), 1) // SSD_HEAD_DIM, 1.0, 0.0).astype(BF16)
    a_neg = -jnp.exp(alog_ref[...])

    def split3(v):
        hi = v.astype(BF16)
        r1 = v - hi.astype(F32)
        mid = r1.astype(BF16)
        return hi, mid, (r1 - mid.astype(F32)).astype(BF16)

    def dot3_right(parts, m):
        return sum(jnp.dot(p, m, preferred_element_type=F32) for p in parts)

    def expand(cols):
        return dot3_right(split3(cols), spread)

    for ci in range(SSD_CPS):
        lo = ci * q
        conv = cb_ref[...]
        for kk in range(SSD_CONV):
            off = CONV_PAD + lo - (SSD_CONV - 1) + kk
            conv = conv + cw_ref[kk:kk + 1, :] * ext_ref[off:off + q, :]
        u = conv * jax.nn.sigmoid(conv)
        xs = u[:, 0:512]
        bm = u[:, 512:768]
        cm = u[:, 768:1024]

        xdt = dtm_ref[lo:lo + q, :] + dtb_ref[...]
        dt = jnp.maximum(xdt, 0.0) + jnp.log1p(jnp.exp(-jnp.abs(xdt)))
        adt = jnp.where(lane < SSD_HEADS, dt * a_neg, 0.0)
        cum_col = sum(jnp.dot(tril_b, p, preferred_element_type=F32) for p in split3(adt))
        cum_row = cum_col.T

        dt_e = expand(dt)
        ac_e = expand(cum_col)
        last_e = ac_e[q - 1:q, :]
        xd = xs * dt_e
        w_end = xd * jnp.exp(last_e - ac_e)
        eac = jnp.exp(ac_e)
        cdec = jnp.exp(last_e)

        y_parts = []
        for g in range(2):
            gl = g * 256
            bg = bm[:, g * SSD_STATE:(g + 1) * SSD_STATE]
            cg = cm[:, g * SSD_STATE:(g + 1) * SSD_STATE].astype(BF16)
            scores = lax.dot_general(cg, bg.astype(BF16), NT_DIMS, preferred_element_type=F32)
            bgt = bg.T.astype(BF16)
            sprev = st_ref[g]
            yoff = jnp.dot(cg, sprev.astype(BF16), preferred_element_type=F32)
            st_ref[g] = sprev * cdec[:, gl:gl + 256] + jnp.dot(
                bgt, w_end[:, gl:gl + 256].astype(BF16), preferred_element_type=F32)
            for pr in range(2):
                pl_lo = gl + pr * LANES
                xdp = xd[:, pl_lo:pl_lo + LANES].astype(BF16)
                res = []
                for jj in range(2):
                    h_i = g * 4 + pr * 2 + jj
                    seg = cum_col[:, h_i:h_i + 1] - cum_row[h_i:h_i + 1, :]
                    dec = jnp.exp(jnp.where(tril, seg, -jnp.inf))
                    res.append(jnp.dot((scores * dec).astype(BF16), xdp, preferred_element_type=F32))
                ydiag = jnp.where(lane < SSD_HEAD_DIM, res[0], res[1])
                y_parts.append(ydiag + yoff[:, pr * LANES:(pr + 1) * LANES] * eac[:, pl_lo:pl_lo + LANES])
        y = jnp.concatenate(y_parts, axis=1) + dsk_ref[...] * xs
        zz = z_ref[lo:lo + q, :]
        y = y * (zz * jax.nn.sigmoid(zz))
        outs = []
        for g in range(2):
            yg = y[:, g * 256:(g + 1) * 256]
            outs.append(_rms(yg))
        y_ref[lo:lo + q, :] = (jnp.concatenate(outs, axis=1) * ng_ref[...]).astype(BF16)

    ext_ref[0:CONV_PAD, :] = ext_ref[rows:rows + CONV_PAD, :]


def _ssd(z, xbc, dtm, cw, cb, dtb, alog, dsk, ng, bsz, seqlen):
    t = z.shape[0]
    q = SSD_CHUNK
    rows = SSD_CPS * q
    nc = seqlen // rows
    full = lambda shape: pl.BlockSpec(shape, lambda b, c: (0,) * len(shape))
    row_spec = lambda width: pl.BlockSpec((rows, width), lambda b, c: (b * nc + c, 0))
    return pl.pallas_call(
        _ssd_kernel,
        grid=(bsz, nc),
        in_specs=[row_spec(512), row_spec(1024), row_spec(LANES),
                  full((SSD_CONV, SSD_CONV_DIM)), full((1, SSD_CONV_DIM)),
                  full((1, LANES)), full((1, LANES)), full((1, SSD_WIDTH)), full((1, SSD_WIDTH))],
        out_specs=row_spec(512),
        out_shape=jax.ShapeDtypeStruct((t, SSD_WIDTH), BF16),
        scratch_shapes=[pltpu.VMEM((rows + CONV_PAD, SSD_CONV_DIM), F32),
                        pltpu.VMEM((2, SSD_STATE, 256), F32)],
        compiler_params=pltpu.CompilerParams(dimension_semantics=("arbitrary", "arbitrary")),
        name="ssd",
    )(z, xbc, dtm, cw, cb, dtb, alog, dsk, ng)


def _attn_kernel(q_ref, k_ref, v_ref, o_ref, acc_ref, s_ref, bmax_ref):
    i = pl.program_id(2)
    lane = lax.broadcasted_iota(I32, (1, LANES), 1)
    acc_ref[...] = jnp.zeros_like(acc_ref)

    def scores(j, slot):
        start = pl.multiple_of(j * TK, TK)
        for hh in range(ATT_HPS):
            kb = k_ref[0, hh, pl.ds(start, TK), :]
            s = lax.dot_general(q_ref[0, hh], kb, NT_DIMS, preferred_element_type=F32)
            s_ref[slot, hh] = s
            bmax_ref[slot, hh] = jnp.broadcast_to(jnp.max(s, axis=-1, keepdims=True), (TQ, LANES))

    def consume(j, slot, m_all, masked):
        start = pl.multiple_of(j * TK, TK)
        new_m = []
        for hh in range(ATT_HPS):
            vb = v_ref[0, hh, pl.ds(start, TK), :]
            s = s_ref[slot, hh]
            if masked:
                r = lax.broadcasted_iota(I32, (TQ, TK), 0)
                c = lax.broadcasted_iota(I32, (TQ, TK), 1)
                s = jnp.where(c <= r, s, -jnp.inf)
                block_max = jnp.broadcast_to(jnp.max(s, axis=-1, keepdims=True), (TQ, LANES))
            else:
                block_max = bmax_ref[slot, hh]
            m_old = m_all[hh]
            m_new = jnp.maximum(m_old, block_max)
            alpha = jnp.exp2(m_old - m_new)
            p = jnp.exp2(s - jnp.tile(m_new, (1, TK // LANES))).astype(BF16)
            acc_ref[hh] = acc_ref[hh] * alpha + jnp.dot(p, vb, preferred_element_type=F32)
            new_m.append(m_new)
        return tuple(new_m)

    def pair(p, m_all):
        scores(2 * p + 1, 1)
        m_all = consume(2 * p, 0, m_all, False)
        scores(2 * p + 2, 0)
        return consume(2 * p + 1, 1, m_all, False)

    def odd_tail(_, m_all):
        scores(i, 1)
        return consume(i - 1, 0, m_all, False)

    m0 = jnp.full((TQ, LANES), -jnp.inf, F32)
    scores(0, 0)
    m_all = lax.fori_loop(0, i // 2, pair, (m0,) * ATT_HPS)
    m_all = lax.fori_loop(0, i & 1, odd_tail, m_all)

    @pl.when((i & 1) == 0)
    def _():
        consume(i, 0, m_all, True)

    @pl.when((i & 1) == 1)
    def _():
        consume(i, 1, m_all, True)

    outs = []
    for pr in range(ATT_HPS // 2):
        a0 = acc_ref[2 * pr]
        a1 = acc_ref[2 * pr + 1]
        o0 = a0 / a0[:, MLA_V:MLA_V + 1]
        o1 = a1 / a1[:, 0:1]
        outs.append(jnp.where(lane < MLA_V, o0, o1))
    o_ref[0] = jnp.concatenate(outs, axis=1).astype(BF16)


def _attention(q, k, v, bsz, seqlen):
    nq = seqlen // TQ
    hps = ATT_HPS
    kv_spec = pl.BlockSpec((1, hps, seqlen, LANES), lambda b, p, i: (b, p, 0, 0))
    return pl.pallas_call(
        _attn_kernel,
        grid=(bsz, MLA_HEADS // hps, nq),
        in_specs=[pl.BlockSpec((1, hps, TQ, LANES), lambda b, p, i: (b, p, i, 0)), kv_spec, kv_spec],
        out_specs=pl.BlockSpec((1, TQ, hps * MLA_V), lambda b, p, i: (b, i, p)),
        out_shape=jax.ShapeDtypeStruct((bsz, seqlen, MLA_WIDTH), BF16),
        scratch_shapes=[pltpu.VMEM((hps, TQ, LANES), F32), pltpu.VMEM((2, hps, TQ, TK), F32),
                        pltpu.VMEM((2, hps, TQ, LANES), F32)],
        compiler_params=pltpu.CompilerParams(
            dimension_semantics=("arbitrary", "arbitrary", "arbitrary"),
            vmem_limit_bytes=56 * 1024 * 1024),
        name="attention",
    )(q, k, v)


def _outproj_kernel(x_ref, ys_ref, ym_ref, wo_ref, g_ref, wr_ref, br_ref,
                    x1_ref, h2_ref, route_ref, cnt_ref):
    tm = x_ref.shape[0]

    @pl.when(pl.program_id(0) == 0)
    def _():
        cnt_ref[...] = jnp.zeros_like(cnt_ref)

    mix = (jnp.dot(ys_ref[...], wo_ref[0:512, :], preferred_element_type=F32)
           + jnp.dot(ym_ref[...], wo_ref[512:1024, :], preferred_element_type=F32))
    x1 = x_ref[...] + mix
    x1_ref[...] = x1
    h2 = _rms(x1) * g_ref[...]
    h2_ref[...] = _pack_rows(h2)

    h_hi = h2.astype(BF16)
    h_lo = (h2 - h_hi.astype(F32)).astype(BF16)
    hh = jnp.dot(h_hi, wr_ref[...], preferred_element_type=F32)
    lh = jnp.dot(h_lo, wr_ref[:, 0:LANES], preferred_element_type=F32)
    logits = hh[:, 0:LANES] + (hh[:, LANES:2 * LANES] + lh) + br_ref[...]
    lt = logits.T[0:N_EXPERTS, :]
    eid = lax.broadcasted_iota(I32, (N_EXPERTS, 1), 0).astype(F32)

    vals, idxs, hots = [], [], []
    for _ in range(TOP_K):
        mx = jnp.max(lt, axis=0, keepdims=True)
        idx = jnp.min(jnp.where(lt == mx, eid, float(N_EXPERTS)), axis=0, keepdims=True)
        hot = eid == idx
        lt = jnp.where(hot, -jnp.inf, lt)
        vals.append(mx)
        idxs.append(idx)
        hots.append(hot)
    exps = [jnp.exp(v - vals[0]) for v in vals]
    denom = exps[0] + exps[1] + exps[2] + exps[3]

    multi_f = jnp.where(hots[0] | hots[1] | hots[2] | hots[3], 1.0, 0.0)
    r = lax.broadcasted_iota(I32, (tm, tm), 0)
    c = lax.broadcasted_iota(I32, (tm, tm), 1)
    earlier = jnp.where(r < c, 1.0, 0.0).astype(BF16)
    before = jnp.dot(multi_f.astype(BF16), earlier, preferred_element_type=F32) + cnt_ref[:, 0:1]
    cnt_ref[...] = cnt_ref[...] + jnp.sum(multi_f, axis=1, keepdims=True)

    ranks = [jnp.sum(jnp.where(hots[kk], before, 0.0), axis=0, keepdims=True) for kk in range(TOP_K)]
    gates = [e / denom for e in exps]
    route_ref[...] = jnp.concatenate(idxs + gates + ranks + [jnp.zeros((ROUTE_ROWS - 3 * TOP_K, tm), F32)], axis=0)


def _outproj(x2, y_ssd, y_mla, wo, g_ffn, wr, br):
    t = x2.shape[0]
    tm = TM_PROJ
    full = lambda shape: pl.BlockSpec(shape, lambda i: (0,) * len(shape))
    rows = lambda width: pl.BlockSpec((tm, width), lambda i: (i, 0))
    return pl.pallas_call(
        _outproj_kernel,
        grid=(t // tm,),
        in_specs=[rows(D_MODEL), rows(512), rows(512), full((1024, D_MODEL)), full((1, D_MODEL)),
                  full((D_MODEL, 2 * LANES)), full((1, LANES))],
        out_specs=[rows(D_MODEL), rows(HALF), pl.BlockSpec((ROUTE_ROWS, tm), lambda i: (0, i)),
                   full((N_EXPERTS, LANES))],
        out_shape=[jax.ShapeDtypeStruct((t, D_MODEL), F32),
                   jax.ShapeDtypeStruct((t, HALF), I32),
                   jax.ShapeDtypeStruct((ROUTE_ROWS, t), F32),
                   jax.ShapeDtypeStruct((N_EXPERTS, LANES), F32)],
        compiler_params=pltpu.CompilerParams(
            dimension_semantics=("arbitrary",), vmem_limit_bytes=40 * 1024 * 1024),
        name="outproj_router",
    )(x2, y_ssd, y_mla, wo, g_ffn, wr, br)


def _ffn_kernel(be_ref, bv_ref, nx_ref, sl_ref, xs_ref, wgu_hbm, bgu_ref, wd_hbm, bd_ref, ys_ref,
                wgu_st, wd_st, wgu_bf, wd_bf, sem):
    i = pl.program_id(0)
    e = be_ref[i]
    valid = bv_ref[i]
    slot = sl_ref[i]
    first = ((i == 0) | (e != be_ref[jnp.maximum(i - 1, 0)])) & (valid > 0)

    def weight_copies(expert, dst_slot):
        return (pltpu.make_async_copy(wgu_hbm.at[expert], wgu_st.at[dst_slot], sem.at[0, dst_slot]),
                pltpu.make_async_copy(wd_hbm.at[expert], wd_st.at[dst_slot], sem.at[1, dst_slot]))

    @pl.when(i == 0)
    def _():
        for cp in weight_copies(e, slot):
            cp.start()

    @pl.when(first)
    def _():
        for cp in weight_copies(e, slot):
            cp.wait()

        @pl.when(nx_ref[i] >= 0)
        def _():
            for cp in weight_copies(nx_ref[i], 1 - slot):
                cp.start(priority=1)

        wgu_bf[...] = wgu_st[slot].astype(BF16)
        wd_bf[...] = wd_st[slot].astype(BF16)

    def chain(r0, masked):
        rows = pl.ds(r0, MOE_BM)
        xp = xs_ref[rows, :]
        if masked:
            xp = jnp.where(r0 + lax.broadcasted_iota(I32, (MOE_BM, 1), 0) < valid, xp, 0)
        x_lo, x_hi = _unpack_rows(xp)
        gu = (jnp.dot(x_lo.astype(BF16), wgu_bf[0:HALF, :], preferred_element_type=F32)
              + jnp.dot(x_hi.astype(BF16), wgu_bf[HALF:D_MODEL, :], preferred_element_type=F32)
              + bgu_ref[0])
        gate = jnp.minimum(gu[:, :D_FF], SWIGLU_LIMIT)
        up = jnp.clip(gu[:, D_FF:], -SWIGLU_LIMIT, SWIGLU_LIMIT)
        glu = gate * jax.nn.sigmoid(SWIGLU_ALPHA * gate)
        mid = ((up + 1.0) * glu).astype(BF16)
        ys_ref[rows, :] = _pack_rows(jnp.dot(mid, wd_bf[...], preferred_element_type=F32) + bd_ref[0])

    @pl.when(valid == MOE_STEP)
    def _():
        for r0 in range(0, MOE_STEP, MOE_BM):
            chain(r0, False)

    @pl.when(valid < MOE_STEP)
    def _():
        for r0 in range(0, MOE_STEP, MOE_BM):
            pl.when(valid > r0)(functools.partial(chain, r0, True))

            @pl.when(valid <= r0)
            def _():
                ys_ref[pl.ds(r0, MOE_BM), :] = jnp.zeros((MOE_BM, HALF), I32)


def _expert_ffn(block_e, block_valid, block_next, block_slot, xs, wgu, bgu, wd, bd):
    n_slots = xs.shape[0]
    bm = MOE_STEP
    grid_spec = pltpu.PrefetchScalarGridSpec(
        num_scalar_prefetch=4,
        grid=(n_slots // bm,),
        in_specs=[
            pl.BlockSpec((bm, HALF), lambda i, be, bv, nx, sl: (i, 0)),
            pl.BlockSpec(memory_space=pl.ANY),
            pl.BlockSpec((1, 1, 2 * D_FF), lambda i, be, bv, nx, sl: (be[i], 0, 0)),
            pl.BlockSpec(memory_space=pl.ANY),
            pl.BlockSpec((1, 1, D_MODEL), lambda i, be, bv, nx, sl: (be[i], 0, 0)),
        ],
        out_specs=pl.BlockSpec((bm, HALF), lambda i, be, bv, nx, sl: (i, 0)),
        scratch_shapes=[pltpu.VMEM((2, D_MODEL, 2 * D_FF), F32), pltpu.VMEM((2, D_FF, D_MODEL), F32),
                        pltpu.VMEM((D_MODEL, 2 * D_FF), BF16), pltpu.VMEM((D_FF, D_MODEL), BF16),
                        pltpu.SemaphoreType.DMA((2, 2))],
    )
    return pl.pallas_call(
        _ffn_kernel,
        grid_spec=grid_spec,
        out_shape=jax.ShapeDtypeStruct((n_slots, HALF), I32),
        compiler_params=pltpu.CompilerParams(
            dimension_semantics=("arbitrary",), vmem_limit_bytes=56 * 1024 * 1024),
        name="expert_ffn",
    )(block_e, block_valid, block_next, block_slot, xs, wgu, bgu, wd, bd)


def _plan_kernel(sp_ref, route_ref, dest_ref):
    idx = route_ref[0:TOP_K, :]
    rank = route_ref[2 * TOP_K:3 * TOP_K, :]
    start = jnp.zeros(idx.shape, F32)
    for e_i in range(N_EXPERTS):
        start = jnp.where(idx == float(e_i), sp_ref[e_i].astype(F32), start)
    dest_ref[...] = (start + rank).astype(I32)


def _slot_plan(route, start_pad):
    t = route.shape[1]
    tm = TM_PLAN
    grid_spec = pltpu.PrefetchScalarGridSpec(
        num_scalar_prefetch=1,
        grid=(t // tm,),
        in_specs=[pl.BlockSpec((ROUTE_ROWS, tm), lambda i, sp: (0, i))],
        out_specs=pl.BlockSpec((TOP_K, tm), lambda i, sp: (0, i)),
    )
    return pl.pallas_call(
        _plan_kernel,
        grid_spec=grid_spec,
        out_shape=jax.ShapeDtypeStruct((TOP_K, t), I32),
        compiler_params=pltpu.CompilerParams(dimension_semantics=("arbitrary",)),
        name="slot_plan",
    )(start_pad, route)


SC_SCATTER_CHUNK = 32
SC_GATHER_CHUNK = 16


def _sc_workers():
    info = plsc.get_sparse_core_info()
    return info.num_cores, info.num_cores * info.num_subcores


def _sc_scatter_rows(rows, dest_km, n_out):
    t, w = rows.shape
    ch = SC_SCATTER_CHUNK
    n_cores, n_workers = _sc_workers()
    n_chunks = t // n_workers // ch
    mesh = plsc.VectorSubcoreMesh(core_axis_name="c", subcore_axis_name="s")

    @functools.partial(
        pl.kernel, mesh=mesh, out_type=jax.ShapeDtypeStruct((n_out, w), rows.dtype),
        scratch_types=[pltpu.VMEM((TOP_K, n_chunks, ch), I32), pltpu.VMEM((2, ch, w), rows.dtype),
                       pltpu.SemaphoreType.DMA((2,)), pltpu.SemaphoreType.DMA((2,))],
        name="sc_dispatch_scatter")
    def scatter_kernel(rows_hbm, dest_hbm, out_hbm, idx_v, rows_v, sem_in, sem_out):
        wid = lax.axis_index("s") * n_cores + lax.axis_index("c")
        first = wid * n_chunks
        for kk in range(TOP_K):
            pltpu.sync_copy(dest_hbm.at[kk, pl.ds(first, n_chunks)], idx_v.at[kk])

        def load(cc, b):
            return pltpu.make_async_copy(rows_hbm.at[pl.ds((first + cc) * ch, ch)], rows_v.at[b], sem_in.at[b])

        def scatters(cc, b):
            return [pltpu.make_async_copy(rows_v.at[b], out_hbm.at[idx_v.at[kk, cc]], sem_out.at[b])
                    for kk in range(TOP_K)]

        load(0, 0).start()
        load(1, 1).start()

        @pl.loop(0, n_chunks, step=2)
        def _(c):
            for b in range(2):
                load(c + b, b).wait()
                for cp in scatters(c + b, b):
                    cp.start()
            for b in range(2):
                for cp in scatters(c + b, b):
                    cp.wait()

                @pl.when(c + 2 + b < n_chunks)
                def _():
                    load(c + 2 + b, b).start()

    return scatter_kernel(rows, dest_km.reshape(TOP_K, t // ch, ch))


def _sc_gather_rows(table, dest_km):
    _, w = table.shape
    t = dest_km.shape[1]
    ch = SC_GATHER_CHUNK
    n_cores, n_workers = _sc_workers()
    n_chunks = t // n_workers // ch
    mesh = plsc.VectorSubcoreMesh(core_axis_name="c", subcore_axis_name="s")

    @functools.partial(
        pl.kernel, mesh=mesh, out_type=jax.ShapeDtypeStruct((TOP_K, t, w), table.dtype),
        scratch_types=[pltpu.VMEM((TOP_K, n_chunks, ch), I32), pltpu.VMEM((2, TOP_K, ch, w), table.dtype),
                       pltpu.SemaphoreType.DMA((2,)), pltpu.SemaphoreType.DMA((2,))],
        name="sc_combine_gather")
    def gather_kernel(table_hbm, dest_hbm, out_hbm, idx_v, rows_v, sem_in, sem_out):
        wid = lax.axis_index("s") * n_cores + lax.axis_index("c")
        first = wid * n_chunks
        for kk in range(TOP_K):
            pltpu.sync_copy(dest_hbm.at[kk, pl.ds(first, n_chunks)], idx_v.at[kk])

        def gathers(cc, b):
            return [pltpu.make_async_copy(table_hbm.at[idx_v.at[kk, cc]], rows_v.at[b, kk], sem_in.at[b])
                    for kk in range(TOP_K)]

        def stores(cc, b):
            return [pltpu.make_async_copy(rows_v.at[b, kk], out_hbm.at[kk, pl.ds((first + cc) * ch, ch)],
                                          sem_out.at[b]) for kk in range(TOP_K)]

        for b in range(2):
            for cp in gathers(b, b):
                cp.start()

        @pl.loop(0, n_chunks, step=2)
        def _(c):
            for b in range(2):
                for cp in gathers(c + b, b):
                    cp.wait()
                for cp in stores(c + b, b):
                    cp.start()
            for b in range(2):
                for cp in stores(c + b, b):
                    cp.wait()

                @pl.when(c + 2 + b < n_chunks)
                def _():
                    for cp in gathers(c + 2 + b, b):
                        cp.start()

    return gather_kernel(table, dest_km.reshape(TOP_K, t // ch, ch))


def _combine_kernel(x1_ref, yg_ref, route_ref, g_ref, o_ref, *, final_norm):
    tm = x1_ref.shape[0]
    moe_lo = jnp.zeros((tm, HALF), F32)
    moe_hi = jnp.zeros((tm, HALF), F32)
    route_t = jnp.concatenate([route_ref[...], jnp.zeros((LANES - ROUTE_ROWS, tm), F32)], axis=0).T
    for kk in range(TOP_K):
        gate = route_t[:, TOP_K + kk:TOP_K + kk + 1]
        y_lo, y_hi = _unpack_rows(yg_ref[kk])
        moe_lo = moe_lo + gate * y_lo
        moe_hi = moe_hi + gate * y_hi
    acc = x1_ref[...] + jnp.concatenate([moe_lo, moe_hi], axis=1)
    o_ref[...] = _rms(acc) * g_ref[...] if final_norm else acc


def _combine(x1, yg, route, g_final, final_norm):
    t = x1.shape[0]
    tm = TM_COMB
    return pl.pallas_call(
        functools.partial(_combine_kernel, final_norm=final_norm),
        grid=(t // tm,),
        in_specs=[pl.BlockSpec((tm, D_MODEL), lambda i: (i, 0)),
                  pl.BlockSpec((TOP_K, tm, HALF), lambda i: (0, i, 0)),
                  pl.BlockSpec((ROUTE_ROWS, tm), lambda i: (0, i)),
                  pl.BlockSpec((1, D_MODEL), lambda i: (0, 0))],
        out_specs=pl.BlockSpec((tm, D_MODEL), lambda i: (i, 0)),
        out_shape=jax.ShapeDtypeStruct((t, D_MODEL), F32),
        compiler_params=pltpu.CompilerParams(dimension_semantics=("arbitrary",)),
        name="combine",
    )(x1, yg, route, g_final)


def _prep_in_weights(w_in, w_uq, w_ukv):
    w_z = w_in[:, 0:512]
    w_xbc = w_in[:, 512:1536]
    w_dt = w_in[:, 1536:1544]
    w_cq = w_in[:, 1544:1800]
    w_ckv = w_in[:, 1800:1928]
    w_kr = w_in[:, 1928:1960]
    half = MLA_ROPE // 2
    zeros = lambda rows, width: jnp.zeros(rows + (width,), BF16)
    cat = lambda parts: jnp.concatenate([p.astype(BF16) for p in parts], axis=-1)
    d = (D_MODEL,)
    misc1 = [w_dt, zeros(d, MLA_NOPE - SSD_HEADS), w_kr, zeros(d, LANES - MLA_QK)]
    misc2 = [zeros(d, MLA_NOPE), w_kr[:, half:], w_kr[:, :half], zeros(d, LANES - MLA_QK)]
    w1 = cat([w_z, w_xbc, w_cq, w_ckv] + misc1 + misc2)

    wq3 = w_uq.reshape(MLA_Q_RANK, MLA_HEADS, MLA_QK)
    qh = (MLA_Q_RANK, MLA_HEADS)
    main = cat([wq3, zeros(qh, LANES - MLA_QK)])
    swap = cat([zeros(qh, MLA_NOPE), wq3[:, :, MLA_NOPE + half:], wq3[:, :, MLA_NOPE:MLA_NOPE + half],
                zeros(qh, LANES - MLA_QK)])
    wq = jnp.concatenate([main.reshape(MLA_Q_RANK, -1), swap.reshape(MLA_Q_RANK, -1)], axis=1)

    wkv3 = w_ukv.reshape(MLA_KV_RANK, MLA_HEADS, MLA_NOPE + MLA_V)
    kh = (MLA_KV_RANK, MLA_HEADS)
    kpart = cat([wkv3[:, :, :MLA_NOPE], zeros(kh, LANES - MLA_NOPE)])
    vpart = wkv3[:, :, MLA_NOPE:].astype(BF16)
    wkv = jnp.concatenate([kpart.reshape(MLA_KV_RANK, -1), vpart.reshape(MLA_KV_RANK, -1)], axis=1)
    return w1, wq, wkv


def _rope_inv_freq():
    inv_freq = ROPE_THETA ** (-jnp.arange(0, MLA_ROPE, 2, dtype=F32) / MLA_ROPE)
    return inv_freq[:, None]


def _pad_lanes(v, fill=0.0):
    return jnp.full((1, LANES), fill, F32).at[0, :v.shape[0]].set(v)


def kernel(x, positions, norm_mix_g, w_in, conv_w, conv_b, dt_bias, a_log, d_skip, ssd_norm_g, q_norm_g, w_uq, kv_norm_g, w_ukv, w_out, norm_ffn_g, w_router, b_router, w_gate_up, b_gate_up, w_down, b_down, norm_final_g):
    bsz, seqlen, d = x.shape
    t = bsz * seqlen
    depth = w_in.shape[0]
    x2 = x.reshape(t, d)
    pos_rows = positions.reshape(t // TM_PROJ, 1, TM_PROJ).astype(I32)
    invf = _rope_inv_freq()

    for l in range(depth):
        w1, wq, wkv = _prep_in_weights(w_in[l], w_uq[l], w_ukv[l])
        z, xbc, dtm, q, k, v = _inproj(
            x2, pos_rows, norm_mix_g[l][None, :], w1, q_norm_g[l][None, :], wq, kv_norm_g[l][None, :], wkv,
            invf, bsz, seqlen)
        y_ssd = _ssd(z, xbc, dtm, conv_w[l], conv_b[l][None, :], _pad_lanes(dt_bias[l]), _pad_lanes(a_log[l]),
                     jnp.repeat(d_skip[l], SSD_HEAD_DIM)[None, :], ssd_norm_g[l][None, :], bsz, seqlen)
        y_mla = _attention(q, k, v, bsz, seqlen).reshape(t, MLA_WIDTH)

        wr = jnp.zeros((d, LANES), F32).at[:, :N_EXPERTS].set(w_router[l])
        wr_hi = wr.astype(BF16)
        wr_lo = (wr - wr_hi.astype(F32)).astype(BF16)
        x1, h2p, route, cnt = _outproj(x2, y_ssd, y_mla, w_out[l].astype(BF16), norm_ffn_g[l][None, :],
                                       jnp.concatenate([wr_hi, wr_lo], axis=1), _pad_lanes(b_router[l]))

        counts = cnt[:, 0].astype(I32)
        padded = ((counts + MOE_STEP - 1) // MOE_STEP) * MOE_STEP
        end_pad = jnp.cumsum(padded)
        start_pad = end_pad - padded
        n_slots = t * TOP_K + N_EXPERTS * MOE_STEP
        n_blocks = n_slots // MOE_STEP
        block_start = jnp.arange(n_blocks, dtype=I32) * MOE_STEP
        block_e = jnp.minimum(jnp.sum(block_start[:, None] >= end_pad[None, :], axis=1), N_EXPERTS - 1).astype(I32)
        eids = jnp.arange(N_EXPERTS, dtype=I32)
        block_hot = block_e[:, None] == eids[None, :]
        per_block = lambda table: jnp.sum(jnp.where(block_hot, table[None, :], 0), axis=1).astype(I32)
        block_valid = jnp.clip(per_block(counts) - (block_start - per_block(start_pad)), 0, MOE_STEP).astype(I32)
        used = counts > 0
        later_used = jnp.where((eids[None, :] > eids[:, None]) & used[None, :], eids[None, :], N_EXPERTS)
        next_used = jnp.min(later_used, axis=1)
        next_used = jnp.where(next_used < N_EXPERTS, next_used, -1).astype(I32)
        stage_slot = ((jnp.cumsum(used.astype(I32)) - 1) & 1).astype(I32)
        dest_km = _slot_plan(route, start_pad.astype(I32))

        xs = _sc_scatter_rows(h2p, dest_km, n_slots)
        ys = _expert_ffn(block_e, block_valid, per_block(next_used), per_block(stage_slot), xs,
                         w_gate_up[l], b_gate_up[l][:, None, :], w_down[l], b_down[l][:, None, :])
        yg = _sc_gather_rows(ys, dest_km)
        x2 = _combine(x1, yg, route, norm_final_g[None, :], l == depth - 1)
    return x2.reshape(bsz, seqlen, d)
```

```python
import functools

import jax
import jax.numpy as jnp
import numpy as np
from jax import lax
from jax.experimental import pallas as pl
from jax.experimental.pallas import tpu as pltpu
from jax.experimental.pallas import tpu_sc as plsc

F32 = jnp.float32
BF16 = jnp.bfloat16
I32 = jnp.int32
HIGHEST = lax.Precision.HIGHEST

D_MODEL = 1024
EPS = 1e-6
LANES = 128

SSD_HEADS = 8
SSD_HEAD_DIM = 64
SSD_WIDTH = 512
SSD_STATE = 128
SSD_CONV = 4
SSD_CHUNK = 128
SSD_CONV_DIM = 1024
CONV_PAD = 8
SSD_CPS = 2

MLA_HEADS = 8
MLA_Q_RANK = 256
MLA_KV_RANK = 128
MLA_NOPE = 64
MLA_ROPE = 32
MLA_V = 64
MLA_QK = MLA_NOPE + MLA_ROPE
MLA_WIDTH = 512
ROPE_THETA = 10000.0
LOG2_E = 1.4426950408889634

N_EXPERTS = 32
TOP_K = 4
D_FF = 1024
SWIGLU_LIMIT = 7.0
SWIGLU_ALPHA = 1.702

IN_W = 512 + 1024 + 256 + 128 + 128 + 128

TM_PROJ = 512
TQ = 512
TK = 512
ATT_HPS = 4
MOE_BM = 256
MOE_STEP = 1024
TM_COMB = 512
TM_PLAN = 2048
ROUTE_ROWS = 16

NT_DIMS = (((1,), (1,)), ((), ()))


def _rms(x):
    return x * lax.rsqrt(jnp.mean(x * x, axis=-1, keepdims=True) + EPS)


HALF = D_MODEL // 2
HI_MASK = np.int32(-65536)


def _pack_rows(a):
    lo = lax.bitcast_convert_type(a[:, :HALF].astype(BF16).astype(F32), I32)
    hi = lax.bitcast_convert_type(a[:, HALF:].astype(BF16).astype(F32), I32)
    return (hi & HI_MASK) | lax.shift_right_logical(lo, 16)


def _unpack_rows(p):
    lo = lax.bitcast_convert_type(lax.shift_left(p, 16), F32)
    hi = lax.bitcast_convert_type(p & HI_MASK, F32)
    return lo, hi


def _inproj_kernel(x_ref, pos_ref, g_ref, w1_ref, qg_ref, wq_ref, kvg_ref, wkv_ref, invf_ref,
                   z_ref, xbc_ref, dtm_ref, q_ref, k_ref, v_ref):
    x = x_ref[...]
    h = (_rms(x) * g_ref[...]).astype(BF16)
    p = jnp.dot(h, w1_ref[...], preferred_element_type=F32)
    z_ref[...] = p[:, 0:512]
    xbc_ref[...] = p[:, 512:1536]
    cq = p[:, 1536:1792]
    ckv = p[:, 1792:1920]
    m1 = p[:, 1920:2048]
    m2 = p[:, 2048:2176]
    dtm_ref[...] = m1

    lane = lax.broadcasted_iota(I32, (1, LANES), 1)
    tm = x.shape[0]
    ang = invf_ref[...] * pos_ref[0].astype(F32)
    cos_c = jnp.cos(ang)
    sin_c = jnp.sin(ang)
    z_lo = jnp.zeros((MLA_NOPE, tm), F32)
    z_hi = jnp.zeros((LANES - MLA_QK, tm), F32)
    cos_t = jnp.concatenate([z_lo, cos_c, cos_c, z_hi], axis=0).T
    sin_t = jnp.concatenate([z_lo, -sin_c, sin_c, z_hi], axis=0).T
    cosq_t = jnp.where(lane < MLA_NOPE, 1.0, cos_t)
    scale = MLA_QK ** -0.5 * LOG2_E

    cqn = (_rms(cq) * qg_ref[...]).astype(BF16)
    qq = jnp.dot(cqn, wq_ref[...], preferred_element_type=F32)
    ckvn = (_rms(ckv) * kvg_ref[...]).astype(BF16)
    kv = jnp.dot(ckvn, wkv_ref[...], preferred_element_type=F32)
    krot = m1 * cos_t + m2 * sin_t
    for h_i in range(MLA_HEADS):
        lo = h_i * LANES
        qm = qq[:, lo:lo + LANES]
        qs = qq[:, 1024 + lo:1024 + lo + LANES]
        q_ref[0, h_i] = ((qm * cosq_t + qs * sin_t) * scale).astype(BF16)
        k_ref[0, h_i] = (kv[:, lo:lo + LANES] + krot).astype(BF16)
    for pr in range(MLA_HEADS // 2):
        vp = kv[:, 1024 + pr * LANES:1024 + (pr + 1) * LANES]
        v_ref[0, 2 * pr] = jnp.where(lane < MLA_V, vp, 1.0).astype(BF16)
        v_ref[0, 2 * pr + 1] = jnp.where(lane >= MLA_V, vp, 1.0).astype(BF16)


def _inproj(x2, pos_rows, g_mix, w1, qg, wq, kvg, wkv, invf, bsz, seqlen):
    t = x2.shape[0]
    tm = TM_PROJ
    per_b = seqlen // tm
    full = lambda shape: pl.BlockSpec(shape, lambda i: (0,) * len(shape))
    head_spec = pl.BlockSpec((1, MLA_HEADS, tm, LANES), lambda i: (i // per_b, 0, i % per_b, 0))
    head_shape = jax.ShapeDtypeStruct((bsz, MLA_HEADS, seqlen, LANES), BF16)
    return pl.pallas_call(
        _inproj_kernel,
        grid=(t // tm,),
        in_specs=[
            pl.BlockSpec((tm, D_MODEL), lambda i: (i, 0)),
            pl.BlockSpec((1, 1, tm), lambda i: (i, 0, 0)),
            full((1, D_MODEL)), full((D_MODEL, IN_W)),
            full((1, MLA_Q_RANK)), full((MLA_Q_RANK, 2048)),
            full((1, MLA_KV_RANK)), full((MLA_KV_RANK, 1536)),
            full((MLA_ROPE // 2, 1)),
        ],
        out_specs=[
            pl.BlockSpec((tm, 512), lambda i: (i, 0)),
            pl.BlockSpec((tm, 1024), lambda i: (i, 0)),
            pl.BlockSpec((tm, LANES), lambda i: (i, 0)),
            head_spec, head_spec, head_spec,
        ],
        out_shape=[
            jax.ShapeDtypeStruct((t, 512), F32),
            jax.ShapeDtypeStruct((t, 1024), F32),
            jax.ShapeDtypeStruct((t, LANES), F32),
            head_shape, head_shape, head_shape,
        ],
        compiler_params=pltpu.CompilerParams(
            dimension_semantics=("arbitrary",), vmem_limit_bytes=56 * 1024 * 1024),
        name="inproj",
    )(x2, pos_rows, g_mix, w1, qg, wq, kvg, wkv, invf)


def _ssd_kernel(z_ref, xbc_ref, dtm_ref, cw_ref, cb_ref, dtb_ref, alog_ref, dsk_ref, ng_ref,
                y_ref, ext_ref, st_ref):
    q = SSD_CHUNK
    rows = SSD_CPS * q

    @pl.when(pl.program_id(1) == 0)
    def _():
        ext_ref[0:CONV_PAD, :] = jnp.zeros((CONV_PAD, SSD_CONV_DIM), F32)
        st_ref[...] = jnp.zeros_like(st_ref)

    ext_ref[CONV_PAD:CONV_PAD + rows, :] = xbc_ref[...]

    lane = lax.broadcasted_iota(I32, (1, LANES), 1)
    row = lax.broadcasted_iota(I32, (q, q), 0)
    col = lax.broadcasted_iota(I32, (q, q), 1)
    tril = row >= col
    tril_b = jnp.where(tril, 1.0, 0.0).astype(BF16)
    spread = jnp.where(
        lax.broadcasted_iota(I32, (LANES, SSD_WIDTH), 0)
        == lax.broadcasted_iota(I32, (LANES, SSD_WIDTH), 1) // SSD_HEAD_DIM, 1.0, 0.0).astype(BF16)
    a_neg = -jnp.exp(alog_ref[...])

    def split3(v):
        hi = v.astype(BF16)
        r1 = v - hi.astype(F32)
        mid = r1.astype(BF16)
        return hi, mid, (r1 - mid.astype(F32)).astype(BF16)

    def dot3_right(parts, m):
        return sum(jnp.dot(p, m, preferred_element_type=F32) for p in parts)

    def expand(cols):
        return dot3_right(split3(cols), spread)

    for ci in range(SSD_CPS):
        lo = ci * q
        conv = cb_ref[...]
        for kk in range(SSD_CONV):
            off = CONV_PAD + lo - (SSD_CONV - 1) + kk
            conv = conv + cw_ref[kk:kk + 1, :] * ext_ref[off:off + q, :]
        u = conv * jax.nn.sigmoid(conv)
        xs = u[:, 0:512]
        bm = u[:, 512:768]
        cm = u[:, 768:1024]

        xdt = dtm_ref[lo:lo + q, :] + dtb_ref[...]
        dt = jnp.maximum(xdt, 0.0) + jnp.log1p(jnp.exp(-jnp.abs(xdt)))
        adt = jnp.where(lane < SSD_HEADS, dt * a_neg, 0.0)
        cum_col = sum(jnp.dot(tril_b, p, preferred_element_type=F32) for p in split3(adt))
        cum_row = cum_col.T

        dt_e = expand(dt)
        ac_e = expand(cum_col)
        last_e = ac_e[q - 1:q, :]
        xd = xs * dt_e
        w_end = xd * jnp.exp(last_e - ac_e)
        eac = jnp.exp(ac_e)
        cdec = jnp.exp(last_e)

        y_parts = []
        for g in range(2):
            gl = g * 256
            bg = bm[:, g * SSD_STATE:(g + 1) * SSD_STATE]
            cg = cm[:, g * SSD_STATE:(g + 1) * SSD_STATE].astype(BF16)
            scores = lax.dot_general(cg, bg.astype(BF16), NT_DIMS, preferred_element_type=F32)
            bgt = bg.T.astype(BF16)
            sprev = st_ref[g]
            yoff = jnp.dot(cg, sprev.astype(BF16), preferred_element_type=F32)
            st_ref[g] = sprev * cdec[:, gl:gl + 256] + jnp.dot(
                bgt, w_end[:, gl:gl + 256].astype(BF16), preferred_element_type=F32)
            for pr in range(2):
                pl_lo = gl + pr * LANES
                xdp = xd[:, pl_lo:pl_lo + LANES].astype(BF16)
                res = []
                for jj in range(2):
                    h_i = g * 4 + pr * 2 + jj
                    seg = cum_col[:, h_i:h_i + 1] - cum_row[h_i:h_i + 1, :]
                    dec = jnp.exp(jnp.where(tril, seg, -jnp.inf))
                    res.append(jnp.dot((scores * dec).astype(BF16), xdp, preferred_element_type=F32))
                ydiag = jnp.where(lane < SSD_HEAD_DIM, res[0], res[1])
                y_parts.append(ydiag + yoff[:, pr * LANES:(pr + 1) * LANES] * eac[:, pl_lo:pl_lo + LANES])
        y = jnp.concatenate(y_parts, axis=1) + dsk_ref[...] * xs
        zz = z_ref[lo:lo + q, :]
        y = y * (zz * jax.nn.sigmoid(zz))
        outs = []
        for g in range(2):
            yg = y[:, g * 256:(g + 1) * 256]
            outs.append(_rms(yg))
        y_ref[lo:lo + q, :] = (jnp.concatenate(outs, axis=1) * ng_ref[...]).astype(BF16)

    ext_ref[0:CONV_PAD, :] = ext_ref[rows:rows + CONV_PAD, :]


def _ssd(z, xbc, dtm, cw, cb, dtb, alog, dsk, ng, bsz, seqlen):
    t = z.shape[0]
    q = SSD_CHUNK
    rows = SSD_CPS * q
    nc = seqlen // rows
    full = lambda shape: pl.BlockSpec(shape, lambda b, c: (0,) * len(shape))
    row_spec = lambda width: pl.BlockSpec((rows, width), lambda b, c: (b * nc + c, 0))
    return pl.pallas_call(
        _ssd_kernel,
        grid=(bsz, nc),
        in_specs=[row_spec(512), row_spec(1024), row_spec(LANES),
                  full((SSD_CONV, SSD_CONV_DIM)), full((1, SSD_CONV_DIM)),
                  full((1, LANES)), full((1, LANES)), full((1, SSD_WIDTH)), full((1, SSD_WIDTH))],
        out_specs=row_spec(512),
        out_shape=jax.ShapeDtypeStruct((t, SSD_WIDTH), BF16),
        scratch_shapes=[pltpu.VMEM((rows + CONV_PAD, SSD_CONV_DIM), F32),
                        pltpu.VMEM((2, SSD_STATE, 256), F32)],
        compiler_params=pltpu.CompilerParams(dimension_semantics=("arbitrary", "arbitrary")),
        name="ssd",
    )(z, xbc, dtm, cw, cb, dtb, alog, dsk, ng)


def _attn_kernel(q_ref, k_ref, v_ref, o_ref, acc_ref, s_ref, bmax_ref):
    i = pl.program_id(2)
    lane = lax.broadcasted_iota(I32, (1, LANES), 1)
    acc_ref[...] = jnp.zeros_like(acc_ref)

    def scores(j, slot):
        start = pl.multiple_of(j * TK, TK)
        for hh in range(ATT_HPS):
            kb = k_ref[0, hh, pl.ds(start, TK), :]
            s = lax.dot_general(q_ref[0, hh], kb, NT_DIMS, preferred_element_type=F32)
            s_ref[slot, hh] = s
            bmax_ref[slot, hh] = jnp.broadcast_to(jnp.max(s, axis=-1, keepdims=True), (TQ, LANES))

    def consume(j, slot, m_all, masked):
        start = pl.multiple_of(j * TK, TK)
        new_m = []
        for hh in range(ATT_HPS):
            vb = v_ref[0, hh, pl.ds(start, TK), :]
            s = s_ref[slot, hh]
            if masked:
                r = lax.broadcasted_iota(I32, (TQ, TK), 0)
                c = lax.broadcasted_iota(I32, (TQ, TK), 1)
                s = jnp.where(c <= r, s, -jnp.inf)
                block_max = jnp.broadcast_to(jnp.max(s, axis=-1, keepdims=True), (TQ, LANES))
            else:
                block_max = bmax_ref[slot, hh]
            m_old = m_all[hh]
            m_new = jnp.maximum(m_old, block_max)
            alpha = jnp.exp2(m_old - m_new)
            p = jnp.exp2(s - jnp.tile(m_new, (1, TK // LANES))).astype(BF16)
            acc_ref[hh] = acc_ref[hh] * alpha + jnp.dot(p, vb, preferred_element_type=F32)
            new_m.append(m_new)
        return tuple(new_m)

    def pair(p, m_all):
        scores(2 * p + 1, 1)
        m_all = consume(2 * p, 0, m_all, False)
        scores(2 * p + 2, 0)
        return consume(2 * p + 1, 1, m_all, False)

    def odd_tail(_, m_all):
        scores(i, 1)
        return consume(i - 1, 0, m_all, False)

    def quad(g, m_all):
        return pair(2 * g + 1, pair(2 * g, m_all))

    m0 = jnp.full((TQ, LANES), -jnp.inf, F32)
    scores(0, 0)
    m_all = lax.fori_loop(0, i // 4, quad, (m0,) * ATT_HPS)
    m_all = lax.fori_loop(2 * (i // 4), i // 2, pair, m_all)
    m_all = lax.fori_loop(0, i & 1, odd_tail, m_all)

    @pl.when((i & 1) == 0)
    def _():
        consume(i, 0, m_all, True)

    @pl.when((i & 1) == 1)
    def _():
        consume(i, 1, m_all, True)

    outs = []
    for pr in range(ATT_HPS // 2):
        a0 = acc_ref[2 * pr]
        a1 = acc_ref[2 * pr + 1]
        o0 = a0 / a0[:, MLA_V:MLA_V + 1]
        o1 = a1 / a1[:, 0:1]
        outs.append(jnp.where(lane < MLA_V, o0, o1))
    o_ref[0] = jnp.concatenate(outs, axis=1).astype(BF16)


def _attention(q, k, v, bsz, seqlen):
    nq = seqlen // TQ
    hps = ATT_HPS
    kv_spec = pl.BlockSpec((1, hps, seqlen, LANES), lambda b, p, i: (b, p, 0, 0))
    return pl.pallas_call(
        _attn_kernel,
        grid=(bsz, MLA_HEADS // hps, nq),
        in_specs=[pl.BlockSpec((1, hps, TQ, LANES), lambda b, p, i: (b, p, i, 0)), kv_spec, kv_spec],
        out_specs=pl.BlockSpec((1, TQ, hps * MLA_V), lambda b, p, i: (b, i, p)),
        out_shape=jax.ShapeDtypeStruct((bsz, seqlen, MLA_WIDTH), BF16),
        scratch_shapes=[pltpu.VMEM((hps, TQ, LANES), F32), pltpu.VMEM((2, hps, TQ, TK), F32),
                        pltpu.VMEM((2, hps, TQ, LANES), F32)],
        compiler_params=pltpu.CompilerParams(
            dimension_semantics=("arbitrary", "arbitrary", "arbitrary"),
            vmem_limit_bytes=56 * 1024 * 1024),
        name="attention",
    )(q, k, v)


def _outproj_kernel(x_ref, ys_ref, ym_ref, wo_ref, g_ref, wr_ref, br_ref,
                    x1_ref, h2_ref, route_ref, cnt_ref):
    tm = x_ref.shape[0]

    @pl.when(pl.program_id(0) == 0)
    def _():
        cnt_ref[...] = jnp.zeros_like(cnt_ref)

    mix = (jnp.dot(ys_ref[...], wo_ref[0:512, :], preferred_element_type=F32)
           + jnp.dot(ym_ref[...], wo_ref[512:1024, :], preferred_element_type=F32))
    x1 = x_ref[...] + mix
    x1_ref[...] = x1
    h2 = _rms(x1) * g_ref[...]
    h2_ref[...] = _pack_rows(h2)

    h_hi = h2.astype(BF16)
    h_lo = (h2 - h_hi.astype(F32)).astype(BF16)
    hh = jnp.dot(h_hi, wr_ref[...], preferred_element_type=F32)
    lh = jnp.dot(h_lo, wr_ref[:, 0:LANES], preferred_element_type=F32)
    logits = hh[:, 0:LANES] + (hh[:, LANES:2 * LANES] + lh) + br_ref[...]
    lt = logits.T[0:N_EXPERTS, :]
    eid = lax.broadcasted_iota(I32, (N_EXPERTS, 1), 0).astype(F32)

    vals, idxs, hots = [], [], []
    for _ in range(TOP_K):
        mx = jnp.max(lt, axis=0, keepdims=True)
        idx = jnp.min(jnp.where(lt == mx, eid, float(N_EXPERTS)), axis=0, keepdims=True)
        hot = eid == idx
        lt = jnp.where(hot, -jnp.inf, lt)
        vals.append(mx)
        idxs.append(idx)
        hots.append(hot)
    exps = [jnp.exp(v - vals[0]) for v in vals]
    denom = exps[0] + exps[1] + exps[2] + exps[3]

    multi_f = jnp.where(hots[0] | hots[1] | hots[2] | hots[3], 1.0, 0.0)
    r = lax.broadcasted_iota(I32, (tm, tm), 0)
    c = lax.broadcasted_iota(I32, (tm, tm), 1)
    earlier = jnp.where(r < c, 1.0, 0.0).astype(BF16)
    before = jnp.dot(multi_f.astype(BF16), earlier, preferred_element_type=F32) + cnt_ref[:, 0:1]
    cnt_ref[...] = cnt_ref[...] + jnp.sum(multi_f, axis=1, keepdims=True)

    ranks = [jnp.sum(jnp.where(hots[kk], before, 0.0), axis=0, keepdims=True) for kk in range(TOP_K)]
    gates = [e / denom for e in exps]
    route_ref[...] = jnp.concatenate(idxs + gates + ranks + [jnp.zeros((ROUTE_ROWS - 3 * TOP_K, tm), F32)], axis=0)


def _outproj(x2, y_ssd, y_mla, wo, g_ffn, wr, br):
    t = x2.shape[0]
    tm = TM_PROJ
    full = lambda shape: pl.BlockSpec(shape, lambda i: (0,) * len(shape))
    rows = lambda width: pl.BlockSpec((tm, width), lambda i: (i, 0))
    return pl.pallas_call(
        _outproj_kernel,
        grid=(t // tm,),
        in_specs=[rows(D_MODEL), rows(512), rows(512), full((1024, D_MODEL)), full((1, D_MODEL)),
                  full((D_MODEL, 2 * LANES)), full((1, LANES))],
        out_specs=[rows(D_MODEL), rows(HALF), pl.BlockSpec((ROUTE_ROWS, tm), lambda i: (0, i)),
                   full((N_EXPERTS, LANES))],
        out_shape=[jax.ShapeDtypeStruct((t, D_MODEL), F32),
                   jax.ShapeDtypeStruct((t, HALF), I32),
                   jax.ShapeDtypeStruct((ROUTE_ROWS, t), F32),
                   jax.ShapeDtypeStruct((N_EXPERTS, LANES), F32)],
        compiler_params=pltpu.CompilerParams(
            dimension_semantics=("arbitrary",), vmem_limit_bytes=40 * 1024 * 1024),
        name="outproj_router",
    )(x2, y_ssd, y_mla, wo, g_ffn, wr, br)


def _ffn_kernel(be_ref, bv_ref, nx_ref, sl_ref, xs_ref, wgu_hbm, bgu_ref, wd_hbm, bd_ref, ys_ref,
                wgu_st, wd_st, wgu_bf, wd_bf, sem):
    i = pl.program_id(0)
    e = be_ref[i]
    valid = bv_ref[i]
    slot = sl_ref[i]
    first = ((i == 0) | (e != be_ref[jnp.maximum(i - 1, 0)])) & (valid > 0)

    def weight_copies(expert, dst_slot):
        return (pltpu.make_async_copy(wgu_hbm.at[expert], wgu_st.at[dst_slot], sem.at[0, dst_slot]),
                pltpu.make_async_copy(wd_hbm.at[expert], wd_st.at[dst_slot], sem.at[1, dst_slot]))

    @pl.when(i == 0)
    def _():
        for cp in weight_copies(e, slot):
            cp.start()

    @pl.when(first)
    def _():
        for cp in weight_copies(e, slot):
            cp.wait()

        @pl.when(nx_ref[i] >= 0)
        def _():
            for cp in weight_copies(nx_ref[i], 1 - slot):
                cp.start(priority=1)

        wgu_bf[...] = wgu_st[slot].astype(BF16)
        wd_bf[...] = wd_st[slot].astype(BF16)

    def chain(r0, masked):
        rows = pl.ds(r0, MOE_BM)
        xp = xs_ref[rows, :]
        if masked:
            xp = jnp.where(r0 + lax.broadcasted_iota(I32, (MOE_BM, 1), 0) < valid, xp, 0)
        x_lo, x_hi = _unpack_rows(xp)
        gu = (jnp.dot(x_lo.astype(BF16), wgu_bf[0:HALF, :], preferred_element_type=F32)
              + jnp.dot(x_hi.astype(BF16), wgu_bf[HALF:D_MODEL, :], preferred_element_type=F32)
              + bgu_ref[0])
        gate = jnp.minimum(gu[:, :D_FF], SWIGLU_LIMIT)
        up = jnp.clip(gu[:, D_FF:], -SWIGLU_LIMIT, SWIGLU_LIMIT)
        glu = gate * jax.nn.sigmoid(SWIGLU_ALPHA * gate)
        mid = ((up + 1.0) * glu).astype(BF16)
        ys_ref[rows, :] = _pack_rows(jnp.dot(mid, wd_bf[...], preferred_element_type=F32) + bd_ref[0])

    @pl.when(valid == MOE_STEP)
    def _():
        for r0 in range(0, MOE_STEP, MOE_BM):
            chain(r0, False)

    @pl.when(valid < MOE_STEP)
    def _():
        for r0 in range(0, MOE_STEP, MOE_BM):
            pl.when(valid > r0)(functools.partial(chain, r0, True))

            @pl.when(valid <= r0)
            def _():
                ys_ref[pl.ds(r0, MOE_BM), :] = jnp.zeros((MOE_BM, HALF), I32)


def _expert_ffn(block_e, block_valid, block_next, block_slot, xs, wgu, bgu, wd, bd):
    n_slots = xs.shape[0]
    bm = MOE_STEP
    grid_spec = pltpu.PrefetchScalarGridSpec(
        num_scalar_prefetch=4,
        grid=(n_slots // bm,),
        in_specs=[
            pl.BlockSpec((bm, HALF), lambda i, be, bv, nx, sl: (i, 0)),
            pl.BlockSpec(memory_space=pl.ANY),
            pl.BlockSpec((1, 1, 2 * D_FF), lambda i, be, bv, nx, sl: (be[i], 0, 0)),
            pl.BlockSpec(memory_space=pl.ANY),
            pl.BlockSpec((1, 1, D_MODEL), lambda i, be, bv, nx, sl: (be[i], 0, 0)),
        ],
        out_specs=pl.BlockSpec((bm, HALF), lambda i, be, bv, nx, sl: (i, 0)),
        scratch_shapes=[pltpu.VMEM((2, D_MODEL, 2 * D_FF), F32), pltpu.VMEM((2, D_FF, D_MODEL), F32),
                        pltpu.VMEM((D_MODEL, 2 * D_FF), BF16), pltpu.VMEM((D_FF, D_MODEL), BF16),
                        pltpu.SemaphoreType.DMA((2, 2))],
    )
    return pl.pallas_call(
        _ffn_kernel,
        grid_spec=grid_spec,
        out_shape=jax.ShapeDtypeStruct((n_slots, HALF), I32),
        compiler_params=pltpu.CompilerParams(
            dimension_semantics=("arbitrary",), vmem_limit_bytes=56 * 1024 * 1024),
        name="expert_ffn",
    )(block_e, block_valid, block_next, block_slot, xs, wgu, bgu, wd, bd)


def _plan_kernel(sp_ref, route_ref, dest_ref):
    idx = route_ref[0:TOP_K, :]
    rank = route_ref[2 * TOP_K:3 * TOP_K, :]
    start = jnp.zeros(idx.shape, F32)
    for e_i in range(N_EXPERTS):
        start = jnp.where(idx == float(e_i), sp_ref[e_i].astype(F32), start)
    dest_ref[...] = (start + rank).astype(I32)


def _slot_plan(route, start_pad):
    t = route.shape[1]
    tm = TM_PLAN
    grid_spec = pltpu.PrefetchScalarGridSpec(
        num_scalar_prefetch=1,
        grid=(t // tm,),
        in_specs=[pl.BlockSpec((ROUTE_ROWS, tm), lambda i, sp: (0, i))],
        out_specs=pl.BlockSpec((TOP_K, tm), lambda i, sp: (0, i)),
    )
    return pl.pallas_call(
        _plan_kernel,
        grid_spec=grid_spec,
        out_shape=jax.ShapeDtypeStruct((TOP_K, t), I32),
        compiler_params=pltpu.CompilerParams(dimension_semantics=("arbitrary",)),
        name="slot_plan",
    )(start_pad, route)


SC_SCATTER_CHUNK = 32
SC_GATHER_CHUNK = 16


def _sc_workers():
    info = plsc.get_sparse_core_info()
    return info.num_cores, info.num_cores * info.num_subcores


def _sc_scatter_rows(rows, dest_km, n_out):
    t, w = rows.shape
    ch = SC_SCATTER_CHUNK
    n_cores, n_workers = _sc_workers()
    n_chunks = t // n_workers // ch
    mesh = plsc.VectorSubcoreMesh(core_axis_name="c", subcore_axis_name="s")

    @functools.partial(
        pl.kernel, mesh=mesh, out_type=jax.ShapeDtypeStruct((n_out, w), rows.dtype),
        scratch_types=[pltpu.VMEM((TOP_K, n_chunks, ch), I32), pltpu.VMEM((2, ch, w), rows.dtype),
                       pltpu.SemaphoreType.DMA((2,)), pltpu.SemaphoreType.DMA((2,))],
        name="sc_dispatch_scatter")
    def scatter_kernel(rows_hbm, dest_hbm, out_hbm, idx_v, rows_v, sem_in, sem_out):
        wid = lax.axis_index("s") * n_cores + lax.axis_index("c")
        first = wid * n_chunks
        for kk in range(TOP_K):
            pltpu.sync_copy(dest_hbm.at[kk, pl.ds(first, n_chunks)], idx_v.at[kk])

        def load(cc, b):
            return pltpu.make_async_copy(rows_hbm.at[pl.ds((first + cc) * ch, ch)], rows_v.at[b], sem_in.at[b])

        def scatters(cc, b):
            return [pltpu.make_async_copy(rows_v.at[b], out_hbm.at[idx_v.at[kk, cc]], sem_out.at[b])
                    for kk in range(TOP_K)]

        load(0, 0).start()
        load(1, 1).start()

        @pl.loop(0, n_chunks, step=2)
        def _(c):
            for b in range(2):
                load(c + b, b).wait()
                for cp in scatters(c + b, b):
                    cp.start()
            for b in range(2):
                for cp in scatters(c + b, b):
                    cp.wait()

                @pl.when(c + 2 + b < n_chunks)
                def _():
                    load(c + 2 + b, b).start()

    return scatter_kernel(rows, dest_km.reshape(TOP_K, t // ch, ch))


def _sc_gather_rows(table, dest_km):
    _, w = table.shape
    t = dest_km.shape[1]
    ch = SC_GATHER_CHUNK
    n_cores, n_workers = _sc_workers()
    n_chunks = t // n_workers // ch
    mesh = plsc.VectorSubcoreMesh(core_axis_name="c", subcore_axis_name="s")

    @functools.partial(
        pl.kernel, mesh=mesh, out_type=jax.ShapeDtypeStruct((TOP_K, t, w), table.dtype),
        scratch_types=[pltpu.VMEM((TOP_K, n_chunks, ch), I32), pltpu.VMEM((2, TOP_K, ch, w), table.dtype),
                       pltpu.SemaphoreType.DMA((2,)), pltpu.SemaphoreType.DMA((2,))],
        name="sc_combine_gather")
    def gather_kernel(table_hbm, dest_hbm, out_hbm, idx_v, rows_v, sem_in, sem_out):
        wid = lax.axis_index("s") * n_cores + lax.axis_index("c")
        first = wid * n_chunks
        for kk in range(TOP_K):
            pltpu.sync_copy(dest_hbm.at[kk, pl.ds(first, n_chunks)], idx_v.at[kk])

        def gathers(cc, b):
            return [pltpu.make_async_copy(table_hbm.at[idx_v.at[kk, cc]], rows_v.at[b, kk], sem_in.at[b])
                    for kk in range(TOP_K)]

        def stores(cc, b):
            return [pltpu.make_async_copy(rows_v.at[b, kk], out_hbm.at[kk, pl.ds((first + cc) * ch, ch)],
                                          sem_out.at[b]) for kk in range(TOP_K)]

        for b in range(2):
            for cp in gathers(b, b):
                cp.start()

        @pl.loop(0, n_chunks, step=2)
        def _(c):
            for b in range(2):
                for cp in gathers(c + b, b):
                    cp.wait()
                for cp in stores(c + b, b):
                    cp.start()
            for b in range(2):
                for cp in stores(c + b, b):
                    cp.wait()

                @pl.when(c + 2 + b < n_chunks)
                def _():
                    for cp in gathers(c + 2 + b, b):
                        cp.start()

    return gather_kernel(table, dest_km.reshape(TOP_K, t // ch, ch))


def _combine_kernel(x1_ref, yg_ref, route_ref, g_ref, o_ref, *, final_norm):
    tm = x1_ref.shape[0]
    moe_lo = jnp.zeros((tm, HALF), F32)
    moe_hi = jnp.zeros((tm, HALF), F32)
    route_t = jnp.concatenate([route_ref[...], jnp.zeros((LANES - ROUTE_ROWS, tm), F32)], axis=0).T
    for kk in range(TOP_K):
        gate = route_t[:, TOP_K + kk:TOP_K + kk + 1]
        y_lo, y_hi = _unpack_rows(yg_ref[kk])
        moe_lo = moe_lo + gate * y_lo
        moe_hi = moe_hi + gate * y_hi
    acc = x1_ref[...] + jnp.concatenate([moe_lo, moe_hi], axis=1)
    o_ref[...] = _rms(acc) * g_ref[...] if final_norm else acc


def _combine(x1, yg, route, g_final, final_norm):
    t = x1.shape[0]
    tm = TM_COMB
    return pl.pallas_call(
        functools.partial(_combine_kernel, final_norm=final_norm),
        grid=(t // tm,),
        in_specs=[pl.BlockSpec((tm, D_MODEL), lambda i: (i, 0)),
                  pl.BlockSpec((TOP_K, tm, HALF), lambda i: (0, i, 0)),
                  pl.BlockSpec((ROUTE_ROWS, tm), lambda i: (0, i)),
                  pl.BlockSpec((1, D_MODEL), lambda i: (0, 0))],
        out_specs=pl.BlockSpec((tm, D_MODEL), lambda i: (i, 0)),
        out_shape=jax.ShapeDtypeStruct((t, D_MODEL), F32),
        compiler_params=pltpu.CompilerParams(dimension_semantics=("arbitrary",)),
        name="combine",
    )(x1, yg, route, g_final)


def _prep_in_weights(w_in, w_uq, w_ukv):
    w_z = w_in[:, 0:512]
    w_xbc = w_in[:, 512:1536]
    w_dt = w_in[:, 1536:1544]
    w_cq = w_in[:, 1544:1800]
    w_ckv = w_in[:, 1800:1928]
    w_kr = w_in[:, 1928:1960]
    half = MLA_ROPE // 2
    zeros = lambda rows, width: jnp.zeros(rows + (width,), BF16)
    cat = lambda parts: jnp.concatenate([p.astype(BF16) for p in parts], axis=-1)
    d = (D_MODEL,)
    misc1 = [w_dt, zeros(d, MLA_NOPE - SSD_HEADS), w_kr, zeros(d, LANES - MLA_QK)]
    misc2 = [zeros(d, MLA_NOPE), w_kr[:, half:], w_kr[:, :half], zeros(d, LANES - MLA_QK)]
    w1 = cat([w_z, w_xbc, w_cq, w_ckv] + misc1 + misc2)

    wq3 = w_uq.reshape(MLA_Q_RANK, MLA_HEADS, MLA_QK)
    qh = (MLA_Q_RANK, MLA_HEADS)
    main = cat([wq3, zeros(qh, LANES - MLA_QK)])
    swap = cat([zeros(qh, MLA_NOPE), wq3[:, :, MLA_NOPE + half:], wq3[:, :, MLA_NOPE:MLA_NOPE + half],
                zeros(qh, LANES - MLA_QK)])
    wq = jnp.concatenate([main.reshape(MLA_Q_RANK, -1), swap.reshape(MLA_Q_RANK, -1)], axis=1)

    wkv3 = w_ukv.reshape(MLA_KV_RANK, MLA_HEADS, MLA_NOPE + MLA_V)
    kh = (MLA_KV_RANK, MLA_HEADS)
    kpart = cat([wkv3[:, :, :MLA_NOPE], zeros(kh, LANES - MLA_NOPE)])
    vpart = wkv3[:, :, MLA_NOPE:].astype(BF16)
    wkv = jnp.concatenate([kpart.reshape(MLA_KV_RANK, -1), vpart.reshape(MLA_KV_RANK, -1)], axis=1)
    return w1, wq, wkv


def _rope_inv_freq():
    inv_freq = ROPE_THETA ** (-jnp.arange(0, MLA_ROPE, 2, dtype=F32) / MLA_ROPE)
    return inv_freq[:, None]


def _pad_lanes(v, fill=0.0):
    return jnp.full((1, LANES), fill, F32).at[0, :v.shape[0]].set(v)


def kernel(x, positions, norm_mix_g, w_in, conv_w, conv_b, dt_bias, a_log, d_skip, ssd_norm_g, q_norm_g, w_uq, kv_norm_g, w_ukv, w_out, norm_ffn_g, w_router, b_router, w_gate_up, b_gate_up, w_down, b_down, norm_final_g):
    bsz, seqlen, d = x.shape
    t = bsz * seqlen
    depth = w_in.shape[0]
    x2 = x.reshape(t, d)
    pos_rows = positions.reshape(t // TM_PROJ, 1, TM_PROJ).astype(I32)
    invf = _rope_inv_freq()

    for l in range(depth):
        w1, wq, wkv = _prep_in_weights(w_in[l], w_uq[l], w_ukv[l])
        z, xbc, dtm, q, k, v = _inproj(
            x2, pos_rows, norm_mix_g[l][None, :], w1, q_norm_g[l][None, :], wq, kv_norm_g[l][None, :], wkv,
            invf, bsz, seqlen)
        y_ssd = _ssd(z, xbc, dtm, conv_w[l], conv_b[l][None, :], _pad_lanes(dt_bias[l]), _pad_lanes(a_log[l]),
                     jnp.repeat(d_skip[l], SSD_HEAD_DIM)[None, :], ssd_norm_g[l][None, :], bsz, seqlen)
        y_mla = _attention(q, k, v, bsz, seqlen).reshape(t, MLA_WIDTH)

        wr = jnp.zeros((d, LANES), F32).at[:, :N_EXPERTS].set(w_router[l])
        wr_hi = wr.astype(BF16)
        wr_lo = (wr - wr_hi.astype(F32)).astype(BF16)
        x1, h2p, route, cnt = _outproj(x2, y_ssd, y_mla, w_out[l].astype(BF16), norm_ffn_g[l][None, :],
                                       jnp.concatenate([wr_hi, wr_lo], axis=1), _pad_lanes(b_router[l]))

        counts = cnt[:, 0].astype(I32)
        padded = ((counts + MOE_STEP - 1) // MOE_STEP) * MOE_STEP
        end_pad = jnp.cumsum(padded)
        start_pad = end_pad - padded
        n_slots = t * TOP_K + N_EXPERTS * MOE_STEP
        n_blocks = n_slots // MOE_STEP
        block_start = jnp.arange(n_blocks, dtype=I32) * MOE_STEP
        block_e = jnp.minimum(jnp.sum(block_start[:, None] >= end_pad[None, :], axis=1), N_EXPERTS - 1).astype(I32)
        eids = jnp.arange(N_EXPERTS, dtype=I32)
        block_hot = block_e[:, None] == eids[None, :]
        per_block = lambda table: jnp.sum(jnp.where(block_hot, table[None, :], 0), axis=1).astype(I32)
        block_valid = jnp.clip(per_block(counts) - (block_start - per_block(start_pad)), 0, MOE_STEP).astype(I32)
        used = counts > 0
        later_used = jnp.where((eids[None, :] > eids[:, None]) & used[None, :], eids[None, :], N_EXPERTS)
        next_used = jnp.min(later_used, axis=1)
        next_used = jnp.where(next_used < N_EXPERTS, next_used, -1).astype(I32)
        stage_slot = ((jnp.cumsum(used.astype(I32)) - 1) & 1).astype(I32)
        dest_km = _slot_plan(route, start_pad.astype(I32))

        xs = _sc_scatter_rows(h2p, dest_km, n_slots)
        ys = _expert_ffn(block_e, block_valid, per_block(next_used), per_block(stage_slot), xs,
                         w_gate_up[l], b_gate_up[l][:, None, :], w_down[l], b_down[l][:, None, :])
        yg = _sc_gather_rows(ys, dest_km)
        x2 = _combine(x1, yg, route, norm_final_g[None, :], l == depth - 1)
    return x2.reshape(bsz, seqlen, d)
```

```python
import functools

import jax
import jax.numpy as jnp
import numpy as np
from jax import lax
from jax.experimental import pallas as pl
from jax.experimental.pallas import tpu as pltpu
from jax.experimental.pallas import tpu_sc as plsc

F32 = jnp.float32
BF16 = jnp.bfloat16
I32 = jnp.int32
HIGHEST = lax.Precision.HIGHEST

D_MODEL = 1024
EPS = 1e-6
LANES = 128

SSD_HEADS = 8
SSD_HEAD_DIM = 64
SSD_WIDTH = 512
SSD_STATE = 128
SSD_CONV = 4
SSD_CHUNK = 128
SSD_CONV_DIM = 1024
CONV_PAD = 8
SSD_CPS = 2

MLA_HEADS = 8
MLA_Q_RANK = 256
MLA_KV_RANK = 128
MLA_NOPE = 64
MLA_ROPE = 32
MLA_V = 64
MLA_QK = MLA_NOPE + MLA_ROPE
MLA_WIDTH = 512
ROPE_THETA = 10000.0
LOG2_E = 1.4426950408889634

N_EXPERTS = 32
TOP_K = 4
D_FF = 1024
SWIGLU_LIMIT = 7.0
SWIGLU_ALPHA = 1.702

IN_W = 512 + 1024 + 256 + 128 + 128 + 128

TM_PROJ = 512
TQ = 512
TK = 512
ATT_HPS = 4
MOE_BM = 256
MOE_STEP = 1024
TM_COMB = 512
TM_PLAN = 2048
COMBINE_PARTS = 2
ROUTE_ROWS = 16

NT_DIMS = (((1,), (1,)), ((), ()))


def _rms(x):
    return x * lax.rsqrt(jnp.mean(x * x, axis=-1, keepdims=True) + EPS)


HALF = D_MODEL // 2
HI_MASK = np.int32(-65536)


def _pack_rows(a):
    lo = lax.bitcast_convert_type(a[:, :HALF].astype(BF16).astype(F32), I32)
    hi = lax.bitcast_convert_type(a[:, HALF:].astype(BF16).astype(F32), I32)
    return (hi & HI_MASK) | lax.shift_right_logical(lo, 16)


def _unpack_rows(p):
    lo = lax.bitcast_convert_type(lax.shift_left(p, 16), F32)
    hi = lax.bitcast_convert_type(p & HI_MASK, F32)
    return lo, hi


def _inproj_kernel(x_ref, pos_ref, g_ref, w1_ref, qg_ref, wq_ref, kvg_ref, wkv_ref, invf_ref,
                   z_ref, xbc_ref, dtm_ref, q_ref, k_ref, v_ref):
    x = x_ref[...]
    h = (_rms(x) * g_ref[...]).astype(BF16)
    p = jnp.dot(h, w1_ref[...], preferred_element_type=F32)
    z_ref[...] = p[:, 0:512]
    xbc_ref[...] = p[:, 512:1536]
    cq = p[:, 1536:1792]
    ckv = p[:, 1792:1920]
    m1 = p[:, 1920:2048]
    m2 = p[:, 2048:2176]
    dtm_ref[...] = m1

    lane = lax.broadcasted_iota(I32, (1, LANES), 1)
    tm = x.shape[0]
    ang = invf_ref[...] * pos_ref[0].astype(F32)
    cos_c = jnp.cos(ang)
    sin_c = jnp.sin(ang)
    z_lo = jnp.zeros((MLA_NOPE, tm), F32)
    z_hi = jnp.zeros((LANES - MLA_QK, tm), F32)
    cos_t = jnp.concatenate([z_lo, cos_c, cos_c, z_hi], axis=0).T
    sin_t = jnp.concatenate([z_lo, -sin_c, sin_c, z_hi], axis=0).T
    cosq_t = jnp.where(lane < MLA_NOPE, 1.0, cos_t)
    scale = MLA_QK ** -0.5 * LOG2_E

    cqn = (_rms(cq) * qg_ref[...]).astype(BF16)
    qq = jnp.dot(cqn, wq_ref[...], preferred_element_type=F32)
    ckvn = (_rms(ckv) * kvg_ref[...]).astype(BF16)
    kv = jnp.dot(ckvn, wkv_ref[...], preferred_element_type=F32)
    krot = m1 * cos_t + m2 * sin_t
    for h_i in range(MLA_HEADS):
        lo = h_i * LANES
        qm = qq[:, lo:lo + LANES]
        qs = qq[:, 1024 + lo:1024 + lo + LANES]
        q_ref[0, h_i] = ((qm * cosq_t + qs * sin_t) * scale).astype(BF16)
        k_ref[0, h_i] = (kv[:, lo:lo + LANES] + krot).astype(BF16)
    for pr in range(MLA_HEADS // 2):
        vp = kv[:, 1024 + pr * LANES:1024 + (pr + 1) * LANES]
        v_ref[0, 2 * pr] = jnp.where(lane < MLA_V, vp, 1.0).astype(BF16)
        v_ref[0, 2 * pr + 1] = jnp.where(lane >= MLA_V, vp, 1.0).astype(BF16)


def _inproj(x2, pos_rows, g_mix, w1, qg, wq, kvg, wkv, invf, bsz, seqlen):
    t = x2.shape[0]
    tm = TM_PROJ
    per_b = seqlen // tm
    full = lambda shape: pl.BlockSpec(shape, lambda i: (0,) * len(shape))
    head_spec = pl.BlockSpec((1, MLA_HEADS, tm, LANES), lambda i: (i // per_b, 0, i % per_b, 0))
    head_shape = jax.ShapeDtypeStruct((bsz, MLA_HEADS, seqlen, LANES), BF16)
    return pl.pallas_call(
        _inproj_kernel,
        grid=(t // tm,),
        in_specs=[
            pl.BlockSpec((tm, D_MODEL), lambda i: (i, 0)),
            pl.BlockSpec((1, 1, tm), lambda i: (i, 0, 0)),
            full((1, D_MODEL)), full((D_MODEL, IN_W)),
            full((1, MLA_Q_RANK)), full((MLA_Q_RANK, 2048)),
            full((1, MLA_KV_RANK)), full((MLA_KV_RANK, 1536)),
            full((MLA_ROPE // 2, 1)),
        ],
        out_specs=[
            pl.BlockSpec((tm, 512), lambda i: (i, 0)),
            pl.BlockSpec((tm, 1024), lambda i: (i, 0)),
            pl.BlockSpec((tm, LANES), lambda i: (i, 0)),
            head_spec, head_spec, head_spec,
        ],
        out_shape=[
            jax.ShapeDtypeStruct((t, 512), F32),
            jax.ShapeDtypeStruct((t, 1024), F32),
            jax.ShapeDtypeStruct((t, LANES), F32),
            head_shape, head_shape, head_shape,
        ],
        compiler_params=pltpu.CompilerParams(
            dimension_semantics=("arbitrary",), vmem_limit_bytes=56 * 1024 * 1024),
        name="inproj",
    )(x2, pos_rows, g_mix, w1, qg, wq, kvg, wkv, invf)


def _ssd_kernel(z_ref, xbc_ref, dtm_ref, cw_ref, cb_ref, dtb_ref, alog_ref, dsk_ref, ng_ref,
                y_ref, ext_ref, st_ref):
    q = SSD_CHUNK
    rows = SSD_CPS * q

    @pl.when(pl.program_id(1) == 0)
    def _():
        ext_ref[0:CONV_PAD, :] = jnp.zeros((CONV_PAD, SSD_CONV_DIM), F32)
        st_ref[...] = jnp.zeros_like(st_ref)

    ext_ref[CONV_PAD:CONV_PAD + rows, :] = xbc_ref[...]

    lane = lax.broadcasted_iota(I32, (1, LANES), 1)
    row = lax.broadcasted_iota(I32, (q, q), 0)
    col = lax.broadcasted_iota(I32, (q, q), 1)
    tril = row >= col
    tril_b = jnp.where(tril, 1.0, 0.0).astype(BF16)
    spread = jnp.where(
        lax.broadcasted_iota(I32, (LANES, SSD_WIDTH), 0)
        == lax.broadcasted_iota(I32, (LANES, SSD_WIDTH), 1) // SSD_HEAD_DIM, 1.0, 0.0).astype(BF16)
    a_neg = -jnp.exp(alog_ref[...])

    def split3(v):
        hi = v.astype(BF16)
        r1 = v - hi.astype(F32)
        mid = r1.astype(BF16)
        return hi, mid, (r1 - mid.astype(F32)).astype(BF16)

    def dot3_right(parts, m):
        return sum(jnp.dot(p, m, preferred_element_type=F32) for p in parts)

    def expand(cols):
        return dot3_right(split3(cols), spread)

    for ci in range(SSD_CPS):
        lo = ci * q
        conv = cb_ref[...]
        for kk in range(SSD_CONV):
            off = CONV_PAD + lo - (SSD_CONV - 1) + kk
            conv = conv + cw_ref[kk:kk + 1, :] * ext_ref[off:off + q, :]
        u = conv * jax.nn.sigmoid(conv)
        xs = u[:, 0:512]
        bm = u[:, 512:768]
        cm = u[:, 768:1024]

        xdt = dtm_ref[lo:lo + q, :] + dtb_ref[...]
        dt = jnp.maximum(xdt, 0.0) + jnp.log1p(jnp.exp(-jnp.abs(xdt)))
        adt = jnp.where(lane < SSD_HEADS, dt * a_neg, 0.0)
        cum_col = sum(jnp.dot(tril_b, p, preferred_element_type=F32) for p in split3(adt))
        cum_row = cum_col.T

        dt_e = expand(dt)
        ac_e = expand(cum_col)
        last_e = ac_e[q - 1:q, :]
        xd = xs * dt_e
        w_end = xd * jnp.exp(last_e - ac_e)
        eac = jnp.exp(ac_e)
        cdec = jnp.exp(last_e)

        y_parts = []
        for g in range(2):
            gl = g * 256
            bg = bm[:, g * SSD_STATE:(g + 1) * SSD_STATE]
            cg = cm[:, g * SSD_STATE:(g + 1) * SSD_STATE].astype(BF16)
            scores = lax.dot_general(cg, bg.astype(BF16), NT_DIMS, preferred_element_type=F32)
            bgt = bg.T.astype(BF16)
            sprev = st_ref[g]
            yoff = jnp.dot(cg, sprev.astype(BF16), preferred_element_type=F32)
            st_ref[g] = sprev * cdec[:, gl:gl + 256] + jnp.dot(
                bgt, w_end[:, gl:gl + 256].astype(BF16), preferred_element_type=F32)
            for pr in range(2):
                pl_lo = gl + pr * LANES
                xdp = xd[:, pl_lo:pl_lo + LANES].astype(BF16)
                res = []
                for jj in range(2):
                    h_i = g * 4 + pr * 2 + jj
                    seg = cum_col[:, h_i:h_i + 1] - cum_row[h_i:h_i + 1, :]
                    dec = jnp.exp(jnp.where(tril, seg, -jnp.inf))
                    res.append(jnp.dot((scores * dec).astype(BF16), xdp, preferred_element_type=F32))
                ydiag = jnp.where(lane < SSD_HEAD_DIM, res[0], res[1])
                y_parts.append(ydiag + yoff[:, pr * LANES:(pr + 1) * LANES] * eac[:, pl_lo:pl_lo + LANES])
        y = jnp.concatenate(y_parts, axis=1) + dsk_ref[...] * xs
        zz = z_ref[lo:lo + q, :]
        y = y * (zz * jax.nn.sigmoid(zz))
        outs = []
        for g in range(2):
            yg = y[:, g * 256:(g + 1) * 256]
            outs.append(_rms(yg))
        y_ref[lo:lo + q, :] = (jnp.concatenate(outs, axis=1) * ng_ref[...]).astype(BF16)

    ext_ref[0:CONV_PAD, :] = ext_ref[rows:rows + CONV_PAD, :]


def _ssd(z, xbc, dtm, cw, cb, dtb, alog, dsk, ng, bsz, seqlen):
    t = z.shape[0]
    q = SSD_CHUNK
    rows = SSD_CPS * q
    nc = seqlen // rows
    full = lambda shape: pl.BlockSpec(shape, lambda b, c: (0,) * len(shape))
    row_spec = lambda width: pl.BlockSpec((rows, width), lambda b, c: (b * nc + c, 0))
    return pl.pallas_call(
        _ssd_kernel,
        grid=(bsz, nc),
        in_specs=[row_spec(512), row_spec(1024), row_spec(LANES),
                  full((SSD_CONV, SSD_CONV_DIM)), full((1, SSD_CONV_DIM)),
                  full((1, LANES)), full((1, LANES)), full((1, SSD_WIDTH)), full((1, SSD_WIDTH))],
        out_specs=row_spec(512),
        out_shape=jax.ShapeDtypeStruct((t, SSD_WIDTH), BF16),
        scratch_shapes=[pltpu.VMEM((rows + CONV_PAD, SSD_CONV_DIM), F32),
                        pltpu.VMEM((2, SSD_STATE, 256), F32)],
        compiler_params=pltpu.CompilerParams(dimension_semantics=("arbitrary", "arbitrary")),
        name="ssd",
    )(z, xbc, dtm, cw, cb, dtb, alog, dsk, ng)


def _attn_kernel(q_ref, k_ref, v_ref, o_ref, acc_ref, s_ref, bmax_ref):
    i = pl.program_id(2)
    lane = lax.broadcasted_iota(I32, (1, LANES), 1)
    acc_ref[...] = jnp.zeros_like(acc_ref)

    def scores(j, slot):
        start = pl.multiple_of(j * TK, TK)
        for hh in range(ATT_HPS):
            kb = k_ref[0, hh, pl.ds(start, TK), :]
            s = lax.dot_general(q_ref[0, hh], kb, NT_DIMS, preferred_element_type=F32)
            s_ref[slot, hh] = s
            bmax_ref[slot, hh] = jnp.broadcast_to(jnp.max(s, axis=-1, keepdims=True), (TQ, LANES))

    def consume(j, slot, m_all, masked):
        start = pl.multiple_of(j * TK, TK)
        new_m = []
        for hh in range(ATT_HPS):
            vb = v_ref[0, hh, pl.ds(start, TK), :]
            s = s_ref[slot, hh]
            if masked:
                r = lax.broadcasted_iota(I32, (TQ, TK), 0)
                c = lax.broadcasted_iota(I32, (TQ, TK), 1)
                s = jnp.where(c <= r, s, -jnp.inf)
                block_max = jnp.broadcast_to(jnp.max(s, axis=-1, keepdims=True), (TQ, LANES))
            else:
                block_max = bmax_ref[slot, hh]
            m_old = m_all[hh]
            m_new = jnp.maximum(m_old, block_max)
            alpha = jnp.exp2(m_old - m_new)
            p = jnp.exp2(s - jnp.tile(m_new, (1, TK // LANES))).astype(BF16)
            acc_ref[hh] = acc_ref[hh] * alpha + jnp.dot(p, vb, preferred_element_type=F32)
            new_m.append(m_new)
        return tuple(new_m)

    def pair(p, m_all):
        scores(2 * p + 1, 1)
        m_all = consume(2 * p, 0, m_all, False)
        scores(2 * p + 2, 0)
        return consume(2 * p + 1, 1, m_all, False)

    def odd_tail(_, m_all):
        scores(i, 1)
        return consume(i - 1, 0, m_all, False)

    def quad(g, m_all):
        return pair(2 * g + 1, pair(2 * g, m_all))

    m0 = jnp.full((TQ, LANES), -jnp.inf, F32)
    scores(0, 0)
    m_all = lax.fori_loop(0, i // 4, quad, (m0,) * ATT_HPS)
    m_all = lax.fori_loop(2 * (i // 4), i // 2, pair, m_all)
    m_all = lax.fori_loop(0, i & 1, odd_tail, m_all)

    @pl.when((i & 1) == 0)
    def _():
        consume(i, 0, m_all, True)

    @pl.when((i & 1) == 1)
    def _():
        consume(i, 1, m_all, True)

    outs = []
    for pr in range(ATT_HPS // 2):
        a0 = acc_ref[2 * pr]
        a1 = acc_ref[2 * pr + 1]
        o0 = a0 / a0[:, MLA_V:MLA_V + 1]
        o1 = a1 / a1[:, 0:1]
        outs.append(jnp.where(lane < MLA_V, o0, o1))
    o_ref[0] = jnp.concatenate(outs, axis=1).astype(BF16)


def _attention(q, k, v, bsz, seqlen):
    nq = seqlen // TQ
    hps = ATT_HPS
    kv_spec = pl.BlockSpec((1, hps, seqlen, LANES), lambda b, p, i: (b, p, 0, 0))
    return pl.pallas_call(
        _attn_kernel,
        grid=(bsz, MLA_HEADS // hps, nq),
        in_specs=[pl.BlockSpec((1, hps, TQ, LANES), lambda b, p, i: (b, p, i, 0)), kv_spec, kv_spec],
        out_specs=pl.BlockSpec((1, TQ, hps * MLA_V), lambda b, p, i: (b, i, p)),
        out_shape=jax.ShapeDtypeStruct((bsz, seqlen, MLA_WIDTH), BF16),
        scratch_shapes=[pltpu.VMEM((hps, TQ, LANES), F32), pltpu.VMEM((2, hps, TQ, TK), F32),
                        pltpu.VMEM((2, hps, TQ, LANES), F32)],
        compiler_params=pltpu.CompilerParams(
            dimension_semantics=("arbitrary", "arbitrary", "arbitrary"),
            vmem_limit_bytes=56 * 1024 * 1024),
        name="attention",
    )(q, k, v)


def _outproj_kernel(x_ref, ys_ref, ym_ref, wo_ref, g_ref, wr_ref, br_ref,
                    x1_ref, h2_ref, route_ref, cnt_ref):
    tm = x_ref.shape[0]

    @pl.when(pl.program_id(0) == 0)
    def _():
        cnt_ref[...] = jnp.zeros_like(cnt_ref)

    mix = (jnp.dot(ys_ref[...], wo_ref[0:512, :], preferred_element_type=F32)
           + jnp.dot(ym_ref[...], wo_ref[512:1024, :], preferred_element_type=F32))
    x1 = x_ref[...] + mix
    x1_ref[...] = x1
    h2 = _rms(x1) * g_ref[...]
    h2_ref[...] = _pack_rows(h2)

    h_hi = h2.astype(BF16)
    h_lo = (h2 - h_hi.astype(F32)).astype(BF16)
    hh = jnp.dot(h_hi, wr_ref[...], preferred_element_type=F32)
    lh = jnp.dot(h_lo, wr_ref[:, 0:LANES], preferred_element_type=F32)
    logits = hh[:, 0:LANES] + (hh[:, LANES:2 * LANES] + lh) + br_ref[...]
    lt = logits.T[0:N_EXPERTS, :]
    eid = lax.broadcasted_iota(I32, (N_EXPERTS, 1), 0).astype(F32)

    vals, idxs, hots = [], [], []
    for _ in range(TOP_K):
        mx = jnp.max(lt, axis=0, keepdims=True)
        idx = jnp.min(jnp.where(lt == mx, eid, float(N_EXPERTS)), axis=0, keepdims=True)
        hot = eid == idx
        lt = jnp.where(hot, -jnp.inf, lt)
        vals.append(mx)
        idxs.append(idx)
        hots.append(hot)
    exps = [jnp.exp(v - vals[0]) for v in vals]
    denom = exps[0] + exps[1] + exps[2] + exps[3]

    multi_f = jnp.where(hots[0] | hots[1] | hots[2] | hots[3], 1.0, 0.0)
    r = lax.broadcasted_iota(I32, (tm, tm), 0)
    c = lax.broadcasted_iota(I32, (tm, tm), 1)
    earlier = jnp.where(r < c, 1.0, 0.0).astype(BF16)
    before = jnp.dot(multi_f.astype(BF16), earlier, preferred_element_type=F32) + cnt_ref[:, 0:1]
    cnt_ref[...] = cnt_ref[...] + jnp.sum(multi_f, axis=1, keepdims=True)

    ranks = [jnp.sum(jnp.where(hots[kk], before, 0.0), axis=0, keepdims=True) for kk in range(TOP_K)]
    gates = [e / denom for e in exps]
    route_ref[...] = jnp.concatenate(idxs + gates + ranks + [jnp.zeros((ROUTE_ROWS - 3 * TOP_K, tm), F32)], axis=0)


def _outproj(x2, y_ssd, y_mla, wo, g_ffn, wr, br):
    t = x2.shape[0]
    tm = TM_PROJ
    full = lambda shape: pl.BlockSpec(shape, lambda i: (0,) * len(shape))
    rows = lambda width: pl.BlockSpec((tm, width), lambda i: (i, 0))
    return pl.pallas_call(
        _outproj_kernel,
        grid=(t // tm,),
        in_specs=[rows(D_MODEL), rows(512), rows(512), full((1024, D_MODEL)), full((1, D_MODEL)),
                  full((D_MODEL, 2 * LANES)), full((1, LANES))],
        out_specs=[rows(D_MODEL), rows(HALF), pl.BlockSpec((ROUTE_ROWS, tm), lambda i: (0, i)),
                   full((N_EXPERTS, LANES))],
        out_shape=[jax.ShapeDtypeStruct((t, D_MODEL), F32),
                   jax.ShapeDtypeStruct((t, HALF), I32),
                   jax.ShapeDtypeStruct((ROUTE_ROWS, t), F32),
                   jax.ShapeDtypeStruct((N_EXPERTS, LANES), F32)],
        compiler_params=pltpu.CompilerParams(
            dimension_semantics=("arbitrary",), vmem_limit_bytes=40 * 1024 * 1024),
        name="outproj_router",
    )(x2, y_ssd, y_mla, wo, g_ffn, wr, br)


def _ffn_kernel(be_ref, bv_ref, nx_ref, sl_ref, xs_ref, wgu_hbm, bgu_ref, wd_hbm, bd_ref, ys_ref,
                wgu_st, wd_st, wgu_bf, wd_bf, sem):
    i = pl.program_id(0)
    e = be_ref[i]
    valid = bv_ref[i]
    slot = sl_ref[i]
    first = ((i == 0) | (e != be_ref[jnp.maximum(i - 1, 0)])) & (valid > 0)

    def weight_copies(expert, dst_slot):
        return (pltpu.make_async_copy(wgu_hbm.at[expert], wgu_st.at[dst_slot], sem.at[0, dst_slot]),
                pltpu.make_async_copy(wd_hbm.at[expert], wd_st.at[dst_slot], sem.at[1, dst_slot]))

    @pl.when(i == 0)
    def _():
        for cp in weight_copies(e, slot):
            cp.start()

    @pl.when(first)
    def _():
        for cp in weight_copies(e, slot):
            cp.wait()

        @pl.when(nx_ref[i] >= 0)
        def _():
            for cp in weight_copies(nx_ref[i], 1 - slot):
                cp.start(priority=1)

        wgu_bf[...] = wgu_st[slot].astype(BF16)
        wd_bf[...] = wd_st[slot].astype(BF16)

    def chain(r0, masked):
        rows = pl.ds(r0, MOE_BM)
        xp = xs_ref[rows, :]
        if masked:
            xp = jnp.where(r0 + lax.broadcasted_iota(I32, (MOE_BM, 1), 0) < valid, xp, 0)
        x_lo, x_hi = _unpack_rows(xp)
        gu = (jnp.dot(x_lo.astype(BF16), wgu_bf[0:HALF, :], preferred_element_type=F32)
              + jnp.dot(x_hi.astype(BF16), wgu_bf[HALF:D_MODEL, :], preferred_element_type=F32)
              + bgu_ref[0])
        gate = jnp.minimum(gu[:, :D_FF], SWIGLU_LIMIT)
        up = jnp.clip(gu[:, D_FF:], -SWIGLU_LIMIT, SWIGLU_LIMIT)
        glu = gate * jax.nn.sigmoid(SWIGLU_ALPHA * gate)
        mid = ((up + 1.0) * glu).astype(BF16)
        ys_ref[rows, :] = _pack_rows(jnp.dot(mid, wd_bf[...], preferred_element_type=F32) + bd_ref[0])

    @pl.when(valid == MOE_STEP)
    def _():
        for r0 in range(0, MOE_STEP, MOE_BM):
            chain(r0, False)

    @pl.when(valid < MOE_STEP)
    def _():
        for r0 in range(0, MOE_STEP, MOE_BM):
            pl.when(valid > r0)(functools.partial(chain, r0, True))

            @pl.when(valid <= r0)
            def _():
                ys_ref[pl.ds(r0, MOE_BM), :] = jnp.zeros((MOE_BM, HALF), I32)


def _expert_ffn(block_e, block_valid, block_next, block_slot, xs, wgu, bgu, wd, bd):
    n_slots = xs.shape[0]
    bm = MOE_STEP
    grid_spec = pltpu.PrefetchScalarGridSpec(
        num_scalar_prefetch=4,
        grid=(n_slots // bm,),
        in_specs=[
            pl.BlockSpec((bm, HALF), lambda i, be, bv, nx, sl: (i, 0)),
            pl.BlockSpec(memory_space=pl.ANY),
            pl.BlockSpec((1, 1, 2 * D_FF), lambda i, be, bv, nx, sl: (be[i], 0, 0)),
            pl.BlockSpec(memory_space=pl.ANY),
            pl.BlockSpec((1, 1, D_MODEL), lambda i, be, bv, nx, sl: (be[i], 0, 0)),
        ],
        out_specs=pl.BlockSpec((bm, HALF), lambda i, be, bv, nx, sl: (i, 0)),
        scratch_shapes=[pltpu.VMEM((2, D_MODEL, 2 * D_FF), F32), pltpu.VMEM((2, D_FF, D_MODEL), F32),
                        pltpu.VMEM((D_MODEL, 2 * D_FF), BF16), pltpu.VMEM((D_FF, D_MODEL), BF16),
                        pltpu.SemaphoreType.DMA((2, 2))],
    )
    return pl.pallas_call(
        _ffn_kernel,
        grid_spec=grid_spec,
        out_shape=jax.ShapeDtypeStruct((n_slots, HALF), I32),
        compiler_params=pltpu.CompilerParams(
            dimension_semantics=("arbitrary",), vmem_limit_bytes=56 * 1024 * 1024),
        name="expert_ffn",
    )(block_e, block_valid, block_next, block_slot, xs, wgu, bgu, wd, bd)


def _plan_kernel(sp_ref, route_ref, dest_ref):
    idx = route_ref[0:TOP_K, :]
    rank = route_ref[2 * TOP_K:3 * TOP_K, :]
    start = jnp.zeros(idx.shape, F32)
    for e_i in range(N_EXPERTS):
        start = jnp.where(idx == float(e_i), sp_ref[e_i].astype(F32), start)
    dest_ref[...] = (start + rank).astype(I32)


def _slot_plan(route, start_pad):
    t = route.shape[1]
    tm = TM_PLAN
    grid_spec = pltpu.PrefetchScalarGridSpec(
        num_scalar_prefetch=1,
        grid=(t // tm,),
        in_specs=[pl.BlockSpec((ROUTE_ROWS, tm), lambda i, sp: (0, i))],
        out_specs=pl.BlockSpec((TOP_K, tm), lambda i, sp: (0, i)),
    )
    return pl.pallas_call(
        _plan_kernel,
        grid_spec=grid_spec,
        out_shape=jax.ShapeDtypeStruct((TOP_K, t), I32),
        compiler_params=pltpu.CompilerParams(dimension_semantics=("arbitrary",)),
        name="slot_plan",
    )(start_pad, route)


SC_SCATTER_CHUNK = 32
SC_GATHER_CHUNK = 16


def _sc_workers():
    info = plsc.get_sparse_core_info()
    return info.num_cores, info.num_cores * info.num_subcores


def _sc_scatter_rows(rows, dest_km, n_out):
    t, w = rows.shape
    ch = SC_SCATTER_CHUNK
    n_cores, n_workers = _sc_workers()
    n_chunks = t // n_workers // ch
    mesh = plsc.VectorSubcoreMesh(core_axis_name="c", subcore_axis_name="s")

    @functools.partial(
        pl.kernel, mesh=mesh, out_type=jax.ShapeDtypeStruct((n_out, w), rows.dtype),
        scratch_types=[pltpu.VMEM((TOP_K, n_chunks, ch), I32), pltpu.VMEM((2, ch, w), rows.dtype),
                       pltpu.SemaphoreType.DMA((2,)), pltpu.SemaphoreType.DMA((2,))],
        name="sc_dispatch_scatter")
    def scatter_kernel(rows_hbm, dest_hbm, out_hbm, idx_v, rows_v, sem_in, sem_out):
        wid = lax.axis_index("s") * n_cores + lax.axis_index("c")
        first = wid * n_chunks
        for kk in range(TOP_K):
            pltpu.sync_copy(dest_hbm.at[kk, pl.ds(first, n_chunks)], idx_v.at[kk])

        def load(cc, b):
            return pltpu.make_async_copy(rows_hbm.at[pl.ds((first + cc) * ch, ch)], rows_v.at[b], sem_in.at[b])

        def scatters(cc, b):
            return [pltpu.make_async_copy(rows_v.at[b], out_hbm.at[idx_v.at[kk, cc]], sem_out.at[b])
                    for kk in range(TOP_K)]

        load(0, 0).start()
        load(1, 1).start()

        @pl.loop(0, n_chunks, step=2)
        def _(c):
            for b in range(2):
                load(c + b, b).wait()
                for cp in scatters(c + b, b):
                    cp.start()
            for b in range(2):
                for cp in scatters(c + b, b):
                    cp.wait()

                @pl.when(c + 2 + b < n_chunks)
                def _():
                    load(c + 2 + b, b).start()

    return scatter_kernel(rows, dest_km.reshape(TOP_K, t // ch, ch))


def _sc_gather_rows(table, dest_km):
    _, w = table.shape
    t = dest_km.shape[1]
    ch = SC_GATHER_CHUNK
    n_cores, n_workers = _sc_workers()
    n_chunks = t // n_workers // ch
    mesh = plsc.VectorSubcoreMesh(core_axis_name="c", subcore_axis_name="s")

    @functools.partial(
        pl.kernel, mesh=mesh, out_type=jax.ShapeDtypeStruct((TOP_K, t, w), table.dtype),
        scratch_types=[pltpu.VMEM((TOP_K, n_chunks, ch), I32), pltpu.VMEM((2, TOP_K, ch, w), table.dtype),
                       pltpu.SemaphoreType.DMA((2,)), pltpu.SemaphoreType.DMA((2,))],
        name="sc_combine_gather")
    def gather_kernel(table_hbm, dest_hbm, out_hbm, idx_v, rows_v, sem_in, sem_out):
        wid = lax.axis_index("s") * n_cores + lax.axis_index("c")
        first = wid * n_chunks
        for kk in range(TOP_K):
            pltpu.sync_copy(dest_hbm.at[kk, pl.ds(first, n_chunks)], idx_v.at[kk])

        def gathers(cc, b):
            return [pltpu.make_async_copy(table_hbm.at[idx_v.at[kk, cc]], rows_v.at[b, kk], sem_in.at[b])
                    for kk in range(TOP_K)]

        def stores(cc, b):
            return [pltpu.make_async_copy(rows_v.at[b, kk], out_hbm.at[kk, pl.ds((first + cc) * ch, ch)],
                                          sem_out.at[b]) for kk in range(TOP_K)]

        for b in range(2):
            for cp in gathers(b, b):
                cp.start()

        @pl.loop(0, n_chunks, step=2)
        def _(c):
            for b in range(2):
                for cp in gathers(c + b, b):
                    cp.wait()
                for cp in stores(c + b, b):
                    cp.start()
            for b in range(2):
                for cp in stores(c + b, b):
                    cp.wait()

                @pl.when(c + 2 + b < n_chunks)
                def _():
                    for cp in gathers(c + 2 + b, b):
                        cp.start()

    return gather_kernel(table, dest_km.reshape(TOP_K, t // ch, ch))


def _combine_kernel(x1_ref, yg_ref, route_ref, g_ref, o_ref, *, final_norm):
    tm = x1_ref.shape[0]
    moe_lo = jnp.zeros((tm, HALF), F32)
    moe_hi = jnp.zeros((tm, HALF), F32)
    route_t = jnp.concatenate([route_ref[...], jnp.zeros((LANES - ROUTE_ROWS, tm), F32)], axis=0).T
    for kk in range(TOP_K):
        gate = route_t[:, TOP_K + kk:TOP_K + kk + 1]
        y_lo, y_hi = _unpack_rows(yg_ref[kk])
        moe_lo = moe_lo + gate * y_lo
        moe_hi = moe_hi + gate * y_hi
    acc = x1_ref[...] + jnp.concatenate([moe_lo, moe_hi], axis=1)
    o_ref[...] = _rms(acc) * g_ref[...] if final_norm else acc


def _combine(x1, yg, route, g_final, final_norm, part):
    t = x1.shape[0]
    tm = TM_COMB
    steps = yg.shape[1] // tm
    off = part * steps
    return pl.pallas_call(
        functools.partial(_combine_kernel, final_norm=final_norm),
        grid=(steps,),
        in_specs=[pl.BlockSpec((tm, D_MODEL), lambda i: (off + i, 0)),
                  pl.BlockSpec((TOP_K, tm, HALF), lambda i: (0, i, 0)),
                  pl.BlockSpec((ROUTE_ROWS, tm), lambda i: (0, off + i)),
                  pl.BlockSpec((1, D_MODEL), lambda i: (0, 0))],
        out_specs=pl.BlockSpec((tm, D_MODEL), lambda i: (off + i, 0)),
        out_shape=jax.ShapeDtypeStruct((t, D_MODEL), F32),
        input_output_aliases={0: 0},
        compiler_params=pltpu.CompilerParams(dimension_semantics=("arbitrary",)),
        name="combine",
    )(x1, yg, route, g_final)


def _prep_in_weights(w_in, w_uq, w_ukv):
    w_z = w_in[:, 0:512]
    w_xbc = w_in[:, 512:1536]
    w_dt = w_in[:, 1536:1544]
    w_cq = w_in[:, 1544:1800]
    w_ckv = w_in[:, 1800:1928]
    w_kr = w_in[:, 1928:1960]
    half = MLA_ROPE // 2
    zeros = lambda rows, width: jnp.zeros(rows + (width,), BF16)
    cat = lambda parts: jnp.concatenate([p.astype(BF16) for p in parts], axis=-1)
    d = (D_MODEL,)
    misc1 = [w_dt, zeros(d, MLA_NOPE - SSD_HEADS), w_kr, zeros(d, LANES - MLA_QK)]
    misc2 = [zeros(d, MLA_NOPE), w_kr[:, half:], w_kr[:, :half], zeros(d, LANES - MLA_QK)]
    w1 = cat([w_z, w_xbc, w_cq, w_ckv] + misc1 + misc2)

    wq3 = w_uq.reshape(MLA_Q_RANK, MLA_HEADS, MLA_QK)
    qh = (MLA_Q_RANK, MLA_HEADS)
    main = cat([wq3, zeros(qh, LANES - MLA_QK)])
    swap = cat([zeros(qh, MLA_NOPE), wq3[:, :, MLA_NOPE + half:], wq3[:, :, MLA_NOPE:MLA_NOPE + half],
                zeros(qh, LANES - MLA_QK)])
    wq = jnp.concatenate([main.reshape(MLA_Q_RANK, -1), swap.reshape(MLA_Q_RANK, -1)], axis=1)

    wkv3 = w_ukv.reshape(MLA_KV_RANK, MLA_HEADS, MLA_NOPE + MLA_V)
    kh = (MLA_KV_RANK, MLA_HEADS)
    kpart = cat([wkv3[:, :, :MLA_NOPE], zeros(kh, LANES - MLA_NOPE)])
    vpart = wkv3[:, :, MLA_NOPE:].astype(BF16)
    wkv = jnp.concatenate([kpart.reshape(MLA_KV_RANK, -1), vpart.reshape(MLA_KV_RANK, -1)], axis=1)
    return w1, wq, wkv


def _rope_inv_freq():
    inv_freq = ROPE_THETA ** (-jnp.arange(0, MLA_ROPE, 2, dtype=F32) / MLA_ROPE)
    return inv_freq[:, None]


def _pad_lanes(v, fill=0.0):
    return jnp.full((1, LANES), fill, F32).at[0, :v.shape[0]].set(v)


def kernel(x, positions, norm_mix_g, w_in, conv_w, conv_b, dt_bias, a_log, d_skip, ssd_norm_g, q_norm_g, w_uq, kv_norm_g, w_ukv, w_out, norm_ffn_g, w_router, b_router, w_gate_up, b_gate_up, w_down, b_down, norm_final_g):
    bsz, seqlen, d = x.shape
    t = bsz * seqlen
    depth = w_in.shape[0]
    x2 = x.reshape(t, d)
    pos_rows = positions.reshape(t // TM_PROJ, 1, TM_PROJ).astype(I32)
    invf = _rope_inv_freq()

    for l in range(depth):
        w1, wq, wkv = _prep_in_weights(w_in[l], w_uq[l], w_ukv[l])
        z, xbc, dtm, q, k, v = _inproj(
            x2, pos_rows, norm_mix_g[l][None, :], w1, q_norm_g[l][None, :], wq, kv_norm_g[l][None, :], wkv,
            invf, bsz, seqlen)
        y_ssd = _ssd(z, xbc, dtm, conv_w[l], conv_b[l][None, :], _pad_lanes(dt_bias[l]), _pad_lanes(a_log[l]),
                     jnp.repeat(d_skip[l], SSD_HEAD_DIM)[None, :], ssd_norm_g[l][None, :], bsz, seqlen)
        y_mla = _attention(q, k, v, bsz, seqlen).reshape(t, MLA_WIDTH)

        wr = jnp.zeros((d, LANES), F32).at[:, :N_EXPERTS].set(w_router[l])
        wr_hi = wr.astype(BF16)
        wr_lo = (wr - wr_hi.astype(F32)).astype(BF16)
        x1, h2p, route, cnt = _outproj(x2, y_ssd, y_mla, w_out[l].astype(BF16), norm_ffn_g[l][None, :],
                                       jnp.concatenate([wr_hi, wr_lo], axis=1), _pad_lanes(b_router[l]))

        counts = cnt[:, 0].astype(I32)
        padded = ((counts + MOE_STEP - 1) // MOE_STEP) * MOE_STEP
        end_pad = jnp.cumsum(padded)
        start_pad = end_pad - padded
        n_slots = t * TOP_K + N_EXPERTS * MOE_STEP
        n_blocks = n_slots // MOE_STEP
        block_start = jnp.arange(n_blocks, dtype=I32) * MOE_STEP
        block_e = jnp.minimum(jnp.sum(block_start[:, None] >= end_pad[None, :], axis=1), N_EXPERTS - 1).astype(I32)
        eids = jnp.arange(N_EXPERTS, dtype=I32)
        block_hot = block_e[:, None] == eids[None, :]
        per_block = lambda table: jnp.sum(jnp.where(block_hot, table[None, :], 0), axis=1).astype(I32)
        block_valid = jnp.clip(per_block(counts) - (block_start - per_block(start_pad)), 0, MOE_STEP).astype(I32)
        used = counts > 0
        later_used = jnp.where((eids[None, :] > eids[:, None]) & used[None, :], eids[None, :], N_EXPERTS)
        next_used = jnp.min(later_used, axis=1)
        next_used = jnp.where(next_used < N_EXPERTS, next_used, -1).astype(I32)
        stage_slot = ((jnp.cumsum(used.astype(I32)) - 1) & 1).astype(I32)
        dest_km = _slot_plan(route, start_pad.astype(I32))

        xs = _sc_scatter_rows(h2p, dest_km, n_slots)
        ys = _expert_ffn(block_e, block_valid, per_block(next_used), per_block(stage_slot), xs,
                         w_gate_up[l], b_gate_up[l][:, None, :], w_down[l], b_down[l][:, None, :])
        per_part = t // COMBINE_PARTS
        x2 = x1
        for part in range(COMBINE_PARTS):
            yg = _sc_gather_rows(ys, dest_km[:, part * per_part:(part + 1) * per_part])
            x2 = _combine(x2, yg, route, norm_final_g[None, :], l == depth - 1, part)
    return x2.reshape(bsz, seqlen, d)
```

```python
import functools

import jax
import jax.numpy as jnp
import numpy as np
from jax import lax
from jax.experimental import pallas as pl
from jax.experimental.pallas import tpu as pltpu
from jax.experimental.pallas import tpu_sc as plsc

F32 = jnp.float32
BF16 = jnp.bfloat16
I32 = jnp.int32

D_MODEL = 1024
EPS = 1e-6
LANES = 128
V7X_VMEM_BYTES = 64 * 1024 * 1024
VMEM_LIMIT_BYTES = V7X_VMEM_BYTES * 7 // 8

SSD_HEADS = 8
SSD_HEAD_DIM = 64
SSD_WIDTH = 512
SSD_STATE = 128
SSD_CONV = 4
SSD_CHUNK = 128
SSD_CONV_DIM = 1024
CONV_PAD = 8
SSD_CPS = 4

MLA_HEADS = 8
MLA_Q_RANK = 256
MLA_KV_RANK = 128
MLA_NOPE = 64
MLA_ROPE = 32
MLA_V = 64
MLA_QK = MLA_NOPE + MLA_ROPE
MLA_WIDTH = 512
ROPE_THETA = 10000.0
LOG2_E = 1.4426950408889634

N_EXPERTS = 32
TOP_K = 4
D_FF = 1024
SWIGLU_LIMIT = 7.0
SWIGLU_ALPHA = 1.702

IN_W = 512 + 1024 + 256 + 128 + 128 + 128

TM_PROJ = 1024
TQ = 512
TK = 512
ATT_HPS = 4
MOE_BM = 256
MOE_STEP = 1024
TM_COMB = 512
TM_PLAN = 2048
ROUTE_ROWS = 16

NT_DIMS = (((1,), (1,)), ((), ()))


def _rms(x):
    return x * lax.rsqrt(jnp.mean(x * x, axis=-1, keepdims=True) + EPS)


HALF = D_MODEL // 2
HI_MASK = np.int32(-65536)


def _pack_rows(a):
    lo = lax.bitcast_convert_type(a[:, :HALF].astype(BF16).astype(F32), I32)
    hi = lax.bitcast_convert_type(a[:, HALF:].astype(BF16).astype(F32), I32)
    return (hi & HI_MASK) | lax.shift_right_logical(lo, 16)


def _unpack_rows(p):
    lo = lax.bitcast_convert_type(lax.shift_left(p, 16), F32)
    hi = lax.bitcast_convert_type(p & HI_MASK, F32)
    return lo, hi


def _inproj_kernel(x_ref, pos_ref, g_ref, w1_ref, qg_ref, wq_ref, kvg_ref, wkv_ref, invf_ref,
                   z_ref, xbc_ref, dtm_ref, q_ref, k_ref, v_ref):
    x = x_ref[...]
    h = (_rms(x) * g_ref[...]).astype(BF16)
    p = jnp.dot(h, w1_ref[...], preferred_element_type=F32)
    z_ref[...] = p[:, 0:512]
    xbc_ref[...] = p[:, 512:1536]
    cq = p[:, 1536:1792]
    ckv = p[:, 1792:1920]
    m1 = p[:, 1920:2048]
    m2 = p[:, 2048:2176]
    dtm_ref[...] = m1

    lane = lax.broadcasted_iota(I32, (1, LANES), 1)
    tm = x.shape[0]
    ang = invf_ref[...] * pos_ref[0].astype(F32)
    cos_c = jnp.cos(ang)
    sin_c = jnp.sin(ang)
    z_lo = jnp.zeros((MLA_NOPE, tm), F32)
    z_hi = jnp.zeros((LANES - MLA_QK, tm), F32)
    cos_t = jnp.concatenate([z_lo, cos_c, cos_c, z_hi], axis=0).T
    sin_t = jnp.concatenate([z_lo, -sin_c, sin_c, z_hi], axis=0).T
    cosq_t = jnp.where(lane < MLA_NOPE, 1.0, cos_t)
    scale = MLA_QK ** -0.5 * LOG2_E

    cqn = (_rms(cq) * qg_ref[...]).astype(BF16)
    qq = jnp.dot(cqn, wq_ref[...], preferred_element_type=F32)
    ckvn = (_rms(ckv) * kvg_ref[...]).astype(BF16)
    kv = jnp.dot(ckvn, wkv_ref[...], preferred_element_type=F32)
    krot = m1 * cos_t + m2 * sin_t
    for h_i in range(MLA_HEADS):
        lo = h_i * LANES
        qm = qq[:, lo:lo + LANES]
        qs = qq[:, 1024 + lo:1024 + lo + LANES]
        q_ref[0, h_i] = ((qm * cosq_t + qs * sin_t) * scale).astype(BF16)
        k_ref[0, h_i] = (kv[:, lo:lo + LANES] + krot).astype(BF16)
    for pr in range(MLA_HEADS // 2):
        vp = kv[:, 1024 + pr * LANES:1024 + (pr + 1) * LANES]
        v_ref[0, 2 * pr] = jnp.where(lane < MLA_V, vp, 1.0).astype(BF16)
        v_ref[0, 2 * pr + 1] = jnp.where(lane >= MLA_V, vp, 1.0).astype(BF16)


def _inproj(x2, pos_rows, g_mix, w1, qg, wq, kvg, wkv, invf, bsz, seqlen):
    t = x2.shape[0]
    tm = TM_PROJ
    per_b = seqlen // tm
    full = lambda shape: pl.BlockSpec(shape, lambda i: (0,) * len(shape))
    head_spec = pl.BlockSpec((1, MLA_HEADS, tm, LANES), lambda i: (i // per_b, 0, i % per_b, 0))
    head_shape = jax.ShapeDtypeStruct((bsz, MLA_HEADS, seqlen, LANES), BF16)
    return pl.pallas_call(
        _inproj_kernel,
        grid=(t // tm,),
        in_specs=[
            pl.BlockSpec((tm, D_MODEL), lambda i: (i, 0)),
            pl.BlockSpec((1, 1, tm), lambda i: (i, 0, 0)),
            full((1, D_MODEL)), full((D_MODEL, IN_W)),
            full((1, MLA_Q_RANK)), full((MLA_Q_RANK, 2048)),
            full((1, MLA_KV_RANK)), full((MLA_KV_RANK, 1536)),
            full((MLA_ROPE // 2, 1)),
        ],
        out_specs=[
            pl.BlockSpec((tm, 512), lambda i: (i, 0)),
            pl.BlockSpec((tm, 1024), lambda i: (i, 0)),
            pl.BlockSpec((tm, LANES), lambda i: (i, 0)),
            head_spec, head_spec, head_spec,
        ],
        out_shape=[
            jax.ShapeDtypeStruct((t, 512), F32),
            jax.ShapeDtypeStruct((t, 1024), F32),
            jax.ShapeDtypeStruct((t, LANES), F32),
            head_shape, head_shape, head_shape,
        ],
        compiler_params=pltpu.CompilerParams(
            dimension_semantics=("arbitrary",), vmem_limit_bytes=VMEM_LIMIT_BYTES),
        name="inproj",
    )(x2, pos_rows, g_mix, w1, qg, wq, kvg, wkv, invf)


def _ssd_kernel(z_ref, xbc_ref, dtm_ref, cw_ref, cb_ref, dtb_ref, alog_ref, dsk_ref, ng_ref,
                y_ref, ext_ref, st_ref):
    q = SSD_CHUNK
    rows = SSD_CPS * q

    @pl.when(pl.program_id(1) == 0)
    def _():
        ext_ref[0:CONV_PAD, :] = jnp.zeros((CONV_PAD, SSD_CONV_DIM), F32)
        st_ref[...] = jnp.zeros_like(st_ref)

    ext_ref[CONV_PAD:CONV_PAD + rows, :] = xbc_ref[...]

    lane = lax.broadcasted_iota(I32, (1, LANES), 1)
    row = lax.broadcasted_iota(I32, (q, q), 0)
    col = lax.broadcasted_iota(I32, (q, q), 1)
    tril = row >= col
    tril_b = jnp.where(tril, 1.0, 0.0).astype(BF16)
    spread = jnp.where(
        lax.broadcasted_iota(I32, (LANES, SSD_WIDTH), 0)
        == lax.broadcasted_iota(I32, (LANES, SSD_WIDTH), 1) // SSD_HEAD_DIM, 1.0, 0.0).astype(BF16)
    a_neg = -jnp.exp(alog_ref[...]) * LOG2_E

    def split3(v):
        hi = v.astype(BF16)
        r1 = v - hi.astype(F32)
        mid = r1.astype(BF16)
        return hi, mid, (r1 - mid.astype(F32)).astype(BF16)

    def dot3_right(parts, m):
        return sum(jnp.dot(p, m, preferred_element_type=F32) for p in parts)

    def expand(cols):
        return dot3_right(split3(cols), spread)

    for ci in range(SSD_CPS):
        lo = ci * q
        conv = cb_ref[...]
        for kk in range(SSD_CONV):
            off = CONV_PAD + lo - (SSD_CONV - 1) + kk
            conv = conv + cw_ref[kk:kk + 1, :] * ext_ref[off:off + q, :]
        u = conv * jax.nn.sigmoid(conv)
        xs = u[:, 0:512]
        bm = u[:, 512:768]
        cm = u[:, 768:1024]

        xdt = dtm_ref[lo:lo + q, :] + dtb_ref[...]
        dt = jnp.maximum(xdt, 0.0) + jnp.log1p(jnp.exp(-jnp.abs(xdt)))
        adt = jnp.where(lane < SSD_HEADS, dt * a_neg, 0.0)
        cum_col = sum(jnp.dot(tril_b, p, preferred_element_type=F32) for p in split3(adt))
        cum_row = cum_col.T

        dt_e = expand(dt)
        ac_e = expand(cum_col)
        last_e = ac_e[q - 1:q, :]
        xd = xs * dt_e
        w_end = xd * jnp.exp2(last_e - ac_e)
        eac = jnp.exp2(ac_e)
        cdec = jnp.exp2(last_e)

        y_parts = []
        for g in range(2):
            gl = g * 256
            bg = bm[:, g * SSD_STATE:(g + 1) * SSD_STATE]
            cg = cm[:, g * SSD_STATE:(g + 1) * SSD_STATE].astype(BF16)
            scores = lax.dot_general(cg, bg.astype(BF16), NT_DIMS, preferred_element_type=F32)
            bgt = bg.T.astype(BF16)
            sprev = st_ref[g]
            yoff = jnp.dot(cg, sprev.astype(BF16), preferred_element_type=F32)
            st_ref[g] = sprev * cdec[:, gl:gl + 256] + jnp.dot(
                bgt, w_end[:, gl:gl + 256].astype(BF16), preferred_element_type=F32)
            for pr in range(2):
                pl_lo = gl + pr * LANES
                xdp = xd[:, pl_lo:pl_lo + LANES].astype(BF16)
                res = []
                for jj in range(2):
                    h_i = g * 4 + pr * 2 + jj
                    seg = cum_col[:, h_i:h_i + 1] - cum_row[h_i:h_i + 1, :]
                    dec = jnp.exp2(jnp.where(tril, seg, -jnp.inf))
                    res.append(jnp.dot((scores * dec).astype(BF16), xdp, preferred_element_type=F32))
                ydiag = jnp.where(lane < SSD_HEAD_DIM, res[0], res[1])
                y_parts.append(ydiag + yoff[:, pr * LANES:(pr + 1) * LANES] * eac[:, pl_lo:pl_lo + LANES])
        y = jnp.concatenate(y_parts, axis=1) + dsk_ref[...] * xs
        zz = z_ref[lo:lo + q, :]
        y = y * (zz * jax.nn.sigmoid(zz))
        outs = []
        for g in range(2):
            yg = y[:, g * 256:(g + 1) * 256]
            outs.append(_rms(yg))
        y_ref[lo:lo + q, :] = (jnp.concatenate(outs, axis=1) * ng_ref[...]).astype(BF16)

    ext_ref[0:CONV_PAD, :] = ext_ref[rows:rows + CONV_PAD, :]


def _ssd(z, xbc, dtm, cw, cb, dtb, alog, dsk, ng, bsz, seqlen):
    t = z.shape[0]
    q = SSD_CHUNK
    rows = SSD_CPS * q
    nc = seqlen // rows
    full = lambda shape: pl.BlockSpec(shape, lambda b, c: (0,) * len(shape))
    row_spec = lambda width: pl.BlockSpec((rows, width), lambda b, c: (b * nc + c, 0))
    return pl.pallas_call(
        _ssd_kernel,
        grid=(bsz, nc),
        in_specs=[row_spec(512), row_spec(1024), row_spec(LANES),
                  full((SSD_CONV, SSD_CONV_DIM)), full((1, SSD_CONV_DIM)),
                  full((1, LANES)), full((1, LANES)), full((1, SSD_WIDTH)), full((1, SSD_WIDTH))],
        out_specs=row_spec(512),
        out_shape=jax.ShapeDtypeStruct((t, SSD_WIDTH), BF16),
        scratch_shapes=[pltpu.VMEM((rows + CONV_PAD, SSD_CONV_DIM), F32),
                        pltpu.VMEM((2, SSD_STATE, 256), F32)],
        compiler_params=pltpu.CompilerParams(dimension_semantics=("arbitrary", "arbitrary")),
        name="ssd",
    )(z, xbc, dtm, cw, cb, dtb, alog, dsk, ng)


def _attn_kernel(q_ref, k_ref, v_ref, o_ref, acc_ref, s_ref, bmax_ref):
    i = pl.program_id(2)
    lane = lax.broadcasted_iota(I32, (1, LANES), 1)
    acc_ref[...] = jnp.zeros_like(acc_ref)

    def scores(j, slot):
        start = pl.multiple_of(j * TK, TK)
        for hh in range(ATT_HPS):
            kb = k_ref[0, hh, pl.ds(start, TK), :]
            s = lax.dot_general(q_ref[0, hh], kb, NT_DIMS, preferred_element_type=F32)
            s_ref[slot, hh] = s
            bmax_ref[slot, hh] = jnp.broadcast_to(jnp.max(s, axis=-1, keepdims=True), (TQ, LANES))

    def consume(j, slot, m_all, masked):
        start = pl.multiple_of(j * TK, TK)
        new_m = []
        for hh in range(ATT_HPS):
            vb = v_ref[0, hh, pl.ds(start, TK), :]
            s = s_ref[slot, hh]
            if masked:
                r = lax.broadcasted_iota(I32, (TQ, TK), 0)
                c = lax.broadcasted_iota(I32, (TQ, TK), 1)
                s = jnp.where(c <= r, s, -jnp.inf)
                block_max = jnp.broadcast_to(jnp.max(s, axis=-1, keepdims=True), (TQ, LANES))
            else:
                block_max = bmax_ref[slot, hh]
            m_old = m_all[hh]
            m_new = jnp.maximum(m_old, block_max)
            alpha = jnp.exp2(m_old - m_new)
            p = jnp.exp2(s - jnp.tile(m_new, (1, TK // LANES))).astype(BF16)
            acc_ref[hh] = acc_ref[hh] * alpha + jnp.dot(p, vb, preferred_element_type=F32)
            new_m.append(m_new)
        return tuple(new_m)

    def pair(p, m_all):
        scores(2 * p + 1, 1)
        m_all = consume(2 * p, 0, m_all, False)
        scores(2 * p + 2, 0)
        return consume(2 * p + 1, 1, m_all, False)

    def odd_tail(_, m_all):
        scores(i, 1)
        return consume(i - 1, 0, m_all, False)

    def quad(g, m_all):
        return pair(2 * g + 1, pair(2 * g, m_all))

    m0 = jnp.full((TQ, LANES), -jnp.inf, F32)
    scores(0, 0)
    m_all = lax.fori_loop(0, i // 4, quad, (m0,) * ATT_HPS)
    m_all = lax.fori_loop(2 * (i // 4), i // 2, pair, m_all)
    m_all = lax.fori_loop(0, i & 1, odd_tail, m_all)

    @pl.when((i & 1) == 0)
    def _():
        consume(i, 0, m_all, True)

    @pl.when((i & 1) == 1)
    def _():
        consume(i, 1, m_all, True)

    outs = []
    for pr in range(ATT_HPS // 2):
        a0 = acc_ref[2 * pr]
        a1 = acc_ref[2 * pr + 1]
        o0 = a0 / a0[:, MLA_V:MLA_V + 1]
        o1 = a1 / a1[:, 0:1]
        outs.append(jnp.where(lane < MLA_V, o0, o1))
    o_ref[0] = jnp.concatenate(outs, axis=1).astype(BF16)


def _attention(q, k, v, bsz, seqlen):
    nq = seqlen // TQ
    hps = ATT_HPS
    kv_spec = pl.BlockSpec((1, hps, seqlen, LANES), lambda b, p, i: (b, p, 0, 0))
    return pl.pallas_call(
        _attn_kernel,
        grid=(bsz, MLA_HEADS // hps, nq),
        in_specs=[pl.BlockSpec((1, hps, TQ, LANES), lambda b, p, i: (b, p, i, 0)), kv_spec, kv_spec],
        out_specs=pl.BlockSpec((1, TQ, hps * MLA_V), lambda b, p, i: (b, i, p)),
        out_shape=jax.ShapeDtypeStruct((bsz, seqlen, MLA_WIDTH), BF16),
        scratch_shapes=[pltpu.VMEM((hps, TQ, LANES), F32), pltpu.VMEM((2, hps, TQ, TK), F32),
                        pltpu.VMEM((2, hps, TQ, LANES), F32)],
        compiler_params=pltpu.CompilerParams(
            dimension_semantics=("arbitrary", "arbitrary", "arbitrary"),
            vmem_limit_bytes=VMEM_LIMIT_BYTES),
        name="attention",
    )(q, k, v)


def _outproj_kernel(x_ref, ys_ref, ym_ref, wo_ref, g_ref, wr_ref, br_ref,
                    x1_ref, h2_ref, route_ref, cnt_ref):
    tm = x_ref.shape[0]

    @pl.when(pl.program_id(0) == 0)
    def _():
        cnt_ref[...] = jnp.zeros_like(cnt_ref)

    mix = (jnp.dot(ys_ref[...], wo_ref[0:512, :], preferred_element_type=F32)
           + jnp.dot(ym_ref[...], wo_ref[512:1024, :], preferred_element_type=F32))
    x1 = x_ref[...] + mix
    x1_ref[...] = x1
    h2 = _rms(x1) * g_ref[...]
    h2_ref[...] = _pack_rows(h2)

    h_hi = h2.astype(BF16)
    h_lo = (h2 - h_hi.astype(F32)).astype(BF16)
    hh = jnp.dot(h_hi, wr_ref[...], preferred_element_type=F32)
    lh = jnp.dot(h_lo, wr_ref[:, 0:LANES], preferred_element_type=F32)
    logits = hh[:, 0:LANES] + (hh[:, LANES:2 * LANES] + lh) + br_ref[...]
    lt = logits.T[0:N_EXPERTS, :]
    eid = lax.broadcasted_iota(I32, (N_EXPERTS, 1), 0).astype(F32)

    vals, idxs, hots = [], [], []
    for _ in range(TOP_K):
        mx = jnp.max(lt, axis=0, keepdims=True)
        idx = jnp.min(jnp.where(lt == mx, eid, float(N_EXPERTS)), axis=0, keepdims=True)
        hot = eid == idx
        lt = jnp.where(hot, -jnp.inf, lt)
        vals.append(mx)
        idxs.append(idx)
        hots.append(hot)
    exps = [jnp.exp(v - vals[0]) for v in vals]
    denom = exps[0] + exps[1] + exps[2] + exps[3]

    multi_f = jnp.where(hots[0] | hots[1] | hots[2] | hots[3], 1.0, 0.0)
    r = lax.broadcasted_iota(I32, (tm, tm), 0)
    c = lax.broadcasted_iota(I32, (tm, tm), 1)
    earlier = jnp.where(r < c, 1.0, 0.0).astype(BF16)
    before = jnp.dot(multi_f.astype(BF16), earlier, preferred_element_type=F32) + cnt_ref[:, 0:1]
    cnt_ref[...] = cnt_ref[...] + jnp.sum(multi_f, axis=1, keepdims=True)

    ranks = [jnp.sum(jnp.where(hots[kk], before, 0.0), axis=0, keepdims=True) for kk in range(TOP_K)]
    gates = [e / denom for e in exps]
    route_ref[...] = jnp.concatenate(idxs + gates + ranks + [jnp.zeros((ROUTE_ROWS - 3 * TOP_K, tm), F32)], axis=0)


def _outproj(x2, y_ssd, y_mla, wo, g_ffn, wr, br):
    t = x2.shape[0]
    tm = TM_PROJ
    full = lambda shape: pl.BlockSpec(shape, lambda i: (0,) * len(shape))
    rows = lambda width: pl.BlockSpec((tm, width), lambda i: (i, 0))
    return pl.pallas_call(
        _outproj_kernel,
        grid=(t // tm,),
        in_specs=[rows(D_MODEL), rows(512), rows(512), full((1024, D_MODEL)), full((1, D_MODEL)),
                  full((D_MODEL, 2 * LANES)), full((1, LANES))],
        out_specs=[rows(D_MODEL), rows(HALF), pl.BlockSpec((ROUTE_ROWS, tm), lambda i: (0, i)),
                   full((N_EXPERTS, LANES))],
        out_shape=[jax.ShapeDtypeStruct((t, D_MODEL), F32),
                   jax.ShapeDtypeStruct((t, HALF), I32),
                   jax.ShapeDtypeStruct((ROUTE_ROWS, t), F32),
                   jax.ShapeDtypeStruct((N_EXPERTS, LANES), F32)],
        compiler_params=pltpu.CompilerParams(
            dimension_semantics=("arbitrary",), vmem_limit_bytes=VMEM_LIMIT_BYTES),
        name="outproj_router",
    )(x2, y_ssd, y_mla, wo, g_ffn, wr, br)


def _ffn_kernel(be_ref, bv_ref, nx_ref, sl_ref, xs_ref, wgu_hbm, bgu_ref, wd_hbm, bd_ref, ys_ref,
                wgu_st, wd_st, wgu_bf, wd_bf, sem):
    i = pl.program_id(0)
    e = be_ref[i]
    valid = bv_ref[i]
    slot = sl_ref[i]
    first = ((i == 0) | (e != be_ref[jnp.maximum(i - 1, 0)])) & (valid > 0)

    def weight_copies(expert, dst_slot):
        return (pltpu.make_async_copy(wgu_hbm.at[expert], wgu_st.at[dst_slot], sem.at[0, dst_slot]),
                pltpu.make_async_copy(wd_hbm.at[expert], wd_st.at[dst_slot], sem.at[1, dst_slot]))

    @pl.when(i == 0)
    def _():
        for cp in weight_copies(e, slot):
            cp.start()

    @pl.when(first)
    def _():
        for cp in weight_copies(e, slot):
            cp.wait()

        @pl.when(nx_ref[i] >= 0)
        def _():
            for cp in weight_copies(nx_ref[i], 1 - slot):
                cp.start(priority=1)

        wgu_bf[...] = wgu_st[slot].astype(BF16)
        wd_bf[...] = wd_st[slot].astype(BF16)

    def chain(r0, masked):
        rows = pl.ds(r0, MOE_BM)
        xp = xs_ref[rows, :]
        if masked:
            xp = jnp.where(r0 + lax.broadcasted_iota(I32, (MOE_BM, 1), 0) < valid, xp, 0)
        x_lo, x_hi = _unpack_rows(xp)
        gu = (jnp.dot(x_lo.astype(BF16), wgu_bf[0:HALF, :], preferred_element_type=F32)
              + jnp.dot(x_hi.astype(BF16), wgu_bf[HALF:D_MODEL, :], preferred_element_type=F32)
              + bgu_ref[0])
        gate = jnp.minimum(gu[:, :D_FF], SWIGLU_LIMIT)
        up = jnp.clip(gu[:, D_FF:], -SWIGLU_LIMIT, SWIGLU_LIMIT)
        glu = gate * jax.nn.sigmoid(SWIGLU_ALPHA * gate)
        mid = ((up + 1.0) * glu).astype(BF16)
        ys_ref[rows, :] = _pack_rows(jnp.dot(mid, wd_bf[...], preferred_element_type=F32) + bd_ref[0])

    @pl.when(valid == MOE_STEP)
    def _():
        for r0 in range(0, MOE_STEP, MOE_BM):
            chain(r0, False)

    @pl.when(valid < MOE_STEP)
    def _():
        for r0 in range(0, MOE_STEP, MOE_BM):
            pl.when(valid > r0)(functools.partial(chain, r0, True))

            @pl.when(valid <= r0)
            def _():
                ys_ref[pl.ds(r0, MOE_BM), :] = jnp.zeros((MOE_BM, HALF), I32)


def _expert_ffn(block_e, block_valid, block_next, block_slot, xs, wgu, bgu, wd, bd):
    n_slots = xs.shape[0]
    bm = MOE_STEP
    grid_spec = pltpu.PrefetchScalarGridSpec(
        num_scalar_prefetch=4,
        grid=(n_slots // bm,),
        in_specs=[
            pl.BlockSpec((bm, HALF), lambda i, be, bv, nx, sl: (i, 0)),
            pl.BlockSpec(memory_space=pl.ANY),
            pl.BlockSpec((1, 1, 2 * D_FF), lambda i, be, bv, nx, sl: (be[i], 0, 0)),
            pl.BlockSpec(memory_space=pl.ANY),
            pl.BlockSpec((1, 1, D_MODEL), lambda i, be, bv, nx, sl: (be[i], 0, 0)),
        ],
        out_specs=pl.BlockSpec((bm, HALF), lambda i, be, bv, nx, sl: (i, 0)),
        scratch_shapes=[pltpu.VMEM((2, D_MODEL, 2 * D_FF), F32), pltpu.VMEM((2, D_FF, D_MODEL), F32),
                        pltpu.VMEM((D_MODEL, 2 * D_FF), BF16), pltpu.VMEM((D_FF, D_MODEL), BF16),
                        pltpu.SemaphoreType.DMA((2, 2))],
    )
    return pl.pallas_call(
        _ffn_kernel,
        grid_spec=grid_spec,
        out_shape=jax.ShapeDtypeStruct((n_slots, HALF), I32),
        compiler_params=pltpu.CompilerParams(
            dimension_semantics=("arbitrary",), vmem_limit_bytes=VMEM_LIMIT_BYTES),
        name="expert_ffn",
    )(block_e, block_valid, block_next, block_slot, xs, wgu, bgu, wd, bd)


def _plan_kernel(sp_ref, route_ref, dest_ref):
    idx = route_ref[0:TOP_K, :]
    rank = route_ref[2 * TOP_K:3 * TOP_K, :]
    start = jnp.zeros(idx.shape, F32)
    for e_i in range(N_EXPERTS):
        start = jnp.where(idx == float(e_i), sp_ref[e_i].astype(F32), start)
    dest_ref[...] = (start + rank).astype(I32)


def _slot_plan(route, start_pad):
    t = route.shape[1]
    tm = TM_PLAN
    grid_spec = pltpu.PrefetchScalarGridSpec(
        num_scalar_prefetch=1,
        grid=(t // tm,),
        in_specs=[pl.BlockSpec((ROUTE_ROWS, tm), lambda i, sp: (0, i))],
        out_specs=pl.BlockSpec((TOP_K, tm), lambda i, sp: (0, i)),
    )
    return pl.pallas_call(
        _plan_kernel,
        grid_spec=grid_spec,
        out_shape=jax.ShapeDtypeStruct((TOP_K, t), I32),
        compiler_params=pltpu.CompilerParams(dimension_semantics=("arbitrary",)),
        name="slot_plan",
    )(start_pad, route)


SC_SCATTER_CHUNK = 32
SC_GATHER_CHUNK = 16


def _sc_workers():
    info = plsc.get_sparse_core_info()
    return info.num_cores, info.num_cores * info.num_subcores


def _sc_scatter_rows(rows, dest_km, n_out):
    t, w = rows.shape
    ch = SC_SCATTER_CHUNK
    n_cores, n_workers = _sc_workers()
    n_chunks = t // n_workers // ch
    mesh = plsc.VectorSubcoreMesh(core_axis_name="c", subcore_axis_name="s")

    @functools.partial(
        pl.kernel, mesh=mesh, out_type=jax.ShapeDtypeStruct((n_out, w), rows.dtype),
        scratch_types=[pltpu.VMEM((TOP_K, n_chunks, ch), I32), pltpu.VMEM((2, ch, w), rows.dtype),
                       pltpu.SemaphoreType.DMA((2,)), pltpu.SemaphoreType.DMA((2,))],
        name="sc_dispatch_scatter")
    def scatter_kernel(rows_hbm, dest_hbm, out_hbm, idx_v, rows_v, sem_in, sem_out):
        wid = lax.axis_index("s") * n_cores + lax.axis_index("c")
        first = wid * n_chunks
        for kk in range(TOP_K):
            pltpu.sync_copy(dest_hbm.at[kk, pl.ds(first, n_chunks)], idx_v.at[kk])

        def load(cc, b):
            return pltpu.make_async_copy(rows_hbm.at[pl.ds((first + cc) * ch, ch)], rows_v.at[b], sem_in.at[b])

        def scatters(cc, b):
            return [pltpu.make_async_copy(rows_v.at[b], out_hbm.at[idx_v.at[kk, cc]], sem_out.at[b])
                    for kk in range(TOP_K)]

        load(0, 0).start()
        load(1, 1).start()

        @pl.loop(0, n_chunks, step=2)
        def _(c):
            for b in range(2):
                load(c + b, b).wait()
                for cp in scatters(c + b, b):
                    cp.start()
            for b in range(2):
                for cp in scatters(c + b, b):
                    cp.wait()

                @pl.when(c + 2 + b < n_chunks)
                def _():
                    load(c + 2 + b, b).start()

    return scatter_kernel(rows, dest_km.reshape(TOP_K, t // ch, ch))


def _sc_gather_rows(table, dest_km):
    _, w = table.shape
    t = dest_km.shape[1]
    ch = SC_GATHER_CHUNK
    n_cores, n_workers = _sc_workers()
    n_chunks = t // n_workers // ch
    mesh = plsc.VectorSubcoreMesh(core_axis_name="c", subcore_axis_name="s")

    @functools.partial(
        pl.kernel, mesh=mesh, out_type=jax.ShapeDtypeStruct((TOP_K, t, w), table.dtype),
        scratch_types=[pltpu.VMEM((TOP_K, n_chunks, ch), I32), pltpu.VMEM((2, TOP_K, ch, w), table.dtype),
                       pltpu.SemaphoreType.DMA((2,)), pltpu.SemaphoreType.DMA((2,))],
        name="sc_combine_gather")
    def gather_kernel(table_hbm, dest_hbm, out_hbm, idx_v, rows_v, sem_in, sem_out):
        wid = lax.axis_index("s") * n_cores + lax.axis_index("c")
        first = wid * n_chunks
        for kk in range(TOP_K):
            pltpu.sync_copy(dest_hbm.at[kk, pl.ds(first, n_chunks)], idx_v.at[kk])

        def gathers(cc, b):
            return [pltpu.make_async_copy(table_hbm.at[idx_v.at[kk, cc]], rows_v.at[b, kk], sem_in.at[b])
                    for kk in range(TOP_K)]

        def stores(cc, b):
            return [pltpu.make_async_copy(rows_v.at[b, kk], out_hbm.at[kk, pl.ds((first + cc) * ch, ch)],
                                          sem_out.at[b]) for kk in range(TOP_K)]

        for b in range(2):
            for cp in gathers(b, b):
                cp.start()

        @pl.loop(0, n_chunks, step=2)
        def _(c):
            for b in range(2):
                for cp in gathers(c + b, b):
                    cp.wait()
                for cp in stores(c + b, b):
                    cp.start()
            for b in range(2):
                for cp in stores(c + b, b):
                    cp.wait()

                @pl.when(c + 2 + b < n_chunks)
                def _():
                    for cp in gathers(c + 2 + b, b):
                        cp.start()

    return gather_kernel(table, dest_km.reshape(TOP_K, t // ch, ch))


def _combine_kernel(x1_ref, yg_ref, route_ref, g_ref, o_ref, *, final_norm):
    tm = x1_ref.shape[0]
    moe_lo = jnp.zeros((tm, HALF), F32)
    moe_hi = jnp.zeros((tm, HALF), F32)
    route_t = jnp.concatenate([route_ref[...], jnp.zeros((LANES - ROUTE_ROWS, tm), F32)], axis=0).T
    for kk in range(TOP_K):
        gate = route_t[:, TOP_K + kk:TOP_K + kk + 1]
        y_lo, y_hi = _unpack_rows(yg_ref[kk])
        moe_lo = moe_lo + gate * y_lo
        moe_hi = moe_hi + gate * y_hi
    acc = x1_ref[...] + jnp.concatenate([moe_lo, moe_hi], axis=1)
    o_ref[...] = _rms(acc) * g_ref[...] if final_norm else acc


def _combine(x1, yg, route, g_final, final_norm):
    t = x1.shape[0]
    tm = TM_COMB
    return pl.pallas_call(
        functools.partial(_combine_kernel, final_norm=final_norm),
        grid=(t // tm,),
        in_specs=[pl.BlockSpec((tm, D_MODEL), lambda i: (i, 0)),
                  pl.BlockSpec((TOP_K, tm, HALF), lambda i: (0, i, 0)),
                  pl.BlockSpec((ROUTE_ROWS, tm), lambda i: (0, i)),
                  pl.BlockSpec((1, D_MODEL), lambda i: (0, 0))],
        out_specs=pl.BlockSpec((tm, D_MODEL), lambda i: (i, 0)),
        out_shape=jax.ShapeDtypeStruct((t, D_MODEL), F32),
        compiler_params=pltpu.CompilerParams(dimension_semantics=("arbitrary",)),
        name="combine",
    )(x1, yg, route, g_final)


def _prep_in_weights(w_in, w_uq, w_ukv):
    w_z = w_in[:, 0:512]
    w_xbc = w_in[:, 512:1536]
    w_dt = w_in[:, 1536:1544]
    w_cq = w_in[:, 1544:1800]
    w_ckv = w_in[:, 1800:1928]
    w_kr = w_in[:, 1928:1960]
    half = MLA_ROPE // 2
    zeros = lambda rows, width: jnp.zeros(rows + (width,), BF16)
    cat = lambda parts: jnp.concatenate([p.astype(BF16) for p in parts], axis=-1)
    d = (D_MODEL,)
    misc1 = [w_dt, zeros(d, MLA_NOPE - SSD_HEADS), w_kr, zeros(d, LANES - MLA_QK)]
    misc2 = [zeros(d, MLA_NOPE), w_kr[:, half:], w_kr[:, :half], zeros(d, LANES - MLA_QK)]
    w1 = cat([w_z, w_xbc, w_cq, w_ckv] + misc1 + misc2)

    wq3 = w_uq.reshape(MLA_Q_RANK, MLA_HEADS, MLA_QK)
    qh = (MLA_Q_RANK, MLA_HEADS)
    main = cat([wq3, zeros(qh, LANES - MLA_QK)])
    swap = cat([zeros(qh, MLA_NOPE), wq3[:, :, MLA_NOPE + half:], wq3[:, :, MLA_NOPE:MLA_NOPE + half],
                zeros(qh, LANES - MLA_QK)])
    wq = jnp.concatenate([main.reshape(MLA_Q_RANK, -1), swap.reshape(MLA_Q_RANK, -1)], axis=1)

    wkv3 = w_ukv.reshape(MLA_KV_RANK, MLA_HEADS, MLA_NOPE + MLA_V)
    kh = (MLA_KV_RANK, MLA_HEADS)
    kpart = cat([wkv3[:, :, :MLA_NOPE], zeros(kh, LANES - MLA_NOPE)])
    vpart = wkv3[:, :, MLA_NOPE:].astype(BF16)
    wkv = jnp.concatenate([kpart.reshape(MLA_KV_RANK, -1), vpart.reshape(MLA_KV_RANK, -1)], axis=1)
    return w1, wq, wkv


def _rope_inv_freq():
    inv_freq = ROPE_THETA ** (-jnp.arange(0, MLA_ROPE, 2, dtype=F32) / MLA_ROPE)
    return inv_freq[:, None]


def _pad_lanes(v, fill=0.0):
    return jnp.full((1, LANES), fill, F32).at[0, :v.shape[0]].set(v)


def kernel(x, positions, norm_mix_g, w_in, conv_w, conv_b, dt_bias, a_log, d_skip, ssd_norm_g, q_norm_g, w_uq, kv_norm_g, w_ukv, w_out, norm_ffn_g, w_router, b_router, w_gate_up, b_gate_up, w_down, b_down, norm_final_g):
    bsz, seqlen, d = x.shape
    t = bsz * seqlen
    depth = w_in.shape[0]
    x2 = x.reshape(t, d)
    pos_rows = positions.reshape(t // TM_PROJ, 1, TM_PROJ).astype(I32)
    invf = _rope_inv_freq()

    for l in range(depth):
        w1, wq, wkv = _prep_in_weights(w_in[l], w_uq[l], w_ukv[l])
        z, xbc, dtm, q, k, v = _inproj(
            x2, pos_rows, norm_mix_g[l][None, :], w1, q_norm_g[l][None, :], wq, kv_norm_g[l][None, :], wkv,
            invf, bsz, seqlen)
        y_ssd = _ssd(z, xbc, dtm, conv_w[l], conv_b[l][None, :], _pad_lanes(dt_bias[l]), _pad_lanes(a_log[l]),
                     jnp.repeat(d_skip[l], SSD_HEAD_DIM)[None, :], ssd_norm_g[l][None, :], bsz, seqlen)
        y_mla = _attention(q, k, v, bsz, seqlen).reshape(t, MLA_WIDTH)

        wr = jnp.zeros((d, LANES), F32).at[:, :N_EXPERTS].set(w_router[l])
        wr_hi = wr.astype(BF16)
        wr_lo = (wr - wr_hi.astype(F32)).astype(BF16)
        x1, h2p, route, cnt = _outproj(x2, y_ssd, y_mla, w_out[l].astype(BF16), norm_ffn_g[l][None, :],
                                       jnp.concatenate([wr_hi, wr_lo], axis=1), _pad_lanes(b_router[l]))

        counts = cnt[:, 0].astype(I32)
        padded = ((counts + MOE_STEP - 1) // MOE_STEP) * MOE_STEP
        end_pad = jnp.cumsum(padded)
        start_pad = end_pad - padded
        n_slots = t * TOP_K + N_EXPERTS * MOE_STEP
        n_blocks = n_slots // MOE_STEP
        block_start = jnp.arange(n_blocks, dtype=I32) * MOE_STEP
        block_e = jnp.minimum(jnp.sum(block_start[:, None] >= end_pad[None, :], axis=1), N_EXPERTS - 1).astype(I32)
        eids = jnp.arange(N_EXPERTS, dtype=I32)
        block_hot = block_e[:, None] == eids[None, :]
        per_block = lambda table: jnp.sum(jnp.where(block_hot, table[None, :], 0), axis=1).astype(I32)
        block_valid = jnp.clip(per_block(counts) - (block_start - per_block(start_pad)), 0, MOE_STEP).astype(I32)
        used = counts > 0
        later_used = jnp.where((eids[None, :] > eids[:, None]) & used[None, :], eids[None, :], N_EXPERTS)
        next_used = jnp.min(later_used, axis=1)
        next_used = jnp.where(next_used < N_EXPERTS, next_used, -1).astype(I32)
        stage_slot = ((jnp.cumsum(used.astype(I32)) - 1) & 1).astype(I32)
        dest_km = _slot_plan(route, start_pad.astype(I32))

        xs = _sc_scatter_rows(h2p, dest_km, n_slots)
        ys = _expert_ffn(block_e, block_valid, per_block(next_used), per_block(stage_slot), xs,
                         w_gate_up[l], b_gate_up[l][:, None, :], w_down[l], b_down[l][:, None, :])
        yg = _sc_gather_rows(ys, dest_km)
        x2 = _combine(x1, yg, route, norm_final_g[None, :], l == depth - 1)
    return x2.reshape(bsz, seqlen, d)
```

```python
import functools

import jax
import jax.numpy as jnp
import numpy as np
from jax import lax
from jax.experimental import pallas as pl
from jax.experimental.pallas import tpu as pltpu
from jax.experimental.pallas import tpu_sc as plsc

F32 = jnp.float32
BF16 = jnp.bfloat16
I32 = jnp.int32

D_MODEL = 1024
EPS = 1e-6
LANES = 128
V7X_VMEM_BYTES = 64 * 1024 * 1024
VMEM_LIMIT_BYTES = V7X_VMEM_BYTES * 7 // 8

SSD_HEADS = 8
SSD_HEAD_DIM = 64
SSD_WIDTH = 512
SSD_STATE = 128
SSD_CONV = 4
SSD_CHUNK = 128
SSD_CONV_DIM = 1024
CONV_PAD = 8
SSD_CPS = 4

MLA_HEADS = 8
MLA_Q_RANK = 256
MLA_KV_RANK = 128
MLA_NOPE = 64
MLA_ROPE = 32
MLA_V = 64
MLA_QK = MLA_NOPE + MLA_ROPE
MLA_WIDTH = 512
ROPE_THETA = 10000.0
LOG2_E = 1.4426950408889634

N_EXPERTS = 32
TOP_K = 4
D_FF = 1024
SWIGLU_LIMIT = 7.0
SWIGLU_ALPHA = 1.702

IN_W = 512 + 1024 + 256 + 128 + 128 + 128

TM_PROJ = 1024
TQ = 512
TK = 512
ATT_HPS = 4
VT_ROWS = 80
MOE_BM = 256
MOE_STEP = 1024
TM_COMB = 512
TM_PLAN = 2048
ROUTE_ROWS = 16

NT_DIMS = (((1,), (1,)), ((), ()))


def _rms(x):
    return x * lax.rsqrt(jnp.mean(x * x, axis=-1, keepdims=True) + EPS)


HALF = D_MODEL // 2
HI_MASK = np.int32(-65536)


def _pack_rows(a):
    lo = lax.bitcast_convert_type(a[:, :HALF].astype(BF16).astype(F32), I32)
    hi = lax.bitcast_convert_type(a[:, HALF:].astype(BF16).astype(F32), I32)
    return (hi & HI_MASK) | lax.shift_right_logical(lo, 16)


def _unpack_rows(p):
    lo = lax.bitcast_convert_type(lax.shift_left(p, 16), F32)
    hi = lax.bitcast_convert_type(p & HI_MASK, F32)
    return lo, hi


def _inproj_kernel(x_ref, pos_ref, g_ref, w1_ref, qg_ref, wq_ref, kvg_ref, wkv_ref, invf_ref,
                   z_ref, xbc_ref, dtm_ref, q_ref, k_ref, vt_ref):
    x = x_ref[...]
    h = (_rms(x) * g_ref[...]).astype(BF16)
    p = jnp.dot(h, w1_ref[...], preferred_element_type=F32)
    z_ref[...] = p[:, 0:512]
    xbc_ref[...] = p[:, 512:1536]
    cq = p[:, 1536:1792]
    ckv = p[:, 1792:1920]
    m1 = p[:, 1920:2048]
    m2 = p[:, 2048:2176]
    dtm_ref[...] = m1

    lane = lax.broadcasted_iota(I32, (1, LANES), 1)
    tm = x.shape[0]
    ang = invf_ref[...] * pos_ref[0].astype(F32)
    cos_c = jnp.cos(ang)
    sin_c = jnp.sin(ang)
    z_lo = jnp.zeros((MLA_NOPE, tm), F32)
    z_hi = jnp.zeros((LANES - MLA_QK, tm), F32)
    cos_t = jnp.concatenate([z_lo, cos_c, cos_c, z_hi], axis=0).T
    sin_t = jnp.concatenate([z_lo, -sin_c, sin_c, z_hi], axis=0).T
    cosq_t = jnp.where(lane < MLA_NOPE, 1.0, cos_t)
    scale = MLA_QK ** -0.5 * LOG2_E

    cqn = (_rms(cq) * qg_ref[...]).astype(BF16)
    qq = jnp.dot(cqn, wq_ref[...], preferred_element_type=F32)
    ckvn = (_rms(ckv) * kvg_ref[...]).astype(BF16)
    kv = jnp.dot(ckvn, wkv_ref[...], preferred_element_type=F32)
    krot = m1 * cos_t + m2 * sin_t
    for h_i in range(MLA_HEADS):
        lo = h_i * LANES
        qm = qq[:, lo:lo + LANES]
        qs = qq[:, 1024 + lo:1024 + lo + LANES]
        q_ref[0, h_i] = ((qm * cosq_t + qs * sin_t) * scale).astype(BF16)
        k_ref[0, h_i] = (kv[:, lo:lo + LANES] + krot).astype(BF16)
    ones_rows = jnp.ones((VT_ROWS - MLA_V, tm), BF16)
    for pr in range(MLA_HEADS // 2):
        vpt = kv[:, 1024 + pr * LANES:1024 + (pr + 1) * LANES].T.astype(BF16)
        vt_ref[0, 2 * pr] = jnp.concatenate([vpt[0:MLA_V, :], ones_rows], axis=0)
        vt_ref[0, 2 * pr + 1] = jnp.concatenate([vpt[MLA_V:2 * MLA_V, :], ones_rows], axis=0)


def _inproj(x2, pos_rows, g_mix, w1, qg, wq, kvg, wkv, invf, bsz, seqlen):
    t = x2.shape[0]
    tm = TM_PROJ
    per_b = seqlen // tm
    full = lambda shape: pl.BlockSpec(shape, lambda i: (0,) * len(shape))
    head_spec = pl.BlockSpec((1, MLA_HEADS, tm, LANES), lambda i: (i // per_b, 0, i % per_b, 0))
    head_shape = jax.ShapeDtypeStruct((bsz, MLA_HEADS, seqlen, LANES), BF16)
    vt_spec = pl.BlockSpec((1, MLA_HEADS, VT_ROWS, tm), lambda i: (i // per_b, 0, 0, i % per_b))
    vt_shape = jax.ShapeDtypeStruct((bsz, MLA_HEADS, VT_ROWS, seqlen), BF16)
    return pl.pallas_call(
        _inproj_kernel,
        grid=(t // tm,),
        in_specs=[
            pl.BlockSpec((tm, D_MODEL), lambda i: (i, 0)),
            pl.BlockSpec((1, 1, tm), lambda i: (i, 0, 0)),
            full((1, D_MODEL)), full((D_MODEL, IN_W)),
            full((1, MLA_Q_RANK)), full((MLA_Q_RANK, 2048)),
            full((1, MLA_KV_RANK)), full((MLA_KV_RANK, 1536)),
            full((MLA_ROPE // 2, 1)),
        ],
        out_specs=[
            pl.BlockSpec((tm, 512), lambda i: (i, 0)),
            pl.BlockSpec((tm, 1024), lambda i: (i, 0)),
            pl.BlockSpec((tm, LANES), lambda i: (i, 0)),
            head_spec, head_spec, vt_spec,
        ],
        out_shape=[
            jax.ShapeDtypeStruct((t, 512), F32),
            jax.ShapeDtypeStruct((t, 1024), F32),
            jax.ShapeDtypeStruct((t, LANES), F32),
            head_shape, head_shape, vt_shape,
        ],
        compiler_params=pltpu.CompilerParams(
            dimension_semantics=("arbitrary",), vmem_limit_bytes=VMEM_LIMIT_BYTES),
        name="inproj",
    )(x2, pos_rows, g_mix, w1, qg, wq, kvg, wkv, invf)


def _ssd_kernel(z_ref, xbc_ref, dtm_ref, cw_ref, cb_ref, dtb_ref, alog_ref, dsk_ref, ng_ref,
                y_ref, ext_ref, st_ref):
    q = SSD_CHUNK
    rows = SSD_CPS * q

    @pl.when(pl.program_id(1) == 0)
    def _():
        ext_ref[0:CONV_PAD, :] = jnp.zeros((CONV_PAD, SSD_CONV_DIM), F32)
        st_ref[...] = jnp.zeros_like(st_ref)

    ext_ref[CONV_PAD:CONV_PAD + rows, :] = xbc_ref[...]

    lane = lax.broadcasted_iota(I32, (1, LANES), 1)
    row = lax.broadcasted_iota(I32, (q, q), 0)
    col = lax.broadcasted_iota(I32, (q, q), 1)
    tril = row >= col
    tril_b = jnp.where(tril, 1.0, 0.0).astype(BF16)
    spread = jnp.where(
        lax.broadcasted_iota(I32, (LANES, SSD_WIDTH), 0)
        == lax.broadcasted_iota(I32, (LANES, SSD_WIDTH), 1) // SSD_HEAD_DIM, 1.0, 0.0).astype(BF16)
    a_neg = -jnp.exp(alog_ref[...]) * LOG2_E

    def split3(v):
        hi = v.astype(BF16)
        r1 = v - hi.astype(F32)
        mid = r1.astype(BF16)
        return hi, mid, (r1 - mid.astype(F32)).astype(BF16)

    def dot3_right(parts, m):
        return sum(jnp.dot(p, m, preferred_element_type=F32) for p in parts)

    def expand(cols):
        return dot3_right(split3(cols), spread)

    for ci in range(SSD_CPS):
        lo = ci * q
        conv = cb_ref[...]
        for kk in range(SSD_CONV):
            off = CONV_PAD + lo - (SSD_CONV - 1) + kk
            conv = conv + cw_ref[kk:kk + 1, :] * ext_ref[off:off + q, :]
        u = conv * jax.nn.sigmoid(conv)
        xs = u[:, 0:512]
        bm = u[:, 512:768]
        cm = u[:, 768:1024]

        xdt = dtm_ref[lo:lo + q, :] + dtb_ref[...]
        dt = jnp.maximum(xdt, 0.0) + jnp.log1p(jnp.exp(-jnp.abs(xdt)))
        adt = jnp.where(lane < SSD_HEADS, dt * a_neg, 0.0)
        cum_col = sum(jnp.dot(tril_b, p, preferred_element_type=F32) for p in split3(adt))
        cum_row = cum_col.T

        dt_e = expand(dt)
        ac_e = expand(cum_col)
        last_e = ac_e[q - 1:q, :]
        xd = xs * dt_e
        w_end = xd * jnp.exp2(last_e - ac_e)
        eac = jnp.exp2(ac_e)
        cdec = jnp.exp2(last_e)

        y_parts = []
        for g in range(2):
            gl = g * 256
            bg = bm[:, g * SSD_STATE:(g + 1) * SSD_STATE]
            cg = cm[:, g * SSD_STATE:(g + 1) * SSD_STATE].astype(BF16)
            scores = lax.dot_general(cg, bg.astype(BF16), NT_DIMS, preferred_element_type=F32)
            bgt = bg.T.astype(BF16)
            sprev = st_ref[g]
            yoff = jnp.dot(cg, sprev.astype(BF16), preferred_element_type=F32)
            st_ref[g] = sprev * cdec[:, gl:gl + 256] + jnp.dot(
                bgt, w_end[:, gl:gl + 256].astype(BF16), preferred_element_type=F32)
            for pr in range(2):
                pl_lo = gl + pr * LANES
                xdp = xd[:, pl_lo:pl_lo + LANES].astype(BF16)
                res = []
                for jj in range(2):
                    h_i = g * 4 + pr * 2 + jj
                    seg = cum_col[:, h_i:h_i + 1] - cum_row[h_i:h_i + 1, :]
                    dec = jnp.exp2(jnp.where(tril, seg, -jnp.inf))
                    res.append(jnp.dot((scores * dec).astype(BF16), xdp, preferred_element_type=F32))
                ydiag = jnp.where(lane < SSD_HEAD_DIM, res[0], res[1])
                y_parts.append(ydiag + yoff[:, pr * LANES:(pr + 1) * LANES] * eac[:, pl_lo:pl_lo + LANES])
        y = jnp.concatenate(y_parts, axis=1) + dsk_ref[...] * xs
        zz = z_ref[lo:lo + q, :]
        y = y * (zz * jax.nn.sigmoid(zz))
        outs = []
        for g in range(2):
            yg = y[:, g * 256:(g + 1) * 256]
            outs.append(_rms(yg))
        y_ref[lo:lo + q, :] = (jnp.concatenate(outs, axis=1) * ng_ref[...]).astype(BF16)

    ext_ref[0:CONV_PAD, :] = ext_ref[rows:rows + CONV_PAD, :]


def _ssd(z, xbc, dtm, cw, cb, dtb, alog, dsk, ng, bsz, seqlen):
    t = z.shape[0]
    q = SSD_CHUNK
    rows = SSD_CPS * q
    nc = seqlen // rows
    full = lambda shape: pl.BlockSpec(shape, lambda b, c: (0,) * len(shape))
    row_spec = lambda width: pl.BlockSpec((rows, width), lambda b, c: (b * nc + c, 0))
    return pl.pallas_call(
        _ssd_kernel,
        grid=(bsz, nc),
        in_specs=[row_spec(512), row_spec(1024), row_spec(LANES),
                  full((SSD_CONV, SSD_CONV_DIM)), full((1, SSD_CONV_DIM)),
                  full((1, LANES)), full((1, LANES)), full((1, SSD_WIDTH)), full((1, SSD_WIDTH))],
        out_specs=row_spec(512),
        out_shape=jax.ShapeDtypeStruct((t, SSD_WIDTH), BF16),
        scratch_shapes=[pltpu.VMEM((rows + CONV_PAD, SSD_CONV_DIM), F32),
                        pltpu.VMEM((2, SSD_STATE, 256), F32)],
        compiler_params=pltpu.CompilerParams(dimension_semantics=("arbitrary", "arbitrary")),
        name="ssd",
    )(z, xbc, dtm, cw, cb, dtb, alog, dsk, ng)


def _attn_kernel(q_ref, k_ref, vt_ref, o_ref, acc_ref, s_ref, bmax_ref):
    i = pl.program_id(2)
    acc_ref[...] = jnp.zeros_like(acc_ref)

    def col_max(st):
        part = st[0:LANES, :]
        for r0 in range(LANES, TK, LANES):
            part = jnp.maximum(part, st[r0:r0 + LANES, :])
        return jnp.broadcast_to(jnp.max(part, axis=0, keepdims=True), (8, TQ))

    def scores(j, slot):
        start = pl.multiple_of(j * TK, TK)
        for hh in range(ATT_HPS):
            kb = k_ref[0, hh, pl.ds(start, TK), :]
            st = lax.dot_general(kb, q_ref[0, hh], NT_DIMS, preferred_element_type=F32)
            s_ref[slot, hh] = st
            bmax_ref[slot, hh] = col_max(st)

    def consume(j, slot, m_all, masked):
        start = pl.multiple_of(j * TK, TK)
        new_m = []
        for hh in range(ATT_HPS):
            vt = vt_ref[0, hh, :, pl.ds(start, TK)]
            st = s_ref[slot, hh]
            if masked:
                key = lax.broadcasted_iota(I32, (TK, TQ), 0)
                qry = lax.broadcasted_iota(I32, (TK, TQ), 1)
                st = jnp.where(key <= qry, st, -jnp.inf)
                block_max = col_max(st)
            else:
                block_max = bmax_ref[slot, hh]
            m_old = m_all[hh]
            m_new = jnp.maximum(m_old, block_max)
            alpha = jnp.exp2(m_old - m_new)
            p = jnp.exp2(st - m_new[0:1, :]).astype(BF16)
            acc_ref[hh] = acc_ref[hh] * alpha[0:1, :] + jnp.dot(vt, p, preferred_element_type=F32)
            new_m.append(m_new)
        return tuple(new_m)

    def pair(p, m_all):
        scores(2 * p + 1, 1)
        m_all = consume(2 * p, 0, m_all, False)
        scores(2 * p + 2, 0)
        return consume(2 * p + 1, 1, m_all, False)

    def odd_tail(_, m_all):
        scores(i, 1)
        return consume(i - 1, 0, m_all, False)

    def quad(g, m_all):
        return pair(2 * g + 1, pair(2 * g, m_all))

    m0 = jnp.full((8, TQ), -jnp.inf, F32)
    scores(0, 0)
    m_all = lax.fori_loop(0, i // 4, quad, (m0,) * ATT_HPS)
    m_all = lax.fori_loop(2 * (i // 4), i // 2, pair, m_all)
    m_all = lax.fori_loop(0, i & 1, odd_tail, m_all)

    @pl.when((i & 1) == 0)
    def _():
        consume(i, 0, m_all, True)

    @pl.when((i & 1) == 1)
    def _():
        consume(i, 1, m_all, True)

    outs = []
    for pr in range(ATT_HPS // 2):
        pair_t = []
        for hh in (2 * pr, 2 * pr + 1):
            a = acc_ref[hh]
            pair_t.append(a[0:MLA_V, :] / a[MLA_V:MLA_V + 1, :])
        outs.append(jnp.concatenate(pair_t, axis=0).T)
    o_ref[0] = jnp.concatenate(outs, axis=1).astype(BF16)


def _attention(q, k, vt, bsz, seqlen):
    nq = seqlen // TQ
    hps = ATT_HPS
    return pl.pallas_call(
        _attn_kernel,
        grid=(bsz, MLA_HEADS // hps, nq),
        in_specs=[pl.BlockSpec((1, hps, TQ, LANES), lambda b, p, i: (b, p, i, 0)),
                  pl.BlockSpec((1, hps, seqlen, LANES), lambda b, p, i: (b, p, 0, 0)),
                  pl.BlockSpec((1, hps, VT_ROWS, seqlen), lambda b, p, i: (b, p, 0, 0))],
        out_specs=pl.BlockSpec((1, TQ, hps * MLA_V), lambda b, p, i: (b, i, p)),
        out_shape=jax.ShapeDtypeStruct((bsz, seqlen, MLA_WIDTH), BF16),
        scratch_shapes=[pltpu.VMEM((hps, VT_ROWS, TQ), F32), pltpu.VMEM((2, hps, TK, TQ), F32),
                        pltpu.VMEM((2, hps, 8, TQ), F32)],
        compiler_params=pltpu.CompilerParams(
            dimension_semantics=("arbitrary", "arbitrary", "arbitrary"),
            vmem_limit_bytes=VMEM_LIMIT_BYTES),
        name="attention",
    )(q, k, vt)


def _outproj_kernel(x_ref, ys_ref, ym_ref, wo_ref, g_ref, wr_ref, br_ref,
                    x1_ref, h2_ref, route_ref, cnt_ref):
    tm = x_ref.shape[0]

    @pl.when(pl.program_id(0) == 0)
    def _():
        cnt_ref[...] = jnp.zeros_like(cnt_ref)

    mix = (jnp.dot(ys_ref[...], wo_ref[0:512, :], preferred_element_type=F32)
           + jnp.dot(ym_ref[...], wo_ref[512:1024, :], preferred_element_type=F32))
    x1 = x_ref[...] + mix
    x1_ref[...] = x1
    h2 = _rms(x1) * g_ref[...]
    h2_ref[...] = _pack_rows(h2)

    h_hi = h2.astype(BF16)
    h_lo = (h2 - h_hi.astype(F32)).astype(BF16)
    hh = jnp.dot(h_hi, wr_ref[...], preferred_element_type=F32)
    lh = jnp.dot(h_lo, wr_ref[:, 0:LANES], preferred_element_type=F32)
    logits = hh[:, 0:LANES] + (hh[:, LANES:2 * LANES] + lh) + br_ref[...]
    lt = logits.T[0:N_EXPERTS, :]
    eid = lax.broadcasted_iota(I32, (N_EXPERTS, 1), 0).astype(F32)

    vals, idxs, hots = [], [], []
    for _ in range(TOP_K):
        mx = jnp.max(lt, axis=0, keepdims=True)
        idx = jnp.min(jnp.where(lt == mx, eid, float(N_EXPERTS)), axis=0, keepdims=True)
        hot = eid == idx
        lt = jnp.where(hot, -jnp.inf, lt)
        vals.append(mx)
        idxs.append(idx)
        hots.append(hot)
    exps = [jnp.exp(v - vals[0]) for v in vals]
    denom = exps[0] + exps[1] + exps[2] + exps[3]

    multi_f = jnp.where(hots[0] | hots[1] | hots[2] | hots[3], 1.0, 0.0)
    r = lax.broadcasted_iota(I32, (tm, tm), 0)
    c = lax.broadcasted_iota(I32, (tm, tm), 1)
    earlier = jnp.where(r < c, 1.0, 0.0).astype(BF16)
    before = jnp.dot(multi_f.astype(BF16), earlier, preferred_element_type=F32) + cnt_ref[:, 0:1]
    cnt_ref[...] = cnt_ref[...] + jnp.sum(multi_f, axis=1, keepdims=True)

    ranks = [jnp.sum(jnp.where(hots[kk], before, 0.0), axis=0, keepdims=True) for kk in range(TOP_K)]
    gates = [e / denom for e in exps]
    route_ref[...] = jnp.concatenate(idxs + gates + ranks + [jnp.zeros((ROUTE_ROWS - 3 * TOP_K, tm), F32)], axis=0)


def _outproj(x2, y_ssd, y_mla, wo, g_ffn, wr, br):
    t = x2.shape[0]
    tm = TM_PROJ
    full = lambda shape: pl.BlockSpec(shape, lambda i: (0,) * len(shape))
    rows = lambda width: pl.BlockSpec((tm, width), lambda i: (i, 0))
    return pl.pallas_call(
        _outproj_kernel,
        grid=(t // tm,),
        in_specs=[rows(D_MODEL), rows(512), rows(512), full((1024, D_MODEL)), full((1, D_MODEL)),
                  full((D_MODEL, 2 * LANES)), full((1, LANES))],
        out_specs=[rows(D_MODEL), rows(HALF), pl.BlockSpec((ROUTE_ROWS, tm), lambda i: (0, i)),
                   full((N_EXPERTS, LANES))],
        out_shape=[jax.ShapeDtypeStruct((t, D_MODEL), F32),
                   jax.ShapeDtypeStruct((t, HALF), I32),
                   jax.ShapeDtypeStruct((ROUTE_ROWS, t), F32),
                   jax.ShapeDtypeStruct((N_EXPERTS, LANES), F32)],
        compiler_params=pltpu.CompilerParams(
            dimension_semantics=("arbitrary",), vmem_limit_bytes=VMEM_LIMIT_BYTES),
        name="outproj_router",
    )(x2, y_ssd, y_mla, wo, g_ffn, wr, br)


def _ffn_kernel(be_ref, bv_ref, nx_ref, sl_ref, xs_ref, wgu_hbm, bgu_ref, wd_hbm, bd_ref, ys_ref,
                wgu_st, wd_st, wgu_bf, wd_bf, sem):
    i = pl.program_id(0)
    e = be_ref[i]
    valid = bv_ref[i]
    slot = sl_ref[i]
    first = ((i == 0) | (e != be_ref[jnp.maximum(i - 1, 0)])) & (valid > 0)

    def weight_copies(expert, dst_slot):
        return (pltpu.make_async_copy(wgu_hbm.at[expert], wgu_st.at[dst_slot], sem.at[0, dst_slot]),
                pltpu.make_async_copy(wd_hbm.at[expert], wd_st.at[dst_slot], sem.at[1, dst_slot]))

    @pl.when(i == 0)
    def _():
        for cp in weight_copies(e, slot):
            cp.start()

    @pl.when(first)
    def _():
        for cp in weight_copies(e, slot):
            cp.wait()

        @pl.when(nx_ref[i] >= 0)
        def _():
            for cp in weight_copies(nx_ref[i], 1 - slot):
                cp.start(priority=1)

        wgu_bf[...] = wgu_st[slot].astype(BF16)
        wd_bf[...] = wd_st[slot].astype(BF16)

    def chain(r0, masked):
        rows = pl.ds(r0, MOE_BM)
        xp = xs_ref[rows, :]
        if masked:
            xp = jnp.where(r0 + lax.broadcasted_iota(I32, (MOE_BM, 1), 0) < valid, xp, 0)
        x_lo, x_hi = _unpack_rows(xp)
        gu = (jnp.dot(x_lo.astype(BF16), wgu_bf[0:HALF, :], preferred_element_type=F32)
              + jnp.dot(x_hi.astype(BF16), wgu_bf[HALF:D_MODEL, :], preferred_element_type=F32)
              + bgu_ref[0])
        gate = jnp.minimum(gu[:, :D_FF], SWIGLU_LIMIT)
        up = jnp.clip(gu[:, D_FF:], -SWIGLU_LIMIT, SWIGLU_LIMIT)
        glu = gate * jax.nn.sigmoid(SWIGLU_ALPHA * gate)
        mid = ((up + 1.0) * glu).astype(BF16)
        ys_ref[rows, :] = _pack_rows(jnp.dot(mid, wd_bf[...], preferred_element_type=F32) + bd_ref[0])

    @pl.when(valid == MOE_STEP)
    def _():
        for r0 in range(0, MOE_STEP, MOE_BM):
            chain(r0, False)

    @pl.when(valid < MOE_STEP)
    def _():
        for r0 in range(0, MOE_STEP, MOE_BM):
            pl.when(valid > r0)(functools.partial(chain, r0, True))

            @pl.when(valid <= r0)
            def _():
                ys_ref[pl.ds(r0, MOE_BM), :] = jnp.zeros((MOE_BM, HALF), I32)


def _expert_ffn(block_e, block_valid, block_next, block_slot, xs, wgu, bgu, wd, bd):
    n_slots = xs.shape[0]
    bm = MOE_STEP
    grid_spec = pltpu.PrefetchScalarGridSpec(
        num_scalar_prefetch=4,
        grid=(n_slots // bm,),
        in_specs=[
            pl.BlockSpec((bm, HALF), lambda i, be, bv, nx, sl: (i, 0)),
            pl.BlockSpec(memory_space=pl.ANY),
            pl.BlockSpec((1, 1, 2 * D_FF), lambda i, be, bv, nx, sl: (be[i], 0, 0)),
            pl.BlockSpec(memory_space=pl.ANY),
            pl.BlockSpec((1, 1, D_MODEL), lambda i, be, bv, nx, sl: (be[i], 0, 0)),
        ],
        out_specs=pl.BlockSpec((bm, HALF), lambda i, be, bv, nx, sl: (i, 0)),
        scratch_shapes=[pltpu.VMEM((2, D_MODEL, 2 * D_FF), F32), pltpu.VMEM((2, D_FF, D_MODEL), F32),
                        pltpu.VMEM((D_MODEL, 2 * D_FF), BF16), pltpu.VMEM((D_FF, D_MODEL), BF16),
                        pltpu.SemaphoreType.DMA((2, 2))],
    )
    return pl.pallas_call(
        _ffn_kernel,
        grid_spec=grid_spec,
        out_shape=jax.ShapeDtypeStruct((n_slots, HALF), I32),
        compiler_params=pltpu.CompilerParams(
            dimension_semantics=("arbitrary",), vmem_limit_bytes=VMEM_LIMIT_BYTES),
        name="expert_ffn",
    )(block_e, block_valid, block_next, block_slot, xs, wgu, bgu, wd, bd)


def _plan_kernel(sp_ref, route_ref, dest_ref):
    idx = route_ref[0:TOP_K, :]
    rank = route_ref[2 * TOP_K:3 * TOP_K, :]
    start = jnp.zeros(idx.shape, F32)
    for e_i in range(N_EXPERTS):
        start = jnp.where(idx == float(e_i), sp_ref[e_i].astype(F32), start)
    dest_ref[...] = (start + rank).astype(I32)


def _slot_plan(route, start_pad):
    t = route.shape[1]
    tm = TM_PLAN
    grid_spec = pltpu.PrefetchScalarGridSpec(
        num_scalar_prefetch=1,
        grid=(t // tm,),
        in_specs=[pl.BlockSpec((ROUTE_ROWS, tm), lambda i, sp: (0, i))],
        out_specs=pl.BlockSpec((TOP_K, tm), lambda i, sp: (0, i)),
    )
    return pl.pallas_call(
        _plan_kernel,
        grid_spec=grid_spec,
        out_shape=jax.ShapeDtypeStruct((TOP_K, t), I32),
        compiler_params=pltpu.CompilerParams(dimension_semantics=("arbitrary",)),
        name="slot_plan",
    )(start_pad, route)


SC_SCATTER_CHUNK = 32
SC_GATHER_CHUNK = 16


def _sc_workers():
    info = plsc.get_sparse_core_info()
    return info.num_cores, info.num_cores * info.num_subcores


def _sc_scatter_rows(rows, dest_km, n_out):
    t, w = rows.shape
    ch = SC_SCATTER_CHUNK
    n_cores, n_workers = _sc_workers()
    n_chunks = t // n_workers // ch
    mesh = plsc.VectorSubcoreMesh(core_axis_name="c", subcore_axis_name="s")

    @functools.partial(
        pl.kernel, mesh=mesh, out_type=jax.ShapeDtypeStruct((n_out, w), rows.dtype),
        scratch_types=[pltpu.VMEM((TOP_K, n_chunks, ch), I32), pltpu.VMEM((2, ch, w), rows.dtype),
                       pltpu.SemaphoreType.DMA((2,)), pltpu.SemaphoreType.DMA((2,))],
        name="sc_dispatch_scatter")
    def scatter_kernel(rows_hbm, dest_hbm, out_hbm, idx_v, rows_v, sem_in, sem_out):
        wid = lax.axis_index("s") * n_cores + lax.axis_index("c")
        first = wid * n_chunks
        for kk in range(TOP_K):
            pltpu.sync_copy(dest_hbm.at[kk, pl.ds(first, n_chunks)], idx_v.at[kk])

        def load(cc, b):
            return pltpu.make_async_copy(rows_hbm.at[pl.ds((first + cc) * ch, ch)], rows_v.at[b], sem_in.at[b])

        def scatters(cc, b):
            return [pltpu.make_async_copy(rows_v.at[b], out_hbm.at[idx_v.at[kk, cc]], sem_out.at[b])
                    for kk in range(TOP_K)]

        load(0, 0).start()
        load(1, 1).start()

        @pl.loop(0, n_chunks, step=2)
        def _(c):
            for b in range(2):
                load(c + b, b).wait()
                for cp in scatters(c + b, b):
                    cp.start()
            for b in range(2):
                for cp in scatters(c + b, b):
                    cp.wait()

                @pl.when(c + 2 + b < n_chunks)
                def _():
                    load(c + 2 + b, b).start()

    return scatter_kernel(rows, dest_km.reshape(TOP_K, t // ch, ch))


def _sc_gather_rows(table, dest_km):
    _, w = table.shape
    t = dest_km.shape[1]
    ch = SC_GATHER_CHUNK
    n_cores, n_workers = _sc_workers()
    n_chunks = t // n_workers // ch
    mesh = plsc.VectorSubcoreMesh(core_axis_name="c", subcore_axis_name="s")

    @functools.partial(
        pl.kernel, mesh=mesh, out_type=jax.ShapeDtypeStruct((TOP_K, t, w), table.dtype),
        scratch_types=[pltpu.VMEM((TOP_K, n_chunks, ch), I32), pltpu.VMEM((2, TOP_K, ch, w), table.dtype),
                       pltpu.SemaphoreType.DMA((2,)), pltpu.SemaphoreType.DMA((2,))],
        name="sc_combine_gather")
    def gather_kernel(table_hbm, dest_hbm, out_hbm, idx_v, rows_v, sem_in, sem_out):
        wid = lax.axis_index("s") * n_cores + lax.axis_index("c")
        first = wid * n_chunks
        for kk in range(TOP_K):
            pltpu.sync_copy(dest_hbm.at[kk, pl.ds(first, n_chunks)], idx_v.at[kk])

        def gathers(cc, b):
            return [pltpu.make_async_copy(table_hbm.at[idx_v.at[kk, cc]], rows_v.at[b, kk], sem_in.at[b])
                    for kk in range(TOP_K)]

        def stores(cc, b):
            return [pltpu.make_async_copy(rows_v.at[b, kk], out_hbm.at[kk, pl.ds((first + cc) * ch, ch)],
                                          sem_out.at[b]) for kk in range(TOP_K)]

        for b in range(2):
            for cp in gathers(b, b):
                cp.start()

        @pl.loop(0, n_chunks, step=2)
        def _(c):
            for b in range(2):
                for cp in gathers(c + b, b):
                    cp.wait()
                for cp in stores(c + b, b):
                    cp.start()
            for b in range(2):
                for cp in stores(c + b, b):
                    cp.wait()

                @pl.when(c + 2 + b < n_chunks)
                def _():
                    for cp in gathers(c + 2 + b, b):
                        cp.start()

    return gather_kernel(table, dest_km.reshape(TOP_K, t // ch, ch))


def _combine_kernel(x1_ref, yg_ref, route_ref, g_ref, o_ref, *, final_norm):
    tm = x1_ref.shape[0]
    moe_lo = jnp.zeros((tm, HALF), F32)
    moe_hi = jnp.zeros((tm, HALF), F32)
    route_t = jnp.concatenate([route_ref[...], jnp.zeros((LANES - ROUTE_ROWS, tm), F32)], axis=0).T
    for kk in range(TOP_K):
        gate = route_t[:, TOP_K + kk:TOP_K + kk + 1]
        y_lo, y_hi = _unpack_rows(yg_ref[kk])
        moe_lo = moe_lo + gate * y_lo
        moe_hi = moe_hi + gate * y_hi
    acc = x1_ref[...] + jnp.concatenate([moe_lo, moe_hi], axis=1)
    o_ref[...] = _rms(acc) * g_ref[...] if final_norm else acc


def _combine(x1, yg, route, g_final, final_norm):
    t = x1.shape[0]
    tm = TM_COMB
    return pl.pallas_call(
        functools.partial(_combine_kernel, final_norm=final_norm),
        grid=(t // tm,),
        in_specs=[pl.BlockSpec((tm, D_MODEL), lambda i: (i, 0)),
                  pl.BlockSpec((TOP_K, tm, HALF), lambda i: (0, i, 0)),
                  pl.BlockSpec((ROUTE_ROWS, tm), lambda i: (0, i)),
                  pl.BlockSpec((1, D_MODEL), lambda i: (0, 0))],
        out_specs=pl.BlockSpec((tm, D_MODEL), lambda i: (i, 0)),
        out_shape=jax.ShapeDtypeStruct((t, D_MODEL), F32),
        compiler_params=pltpu.CompilerParams(dimension_semantics=("arbitrary",)),
        name="combine",
    )(x1, yg, route, g_final)


def _prep_in_weights(w_in, w_uq, w_ukv):
    w_z = w_in[:, 0:512]
    w_xbc = w_in[:, 512:1536]
    w_dt = w_in[:, 1536:1544]
    w_cq = w_in[:, 1544:1800]
    w_ckv = w_in[:, 1800:1928]
    w_kr = w_in[:, 1928:1960]
    half = MLA_ROPE // 2
    zeros = lambda rows, width: jnp.zeros(rows + (width,), BF16)
    cat = lambda parts: jnp.concatenate([p.astype(BF16) for p in parts], axis=-1)
    d = (D_MODEL,)
    misc1 = [w_dt, zeros(d, MLA_NOPE - SSD_HEADS), w_kr, zeros(d, LANES - MLA_QK)]
    misc2 = [zeros(d, MLA_NOPE), w_kr[:, half:], w_kr[:, :half], zeros(d, LANES - MLA_QK)]
    w1 = cat([w_z, w_xbc, w_cq, w_ckv] + misc1 + misc2)

    wq3 = w_uq.reshape(MLA_Q_RANK, MLA_HEADS, MLA_QK)
    qh = (MLA_Q_RANK, MLA_HEADS)
    main = cat([wq3, zeros(qh, LANES - MLA_QK)])
    swap = cat([zeros(qh, MLA_NOPE), wq3[:, :, MLA_NOPE + half:], wq3[:, :, MLA_NOPE:MLA_NOPE + half],
                zeros(qh, LANES - MLA_QK)])
    wq = jnp.concatenate([main.reshape(MLA_Q_RANK, -1), swap.reshape(MLA_Q_RANK, -1)], axis=1)

    wkv3 = w_ukv.reshape(MLA_KV_RANK, MLA_HEADS, MLA_NOPE + MLA_V)
    kh = (MLA_KV_RANK, MLA_HEADS)
    kpart = cat([wkv3[:, :, :MLA_NOPE], zeros(kh, LANES - MLA_NOPE)])
    vpart = wkv3[:, :, MLA_NOPE:].astype(BF16)
    wkv = jnp.concatenate([kpart.reshape(MLA_KV_RANK, -1), vpart.reshape(MLA_KV_RANK, -1)], axis=1)
    return w1, wq, wkv


def _rope_inv_freq():
    inv_freq = ROPE_THETA ** (-jnp.arange(0, MLA_ROPE, 2, dtype=F32) / MLA_ROPE)
    return inv_freq[:, None]


def _pad_lanes(v, fill=0.0):
    return jnp.full((1, LANES), fill, F32).at[0, :v.shape[0]].set(v)


def kernel(x, positions, norm_mix_g, w_in, conv_w, conv_b, dt_bias, a_log, d_skip, ssd_norm_g, q_norm_g, w_uq, kv_norm_g, w_ukv, w_out, norm_ffn_g, w_router, b_router, w_gate_up, b_gate_up, w_down, b_down, norm_final_g):
    bsz, seqlen, d = x.shape
    t = bsz * seqlen
    depth = w_in.shape[0]
    x2 = x.reshape(t, d)
    pos_rows = positions.reshape(t // TM_PROJ, 1, TM_PROJ).astype(I32)
    invf = _rope_inv_freq()

    for l in range(depth):
        w1, wq, wkv = _prep_in_weights(w_in[l], w_uq[l], w_ukv[l])
        z, xbc, dtm, q, k, v = _inproj(
            x2, pos_rows, norm_mix_g[l][None, :], w1, q_norm_g[l][None, :], wq, kv_norm_g[l][None, :], wkv,
            invf, bsz, seqlen)
        y_ssd = _ssd(z, xbc, dtm, conv_w[l], conv_b[l][None, :], _pad_lanes(dt_bias[l]), _pad_lanes(a_log[l]),
                     jnp.repeat(d_skip[l], SSD_HEAD_DIM)[None, :], ssd_norm_g[l][None, :], bsz, seqlen)
        y_mla = _attention(q, k, v, bsz, seqlen).reshape(t, MLA_WIDTH)

        wr = jnp.zeros((d, LANES), F32).at[:, :N_EXPERTS].set(w_router[l])
        wr_hi = wr.astype(BF16)
        wr_lo = (wr - wr_hi.astype(F32)).astype(BF16)
        x1, h2p, route, cnt = _outproj(x2, y_ssd, y_mla, w_out[l].astype(BF16), norm_ffn_g[l][None, :],
                                       jnp.concatenate([wr_hi, wr_lo], axis=1), _pad_lanes(b_router[l]))

        counts = cnt[:, 0].astype(I32)
        padded = ((counts + MOE_STEP - 1) // MOE_STEP) * MOE_STEP
        end_pad = jnp.cumsum(padded)
        start_pad = end_pad - padded
        n_slots = t * TOP_K + N_EXPERTS * MOE_STEP
        n_blocks = n_slots // MOE_STEP
        block_start = jnp.arange(n_blocks, dtype=I32) * MOE_STEP
        block_e = jnp.minimum(jnp.sum(block_start[:, None] >= end_pad[None, :], axis=1), N_EXPERTS - 1).astype(I32)
        eids = jnp.arange(N_EXPERTS, dtype=I32)
        block_hot = block_e[:, None] == eids[None, :]
        per_block = lambda table: jnp.sum(jnp.where(block_hot, table[None, :], 0), axis=1).astype(I32)
        block_valid = jnp.clip(per_block(counts) - (block_start - per_block(start_pad)), 0, MOE_STEP).astype(I32)
        used = counts > 0
        later_used = jnp.where((eids[None, :] > eids[:, None]) & used[None, :], eids[None, :], N_EXPERTS)
        next_used = jnp.min(later_used, axis=1)
        next_used = jnp.where(next_used < N_EXPERTS, next_used, -1).astype(I32)
        stage_slot = ((jnp.cumsum(used.astype(I32)) - 1) & 1).astype(I32)
        dest_km = _slot_plan(route, start_pad.astype(I32))

        xs = _sc_scatter_rows(h2p, dest_km, n_slots)
        ys = _expert_ffn(block_e, block_valid, per_block(next_used), per_block(stage_slot), xs,
                         w_gate_up[l], b_gate_up[l][:, None, :], w_down[l], b_down[l][:, None, :])
        yg = _sc_gather_rows(ys, dest_km)
        x2 = _combine(x1, yg, route, norm_final_g[None, :], l == depth - 1)
    return x2.reshape(bsz, seqlen, d)
```

```python
import functools

import jax
import jax.numpy as jnp
import numpy as np
from jax import lax
from jax.experimental import pallas as pl
from jax.experimental.pallas import tpu as pltpu
from jax.experimental.pallas import tpu_sc as plsc

F32 = jnp.float32
BF16 = jnp.bfloat16
I32 = jnp.int32

D_MODEL = 1024
EPS = 1e-6
LANES = 128
V7X_VMEM_BYTES = 64 * 1024 * 1024
VMEM_LIMIT_BYTES = V7X_VMEM_BYTES * 7 // 8

SSD_HEADS = 8
SSD_HEAD_DIM = 64
SSD_WIDTH = 512
SSD_STATE = 128
SSD_CONV = 4
SSD_CHUNK = 128
SSD_CONV_DIM = 1024
CONV_PAD = 8
SSD_CPS = 4

MLA_HEADS = 8
MLA_Q_RANK = 256
MLA_KV_RANK = 128
MLA_NOPE = 64
MLA_ROPE = 32
MLA_V = 64
MLA_QK = MLA_NOPE + MLA_ROPE
MLA_WIDTH = 512
ROPE_THETA = 10000.0
LOG2_E = 1.4426950408889634

N_EXPERTS = 32
TOP_K = 4
D_FF = 1024
SWIGLU_LIMIT = 7.0
SWIGLU_ALPHA = 1.702

IN_W = 512 + 1024 + 256 + 128 + 128 + 128

TM_PROJ = 1024
TQ = 512
TK = 512
ATT_HPS = 4
VT_ROWS = 80
MOE_BM = 256
MOE_STEP = 1024
TM_COMB = 512
TM_PLAN = 2048
ROUTE_ROWS = 16

NT_DIMS = (((1,), (1,)), ((), ()))


def _rms(x):
    return x * lax.rsqrt(jnp.mean(x * x, axis=-1, keepdims=True) + EPS)


HALF = D_MODEL // 2
HI_MASK = np.int32(-65536)


def _pack_rows(a):
    lo = lax.bitcast_convert_type(a[:, :HALF].astype(BF16).astype(F32), I32)
    hi = lax.bitcast_convert_type(a[:, HALF:].astype(BF16).astype(F32), I32)
    return (hi & HI_MASK) | lax.shift_right_logical(lo, 16)


def _unpack_rows(p):
    lo = lax.bitcast_convert_type(lax.shift_left(p, 16), F32)
    hi = lax.bitcast_convert_type(p & HI_MASK, F32)
    return lo, hi


def _inproj_kernel(x_ref, pos_ref, g_ref, w1_ref, qg_ref, wq_ref, kvg_ref, wkv_ref, invf_ref,
                   z_ref, xbc_ref, dtm_ref, q_ref, k_ref, vt_ref):
    x = x_ref[...]
    h = (_rms(x) * g_ref[...]).astype(BF16)
    p = jnp.dot(h, w1_ref[...], preferred_element_type=F32)
    z_ref[...] = p[:, 0:512]
    xbc_ref[...] = p[:, 512:1536]
    cq = p[:, 1536:1792]
    ckv = p[:, 1792:1920]
    m1 = p[:, 1920:2048]
    m2 = p[:, 2048:2176]
    dtm_ref[...] = m1

    lane = lax.broadcasted_iota(I32, (1, LANES), 1)
    tm = x.shape[0]
    ang = invf_ref[...] * pos_ref[0].astype(F32)
    cos_c = jnp.cos(ang)
    sin_c = jnp.sin(ang)
    z_lo = jnp.zeros((MLA_NOPE, tm), F32)
    z_hi = jnp.zeros((LANES - MLA_QK, tm), F32)
    cos_t = jnp.concatenate([z_lo, cos_c, cos_c, z_hi], axis=0).T
    sin_t = jnp.concatenate([z_lo, -sin_c, sin_c, z_hi], axis=0).T
    cosq_t = jnp.where(lane < MLA_NOPE, 1.0, cos_t)
    scale = MLA_QK ** -0.5 * LOG2_E

    cqn = (_rms(cq) * qg_ref[...]).astype(BF16)
    qq = jnp.dot(cqn, wq_ref[...], preferred_element_type=F32)
    ckvn = (_rms(ckv) * kvg_ref[...]).astype(BF16)
    kv = jnp.dot(ckvn, wkv_ref[...], preferred_element_type=F32)
    krot = m1 * cos_t + m2 * sin_t
    for h_i in range(MLA_HEADS):
        lo = h_i * LANES
        qm = qq[:, lo:lo + LANES]
        qs = qq[:, 1024 + lo:1024 + lo + LANES]
        q_ref[0, h_i] = ((qm * cosq_t + qs * sin_t) * scale).astype(BF16)
        k_ref[0, h_i] = (kv[:, lo:lo + LANES] + krot).astype(BF16)
    ones_rows = jnp.ones((VT_ROWS - MLA_V, tm), BF16)
    for pr in range(MLA_HEADS // 2):
        vpt = kv[:, 1024 + pr * LANES:1024 + (pr + 1) * LANES].T.astype(BF16)
        vt_ref[0, 2 * pr] = jnp.concatenate([vpt[0:MLA_V, :], ones_rows], axis=0)
        vt_ref[0, 2 * pr + 1] = jnp.concatenate([vpt[MLA_V:2 * MLA_V, :], ones_rows], axis=0)


def _inproj(x2, pos_rows, g_mix, w1, qg, wq, kvg, wkv, invf, bsz, seqlen):
    t = x2.shape[0]
    tm = TM_PROJ
    per_b = seqlen // tm
    full = lambda shape: pl.BlockSpec(shape, lambda i: (0,) * len(shape))
    head_spec = pl.BlockSpec((1, MLA_HEADS, tm, LANES), lambda i: (i // per_b, 0, i % per_b, 0))
    head_shape = jax.ShapeDtypeStruct((bsz, MLA_HEADS, seqlen, LANES), BF16)
    vt_spec = pl.BlockSpec((1, MLA_HEADS, VT_ROWS, tm), lambda i: (i // per_b, 0, 0, i % per_b))
    vt_shape = jax.ShapeDtypeStruct((bsz, MLA_HEADS, VT_ROWS, seqlen), BF16)
    return pl.pallas_call(
        _inproj_kernel,
        grid=(t // tm,),
        in_specs=[
            pl.BlockSpec((tm, D_MODEL), lambda i: (i, 0)),
            pl.BlockSpec((1, 1, tm), lambda i: (i, 0, 0)),
            full((1, D_MODEL)), full((D_MODEL, IN_W)),
            full((1, MLA_Q_RANK)), full((MLA_Q_RANK, 2048)),
            full((1, MLA_KV_RANK)), full((MLA_KV_RANK, 1536)),
            full((MLA_ROPE // 2, 1)),
        ],
        out_specs=[
            pl.BlockSpec((tm, 512), lambda i: (i, 0)),
            pl.BlockSpec((tm, 1024), lambda i: (i, 0)),
            pl.BlockSpec((tm, LANES), lambda i: (i, 0)),
            head_spec, head_spec, vt_spec,
        ],
        out_shape=[
            jax.ShapeDtypeStruct((t, 512), F32),
            jax.ShapeDtypeStruct((t, 1024), F32),
            jax.ShapeDtypeStruct((t, LANES), F32),
            head_shape, head_shape, vt_shape,
        ],
        compiler_params=pltpu.CompilerParams(
            dimension_semantics=("arbitrary",), vmem_limit_bytes=VMEM_LIMIT_BYTES),
        name="inproj",
    )(x2, pos_rows, g_mix, w1, qg, wq, kvg, wkv, invf)


def _ssd_kernel(z_ref, xbc_ref, dtm_ref, cw_ref, cb_ref, dtb_ref, alog_ref, dsk_ref, ng_ref,
                y_ref, ext_ref, st_ref):
    q = SSD_CHUNK
    rows = SSD_CPS * q

    @pl.when(pl.program_id(1) == 0)
    def _():
        ext_ref[0:CONV_PAD, :] = jnp.zeros((CONV_PAD, SSD_CONV_DIM), F32)
        st_ref[...] = jnp.zeros_like(st_ref)

    ext_ref[CONV_PAD:CONV_PAD + rows, :] = xbc_ref[...]

    lane = lax.broadcasted_iota(I32, (1, LANES), 1)
    row = lax.broadcasted_iota(I32, (q, q), 0)
    col = lax.broadcasted_iota(I32, (q, q), 1)
    tril = row >= col
    tril_b = jnp.where(tril, 1.0, 0.0).astype(BF16)
    spread = jnp.where(
        lax.broadcasted_iota(I32, (LANES, SSD_WIDTH), 0)
        == lax.broadcasted_iota(I32, (LANES, SSD_WIDTH), 1) // SSD_HEAD_DIM, 1.0, 0.0).astype(BF16)
    a_neg = -jnp.exp(alog_ref[...]) * LOG2_E

    def split3(v):
        hi = v.astype(BF16)
        r1 = v - hi.astype(F32)
        mid = r1.astype(BF16)
        return hi, mid, (r1 - mid.astype(F32)).astype(BF16)

    def dot3_right(parts, m):
        return sum(jnp.dot(p, m, preferred_element_type=F32) for p in parts)

    def expand(cols):
        return dot3_right(split3(cols), spread)

    for ci in range(SSD_CPS):
        lo = ci * q
        conv = cb_ref[...]
        for kk in range(SSD_CONV):
            off = CONV_PAD + lo - (SSD_CONV - 1) + kk
            conv = conv + cw_ref[kk:kk + 1, :] * ext_ref[off:off + q, :]
        u = conv * jax.nn.sigmoid(conv)
        xs = u[:, 0:512]
        bm = u[:, 512:768]
        cm = u[:, 768:1024]

        xdt = dtm_ref[lo:lo + q, :] + dtb_ref[...]
        dt = jnp.maximum(xdt, 0.0) + jnp.log1p(jnp.exp(-jnp.abs(xdt)))
        adt = jnp.where(lane < SSD_HEADS, dt * a_neg, 0.0)
        cum_col = sum(jnp.dot(tril_b, p, preferred_element_type=F32) for p in split3(adt))
        cum_row = cum_col.T

        dt_e = expand(dt)
        ac_e = expand(cum_col)
        last_e = ac_e[q - 1:q, :]
        xd = xs * dt_e
        w_end = xd * jnp.exp2(last_e - ac_e)
        eac = jnp.exp2(ac_e)
        cdec = jnp.exp2(last_e)

        y_parts = []
        for g in range(2):
            gl = g * 256
            bg = bm[:, g * SSD_STATE:(g + 1) * SSD_STATE]
            cg = cm[:, g * SSD_STATE:(g + 1) * SSD_STATE].astype(BF16)
            scores = lax.dot_general(cg, bg.astype(BF16), NT_DIMS, preferred_element_type=F32)
            bgt = bg.T.astype(BF16)
            sprev = st_ref[g]
            yoff = jnp.dot(cg, sprev.astype(BF16), preferred_element_type=F32)
            st_ref[g] = sprev * cdec[:, gl:gl + 256] + jnp.dot(
                bgt, w_end[:, gl:gl + 256].astype(BF16), preferred_element_type=F32)
            for pr in range(2):
                pl_lo = gl + pr * LANES
                xdp = xd[:, pl_lo:pl_lo + LANES].astype(BF16)
                res = []
                for jj in range(2):
                    h_i = g * 4 + pr * 2 + jj
                    seg = cum_col[:, h_i:h_i + 1] - cum_row[h_i:h_i + 1, :]
                    dec = jnp.exp2(jnp.where(tril, seg, -jnp.inf))
                    res.append(jnp.dot((scores * dec).astype(BF16), xdp, preferred_element_type=F32))
                ydiag = jnp.where(lane < SSD_HEAD_DIM, res[0], res[1])
                y_parts.append(ydiag + yoff[:, pr * LANES:(pr + 1) * LANES] * eac[:, pl_lo:pl_lo + LANES])
        y = jnp.concatenate(y_parts, axis=1) + dsk_ref[...] * xs
        zz = z_ref[lo:lo + q, :]
        y = y * (zz * jax.nn.sigmoid(zz))
        outs = []
        for g in range(2):
            yg = y[:, g * 256:(g + 1) * 256]
            outs.append(_rms(yg))
        y_ref[lo:lo + q, :] = (jnp.concatenate(outs, axis=1) * ng_ref[...]).astype(BF16)

    ext_ref[0:CONV_PAD, :] = ext_ref[rows:rows + CONV_PAD, :]


def _ssd(z, xbc, dtm, cw, cb, dtb, alog, dsk, ng, bsz, seqlen):
    t = z.shape[0]
    q = SSD_CHUNK
    rows = SSD_CPS * q
    nc = seqlen // rows
    full = lambda shape: pl.BlockSpec(shape, lambda b, c: (0,) * len(shape))
    row_spec = lambda width: pl.BlockSpec((rows, width), lambda b, c: (b * nc + c, 0))
    return pl.pallas_call(
        _ssd_kernel,
        grid=(bsz, nc),
        in_specs=[row_spec(512), row_spec(1024), row_spec(LANES),
                  full((SSD_CONV, SSD_CONV_DIM)), full((1, SSD_CONV_DIM)),
                  full((1, LANES)), full((1, LANES)), full((1, SSD_WIDTH)), full((1, SSD_WIDTH))],
        out_specs=row_spec(512),
        out_shape=jax.ShapeDtypeStruct((t, SSD_WIDTH), BF16),
        scratch_shapes=[pltpu.VMEM((rows + CONV_PAD, SSD_CONV_DIM), F32),
                        pltpu.VMEM((2, SSD_STATE, 256), F32)],
        compiler_params=pltpu.CompilerParams(dimension_semantics=("arbitrary", "arbitrary")),
        name="ssd",
    )(z, xbc, dtm, cw, cb, dtb, alog, dsk, ng)


def _attn_kernel(q_ref, k_ref, vt_ref, o_ref, acc_ref, s_ref, bmax_ref):
    i = pl.program_id(2)
    acc_ref[...] = jnp.zeros_like(acc_ref)

    def col_max(st):
        part = st[0:LANES, :]
        for r0 in range(LANES, TK, LANES):
            part = jnp.maximum(part, st[r0:r0 + LANES, :])
        return jnp.broadcast_to(jnp.max(part, axis=0, keepdims=True), (8, TQ))

    def scores(j, slot):
        start = pl.multiple_of(j * TK, TK)
        for hh in range(ATT_HPS):
            kb = k_ref[0, hh, pl.ds(start, TK), :]
            st = lax.dot_general(kb, q_ref[0, hh], NT_DIMS, preferred_element_type=F32)
            s_ref[slot, hh] = st
            bmax_ref[slot, hh] = col_max(st)

    def consume(j, slot, m_all, masked):
        start = pl.multiple_of(j * TK, TK)
        new_m = []
        for hh in range(ATT_HPS):
            vt = vt_ref[0, hh, :, pl.ds(start, TK)]
            st = s_ref[slot, hh]
            if masked:
                key = lax.broadcasted_iota(I32, (TK, TQ), 0)
                qry = lax.broadcasted_iota(I32, (TK, TQ), 1)
                st = jnp.where(key <= qry, st, -jnp.inf)
                block_max = col_max(st)
            else:
                block_max = bmax_ref[slot, hh]
            m_old = m_all[hh]
            m_new = jnp.maximum(m_old, block_max)
            alpha = jnp.exp2(m_old - m_new)
            p = jnp.exp2(st - m_new[0:1, :]).astype(BF16)
            acc_ref[hh] = acc_ref[hh] * alpha[0:1, :] + jnp.dot(vt, p, preferred_element_type=F32)
            new_m.append(m_new)
        return tuple(new_m)

    def pair(p, m_all):
        scores(2 * p + 1, 1)
        m_all = consume(2 * p, 0, m_all, False)
        scores(2 * p + 2, 0)
        return consume(2 * p + 1, 1, m_all, False)

    def odd_tail(_, m_all):
        scores(i, 1)
        return consume(i - 1, 0, m_all, False)

    def quad(g, m_all):
        return pair(2 * g + 1, pair(2 * g, m_all))

    m0 = jnp.full((8, TQ), -jnp.inf, F32)
    scores(0, 0)
    m_all = lax.fori_loop(0, i // 4, quad, (m0,) * ATT_HPS)
    m_all = lax.fori_loop(2 * (i // 4), i // 2, pair, m_all)
    m_all = lax.fori_loop(0, i & 1, odd_tail, m_all)

    @pl.when((i & 1) == 0)
    def _():
        consume(i, 0, m_all, True)

    @pl.when((i & 1) == 1)
    def _():
        consume(i, 1, m_all, True)

    outs = []
    for pr in range(ATT_HPS // 2):
        pair_t = []
        for hh in (2 * pr, 2 * pr + 1):
            a = acc_ref[hh]
            pair_t.append(a[0:MLA_V, :] / a[MLA_V:MLA_V + 1, :])
        outs.append(jnp.concatenate(pair_t, axis=0).T)
    o_ref[0] = jnp.concatenate(outs, axis=1).astype(BF16)


def _attention(q, k, vt, bsz, seqlen):
    nq = seqlen // TQ
    hps = ATT_HPS
    return pl.pallas_call(
        _attn_kernel,
        grid=(bsz, MLA_HEADS // hps, nq),
        in_specs=[pl.BlockSpec((1, hps, TQ, LANES), lambda b, p, i: (b, p, i, 0)),
                  pl.BlockSpec((1, hps, seqlen, LANES), lambda b, p, i: (b, p, 0, 0)),
                  pl.BlockSpec((1, hps, VT_ROWS, seqlen), lambda b, p, i: (b, p, 0, 0))],
        out_specs=pl.BlockSpec((1, TQ, hps * MLA_V), lambda b, p, i: (b, i, p)),
        out_shape=jax.ShapeDtypeStruct((bsz, seqlen, MLA_WIDTH), BF16),
        scratch_shapes=[pltpu.VMEM((hps, VT_ROWS, TQ), F32), pltpu.VMEM((2, hps, TK, TQ), F32),
                        pltpu.VMEM((2, hps, 8, TQ), F32)],
        compiler_params=pltpu.CompilerParams(
            dimension_semantics=("arbitrary", "arbitrary", "arbitrary"),
            vmem_limit_bytes=VMEM_LIMIT_BYTES),
        name="attention",
    )(q, k, vt)


def _outproj_kernel(x_ref, ys_ref, ym_ref, wo_ref, g_ref, wr_ref, br_ref,
                    x1_ref, h2_ref, route_ref, cnt_ref):
    tm = x_ref.shape[0]

    @pl.when(pl.program_id(0) == 0)
    def _():
        cnt_ref[...] = jnp.zeros_like(cnt_ref)

    mix = (jnp.dot(ys_ref[...], wo_ref[0:512, :], preferred_element_type=F32)
           + jnp.dot(ym_ref[...], wo_ref[512:1024, :], preferred_element_type=F32))
    x1 = x_ref[...] + mix
    x1_ref[...] = x1
    h2 = _rms(x1) * g_ref[...]
    h2_ref[...] = _pack_rows(h2)

    h_hi = h2.astype(BF16)
    h_lo = (h2 - h_hi.astype(F32)).astype(BF16)
    hh = jnp.dot(h_hi, wr_ref[...], preferred_element_type=F32)
    lh = jnp.dot(h_lo, wr_ref[:, 0:LANES], preferred_element_type=F32)
    logits = hh[:, 0:LANES] + (hh[:, LANES:2 * LANES] + lh) + br_ref[...]
    lt = logits.T[0:N_EXPERTS, :]
    eid = lax.broadcasted_iota(I32, (N_EXPERTS, 1), 0).astype(F32)

    vals, idxs, hots = [], [], []
    for _ in range(TOP_K):
        mx = jnp.max(lt, axis=0, keepdims=True)
        idx = jnp.min(jnp.where(lt == mx, eid, float(N_EXPERTS)), axis=0, keepdims=True)
        hot = eid == idx
        lt = jnp.where(hot, -jnp.inf, lt)
        vals.append(mx)
        idxs.append(idx)
        hots.append(hot)
    exps = [jnp.exp(v - vals[0]) for v in vals]
    denom = exps[0] + exps[1] + exps[2] + exps[3]

    multi_f = jnp.where(hots[0] | hots[1] | hots[2] | hots[3], 1.0, 0.0)
    r = lax.broadcasted_iota(I32, (tm, tm), 0)
    c = lax.broadcasted_iota(I32, (tm, tm), 1)
    earlier = jnp.where(r < c, 1.0, 0.0).astype(BF16)
    before = jnp.dot(multi_f.astype(BF16), earlier, preferred_element_type=F32) + cnt_ref[:, 0:1]
    cnt_ref[...] = cnt_ref[...] + jnp.sum(multi_f, axis=1, keepdims=True)

    ranks = [jnp.sum(jnp.where(hots[kk], before, 0.0), axis=0, keepdims=True) for kk in range(TOP_K)]
    gates = [e / denom for e in exps]
    route_ref[...] = jnp.concatenate(idxs + gates + ranks + [jnp.zeros((ROUTE_ROWS - 3 * TOP_K, tm), F32)], axis=0)


def _outproj(x2, y_ssd, y_mla, wo, g_ffn, wr, br):
    t = x2.shape[0]
    tm = TM_PROJ
    full = lambda shape: pl.BlockSpec(shape, lambda i: (0,) * len(shape))
    rows = lambda width: pl.BlockSpec((tm, width), lambda i: (i, 0))
    return pl.pallas_call(
        _outproj_kernel,
        grid=(t // tm,),
        in_specs=[rows(D_MODEL), rows(512), rows(512), full((1024, D_MODEL)), full((1, D_MODEL)),
                  full((D_MODEL, 2 * LANES)), full((1, LANES))],
        out_specs=[rows(D_MODEL), rows(HALF), pl.BlockSpec((ROUTE_ROWS, tm), lambda i: (0, i)),
                   full((N_EXPERTS, LANES))],
        out_shape=[jax.ShapeDtypeStruct((t, D_MODEL), F32),
                   jax.ShapeDtypeStruct((t, HALF), I32),
                   jax.ShapeDtypeStruct((ROUTE_ROWS, t), F32),
                   jax.ShapeDtypeStruct((N_EXPERTS, LANES), F32)],
        compiler_params=pltpu.CompilerParams(
            dimension_semantics=("arbitrary",), vmem_limit_bytes=VMEM_LIMIT_BYTES),
        name="outproj_router",
    )(x2, y_ssd, y_mla, wo, g_ffn, wr, br)


def _ffn_kernel(be_ref, bv_ref, nx_ref, sl_ref, xs_ref, wgu_hbm, bgu_ref, wd_hbm, bd_ref, ys_ref,
                wgu_st, wd_st, wgu_bf, wd_bf, sem):
    i = pl.program_id(0)
    e = be_ref[i]
    valid = bv_ref[i]
    slot = sl_ref[i]
    first = ((i == 0) | (e != be_ref[jnp.maximum(i - 1, 0)])) & (valid > 0)

    def weight_copies(expert, dst_slot):
        return (pltpu.make_async_copy(wgu_hbm.at[expert], wgu_st.at[dst_slot], sem.at[0, dst_slot]),
                pltpu.make_async_copy(wd_hbm.at[expert], wd_st.at[dst_slot], sem.at[1, dst_slot]))

    @pl.when(i == 0)
    def _():
        for cp in weight_copies(e, slot):
            cp.start()

    @pl.when(first)
    def _():
        for cp in weight_copies(e, slot):
            cp.wait()

        @pl.when(nx_ref[i] >= 0)
        def _():
            for cp in weight_copies(nx_ref[i], 1 - slot):
                cp.start(priority=1)

        wgu_bf[...] = wgu_st[slot].astype(BF16)
        wd_bf[...] = wd_st[slot].astype(BF16)

    def chain(r0, n_rows, masked):
        rows = pl.ds(r0, n_rows)
        xp = xs_ref[rows, :]
        if masked:
            xp = jnp.where(r0 + lax.broadcasted_iota(I32, (n_rows, 1), 0) < valid, xp, 0)
        x_lo, x_hi = _unpack_rows(xp)
        gu = (jnp.dot(x_lo.astype(BF16), wgu_bf[0:HALF, :], preferred_element_type=F32)
              + jnp.dot(x_hi.astype(BF16), wgu_bf[HALF:D_MODEL, :], preferred_element_type=F32)
              + bgu_ref[0])
        gate = jnp.minimum(gu[:, :D_FF], SWIGLU_LIMIT)
        up = jnp.clip(gu[:, D_FF:], -SWIGLU_LIMIT, SWIGLU_LIMIT)
        glu = gate * jax.nn.sigmoid(SWIGLU_ALPHA * gate)
        mid = ((up + 1.0) * glu).astype(BF16)
        ys_ref[rows, :] = _pack_rows(jnp.dot(mid, wd_bf[...], preferred_element_type=F32) + bd_ref[0])

    @pl.when(valid == MOE_STEP)
    def _():
        for r0 in range(0, MOE_STEP, MOE_BM):
            chain(r0, MOE_BM, False)

    @pl.when(valid < MOE_STEP)
    def _():
        half_bm = MOE_BM // 2
        for r0 in range(0, MOE_STEP, MOE_BM):
            pl.when(valid > r0 + half_bm)(functools.partial(chain, r0, MOE_BM, True))
            pl.when((valid > r0) & (valid <= r0 + half_bm))(functools.partial(chain, r0, half_bm, True))

            @pl.when(valid <= r0 + half_bm)
            def _():
                ys_ref[pl.ds(r0 + half_bm, half_bm), :] = jnp.zeros((half_bm, HALF), I32)

            @pl.when(valid <= r0)
            def _():
                ys_ref[pl.ds(r0, half_bm), :] = jnp.zeros((half_bm, HALF), I32)


def _expert_ffn(block_e, block_valid, block_next, block_slot, xs, wgu, bgu, wd, bd):
    n_slots = xs.shape[0]
    bm = MOE_STEP
    grid_spec = pltpu.PrefetchScalarGridSpec(
        num_scalar_prefetch=4,
        grid=(n_slots // bm,),
        in_specs=[
            pl.BlockSpec((bm, HALF), lambda i, be, bv, nx, sl: (i, 0)),
            pl.BlockSpec(memory_space=pl.ANY),
            pl.BlockSpec((1, 1, 2 * D_FF), lambda i, be, bv, nx, sl: (be[i], 0, 0)),
            pl.BlockSpec(memory_space=pl.ANY),
            pl.BlockSpec((1, 1, D_MODEL), lambda i, be, bv, nx, sl: (be[i], 0, 0)),
        ],
        out_specs=pl.BlockSpec((bm, HALF), lambda i, be, bv, nx, sl: (i, 0)),
        scratch_shapes=[pltpu.VMEM((2, D_MODEL, 2 * D_FF), F32), pltpu.VMEM((2, D_FF, D_MODEL), F32),
                        pltpu.VMEM((D_MODEL, 2 * D_FF), BF16), pltpu.VMEM((D_FF, D_MODEL), BF16),
                        pltpu.SemaphoreType.DMA((2, 2))],
    )
    return pl.pallas_call(
        _ffn_kernel,
        grid_spec=grid_spec,
        out_shape=jax.ShapeDtypeStruct((n_slots, HALF), I32),
        compiler_params=pltpu.CompilerParams(
            dimension_semantics=("arbitrary",), vmem_limit_bytes=VMEM_LIMIT_BYTES),
        name="expert_ffn",
    )(block_e, block_valid, block_next, block_slot, xs, wgu, bgu, wd, bd)


def _plan_kernel(sp_ref, route_ref, dest_ref):
    idx = route_ref[0:TOP_K, :]
    rank = route_ref[2 * TOP_K:3 * TOP_K, :]
    start = jnp.zeros(idx.shape, F32)
    for e_i in range(N_EXPERTS):
        start = jnp.where(idx == float(e_i), sp_ref[e_i].astype(F32), start)
    dest_ref[...] = (start + rank).astype(I32)


def _slot_plan(route, start_pad):
    t = route.shape[1]
    tm = TM_PLAN
    grid_spec = pltpu.PrefetchScalarGridSpec(
        num_scalar_prefetch=1,
        grid=(t // tm,),
        in_specs=[pl.BlockSpec((ROUTE_ROWS, tm), lambda i, sp: (0, i))],
        out_specs=pl.BlockSpec((TOP_K, tm), lambda i, sp: (0, i)),
    )
    return pl.pallas_call(
        _plan_kernel,
        grid_spec=grid_spec,
        out_shape=jax.ShapeDtypeStruct((TOP_K, t), I32),
        compiler_params=pltpu.CompilerParams(dimension_semantics=("arbitrary",)),
        name="slot_plan",
    )(start_pad, route)


SC_SCATTER_CHUNK = 32
SC_GATHER_CHUNK = 16


def _sc_workers():
    info = plsc.get_sparse_core_info()
    return info.num_cores, info.num_cores * info.num_subcores


def _sc_scatter_rows(rows, dest_km, n_out):
    t, w = rows.shape
    ch = SC_SCATTER_CHUNK
    n_cores, n_workers = _sc_workers()
    n_chunks = t // n_workers // ch
    mesh = plsc.VectorSubcoreMesh(core_axis_name="c", subcore_axis_name="s")

    @functools.partial(
        pl.kernel, mesh=mesh, out_type=jax.ShapeDtypeStruct((n_out, w), rows.dtype),
        scratch_types=[pltpu.VMEM((TOP_K, n_chunks, ch), I32), pltpu.VMEM((2, ch, w), rows.dtype),
                       pltpu.SemaphoreType.DMA((2,)), pltpu.SemaphoreType.DMA((2,))],
        name="sc_dispatch_scatter")
    def scatter_kernel(rows_hbm, dest_hbm, out_hbm, idx_v, rows_v, sem_in, sem_out):
        wid = lax.axis_index("s") * n_cores + lax.axis_index("c")
        first = wid * n_chunks
        for kk in range(TOP_K):
            pltpu.sync_copy(dest_hbm.at[kk, pl.ds(first, n_chunks)], idx_v.at[kk])

        def load(cc, b):
            return pltpu.make_async_copy(rows_hbm.at[pl.ds((first + cc) * ch, ch)], rows_v.at[b], sem_in.at[b])

        def scatters(cc, b):
            return [pltpu.make_async_copy(rows_v.at[b], out_hbm.at[idx_v.at[kk, cc]], sem_out.at[b])
                    for kk in range(TOP_K)]

        load(0, 0).start()
        load(1, 1).start()

        @pl.loop(0, n_chunks, step=2)
        def _(c):
            for b in range(2):
                load(c + b, b).wait()
                for cp in scatters(c + b, b):
                    cp.start()
            for b in range(2):
                for cp in scatters(c + b, b):
                    cp.wait()

                @pl.when(c + 2 + b < n_chunks)
                def _():
                    load(c + 2 + b, b).start()

    return scatter_kernel(rows, dest_km.reshape(TOP_K, t // ch, ch))


def _sc_gather_rows(table, dest_km):
    _, w = table.shape
    t = dest_km.shape[1]
    ch = SC_GATHER_CHUNK
    n_cores, n_workers = _sc_workers()
    n_chunks = t // n_workers // ch
    mesh = plsc.VectorSubcoreMesh(core_axis_name="c", subcore_axis_name="s")

    @functools.partial(
        pl.kernel, mesh=mesh, out_type=jax.ShapeDtypeStruct((TOP_K, t, w), table.dtype),
        scratch_types=[pltpu.VMEM((TOP_K, n_chunks, ch), I32), pltpu.VMEM((2, TOP_K, ch, w), table.dtype),
                       pltpu.SemaphoreType.DMA((2,)), pltpu.SemaphoreType.DMA((2,))],
        name="sc_combine_gather")
    def gather_kernel(table_hbm, dest_hbm, out_hbm, idx_v, rows_v, sem_in, sem_out):
        wid = lax.axis_index("s") * n_cores + lax.axis_index("c")
        first = wid * n_chunks
        for kk in range(TOP_K):
            pltpu.sync_copy(dest_hbm.at[kk, pl.ds(first, n_chunks)], idx_v.at[kk])

        def gathers(cc, b):
            return [pltpu.make_async_copy(table_hbm.at[idx_v.at[kk, cc]], rows_v.at[b, kk], sem_in.at[b])
                    for kk in range(TOP_K)]

        def stores(cc, b):
            return [pltpu.make_async_copy(rows_v.at[b, kk], out_hbm.at[kk, pl.ds((first + cc) * ch, ch)],
                                          sem_out.at[b]) for kk in range(TOP_K)]

        for b in range(2):
            for cp in gathers(b, b):
                cp.start()

        @pl.loop(0, n_chunks, step=2)
        def _(c):
            for b in range(2):
                for cp in gathers(c + b, b):
                    cp.wait()
                for cp in stores(c + b, b):
                    cp.start()
            for b in range(2):
                for cp in stores(c + b, b):
                    cp.wait()

                @pl.when(c + 2 + b < n_chunks)
                def _():
                    for cp in gathers(c + 2 + b, b):
                        cp.start()

    return gather_kernel(table, dest_km.reshape(TOP_K, t // ch, ch))


def _combine_kernel(x1_ref, yg_ref, route_ref, g_ref, o_ref, *, final_norm):
    tm = x1_ref.shape[0]
    moe_lo = jnp.zeros((tm, HALF), F32)
    moe_hi = jnp.zeros((tm, HALF), F32)
    route_t = jnp.concatenate([route_ref[...], jnp.zeros((LANES - ROUTE_ROWS, tm), F32)], axis=0).T
    for kk in range(TOP_K):
        gate = route_t[:, TOP_K + kk:TOP_K + kk + 1]
        y_lo, y_hi = _unpack_rows(yg_ref[kk])
        moe_lo = moe_lo + gate * y_lo
        moe_hi = moe_hi + gate * y_hi
    acc = x1_ref[...] + jnp.concatenate([moe_lo, moe_hi], axis=1)
    o_ref[...] = _rms(acc) * g_ref[...] if final_norm else acc


def _combine(x1, yg, route, g_final, final_norm):
    t = x1.shape[0]
    tm = TM_COMB
    return pl.pallas_call(
        functools.partial(_combine_kernel, final_norm=final_norm),
        grid=(t // tm,),
        in_specs=[pl.BlockSpec((tm, D_MODEL), lambda i: (i, 0)),
                  pl.BlockSpec((TOP_K, tm, HALF), lambda i: (0, i, 0)),
                  pl.BlockSpec((ROUTE_ROWS, tm), lambda i: (0, i)),
                  pl.BlockSpec((1, D_MODEL), lambda i: (0, 0))],
        out_specs=pl.BlockSpec((tm, D_MODEL), lambda i: (i, 0)),
        out_shape=jax.ShapeDtypeStruct((t, D_MODEL), F32),
        compiler_params=pltpu.CompilerParams(dimension_semantics=("arbitrary",)),
        name="combine",
    )(x1, yg, route, g_final)


def _prep_in_weights(w_in, w_uq, w_ukv):
    w_z = w_in[:, 0:512]
    w_xbc = w_in[:, 512:1536]
    w_dt = w_in[:, 1536:1544]
    w_cq = w_in[:, 1544:1800]
    w_ckv = w_in[:, 1800:1928]
    w_kr = w_in[:, 1928:1960]
    half = MLA_ROPE // 2
    zeros = lambda rows, width: jnp.zeros(rows + (width,), BF16)
    cat = lambda parts: jnp.concatenate([p.astype(BF16) for p in parts], axis=-1)
    d = (D_MODEL,)
    misc1 = [w_dt, zeros(d, MLA_NOPE - SSD_HEADS), w_kr, zeros(d, LANES - MLA_QK)]
    misc2 = [zeros(d, MLA_NOPE), w_kr[:, half:], w_kr[:, :half], zeros(d, LANES - MLA_QK)]
    w1 = cat([w_z, w_xbc, w_cq, w_ckv] + misc1 + misc2)

    wq3 = w_uq.reshape(MLA_Q_RANK, MLA_HEADS, MLA_QK)
    qh = (MLA_Q_RANK, MLA_HEADS)
    main = cat([wq3, zeros(qh, LANES - MLA_QK)])
    swap = cat([zeros(qh, MLA_NOPE), wq3[:, :, MLA_NOPE + half:], wq3[:, :, MLA_NOPE:MLA_NOPE + half],
                zeros(qh, LANES - MLA_QK)])
    wq = jnp.concatenate([main.reshape(MLA_Q_RANK, -1), swap.reshape(MLA_Q_RANK, -1)], axis=1)

    wkv3 = w_ukv.reshape(MLA_KV_RANK, MLA_HEADS, MLA_NOPE + MLA_V)
    kh = (MLA_KV_RANK, MLA_HEADS)
    kpart = cat([wkv3[:, :, :MLA_NOPE], zeros(kh, LANES - MLA_NOPE)])
    vpart = wkv3[:, :, MLA_NOPE:].astype(BF16)
    wkv = jnp.concatenate([kpart.reshape(MLA_KV_RANK, -1), vpart.reshape(MLA_KV_RANK, -1)], axis=1)
    return w1, wq, wkv


def _rope_inv_freq():
    inv_freq = ROPE_THETA ** (-jnp.arange(0, MLA_ROPE, 2, dtype=F32) / MLA_ROPE)
    return inv_freq[:, None]


def _pad_lanes(v, fill=0.0):
    return jnp.full((1, LANES), fill, F32).at[0, :v.shape[0]].set(v)


def kernel(x, positions, norm_mix_g, w_in, conv_w, conv_b, dt_bias, a_log, d_skip, ssd_norm_g, q_norm_g, w_uq, kv_norm_g, w_ukv, w_out, norm_ffn_g, w_router, b_router, w_gate_up, b_gate_up, w_down, b_down, norm_final_g):
    bsz, seqlen, d = x.shape
    t = bsz * seqlen
    depth = w_in.shape[0]
    x2 = x.reshape(t, d)
    pos_rows = positions.reshape(t // TM_PROJ, 1, TM_PROJ).astype(I32)
    invf = _rope_inv_freq()

    for l in range(depth):
        w1, wq, wkv = _prep_in_weights(w_in[l], w_uq[l], w_ukv[l])
        z, xbc, dtm, q, k, v = _inproj(
            x2, pos_rows, norm_mix_g[l][None, :], w1, q_norm_g[l][None, :], wq, kv_norm_g[l][None, :], wkv,
            invf, bsz, seqlen)
        y_ssd = _ssd(z, xbc, dtm, conv_w[l], conv_b[l][None, :], _pad_lanes(dt_bias[l]), _pad_lanes(a_log[l]),
                     jnp.repeat(d_skip[l], SSD_HEAD_DIM)[None, :], ssd_norm_g[l][None, :], bsz, seqlen)
        y_mla = _attention(q, k, v, bsz, seqlen).reshape(t, MLA_WIDTH)

        wr = jnp.zeros((d, LANES), F32).at[:, :N_EXPERTS].set(w_router[l])
        wr_hi = wr.astype(BF16)
        wr_lo = (wr - wr_hi.astype(F32)).astype(BF16)
        x1, h2p, route, cnt = _outproj(x2, y_ssd, y_mla, w_out[l].astype(BF16), norm_ffn_g[l][None, :],
                                       jnp.concatenate([wr_hi, wr_lo], axis=1), _pad_lanes(b_router[l]))

        counts = cnt[:, 0].astype(I32)
        padded = ((counts + MOE_STEP - 1) // MOE_STEP) * MOE_STEP
        end_pad = jnp.cumsum(padded)
        start_pad = end_pad - padded
        n_slots = t * TOP_K + N_EXPERTS * MOE_STEP
        n_blocks = n_slots // MOE_STEP
        block_start = jnp.arange(n_blocks, dtype=I32) * MOE_STEP
        block_e = jnp.minimum(jnp.sum(block_start[:, None] >= end_pad[None, :], axis=1), N_EXPERTS - 1).astype(I32)
        eids = jnp.arange(N_EXPERTS, dtype=I32)
        block_hot = block_e[:, None] == eids[None, :]
        per_block = lambda table: jnp.sum(jnp.where(block_hot, table[None, :], 0), axis=1).astype(I32)
        block_valid = jnp.clip(per_block(counts) - (block_start - per_block(start_pad)), 0, MOE_STEP).astype(I32)
        used = counts > 0
        later_used = jnp.where((eids[None, :] > eids[:, None]) & used[None, :], eids[None, :], N_EXPERTS)
        next_used = jnp.min(later_used, axis=1)
        next_used = jnp.where(next_used < N_EXPERTS, next_used, -1).astype(I32)
        stage_slot = ((jnp.cumsum(used.astype(I32)) - 1) & 1).astype(I32)
        dest_km = _slot_plan(route, start_pad.astype(I32))

        xs = _sc_scatter_rows(h2p, dest_km, n_slots)
        ys = _expert_ffn(block_e, block_valid, per_block(next_used), per_block(stage_slot), xs,
                         w_gate_up[l], b_gate_up[l][:, None, :], w_down[l], b_down[l][:, None, :])
        yg = _sc_gather_rows(ys, dest_km)
        x2 = _combine(x1, yg, route, norm_final_g[None, :], l == depth - 1)
    return x2.reshape(bsz, seqlen, d)
```

```python
import functools

import jax
import jax.numpy as jnp
import numpy as np
from jax import lax
from jax.experimental import pallas as pl
from jax.experimental.pallas import tpu as pltpu
from jax.experimental.pallas import tpu_sc as plsc

F32 = jnp.float32
BF16 = jnp.bfloat16
I32 = jnp.int32

D_MODEL = 1024
EPS = 1e-6
LANES = 128
V7X_VMEM_BYTES = 64 * 1024 * 1024
VMEM_LIMIT_BYTES = V7X_VMEM_BYTES * 7 // 8

SSD_HEADS = 8
SSD_HEAD_DIM = 64
SSD_WIDTH = 512
SSD_STATE = 128
SSD_CONV = 4
SSD_CHUNK = 128
SSD_CONV_DIM = 1024
CONV_PAD = 8
SSD_CPS = 4

MLA_HEADS = 8
MLA_Q_RANK = 256
MLA_KV_RANK = 128
MLA_NOPE = 64
MLA_ROPE = 32
MLA_V = 64
MLA_QK = MLA_NOPE + MLA_ROPE
MLA_WIDTH = 512
ROPE_THETA = 10000.0
LOG2_E = 1.4426950408889634

N_EXPERTS = 32
TOP_K = 4
D_FF = 1024
SWIGLU_LIMIT = 7.0
SWIGLU_ALPHA = 1.702

IN_W = 512 + 1024 + 256 + 128 + 128 + 128

TM_PROJ = 1024
TQ = 512
TK = 512
ATT_HPS = 4
VT_ROWS = 80
MOE_BM = 256
MOE_STEP = 1024
TM_COMB = 512
TM_PLAN = 2048
ROUTE_ROWS = 16

NT_DIMS = (((1,), (1,)), ((), ()))


def _rms(x):
    return x * lax.rsqrt(jnp.mean(x * x, axis=-1, keepdims=True) + EPS)


HALF = D_MODEL // 2
HI_MASK = np.int32(-65536)


def _pack_rows(a):
    lo = lax.bitcast_convert_type(a[:, :HALF].astype(BF16).astype(F32), I32)
    hi = lax.bitcast_convert_type(a[:, HALF:].astype(BF16).astype(F32), I32)
    return (hi & HI_MASK) | lax.shift_right_logical(lo, 16)


def _unpack_rows(p):
    lo = lax.bitcast_convert_type(lax.shift_left(p, 16), F32)
    hi = lax.bitcast_convert_type(p & HI_MASK, F32)
    return lo, hi


def _inproj_kernel(x_ref, pos_ref, g_ref, w1_ref, qg_ref, wq_ref, kvg_ref, wkv_ref, invf_ref,
                   z_ref, xbc_ref, dtm_ref, q_ref, k_ref, vt_ref):
    x = x_ref[...]
    h = (_rms(x) * g_ref[...]).astype(BF16)
    p = jnp.dot(h, w1_ref[...], preferred_element_type=F32)
    z_ref[...] = p[:, 0:512]
    xbc_ref[...] = p[:, 512:1536]
    cq = p[:, 1536:1792]
    ckv = p[:, 1792:1920]
    m1 = p[:, 1920:2048]
    m2 = p[:, 2048:2176]
    dtm_ref[...] = m1

    lane = lax.broadcasted_iota(I32, (1, LANES), 1)
    tm = x.shape[0]
    ang = invf_ref[...] * pos_ref[0].astype(F32)
    cos_c = jnp.cos(ang)
    sin_c = jnp.sin(ang)
    z_lo = jnp.zeros((MLA_NOPE, tm), F32)
    z_hi = jnp.zeros((LANES - MLA_QK, tm), F32)
    cos_t = jnp.concatenate([z_lo, cos_c, cos_c, z_hi], axis=0).T
    sin_t = jnp.concatenate([z_lo, -sin_c, sin_c, z_hi], axis=0).T
    cosq_t = jnp.where(lane < MLA_NOPE, 1.0, cos_t)
    scale = MLA_QK ** -0.5 * LOG2_E

    cqn = (_rms(cq) * qg_ref[...]).astype(BF16)
    qq = jnp.dot(cqn, wq_ref[...], preferred_element_type=F32)
    ckvn = (_rms(ckv) * kvg_ref[...]).astype(BF16)
    kv = jnp.dot(ckvn, wkv_ref[...], preferred_element_type=F32)
    krot = m1 * cos_t + m2 * sin_t
    for h_i in range(MLA_HEADS):
        lo = h_i * LANES
        qm = qq[:, lo:lo + LANES]
        qs = qq[:, 1024 + lo:1024 + lo + LANES]
        q_ref[0, h_i] = ((qm * cosq_t + qs * sin_t) * scale).astype(BF16)
        k_ref[0, h_i] = (kv[:, lo:lo + LANES] + krot).astype(BF16)
    ones_rows = jnp.ones((VT_ROWS - MLA_V, tm), BF16)
    for pr in range(MLA_HEADS // 2):
        vpt = kv[:, 1024 + pr * LANES:1024 + (pr + 1) * LANES].T.astype(BF16)
        vt_ref[0, 2 * pr] = jnp.concatenate([vpt[0:MLA_V, :], ones_rows], axis=0)
        vt_ref[0, 2 * pr + 1] = jnp.concatenate([vpt[MLA_V:2 * MLA_V, :], ones_rows], axis=0)


def _inproj(x2, pos_rows, g_mix, w1, qg, wq, kvg, wkv, invf, bsz, seqlen):
    t = x2.shape[0]
    tm = TM_PROJ
    per_b = seqlen // tm
    full = lambda shape: pl.BlockSpec(shape, lambda i: (0,) * len(shape))
    head_spec = pl.BlockSpec((1, MLA_HEADS, tm, LANES), lambda i: (i // per_b, 0, i % per_b, 0))
    head_shape = jax.ShapeDtypeStruct((bsz, MLA_HEADS, seqlen, LANES), BF16)
    vt_spec = pl.BlockSpec((1, MLA_HEADS, VT_ROWS, tm), lambda i: (i // per_b, 0, 0, i % per_b))
    vt_shape = jax.ShapeDtypeStruct((bsz, MLA_HEADS, VT_ROWS, seqlen), BF16)
    return pl.pallas_call(
        _inproj_kernel,
        grid=(t // tm,),
        in_specs=[
            pl.BlockSpec((tm, D_MODEL), lambda i: (i, 0)),
            pl.BlockSpec((1, 1, tm), lambda i: (i, 0, 0)),
            full((1, D_MODEL)), full((D_MODEL, IN_W)),
            full((1, MLA_Q_RANK)), full((MLA_Q_RANK, 2048)),
            full((1, MLA_KV_RANK)), full((MLA_KV_RANK, 1536)),
            full((MLA_ROPE // 2, 1)),
        ],
        out_specs=[
            pl.BlockSpec((tm, 512), lambda i: (i, 0)),
            pl.BlockSpec((tm, 1024), lambda i: (i, 0)),
            pl.BlockSpec((tm, LANES), lambda i: (i, 0)),
            head_spec, head_spec, vt_spec,
        ],
        out_shape=[
            jax.ShapeDtypeStruct((t, 512), F32),
            jax.ShapeDtypeStruct((t, 1024), F32),
            jax.ShapeDtypeStruct((t, LANES), F32),
            head_shape, head_shape, vt_shape,
        ],
        compiler_params=pltpu.CompilerParams(
            dimension_semantics=("arbitrary",), vmem_limit_bytes=VMEM_LIMIT_BYTES),
        name="inproj",
    )(x2, pos_rows, g_mix, w1, qg, wq, kvg, wkv, invf)


def _ssd_kernel(z_ref, xbc_ref, dtm_ref, cw_ref, cb_ref, dtb_ref, alog_ref, dsk_ref, ng_ref,
                y_ref, ext_ref, st_ref):
    q = SSD_CHUNK
    rows = SSD_CPS * q

    @pl.when(pl.program_id(1) == 0)
    def _():
        ext_ref[0:CONV_PAD, :] = jnp.zeros((CONV_PAD, SSD_CONV_DIM), F32)
        st_ref[...] = jnp.zeros_like(st_ref)

    ext_ref[CONV_PAD:CONV_PAD + rows, :] = xbc_ref[...]

    lane = lax.broadcasted_iota(I32, (1, LANES), 1)
    row = lax.broadcasted_iota(I32, (q, q), 0)
    col = lax.broadcasted_iota(I32, (q, q), 1)
    tril = row >= col
    tril_b = jnp.where(tril, 1.0, 0.0).astype(BF16)
    spread = jnp.where(
        lax.broadcasted_iota(I32, (LANES, SSD_WIDTH), 0)
        == lax.broadcasted_iota(I32, (LANES, SSD_WIDTH), 1) // SSD_HEAD_DIM, 1.0, 0.0).astype(BF16)
    a_neg = -jnp.exp(alog_ref[...]) * LOG2_E

    def split3(v):
        hi = v.astype(BF16)
        r1 = v - hi.astype(F32)
        mid = r1.astype(BF16)
        return hi, mid, (r1 - mid.astype(F32)).astype(BF16)

    def dot3_right(parts, m):
        return sum(jnp.dot(p, m, preferred_element_type=F32) for p in parts)

    def expand(cols):
        return dot3_right(split3(cols), spread)

    for ci in range(SSD_CPS):
        lo = ci * q
        window = ext_ref[lo:lo + CONV_PAD + q, :]
        conv = cb_ref[...] + cw_ref[SSD_CONV - 1:SSD_CONV, :] * window[CONV_PAD:, :]
        for kk in range(SSD_CONV - 1):
            shifted = pltpu.roll(window, SSD_CONV - 1 - kk, axis=0)
            conv = conv + cw_ref[kk:kk + 1, :] * shifted[CONV_PAD:, :]
        u = conv * jax.nn.sigmoid(conv)
        xs = u[:, 0:512]
        bm = u[:, 512:768]
        cm = u[:, 768:1024]

        xdt = dtm_ref[lo:lo + q, :] + dtb_ref[...]
        dt = jnp.maximum(xdt, 0.0) + jnp.log1p(jnp.exp(-jnp.abs(xdt)))
        adt = jnp.where(lane < SSD_HEADS, dt * a_neg, 0.0)
        cum_col = sum(jnp.dot(tril_b, p, preferred_element_type=F32) for p in split3(adt))
        cum_row = cum_col.T

        dt_e = expand(dt)
        ac_e = expand(cum_col)
        last_e = ac_e[q - 1:q, :]
        xd = xs * dt_e
        w_end = xd * jnp.exp2(last_e - ac_e)
        eac = jnp.exp2(ac_e)
        cdec = jnp.exp2(last_e)

        y_parts = []
        for g in range(2):
            gl = g * 256
            bg = bm[:, g * SSD_STATE:(g + 1) * SSD_STATE]
            cg = cm[:, g * SSD_STATE:(g + 1) * SSD_STATE].astype(BF16)
            scores = lax.dot_general(cg, bg.astype(BF16), NT_DIMS, preferred_element_type=F32)
            bgt = bg.T.astype(BF16)
            sprev = st_ref[g]
            yoff = jnp.dot(cg, sprev.astype(BF16), preferred_element_type=F32)
            st_ref[g] = sprev * cdec[:, gl:gl + 256] + jnp.dot(
                bgt, w_end[:, gl:gl + 256].astype(BF16), preferred_element_type=F32)
            for pr in range(2):
                pl_lo = gl + pr * LANES
                xdp = xd[:, pl_lo:pl_lo + LANES].astype(BF16)
                res = []
                for jj in range(2):
                    h_i = g * 4 + pr * 2 + jj
                    seg = cum_col[:, h_i:h_i + 1] - cum_row[h_i:h_i + 1, :]
                    dec = jnp.exp2(jnp.where(tril, seg, -jnp.inf))
                    res.append(jnp.dot((scores * dec).astype(BF16), xdp, preferred_element_type=F32))
                ydiag = jnp.where(lane < SSD_HEAD_DIM, res[0], res[1])
                y_parts.append(ydiag + yoff[:, pr * LANES:(pr + 1) * LANES] * eac[:, pl_lo:pl_lo + LANES])
        y = jnp.concatenate(y_parts, axis=1) + dsk_ref[...] * xs
        zz = z_ref[lo:lo + q, :]
        y = y * (zz * jax.nn.sigmoid(zz))
        outs = []
        for g in range(2):
            yg = y[:, g * 256:(g + 1) * 256]
            outs.append(_rms(yg))
        y_ref[lo:lo + q, :] = (jnp.concatenate(outs, axis=1) * ng_ref[...]).astype(BF16)

    ext_ref[0:CONV_PAD, :] = ext_ref[rows:rows + CONV_PAD, :]


def _ssd(z, xbc, dtm, cw, cb, dtb, alog, dsk, ng, bsz, seqlen):
    t = z.shape[0]
    q = SSD_CHUNK
    rows = SSD_CPS * q
    nc = seqlen // rows
    full = lambda shape: pl.BlockSpec(shape, lambda b, c: (0,) * len(shape))
    row_spec = lambda width: pl.BlockSpec((rows, width), lambda b, c: (b * nc + c, 0))
    return pl.pallas_call(
        _ssd_kernel,
        grid=(bsz, nc),
        in_specs=[row_spec(512), row_spec(1024), row_spec(LANES),
                  full((SSD_CONV, SSD_CONV_DIM)), full((1, SSD_CONV_DIM)),
                  full((1, LANES)), full((1, LANES)), full((1, SSD_WIDTH)), full((1, SSD_WIDTH))],
        out_specs=row_spec(512),
        out_shape=jax.ShapeDtypeStruct((t, SSD_WIDTH), BF16),
        scratch_shapes=[pltpu.VMEM((rows + CONV_PAD, SSD_CONV_DIM), F32),
                        pltpu.VMEM((2, SSD_STATE, 256), F32)],
        compiler_params=pltpu.CompilerParams(dimension_semantics=("arbitrary", "arbitrary")),
        name="ssd",
    )(z, xbc, dtm, cw, cb, dtb, alog, dsk, ng)


def _attn_kernel(q_ref, k_ref, vt_ref, o_ref, acc_ref, s_ref, bmax_ref):
    i = pl.program_id(2)
    acc_ref[...] = jnp.zeros_like(acc_ref)

    def col_max(st):
        part = st[0:LANES, :]
        for r0 in range(LANES, TK, LANES):
            part = jnp.maximum(part, st[r0:r0 + LANES, :])
        return jnp.broadcast_to(jnp.max(part, axis=0, keepdims=True), (8, TQ))

    def scores(j, slot):
        start = pl.multiple_of(j * TK, TK)
        for hh in range(ATT_HPS):
            kb = k_ref[0, hh, pl.ds(start, TK), :]
            st = lax.dot_general(kb, q_ref[0, hh], NT_DIMS, preferred_element_type=F32)
            s_ref[slot, hh] = st
            bmax_ref[slot, hh] = col_max(st)

    def consume(j, slot, m_all, masked):
        start = pl.multiple_of(j * TK, TK)
        new_m = []
        for hh in range(ATT_HPS):
            vt = vt_ref[0, hh, :, pl.ds(start, TK)]
            st = s_ref[slot, hh]
            if masked:
                key = lax.broadcasted_iota(I32, (TK, TQ), 0)
                qry = lax.broadcasted_iota(I32, (TK, TQ), 1)
                st = jnp.where(key <= qry, st, -jnp.inf)
                block_max = col_max(st)
            else:
                block_max = bmax_ref[slot, hh]
            m_old = m_all[hh]
            m_new = jnp.maximum(m_old, block_max)
            alpha = jnp.exp2(m_old - m_new)
            p = jnp.exp2(st - m_new[0:1, :]).astype(BF16)
            acc_ref[hh] = acc_ref[hh] * alpha[0:1, :] + jnp.dot(vt, p, preferred_element_type=F32)
            new_m.append(m_new)
        return tuple(new_m)

    def pair(p, m_all):
        scores(2 * p + 1, 1)
        m_all = consume(2 * p, 0, m_all, False)
        scores(2 * p + 2, 0)
        return consume(2 * p + 1, 1, m_all, False)

    def odd_tail(_, m_all):
        scores(i, 1)
        return consume(i - 1, 0, m_all, False)

    def quad(g, m_all):
        return pair(2 * g + 1, pair(2 * g, m_all))

    m0 = jnp.full((8, TQ), -jnp.inf, F32)
    scores(0, 0)
    m_all = lax.fori_loop(0, i // 4, quad, (m0,) * ATT_HPS)
    m_all = lax.fori_loop(2 * (i // 4), i // 2, pair, m_all)
    m_all = lax.fori_loop(0, i & 1, odd_tail, m_all)

    @pl.when((i & 1) == 0)
    def _():
        consume(i, 0, m_all, True)

    @pl.when((i & 1) == 1)
    def _():
        consume(i, 1, m_all, True)

    outs = []
    for pr in range(ATT_HPS // 2):
        pair_t = []
        for hh in (2 * pr, 2 * pr + 1):
            a = acc_ref[hh]
            pair_t.append(a[0:MLA_V, :] / a[MLA_V:MLA_V + 1, :])
        outs.append(jnp.concatenate(pair_t, axis=0).T)
    o_ref[0] = jnp.concatenate(outs, axis=1).astype(BF16)


def _attention(q, k, vt, bsz, seqlen):
    nq = seqlen // TQ
    hps = ATT_HPS
    return pl.pallas_call(
        _attn_kernel,
        grid=(bsz, MLA_HEADS // hps, nq),
        in_specs=[pl.BlockSpec((1, hps, TQ, LANES), lambda b, p, i: (b, p, i, 0)),
                  pl.BlockSpec((1, hps, seqlen, LANES), lambda b, p, i: (b, p, 0, 0)),
                  pl.BlockSpec((1, hps, VT_ROWS, seqlen), lambda b, p, i: (b, p, 0, 0))],
        out_specs=pl.BlockSpec((1, TQ, hps * MLA_V), lambda b, p, i: (b, i, p)),
        out_shape=jax.ShapeDtypeStruct((bsz, seqlen, MLA_WIDTH), BF16),
        scratch_shapes=[pltpu.VMEM((hps, VT_ROWS, TQ), F32), pltpu.VMEM((2, hps, TK, TQ), F32),
                        pltpu.VMEM((2, hps, 8, TQ), F32)],
        compiler_params=pltpu.CompilerParams(
            dimension_semantics=("arbitrary", "arbitrary", "arbitrary"),
            vmem_limit_bytes=VMEM_LIMIT_BYTES),
        name="attention",
    )(q, k, vt)


def _outproj_kernel(x_ref, ys_ref, ym_ref, wo_ref, g_ref, wr_ref, br_ref,
                    x1_ref, h2_ref, route_ref, cnt_ref):
    tm = x_ref.shape[0]

    @pl.when(pl.program_id(0) == 0)
    def _():
        cnt_ref[...] = jnp.zeros_like(cnt_ref)

    mix = (jnp.dot(ys_ref[...], wo_ref[0:512, :], preferred_element_type=F32)
           + jnp.dot(ym_ref[...], wo_ref[512:1024, :], preferred_element_type=F32))
    x1 = x_ref[...] + mix
    x1_ref[...] = x1
    h2 = _rms(x1) * g_ref[...]
    h2_ref[...] = _pack_rows(h2)

    h_hi = h2.astype(BF16)
    h_lo = (h2 - h_hi.astype(F32)).astype(BF16)
    hh = jnp.dot(h_hi, wr_ref[...], preferred_element_type=F32)
    lh = jnp.dot(h_lo, wr_ref[:, 0:LANES], preferred_element_type=F32)
    logits = hh[:, 0:LANES] + (hh[:, LANES:2 * LANES] + lh) + br_ref[...]
    lt = logits.T[0:N_EXPERTS, :]
    eid = lax.broadcasted_iota(I32, (N_EXPERTS, 1), 0).astype(F32)

    vals, idxs, hots = [], [], []
    for _ in range(TOP_K):
        mx = jnp.max(lt, axis=0, keepdims=True)
        idx = jnp.min(jnp.where(lt == mx, eid, float(N_EXPERTS)), axis=0, keepdims=True)
        hot = eid == idx
        lt = jnp.where(hot, -jnp.inf, lt)
        vals.append(mx)
        idxs.append(idx)
        hots.append(hot)
    exps = [jnp.exp(v - vals[0]) for v in vals]
    denom = exps[0] + exps[1] + exps[2] + exps[3]

    multi_f = jnp.where(hots[0] | hots[1] | hots[2] | hots[3], 1.0, 0.0)
    r = lax.broadcasted_iota(I32, (tm, tm), 0)
    c = lax.broadcasted_iota(I32, (tm, tm), 1)
    earlier = jnp.where(r < c, 1.0, 0.0).astype(BF16)
    before = jnp.dot(multi_f.astype(BF16), earlier, preferred_element_type=F32) + cnt_ref[:, 0:1]
    cnt_ref[...] = cnt_ref[...] + jnp.sum(multi_f, axis=1, keepdims=True)

    ranks = [jnp.sum(jnp.where(hots[kk], before, 0.0), axis=0, keepdims=True) for kk in range(TOP_K)]
    gates = [e / denom for e in exps]
    route_ref[...] = jnp.concatenate(idxs + gates + ranks + [jnp.zeros((ROUTE_ROWS - 3 * TOP_K, tm), F32)], axis=0)


def _outproj(x2, y_ssd, y_mla, wo, g_ffn, wr, br):
    t = x2.shape[0]
    tm = TM_PROJ
    full = lambda shape: pl.BlockSpec(shape, lambda i: (0,) * len(shape))
    rows = lambda width: pl.BlockSpec((tm, width), lambda i: (i, 0))
    return pl.pallas_call(
        _outproj_kernel,
        grid=(t // tm,),
        in_specs=[rows(D_MODEL), rows(512), rows(512), full((1024, D_MODEL)), full((1, D_MODEL)),
                  full((D_MODEL, 2 * LANES)), full((1, LANES))],
        out_specs=[rows(D_MODEL), rows(HALF), pl.BlockSpec((ROUTE_ROWS, tm), lambda i: (0, i)),
                   full((N_EXPERTS, LANES))],
        out_shape=[jax.ShapeDtypeStruct((t, D_MODEL), F32),
                   jax.ShapeDtypeStruct((t, HALF), I32),
                   jax.ShapeDtypeStruct((ROUTE_ROWS, t), F32),
                   jax.ShapeDtypeStruct((N_EXPERTS, LANES), F32)],
        compiler_params=pltpu.CompilerParams(
            dimension_semantics=("arbitrary",), vmem_limit_bytes=VMEM_LIMIT_BYTES),
        name="outproj_router",
    )(x2, y_ssd, y_mla, wo, g_ffn, wr, br)


def _ffn_kernel(be_ref, bv_ref, nx_ref, sl_ref, xs_ref, wgu_hbm, bgu_ref, wd_hbm, bd_ref, ys_ref,
                wgu_st, wd_st, wgu_bf, wd_bf, sem):
    i = pl.program_id(0)
    e = be_ref[i]
    valid = bv_ref[i]
    slot = sl_ref[i]
    first = ((i == 0) | (e != be_ref[jnp.maximum(i - 1, 0)])) & (valid > 0)

    def weight_copies(expert, dst_slot):
        return (pltpu.make_async_copy(wgu_hbm.at[expert], wgu_st.at[dst_slot], sem.at[0, dst_slot]),
                pltpu.make_async_copy(wd_hbm.at[expert], wd_st.at[dst_slot], sem.at[1, dst_slot]))

    @pl.when(i == 0)
    def _():
        for cp in weight_copies(e, slot):
            cp.start()

    @pl.when(first)
    def _():
        for cp in weight_copies(e, slot):
            cp.wait()

        @pl.when(nx_ref[i] >= 0)
        def _():
            for cp in weight_copies(nx_ref[i], 1 - slot):
                cp.start(priority=1)

        wgu_bf[...] = wgu_st[slot].astype(BF16)
        wd_bf[...] = wd_st[slot].astype(BF16)

    def chain(r0, n_rows, masked):
        rows = pl.ds(r0, n_rows)
        xp = xs_ref[rows, :]
        if masked:
            xp = jnp.where(r0 + lax.broadcasted_iota(I32, (n_rows, 1), 0) < valid, xp, 0)
        x_lo, x_hi = _unpack_rows(xp)
        gu = (jnp.dot(x_lo.astype(BF16), wgu_bf[0:HALF, :], preferred_element_type=F32)
              + jnp.dot(x_hi.astype(BF16), wgu_bf[HALF:D_MODEL, :], preferred_element_type=F32)
              + bgu_ref[0])
        gate = jnp.minimum(gu[:, :D_FF], SWIGLU_LIMIT)
        up = jnp.clip(gu[:, D_FF:], -SWIGLU_LIMIT, SWIGLU_LIMIT)
        glu = gate * jax.nn.sigmoid(SWIGLU_ALPHA * gate)
        mid = ((up + 1.0) * glu).astype(BF16)
        ys_ref[rows, :] = _pack_rows(jnp.dot(mid, wd_bf[...], preferred_element_type=F32) + bd_ref[0])

    @pl.when(valid == MOE_STEP)
    def _():
        for r0 in range(0, MOE_STEP, MOE_BM):
            chain(r0, MOE_BM, False)

    @pl.when(valid < MOE_STEP)
    def _():
        half_bm = MOE_BM // 2
        for r0 in range(0, MOE_STEP, MOE_BM):
            pl.when(valid > r0 + half_bm)(functools.partial(chain, r0, MOE_BM, True))
            pl.when((valid > r0) & (valid <= r0 + half_bm))(functools.partial(chain, r0, half_bm, True))

            @pl.when(valid <= r0 + half_bm)
            def _():
                ys_ref[pl.ds(r0 + half_bm, half_bm), :] = jnp.zeros((half_bm, HALF), I32)

            @pl.when(valid <= r0)
            def _():
                ys_ref[pl.ds(r0, half_bm), :] = jnp.zeros((half_bm, HALF), I32)


def _expert_ffn(block_e, block_valid, block_next, block_slot, xs, wgu, bgu, wd, bd):
    n_slots = xs.shape[0]
    bm = MOE_STEP
    grid_spec = pltpu.PrefetchScalarGridSpec(
        num_scalar_prefetch=4,
        grid=(n_slots // bm,),
        in_specs=[
            pl.BlockSpec((bm, HALF), lambda i, be, bv, nx, sl: (i, 0)),
            pl.BlockSpec(memory_space=pl.ANY),
            pl.BlockSpec((1, 1, 2 * D_FF), lambda i, be, bv, nx, sl: (be[i], 0, 0)),
            pl.BlockSpec(memory_space=pl.ANY),
            pl.BlockSpec((1, 1, D_MODEL), lambda i, be, bv, nx, sl: (be[i], 0, 0)),
        ],
        out_specs=pl.BlockSpec((bm, HALF), lambda i, be, bv, nx, sl: (i, 0)),
        scratch_shapes=[pltpu.VMEM((2, D_MODEL, 2 * D_FF), F32), pltpu.VMEM((2, D_FF, D_MODEL), F32),
                        pltpu.VMEM((D_MODEL, 2 * D_FF), BF16), pltpu.VMEM((D_FF, D_MODEL), BF16),
                        pltpu.SemaphoreType.DMA((2, 2))],
    )
    return pl.pallas_call(
        _ffn_kernel,
        grid_spec=grid_spec,
        out_shape=jax.ShapeDtypeStruct((n_slots, HALF), I32),
        compiler_params=pltpu.CompilerParams(
            dimension_semantics=("arbitrary",), vmem_limit_bytes=VMEM_LIMIT_BYTES),
        name="expert_ffn",
    )(block_e, block_valid, block_next, block_slot, xs, wgu, bgu, wd, bd)


def _plan_kernel(sp_ref, route_ref, dest_ref):
    idx = route_ref[0:TOP_K, :]
    rank = route_ref[2 * TOP_K:3 * TOP_K, :]
    start = jnp.zeros(idx.shape, F32)
    for e_i in range(N_EXPERTS):
        start = jnp.where(idx == float(e_i), sp_ref[e_i].astype(F32), start)
    dest_ref[...] = (start + rank).astype(I32)


def _slot_plan(route, start_pad):
    t = route.shape[1]
    tm = TM_PLAN
    grid_spec = pltpu.PrefetchScalarGridSpec(
        num_scalar_prefetch=1,
        grid=(t // tm,),
        in_specs=[pl.BlockSpec((ROUTE_ROWS, tm), lambda i, sp: (0, i))],
        out_specs=pl.BlockSpec((TOP_K, tm), lambda i, sp: (0, i)),
    )
    return pl.pallas_call(
        _plan_kernel,
        grid_spec=grid_spec,
        out_shape=jax.ShapeDtypeStruct((TOP_K, t), I32),
        compiler_params=pltpu.CompilerParams(dimension_semantics=("arbitrary",)),
        name="slot_plan",
    )(start_pad, route)


SC_SCATTER_CHUNK = 32
SC_GATHER_CHUNK = 16


def _sc_workers():
    info = plsc.get_sparse_core_info()
    return info.num_cores, info.num_cores * info.num_subcores


def _sc_scatter_rows(rows, dest_km, n_out):
    t, w = rows.shape
    ch = SC_SCATTER_CHUNK
    n_cores, n_workers = _sc_workers()
    n_chunks = t // n_workers // ch
    mesh = plsc.VectorSubcoreMesh(core_axis_name="c", subcore_axis_name="s")

    @functools.partial(
        pl.kernel, mesh=mesh, out_type=jax.ShapeDtypeStruct((n_out, w), rows.dtype),
        scratch_types=[pltpu.VMEM((TOP_K, n_chunks, ch), I32), pltpu.VMEM((2, ch, w), rows.dtype),
                       pltpu.SemaphoreType.DMA((2,)), pltpu.SemaphoreType.DMA((2,))],
        name="sc_dispatch_scatter")
    def scatter_kernel(rows_hbm, dest_hbm, out_hbm, idx_v, rows_v, sem_in, sem_out):
        wid = lax.axis_index("s") * n_cores + lax.axis_index("c")
        first = wid * n_chunks
        for kk in range(TOP_K):
            pltpu.sync_copy(dest_hbm.at[kk, pl.ds(first, n_chunks)], idx_v.at[kk])

        def load(cc, b):
            return pltpu.make_async_copy(rows_hbm.at[pl.ds((first + cc) * ch, ch)], rows_v.at[b], sem_in.at[b])

        def scatters(cc, b):
            return [pltpu.make_async_copy(rows_v.at[b], out_hbm.at[idx_v.at[kk, cc]], sem_out.at[b])
                    for kk in range(TOP_K)]

        load(0, 0).start()
        load(1, 1).start()

        @pl.loop(0, n_chunks, step=2)
        def _(c):
            for b in range(2):
                load(c + b, b).wait()
                for cp in scatters(c + b, b):
                    cp.start()
            for b in range(2):
                for cp in scatters(c + b, b):
                    cp.wait()

                @pl.when(c + 2 + b < n_chunks)
                def _():
                    load(c + 2 + b, b).start()

    return scatter_kernel(rows, dest_km.reshape(TOP_K, t // ch, ch))


def _sc_gather_rows(table, dest_km):
    _, w = table.shape
    t = dest_km.shape[1]
    ch = SC_GATHER_CHUNK
    n_cores, n_workers = _sc_workers()
    n_chunks = t // n_workers // ch
    mesh = plsc.VectorSubcoreMesh(core_axis_name="c", subcore_axis_name="s")

    @functools.partial(
        pl.kernel, mesh=mesh, out_type=jax.ShapeDtypeStruct((TOP_K, t, w), table.dtype),
        scratch_types=[pltpu.VMEM((TOP_K, n_chunks, ch), I32), pltpu.VMEM((2, TOP_K, ch, w), table.dtype),
                       pltpu.SemaphoreType.DMA((2,)), pltpu.SemaphoreType.DMA((2,))],
        name="sc_combine_gather")
    def gather_kernel(table_hbm, dest_hbm, out_hbm, idx_v, rows_v, sem_in, sem_out):
        wid = lax.axis_index("s") * n_cores + lax.axis_index("c")
        first = wid * n_chunks
        for kk in range(TOP_K):
            pltpu.sync_copy(dest_hbm.at[kk, pl.ds(first, n_chunks)], idx_v.at[kk])

        def gathers(cc, b):
            return [pltpu.make_async_copy(table_hbm.at[idx_v.at[kk, cc]], rows_v.at[b, kk], sem_in.at[b])
                    for kk in range(TOP_K)]

        def stores(cc, b):
            return [pltpu.make_async_copy(rows_v.at[b, kk], out_hbm.at[kk, pl.ds((first + cc) * ch, ch)],
                                          sem_out.at[b]) for kk in range(TOP_K)]

        for b in range(2):
            for cp in gathers(b, b):
                cp.start()

        @pl.loop(0, n_chunks, step=2)
        def _(c):
            for b in range(2):
                for cp in gathers(c + b, b):
                    cp.wait()
                for cp in stores(c + b, b):
                    cp.start()
            for b in range(2):
                for cp in stores(c + b, b):
                    cp.wait()

                @pl.when(c + 2 + b < n_chunks)
                def _():
                    for cp in gathers(c + 2 + b, b):
                        cp.start()

    return gather_kernel(table, dest_km.reshape(TOP_K, t // ch, ch))


def _combine_kernel(x1_ref, yg_ref, route_ref, g_ref, o_ref, *, final_norm):
    tm = x1_ref.shape[0]
    moe_lo = jnp.zeros((tm, HALF), F32)
    moe_hi = jnp.zeros((tm, HALF), F32)
    route_t = jnp.concatenate([route_ref[...], jnp.zeros((LANES - ROUTE_ROWS, tm), F32)], axis=0).T
    for kk in range(TOP_K):
        gate = route_t[:, TOP_K + kk:TOP_K + kk + 1]
        y_lo, y_hi = _unpack_rows(yg_ref[kk])
        moe_lo = moe_lo + gate * y_lo
        moe_hi = moe_hi + gate * y_hi
    acc = x1_ref[...] + jnp.concatenate([moe_lo, moe_hi], axis=1)
    o_ref[...] = _rms(acc) * g_ref[...] if final_norm else acc


def _combine(x1, yg, route, g_final, final_norm):
    t = x1.shape[0]
    tm = TM_COMB
    return pl.pallas_call(
        functools.partial(_combine_kernel, final_norm=final_norm),
        grid=(t // tm,),
        in_specs=[pl.BlockSpec((tm, D_MODEL), lambda i: (i, 0)),
                  pl.BlockSpec((TOP_K, tm, HALF), lambda i: (0, i, 0)),
                  pl.BlockSpec((ROUTE_ROWS, tm), lambda i: (0, i)),
                  pl.BlockSpec((1, D_MODEL), lambda i: (0, 0))],
        out_specs=pl.BlockSpec((tm, D_MODEL), lambda i: (i, 0)),
        out_shape=jax.ShapeDtypeStruct((t, D_MODEL), F32),
        compiler_params=pltpu.CompilerParams(dimension_semantics=("arbitrary",)),
        name="combine",
    )(x1, yg, route, g_final)


def _prep_in_weights(w_in, w_uq, w_ukv):
    w_z = w_in[:, 0:512]
    w_xbc = w_in[:, 512:1536]
    w_dt = w_in[:, 1536:1544]
    w_cq = w_in[:, 1544:1800]
    w_ckv = w_in[:, 1800:1928]
    w_kr = w_in[:, 1928:1960]
    half = MLA_ROPE // 2
    zeros = lambda rows, width: jnp.zeros(rows + (width,), BF16)
    cat = lambda parts: jnp.concatenate([p.astype(BF16) for p in parts], axis=-1)
    d = (D_MODEL,)
    misc1 = [w_dt, zeros(d, MLA_NOPE - SSD_HEADS), w_kr, zeros(d, LANES - MLA_QK)]
    misc2 = [zeros(d, MLA_NOPE), w_kr[:, half:], w_kr[:, :half], zeros(d, LANES - MLA_QK)]
    w1 = cat([w_z, w_xbc, w_cq, w_ckv] + misc1 + misc2)

    wq3 = w_uq.reshape(MLA_Q_RANK, MLA_HEADS, MLA_QK)
    qh = (MLA_Q_RANK, MLA_HEADS)
    main = cat([wq3, zeros(qh, LANES - MLA_QK)])
    swap = cat([zeros(qh, MLA_NOPE), wq3[:, :, MLA_NOPE + half:], wq3[:, :, MLA_NOPE:MLA_NOPE + half],
                zeros(qh, LANES - MLA_QK)])
    wq = jnp.concatenate([main.reshape(MLA_Q_RANK, -1), swap.reshape(MLA_Q_RANK, -1)], axis=1)

    wkv3 = w_ukv.reshape(MLA_KV_RANK, MLA_HEADS, MLA_NOPE + MLA_V)
    kh = (MLA_KV_RANK, MLA_HEADS)
    kpart = cat([wkv3[:, :, :MLA_NOPE], zeros(kh, LANES - MLA_NOPE)])
    vpart = wkv3[:, :, MLA_NOPE:].astype(BF16)
    wkv = jnp.concatenate([kpart.reshape(MLA_KV_RANK, -1), vpart.reshape(MLA_KV_RANK, -1)], axis=1)
    return w1, wq, wkv


def _rope_inv_freq():
    inv_freq = ROPE_THETA ** (-jnp.arange(0, MLA_ROPE, 2, dtype=F32) / MLA_ROPE)
    return inv_freq[:, None]


def _pad_lanes(v, fill=0.0):
    return jnp.full((1, LANES), fill, F32).at[0, :v.shape[0]].set(v)


def kernel(x, positions, norm_mix_g, w_in, conv_w, conv_b, dt_bias, a_log, d_skip, ssd_norm_g, q_norm_g, w_uq, kv_norm_g, w_ukv, w_out, norm_ffn_g, w_router, b_router, w_gate_up, b_gate_up, w_down, b_down, norm_final_g):
    bsz, seqlen, d = x.shape
    t = bsz * seqlen
    depth = w_in.shape[0]
    x2 = x.reshape(t, d)
    pos_rows = positions.reshape(t // TM_PROJ, 1, TM_PROJ).astype(I32)
    invf = _rope_inv_freq()

    for l in range(depth):
        w1, wq, wkv = _prep_in_weights(w_in[l], w_uq[l], w_ukv[l])
        z, xbc, dtm, q, k, v = _inproj(
            x2, pos_rows, norm_mix_g[l][None, :], w1, q_norm_g[l][None, :], wq, kv_norm_g[l][None, :], wkv,
            invf, bsz, seqlen)
        y_ssd = _ssd(z, xbc, dtm, conv_w[l], conv_b[l][None, :], _pad_lanes(dt_bias[l]), _pad_lanes(a_log[l]),
                     jnp.repeat(d_skip[l], SSD_HEAD_DIM)[None, :], ssd_norm_g[l][None, :], bsz, seqlen)
        y_mla = _attention(q, k, v, bsz, seqlen).reshape(t, MLA_WIDTH)

        wr = jnp.zeros((d, LANES), F32).at[:, :N_EXPERTS].set(w_router[l])
        wr_hi = wr.astype(BF16)
        wr_lo = (wr - wr_hi.astype(F32)).astype(BF16)
        x1, h2p, route, cnt = _outproj(x2, y_ssd, y_mla, w_out[l].astype(BF16), norm_ffn_g[l][None, :],
                                       jnp.concatenate([wr_hi, wr_lo], axis=1), _pad_lanes(b_router[l]))

        counts = cnt[:, 0].astype(I32)
        padded = ((counts + MOE_STEP - 1) // MOE_STEP) * MOE_STEP
        end_pad = jnp.cumsum(padded)
        start_pad = end_pad - padded
        n_slots = t * TOP_K + N_EXPERTS * MOE_STEP
        n_blocks = n_slots // MOE_STEP
        block_start = jnp.arange(n_blocks, dtype=I32) * MOE_STEP
        block_e = jnp.minimum(jnp.sum(block_start[:, None] >= end_pad[None, :], axis=1), N_EXPERTS - 1).astype(I32)
        eids = jnp.arange(N_EXPERTS, dtype=I32)
        block_hot = block_e[:, None] == eids[None, :]
        per_block = lambda table: jnp.sum(jnp.where(block_hot, table[None, :], 0), axis=1).astype(I32)
        block_valid = jnp.clip(per_block(counts) - (block_start - per_block(start_pad)), 0, MOE_STEP).astype(I32)
        used = counts > 0
        later_used = jnp.where((eids[None, :] > eids[:, None]) & used[None, :], eids[None, :], N_EXPERTS)
        next_used = jnp.min(later_used, axis=1)
        next_used = jnp.where(next_used < N_EXPERTS, next_used, -1).astype(I32)
        stage_slot = ((jnp.cumsum(used.astype(I32)) - 1) & 1).astype(I32)
        dest_km = _slot_plan(route, start_pad.astype(I32))

        xs = _sc_scatter_rows(h2p, dest_km, n_slots)
        ys = _expert_ffn(block_e, block_valid, per_block(next_used), per_block(stage_slot), xs,
                         w_gate_up[l], b_gate_up[l][:, None, :], w_down[l], b_down[l][:, None, :])
        yg = _sc_gather_rows(ys, dest_km)
        x2 = _combine(x1, yg, route, norm_final_g[None, :], l == depth - 1)
    return x2.reshape(bsz, seqlen, d)
```

```python
import functools

import jax
import jax.numpy as jnp
import numpy as np
from jax import lax
from jax.experimental import pallas as pl
from jax.experimental.pallas import tpu as pltpu
from jax.experimental.pallas import tpu_sc as plsc

F32 = jnp.float32
BF16 = jnp.bfloat16
I32 = jnp.int32

D_MODEL = 1024
EPS = 1e-6
LANES = 128
V7X_VMEM_BYTES = 64 * 1024 * 1024
VMEM_LIMIT_BYTES = V7X_VMEM_BYTES * 7 // 8

SSD_HEADS = 8
SSD_HEAD_DIM = 64
SSD_WIDTH = 512
SSD_STATE = 128
SSD_CONV = 4
SSD_CHUNK = 128
SSD_CONV_DIM = 1024
CONV_PAD = 8
SSD_CPS = 8

MLA_HEADS = 8
MLA_Q_RANK = 256
MLA_KV_RANK = 128
MLA_NOPE = 64
MLA_ROPE = 32
MLA_V = 64
MLA_QK = MLA_NOPE + MLA_ROPE
MLA_WIDTH = 512
ROPE_THETA = 10000.0
LOG2_E = 1.4426950408889634

N_EXPERTS = 32
TOP_K = 4
D_FF = 1024
SWIGLU_LIMIT = 7.0
SWIGLU_ALPHA = 1.702

IN_W = 512 + 1024 + 256 + 128 + 128 + 128

TM_PROJ = 1024
TQ = 512
TK = 512
ATT_HPS = 4
VT_ROWS = 80
MOE_BM = 256
MOE_STEP = 1024
TM_COMB = 512
TM_PLAN = 2048
ROUTE_ROWS = 16

NT_DIMS = (((1,), (1,)), ((), ()))


def _rms(x):
    return x * lax.rsqrt(jnp.mean(x * x, axis=-1, keepdims=True) + EPS)


HALF = D_MODEL // 2
HI_MASK = np.int32(-65536)


def _pack_rows(a):
    lo = lax.bitcast_convert_type(a[:, :HALF].astype(BF16).astype(F32), I32)
    hi = lax.bitcast_convert_type(a[:, HALF:].astype(BF16).astype(F32), I32)
    return (hi & HI_MASK) | lax.shift_right_logical(lo, 16)


def _unpack_rows(p):
    lo = lax.bitcast_convert_type(lax.shift_left(p, 16), F32)
    hi = lax.bitcast_convert_type(p & HI_MASK, F32)
    return lo, hi


def _inproj_kernel(x_ref, pos_ref, g_ref, w1_ref, qg_ref, wq_ref, kvg_ref, wkv_ref, invf_ref,
                   z_ref, xbc_ref, dtm_ref, q_ref, k_ref, vt_ref):
    x = x_ref[...]
    h = (_rms(x) * g_ref[...]).astype(BF16)
    p = jnp.dot(h, w1_ref[...], preferred_element_type=F32)
    z_ref[...] = p[:, 0:512]
    xbc_ref[...] = p[:, 512:1536]
    cq = p[:, 1536:1792]
    ckv = p[:, 1792:1920]
    m1 = p[:, 1920:2048]
    m2 = p[:, 2048:2176]
    dtm_ref[...] = m1

    lane = lax.broadcasted_iota(I32, (1, LANES), 1)
    tm = x.shape[0]
    ang = invf_ref[...] * pos_ref[0].astype(F32)
    cos_c = jnp.cos(ang)
    sin_c = jnp.sin(ang)
    z_lo = jnp.zeros((MLA_NOPE, tm), F32)
    z_hi = jnp.zeros((LANES - MLA_QK, tm), F32)
    cos_t = jnp.concatenate([z_lo, cos_c, cos_c, z_hi], axis=0).T
    sin_t = jnp.concatenate([z_lo, -sin_c, sin_c, z_hi], axis=0).T
    cosq_t = jnp.where(lane < MLA_NOPE, 1.0, cos_t)
    scale = MLA_QK ** -0.5 * LOG2_E

    cqn = (_rms(cq) * qg_ref[...]).astype(BF16)
    qq = jnp.dot(cqn, wq_ref[...], preferred_element_type=F32)
    ckvn = (_rms(ckv) * kvg_ref[...]).astype(BF16)
    kv = jnp.dot(ckvn, wkv_ref[...], preferred_element_type=F32)
    krot = m1 * cos_t + m2 * sin_t
    for h_i in range(MLA_HEADS):
        lo = h_i * LANES
        qm = qq[:, lo:lo + LANES]
        qs = qq[:, 1024 + lo:1024 + lo + LANES]
        q_ref[0, h_i] = ((qm * cosq_t + qs * sin_t) * scale).astype(BF16)
        k_ref[0, h_i] = (kv[:, lo:lo + LANES] + krot).astype(BF16)
    ones_rows = jnp.ones((VT_ROWS - MLA_V, tm), BF16)
    for pr in range(MLA_HEADS // 2):
        vpt = kv[:, 1024 + pr * LANES:1024 + (pr + 1) * LANES].T.astype(BF16)
        vt_ref[0, 2 * pr] = jnp.concatenate([vpt[0:MLA_V, :], ones_rows], axis=0)
        vt_ref[0, 2 * pr + 1] = jnp.concatenate([vpt[MLA_V:2 * MLA_V, :], ones_rows], axis=0)


def _inproj(x2, pos_rows, g_mix, w1, qg, wq, kvg, wkv, invf, bsz, seqlen):
    t = x2.shape[0]
    tm = TM_PROJ
    per_b = seqlen // tm
    full = lambda shape: pl.BlockSpec(shape, lambda i: (0,) * len(shape))
    head_spec = pl.BlockSpec((1, MLA_HEADS, tm, LANES), lambda i: (i // per_b, 0, i % per_b, 0))
    head_shape = jax.ShapeDtypeStruct((bsz, MLA_HEADS, seqlen, LANES), BF16)
    vt_spec = pl.BlockSpec((1, MLA_HEADS, VT_ROWS, tm), lambda i: (i // per_b, 0, 0, i % per_b))
    vt_shape = jax.ShapeDtypeStruct((bsz, MLA_HEADS, VT_ROWS, seqlen), BF16)
    return pl.pallas_call(
        _inproj_kernel,
        grid=(t // tm,),
        in_specs=[
            pl.BlockSpec((tm, D_MODEL), lambda i: (i, 0)),
            pl.BlockSpec((1, 1, tm), lambda i: (i, 0, 0)),
            full((1, D_MODEL)), full((D_MODEL, IN_W)),
            full((1, MLA_Q_RANK)), full((MLA_Q_RANK, 2048)),
            full((1, MLA_KV_RANK)), full((MLA_KV_RANK, 1536)),
            full((MLA_ROPE // 2, 1)),
        ],
        out_specs=[
            pl.BlockSpec((tm, 512), lambda i: (i, 0)),
            pl.BlockSpec((tm, 1024), lambda i: (i, 0)),
            pl.BlockSpec((tm, LANES), lambda i: (i, 0)),
            head_spec, head_spec, vt_spec,
        ],
        out_shape=[
            jax.ShapeDtypeStruct((t, 512), F32),
            jax.ShapeDtypeStruct((t, 1024), F32),
            jax.ShapeDtypeStruct((t, LANES), F32),
            head_shape, head_shape, vt_shape,
        ],
        compiler_params=pltpu.CompilerParams(
            dimension_semantics=("arbitrary",), vmem_limit_bytes=VMEM_LIMIT_BYTES),
        name="inproj",
    )(x2, pos_rows, g_mix, w1, qg, wq, kvg, wkv, invf)


def _ssd_kernel(z_ref, xbc_ref, dtm_ref, cw_ref, cb_ref, dtb_ref, alog_ref, dsk_ref, ng_ref,
                y_ref, ext_ref, st_ref):
    q = SSD_CHUNK
    rows = SSD_CPS * q

    @pl.when(pl.program_id(1) == 0)
    def _():
        ext_ref[0:CONV_PAD, :] = jnp.zeros((CONV_PAD, SSD_CONV_DIM), F32)
        st_ref[...] = jnp.zeros_like(st_ref)

    ext_ref[CONV_PAD:CONV_PAD + rows, :] = xbc_ref[...]

    lane = lax.broadcasted_iota(I32, (1, LANES), 1)
    row = lax.broadcasted_iota(I32, (q, q), 0)
    col = lax.broadcasted_iota(I32, (q, q), 1)
    tril = row >= col
    tril_b = jnp.where(tril, 1.0, 0.0).astype(BF16)
    spread = jnp.where(
        lax.broadcasted_iota(I32, (LANES, SSD_WIDTH), 0)
        == lax.broadcasted_iota(I32, (LANES, SSD_WIDTH), 1) // SSD_HEAD_DIM, 1.0, 0.0).astype(BF16)
    a_neg = -jnp.exp(alog_ref[...]) * LOG2_E

    def split3(v):
        hi = v.astype(BF16)
        r1 = v - hi.astype(F32)
        mid = r1.astype(BF16)
        return hi, mid, (r1 - mid.astype(F32)).astype(BF16)

    def dot3_right(parts, m):
        return sum(jnp.dot(p, m, preferred_element_type=F32) for p in parts)

    def expand(cols):
        return dot3_right(split3(cols), spread)

    for ci in range(SSD_CPS):
        lo = ci * q
        window = ext_ref[lo:lo + CONV_PAD + q, :]
        conv = cb_ref[...] + cw_ref[SSD_CONV - 1:SSD_CONV, :] * window[CONV_PAD:, :]
        for kk in range(SSD_CONV - 1):
            shifted = pltpu.roll(window, SSD_CONV - 1 - kk, axis=0)
            conv = conv + cw_ref[kk:kk + 1, :] * shifted[CONV_PAD:, :]
        u = conv * jax.nn.sigmoid(conv)
        xs = u[:, 0:512]
        bm = u[:, 512:768]
        cm = u[:, 768:1024]

        xdt = dtm_ref[lo:lo + q, :] + dtb_ref[...]
        dt = jnp.maximum(xdt, 0.0) + jnp.log1p(jnp.exp(-jnp.abs(xdt)))
        adt = jnp.where(lane < SSD_HEADS, dt * a_neg, 0.0)
        cum_col = sum(jnp.dot(tril_b, p, preferred_element_type=F32) for p in split3(adt))
        cum_row = cum_col.T

        dt_e = expand(dt)
        ac_e = expand(cum_col)
        last_e = ac_e[q - 1:q, :]
        xd = xs * dt_e
        w_end = xd * jnp.exp2(last_e - ac_e)
        eac = jnp.exp2(ac_e)
        cdec = jnp.exp2(last_e)

        y_parts = []
        for g in range(2):
            gl = g * 256
            bg = bm[:, g * SSD_STATE:(g + 1) * SSD_STATE]
            cg = cm[:, g * SSD_STATE:(g + 1) * SSD_STATE].astype(BF16)
            scores = lax.dot_general(cg, bg.astype(BF16), NT_DIMS, preferred_element_type=F32)
            bgt = bg.T.astype(BF16)
            sprev = st_ref[g]
            yoff = jnp.dot(cg, sprev.astype(BF16), preferred_element_type=F32)
            st_ref[g] = sprev * cdec[:, gl:gl + 256] + jnp.dot(
                bgt, w_end[:, gl:gl + 256].astype(BF16), preferred_element_type=F32)
            for pr in range(2):
                pl_lo = gl + pr * LANES
                xdp = xd[:, pl_lo:pl_lo + LANES].astype(BF16)
                res = []
                for jj in range(2):
                    h_i = g * 4 + pr * 2 + jj
                    seg = cum_col[:, h_i:h_i + 1] - cum_row[h_i:h_i + 1, :]
                    dec = jnp.exp2(jnp.where(tril, seg, -jnp.inf))
                    res.append(jnp.dot((scores * dec).astype(BF16), xdp, preferred_element_type=F32))
                ydiag = jnp.where(lane < SSD_HEAD_DIM, res[0], res[1])
                y_parts.append(ydiag + yoff[:, pr * LANES:(pr + 1) * LANES] * eac[:, pl_lo:pl_lo + LANES])
        y = jnp.concatenate(y_parts, axis=1) + dsk_ref[...] * xs
        zz = z_ref[lo:lo + q, :]
        y = y * (zz * jax.nn.sigmoid(zz))
        outs = []
        for g in range(2):
            yg = y[:, g * 256:(g + 1) * 256]
            outs.append(_rms(yg))
        y_ref[lo:lo + q, :] = (jnp.concatenate(outs, axis=1) * ng_ref[...]).astype(BF16)

    ext_ref[0:CONV_PAD, :] = ext_ref[rows:rows + CONV_PAD, :]


def _ssd(z, xbc, dtm, cw, cb, dtb, alog, dsk, ng, bsz, seqlen):
    t = z.shape[0]
    q = SSD_CHUNK
    rows = SSD_CPS * q
    nc = seqlen // rows
    full = lambda shape: pl.BlockSpec(shape, lambda b, c: (0,) * len(shape))
    row_spec = lambda width: pl.BlockSpec((rows, width), lambda b, c: (b * nc + c, 0))
    return pl.pallas_call(
        _ssd_kernel,
        grid=(bsz, nc),
        in_specs=[row_spec(512), row_spec(1024), row_spec(LANES),
                  full((SSD_CONV, SSD_CONV_DIM)), full((1, SSD_CONV_DIM)),
                  full((1, LANES)), full((1, LANES)), full((1, SSD_WIDTH)), full((1, SSD_WIDTH))],
        out_specs=row_spec(512),
        out_shape=jax.ShapeDtypeStruct((t, SSD_WIDTH), BF16),
        scratch_shapes=[pltpu.VMEM((rows + CONV_PAD, SSD_CONV_DIM), F32),
                        pltpu.VMEM((2, SSD_STATE, 256), F32)],
        compiler_params=pltpu.CompilerParams(dimension_semantics=("arbitrary", "arbitrary")),
        name="ssd",
    )(z, xbc, dtm, cw, cb, dtb, alog, dsk, ng)


def _attn_kernel(q_ref, k_ref, vt_ref, o_ref, acc_ref, s_ref, bmax_ref):
    i = pl.program_id(2)
    acc_ref[...] = jnp.zeros_like(acc_ref)

    def col_max(st):
        part = st[0:LANES, :]
        for r0 in range(LANES, TK, LANES):
            part = jnp.maximum(part, st[r0:r0 + LANES, :])
        return jnp.broadcast_to(jnp.max(part, axis=0, keepdims=True), (8, TQ))

    def scores(j, slot):
        start = pl.multiple_of(j * TK, TK)
        for hh in range(ATT_HPS):
            kb = k_ref[0, hh, pl.ds(start, TK), :]
            st = lax.dot_general(kb, q_ref[0, hh], NT_DIMS, preferred_element_type=F32)
            s_ref[slot, hh] = st
            bmax_ref[slot, hh] = col_max(st)

    def consume(j, slot, m_all, masked):
        start = pl.multiple_of(j * TK, TK)
        new_m = []
        for hh in range(ATT_HPS):
            vt = vt_ref[0, hh, :, pl.ds(start, TK)]
            st = s_ref[slot, hh]
            if masked:
                key = lax.broadcasted_iota(I32, (TK, TQ), 0)
                qry = lax.broadcasted_iota(I32, (TK, TQ), 1)
                st = jnp.where(key <= qry, st, -jnp.inf)
                block_max = col_max(st)
            else:
                block_max = bmax_ref[slot, hh]
            m_old = m_all[hh]
            m_new = jnp.maximum(m_old, block_max)
            alpha = jnp.exp2(m_old - m_new)
            p = jnp.exp2(st - m_new[0:1, :]).astype(BF16)
            acc_ref[hh] = acc_ref[hh] * alpha[0:1, :] + jnp.dot(vt, p, preferred_element_type=F32)
            new_m.append(m_new)
        return tuple(new_m)

    def pair(p, m_all):
        scores(2 * p + 1, 1)
        m_all = consume(2 * p, 0, m_all, False)
        scores(2 * p + 2, 0)
        return consume(2 * p + 1, 1, m_all, False)

    def odd_tail(_, m_all):
        scores(i, 1)
        return consume(i - 1, 0, m_all, False)

    def quad(g, m_all):
        return pair(2 * g + 1, pair(2 * g, m_all))

    m0 = jnp.full((8, TQ), -jnp.inf, F32)
    scores(0, 0)
    m_all = lax.fori_loop(0, i // 4, quad, (m0,) * ATT_HPS)
    m_all = lax.fori_loop(2 * (i // 4), i // 2, pair, m_all)
    m_all = lax.fori_loop(0, i & 1, odd_tail, m_all)

    @pl.when((i & 1) == 0)
    def _():
        consume(i, 0, m_all, True)

    @pl.when((i & 1) == 1)
    def _():
        consume(i, 1, m_all, True)

    outs = []
    for pr in range(ATT_HPS // 2):
        pair_t = []
        for hh in (2 * pr, 2 * pr + 1):
            a = acc_ref[hh]
            pair_t.append(a[0:MLA_V, :] / a[MLA_V:MLA_V + 1, :])
        outs.append(jnp.concatenate(pair_t, axis=0).T)
    o_ref[0] = jnp.concatenate(outs, axis=1).astype(BF16)


def _attention(q, k, vt, bsz, seqlen):
    nq = seqlen // TQ
    hps = ATT_HPS
    return pl.pallas_call(
        _attn_kernel,
        grid=(bsz, MLA_HEADS // hps, nq),
        in_specs=[pl.BlockSpec((1, hps, TQ, LANES), lambda b, p, i: (b, p, i, 0)),
                  pl.BlockSpec((1, hps, seqlen, LANES), lambda b, p, i: (b, p, 0, 0)),
                  pl.BlockSpec((1, hps, VT_ROWS, seqlen), lambda b, p, i: (b, p, 0, 0))],
        out_specs=pl.BlockSpec((1, TQ, hps * MLA_V), lambda b, p, i: (b, i, p)),
        out_shape=jax.ShapeDtypeStruct((bsz, seqlen, MLA_WIDTH), BF16),
        scratch_shapes=[pltpu.VMEM((hps, VT_ROWS, TQ), F32), pltpu.VMEM((2, hps, TK, TQ), F32),
                        pltpu.VMEM((2, hps, 8, TQ), F32)],
        compiler_params=pltpu.CompilerParams(
            dimension_semantics=("arbitrary", "arbitrary", "arbitrary"),
            vmem_limit_bytes=VMEM_LIMIT_BYTES),
        name="attention",
    )(q, k, vt)


def _outproj_kernel(x_ref, ys_ref, ym_ref, wo_ref, g_ref, wr_ref, br_ref,
                    x1_ref, h2_ref, route_ref, cnt_ref, wo_bf):
    tm = x_ref.shape[0]

    @pl.when(pl.program_id(0) == 0)
    def _():
        cnt_ref[...] = jnp.zeros_like(cnt_ref)
        wo_bf[...] = wo_ref[...].astype(BF16)

    mix = (jnp.dot(ys_ref[...], wo_bf[0:512, :], preferred_element_type=F32)
           + jnp.dot(ym_ref[...], wo_bf[512:1024, :], preferred_element_type=F32))
    x1 = x_ref[...] + mix
    x1_ref[...] = x1
    h2 = _rms(x1) * g_ref[...]
    h2_ref[...] = _pack_rows(h2)

    h_hi = h2.astype(BF16)
    h_lo = (h2 - h_hi.astype(F32)).astype(BF16)
    hh = jnp.dot(h_hi, wr_ref[...], preferred_element_type=F32)
    lh = jnp.dot(h_lo, wr_ref[:, 0:LANES], preferred_element_type=F32)
    logits = hh[:, 0:LANES] + (hh[:, LANES:2 * LANES] + lh) + br_ref[...]
    lt = logits.T[0:N_EXPERTS, :]
    eid = lax.broadcasted_iota(I32, (N_EXPERTS, 1), 0).astype(F32)

    vals, idxs, hots = [], [], []
    for _ in range(TOP_K):
        mx = jnp.max(lt, axis=0, keepdims=True)
        idx = jnp.min(jnp.where(lt == mx, eid, float(N_EXPERTS)), axis=0, keepdims=True)
        hot = eid == idx
        lt = jnp.where(hot, -jnp.inf, lt)
        vals.append(mx)
        idxs.append(idx)
        hots.append(hot)
    exps = [jnp.exp(v - vals[0]) for v in vals]
    denom = exps[0] + exps[1] + exps[2] + exps[3]

    multi_f = jnp.where(hots[0] | hots[1] | hots[2] | hots[3], 1.0, 0.0)
    r = lax.broadcasted_iota(I32, (tm, tm), 0)
    c = lax.broadcasted_iota(I32, (tm, tm), 1)
    earlier = jnp.where(r < c, 1.0, 0.0).astype(BF16)
    before = jnp.dot(multi_f.astype(BF16), earlier, preferred_element_type=F32) + cnt_ref[:, 0:1]
    cnt_ref[...] = cnt_ref[...] + jnp.sum(multi_f, axis=1, keepdims=True)

    ranks = [jnp.sum(jnp.where(hots[kk], before, 0.0), axis=0, keepdims=True) for kk in range(TOP_K)]
    gates = [e / denom for e in exps]
    route_ref[...] = jnp.concatenate(idxs + gates + ranks + [jnp.zeros((ROUTE_ROWS - 3 * TOP_K, tm), F32)], axis=0)


def _outproj(x2, y_ssd, y_mla, wo, g_ffn, wr, br):
    t = x2.shape[0]
    tm = TM_PROJ
    full = lambda shape: pl.BlockSpec(shape, lambda i: (0,) * len(shape))
    rows = lambda width: pl.BlockSpec((tm, width), lambda i: (i, 0))
    return pl.pallas_call(
        _outproj_kernel,
        grid=(t // tm,),
        in_specs=[rows(D_MODEL), rows(512), rows(512), full((1024, D_MODEL)), full((1, D_MODEL)),
                  full((D_MODEL, 2 * LANES)), full((1, LANES))],
        out_specs=[rows(D_MODEL), rows(HALF), pl.BlockSpec((ROUTE_ROWS, tm), lambda i: (0, i)),
                   full((N_EXPERTS, LANES))],
        out_shape=[jax.ShapeDtypeStruct((t, D_MODEL), F32),
                   jax.ShapeDtypeStruct((t, HALF), I32),
                   jax.ShapeDtypeStruct((ROUTE_ROWS, t), F32),
                   jax.ShapeDtypeStruct((N_EXPERTS, LANES), F32)],
        scratch_shapes=[pltpu.VMEM((1024, D_MODEL), BF16)],
        compiler_params=pltpu.CompilerParams(
            dimension_semantics=("arbitrary",), vmem_limit_bytes=VMEM_LIMIT_BYTES),
        name="outproj_router",
    )(x2, y_ssd, y_mla, wo, g_ffn, wr, br)


def _ffn_kernel(be_ref, bv_ref, nx_ref, sl_ref, xs_ref, wgu_hbm, bgu_ref, wd_hbm, bd_ref, ys_ref,
                wgu_st, wd_st, wgu_bf, wd_bf, sem):
    i = pl.program_id(0)
    e = be_ref[i]
    valid = bv_ref[i]
    slot = sl_ref[i]
    first = ((i == 0) | (e != be_ref[jnp.maximum(i - 1, 0)])) & (valid > 0)

    def weight_copies(expert, dst_slot):
        return (pltpu.make_async_copy(wgu_hbm.at[expert], wgu_st.at[dst_slot], sem.at[0, dst_slot]),
                pltpu.make_async_copy(wd_hbm.at[expert], wd_st.at[dst_slot], sem.at[1, dst_slot]))

    @pl.when(i == 0)
    def _():
        for cp in weight_copies(e, slot):
            cp.start()

    @pl.when(first)
    def _():
        for cp in weight_copies(e, slot):
            cp.wait()

        @pl.when(nx_ref[i] >= 0)
        def _():
            for cp in weight_copies(nx_ref[i], 1 - slot):
                cp.start(priority=1)

        wgu_bf[...] = wgu_st[slot].astype(BF16)
        wd_bf[...] = wd_st[slot].astype(BF16)

    def chain(r0, n_rows, masked):
        rows = pl.ds(r0, n_rows)
        xp = xs_ref[rows, :]
        if masked:
            xp = jnp.where(r0 + lax.broadcasted_iota(I32, (n_rows, 1), 0) < valid, xp, 0)
        x_lo, x_hi = _unpack_rows(xp)
        gu = (jnp.dot(x_lo.astype(BF16), wgu_bf[0:HALF, :], preferred_element_type=F32)
              + jnp.dot(x_hi.astype(BF16), wgu_bf[HALF:D_MODEL, :], preferred_element_type=F32)
              + bgu_ref[0])
        gate = jnp.minimum(gu[:, :D_FF], SWIGLU_LIMIT)
        up = jnp.clip(gu[:, D_FF:], -SWIGLU_LIMIT, SWIGLU_LIMIT)
        glu = gate * jax.nn.sigmoid(SWIGLU_ALPHA * gate)
        mid = ((up + 1.0) * glu).astype(BF16)
        ys_ref[rows, :] = _pack_rows(jnp.dot(mid, wd_bf[...], preferred_element_type=F32) + bd_ref[0])

    @pl.when(valid == MOE_STEP)
    def _():
        for r0 in range(0, MOE_STEP, MOE_BM):
            chain(r0, MOE_BM, False)

    @pl.when(valid < MOE_STEP)
    def _():
        half_bm = MOE_BM // 2
        for r0 in range(0, MOE_STEP, MOE_BM):
            pl.when(valid > r0 + half_bm)(functools.partial(chain, r0, MOE_BM, True))
            pl.when((valid > r0) & (valid <= r0 + half_bm))(functools.partial(chain, r0, half_bm, True))

            @pl.when(valid <= r0 + half_bm)
            def _():
                ys_ref[pl.ds(r0 + half_bm, half_bm), :] = jnp.zeros((half_bm, HALF), I32)

            @pl.when(valid <= r0)
            def _():
                ys_ref[pl.ds(r0, half_bm), :] = jnp.zeros((half_bm, HALF), I32)


def _expert_ffn(block_e, block_valid, block_next, block_slot, xs, wgu, bgu, wd, bd):
    n_slots = xs.shape[0]
    bm = MOE_STEP
    grid_spec = pltpu.PrefetchScalarGridSpec(
        num_scalar_prefetch=4,
        grid=(n_slots // bm,),
        in_specs=[
            pl.BlockSpec((bm, HALF), lambda i, be, bv, nx, sl: (i, 0)),
            pl.BlockSpec(memory_space=pl.ANY),
            pl.BlockSpec((1, 1, 2 * D_FF), lambda i, be, bv, nx, sl: (be[i], 0, 0)),
            pl.BlockSpec(memory_space=pl.ANY),
            pl.BlockSpec((1, 1, D_MODEL), lambda i, be, bv, nx, sl: (be[i], 0, 0)),
        ],
        out_specs=pl.BlockSpec((bm, HALF), lambda i, be, bv, nx, sl: (i, 0)),
        scratch_shapes=[pltpu.VMEM((2, D_MODEL, 2 * D_FF), F32), pltpu.VMEM((2, D_FF, D_MODEL), F32),
                        pltpu.VMEM((D_MODEL, 2 * D_FF), BF16), pltpu.VMEM((D_FF, D_MODEL), BF16),
                        pltpu.SemaphoreType.DMA((2, 2))],
    )
    return pl.pallas_call(
        _ffn_kernel,
        grid_spec=grid_spec,
        out_shape=jax.ShapeDtypeStruct((n_slots, HALF), I32),
        compiler_params=pltpu.CompilerParams(
            dimension_semantics=("arbitrary",), vmem_limit_bytes=VMEM_LIMIT_BYTES),
        name="expert_ffn",
    )(block_e, block_valid, block_next, block_slot, xs, wgu, bgu, wd, bd)


def _plan_kernel(sp_ref, route_ref, dest_ref):
    idx = route_ref[0:TOP_K, :]
    rank = route_ref[2 * TOP_K:3 * TOP_K, :]
    start = jnp.zeros(idx.shape, F32)
    for e_i in range(N_EXPERTS):
        start = jnp.where(idx == float(e_i), sp_ref[e_i].astype(F32), start)
    dest_ref[...] = (start + rank).astype(I32)


def _slot_plan(route, start_pad):
    t = route.shape[1]
    tm = TM_PLAN
    grid_spec = pltpu.PrefetchScalarGridSpec(
        num_scalar_prefetch=1,
        grid=(t // tm,),
        in_specs=[pl.BlockSpec((ROUTE_ROWS, tm), lambda i, sp: (0, i))],
        out_specs=pl.BlockSpec((TOP_K, tm), lambda i, sp: (0, i)),
    )
    return pl.pallas_call(
        _plan_kernel,
        grid_spec=grid_spec,
        out_shape=jax.ShapeDtypeStruct((TOP_K, t), I32),
        compiler_params=pltpu.CompilerParams(dimension_semantics=("arbitrary",)),
        name="slot_plan",
    )(start_pad, route)


SC_SCATTER_CHUNK = 32
SC_GATHER_CHUNK = 16


def _sc_workers():
    info = plsc.get_sparse_core_info()
    return info.num_cores, info.num_cores * info.num_subcores


def _sc_scatter_rows(rows, dest_km, n_out):
    t, w = rows.shape
    ch = SC_SCATTER_CHUNK
    n_cores, n_workers = _sc_workers()
    n_chunks = t // n_workers // ch
    mesh = plsc.VectorSubcoreMesh(core_axis_name="c", subcore_axis_name="s")

    @functools.partial(
        pl.kernel, mesh=mesh, out_type=jax.ShapeDtypeStruct((n_out, w), rows.dtype),
        scratch_types=[pltpu.VMEM((TOP_K, n_chunks, ch), I32), pltpu.VMEM((2, ch, w), rows.dtype),
                       pltpu.SemaphoreType.DMA((2,)), pltpu.SemaphoreType.DMA((2,))],
        name="sc_dispatch_scatter")
    def scatter_kernel(rows_hbm, dest_hbm, out_hbm, idx_v, rows_v, sem_in, sem_out):
        wid = lax.axis_index("s") * n_cores + lax.axis_index("c")
        first = wid * n_chunks
        for kk in range(TOP_K):
            pltpu.sync_copy(dest_hbm.at[kk, pl.ds(first, n_chunks)], idx_v.at[kk])

        def load(cc, b):
            return pltpu.make_async_copy(rows_hbm.at[pl.ds((first + cc) * ch, ch)], rows_v.at[b], sem_in.at[b])

        def scatters(cc, b):
            return [pltpu.make_async_copy(rows_v.at[b], out_hbm.at[idx_v.at[kk, cc]], sem_out.at[b])
                    for kk in range(TOP_K)]

        load(0, 0).start()
        load(1, 1).start()

        @pl.loop(0, n_chunks, step=2)
        def _(c):
            for b in range(2):
                load(c + b, b).wait()
                for cp in scatters(c + b, b):
                    cp.start()
            for b in range(2):
                for cp in scatters(c + b, b):
                    cp.wait()

                @pl.when(c + 2 + b < n_chunks)
                def _():
                    load(c + 2 + b, b).start()

    return scatter_kernel(rows, dest_km.reshape(TOP_K, t // ch, ch))


def _sc_gather_rows(table, dest_km):
    _, w = table.shape
    t = dest_km.shape[1]
    ch = SC_GATHER_CHUNK
    n_cores, n_workers = _sc_workers()
    n_chunks = t // n_workers // ch
    mesh = plsc.VectorSubcoreMesh(core_axis_name="c", subcore_axis_name="s")

    @functools.partial(
        pl.kernel, mesh=mesh, out_type=jax.ShapeDtypeStruct((TOP_K, t, w), table.dtype),
        scratch_types=[pltpu.VMEM((TOP_K, n_chunks, ch), I32), pltpu.VMEM((2, TOP_K, ch, w), table.dtype),
                       pltpu.SemaphoreType.DMA((2,)), pltpu.SemaphoreType.DMA((2,))],
        name="sc_combine_gather")
    def gather_kernel(table_hbm, dest_hbm, out_hbm, idx_v, rows_v, sem_in, sem_out):
        wid = lax.axis_index("s") * n_cores + lax.axis_index("c")
        first = wid * n_chunks
        for kk in range(TOP_K):
            pltpu.sync_copy(dest_hbm.at[kk, pl.ds(first, n_chunks)], idx_v.at[kk])

        def gathers(cc, b):
            return [pltpu.make_async_copy(table_hbm.at[idx_v.at[kk, cc]], rows_v.at[b, kk], sem_in.at[b])
                    for kk in range(TOP_K)]

        def stores(cc, b):
            return [pltpu.make_async_copy(rows_v.at[b, kk], out_hbm.at[kk, pl.ds((first + cc) * ch, ch)],
                                          sem_out.at[b]) for kk in range(TOP_K)]

        for b in range(2):
            for cp in gathers(b, b):
                cp.start()

        @pl.loop(0, n_chunks, step=2)
        def _(c):
            for b in range(2):
                for cp in gathers(c + b, b):
                    cp.wait()
                for cp in stores(c + b, b):
                    cp.start()
            for b in range(2):
                for cp in stores(c + b, b):
                    cp.wait()

                @pl.when(c + 2 + b < n_chunks)
                def _():
                    for cp in gathers(c + 2 + b, b):
                        cp.start()

    return gather_kernel(table, dest_km.reshape(TOP_K, t // ch, ch))


def _combine_kernel(x1_ref, yg_ref, route_ref, g_ref, o_ref, *, final_norm):
    tm = x1_ref.shape[0]
    moe_lo = jnp.zeros((tm, HALF), F32)
    moe_hi = jnp.zeros((tm, HALF), F32)
    route_t = jnp.concatenate([route_ref[...], jnp.zeros((LANES - ROUTE_ROWS, tm), F32)], axis=0).T
    for kk in range(TOP_K):
        gate = route_t[:, TOP_K + kk:TOP_K + kk + 1]
        y_lo, y_hi = _unpack_rows(yg_ref[kk])
        moe_lo = moe_lo + gate * y_lo
        moe_hi = moe_hi + gate * y_hi
    acc = x1_ref[...] + jnp.concatenate([moe_lo, moe_hi], axis=1)
    o_ref[...] = _rms(acc) * g_ref[...] if final_norm else acc


def _combine(x1, yg, route, g_final, final_norm):
    t = x1.shape[0]
    tm = TM_COMB
    return pl.pallas_call(
        functools.partial(_combine_kernel, final_norm=final_norm),
        grid=(t // tm,),
        in_specs=[pl.BlockSpec((tm, D_MODEL), lambda i: (i, 0)),
                  pl.BlockSpec((TOP_K, tm, HALF), lambda i: (0, i, 0)),
                  pl.BlockSpec((ROUTE_ROWS, tm), lambda i: (0, i)),
                  pl.BlockSpec((1, D_MODEL), lambda i: (0, 0))],
        out_specs=pl.BlockSpec((tm, D_MODEL), lambda i: (i, 0)),
        out_shape=jax.ShapeDtypeStruct((t, D_MODEL), F32),
        compiler_params=pltpu.CompilerParams(dimension_semantics=("arbitrary",)),
        name="combine",
    )(x1, yg, route, g_final)


def _prep_in_weights(w_in, w_uq, w_ukv):
    w_z = w_in[:, 0:512]
    w_xbc = w_in[:, 512:1536]
    w_dt = w_in[:, 1536:1544]
    w_cq = w_in[:, 1544:1800]
    w_ckv = w_in[:, 1800:1928]
    w_kr = w_in[:, 1928:1960]
    half = MLA_ROPE // 2
    zeros = lambda rows, width: jnp.zeros(rows + (width,), BF16)
    cat = lambda parts: jnp.concatenate([p.astype(BF16) for p in parts], axis=-1)
    d = (D_MODEL,)
    misc1 = [w_dt, zeros(d, MLA_NOPE - SSD_HEADS), w_kr, zeros(d, LANES - MLA_QK)]
    misc2 = [zeros(d, MLA_NOPE), w_kr[:, half:], w_kr[:, :half], zeros(d, LANES - MLA_QK)]
    w1 = cat([w_z, w_xbc, w_cq, w_ckv] + misc1 + misc2)

    wq3 = w_uq.reshape(MLA_Q_RANK, MLA_HEADS, MLA_QK)
    qh = (MLA_Q_RANK, MLA_HEADS)
    main = cat([wq3, zeros(qh, LANES - MLA_QK)])
    swap = cat([zeros(qh, MLA_NOPE), wq3[:, :, MLA_NOPE + half:], wq3[:, :, MLA_NOPE:MLA_NOPE + half],
                zeros(qh, LANES - MLA_QK)])
    wq = jnp.concatenate([main.reshape(MLA_Q_RANK, -1), swap.reshape(MLA_Q_RANK, -1)], axis=1)

    wkv3 = w_ukv.reshape(MLA_KV_RANK, MLA_HEADS, MLA_NOPE + MLA_V)
    kh = (MLA_KV_RANK, MLA_HEADS)
    kpart = cat([wkv3[:, :, :MLA_NOPE], zeros(kh, LANES - MLA_NOPE)])
    vpart = wkv3[:, :, MLA_NOPE:].astype(BF16)
    wkv = jnp.concatenate([kpart.reshape(MLA_KV_RANK, -1), vpart.reshape(MLA_KV_RANK, -1)], axis=1)
    return w1, wq, wkv


def _rope_inv_freq():
    inv_freq = ROPE_THETA ** (-jnp.arange(0, MLA_ROPE, 2, dtype=F32) / MLA_ROPE)
    return inv_freq[:, None]


def _pad_lanes(v, fill=0.0):
    return jnp.full((1, LANES), fill, F32).at[0, :v.shape[0]].set(v)


def kernel(x, positions, norm_mix_g, w_in, conv_w, conv_b, dt_bias, a_log, d_skip, ssd_norm_g, q_norm_g, w_uq, kv_norm_g, w_ukv, w_out, norm_ffn_g, w_router, b_router, w_gate_up, b_gate_up, w_down, b_down, norm_final_g):
    bsz, seqlen, d = x.shape
    t = bsz * seqlen
    depth = w_in.shape[0]
    x2 = x.reshape(t, d)
    pos_rows = positions.reshape(t // TM_PROJ, 1, TM_PROJ).astype(I32)
    invf = _rope_inv_freq()

    for l in range(depth):
        w1, wq, wkv = _prep_in_weights(w_in[l], w_uq[l], w_ukv[l])
        z, xbc, dtm, q, k, v = _inproj(
            x2, pos_rows, norm_mix_g[l][None, :], w1, q_norm_g[l][None, :], wq, kv_norm_g[l][None, :], wkv,
            invf, bsz, seqlen)
        y_ssd = _ssd(z, xbc, dtm, conv_w[l], conv_b[l][None, :], _pad_lanes(dt_bias[l]), _pad_lanes(a_log[l]),
                     jnp.repeat(d_skip[l], SSD_HEAD_DIM)[None, :], ssd_norm_g[l][None, :], bsz, seqlen)
        y_mla = _attention(q, k, v, bsz, seqlen).reshape(t, MLA_WIDTH)

        wr = jnp.zeros((d, LANES), F32).at[:, :N_EXPERTS].set(w_router[l])
        wr_hi = wr.astype(BF16)
        wr_lo = (wr - wr_hi.astype(F32)).astype(BF16)
        x1, h2p, route, cnt = _outproj(x2, y_ssd, y_mla, w_out[l], norm_ffn_g[l][None, :],
                                       jnp.concatenate([wr_hi, wr_lo], axis=1), _pad_lanes(b_router[l]))

        counts = cnt[:, 0].astype(I32)
        padded = ((counts + MOE_STEP - 1) // MOE_STEP) * MOE_STEP
        end_pad = jnp.cumsum(padded)
        start_pad = end_pad - padded
        n_slots = t * TOP_K + N_EXPERTS * MOE_STEP
        n_blocks = n_slots // MOE_STEP
        block_start = jnp.arange(n_blocks, dtype=I32) * MOE_STEP
        block_e = jnp.minimum(jnp.sum(block_start[:, None] >= end_pad[None, :], axis=1), N_EXPERTS - 1).astype(I32)
        eids = jnp.arange(N_EXPERTS, dtype=I32)
        block_hot = block_e[:, None] == eids[None, :]
        per_block = lambda table: jnp.sum(jnp.where(block_hot, table[None, :], 0), axis=1).astype(I32)
        block_valid = jnp.clip(per_block(counts) - (block_start - per_block(start_pad)), 0, MOE_STEP).astype(I32)
        used = counts > 0
        later_used = jnp.where((eids[None, :] > eids[:, None]) & used[None, :], eids[None, :], N_EXPERTS)
        next_used = jnp.min(later_used, axis=1)
        next_used = jnp.where(next_used < N_EXPERTS, next_used, -1).astype(I32)
        stage_slot = ((jnp.cumsum(used.astype(I32)) - 1) & 1).astype(I32)
        dest_km = _slot_plan(route, start_pad.astype(I32))

        xs = _sc_scatter_rows(h2p, dest_km, n_slots)
        ys = _expert_ffn(block_e, block_valid, per_block(next_used), per_block(stage_slot), xs,
                         w_gate_up[l], b_gate_up[l][:, None, :], w_down[l], b_down[l][:, None, :])
        yg = _sc_gather_rows(ys, dest_km)
        x2 = _combine(x1, yg, route, norm_final_g[None, :], l == depth - 1)
    return x2.reshape(bsz, seqlen, d)
```

```python
import functools

import jax
import jax.numpy as jnp
import numpy as np
from jax import lax
from jax.experimental import pallas as pl
from jax.experimental.pallas import tpu as pltpu
from jax.experimental.pallas import tpu_sc as plsc

F32 = jnp.float32
BF16 = jnp.bfloat16
I32 = jnp.int32

D_MODEL = 1024
EPS = 1e-6
LANES = 128
V7X_VMEM_BYTES = 64 * 1024 * 1024
VMEM_LIMIT_BYTES = V7X_VMEM_BYTES * 7 // 8

SSD_HEADS = 8
SSD_HEAD_DIM = 64
SSD_WIDTH = 512
SSD_STATE = 128
SSD_CONV = 4
SSD_CHUNK = 128
SSD_CONV_DIM = 1024
CONV_PAD = 8
SSD_CPS = 8

MLA_HEADS = 8
MLA_Q_RANK = 256
MLA_KV_RANK = 128
MLA_NOPE = 64
MLA_ROPE = 32
MLA_V = 64
MLA_QK = MLA_NOPE + MLA_ROPE
MLA_WIDTH = 512
ROPE_THETA = 10000.0
LOG2_E = 1.4426950408889634

N_EXPERTS = 32
TOP_K = 4
D_FF = 1024
SWIGLU_LIMIT = 7.0
SWIGLU_ALPHA = 1.702

IN_W = 512 + 1024 + 256 + 128 + 128 + 128

TM_PROJ = 1024
TQ = 512
TK = 512
ATT_HPS = 4
VT_ROWS = 80
MOE_BM = 256
MOE_STEP = 1024
TM_COMB = 512
TM_PLAN = 2048
ROUTE_ROWS = 16

NT_DIMS = (((1,), (1,)), ((), ()))


def _rms(x):
    return x * lax.rsqrt(jnp.mean(x * x, axis=-1, keepdims=True) + EPS)


HALF = D_MODEL // 2
HI_MASK = np.int32(-65536)


def _pack_rows(a):
    lo = lax.bitcast_convert_type(a[:, :HALF].astype(BF16).astype(F32), I32)
    hi = lax.bitcast_convert_type(a[:, HALF:].astype(BF16).astype(F32), I32)
    return (hi & HI_MASK) | lax.shift_right_logical(lo, 16)


def _unpack_rows(p):
    lo = lax.bitcast_convert_type(lax.shift_left(p, 16), F32)
    hi = lax.bitcast_convert_type(p & HI_MASK, F32)
    return lo, hi


def _inproj_kernel(x_ref, pos_ref, g_ref, w1_ref, qg_ref, wq_ref, kvg_ref, wkv_ref, invf_ref,
                   z_ref, xbc_ref, dtm_ref, q_ref, k_ref, vt_ref):
    x = x_ref[...]
    h = (_rms(x) * g_ref[...]).astype(BF16)
    p = jnp.dot(h, w1_ref[...], preferred_element_type=F32)
    z_ref[...] = p[:, 0:512]
    xbc_ref[...] = p[:, 512:1536]
    cq = p[:, 1536:1792]
    ckv = p[:, 1792:1920]
    m1 = p[:, 1920:2048]
    m2 = p[:, 2048:2176]
    dtm_ref[...] = m1

    lane = lax.broadcasted_iota(I32, (1, LANES), 1)
    tm = x.shape[0]
    ang = invf_ref[...] * pos_ref[0].astype(F32)
    cos_c = jnp.cos(ang)
    sin_c = jnp.sin(ang)
    z_lo = jnp.zeros((MLA_NOPE, tm), F32)
    z_hi = jnp.zeros((LANES - MLA_QK, tm), F32)
    cos_t = jnp.concatenate([z_lo, cos_c, cos_c, z_hi], axis=0).T
    sin_t = jnp.concatenate([z_lo, -sin_c, sin_c, z_hi], axis=0).T
    cosq_t = jnp.where(lane < MLA_NOPE, 1.0, cos_t)
    scale = MLA_QK ** -0.5 * LOG2_E

    cqn = (_rms(cq) * qg_ref[...]).astype(BF16)
    qq = jnp.dot(cqn, wq_ref[...], preferred_element_type=F32)
    ckvn = (_rms(ckv) * kvg_ref[...]).astype(BF16)
    kv = jnp.dot(ckvn, wkv_ref[...], preferred_element_type=F32)
    krot = m1 * cos_t + m2 * sin_t
    for h_i in range(MLA_HEADS):
        lo = h_i * LANES
        qm = qq[:, lo:lo + LANES]
        qs = qq[:, 1024 + lo:1024 + lo + LANES]
        q_ref[0, h_i] = ((qm * cosq_t + qs * sin_t) * scale).astype(BF16)
        k_ref[0, h_i] = (kv[:, lo:lo + LANES] + krot).astype(BF16)
    ones_rows = jnp.ones((VT_ROWS - MLA_V, tm), BF16)
    for pr in range(MLA_HEADS // 2):
        vpt = kv[:, 1024 + pr * LANES:1024 + (pr + 1) * LANES].T.astype(BF16)
        vt_ref[0, 2 * pr] = jnp.concatenate([vpt[0:MLA_V, :], ones_rows], axis=0)
        vt_ref[0, 2 * pr + 1] = jnp.concatenate([vpt[MLA_V:2 * MLA_V, :], ones_rows], axis=0)


def _inproj(x2, pos_rows, g_mix, w1, qg, wq, kvg, wkv, invf, bsz, seqlen):
    t = x2.shape[0]
    tm = TM_PROJ
    per_b = seqlen // tm
    full = lambda shape: pl.BlockSpec(shape, lambda i: (0,) * len(shape))
    head_spec = pl.BlockSpec((1, MLA_HEADS, tm, LANES), lambda i: (i // per_b, 0, i % per_b, 0))
    head_shape = jax.ShapeDtypeStruct((bsz, MLA_HEADS, seqlen, LANES), BF16)
    vt_spec = pl.BlockSpec((1, MLA_HEADS, VT_ROWS, tm), lambda i: (i // per_b, 0, 0, i % per_b))
    vt_shape = jax.ShapeDtypeStruct((bsz, MLA_HEADS, VT_ROWS, seqlen), BF16)
    return pl.pallas_call(
        _inproj_kernel,
        grid=(t // tm,),
        in_specs=[
            pl.BlockSpec((tm, D_MODEL), lambda i: (i, 0)),
            pl.BlockSpec((1, 1, tm), lambda i: (i, 0, 0)),
            full((1, D_MODEL)), full((D_MODEL, IN_W)),
            full((1, MLA_Q_RANK)), full((MLA_Q_RANK, 2048)),
            full((1, MLA_KV_RANK)), full((MLA_KV_RANK, 1536)),
            full((MLA_ROPE // 2, 1)),
        ],
        out_specs=[
            pl.BlockSpec((tm, 512), lambda i: (i, 0)),
            pl.BlockSpec((tm, 1024), lambda i: (i, 0)),
            pl.BlockSpec((tm, LANES), lambda i: (i, 0)),
            head_spec, head_spec, vt_spec,
        ],
        out_shape=[
            jax.ShapeDtypeStruct((t, 512), F32),
            jax.ShapeDtypeStruct((t, 1024), F32),
            jax.ShapeDtypeStruct((t, LANES), F32),
            head_shape, head_shape, vt_shape,
        ],
        compiler_params=pltpu.CompilerParams(
            dimension_semantics=("arbitrary",), vmem_limit_bytes=VMEM_LIMIT_BYTES),
        name="inproj",
    )(x2, pos_rows, g_mix, w1, qg, wq, kvg, wkv, invf)


def _ssd_kernel(z_ref, xbc_ref, dtm_ref, cw_ref, cb_ref, dtb_ref, alog_ref, dsk_ref, ng_ref,
                y_ref, ext_ref, st_ref):
    q = SSD_CHUNK
    rows = SSD_CPS * q

    @pl.when(pl.program_id(1) == 0)
    def _():
        ext_ref[0:CONV_PAD, :] = jnp.zeros((CONV_PAD, SSD_CONV_DIM), F32)
        st_ref[...] = jnp.zeros_like(st_ref)

    ext_ref[CONV_PAD:CONV_PAD + rows, :] = xbc_ref[...]

    lane = lax.broadcasted_iota(I32, (1, LANES), 1)
    row = lax.broadcasted_iota(I32, (q, q), 0)
    col = lax.broadcasted_iota(I32, (q, q), 1)
    tril = row >= col
    tril_b = jnp.where(tril, 1.0, 0.0).astype(BF16)
    spread = jnp.where(
        lax.broadcasted_iota(I32, (LANES, SSD_WIDTH), 0)
        == lax.broadcasted_iota(I32, (LANES, SSD_WIDTH), 1) // SSD_HEAD_DIM, 1.0, 0.0).astype(BF16)
    a_neg = -jnp.exp(alog_ref[...]) * LOG2_E

    def split3(v):
        hi = v.astype(BF16)
        r1 = v - hi.astype(F32)
        mid = r1.astype(BF16)
        return hi, mid, (r1 - mid.astype(F32)).astype(BF16)

    def dot3_right(parts, m):
        return sum(jnp.dot(p, m, preferred_element_type=F32) for p in parts)

    def expand(cols):
        return dot3_right(split3(cols), spread)

    for ci in range(SSD_CPS):
        lo = ci * q
        window = ext_ref[lo:lo + CONV_PAD + q, :]
        conv = cb_ref[...] + cw_ref[SSD_CONV - 1:SSD_CONV, :] * window[CONV_PAD:, :]
        for kk in range(SSD_CONV - 1):
            shifted = pltpu.roll(window, SSD_CONV - 1 - kk, axis=0)
            conv = conv + cw_ref[kk:kk + 1, :] * shifted[CONV_PAD:, :]
        u = conv * jax.nn.sigmoid(conv)
        xs = u[:, 0:512]
        bm = u[:, 512:768]
        cm = u[:, 768:1024]

        xdt = dtm_ref[lo:lo + q, :] + dtb_ref[...]
        dt = jnp.maximum(xdt, 0.0) + jnp.log1p(jnp.exp(-jnp.abs(xdt)))
        adt = jnp.where(lane < SSD_HEADS, dt * a_neg, 0.0)
        cum_col = sum(jnp.dot(tril_b, p, preferred_element_type=F32) for p in split3(adt))
        cum_row = cum_col.T

        dt_e = expand(dt)
        ac_e = expand(cum_col)
        last_e = ac_e[q - 1:q, :]
        xd = xs * dt_e
        w_end = xd * jnp.exp2(last_e - ac_e)
        eac = jnp.exp2(ac_e)
        cdec = jnp.exp2(last_e)

        y_parts = []
        for g in range(2):
            gl = g * 256
            bg = bm[:, g * SSD_STATE:(g + 1) * SSD_STATE]
            cg = cm[:, g * SSD_STATE:(g + 1) * SSD_STATE].astype(BF16)
            scores = lax.dot_general(cg, bg.astype(BF16), NT_DIMS, preferred_element_type=F32)
            bgt = bg.T.astype(BF16)
            sprev = st_ref[g]
            yoff = jnp.dot(cg, sprev.astype(BF16), preferred_element_type=F32)
            st_ref[g] = sprev * cdec[:, gl:gl + 256] + jnp.dot(
                bgt, w_end[:, gl:gl + 256].astype(BF16), preferred_element_type=F32)
            for pr in range(2):
                pl_lo = gl + pr * LANES
                xdp = xd[:, pl_lo:pl_lo + LANES].astype(BF16)
                res = []
                for jj in range(2):
                    h_i = g * 4 + pr * 2 + jj
                    seg = cum_col[:, h_i:h_i + 1] - cum_row[h_i:h_i + 1, :]
                    dec = jnp.exp2(jnp.where(tril, seg, -jnp.inf))
                    res.append(jnp.dot((scores * dec).astype(BF16), xdp, preferred_element_type=F32))
                ydiag = jnp.where(lane < SSD_HEAD_DIM, res[0], res[1])
                y_parts.append(ydiag + yoff[:, pr * LANES:(pr + 1) * LANES] * eac[:, pl_lo:pl_lo + LANES])
        y = jnp.concatenate(y_parts, axis=1) + dsk_ref[...] * xs
        zz = z_ref[lo:lo + q, :]
        y = y * (zz * jax.nn.sigmoid(zz))
        outs = []
        for g in range(2):
            yg = y[:, g * 256:(g + 1) * 256]
            outs.append(_rms(yg))
        y_ref[lo:lo + q, :] = (jnp.concatenate(outs, axis=1) * ng_ref[...]).astype(BF16)

    ext_ref[0:CONV_PAD, :] = ext_ref[rows:rows + CONV_PAD, :]


def _ssd(z, xbc, dtm, cw, cb, dtb, alog, dsk, ng, bsz, seqlen):
    t = z.shape[0]
    q = SSD_CHUNK
    rows = SSD_CPS * q
    nc = seqlen // rows
    full = lambda shape: pl.BlockSpec(shape, lambda b, c: (0,) * len(shape))
    row_spec = lambda width: pl.BlockSpec((rows, width), lambda b, c: (b * nc + c, 0))
    return pl.pallas_call(
        _ssd_kernel,
        grid=(bsz, nc),
        in_specs=[row_spec(512), row_spec(1024), row_spec(LANES),
                  full((SSD_CONV, SSD_CONV_DIM)), full((1, SSD_CONV_DIM)),
                  full((1, LANES)), full((1, LANES)), full((1, SSD_WIDTH)), full((1, SSD_WIDTH))],
        out_specs=row_spec(512),
        out_shape=jax.ShapeDtypeStruct((t, SSD_WIDTH), BF16),
        scratch_shapes=[pltpu.VMEM((rows + CONV_PAD, SSD_CONV_DIM), F32),
                        pltpu.VMEM((2, SSD_STATE, 256), F32)],
        compiler_params=pltpu.CompilerParams(dimension_semantics=("arbitrary", "arbitrary")),
        name="ssd",
    )(z, xbc, dtm, cw, cb, dtb, alog, dsk, ng)


def _attn_kernel(q_ref, k_ref, vt_ref, o_ref, acc_ref, s_ref, bmax_ref):
    i = pl.program_id(2)
    acc_ref[...] = jnp.zeros_like(acc_ref)

    def col_max(st):
        part = st[0:LANES, :]
        for r0 in range(LANES, TK, LANES):
            part = jnp.maximum(part, st[r0:r0 + LANES, :])
        return jnp.broadcast_to(jnp.max(part, axis=0, keepdims=True), (8, TQ))

    def scores(j, slot):
        start = pl.multiple_of(j * TK, TK)
        for hh in range(ATT_HPS):
            kb = k_ref[0, hh, pl.ds(start, TK), :]
            st = lax.dot_general(kb, q_ref[0, hh], NT_DIMS, preferred_element_type=F32)
            s_ref[slot, hh] = st
            bmax_ref[slot, hh] = col_max(st)

    def consume(j, slot, m_all, masked):
        start = pl.multiple_of(j * TK, TK)
        new_m = []
        for hh in range(ATT_HPS):
            vt = vt_ref[0, hh, :, pl.ds(start, TK)]
            st = s_ref[slot, hh]
            if masked:
                key = lax.broadcasted_iota(I32, (TK, TQ), 0)
                qry = lax.broadcasted_iota(I32, (TK, TQ), 1)
                st = jnp.where(key <= qry, st, -jnp.inf)
                block_max = col_max(st)
            else:
                block_max = bmax_ref[slot, hh]
            m_old = m_all[hh]
            m_new = jnp.maximum(m_old, block_max)
            alpha = jnp.exp2(m_old - m_new)
            p = jnp.exp2(st - m_new[0:1, :]).astype(BF16)
            acc_ref[hh] = acc_ref[hh] * alpha[0:1, :] + jnp.dot(vt, p, preferred_element_type=F32)
            new_m.append(m_new)
        return tuple(new_m)

    def pair(p, m_all):
        scores(2 * p + 1, 1)
        m_all = consume(2 * p, 0, m_all, False)
        scores(2 * p + 2, 0)
        return consume(2 * p + 1, 1, m_all, False)

    def odd_tail(_, m_all):
        scores(i, 1)
        return consume(i - 1, 0, m_all, False)

    def quad(g, m_all):
        return pair(2 * g + 1, pair(2 * g, m_all))

    m0 = jnp.full((8, TQ), -jnp.inf, F32)
    scores(0, 0)
    m_all = lax.fori_loop(0, i // 4, quad, (m0,) * ATT_HPS)
    m_all = lax.fori_loop(2 * (i // 4), i // 2, pair, m_all)
    m_all = lax.fori_loop(0, i & 1, odd_tail, m_all)

    @pl.when((i & 1) == 0)
    def _():
        consume(i, 0, m_all, True)

    @pl.when((i & 1) == 1)
    def _():
        consume(i, 1, m_all, True)

    outs = []
    for pr in range(ATT_HPS // 2):
        pair_t = []
        for hh in (2 * pr, 2 * pr + 1):
            a = acc_ref[hh]
            pair_t.append(a[0:MLA_V, :] / a[MLA_V:MLA_V + 1, :])
        outs.append(jnp.concatenate(pair_t, axis=0).T)
    o_ref[0] = jnp.concatenate(outs, axis=1).astype(BF16)


def _attention(q, k, vt, bsz, seqlen):
    nq = seqlen // TQ
    hps = ATT_HPS
    return pl.pallas_call(
        _attn_kernel,
        grid=(bsz, MLA_HEADS // hps, nq),
        in_specs=[pl.BlockSpec((1, hps, TQ, LANES), lambda b, p, i: (b, p, i, 0)),
                  pl.BlockSpec((1, hps, seqlen, LANES), lambda b, p, i: (b, p, 0, 0)),
                  pl.BlockSpec((1, hps, VT_ROWS, seqlen), lambda b, p, i: (b, p, 0, 0))],
        out_specs=pl.BlockSpec((1, TQ, hps * MLA_V), lambda b, p, i: (b, i, p)),
        out_shape=jax.ShapeDtypeStruct((bsz, seqlen, MLA_WIDTH), BF16),
        scratch_shapes=[pltpu.VMEM((hps, VT_ROWS, TQ), F32), pltpu.VMEM((2, hps, TK, TQ), F32),
                        pltpu.VMEM((2, hps, 8, TQ), F32)],
        compiler_params=pltpu.CompilerParams(
            dimension_semantics=("arbitrary", "arbitrary", "arbitrary"),
            vmem_limit_bytes=VMEM_LIMIT_BYTES),
        name="attention",
    )(q, k, vt)


def _outproj_kernel(x_ref, ys_ref, ym_ref, wo_ref, g_ref, wr_ref, br_ref,
                    x1_ref, h2_ref, route_ref, cnt_ref, wo_bf):
    tm = x_ref.shape[0]

    @pl.when(pl.program_id(0) == 0)
    def _():
        cnt_ref[...] = jnp.zeros_like(cnt_ref)
        wo_bf[...] = wo_ref[...].astype(BF16)

    mix = (jnp.dot(ys_ref[...], wo_bf[0:512, :], preferred_element_type=F32)
           + jnp.dot(ym_ref[...], wo_bf[512:1024, :], preferred_element_type=F32))
    x1 = x_ref[...] + mix
    x1_ref[...] = x1
    h2 = _rms(x1) * g_ref[...]
    h2_ref[...] = _pack_rows(h2)

    h_hi = h2.astype(BF16)
    h_lo = (h2 - h_hi.astype(F32)).astype(BF16)
    hh = jnp.dot(h_hi, wr_ref[...], preferred_element_type=F32)
    lh = jnp.dot(h_lo, wr_ref[:, 0:LANES], preferred_element_type=F32)
    logits = hh[:, 0:LANES] + (hh[:, LANES:2 * LANES] + lh) + br_ref[...]
    lt = logits.T[0:N_EXPERTS, :]
    eid = lax.broadcasted_iota(I32, (N_EXPERTS, 1), 0).astype(F32)

    vals, idxs, hots = [], [], []
    for _ in range(TOP_K):
        mx = jnp.max(lt, axis=0, keepdims=True)
        idx = jnp.min(jnp.where(lt == mx, eid, float(N_EXPERTS)), axis=0, keepdims=True)
        hot = eid == idx
        lt = jnp.where(hot, -jnp.inf, lt)
        vals.append(mx)
        idxs.append(idx)
        hots.append(hot)
    exps = [jnp.exp(v - vals[0]) for v in vals]
    denom = exps[0] + exps[1] + exps[2] + exps[3]

    multi_f = jnp.where(hots[0] | hots[1] | hots[2] | hots[3], 1.0, 0.0)
    r = lax.broadcasted_iota(I32, (tm, tm), 0)
    c = lax.broadcasted_iota(I32, (tm, tm), 1)
    earlier = jnp.where(r < c, 1.0, 0.0).astype(BF16)
    before = jnp.dot(multi_f.astype(BF16), earlier, preferred_element_type=F32) + cnt_ref[:, 0:1]
    cnt_ref[...] = cnt_ref[...] + jnp.sum(multi_f, axis=1, keepdims=True)

    ranks = [jnp.sum(jnp.where(hots[kk], before, 0.0), axis=0, keepdims=True) for kk in range(TOP_K)]
    gates = [e / denom for e in exps]
    route_ref[...] = jnp.concatenate(idxs + gates + ranks + [jnp.zeros((ROUTE_ROWS - 3 * TOP_K, tm), F32)], axis=0)


def _outproj(x2, y_ssd, y_mla, wo, g_ffn, wr, br):
    t = x2.shape[0]
    tm = TM_PROJ
    full = lambda shape: pl.BlockSpec(shape, lambda i: (0,) * len(shape))
    rows = lambda width: pl.BlockSpec((tm, width), lambda i: (i, 0))
    return pl.pallas_call(
        _outproj_kernel,
        grid=(t // tm,),
        in_specs=[rows(D_MODEL), rows(512), rows(512), full((1024, D_MODEL)), full((1, D_MODEL)),
                  full((D_MODEL, 2 * LANES)), full((1, LANES))],
        out_specs=[rows(D_MODEL), rows(HALF), pl.BlockSpec((ROUTE_ROWS, tm), lambda i: (0, i)),
                   full((N_EXPERTS, LANES))],
        out_shape=[jax.ShapeDtypeStruct((t, D_MODEL), F32),
                   jax.ShapeDtypeStruct((t, HALF), I32),
                   jax.ShapeDtypeStruct((ROUTE_ROWS, t), F32),
                   jax.ShapeDtypeStruct((N_EXPERTS, LANES), F32)],
        scratch_shapes=[pltpu.VMEM((1024, D_MODEL), BF16)],
        compiler_params=pltpu.CompilerParams(
            dimension_semantics=("arbitrary",), vmem_limit_bytes=VMEM_LIMIT_BYTES),
        name="outproj_router",
    )(x2, y_ssd, y_mla, wo, g_ffn, wr, br)


def _ffn_kernel(be_ref, bv_ref, nx_ref, sl_ref, xb_ref, xs_ref, wgu_hbm, bgu_ref, wd_hbm, bd_ref, ys_ref,
                wgu_st, wd_st, wgu_bf, wd_bf, sem):
    i = pl.program_id(0)
    e = be_ref[i]
    valid = bv_ref[i]
    slot = sl_ref[i]
    first = ((i == 0) | (e != be_ref[jnp.maximum(i - 1, 0)])) & (valid > 0)

    def weight_copies(expert, dst_slot):
        return (pltpu.make_async_copy(wgu_hbm.at[expert], wgu_st.at[dst_slot], sem.at[0, dst_slot]),
                pltpu.make_async_copy(wd_hbm.at[expert], wd_st.at[dst_slot], sem.at[1, dst_slot]))

    @pl.when(i == 0)
    def _():
        for cp in weight_copies(e, slot):
            cp.start()

    @pl.when(first)
    def _():
        for cp in weight_copies(e, slot):
            cp.wait()

        @pl.when(nx_ref[i] >= 0)
        def _():
            for cp in weight_copies(nx_ref[i], 1 - slot):
                cp.start(priority=1)

        wgu_bf[...] = wgu_st[slot].astype(BF16)
        wd_bf[...] = wd_st[slot].astype(BF16)

    def chain(r0, n_rows, masked):
        rows = pl.ds(r0, n_rows)
        xp = xs_ref[rows, :]
        if masked:
            xp = jnp.where(r0 + lax.broadcasted_iota(I32, (n_rows, 1), 0) < valid, xp, 0)
        x_lo, x_hi = _unpack_rows(xp)
        gu = (jnp.dot(x_lo.astype(BF16), wgu_bf[0:HALF, :], preferred_element_type=F32)
              + jnp.dot(x_hi.astype(BF16), wgu_bf[HALF:D_MODEL, :], preferred_element_type=F32)
              + bgu_ref[0])
        gate = jnp.minimum(gu[:, :D_FF], SWIGLU_LIMIT)
        up = jnp.clip(gu[:, D_FF:], -SWIGLU_LIMIT, SWIGLU_LIMIT)
        glu = gate * jax.nn.sigmoid(SWIGLU_ALPHA * gate)
        mid = ((up + 1.0) * glu).astype(BF16)
        ys_ref[rows, :] = _pack_rows(jnp.dot(mid, wd_bf[...], preferred_element_type=F32) + bd_ref[0])

    @pl.when(valid == MOE_STEP)
    def _():
        for r0 in range(0, MOE_STEP, MOE_BM):
            chain(r0, MOE_BM, False)

    @pl.when(valid < MOE_STEP)
    def _():
        half_bm = MOE_BM // 2
        for r0 in range(0, MOE_STEP, MOE_BM):
            pl.when(valid > r0 + half_bm)(functools.partial(chain, r0, MOE_BM, True))
            pl.when((valid > r0) & (valid <= r0 + half_bm))(functools.partial(chain, r0, half_bm, True))

            @pl.when(valid <= r0 + half_bm)
            def _():
                ys_ref[pl.ds(r0 + half_bm, half_bm), :] = jnp.zeros((half_bm, HALF), I32)

            @pl.when(valid <= r0)
            def _():
                ys_ref[pl.ds(r0, half_bm), :] = jnp.zeros((half_bm, HALF), I32)


def _expert_ffn(block_e, block_valid, block_next, block_slot, block_src, xs, wgu, bgu, wd, bd):
    n_slots = xs.shape[0]
    bm = MOE_STEP
    grid_spec = pltpu.PrefetchScalarGridSpec(
        num_scalar_prefetch=5,
        grid=(n_slots // bm,),
        in_specs=[
            pl.BlockSpec((bm, HALF), lambda i, be, bv, nx, sl, xb: (xb[i], 0)),
            pl.BlockSpec(memory_space=pl.ANY),
            pl.BlockSpec((1, 1, 2 * D_FF), lambda i, be, bv, nx, sl, xb: (be[i], 0, 0)),
            pl.BlockSpec(memory_space=pl.ANY),
            pl.BlockSpec((1, 1, D_MODEL), lambda i, be, bv, nx, sl, xb: (be[i], 0, 0)),
        ],
        out_specs=pl.BlockSpec((bm, HALF), lambda i, be, bv, nx, sl, xb: (i, 0)),
        scratch_shapes=[pltpu.VMEM((2, D_MODEL, 2 * D_FF), F32), pltpu.VMEM((2, D_FF, D_MODEL), F32),
                        pltpu.VMEM((D_MODEL, 2 * D_FF), BF16), pltpu.VMEM((D_FF, D_MODEL), BF16),
                        pltpu.SemaphoreType.DMA((2, 2))],
    )
    return pl.pallas_call(
        _ffn_kernel,
        grid_spec=grid_spec,
        out_shape=jax.ShapeDtypeStruct((n_slots, HALF), I32),
        compiler_params=pltpu.CompilerParams(
            dimension_semantics=("arbitrary",), vmem_limit_bytes=VMEM_LIMIT_BYTES),
        name="expert_ffn",
    )(block_e, block_valid, block_next, block_slot, block_src, xs, wgu, bgu, wd, bd)


def _plan_kernel(sp_ref, route_ref, dest_ref):
    idx = route_ref[0:TOP_K, :]
    rank = route_ref[2 * TOP_K:3 * TOP_K, :]
    start = jnp.zeros(idx.shape, F32)
    for e_i in range(N_EXPERTS):
        start = jnp.where(idx == float(e_i), sp_ref[e_i].astype(F32), start)
    dest_ref[...] = (start + rank).astype(I32)


def _slot_plan(route, start_pad):
    t = route.shape[1]
    tm = TM_PLAN
    grid_spec = pltpu.PrefetchScalarGridSpec(
        num_scalar_prefetch=1,
        grid=(t // tm,),
        in_specs=[pl.BlockSpec((ROUTE_ROWS, tm), lambda i, sp: (0, i))],
        out_specs=pl.BlockSpec((TOP_K, tm), lambda i, sp: (0, i)),
    )
    return pl.pallas_call(
        _plan_kernel,
        grid_spec=grid_spec,
        out_shape=jax.ShapeDtypeStruct((TOP_K, t), I32),
        compiler_params=pltpu.CompilerParams(dimension_semantics=("arbitrary",)),
        name="slot_plan",
    )(start_pad, route)


SC_SCATTER_CHUNK = 32
SC_GATHER_CHUNK = 16


def _sc_workers():
    info = plsc.get_sparse_core_info()
    return info.num_cores, info.num_cores * info.num_subcores


def _sc_scatter_rows(rows, dest_km, n_out):
    t, w = rows.shape
    ch = SC_SCATTER_CHUNK
    n_cores, n_workers = _sc_workers()
    n_chunks = t // n_workers // ch
    mesh = plsc.VectorSubcoreMesh(core_axis_name="c", subcore_axis_name="s")

    @functools.partial(
        pl.kernel, mesh=mesh, out_type=jax.ShapeDtypeStruct((n_out, w), rows.dtype),
        scratch_types=[pltpu.VMEM((TOP_K, n_chunks, ch), I32), pltpu.VMEM((2, ch, w), rows.dtype),
                       pltpu.SemaphoreType.DMA((2,)), pltpu.SemaphoreType.DMA((2,))],
        name="sc_dispatch_scatter")
    def scatter_kernel(rows_hbm, dest_hbm, out_hbm, idx_v, rows_v, sem_in, sem_out):
        wid = lax.axis_index("s") * n_cores + lax.axis_index("c")
        first = wid * n_chunks
        for kk in range(TOP_K):
            pltpu.sync_copy(dest_hbm.at[kk, pl.ds(first, n_chunks)], idx_v.at[kk])

        def load(cc, b):
            return pltpu.make_async_copy(rows_hbm.at[pl.ds((first + cc) * ch, ch)], rows_v.at[b], sem_in.at[b])

        def scatters(cc, b):
            return [pltpu.make_async_copy(rows_v.at[b], out_hbm.at[idx_v.at[kk, cc]], sem_out.at[b])
                    for kk in range(TOP_K)]

        load(0, 0).start()
        load(1, 1).start()

        @pl.loop(0, n_chunks, step=2)
        def _(c):
            for b in range(2):
                load(c + b, b).wait()
                for cp in scatters(c + b, b):
                    cp.start()
            for b in range(2):
                for cp in scatters(c + b, b):
                    cp.wait()

                @pl.when(c + 2 + b < n_chunks)
                def _():
                    load(c + 2 + b, b).start()

    return scatter_kernel(rows, dest_km.reshape(TOP_K, t // ch, ch))


def _sc_gather_rows(table, dest_km):
    _, w = table.shape
    t = dest_km.shape[1]
    ch = SC_GATHER_CHUNK
    n_cores, n_workers = _sc_workers()
    n_chunks = t // n_workers // ch
    mesh = plsc.VectorSubcoreMesh(core_axis_name="c", subcore_axis_name="s")

    @functools.partial(
        pl.kernel, mesh=mesh, out_type=jax.ShapeDtypeStruct((TOP_K, t, w), table.dtype),
        scratch_types=[pltpu.VMEM((TOP_K, n_chunks, ch), I32), pltpu.VMEM((2, TOP_K, ch, w), table.dtype),
                       pltpu.SemaphoreType.DMA((2,)), pltpu.SemaphoreType.DMA((2,))],
        name="sc_combine_gather")
    def gather_kernel(table_hbm, dest_hbm, out_hbm, idx_v, rows_v, sem_in, sem_out):
        wid = lax.axis_index("s") * n_cores + lax.axis_index("c")
        first = wid * n_chunks
        for kk in range(TOP_K):
            pltpu.sync_copy(dest_hbm.at[kk, pl.ds(first, n_chunks)], idx_v.at[kk])

        def gathers(cc, b):
            return [pltpu.make_async_copy(table_hbm.at[idx_v.at[kk, cc]], rows_v.at[b, kk], sem_in.at[b])
                    for kk in range(TOP_K)]

        def stores(cc, b):
            return [pltpu.make_async_copy(rows_v.at[b, kk], out_hbm.at[kk, pl.ds((first + cc) * ch, ch)],
                                          sem_out.at[b]) for kk in range(TOP_K)]

        for b in range(2):
            for cp in gathers(b, b):
                cp.start()

        @pl.loop(0, n_chunks, step=2)
        def _(c):
            for b in range(2):
                for cp in gathers(c + b, b):
                    cp.wait()
                for cp in stores(c + b, b):
                    cp.start()
            for b in range(2):
                for cp in stores(c + b, b):
                    cp.wait()

                @pl.when(c + 2 + b < n_chunks)
                def _():
                    for cp in gathers(c + 2 + b, b):
                        cp.start()

    return gather_kernel(table, dest_km.reshape(TOP_K, t // ch, ch))


def _combine_kernel(x1_ref, yg_ref, route_ref, g_ref, o_ref, *, final_norm):
    tm = x1_ref.shape[0]
    moe_lo = jnp.zeros((tm, HALF), F32)
    moe_hi = jnp.zeros((tm, HALF), F32)
    route_t = jnp.concatenate([route_ref[...], jnp.zeros((LANES - ROUTE_ROWS, tm), F32)], axis=0).T
    for kk in range(TOP_K):
        gate = route_t[:, TOP_K + kk:TOP_K + kk + 1]
        y_lo, y_hi = _unpack_rows(yg_ref[kk])
        moe_lo = moe_lo + gate * y_lo
        moe_hi = moe_hi + gate * y_hi
    acc = x1_ref[...] + jnp.concatenate([moe_lo, moe_hi], axis=1)
    o_ref[...] = _rms(acc) * g_ref[...] if final_norm else acc


def _combine(x1, yg, route, g_final, final_norm):
    t = x1.shape[0]
    tm = TM_COMB
    return pl.pallas_call(
        functools.partial(_combine_kernel, final_norm=final_norm),
        grid=(t // tm,),
        in_specs=[pl.BlockSpec((tm, D_MODEL), lambda i: (i, 0)),
                  pl.BlockSpec((TOP_K, tm, HALF), lambda i: (0, i, 0)),
                  pl.BlockSpec((ROUTE_ROWS, tm), lambda i: (0, i)),
                  pl.BlockSpec((1, D_MODEL), lambda i: (0, 0))],
        out_specs=pl.BlockSpec((tm, D_MODEL), lambda i: (i, 0)),
        out_shape=jax.ShapeDtypeStruct((t, D_MODEL), F32),
        compiler_params=pltpu.CompilerParams(dimension_semantics=("arbitrary",)),
        name="combine",
    )(x1, yg, route, g_final)


def _prep_in_weights(w_in, w_uq, w_ukv):
    w_z = w_in[:, 0:512]
    w_xbc = w_in[:, 512:1536]
    w_dt = w_in[:, 1536:1544]
    w_cq = w_in[:, 1544:1800]
    w_ckv = w_in[:, 1800:1928]
    w_kr = w_in[:, 1928:1960]
    half = MLA_ROPE // 2
    zeros = lambda rows, width: jnp.zeros(rows + (width,), BF16)
    cat = lambda parts: jnp.concatenate([p.astype(BF16) for p in parts], axis=-1)
    d = (D_MODEL,)
    misc1 = [w_dt, zeros(d, MLA_NOPE - SSD_HEADS), w_kr, zeros(d, LANES - MLA_QK)]
    misc2 = [zeros(d, MLA_NOPE), w_kr[:, half:], w_kr[:, :half], zeros(d, LANES - MLA_QK)]
    w1 = cat([w_z, w_xbc, w_cq, w_ckv] + misc1 + misc2)

    wq3 = w_uq.reshape(MLA_Q_RANK, MLA_HEADS, MLA_QK)
    qh = (MLA_Q_RANK, MLA_HEADS)
    main = cat([wq3, zeros(qh, LANES - MLA_QK)])
    swap = cat([zeros(qh, MLA_NOPE), wq3[:, :, MLA_NOPE + half:], wq3[:, :, MLA_NOPE:MLA_NOPE + half],
                zeros(qh, LANES - MLA_QK)])
    wq = jnp.concatenate([main.reshape(MLA_Q_RANK, -1), swap.reshape(MLA_Q_RANK, -1)], axis=1)

    wkv3 = w_ukv.reshape(MLA_KV_RANK, MLA_HEADS, MLA_NOPE + MLA_V)
    kh = (MLA_KV_RANK, MLA_HEADS)
    kpart = cat([wkv3[:, :, :MLA_NOPE], zeros(kh, LANES - MLA_NOPE)])
    vpart = wkv3[:, :, MLA_NOPE:].astype(BF16)
    wkv = jnp.concatenate([kpart.reshape(MLA_KV_RANK, -1), vpart.reshape(MLA_KV_RANK, -1)], axis=1)
    return w1, wq, wkv


def _rope_inv_freq():
    inv_freq = ROPE_THETA ** (-jnp.arange(0, MLA_ROPE, 2, dtype=F32) / MLA_ROPE)
    return inv_freq[:, None]


def _pad_lanes(v, fill=0.0):
    return jnp.full((1, LANES), fill, F32).at[0, :v.shape[0]].set(v)


def kernel(x, positions, norm_mix_g, w_in, conv_w, conv_b, dt_bias, a_log, d_skip, ssd_norm_g, q_norm_g, w_uq, kv_norm_g, w_ukv, w_out, norm_ffn_g, w_router, b_router, w_gate_up, b_gate_up, w_down, b_down, norm_final_g):
    bsz, seqlen, d = x.shape
    t = bsz * seqlen
    depth = w_in.shape[0]
    x2 = x.reshape(t, d)
    pos_rows = positions.reshape(t // TM_PROJ, 1, TM_PROJ).astype(I32)
    invf = _rope_inv_freq()

    for l in range(depth):
        w1, wq, wkv = _prep_in_weights(w_in[l], w_uq[l], w_ukv[l])
        z, xbc, dtm, q, k, v = _inproj(
            x2, pos_rows, norm_mix_g[l][None, :], w1, q_norm_g[l][None, :], wq, kv_norm_g[l][None, :], wkv,
            invf, bsz, seqlen)
        y_ssd = _ssd(z, xbc, dtm, conv_w[l], conv_b[l][None, :], _pad_lanes(dt_bias[l]), _pad_lanes(a_log[l]),
                     jnp.repeat(d_skip[l], SSD_HEAD_DIM)[None, :], ssd_norm_g[l][None, :], bsz, seqlen)
        y_mla = _attention(q, k, v, bsz, seqlen).reshape(t, MLA_WIDTH)

        wr = jnp.zeros((d, LANES), F32).at[:, :N_EXPERTS].set(w_router[l])
        wr_hi = wr.astype(BF16)
        wr_lo = (wr - wr_hi.astype(F32)).astype(BF16)
        x1, h2p, route, cnt = _outproj(x2, y_ssd, y_mla, w_out[l], norm_ffn_g[l][None, :],
                                       jnp.concatenate([wr_hi, wr_lo], axis=1), _pad_lanes(b_router[l]))

        counts = cnt[:, 0].astype(I32)
        padded = ((counts + MOE_STEP - 1) // MOE_STEP) * MOE_STEP
        end_pad = jnp.cumsum(padded)
        start_pad = end_pad - padded
        n_slots = t * TOP_K + N_EXPERTS * MOE_STEP
        n_blocks = n_slots // MOE_STEP
        block_start = jnp.arange(n_blocks, dtype=I32) * MOE_STEP
        block_e = jnp.minimum(jnp.sum(block_start[:, None] >= end_pad[None, :], axis=1), N_EXPERTS - 1).astype(I32)
        eids = jnp.arange(N_EXPERTS, dtype=I32)
        block_hot = block_e[:, None] == eids[None, :]
        per_block = lambda table: jnp.sum(jnp.where(block_hot, table[None, :], 0), axis=1).astype(I32)
        block_valid = jnp.clip(per_block(counts) - (block_start - per_block(start_pad)), 0, MOE_STEP).astype(I32)
        used = counts > 0
        later_used = jnp.where((eids[None, :] > eids[:, None]) & used[None, :], eids[None, :], N_EXPERTS)
        next_used = jnp.min(later_used, axis=1)
        next_used = jnp.where(next_used < N_EXPERTS, next_used, -1).astype(I32)
        stage_slot = ((jnp.cumsum(used.astype(I32)) - 1) & 1).astype(I32)
        dest_km = _slot_plan(route, start_pad.astype(I32))

        xs = _sc_scatter_rows(h2p, dest_km, n_slots)
        n_used = end_pad[-1] // MOE_STEP
        block_src = jnp.minimum(jnp.arange(n_blocks, dtype=I32), n_used - 1).astype(I32)
        ys = _expert_ffn(block_e, block_valid, per_block(next_used), per_block(stage_slot), block_src, xs,
                         w_gate_up[l], b_gate_up[l][:, None, :], w_down[l], b_down[l][:, None, :])
        yg = _sc_gather_rows(ys, dest_km)
        x2 = _combine(x1, yg, route, norm_final_g[None, :], l == depth - 1)
    return x2.reshape(bsz, seqlen, d)
```

```python
import functools

import jax
import jax.numpy as jnp
import numpy as np
from jax import lax
from jax.experimental import pallas as pl
from jax.experimental.pallas import tpu as pltpu
from jax.experimental.pallas import tpu_sc as plsc

F32 = jnp.float32
BF16 = jnp.bfloat16
I32 = jnp.int32

D_MODEL = 1024
EPS = 1e-6
LANES = 128
V7X_VMEM_BYTES = 64 * 1024 * 1024
VMEM_LIMIT_BYTES = V7X_VMEM_BYTES * 7 // 8

SSD_HEADS = 8
SSD_HEAD_DIM = 64
SSD_WIDTH = 512
SSD_STATE = 128
SSD_CONV = 4
SSD_CHUNK = 128
SSD_CONV_DIM = 1024
CONV_PAD = 8
SSD_CPS = 8

MLA_HEADS = 8
MLA_Q_RANK = 256
MLA_KV_RANK = 128
MLA_NOPE = 64
MLA_ROPE = 32
MLA_V = 64
MLA_QK = MLA_NOPE + MLA_ROPE
MLA_WIDTH = 512
ROPE_THETA = 10000.0
LOG2_E = 1.4426950408889634

N_EXPERTS = 32
TOP_K = 4
D_FF = 1024
SWIGLU_LIMIT = 7.0
SWIGLU_ALPHA = 1.702

IN_W = 512 + 1024 + 256 + 128 + 128 + 128

TM_PROJ = 1024
TQ = 512
TK = 512
ATT_HPS = 4
VT_ROWS = 80
MOE_BM = 256
MOE_STEP = 1024
TM_COMB = 512
TM_PLAN = 2048
ROUTE_ROWS = 16

NT_DIMS = (((1,), (1,)), ((), ()))


def _rms(x):
    return x * lax.rsqrt(jnp.mean(x * x, axis=-1, keepdims=True) + EPS)


HALF = D_MODEL // 2
HI_MASK = np.int32(-65536)


def _pack_rows(a):
    lo = lax.bitcast_convert_type(a[:, :HALF].astype(BF16).astype(F32), I32)
    hi = lax.bitcast_convert_type(a[:, HALF:].astype(BF16).astype(F32), I32)
    return (hi & HI_MASK) | lax.shift_right_logical(lo, 16)


def _unpack_rows(p):
    lo = lax.bitcast_convert_type(lax.shift_left(p, 16), F32)
    hi = lax.bitcast_convert_type(p & HI_MASK, F32)
    return lo, hi


def _inproj_kernel(x_ref, pos_ref, g_ref, w1_ref, qg_ref, wq_ref, kvg_ref, wkv_ref, invf_ref,
                   z_ref, xbc_ref, dtm_ref, q_ref, k_ref, vt_ref):
    x = x_ref[...]
    h = (_rms(x) * g_ref[...]).astype(BF16)
    p = jnp.dot(h, w1_ref[...], preferred_element_type=F32)
    z_ref[...] = p[:, 0:512]
    xbc_ref[...] = p[:, 512:1536]
    cq = p[:, 1536:1792]
    ckv = p[:, 1792:1920]
    m1 = p[:, 1920:2048]
    m2 = p[:, 2048:2176]
    dtm_ref[...] = m1

    lane = lax.broadcasted_iota(I32, (1, LANES), 1)
    tm = x.shape[0]
    ang = invf_ref[...] * pos_ref[0].astype(F32)
    cos_c = jnp.cos(ang)
    sin_c = jnp.sin(ang)
    z_lo = jnp.zeros((MLA_NOPE, tm), F32)
    z_hi = jnp.zeros((LANES - MLA_QK, tm), F32)
    cos_t = jnp.concatenate([z_lo, cos_c, cos_c, z_hi], axis=0).T
    sin_t = jnp.concatenate([z_lo, -sin_c, sin_c, z_hi], axis=0).T
    cosq_t = jnp.where(lane < MLA_NOPE, 1.0, cos_t)
    scale = MLA_QK ** -0.5 * LOG2_E

    cqn = (_rms(cq) * qg_ref[...]).astype(BF16)
    qq = jnp.dot(cqn, wq_ref[...], preferred_element_type=F32)
    ckvn = (_rms(ckv) * kvg_ref[...]).astype(BF16)
    kv = jnp.dot(ckvn, wkv_ref[...], preferred_element_type=F32)
    krot = m1 * cos_t + m2 * sin_t
    for h_i in range(MLA_HEADS):
        lo = h_i * LANES
        qm = qq[:, lo:lo + LANES]
        qs = qq[:, 1024 + lo:1024 + lo + LANES]
        q_ref[0, h_i] = ((qm * cosq_t + qs * sin_t) * scale).astype(BF16)
        k_ref[0, h_i] = (kv[:, lo:lo + LANES] + krot).astype(BF16)
    ones_rows = jnp.ones((VT_ROWS - MLA_V, tm), BF16)
    for pr in range(MLA_HEADS // 2):
        vpt = kv[:, 1024 + pr * LANES:1024 + (pr + 1) * LANES].T.astype(BF16)
        vt_ref[0, 2 * pr] = jnp.concatenate([vpt[0:MLA_V, :], ones_rows], axis=0)
        vt_ref[0, 2 * pr + 1] = jnp.concatenate([vpt[MLA_V:2 * MLA_V, :], ones_rows], axis=0)


def _inproj(x2, pos_rows, g_mix, w1, qg, wq, kvg, wkv, invf, bsz, seqlen):
    t = x2.shape[0]
    tm = TM_PROJ
    per_b = seqlen // tm
    full = lambda shape: pl.BlockSpec(shape, lambda i: (0,) * len(shape))
    head_spec = pl.BlockSpec((1, MLA_HEADS, tm, LANES), lambda i: (i // per_b, 0, i % per_b, 0))
    head_shape = jax.ShapeDtypeStruct((bsz, MLA_HEADS, seqlen, LANES), BF16)
    vt_spec = pl.BlockSpec((1, MLA_HEADS, VT_ROWS, tm), lambda i: (i // per_b, 0, 0, i % per_b))
    vt_shape = jax.ShapeDtypeStruct((bsz, MLA_HEADS, VT_ROWS, seqlen), BF16)
    return pl.pallas_call(
        _inproj_kernel,
        grid=(t // tm,),
        in_specs=[
            pl.BlockSpec((tm, D_MODEL), lambda i: (i, 0)),
            pl.BlockSpec((1, 1, tm), lambda i: (i, 0, 0)),
            full((1, D_MODEL)), full((D_MODEL, IN_W)),
            full((1, MLA_Q_RANK)), full((MLA_Q_RANK, 2048)),
            full((1, MLA_KV_RANK)), full((MLA_KV_RANK, 1536)),
            full((MLA_ROPE // 2, 1)),
        ],
        out_specs=[
            pl.BlockSpec((tm, 512), lambda i: (i, 0)),
            pl.BlockSpec((tm, 1024), lambda i: (i, 0)),
            pl.BlockSpec((tm, LANES), lambda i: (i, 0)),
            head_spec, head_spec, vt_spec,
        ],
        out_shape=[
            jax.ShapeDtypeStruct((t, 512), F32),
            jax.ShapeDtypeStruct((t, 1024), F32),
            jax.ShapeDtypeStruct((t, LANES), F32),
            head_shape, head_shape, vt_shape,
        ],
        compiler_params=pltpu.CompilerParams(
            dimension_semantics=("arbitrary",), vmem_limit_bytes=VMEM_LIMIT_BYTES),
        name="inproj",
    )(x2, pos_rows, g_mix, w1, qg, wq, kvg, wkv, invf)


def _ssd_kernel(z_ref, xbc_ref, dtm_ref, cw_ref, cb_ref, dtb_ref, alog_ref, dsk_ref, ng_ref,
                y_ref, ext_ref, st_ref):
    q = SSD_CHUNK
    rows = SSD_CPS * q

    @pl.when(pl.program_id(1) == 0)
    def _():
        ext_ref[0:CONV_PAD, :] = jnp.zeros((CONV_PAD, SSD_CONV_DIM), F32)
        st_ref[...] = jnp.zeros_like(st_ref)

    ext_ref[CONV_PAD:CONV_PAD + rows, :] = xbc_ref[...]

    lane = lax.broadcasted_iota(I32, (1, LANES), 1)
    row = lax.broadcasted_iota(I32, (q, q), 0)
    col = lax.broadcasted_iota(I32, (q, q), 1)
    tril = row >= col
    tril_b = jnp.where(tril, 1.0, 0.0).astype(BF16)
    spread = jnp.where(
        lax.broadcasted_iota(I32, (LANES, SSD_WIDTH), 0)
        == lax.broadcasted_iota(I32, (LANES, SSD_WIDTH), 1) // SSD_HEAD_DIM, 1.0, 0.0).astype(BF16)
    a_neg = -jnp.exp(alog_ref[...]) * LOG2_E

    def split3(v):
        hi = v.astype(BF16)
        r1 = v - hi.astype(F32)
        mid = r1.astype(BF16)
        return hi, mid, (r1 - mid.astype(F32)).astype(BF16)

    def dot3_right(parts, m):
        return sum(jnp.dot(p, m, preferred_element_type=F32) for p in parts)

    def expand(cols):
        return dot3_right(split3(cols), spread)

    for ci in range(SSD_CPS):
        lo = ci * q
        window = ext_ref[lo:lo + CONV_PAD + q, :]
        conv = cb_ref[...] + cw_ref[SSD_CONV - 1:SSD_CONV, :] * window[CONV_PAD:, :]
        for kk in range(SSD_CONV - 1):
            shifted = pltpu.roll(window, SSD_CONV - 1 - kk, axis=0)
            conv = conv + cw_ref[kk:kk + 1, :] * shifted[CONV_PAD:, :]
        u = conv * jax.nn.sigmoid(conv)
        xs = u[:, 0:512]
        bm = u[:, 512:768]
        cm = u[:, 768:1024]

        xdt = dtm_ref[lo:lo + q, :] + dtb_ref[...]
        dt = jnp.maximum(xdt, 0.0) + jnp.log1p(jnp.exp(-jnp.abs(xdt)))
        adt = jnp.where(lane < SSD_HEADS, dt * a_neg, 0.0)
        cum_col = sum(jnp.dot(tril_b, p, preferred_element_type=F32) for p in split3(adt))
        cum_row = cum_col.T

        dt_e = expand(dt)
        ac_e = expand(cum_col)
        last_e = ac_e[q - 1:q, :]
        xd = xs * dt_e
        w_end = xd * jnp.exp2(last_e - ac_e)
        eac = jnp.exp2(ac_e)
        cdec = jnp.exp2(last_e)

        y_parts = []
        for g in range(2):
            gl = g * 256
            bg = bm[:, g * SSD_STATE:(g + 1) * SSD_STATE]
            cg = cm[:, g * SSD_STATE:(g + 1) * SSD_STATE].astype(BF16)
            scores = lax.dot_general(cg, bg.astype(BF16), NT_DIMS, preferred_element_type=F32)
            bgt = bg.T.astype(BF16)
            sprev = st_ref[g]
            yoff = jnp.dot(cg, sprev.astype(BF16), preferred_element_type=F32)
            st_ref[g] = sprev * cdec[:, gl:gl + 256] + jnp.dot(
                bgt, w_end[:, gl:gl + 256].astype(BF16), preferred_element_type=F32)
            for pr in range(2):
                pl_lo = gl + pr * LANES
                xdp = xd[:, pl_lo:pl_lo + LANES].astype(BF16)
                res = []
                for jj in range(2):
                    h_i = g * 4 + pr * 2 + jj
                    seg = cum_col[:, h_i:h_i + 1] - cum_row[h_i:h_i + 1, :]
                    dec = jnp.exp2(jnp.where(tril, seg, -jnp.inf))
                    res.append(jnp.dot((scores * dec).astype(BF16), xdp, preferred_element_type=F32))
                ydiag = jnp.where(lane < SSD_HEAD_DIM, res[0], res[1])
                y_parts.append(ydiag + yoff[:, pr * LANES:(pr + 1) * LANES] * eac[:, pl_lo:pl_lo + LANES])
        y = jnp.concatenate(y_parts, axis=1) + dsk_ref[...] * xs
        zz = z_ref[lo:lo + q, :]
        y = y * (zz * jax.nn.sigmoid(zz))
        outs = []
        for g in range(2):
            yg = y[:, g * 256:(g + 1) * 256]
            outs.append(_rms(yg))
        y_ref[lo:lo + q, :] = (jnp.concatenate(outs, axis=1) * ng_ref[...]).astype(BF16)

    ext_ref[0:CONV_PAD, :] = ext_ref[rows:rows + CONV_PAD, :]


def _ssd(z, xbc, dtm, cw, cb, dtb, alog, dsk, ng, bsz, seqlen):
    t = z.shape[0]
    q = SSD_CHUNK
    rows = SSD_CPS * q
    nc = seqlen // rows
    full = lambda shape: pl.BlockSpec(shape, lambda b, c: (0,) * len(shape))
    row_spec = lambda width: pl.BlockSpec((rows, width), lambda b, c: (b * nc + c, 0))
    return pl.pallas_call(
        _ssd_kernel,
        grid=(bsz, nc),
        in_specs=[row_spec(512), row_spec(1024), row_spec(LANES),
                  full((SSD_CONV, SSD_CONV_DIM)), full((1, SSD_CONV_DIM)),
                  full((1, LANES)), full((1, LANES)), full((1, SSD_WIDTH)), full((1, SSD_WIDTH))],
        out_specs=row_spec(512),
        out_shape=jax.ShapeDtypeStruct((t, SSD_WIDTH), BF16),
        scratch_shapes=[pltpu.VMEM((rows + CONV_PAD, SSD_CONV_DIM), F32),
                        pltpu.VMEM((2, SSD_STATE, 256), F32)],
        compiler_params=pltpu.CompilerParams(dimension_semantics=("arbitrary", "arbitrary")),
        name="ssd",
    )(z, xbc, dtm, cw, cb, dtb, alog, dsk, ng)


def _attn_kernel(q_ref, k_ref, vt_ref, o_ref, acc_ref, s_ref, bmax_ref):
    i = pl.program_id(2)
    acc_ref[...] = jnp.zeros_like(acc_ref)

    def col_max(st):
        part = st[0:LANES, :]
        for r0 in range(LANES, TK, LANES):
            part = jnp.maximum(part, st[r0:r0 + LANES, :])
        return jnp.broadcast_to(jnp.max(part, axis=0, keepdims=True), (8, TQ))

    def scores(j, slot):
        start = pl.multiple_of(j * TK, TK)
        for hh in range(ATT_HPS):
            kb = k_ref[0, hh, pl.ds(start, TK), :]
            st = lax.dot_general(kb, q_ref[0, hh], NT_DIMS, preferred_element_type=F32)
            s_ref[slot, hh] = st
            bmax_ref[slot, hh] = col_max(st)

    def consume(j, slot, m_all, masked):
        start = pl.multiple_of(j * TK, TK)
        new_m = []
        for hh in range(ATT_HPS):
            vt = vt_ref[0, hh, :, pl.ds(start, TK)]
            st = s_ref[slot, hh]
            if masked:
                key = lax.broadcasted_iota(I32, (TK, TQ), 0)
                qry = lax.broadcasted_iota(I32, (TK, TQ), 1)
                st = jnp.where(key <= qry, st, -jnp.inf)
                block_max = col_max(st)
            else:
                block_max = bmax_ref[slot, hh]
            m_old = m_all[hh]
            m_new = jnp.maximum(m_old, block_max)
            alpha = jnp.exp2(m_old - m_new)
            p = jnp.exp2(st - m_new[0:1, :]).astype(BF16)
            acc_ref[hh] = acc_ref[hh] * alpha[0:1, :] + jnp.dot(vt, p, preferred_element_type=F32)
            new_m.append(m_new)
        return tuple(new_m)

    def pair(p, m_all):
        scores(2 * p + 1, 1)
        m_all = consume(2 * p, 0, m_all, False)
        scores(2 * p + 2, 0)
        return consume(2 * p + 1, 1, m_all, False)

    def odd_tail(_, m_all):
        scores(i, 1)
        return consume(i - 1, 0, m_all, False)

    def quad(g, m_all):
        return pair(2 * g + 1, pair(2 * g, m_all))

    m0 = jnp.full((8, TQ), -jnp.inf, F32)
    scores(0, 0)
    m_all = lax.fori_loop(0, i // 4, quad, (m0,) * ATT_HPS)
    m_all = lax.fori_loop(2 * (i // 4), i // 2, pair, m_all)
    m_all = lax.fori_loop(0, i & 1, odd_tail, m_all)

    @pl.when((i & 1) == 0)
    def _():
        consume(i, 0, m_all, True)

    @pl.when((i & 1) == 1)
    def _():
        consume(i, 1, m_all, True)

    outs = []
    for pr in range(ATT_HPS // 2):
        pair_t = []
        for hh in (2 * pr, 2 * pr + 1):
            a = acc_ref[hh]
            pair_t.append(a[0:MLA_V, :] / a[MLA_V:MLA_V + 1, :])
        outs.append(jnp.concatenate(pair_t, axis=0).T)
    o_ref[0] = jnp.concatenate(outs, axis=1).astype(BF16)


def _attention(q, k, vt, bsz, seqlen):
    nq = seqlen // TQ
    hps = ATT_HPS
    return pl.pallas_call(
        _attn_kernel,
        grid=(bsz, MLA_HEADS // hps, nq),
        in_specs=[pl.BlockSpec((1, hps, TQ, LANES), lambda b, p, i: (b, p, i, 0)),
                  pl.BlockSpec((1, hps, seqlen, LANES), lambda b, p, i: (b, p, 0, 0)),
                  pl.BlockSpec((1, hps, VT_ROWS, seqlen), lambda b, p, i: (b, p, 0, 0))],
        out_specs=pl.BlockSpec((1, TQ, hps * MLA_V), lambda b, p, i: (b, i, p)),
        out_shape=jax.ShapeDtypeStruct((bsz, seqlen, MLA_WIDTH), BF16),
        scratch_shapes=[pltpu.VMEM((hps, VT_ROWS, TQ), F32), pltpu.VMEM((2, hps, TK, TQ), F32),
                        pltpu.VMEM((2, hps, 8, TQ), F32)],
        compiler_params=pltpu.CompilerParams(
            dimension_semantics=("arbitrary", "arbitrary", "arbitrary"),
            vmem_limit_bytes=VMEM_LIMIT_BYTES),
        name="attention",
    )(q, k, vt)


def _outproj_kernel(x_ref, ys_ref, ym_ref, wo_ref, g_ref, wr_ref, br_ref,
                    x1_ref, h2_ref, route_ref, cnt_ref, wo_bf):
    tm = x_ref.shape[0]

    @pl.when(pl.program_id(0) == 0)
    def _():
        cnt_ref[...] = jnp.zeros_like(cnt_ref)
        wo_bf[...] = wo_ref[...].astype(BF16)

    mix = (jnp.dot(ys_ref[...], wo_bf[0:512, :], preferred_element_type=F32)
           + jnp.dot(ym_ref[...], wo_bf[512:1024, :], preferred_element_type=F32))
    x1 = x_ref[...] + mix
    x1_ref[...] = x1
    h2 = _rms(x1) * g_ref[...]
    h2_ref[...] = _pack_rows(h2)

    h_hi = h2.astype(BF16)
    h_lo = (h2 - h_hi.astype(F32)).astype(BF16)
    hh = jnp.dot(h_hi, wr_ref[...], preferred_element_type=F32)
    lh = jnp.dot(h_lo, wr_ref[:, 0:LANES], preferred_element_type=F32)
    logits = hh[:, 0:LANES] + (hh[:, LANES:2 * LANES] + lh) + br_ref[...]
    lt = logits.T[0:N_EXPERTS, :]
    eid = lax.broadcasted_iota(I32, (N_EXPERTS, 1), 0).astype(F32)

    vals, idxs, hots = [], [], []
    for _ in range(TOP_K):
        mx = jnp.max(lt, axis=0, keepdims=True)
        idx = jnp.min(jnp.where(lt == mx, eid, float(N_EXPERTS)), axis=0, keepdims=True)
        hot = eid == idx
        lt = jnp.where(hot, -jnp.inf, lt)
        vals.append(mx)
        idxs.append(idx)
        hots.append(hot)
    exps = [jnp.exp(v - vals[0]) for v in vals]
    denom = exps[0] + exps[1] + exps[2] + exps[3]

    multi_f = jnp.where(hots[0] | hots[1] | hots[2] | hots[3], 1.0, 0.0)
    r = lax.broadcasted_iota(I32, (tm, tm), 0)
    c = lax.broadcasted_iota(I32, (tm, tm), 1)
    earlier = jnp.where(r < c, 1.0, 0.0).astype(BF16)
    before = jnp.dot(multi_f.astype(BF16), earlier, preferred_element_type=F32) + cnt_ref[:, 0:1]
    cnt_ref[...] = cnt_ref[...] + jnp.sum(multi_f, axis=1, keepdims=True)

    ranks = [jnp.sum(jnp.where(hots[kk], before, 0.0), axis=0, keepdims=True) for kk in range(TOP_K)]
    gates = [e / denom for e in exps]
    route_ref[...] = jnp.concatenate(idxs + gates + ranks + [jnp.zeros((ROUTE_ROWS - 3 * TOP_K, tm), F32)], axis=0)


def _outproj(x2, y_ssd, y_mla, wo, g_ffn, wr, br):
    t = x2.shape[0]
    tm = TM_PROJ
    full = lambda shape: pl.BlockSpec(shape, lambda i: (0,) * len(shape))
    rows = lambda width: pl.BlockSpec((tm, width), lambda i: (i, 0))
    return pl.pallas_call(
        _outproj_kernel,
        grid=(t // tm,),
        in_specs=[rows(D_MODEL), rows(512), rows(512), full((1024, D_MODEL)), full((1, D_MODEL)),
                  full((D_MODEL, 2 * LANES)), full((1, LANES))],
        out_specs=[rows(D_MODEL), rows(HALF), pl.BlockSpec((ROUTE_ROWS, tm), lambda i: (0, i)),
                   full((N_EXPERTS, LANES))],
        out_shape=[jax.ShapeDtypeStruct((t, D_MODEL), F32),
                   jax.ShapeDtypeStruct((t, HALF), I32),
                   jax.ShapeDtypeStruct((ROUTE_ROWS, t), F32),
                   jax.ShapeDtypeStruct((N_EXPERTS, LANES), F32)],
        scratch_shapes=[pltpu.VMEM((1024, D_MODEL), BF16)],
        compiler_params=pltpu.CompilerParams(
            dimension_semantics=("arbitrary",), vmem_limit_bytes=VMEM_LIMIT_BYTES),
        name="outproj_router",
    )(x2, y_ssd, y_mla, wo, g_ffn, wr, br)


def _ffn_kernel(be_ref, bv_ref, nx_ref, sl_ref, xb_ref, xs_ref, wgu_hbm, bgu_ref, wd_hbm, bd_ref, ys_ref,
                wgu_st, wd_st, wgu_bf, wd_bf, sem):
    i = pl.program_id(0)
    e = be_ref[i]
    valid = bv_ref[i]
    slot = sl_ref[i]
    first = ((i == 0) | (e != be_ref[jnp.maximum(i - 1, 0)])) & (valid > 0)

    def weight_copies(expert, dst_slot):
        return (pltpu.make_async_copy(wgu_hbm.at[expert], wgu_st.at[dst_slot], sem.at[0, dst_slot]),
                pltpu.make_async_copy(wd_hbm.at[expert], wd_st.at[dst_slot], sem.at[1, dst_slot]))

    @pl.when(i == 0)
    def _():
        for cp in weight_copies(e, slot):
            cp.start()

    @pl.when(first)
    def _():
        for cp in weight_copies(e, slot):
            cp.wait()

        @pl.when(nx_ref[i] >= 0)
        def _():
            for cp in weight_copies(nx_ref[i], 1 - slot):
                cp.start(priority=1)

        wgu_bf[...] = wgu_st[slot].astype(BF16)
        wd_bf[...] = wd_st[slot].astype(BF16)

    def chain(r0, n_rows, masked):
        rows = pl.ds(r0, n_rows)
        xp = xs_ref[rows, :]
        if masked:
            xp = jnp.where(r0 + lax.broadcasted_iota(I32, (n_rows, 1), 0) < valid, xp, 0)
        x_lo, x_hi = _unpack_rows(xp)
        gu = (jnp.dot(x_lo.astype(BF16), wgu_bf[0:HALF, :], preferred_element_type=F32)
              + jnp.dot(x_hi.astype(BF16), wgu_bf[HALF:D_MODEL, :], preferred_element_type=F32)
              + bgu_ref[0])
        gate = jnp.minimum(gu[:, :D_FF], SWIGLU_LIMIT)
        up = jnp.clip(gu[:, D_FF:], -SWIGLU_LIMIT, SWIGLU_LIMIT)
        glu = gate * jax.nn.sigmoid(SWIGLU_ALPHA * gate)
        mid = ((up + 1.0) * glu).astype(BF16)
        ys_ref[rows, :] = _pack_rows(jnp.dot(mid, wd_bf[...], preferred_element_type=F32) + bd_ref[0])

    @pl.when(valid == MOE_STEP)
    def _():
        for r0 in range(0, MOE_STEP, MOE_BM):
            chain(r0, MOE_BM, False)

    @pl.when(valid < MOE_STEP)
    def _():
        half_bm = MOE_BM // 2
        for r0 in range(0, MOE_STEP, MOE_BM):
            pl.when(valid > r0 + half_bm)(functools.partial(chain, r0, MOE_BM, True))
            pl.when((valid > r0) & (valid <= r0 + half_bm))(functools.partial(chain, r0, half_bm, True))

            @pl.when(valid <= r0 + half_bm)
            def _():
                ys_ref[pl.ds(r0 + half_bm, half_bm), :] = jnp.zeros((half_bm, HALF), I32)

            @pl.when(valid <= r0)
            def _():
                ys_ref[pl.ds(r0, half_bm), :] = jnp.zeros((half_bm, HALF), I32)


def _expert_ffn(block_e, block_valid, block_next, block_slot, block_src, xs, wgu, bgu, wd, bd):
    n_slots = xs.shape[0]
    bm = MOE_STEP
    grid_spec = pltpu.PrefetchScalarGridSpec(
        num_scalar_prefetch=5,
        grid=(n_slots // bm,),
        in_specs=[
            pl.BlockSpec((bm, HALF), lambda i, be, bv, nx, sl, xb: (xb[i], 0)),
            pl.BlockSpec(memory_space=pl.ANY),
            pl.BlockSpec((1, 1, 2 * D_FF), lambda i, be, bv, nx, sl, xb: (be[i], 0, 0)),
            pl.BlockSpec(memory_space=pl.ANY),
            pl.BlockSpec((1, 1, D_MODEL), lambda i, be, bv, nx, sl, xb: (be[i], 0, 0)),
        ],
        out_specs=pl.BlockSpec((bm, HALF), lambda i, be, bv, nx, sl, xb: (i, 0)),
        scratch_shapes=[pltpu.VMEM((2, D_MODEL, 2 * D_FF), F32), pltpu.VMEM((2, D_FF, D_MODEL), F32),
                        pltpu.VMEM((D_MODEL, 2 * D_FF), BF16), pltpu.VMEM((D_FF, D_MODEL), BF16),
                        pltpu.SemaphoreType.DMA((2, 2))],
    )
    return pl.pallas_call(
        _ffn_kernel,
        grid_spec=grid_spec,
        out_shape=jax.ShapeDtypeStruct((n_slots, HALF), I32),
        compiler_params=pltpu.CompilerParams(
            dimension_semantics=("arbitrary",), vmem_limit_bytes=VMEM_LIMIT_BYTES),
        name="expert_ffn",
    )(block_e, block_valid, block_next, block_slot, block_src, xs, wgu, bgu, wd, bd)


def _plan_kernel(sp_ref, route_ref, dest_ref):
    idx = route_ref[0:TOP_K, :]
    rank = route_ref[2 * TOP_K:3 * TOP_K, :]
    start = jnp.zeros(idx.shape, F32)
    for e_i in range(N_EXPERTS):
        start = jnp.where(idx == float(e_i), sp_ref[e_i].astype(F32), start)
    dest_ref[...] = (start + rank).astype(I32)


def _slot_plan(route, start_pad):
    t = route.shape[1]
    tm = TM_PLAN
    grid_spec = pltpu.PrefetchScalarGridSpec(
        num_scalar_prefetch=1,
        grid=(t // tm,),
        in_specs=[pl.BlockSpec((ROUTE_ROWS, tm), lambda i, sp: (0, i))],
        out_specs=pl.BlockSpec((TOP_K, tm), lambda i, sp: (0, i)),
    )
    return pl.pallas_call(
        _plan_kernel,
        grid_spec=grid_spec,
        out_shape=jax.ShapeDtypeStruct((TOP_K, t), I32),
        compiler_params=pltpu.CompilerParams(dimension_semantics=("arbitrary",)),
        name="slot_plan",
    )(start_pad, route)


SC_SCATTER_CHUNK = 64
SC_GATHER_CHUNK = 64


def _sc_workers():
    info = plsc.get_sparse_core_info()
    return info.num_cores, info.num_cores * info.num_subcores


def _sc_scatter_rows(rows, dest_km, n_out):
    t, w = rows.shape
    ch = SC_SCATTER_CHUNK
    n_cores, n_workers = _sc_workers()
    n_chunks = t // n_workers // ch
    mesh = plsc.VectorSubcoreMesh(core_axis_name="c", subcore_axis_name="s")

    @functools.partial(
        pl.kernel, mesh=mesh, out_type=jax.ShapeDtypeStruct((n_out, w), rows.dtype),
        scratch_types=[pltpu.VMEM((TOP_K, n_chunks, ch), I32), pltpu.VMEM((2, ch, w), rows.dtype),
                       pltpu.SemaphoreType.DMA((2,)), pltpu.SemaphoreType.DMA((2,))],
        name="sc_dispatch_scatter")
    def scatter_kernel(rows_hbm, dest_hbm, out_hbm, idx_v, rows_v, sem_in, sem_out):
        wid = lax.axis_index("s") * n_cores + lax.axis_index("c")
        first = wid * n_chunks
        for kk in range(TOP_K):
            pltpu.sync_copy(dest_hbm.at[kk, pl.ds(first, n_chunks)], idx_v.at[kk])

        def load(cc, b):
            return pltpu.make_async_copy(rows_hbm.at[pl.ds((first + cc) * ch, ch)], rows_v.at[b], sem_in.at[b])

        def scatters(cc, b):
            return [pltpu.make_async_copy(rows_v.at[b], out_hbm.at[idx_v.at[kk, cc]], sem_out.at[b])
                    for kk in range(TOP_K)]

        load(0, 0).start()
        load(1, 1).start()

        @pl.loop(0, n_chunks, step=2)
        def _(c):
            for b in range(2):
                load(c + b, b).wait()
                for cp in scatters(c + b, b):
                    cp.start()
            for b in range(2):
                for cp in scatters(c + b, b):
                    cp.wait()

                @pl.when(c + 2 + b < n_chunks)
                def _():
                    load(c + 2 + b, b).start()

    return scatter_kernel(rows, dest_km.reshape(TOP_K, t // ch, ch))


def _sc_gather_rows(table, dest_km):
    _, w = table.shape
    t = dest_km.shape[1]
    ch = SC_GATHER_CHUNK
    n_cores, n_workers = _sc_workers()
    per_k = t // n_workers // ch
    n_chunks = TOP_K * per_k
    mesh = plsc.VectorSubcoreMesh(core_axis_name="c", subcore_axis_name="s")

    @functools.partial(
        pl.kernel, mesh=mesh, out_type=jax.ShapeDtypeStruct((TOP_K, t, w), table.dtype),
        scratch_types=[pltpu.VMEM((TOP_K, per_k, ch), I32), pltpu.VMEM((2, ch, w), table.dtype),
                       pltpu.SemaphoreType.DMA((2,)), pltpu.SemaphoreType.DMA((2,))],
        name="sc_combine_gather")
    def gather_kernel(table_hbm, dest_hbm, out_hbm, idx_v, rows_v, sem_in, sem_out):
        wid = lax.axis_index("s") * n_cores + lax.axis_index("c")
        first = wid * per_k
        for kk in range(TOP_K):
            pltpu.sync_copy(dest_hbm.at[kk, pl.ds(first, per_k)], idx_v.at[kk])

        def gather(cc, b):
            return pltpu.make_async_copy(table_hbm.at[idx_v.at[cc // per_k, cc % per_k]], rows_v.at[b], sem_in.at[b])

        def store(cc, b):
            return pltpu.make_async_copy(
                rows_v.at[b], out_hbm.at[cc // per_k, pl.ds((first + cc % per_k) * ch, ch)], sem_out.at[b])

        gather(0, 0).start()
        gather(1, 1).start()

        @pl.loop(0, n_chunks, step=2)
        def _(c):
            for b in range(2):
                gather(c + b, b).wait()
                store(c + b, b).start()
            for b in range(2):
                store(c + b, b).wait()

                @pl.when(c + 2 + b < n_chunks)
                def _():
                    gather(c + 2 + b, b).start()

    return gather_kernel(table, dest_km.reshape(TOP_K, t // ch, ch))


def _combine_kernel(x1_ref, yg_ref, route_ref, g_ref, o_ref, *, final_norm):
    tm = x1_ref.shape[0]
    moe_lo = jnp.zeros((tm, HALF), F32)
    moe_hi = jnp.zeros((tm, HALF), F32)
    route_t = jnp.concatenate([route_ref[...], jnp.zeros((LANES - ROUTE_ROWS, tm), F32)], axis=0).T
    for kk in range(TOP_K):
        gate = route_t[:, TOP_K + kk:TOP_K + kk + 1]
        y_lo, y_hi = _unpack_rows(yg_ref[kk])
        moe_lo = moe_lo + gate * y_lo
        moe_hi = moe_hi + gate * y_hi
    acc = x1_ref[...] + jnp.concatenate([moe_lo, moe_hi], axis=1)
    o_ref[...] = _rms(acc) * g_ref[...] if final_norm else acc


def _combine(x1, yg, route, g_final, final_norm):
    t = x1.shape[0]
    tm = TM_COMB
    return pl.pallas_call(
        functools.partial(_combine_kernel, final_norm=final_norm),
        grid=(t // tm,),
        in_specs=[pl.BlockSpec((tm, D_MODEL), lambda i: (i, 0)),
                  pl.BlockSpec((TOP_K, tm, HALF), lambda i: (0, i, 0)),
                  pl.BlockSpec((ROUTE_ROWS, tm), lambda i: (0, i)),
                  pl.BlockSpec((1, D_MODEL), lambda i: (0, 0))],
        out_specs=pl.BlockSpec((tm, D_MODEL), lambda i: (i, 0)),
        out_shape=jax.ShapeDtypeStruct((t, D_MODEL), F32),
        compiler_params=pltpu.CompilerParams(dimension_semantics=("arbitrary",)),
        name="combine",
    )(x1, yg, route, g_final)


def _prep_in_weights(w_in, w_uq, w_ukv):
    w_z = w_in[:, 0:512]
    w_xbc = w_in[:, 512:1536]
    w_dt = w_in[:, 1536:1544]
    w_cq = w_in[:, 1544:1800]
    w_ckv = w_in[:, 1800:1928]
    w_kr = w_in[:, 1928:1960]
    half = MLA_ROPE // 2
    zeros = lambda rows, width: jnp.zeros(rows + (width,), BF16)
    cat = lambda parts: jnp.concatenate([p.astype(BF16) for p in parts], axis=-1)
    d = (D_MODEL,)
    misc1 = [w_dt, zeros(d, MLA_NOPE - SSD_HEADS), w_kr, zeros(d, LANES - MLA_QK)]
    misc2 = [zeros(d, MLA_NOPE), w_kr[:, half:], w_kr[:, :half], zeros(d, LANES - MLA_QK)]
    w1 = cat([w_z, w_xbc, w_cq, w_ckv] + misc1 + misc2)

    wq3 = w_uq.reshape(MLA_Q_RANK, MLA_HEADS, MLA_QK)
    qh = (MLA_Q_RANK, MLA_HEADS)
    main = cat([wq3, zeros(qh, LANES - MLA_QK)])
    swap = cat([zeros(qh, MLA_NOPE), wq3[:, :, MLA_NOPE + half:], wq3[:, :, MLA_NOPE:MLA_NOPE + half],
                zeros(qh, LANES - MLA_QK)])
    wq = jnp.concatenate([main.reshape(MLA_Q_RANK, -1), swap.reshape(MLA_Q_RANK, -1)], axis=1)

    wkv3 = w_ukv.reshape(MLA_KV_RANK, MLA_HEADS, MLA_NOPE + MLA_V)
    kh = (MLA_KV_RANK, MLA_HEADS)
    kpart = cat([wkv3[:, :, :MLA_NOPE], zeros(kh, LANES - MLA_NOPE)])
    vpart = wkv3[:, :, MLA_NOPE:].astype(BF16)
    wkv = jnp.concatenate([kpart.reshape(MLA_KV_RANK, -1), vpart.reshape(MLA_KV_RANK, -1)], axis=1)
    return w1, wq, wkv


def _rope_inv_freq():
    inv_freq = ROPE_THETA ** (-jnp.arange(0, MLA_ROPE, 2, dtype=F32) / MLA_ROPE)
    return inv_freq[:, None]


def _pad_lanes(v, fill=0.0):
    return jnp.full((1, LANES), fill, F32).at[0, :v.shape[0]].set(v)


def kernel(x, positions, norm_mix_g, w_in, conv_w, conv_b, dt_bias, a_log, d_skip, ssd_norm_g, q_norm_g, w_uq, kv_norm_g, w_ukv, w_out, norm_ffn_g, w_router, b_router, w_gate_up, b_gate_up, w_down, b_down, norm_final_g):
    bsz, seqlen, d = x.shape
    t = bsz * seqlen
    depth = w_in.shape[0]
    x2 = x.reshape(t, d)
    pos_rows = positions.reshape(t // TM_PROJ, 1, TM_PROJ).astype(I32)
    invf = _rope_inv_freq()

    for l in range(depth):
        w1, wq, wkv = _prep_in_weights(w_in[l], w_uq[l], w_ukv[l])
        z, xbc, dtm, q, k, v = _inproj(
            x2, pos_rows, norm_mix_g[l][None, :], w1, q_norm_g[l][None, :], wq, kv_norm_g[l][None, :], wkv,
            invf, bsz, seqlen)
        y_ssd = _ssd(z, xbc, dtm, conv_w[l], conv_b[l][None, :], _pad_lanes(dt_bias[l]), _pad_lanes(a_log[l]),
                     jnp.repeat(d_skip[l], SSD_HEAD_DIM)[None, :], ssd_norm_g[l][None, :], bsz, seqlen)
        y_mla = _attention(q, k, v, bsz, seqlen).reshape(t, MLA_WIDTH)

        wr = jnp.zeros((d, LANES), F32).at[:, :N_EXPERTS].set(w_router[l])
        wr_hi = wr.astype(BF16)
        wr_lo = (wr - wr_hi.astype(F32)).astype(BF16)
        x1, h2p, route, cnt = _outproj(x2, y_ssd, y_mla, w_out[l], norm_ffn_g[l][None, :],
                                       jnp.concatenate([wr_hi, wr_lo], axis=1), _pad_lanes(b_router[l]))

        counts = cnt[:, 0].astype(I32)
        padded = ((counts + MOE_STEP - 1) // MOE_STEP) * MOE_STEP
        end_pad = jnp.cumsum(padded)
        start_pad = end_pad - padded
        n_slots = t * TOP_K + N_EXPERTS * MOE_STEP
        n_blocks = n_slots // MOE_STEP
        block_start = jnp.arange(n_blocks, dtype=I32) * MOE_STEP
        block_e = jnp.minimum(jnp.sum(block_start[:, None] >= end_pad[None, :], axis=1), N_EXPERTS - 1).astype(I32)
        eids = jnp.arange(N_EXPERTS, dtype=I32)
        block_hot = block_e[:, None] == eids[None, :]
        per_block = lambda table: jnp.sum(jnp.where(block_hot, table[None, :], 0), axis=1).astype(I32)
        block_valid = jnp.clip(per_block(counts) - (block_start - per_block(start_pad)), 0, MOE_STEP).astype(I32)
        used = counts > 0
        later_used = jnp.where((eids[None, :] > eids[:, None]) & used[None, :], eids[None, :], N_EXPERTS)
        next_used = jnp.min(later_used, axis=1)
        next_used = jnp.where(next_used < N_EXPERTS, next_used, -1).astype(I32)
        stage_slot = ((jnp.cumsum(used.astype(I32)) - 1) & 1).astype(I32)
        dest_km = _slot_plan(route, start_pad.astype(I32))

        xs = _sc_scatter_rows(h2p, dest_km, n_slots)
        n_used = end_pad[-1] // MOE_STEP
        block_src = jnp.minimum(jnp.arange(n_blocks, dtype=I32), n_used - 1).astype(I32)
        ys = _expert_ffn(block_e, block_valid, per_block(next_used), per_block(stage_slot), block_src, xs,
                         w_gate_up[l], b_gate_up[l][:, None, :], w_down[l], b_down[l][:, None, :])
        yg = _sc_gather_rows(ys, dest_km)
        x2 = _combine(x1, yg, route, norm_final_g[None, :], l == depth - 1)
    return x2.reshape(bsz, seqlen, d)
```

```python
import functools

import jax
import jax.numpy as jnp
import numpy as np
from jax import lax
from jax.experimental import pallas as pl
from jax.experimental.pallas import tpu as pltpu
from jax.experimental.pallas import tpu_sc as plsc

F32 = jnp.float32
BF16 = jnp.bfloat16
I32 = jnp.int32

D_MODEL = 1024
EPS = 1e-6
LANES = 128
V7X_VMEM_BYTES = 64 * 1024 * 1024
VMEM_LIMIT_BYTES = V7X_VMEM_BYTES * 7 // 8

SSD_HEADS = 8
SSD_HEAD_DIM = 64
SSD_WIDTH = 512
SSD_STATE = 128
SSD_CONV = 4
SSD_CHUNK = 128
SSD_CONV_DIM = 1024
CONV_PAD = 8
SSD_CPS = 8

MLA_HEADS = 8
MLA_Q_RANK = 256
MLA_KV_RANK = 128
MLA_NOPE = 64
MLA_ROPE = 32
MLA_V = 64
MLA_QK = MLA_NOPE + MLA_ROPE
MLA_WIDTH = 512
ROPE_THETA = 10000.0
LOG2_E = 1.4426950408889634

N_EXPERTS = 32
TOP_K = 4
D_FF = 1024
SWIGLU_LIMIT = 7.0
SWIGLU_ALPHA = 1.702

IN_W = 512 + 1024 + 256 + 128 + 128

TM_PROJ = 1024
TQ = 512
TK = 512
ATT_HPS = 4
VT_ROWS = 80
MOE_BM = 256
MOE_STEP = 1024
TM_COMB = 512
TM_PLAN = 2048
ROUTE_ROWS = 16

NT_DIMS = (((1,), (1,)), ((), ()))


def _rms(x):
    return x * lax.rsqrt(jnp.mean(x * x, axis=-1, keepdims=True) + EPS)


HALF = D_MODEL // 2
HI_MASK = np.int32(-65536)


def _pack_rows(a):
    lo = lax.bitcast_convert_type(a[:, :HALF].astype(BF16).astype(F32), I32)
    hi = lax.bitcast_convert_type(a[:, HALF:].astype(BF16).astype(F32), I32)
    return (hi & HI_MASK) | lax.shift_right_logical(lo, 16)


def _unpack_rows(p):
    lo = lax.bitcast_convert_type(lax.shift_left(p, 16), F32)
    hi = lax.bitcast_convert_type(p & HI_MASK, F32)
    return lo, hi


def _inproj_kernel(x_ref, pos_ref, g_ref, w1_ref, qg_ref, wq_ref, kvg_ref, wkv_ref, invf_ref,
                   z_ref, xbc_ref, dtm_ref, q_ref, k_ref, vt_ref):
    x = x_ref[...]
    h = (_rms(x) * g_ref[...]).astype(BF16)
    p = jnp.dot(h, w1_ref[...], preferred_element_type=F32)
    z_ref[...] = p[:, 0:512]
    xbc_ref[...] = p[:, 512:1536]
    cq = p[:, 1536:1792]
    ckv = p[:, 1792:1920]
    m1 = p[:, 1920:2048]
    dtm_ref[...] = m1

    lane = lax.broadcasted_iota(I32, (1, LANES), 1)
    tm = x.shape[0]
    ang = invf_ref[...] * pos_ref[0].astype(F32)
    cos_c = jnp.cos(ang)
    sin_c = jnp.sin(ang)
    z_lo = jnp.zeros((MLA_NOPE, tm), F32)
    z_hi = jnp.zeros((LANES - MLA_QK, tm), F32)
    cos_t = jnp.concatenate([z_lo, cos_c, cos_c, z_hi], axis=0).T
    sin_t = jnp.concatenate([z_lo, -sin_c, sin_c, z_hi], axis=0).T
    cosq_t = jnp.where(lane < MLA_NOPE, 1.0, cos_t)
    scale = MLA_QK ** -0.5 * LOG2_E

    cqn = (_rms(cq) * qg_ref[...]).astype(BF16)
    qq = jnp.dot(cqn, wq_ref[...], preferred_element_type=F32)
    ckvn = (_rms(ckv) * kvg_ref[...]).astype(BF16)
    kv = jnp.dot(ckvn, wkv_ref[...], preferred_element_type=F32)
    swap_down = LANES - MLA_ROPE
    krot = m1 * cos_t + pltpu.roll(m1, swap_down, axis=1) * sin_t
    for h_i in range(MLA_HEADS):
        lo = h_i * LANES
        qm = qq[:, lo:lo + LANES]
        qs = qq[:, 1024 + lo:1024 + lo + LANES]
        q_ref[0, h_i] = ((qm * cosq_t + qs * sin_t) * scale).astype(BF16)
        k_ref[0, h_i] = (kv[:, lo:lo + LANES] + krot).astype(BF16)
    ones_rows = jnp.ones((VT_ROWS - MLA_V, tm), BF16)
    for pr in range(MLA_HEADS // 2):
        vpt = kv[:, 1024 + pr * LANES:1024 + (pr + 1) * LANES].T.astype(BF16)
        vt_ref[0, 2 * pr] = jnp.concatenate([vpt[0:MLA_V, :], ones_rows], axis=0)
        vt_ref[0, 2 * pr + 1] = jnp.concatenate([vpt[MLA_V:2 * MLA_V, :], ones_rows], axis=0)


def _inproj(x2, pos_rows, g_mix, w1, qg, wq, kvg, wkv, invf, bsz, seqlen):
    t = x2.shape[0]
    tm = TM_PROJ
    per_b = seqlen // tm
    full = lambda shape: pl.BlockSpec(shape, lambda i: (0,) * len(shape))
    head_spec = pl.BlockSpec((1, MLA_HEADS, tm, LANES), lambda i: (i // per_b, 0, i % per_b, 0))
    head_shape = jax.ShapeDtypeStruct((bsz, MLA_HEADS, seqlen, LANES), BF16)
    vt_spec = pl.BlockSpec((1, MLA_HEADS, VT_ROWS, tm), lambda i: (i // per_b, 0, 0, i % per_b))
    vt_shape = jax.ShapeDtypeStruct((bsz, MLA_HEADS, VT_ROWS, seqlen), BF16)
    return pl.pallas_call(
        _inproj_kernel,
        grid=(t // tm,),
        in_specs=[
            pl.BlockSpec((tm, D_MODEL), lambda i: (i, 0)),
            pl.BlockSpec((1, 1, tm), lambda i: (i, 0, 0)),
            full((1, D_MODEL)), full((D_MODEL, IN_W)),
            full((1, MLA_Q_RANK)), full((MLA_Q_RANK, 2048)),
            full((1, MLA_KV_RANK)), full((MLA_KV_RANK, 1536)),
            full((MLA_ROPE // 2, 1)),
        ],
        out_specs=[
            pl.BlockSpec((tm, 512), lambda i: (i, 0)),
            pl.BlockSpec((tm, 1024), lambda i: (i, 0)),
            pl.BlockSpec((tm, LANES), lambda i: (i, 0)),
            head_spec, head_spec, vt_spec,
        ],
        out_shape=[
            jax.ShapeDtypeStruct((t, 512), F32),
            jax.ShapeDtypeStruct((t, 1024), F32),
            jax.ShapeDtypeStruct((t, LANES), F32),
            head_shape, head_shape, vt_shape,
        ],
        compiler_params=pltpu.CompilerParams(
            dimension_semantics=("arbitrary",), vmem_limit_bytes=VMEM_LIMIT_BYTES),
        name="inproj",
    )(x2, pos_rows, g_mix, w1, qg, wq, kvg, wkv, invf)


def _ssd_kernel(z_ref, xbc_ref, dtm_ref, cw_ref, cb_ref, dtb_ref, alog_ref, dsk_ref, ng_ref,
                y_ref, ext_ref, st_ref):
    q = SSD_CHUNK
    rows = SSD_CPS * q

    @pl.when(pl.program_id(1) == 0)
    def _():
        ext_ref[0:CONV_PAD, :] = jnp.zeros((CONV_PAD, SSD_CONV_DIM), F32)
        st_ref[...] = jnp.zeros_like(st_ref)

    ext_ref[CONV_PAD:CONV_PAD + rows, :] = xbc_ref[...]

    lane = lax.broadcasted_iota(I32, (1, LANES), 1)
    row = lax.broadcasted_iota(I32, (q, q), 0)
    col = lax.broadcasted_iota(I32, (q, q), 1)
    tril = row >= col
    tril_b = jnp.where(tril, 1.0, 0.0).astype(BF16)
    spread = jnp.where(
        lax.broadcasted_iota(I32, (LANES, SSD_WIDTH), 0)
        == lax.broadcasted_iota(I32, (LANES, SSD_WIDTH), 1) // SSD_HEAD_DIM, 1.0, 0.0).astype(BF16)
    a_neg = -jnp.exp(alog_ref[...]) * LOG2_E

    def split3(v):
        hi = v.astype(BF16)
        r1 = v - hi.astype(F32)
        mid = r1.astype(BF16)
        return hi, mid, (r1 - mid.astype(F32)).astype(BF16)

    def dot3_right(parts, m):
        return sum(jnp.dot(p, m, preferred_element_type=F32) for p in parts)

    def expand(cols):
        return dot3_right(split3(cols), spread)

    for ci in range(SSD_CPS):
        lo = ci * q
        window = ext_ref[lo:lo + CONV_PAD + q, :]
        conv = cb_ref[...] + cw_ref[SSD_CONV - 1:SSD_CONV, :] * window[CONV_PAD:, :]
        for kk in range(SSD_CONV - 1):
            shifted = pltpu.roll(window, SSD_CONV - 1 - kk, axis=0)
            conv = conv + cw_ref[kk:kk + 1, :] * shifted[CONV_PAD:, :]
        u = conv * jax.nn.sigmoid(conv)
        xs = u[:, 0:512]
        bm = u[:, 512:768]
        cm = u[:, 768:1024]

        xdt = dtm_ref[lo:lo + q, :] + dtb_ref[...]
        dt = jnp.maximum(xdt, 0.0) + jnp.log1p(jnp.exp(-jnp.abs(xdt)))
        adt = jnp.where(lane < SSD_HEADS, dt * a_neg, 0.0)
        cum_col = sum(jnp.dot(tril_b, p, preferred_element_type=F32) for p in split3(adt))
        cum_row = cum_col.T

        dt_e = expand(dt)
        ac_e = expand(cum_col)
        last_e = ac_e[q - 1:q, :]
        xd = xs * dt_e
        w_end = xd * jnp.exp2(last_e - ac_e)
        eac = jnp.exp2(ac_e)
        cdec = jnp.exp2(last_e)

        y_parts = []
        for g in range(2):
            gl = g * 256
            bg = bm[:, g * SSD_STATE:(g + 1) * SSD_STATE]
            cg = cm[:, g * SSD_STATE:(g + 1) * SSD_STATE].astype(BF16)
            scores = lax.dot_general(cg, bg.astype(BF16), NT_DIMS, preferred_element_type=F32)
            bgt = bg.T.astype(BF16)
            sprev = st_ref[g]
            yoff = jnp.dot(cg, sprev.astype(BF16), preferred_element_type=F32)
            st_ref[g] = sprev * cdec[:, gl:gl + 256] + jnp.dot(
                bgt, w_end[:, gl:gl + 256].astype(BF16), preferred_element_type=F32)
            for pr in range(2):
                pl_lo = gl + pr * LANES
                xdp = xd[:, pl_lo:pl_lo + LANES].astype(BF16)
                res = []
                for jj in range(2):
                    h_i = g * 4 + pr * 2 + jj
                    seg = cum_col[:, h_i:h_i + 1] - cum_row[h_i:h_i + 1, :]
                    dec = jnp.exp2(jnp.where(tril, seg, -jnp.inf))
                    res.append(jnp.dot((scores * dec).astype(BF16), xdp, preferred_element_type=F32))
                ydiag = jnp.where(lane < SSD_HEAD_DIM, res[0], res[1])
                y_parts.append(ydiag + yoff[:, pr * LANES:(pr + 1) * LANES] * eac[:, pl_lo:pl_lo + LANES])
        y = jnp.concatenate(y_parts, axis=1) + dsk_ref[...] * xs
        zz = z_ref[lo:lo + q, :]
        y = y * (zz * jax.nn.sigmoid(zz))
        outs = []
        for g in range(2):
            yg = y[:, g * 256:(g + 1) * 256]
            outs.append(_rms(yg))
        y_ref[lo:lo + q, :] = (jnp.concatenate(outs, axis=1) * ng_ref[...]).astype(BF16)

    ext_ref[0:CONV_PAD, :] = ext_ref[rows:rows + CONV_PAD, :]


def _ssd(z, xbc, dtm, cw, cb, dtb, alog, dsk, ng, bsz, seqlen):
    t = z.shape[0]
    q = SSD_CHUNK
    rows = SSD_CPS * q
    nc = seqlen // rows
    full = lambda shape: pl.BlockSpec(shape, lambda b, c: (0,) * len(shape))
    row_spec = lambda width: pl.BlockSpec((rows, width), lambda b, c: (b * nc + c, 0))
    return pl.pallas_call(
        _ssd_kernel,
        grid=(bsz, nc),
        in_specs=[row_spec(512), row_spec(1024), row_spec(LANES),
                  full((SSD_CONV, SSD_CONV_DIM)), full((1, SSD_CONV_DIM)),
                  full((1, LANES)), full((1, LANES)), full((1, SSD_WIDTH)), full((1, SSD_WIDTH))],
        out_specs=row_spec(512),
        out_shape=jax.ShapeDtypeStruct((t, SSD_WIDTH), BF16),
        scratch_shapes=[pltpu.VMEM((rows + CONV_PAD, SSD_CONV_DIM), F32),
                        pltpu.VMEM((2, SSD_STATE, 256), F32)],
        compiler_params=pltpu.CompilerParams(dimension_semantics=("arbitrary", "arbitrary")),
        name="ssd",
    )(z, xbc, dtm, cw, cb, dtb, alog, dsk, ng)


def _attn_kernel(q_ref, k_ref, vt_ref, o_ref, acc_ref, s_ref, bmax_ref):
    i = pl.program_id(2)
    acc_ref[...] = jnp.zeros_like(acc_ref)

    def col_max(st):
        part = st[0:LANES, :]
        for r0 in range(LANES, TK, LANES):
            part = jnp.maximum(part, st[r0:r0 + LANES, :])
        return jnp.broadcast_to(jnp.max(part, axis=0, keepdims=True), (8, TQ))

    def scores(j, slot):
        start = pl.multiple_of(j * TK, TK)
        for hh in range(ATT_HPS):
            kb = k_ref[0, hh, pl.ds(start, TK), :]
            st = lax.dot_general(kb, q_ref[0, hh], NT_DIMS, preferred_element_type=F32)
            s_ref[slot, hh] = st
            bmax_ref[slot, hh] = col_max(st)

    def consume(j, slot, m_all, masked):
        start = pl.multiple_of(j * TK, TK)
        new_m = []
        for hh in range(ATT_HPS):
            vt = vt_ref[0, hh, :, pl.ds(start, TK)]
            st = s_ref[slot, hh]
            if masked:
                key = lax.broadcasted_iota(I32, (TK, TQ), 0)
                qry = lax.broadcasted_iota(I32, (TK, TQ), 1)
                st = jnp.where(key <= qry, st, -jnp.inf)
                block_max = col_max(st)
            else:
                block_max = bmax_ref[slot, hh]
            m_old = m_all[hh]
            m_new = jnp.maximum(m_old, block_max)
            alpha = jnp.exp2(m_old - m_new)
            p = jnp.exp2(st - m_new[0:1, :]).astype(BF16)
            acc_ref[hh] = acc_ref[hh] * alpha[0:1, :] + jnp.dot(vt, p, preferred_element_type=F32)
            new_m.append(m_new)
        return tuple(new_m)

    def pair(p, m_all):
        scores(2 * p + 1, 1)
        m_all = consume(2 * p, 0, m_all, False)
        scores(2 * p + 2, 0)
        return consume(2 * p + 1, 1, m_all, False)

    def odd_tail(_, m_all):
        scores(i, 1)
        return consume(i - 1, 0, m_all, False)

    def quad(g, m_all):
        return pair(2 * g + 1, pair(2 * g, m_all))

    m0 = jnp.full((8, TQ), -jnp.inf, F32)
    scores(0, 0)
    m_all = lax.fori_loop(0, i // 4, quad, (m0,) * ATT_HPS)
    m_all = lax.fori_loop(2 * (i // 4), i // 2, pair, m_all)
    m_all = lax.fori_loop(0, i & 1, odd_tail, m_all)

    @pl.when((i & 1) == 0)
    def _():
        consume(i, 0, m_all, True)

    @pl.when((i & 1) == 1)
    def _():
        consume(i, 1, m_all, True)

    outs = []
    for pr in range(ATT_HPS // 2):
        pair_t = []
        for hh in (2 * pr, 2 * pr + 1):
            a = acc_ref[hh]
            pair_t.append(a[0:MLA_V, :] / a[MLA_V:MLA_V + 1, :])
        outs.append(jnp.concatenate(pair_t, axis=0).T)
    o_ref[0] = jnp.concatenate(outs, axis=1).astype(BF16)


def _attention(q, k, vt, bsz, seqlen):
    nq = seqlen // TQ
    hps = ATT_HPS
    return pl.pallas_call(
        _attn_kernel,
        grid=(bsz, MLA_HEADS // hps, nq),
        in_specs=[pl.BlockSpec((1, hps, TQ, LANES), lambda b, p, i: (b, p, i, 0)),
                  pl.BlockSpec((1, hps, seqlen, LANES), lambda b, p, i: (b, p, 0, 0)),
                  pl.BlockSpec((1, hps, VT_ROWS, seqlen), lambda b, p, i: (b, p, 0, 0))],
        out_specs=pl.BlockSpec((1, TQ, hps * MLA_V), lambda b, p, i: (b, i, p)),
        out_shape=jax.ShapeDtypeStruct((bsz, seqlen, MLA_WIDTH), BF16),
        scratch_shapes=[pltpu.VMEM((hps, VT_ROWS, TQ), F32), pltpu.VMEM((2, hps, TK, TQ), F32),
                        pltpu.VMEM((2, hps, 8, TQ), F32)],
        compiler_params=pltpu.CompilerParams(
            dimension_semantics=("arbitrary", "arbitrary", "arbitrary"),
            vmem_limit_bytes=VMEM_LIMIT_BYTES),
        name="attention",
    )(q, k, vt)


def _outproj_kernel(x_ref, ys_ref, ym_ref, wo_ref, g_ref, wr_ref, br_ref,
                    x1_ref, h2_ref, route_ref, cnt_ref, wo_bf):
    tm = x_ref.shape[0]

    @pl.when(pl.program_id(0) == 0)
    def _():
        cnt_ref[...] = jnp.zeros_like(cnt_ref)
        wo_bf[...] = wo_ref[...].astype(BF16)

    mix = (jnp.dot(ys_ref[...], wo_bf[0:512, :], preferred_element_type=F32)
           + jnp.dot(ym_ref[...], wo_bf[512:1024, :], preferred_element_type=F32))
    x1 = x_ref[...] + mix
    x1_ref[...] = x1
    h2 = _rms(x1) * g_ref[...]
    h2_ref[...] = _pack_rows(h2)

    h_hi = h2.astype(BF16)
    h_lo = (h2 - h_hi.astype(F32)).astype(BF16)
    hh = jnp.dot(h_hi, wr_ref[...], preferred_element_type=F32)
    lh = jnp.dot(h_lo, wr_ref[:, 0:LANES], preferred_element_type=F32)
    logits = hh[:, 0:LANES] + (hh[:, LANES:2 * LANES] + lh) + br_ref[...]
    lt = logits.T[0:N_EXPERTS, :]
    eid = lax.broadcasted_iota(I32, (N_EXPERTS, 1), 0).astype(F32)

    vals, idxs, hots = [], [], []
    for _ in range(TOP_K):
        mx = jnp.max(lt, axis=0, keepdims=True)
        idx = jnp.min(jnp.where(lt == mx, eid, float(N_EXPERTS)), axis=0, keepdims=True)
        hot = eid == idx
        lt = jnp.where(hot, -jnp.inf, lt)
        vals.append(mx)
        idxs.append(idx)
        hots.append(hot)
    exps = [jnp.exp(v - vals[0]) for v in vals]
    denom = exps[0] + exps[1] + exps[2] + exps[3]

    multi_f = jnp.where(hots[0] | hots[1] | hots[2] | hots[3], 1.0, 0.0)
    r = lax.broadcasted_iota(I32, (tm, tm), 0)
    c = lax.broadcasted_iota(I32, (tm, tm), 1)
    earlier = jnp.where(r < c, 1.0, 0.0).astype(BF16)
    before = jnp.dot(multi_f.astype(BF16), earlier, preferred_element_type=F32) + cnt_ref[:, 0:1]
    cnt_ref[...] = cnt_ref[...] + jnp.sum(multi_f, axis=1, keepdims=True)

    ranks = [jnp.sum(jnp.where(hots[kk], before, 0.0), axis=0, keepdims=True) for kk in range(TOP_K)]
    gates = [e / denom for e in exps]
    route_ref[...] = jnp.concatenate(idxs + gates + ranks + [jnp.zeros((ROUTE_ROWS - 3 * TOP_K, tm), F32)], axis=0)


def _outproj(x2, y_ssd, y_mla, wo, g_ffn, wr, br):
    t = x2.shape[0]
    tm = TM_PROJ
    full = lambda shape: pl.BlockSpec(shape, lambda i: (0,) * len(shape))
    rows = lambda width: pl.BlockSpec((tm, width), lambda i: (i, 0))
    return pl.pallas_call(
        _outproj_kernel,
        grid=(t // tm,),
        in_specs=[rows(D_MODEL), rows(512), rows(512), full((1024, D_MODEL)), full((1, D_MODEL)),
                  full((D_MODEL, 2 * LANES)), full((1, LANES))],
        out_specs=[rows(D_MODEL), rows(HALF), pl.BlockSpec((ROUTE_ROWS, tm), lambda i: (0, i)),
                   full((N_EXPERTS, LANES))],
        out_shape=[jax.ShapeDtypeStruct((t, D_MODEL), F32),
                   jax.ShapeDtypeStruct((t, HALF), I32),
                   jax.ShapeDtypeStruct((ROUTE_ROWS, t), F32),
                   jax.ShapeDtypeStruct((N_EXPERTS, LANES), F32)],
        scratch_shapes=[pltpu.VMEM((1024, D_MODEL), BF16)],
        compiler_params=pltpu.CompilerParams(
            dimension_semantics=("arbitrary",), vmem_limit_bytes=VMEM_LIMIT_BYTES),
        name="outproj_router",
    )(x2, y_ssd, y_mla, wo, g_ffn, wr, br)


def _ffn_kernel(be_ref, bv_ref, nx_ref, sl_ref, xb_ref, xs_ref, wgu_hbm, bgu_ref, wd_hbm, bd_ref, ys_ref,
                wgu_st, wd_st, wgu_bf, wd_bf, sem):
    i = pl.program_id(0)
    e = be_ref[i]
    valid = bv_ref[i]
    slot = sl_ref[i]
    first = ((i == 0) | (e != be_ref[jnp.maximum(i - 1, 0)])) & (valid > 0)

    def weight_copies(expert, dst_slot):
        return (pltpu.make_async_copy(wgu_hbm.at[expert], wgu_st.at[dst_slot], sem.at[0, dst_slot]),
                pltpu.make_async_copy(wd_hbm.at[expert], wd_st.at[dst_slot], sem.at[1, dst_slot]))

    @pl.when(i == 0)
    def _():
        for cp in weight_copies(e, slot):
            cp.start()

    @pl.when(first)
    def _():
        for cp in weight_copies(e, slot):
            cp.wait()

        @pl.when(nx_ref[i] >= 0)
        def _():
            for cp in weight_copies(nx_ref[i], 1 - slot):
                cp.start(priority=1)

        wgu_bf[...] = wgu_st[slot].astype(BF16)
        wd_bf[...] = wd_st[slot].astype(BF16)

    def chain(r0, n_rows, masked):
        rows = pl.ds(r0, n_rows)
        xp = xs_ref[rows, :]
        if masked:
            xp = jnp.where(r0 + lax.broadcasted_iota(I32, (n_rows, 1), 0) < valid, xp, 0)
        x_lo, x_hi = _unpack_rows(xp)
        gu = (jnp.dot(x_lo.astype(BF16), wgu_bf[0:HALF, :], preferred_element_type=F32)
              + jnp.dot(x_hi.astype(BF16), wgu_bf[HALF:D_MODEL, :], preferred_element_type=F32)
              + bgu_ref[0])
        gate = jnp.minimum(gu[:, :D_FF], SWIGLU_LIMIT)
        up = jnp.clip(gu[:, D_FF:], -SWIGLU_LIMIT, SWIGLU_LIMIT)
        glu = gate * jax.nn.sigmoid(SWIGLU_ALPHA * gate)
        mid = ((up + 1.0) * glu).astype(BF16)
        ys_ref[rows, :] = _pack_rows(jnp.dot(mid, wd_bf[...], preferred_element_type=F32) + bd_ref[0])

    @pl.when(valid == MOE_STEP)
    def _():
        for r0 in range(0, MOE_STEP, MOE_BM):
            chain(r0, MOE_BM, False)

    @pl.when(valid < MOE_STEP)
    def _():
        half_bm = MOE_BM // 2
        for r0 in range(0, MOE_STEP, MOE_BM):
            pl.when(valid > r0 + half_bm)(functools.partial(chain, r0, MOE_BM, True))
            pl.when((valid > r0) & (valid <= r0 + half_bm))(functools.partial(chain, r0, half_bm, True))

            @pl.when(valid <= r0 + half_bm)
            def _():
                ys_ref[pl.ds(r0 + half_bm, half_bm), :] = jnp.zeros((half_bm, HALF), I32)

            @pl.when(valid <= r0)
            def _():
                ys_ref[pl.ds(r0, half_bm), :] = jnp.zeros((half_bm, HALF), I32)


def _expert_ffn(block_e, block_valid, block_next, block_slot, block_src, xs, wgu, bgu, wd, bd):
    n_slots = xs.shape[0]
    bm = MOE_STEP
    grid_spec = pltpu.PrefetchScalarGridSpec(
        num_scalar_prefetch=5,
        grid=(n_slots // bm,),
        in_specs=[
            pl.BlockSpec((bm, HALF), lambda i, be, bv, nx, sl, xb: (xb[i], 0)),
            pl.BlockSpec(memory_space=pl.ANY),
            pl.BlockSpec((1, 1, 2 * D_FF), lambda i, be, bv, nx, sl, xb: (be[i], 0, 0)),
            pl.BlockSpec(memory_space=pl.ANY),
            pl.BlockSpec((1, 1, D_MODEL), lambda i, be, bv, nx, sl, xb: (be[i], 0, 0)),
        ],
        out_specs=pl.BlockSpec((bm, HALF), lambda i, be, bv, nx, sl, xb: (i, 0)),
        scratch_shapes=[pltpu.VMEM((2, D_MODEL, 2 * D_FF), F32), pltpu.VMEM((2, D_FF, D_MODEL), F32),
                        pltpu.VMEM((D_MODEL, 2 * D_FF), BF16), pltpu.VMEM((D_FF, D_MODEL), BF16),
                        pltpu.SemaphoreType.DMA((2, 2))],
    )
    return pl.pallas_call(
        _ffn_kernel,
        grid_spec=grid_spec,
        out_shape=jax.ShapeDtypeStruct((n_slots, HALF), I32),
        compiler_params=pltpu.CompilerParams(
            dimension_semantics=("arbitrary",), vmem_limit_bytes=VMEM_LIMIT_BYTES),
        name="expert_ffn",
    )(block_e, block_valid, block_next, block_slot, block_src, xs, wgu, bgu, wd, bd)


def _plan_kernel(sp_ref, route_ref, dest_ref):
    idx = route_ref[0:TOP_K, :]
    rank = route_ref[2 * TOP_K:3 * TOP_K, :]
    start = jnp.zeros(idx.shape, F32)
    for e_i in range(N_EXPERTS):
        start = jnp.where(idx == float(e_i), sp_ref[e_i].astype(F32), start)
    dest_ref[...] = (start + rank).astype(I32)


def _slot_plan(route, start_pad):
    t = route.shape[1]
    tm = TM_PLAN
    grid_spec = pltpu.PrefetchScalarGridSpec(
        num_scalar_prefetch=1,
        grid=(t // tm,),
        in_specs=[pl.BlockSpec((ROUTE_ROWS, tm), lambda i, sp: (0, i))],
        out_specs=pl.BlockSpec((TOP_K, tm), lambda i, sp: (0, i)),
    )
    return pl.pallas_call(
        _plan_kernel,
        grid_spec=grid_spec,
        out_shape=jax.ShapeDtypeStruct((TOP_K, t), I32),
        compiler_params=pltpu.CompilerParams(dimension_semantics=("arbitrary",)),
        name="slot_plan",
    )(start_pad, route)


SC_SCATTER_CHUNK = 64
SC_GATHER_CHUNK = 64


def _sc_workers():
    info = plsc.get_sparse_core_info()
    return info.num_cores, info.num_cores * info.num_subcores


def _sc_scatter_rows(rows, dest_km, n_out):
    t, w = rows.shape
    ch = SC_SCATTER_CHUNK
    n_cores, n_workers = _sc_workers()
    n_chunks = t // n_workers // ch
    mesh = plsc.VectorSubcoreMesh(core_axis_name="c", subcore_axis_name="s")

    @functools.partial(
        pl.kernel, mesh=mesh, out_type=jax.ShapeDtypeStruct((n_out, w), rows.dtype),
        scratch_types=[pltpu.VMEM((TOP_K, n_chunks, ch), I32), pltpu.VMEM((2, ch, w), rows.dtype),
                       pltpu.SemaphoreType.DMA((2,)), pltpu.SemaphoreType.DMA((2,))],
        name="sc_dispatch_scatter")
    def scatter_kernel(rows_hbm, dest_hbm, out_hbm, idx_v, rows_v, sem_in, sem_out):
        wid = lax.axis_index("s") * n_cores + lax.axis_index("c")
        first = wid * n_chunks
        for kk in range(TOP_K):
            pltpu.sync_copy(dest_hbm.at[kk, pl.ds(first, n_chunks)], idx_v.at[kk])

        def load(cc, b):
            return pltpu.make_async_copy(rows_hbm.at[pl.ds((first + cc) * ch, ch)], rows_v.at[b], sem_in.at[b])

        def scatters(cc, b):
            return [pltpu.make_async_copy(rows_v.at[b], out_hbm.at[idx_v.at[kk, cc]], sem_out.at[b])
                    for kk in range(TOP_K)]

        load(0, 0).start()
        load(1, 1).start()

        @pl.loop(0, n_chunks, step=2)
        def _(c):
            for b in range(2):
                load(c + b, b).wait()
                for cp in scatters(c + b, b):
                    cp.start()
            for b in range(2):
                for cp in scatters(c + b, b):
                    cp.wait()

                @pl.when(c + 2 + b < n_chunks)
                def _():
                    load(c + 2 + b, b).start()

    return scatter_kernel(rows, dest_km.reshape(TOP_K, t // ch, ch))


def _sc_gather_rows(table, dest_km):
    _, w = table.shape
    t = dest_km.shape[1]
    ch = SC_GATHER_CHUNK
    n_cores, n_workers = _sc_workers()
    per_k = t // n_workers // ch
    n_chunks = TOP_K * per_k
    mesh = plsc.VectorSubcoreMesh(core_axis_name="c", subcore_axis_name="s")

    @functools.partial(
        pl.kernel, mesh=mesh, out_type=jax.ShapeDtypeStruct((TOP_K, t, w), table.dtype),
        scratch_types=[pltpu.VMEM((TOP_K, per_k, ch), I32), pltpu.VMEM((2, ch, w), table.dtype),
                       pltpu.SemaphoreType.DMA((2,)), pltpu.SemaphoreType.DMA((2,))],
        name="sc_combine_gather")
    def gather_kernel(table_hbm, dest_hbm, out_hbm, idx_v, rows_v, sem_in, sem_out):
        wid = lax.axis_index("s") * n_cores + lax.axis_index("c")
        first = wid * per_k
        for kk in range(TOP_K):
            pltpu.sync_copy(dest_hbm.at[kk, pl.ds(first, per_k)], idx_v.at[kk])

        def gather(cc, b):
            return pltpu.make_async_copy(table_hbm.at[idx_v.at[cc // per_k, cc % per_k]], rows_v.at[b], sem_in.at[b])

        def store(cc, b):
            return pltpu.make_async_copy(
                rows_v.at[b], out_hbm.at[cc // per_k, pl.ds((first + cc % per_k) * ch, ch)], sem_out.at[b])

        gather(0, 0).start()
        gather(1, 1).start()

        @pl.loop(0, n_chunks, step=2)
        def _(c):
            for b in range(2):
                gather(c + b, b).wait()
                store(c + b, b).start()
            for b in range(2):
                store(c + b, b).wait()

                @pl.when(c + 2 + b < n_chunks)
                def _():
                    gather(c + 2 + b, b).start()

    return gather_kernel(table, dest_km.reshape(TOP_K, t // ch, ch))


def _combine_kernel(x1_ref, yg_ref, route_ref, g_ref, o_ref, *, final_norm):
    tm = x1_ref.shape[0]
    moe_lo = jnp.zeros((tm, HALF), F32)
    moe_hi = jnp.zeros((tm, HALF), F32)
    route_t = jnp.concatenate([route_ref[...], jnp.zeros((LANES - ROUTE_ROWS, tm), F32)], axis=0).T
    for kk in range(TOP_K):
        gate = route_t[:, TOP_K + kk:TOP_K + kk + 1]
        y_lo, y_hi = _unpack_rows(yg_ref[kk])
        moe_lo = moe_lo + gate * y_lo
        moe_hi = moe_hi + gate * y_hi
    acc = x1_ref[...] + jnp.concatenate([moe_lo, moe_hi], axis=1)
    o_ref[...] = _rms(acc) * g_ref[...] if final_norm else acc


def _combine(x1, yg, route, g_final, final_norm):
    t = x1.shape[0]
    tm = TM_COMB
    return pl.pallas_call(
        functools.partial(_combine_kernel, final_norm=final_norm),
        grid=(t // tm,),
        in_specs=[pl.BlockSpec((tm, D_MODEL), lambda i: (i, 0)),
                  pl.BlockSpec((TOP_K, tm, HALF), lambda i: (0, i, 0)),
                  pl.BlockSpec((ROUTE_ROWS, tm), lambda i: (0, i)),
                  pl.BlockSpec((1, D_MODEL), lambda i: (0, 0))],
        out_specs=pl.BlockSpec((tm, D_MODEL), lambda i: (i, 0)),
        out_shape=jax.ShapeDtypeStruct((t, D_MODEL), F32),
        compiler_params=pltpu.CompilerParams(dimension_semantics=("arbitrary",)),
        name="combine",
    )(x1, yg, route, g_final)


def _prep_in_weights(w_in, w_uq, w_ukv):
    w_z = w_in[:, 0:512]
    w_xbc = w_in[:, 512:1536]
    w_dt = w_in[:, 1536:1544]
    w_cq = w_in[:, 1544:1800]
    w_ckv = w_in[:, 1800:1928]
    w_kr = w_in[:, 1928:1960]
    half = MLA_ROPE // 2
    zeros = lambda rows, width: jnp.zeros(rows + (width,), BF16)
    cat = lambda parts: jnp.concatenate([p.astype(BF16) for p in parts], axis=-1)
    d = (D_MODEL,)
    misc = [w_dt, zeros(d, MLA_NOPE - SSD_HEADS), w_kr, w_kr[:, half:], w_kr[:, :half]]
    w1 = cat([w_z, w_xbc, w_cq, w_ckv] + misc)

    wq3 = w_uq.reshape(MLA_Q_RANK, MLA_HEADS, MLA_QK)
    qh = (MLA_Q_RANK, MLA_HEADS)
    main = cat([wq3, zeros(qh, LANES - MLA_QK)])
    swap = cat([zeros(qh, MLA_NOPE), wq3[:, :, MLA_NOPE + half:], wq3[:, :, MLA_NOPE:MLA_NOPE + half],
                zeros(qh, LANES - MLA_QK)])
    wq = jnp.concatenate([main.reshape(MLA_Q_RANK, -1), swap.reshape(MLA_Q_RANK, -1)], axis=1)

    wkv3 = w_ukv.reshape(MLA_KV_RANK, MLA_HEADS, MLA_NOPE + MLA_V)
    kh = (MLA_KV_RANK, MLA_HEADS)
    kpart = cat([wkv3[:, :, :MLA_NOPE], zeros(kh, LANES - MLA_NOPE)])
    vpart = wkv3[:, :, MLA_NOPE:].astype(BF16)
    wkv = jnp.concatenate([kpart.reshape(MLA_KV_RANK, -1), vpart.reshape(MLA_KV_RANK, -1)], axis=1)
    return w1, wq, wkv


def _rope_inv_freq():
    inv_freq = ROPE_THETA ** (-jnp.arange(0, MLA_ROPE, 2, dtype=F32) / MLA_ROPE)
    return inv_freq[:, None]


def _pad_lanes(v, fill=0.0):
    return jnp.full((1, LANES), fill, F32).at[0, :v.shape[0]].set(v)


def kernel(x, positions, norm_mix_g, w_in, conv_w, conv_b, dt_bias, a_log, d_skip, ssd_norm_g, q_norm_g, w_uq, kv_norm_g, w_ukv, w_out, norm_ffn_g, w_router, b_router, w_gate_up, b_gate_up, w_down, b_down, norm_final_g):
    bsz, seqlen, d = x.shape
    t = bsz * seqlen
    depth = w_in.shape[0]
    x2 = x.reshape(t, d)
    pos_rows = positions.reshape(t // TM_PROJ, 1, TM_PROJ).astype(I32)
    invf = _rope_inv_freq()

    for l in range(depth):
        w1, wq, wkv = _prep_in_weights(w_in[l], w_uq[l], w_ukv[l])
        z, xbc, dtm, q, k, v = _inproj(
            x2, pos_rows, norm_mix_g[l][None, :], w1, q_norm_g[l][None, :], wq, kv_norm_g[l][None, :], wkv,
            invf, bsz, seqlen)
        y_ssd = _ssd(z, xbc, dtm, conv_w[l], conv_b[l][None, :], _pad_lanes(dt_bias[l]), _pad_lanes(a_log[l]),
                     jnp.repeat(d_skip[l], SSD_HEAD_DIM)[None, :], ssd_norm_g[l][None, :], bsz, seqlen)
        y_mla = _attention(q, k, v, bsz, seqlen).reshape(t, MLA_WIDTH)

        wr = jnp.zeros((d, LANES), F32).at[:, :N_EXPERTS].set(w_router[l])
        wr_hi = wr.astype(BF16)
        wr_lo = (wr - wr_hi.astype(F32)).astype(BF16)
        x1, h2p, route, cnt = _outproj(x2, y_ssd, y_mla, w_out[l], norm_ffn_g[l][None, :],
                                       jnp.concatenate([wr_hi, wr_lo], axis=1), _pad_lanes(b_router[l]))

        counts = cnt[:, 0].astype(I32)
        padded = ((counts + MOE_STEP - 1) // MOE_STEP) * MOE_STEP
        end_pad = jnp.cumsum(padded)
        start_pad = end_pad - padded
        n_slots = t * TOP_K + N_EXPERTS * MOE_STEP
        n_blocks = n_slots // MOE_STEP
        block_start = jnp.arange(n_blocks, dtype=I32) * MOE_STEP
        block_e = jnp.minimum(jnp.sum(block_start[:, None] >= end_pad[None, :], axis=1), N_EXPERTS - 1).astype(I32)
        eids = jnp.arange(N_EXPERTS, dtype=I32)
        block_hot = block_e[:, None] == eids[None, :]
        per_block = lambda table: jnp.sum(jnp.where(block_hot, table[None, :], 0), axis=1).astype(I32)
        block_valid = jnp.clip(per_block(counts) - (block_start - per_block(start_pad)), 0, MOE_STEP).astype(I32)
        used = counts > 0
        later_used = jnp.where((eids[None, :] > eids[:, None]) & used[None, :], eids[None, :], N_EXPERTS)
        next_used = jnp.min(later_used, axis=1)
        next_used = jnp.where(next_used < N_EXPERTS, next_used, -1).astype(I32)
        stage_slot = ((jnp.cumsum(used.astype(I32)) - 1) & 1).astype(I32)
        dest_km = _slot_plan(route, start_pad.astype(I32))

        xs = _sc_scatter_rows(h2p, dest_km, n_slots)
        n_used = end_pad[-1] // MOE_STEP
        block_src = jnp.minimum(jnp.arange(n_blocks, dtype=I32), n_used - 1).astype(I32)
        ys = _expert_ffn(block_e, block_valid, per_block(next_used), per_block(stage_slot), block_src, xs,
                         w_gate_up[l], b_gate_up[l][:, None, :], w_down[l], b_down[l][:, None, :])
        yg = _sc_gather_rows(ys, dest_km)
        x2 = _combine(x1, yg, route, norm_final_g[None, :], l == depth - 1)
    return x2.reshape(bsz, seqlen, d)
```

```python
import functools

import jax
import jax.numpy as jnp
import numpy as np
from jax import lax
from jax.experimental import pallas as pl
from jax.experimental.pallas import tpu as pltpu
from jax.experimental.pallas import tpu_sc as plsc

F32 = jnp.float32
BF16 = jnp.bfloat16
I32 = jnp.int32

D_MODEL = 1024
EPS = 1e-6
LANES = 128
V7X_VMEM_BYTES = 64 * 1024 * 1024
VMEM_LIMIT_BYTES = V7X_VMEM_BYTES * 7 // 8

SSD_HEADS = 8
SSD_HEAD_DIM = 64
SSD_WIDTH = 512
SSD_STATE = 128
SSD_CONV = 4
SSD_CHUNK = 128
SSD_CONV_DIM = 1024
CONV_PAD = 8
SSD_CPS = 8

MLA_HEADS = 8
MLA_Q_RANK = 256
MLA_KV_RANK = 128
MLA_NOPE = 64
MLA_ROPE = 32
MLA_V = 64
MLA_QK = MLA_NOPE + MLA_ROPE
MLA_WIDTH = 512
ROPE_THETA = 10000.0
LOG2_E = 1.4426950408889634

N_EXPERTS = 32
TOP_K = 4
D_FF = 1024
SWIGLU_LIMIT = 7.0
SWIGLU_ALPHA = 1.702

IN_W = 512 + 1024 + 256 + 128 + 128

TM_PROJ = 1024
TQ = 512
TK = 512
ATT_HPS = 4
VT_ROWS = 80
MOE_BM = 256
MOE_STEP = 1024
TM_COMB = 512
TM_PLAN = 2048
MOE_PARTS = 2
ROUTE_ROWS = 16

NT_DIMS = (((1,), (1,)), ((), ()))


def _rms(x):
    return x * lax.rsqrt(jnp.mean(x * x, axis=-1, keepdims=True) + EPS)


HALF = D_MODEL // 2
HI_MASK = np.int32(-65536)


def _pack_rows(a):
    lo = lax.bitcast_convert_type(a[:, :HALF].astype(BF16).astype(F32), I32)
    hi = lax.bitcast_convert_type(a[:, HALF:].astype(BF16).astype(F32), I32)
    return (hi & HI_MASK) | lax.shift_right_logical(lo, 16)


def _unpack_rows(p):
    lo = lax.bitcast_convert_type(lax.shift_left(p, 16), F32)
    hi = lax.bitcast_convert_type(p & HI_MASK, F32)
    return lo, hi


def _inproj_kernel(x_ref, pos_ref, g_ref, w1_ref, qg_ref, wq_ref, kvg_ref, wkv_ref, invf_ref,
                   z_ref, xbc_ref, dtm_ref, q_ref, k_ref, vt_ref):
    x = x_ref[...]
    h = (_rms(x) * g_ref[...]).astype(BF16)
    p = jnp.dot(h, w1_ref[...], preferred_element_type=F32)
    z_ref[...] = p[:, 0:512]
    xbc_ref[...] = p[:, 512:1536]
    cq = p[:, 1536:1792]
    ckv = p[:, 1792:1920]
    m1 = p[:, 1920:2048]
    dtm_ref[...] = m1

    lane = lax.broadcasted_iota(I32, (1, LANES), 1)
    tm = x.shape[0]
    ang = invf_ref[...] * pos_ref[0].astype(F32)
    cos_c = jnp.cos(ang)
    sin_c = jnp.sin(ang)
    z_lo = jnp.zeros((MLA_NOPE, tm), F32)
    z_hi = jnp.zeros((LANES - MLA_QK, tm), F32)
    cos_t = jnp.concatenate([z_lo, cos_c, cos_c, z_hi], axis=0).T
    sin_t = jnp.concatenate([z_lo, -sin_c, sin_c, z_hi], axis=0).T
    cosq_t = jnp.where(lane < MLA_NOPE, 1.0, cos_t)
    scale = MLA_QK ** -0.5 * LOG2_E

    cqn = (_rms(cq) * qg_ref[...]).astype(BF16)
    qq = jnp.dot(cqn, wq_ref[...], preferred_element_type=F32)
    ckvn = (_rms(ckv) * kvg_ref[...]).astype(BF16)
    kv = jnp.dot(ckvn, wkv_ref[...], preferred_element_type=F32)
    swap_down = LANES - MLA_ROPE
    krot = m1 * cos_t + pltpu.roll(m1, swap_down, axis=1) * sin_t
    for h_i in range(MLA_HEADS):
        lo = h_i * LANES
        qm = qq[:, lo:lo + LANES]
        qs = qq[:, 1024 + lo:1024 + lo + LANES]
        q_ref[0, h_i] = ((qm * cosq_t + qs * sin_t) * scale).astype(BF16)
        k_ref[0, h_i] = (kv[:, lo:lo + LANES] + krot).astype(BF16)
    ones_rows = jnp.ones((VT_ROWS - MLA_V, tm), BF16)
    for pr in range(MLA_HEADS // 2):
        vpt = kv[:, 1024 + pr * LANES:1024 + (pr + 1) * LANES].T.astype(BF16)
        vt_ref[0, 2 * pr] = jnp.concatenate([vpt[0:MLA_V, :], ones_rows], axis=0)
        vt_ref[0, 2 * pr + 1] = jnp.concatenate([vpt[MLA_V:2 * MLA_V, :], ones_rows], axis=0)


def _inproj(x2, pos_rows, g_mix, w1, qg, wq, kvg, wkv, invf, bsz, seqlen):
    t = x2.shape[0]
    tm = TM_PROJ
    per_b = seqlen // tm
    full = lambda shape: pl.BlockSpec(shape, lambda i: (0,) * len(shape))
    head_spec = pl.BlockSpec((1, MLA_HEADS, tm, LANES), lambda i: (i // per_b, 0, i % per_b, 0))
    head_shape = jax.ShapeDtypeStruct((bsz, MLA_HEADS, seqlen, LANES), BF16)
    vt_spec = pl.BlockSpec((1, MLA_HEADS, VT_ROWS, tm), lambda i: (i // per_b, 0, 0, i % per_b))
    vt_shape = jax.ShapeDtypeStruct((bsz, MLA_HEADS, VT_ROWS, seqlen), BF16)
    return pl.pallas_call(
        _inproj_kernel,
        grid=(t // tm,),
        in_specs=[
            pl.BlockSpec((tm, D_MODEL), lambda i: (i, 0)),
            pl.BlockSpec((1, 1, tm), lambda i: (i, 0, 0)),
            full((1, D_MODEL)), full((D_MODEL, IN_W)),
            full((1, MLA_Q_RANK)), full((MLA_Q_RANK, 2048)),
            full((1, MLA_KV_RANK)), full((MLA_KV_RANK, 1536)),
            full((MLA_ROPE // 2, 1)),
        ],
        out_specs=[
            pl.BlockSpec((tm, 512), lambda i: (i, 0)),
            pl.BlockSpec((tm, 1024), lambda i: (i, 0)),
            pl.BlockSpec((tm, LANES), lambda i: (i, 0)),
            head_spec, head_spec, vt_spec,
        ],
        out_shape=[
            jax.ShapeDtypeStruct((t, 512), F32),
            jax.ShapeDtypeStruct((t, 1024), F32),
            jax.ShapeDtypeStruct((t, LANES), F32),
            head_shape, head_shape, vt_shape,
        ],
        compiler_params=pltpu.CompilerParams(
            dimension_semantics=("arbitrary",), vmem_limit_bytes=VMEM_LIMIT_BYTES),
        name="inproj",
    )(x2, pos_rows, g_mix, w1, qg, wq, kvg, wkv, invf)


def _ssd_kernel(z_ref, xbc_ref, dtm_ref, cw_ref, cb_ref, dtb_ref, alog_ref, dsk_ref, ng_ref,
                y_ref, ext_ref, st_ref):
    q = SSD_CHUNK
    rows = SSD_CPS * q

    @pl.when(pl.program_id(1) == 0)
    def _():
        ext_ref[0:CONV_PAD, :] = jnp.zeros((CONV_PAD, SSD_CONV_DIM), F32)
        st_ref[...] = jnp.zeros_like(st_ref)

    ext_ref[CONV_PAD:CONV_PAD + rows, :] = xbc_ref[...]

    lane = lax.broadcasted_iota(I32, (1, LANES), 1)
    row = lax.broadcasted_iota(I32, (q, q), 0)
    col = lax.broadcasted_iota(I32, (q, q), 1)
    tril = row >= col
    tril_b = jnp.where(tril, 1.0, 0.0).astype(BF16)
    spread = jnp.where(
        lax.broadcasted_iota(I32, (LANES, SSD_WIDTH), 0)
        == lax.broadcasted_iota(I32, (LANES, SSD_WIDTH), 1) // SSD_HEAD_DIM, 1.0, 0.0).astype(BF16)
    a_neg = -jnp.exp(alog_ref[...]) * LOG2_E

    def split3(v):
        hi = v.astype(BF16)
        r1 = v - hi.astype(F32)
        mid = r1.astype(BF16)
        return hi, mid, (r1 - mid.astype(F32)).astype(BF16)

    def dot3_right(parts, m):
        return sum(jnp.dot(p, m, preferred_element_type=F32) for p in parts)

    def expand(cols):
        return dot3_right(split3(cols), spread)

    for ci in range(SSD_CPS):
        lo = ci * q
        window = ext_ref[lo:lo + CONV_PAD + q, :]
        conv = cb_ref[...] + cw_ref[SSD_CONV - 1:SSD_CONV, :] * window[CONV_PAD:, :]
        for kk in range(SSD_CONV - 1):
            shifted = pltpu.roll(window, SSD_CONV - 1 - kk, axis=0)
            conv = conv + cw_ref[kk:kk + 1, :] * shifted[CONV_PAD:, :]
        u = conv * jax.nn.sigmoid(conv)
        xs = u[:, 0:512]
        bm = u[:, 512:768]
        cm = u[:, 768:1024]

        xdt = dtm_ref[lo:lo + q, :] + dtb_ref[...]
        dt = jnp.maximum(xdt, 0.0) + jnp.log1p(jnp.exp(-jnp.abs(xdt)))
        adt = jnp.where(lane < SSD_HEADS, dt * a_neg, 0.0)
        cum_col = sum(jnp.dot(tril_b, p, preferred_element_type=F32) for p in split3(adt))
        cum_row = cum_col.T

        dt_e = expand(dt)
        ac_e = expand(cum_col)
        last_e = ac_e[q - 1:q, :]
        xd = xs * dt_e
        w_end = xd * jnp.exp2(last_e - ac_e)
        eac = jnp.exp2(ac_e)
        cdec = jnp.exp2(last_e)

        y_parts = []
        for g in range(2):
            gl = g * 256
            bg = bm[:, g * SSD_STATE:(g + 1) * SSD_STATE]
            cg = cm[:, g * SSD_STATE:(g + 1) * SSD_STATE].astype(BF16)
            scores = lax.dot_general(cg, bg.astype(BF16), NT_DIMS, preferred_element_type=F32)
            bgt = bg.T.astype(BF16)
            sprev = st_ref[g]
            yoff = jnp.dot(cg, sprev.astype(BF16), preferred_element_type=F32)
            st_ref[g] = sprev * cdec[:, gl:gl + 256] + jnp.dot(
                bgt, w_end[:, gl:gl + 256].astype(BF16), preferred_element_type=F32)
            for pr in range(2):
                pl_lo = gl + pr * LANES
                xdp = xd[:, pl_lo:pl_lo + LANES].astype(BF16)
                res = []
                for jj in range(2):
                    h_i = g * 4 + pr * 2 + jj
                    seg = cum_col[:, h_i:h_i + 1] - cum_row[h_i:h_i + 1, :]
                    dec = jnp.exp2(jnp.where(tril, seg, -jnp.inf))
                    res.append(jnp.dot((scores * dec).astype(BF16), xdp, preferred_element_type=F32))
                ydiag = jnp.where(lane < SSD_HEAD_DIM, res[0], res[1])
                y_parts.append(ydiag + yoff[:, pr * LANES:(pr + 1) * LANES] * eac[:, pl_lo:pl_lo + LANES])
        y = jnp.concatenate(y_parts, axis=1) + dsk_ref[...] * xs
        zz = z_ref[lo:lo + q, :]
        y = y * (zz * jax.nn.sigmoid(zz))
        outs = []
        for g in range(2):
            yg = y[:, g * 256:(g + 1) * 256]
            outs.append(_rms(yg))
        y_ref[lo:lo + q, :] = (jnp.concatenate(outs, axis=1) * ng_ref[...]).astype(BF16)

    ext_ref[0:CONV_PAD, :] = ext_ref[rows:rows + CONV_PAD, :]


def _ssd(z, xbc, dtm, cw, cb, dtb, alog, dsk, ng, bsz, seqlen):
    t = z.shape[0]
    q = SSD_CHUNK
    rows = SSD_CPS * q
    nc = seqlen // rows
    full = lambda shape: pl.BlockSpec(shape, lambda b, c: (0,) * len(shape))
    row_spec = lambda width: pl.BlockSpec((rows, width), lambda b, c: (b * nc + c, 0))
    return pl.pallas_call(
        _ssd_kernel,
        grid=(bsz, nc),
        in_specs=[row_spec(512), row_spec(1024), row_spec(LANES),
                  full((SSD_CONV, SSD_CONV_DIM)), full((1, SSD_CONV_DIM)),
                  full((1, LANES)), full((1, LANES)), full((1, SSD_WIDTH)), full((1, SSD_WIDTH))],
        out_specs=row_spec(512),
        out_shape=jax.ShapeDtypeStruct((t, SSD_WIDTH), BF16),
        scratch_shapes=[pltpu.VMEM((rows + CONV_PAD, SSD_CONV_DIM), F32),
                        pltpu.VMEM((2, SSD_STATE, 256), F32)],
        compiler_params=pltpu.CompilerParams(dimension_semantics=("arbitrary", "arbitrary")),
        name="ssd",
    )(z, xbc, dtm, cw, cb, dtb, alog, dsk, ng)


def _attn_kernel(q_ref, k_ref, vt_ref, o_ref, acc_ref, s_ref, bmax_ref):
    i = pl.program_id(2)
    acc_ref[...] = jnp.zeros_like(acc_ref)

    def col_max(st):
        part = st[0:LANES, :]
        for r0 in range(LANES, TK, LANES):
            part = jnp.maximum(part, st[r0:r0 + LANES, :])
        return jnp.broadcast_to(jnp.max(part, axis=0, keepdims=True), (8, TQ))

    def scores(j, slot):
        start = pl.multiple_of(j * TK, TK)
        for hh in range(ATT_HPS):
            kb = k_ref[0, hh, pl.ds(start, TK), :]
            st = lax.dot_general(kb, q_ref[0, hh], NT_DIMS, preferred_element_type=F32)
            s_ref[slot, hh] = st
            bmax_ref[slot, hh] = col_max(st)

    def consume(j, slot, m_all, masked):
        start = pl.multiple_of(j * TK, TK)
        new_m = []
        for hh in range(ATT_HPS):
            vt = vt_ref[0, hh, :, pl.ds(start, TK)]
            st = s_ref[slot, hh]
            if masked:
                key = lax.broadcasted_iota(I32, (TK, TQ), 0)
                qry = lax.broadcasted_iota(I32, (TK, TQ), 1)
                st = jnp.where(key <= qry, st, -jnp.inf)
                block_max = col_max(st)
            else:
                block_max = bmax_ref[slot, hh]
            m_old = m_all[hh]
            m_new = jnp.maximum(m_old, block_max)
            alpha = jnp.exp2(m_old - m_new)
            p = jnp.exp2(st - m_new[0:1, :]).astype(BF16)
            acc_ref[hh] = acc_ref[hh] * alpha[0:1, :] + jnp.dot(vt, p, preferred_element_type=F32)
            new_m.append(m_new)
        return tuple(new_m)

    def pair(p, m_all):
        scores(2 * p + 1, 1)
        m_all = consume(2 * p, 0, m_all, False)
        scores(2 * p + 2, 0)
        return consume(2 * p + 1, 1, m_all, False)

    def odd_tail(_, m_all):
        scores(i, 1)
        return consume(i - 1, 0, m_all, False)

    def quad(g, m_all):
        return pair(2 * g + 1, pair(2 * g, m_all))

    m0 = jnp.full((8, TQ), -jnp.inf, F32)
    scores(0, 0)
    m_all = lax.fori_loop(0, i // 4, quad, (m0,) * ATT_HPS)
    m_all = lax.fori_loop(2 * (i // 4), i // 2, pair, m_all)
    m_all = lax.fori_loop(0, i & 1, odd_tail, m_all)

    @pl.when((i & 1) == 0)
    def _():
        consume(i, 0, m_all, True)

    @pl.when((i & 1) == 1)
    def _():
        consume(i, 1, m_all, True)

    outs = []
    for pr in range(ATT_HPS // 2):
        pair_t = []
        for hh in (2 * pr, 2 * pr + 1):
            a = acc_ref[hh]
            pair_t.append(a[0:MLA_V, :] / a[MLA_V:MLA_V + 1, :])
        outs.append(jnp.concatenate(pair_t, axis=0).T)
    o_ref[0] = jnp.concatenate(outs, axis=1).astype(BF16)


def _attention(q, k, vt, bsz, seqlen):
    nq = seqlen // TQ
    hps = ATT_HPS
    return pl.pallas_call(
        _attn_kernel,
        grid=(bsz, MLA_HEADS // hps, nq),
        in_specs=[pl.BlockSpec((1, hps, TQ, LANES), lambda b, p, i: (b, p, i, 0)),
                  pl.BlockSpec((1, hps, seqlen, LANES), lambda b, p, i: (b, p, 0, 0)),
                  pl.BlockSpec((1, hps, VT_ROWS, seqlen), lambda b, p, i: (b, p, 0, 0))],
        out_specs=pl.BlockSpec((1, TQ, hps * MLA_V), lambda b, p, i: (b, i, p)),
        out_shape=jax.ShapeDtypeStruct((bsz, seqlen, MLA_WIDTH), BF16),
        scratch_shapes=[pltpu.VMEM((hps, VT_ROWS, TQ), F32), pltpu.VMEM((2, hps, TK, TQ), F32),
                        pltpu.VMEM((2, hps, 8, TQ), F32)],
        compiler_params=pltpu.CompilerParams(
            dimension_semantics=("arbitrary", "arbitrary", "arbitrary"),
            vmem_limit_bytes=VMEM_LIMIT_BYTES),
        name="attention",
    )(q, k, vt)


def _outproj_kernel(x_ref, ys_ref, ym_ref, wo_ref, g_ref, wr_ref, br_ref,
                    x1_ref, h2_ref, route_ref, cnt_ref, wo_bf, *, steps_per_part):
    tm = x_ref.shape[0]

    @pl.when(pl.program_id(0) == 0)
    def _():
        wo_bf[...] = wo_ref[...].astype(BF16)

    @pl.when(pl.program_id(0) % steps_per_part == 0)
    def _():
        cnt_ref[...] = jnp.zeros_like(cnt_ref)

    mix = (jnp.dot(ys_ref[...], wo_bf[0:512, :], preferred_element_type=F32)
           + jnp.dot(ym_ref[...], wo_bf[512:1024, :], preferred_element_type=F32))
    x1 = x_ref[...] + mix
    x1_ref[...] = x1
    h2 = _rms(x1) * g_ref[...]
    h2_ref[...] = _pack_rows(h2)

    h_hi = h2.astype(BF16)
    h_lo = (h2 - h_hi.astype(F32)).astype(BF16)
    hh = jnp.dot(h_hi, wr_ref[...], preferred_element_type=F32)
    lh = jnp.dot(h_lo, wr_ref[:, 0:LANES], preferred_element_type=F32)
    logits = hh[:, 0:LANES] + (hh[:, LANES:2 * LANES] + lh) + br_ref[...]
    lt = logits.T[0:N_EXPERTS, :]
    eid = lax.broadcasted_iota(I32, (N_EXPERTS, 1), 0).astype(F32)

    vals, idxs, hots = [], [], []
    for _ in range(TOP_K):
        mx = jnp.max(lt, axis=0, keepdims=True)
        idx = jnp.min(jnp.where(lt == mx, eid, float(N_EXPERTS)), axis=0, keepdims=True)
        hot = eid == idx
        lt = jnp.where(hot, -jnp.inf, lt)
        vals.append(mx)
        idxs.append(idx)
        hots.append(hot)
    exps = [jnp.exp(v - vals[0]) for v in vals]
    denom = exps[0] + exps[1] + exps[2] + exps[3]

    multi_f = jnp.where(hots[0] | hots[1] | hots[2] | hots[3], 1.0, 0.0)
    r = lax.broadcasted_iota(I32, (tm, tm), 0)
    c = lax.broadcasted_iota(I32, (tm, tm), 1)
    earlier = jnp.where(r < c, 1.0, 0.0).astype(BF16)
    before = jnp.dot(multi_f.astype(BF16), earlier, preferred_element_type=F32) + cnt_ref[0, :, 0:1]
    cnt_ref[0] = cnt_ref[0] + jnp.sum(multi_f, axis=1, keepdims=True)

    ranks = [jnp.sum(jnp.where(hots[kk], before, 0.0), axis=0, keepdims=True) for kk in range(TOP_K)]
    gates = [e / denom for e in exps]
    route_ref[...] = jnp.concatenate(idxs + gates + ranks + [jnp.zeros((ROUTE_ROWS - 3 * TOP_K, tm), F32)], axis=0)


def _outproj(x2, y_ssd, y_mla, wo, g_ffn, wr, br):
    t = x2.shape[0]
    tm = TM_PROJ
    full = lambda shape: pl.BlockSpec(shape, lambda i: (0,) * len(shape))
    rows = lambda width: pl.BlockSpec((tm, width), lambda i: (i, 0))
    steps_per_part = t // tm // MOE_PARTS
    return pl.pallas_call(
        functools.partial(_outproj_kernel, steps_per_part=steps_per_part),
        grid=(t // tm,),
        in_specs=[rows(D_MODEL), rows(512), rows(512), full((1024, D_MODEL)), full((1, D_MODEL)),
                  full((D_MODEL, 2 * LANES)), full((1, LANES))],
        out_specs=[rows(D_MODEL), rows(HALF), pl.BlockSpec((ROUTE_ROWS, tm), lambda i: (0, i)),
                   pl.BlockSpec((1, N_EXPERTS, LANES), lambda i: (i // steps_per_part, 0, 0))],
        out_shape=[jax.ShapeDtypeStruct((t, D_MODEL), F32),
                   jax.ShapeDtypeStruct((t, HALF), I32),
                   jax.ShapeDtypeStruct((ROUTE_ROWS, t), F32),
                   jax.ShapeDtypeStruct((MOE_PARTS, N_EXPERTS, LANES), F32)],
        scratch_shapes=[pltpu.VMEM((1024, D_MODEL), BF16)],
        compiler_params=pltpu.CompilerParams(
            dimension_semantics=("arbitrary",), vmem_limit_bytes=VMEM_LIMIT_BYTES),
        name="outproj_router",
    )(x2, y_ssd, y_mla, wo, g_ffn, wr, br)


def _ffn_kernel(be_ref, bv_ref, nx_ref, sl_ref, xb_ref, xs_ref, wgu_hbm, bgu_ref, wd_hbm, bd_ref, ys_ref,
                wgu_st, wd_st, wgu_bf, wd_bf, sem):
    i = pl.program_id(0)
    e = be_ref[i]
    valid = bv_ref[i]
    slot = sl_ref[i]
    first = ((i == 0) | (e != be_ref[jnp.maximum(i - 1, 0)])) & (valid > 0)

    def weight_copies(expert, dst_slot):
        return (pltpu.make_async_copy(wgu_hbm.at[expert], wgu_st.at[dst_slot], sem.at[0, dst_slot]),
                pltpu.make_async_copy(wd_hbm.at[expert], wd_st.at[dst_slot], sem.at[1, dst_slot]))

    @pl.when(i == 0)
    def _():
        for cp in weight_copies(e, slot):
            cp.start()

    @pl.when(first)
    def _():
        for cp in weight_copies(e, slot):
            cp.wait()

        @pl.when(nx_ref[i] >= 0)
        def _():
            for cp in weight_copies(nx_ref[i], 1 - slot):
                cp.start(priority=1)

        wgu_bf[...] = wgu_st[slot].astype(BF16)
        wd_bf[...] = wd_st[slot].astype(BF16)

    def chain(r0, n_rows, masked):
        rows = pl.ds(r0, n_rows)
        xp = xs_ref[rows, :]
        if masked:
            xp = jnp.where(r0 + lax.broadcasted_iota(I32, (n_rows, 1), 0) < valid, xp, 0)
        x_lo, x_hi = _unpack_rows(xp)
        gu = (jnp.dot(x_lo.astype(BF16), wgu_bf[0:HALF, :], preferred_element_type=F32)
              + jnp.dot(x_hi.astype(BF16), wgu_bf[HALF:D_MODEL, :], preferred_element_type=F32)
              + bgu_ref[0])
        gate = jnp.minimum(gu[:, :D_FF], SWIGLU_LIMIT)
        up = jnp.clip(gu[:, D_FF:], -SWIGLU_LIMIT, SWIGLU_LIMIT)
        glu = gate * jax.nn.sigmoid(SWIGLU_ALPHA * gate)
        mid = ((up + 1.0) * glu).astype(BF16)
        ys_ref[rows, :] = _pack_rows(jnp.dot(mid, wd_bf[...], preferred_element_type=F32) + bd_ref[0])

    @pl.when(valid == MOE_STEP)
    def _():
        for r0 in range(0, MOE_STEP, MOE_BM):
            chain(r0, MOE_BM, False)

    @pl.when(valid < MOE_STEP)
    def _():
        half_bm = MOE_BM // 2
        for r0 in range(0, MOE_STEP, MOE_BM):
            pl.when(valid > r0 + half_bm)(functools.partial(chain, r0, MOE_BM, True))
            pl.when((valid > r0) & (valid <= r0 + half_bm))(functools.partial(chain, r0, half_bm, True))

            @pl.when(valid <= r0 + half_bm)
            def _():
                ys_ref[pl.ds(r0 + half_bm, half_bm), :] = jnp.zeros((half_bm, HALF), I32)

            @pl.when(valid <= r0)
            def _():
                ys_ref[pl.ds(r0, half_bm), :] = jnp.zeros((half_bm, HALF), I32)


def _expert_ffn(block_e, block_valid, block_next, block_slot, block_src, xs, wgu, bgu, wd, bd):
    n_slots = xs.shape[0]
    bm = MOE_STEP
    grid_spec = pltpu.PrefetchScalarGridSpec(
        num_scalar_prefetch=5,
        grid=(n_slots // bm,),
        in_specs=[
            pl.BlockSpec((bm, HALF), lambda i, be, bv, nx, sl, xb: (xb[i], 0)),
            pl.BlockSpec(memory_space=pl.ANY),
            pl.BlockSpec((1, 1, 2 * D_FF), lambda i, be, bv, nx, sl, xb: (be[i], 0, 0)),
            pl.BlockSpec(memory_space=pl.ANY),
            pl.BlockSpec((1, 1, D_MODEL), lambda i, be, bv, nx, sl, xb: (be[i], 0, 0)),
        ],
        out_specs=pl.BlockSpec((bm, HALF), lambda i, be, bv, nx, sl, xb: (i, 0)),
        scratch_shapes=[pltpu.VMEM((2, D_MODEL, 2 * D_FF), F32), pltpu.VMEM((2, D_FF, D_MODEL), F32),
                        pltpu.VMEM((D_MODEL, 2 * D_FF), BF16), pltpu.VMEM((D_FF, D_MODEL), BF16),
                        pltpu.SemaphoreType.DMA((2, 2))],
    )
    return pl.pallas_call(
        _ffn_kernel,
        grid_spec=grid_spec,
        out_shape=jax.ShapeDtypeStruct((n_slots, HALF), I32),
        compiler_params=pltpu.CompilerParams(
            dimension_semantics=("arbitrary",), vmem_limit_bytes=VMEM_LIMIT_BYTES),
        name="expert_ffn",
    )(block_e, block_valid, block_next, block_slot, block_src, xs, wgu, bgu, wd, bd)


def _plan_kernel(sp_ref, route_ref, dest_ref):
    idx = route_ref[0:TOP_K, :]
    rank = route_ref[2 * TOP_K:3 * TOP_K, :]
    start = jnp.zeros(idx.shape, F32)
    for e_i in range(N_EXPERTS):
        start = jnp.where(idx == float(e_i), sp_ref[e_i].astype(F32), start)
    dest_ref[...] = (start + rank).astype(I32)


def _slot_plan(route, start_pad):
    t = route.shape[1]
    tm = TM_PLAN
    grid_spec = pltpu.PrefetchScalarGridSpec(
        num_scalar_prefetch=1,
        grid=(t // tm,),
        in_specs=[pl.BlockSpec((ROUTE_ROWS, tm), lambda i, sp: (0, i))],
        out_specs=pl.BlockSpec((TOP_K, tm), lambda i, sp: (0, i)),
    )
    return pl.pallas_call(
        _plan_kernel,
        grid_spec=grid_spec,
        out_shape=jax.ShapeDtypeStruct((TOP_K, t), I32),
        compiler_params=pltpu.CompilerParams(dimension_semantics=("arbitrary",)),
        name="slot_plan",
    )(start_pad, route)


SC_SCATTER_CHUNK = 64
SC_GATHER_CHUNK = 64


def _sc_workers():
    info = plsc.get_sparse_core_info()
    return info.num_cores, info.num_cores * info.num_subcores


def _sc_scatter_rows(rows, dest_km, n_out, row_off):
    t, w = dest_km.shape[1], rows.shape[1]
    ch = SC_SCATTER_CHUNK
    n_cores, n_workers = _sc_workers()
    n_chunks = t // n_workers // ch
    mesh = plsc.VectorSubcoreMesh(core_axis_name="c", subcore_axis_name="s")

    @functools.partial(
        pl.kernel, mesh=mesh, out_type=jax.ShapeDtypeStruct((n_out, w), rows.dtype),
        scratch_types=[pltpu.VMEM((TOP_K, n_chunks, ch), I32), pltpu.VMEM((2, ch, w), rows.dtype),
                       pltpu.SemaphoreType.DMA((2,)), pltpu.SemaphoreType.DMA((2,))],
        name="sc_dispatch_scatter")
    def scatter_kernel(rows_hbm, dest_hbm, out_hbm, idx_v, rows_v, sem_in, sem_out):
        wid = lax.axis_index("s") * n_cores + lax.axis_index("c")
        first = wid * n_chunks
        for kk in range(TOP_K):
            pltpu.sync_copy(dest_hbm.at[kk, pl.ds(first, n_chunks)], idx_v.at[kk])

        def load(cc, b):
            return pltpu.make_async_copy(
                rows_hbm.at[pl.ds(row_off + (first + cc) * ch, ch)], rows_v.at[b], sem_in.at[b])

        def scatters(cc, b):
            return [pltpu.make_async_copy(rows_v.at[b], out_hbm.at[idx_v.at[kk, cc]], sem_out.at[b])
                    for kk in range(TOP_K)]

        load(0, 0).start()
        load(1, 1).start()

        @pl.loop(0, n_chunks, step=2)
        def _(c):
            for b in range(2):
                load(c + b, b).wait()
                for cp in scatters(c + b, b):
                    cp.start()
            for b in range(2):
                for cp in scatters(c + b, b):
                    cp.wait()

                @pl.when(c + 2 + b < n_chunks)
                def _():
                    load(c + 2 + b, b).start()

    return scatter_kernel(rows, dest_km.reshape(TOP_K, t // ch, ch))


def _sc_gather_rows(table, dest_km):
    _, w = table.shape
    t = dest_km.shape[1]
    ch = SC_GATHER_CHUNK
    n_cores, n_workers = _sc_workers()
    per_k = t // n_workers // ch
    n_chunks = TOP_K * per_k
    mesh = plsc.VectorSubcoreMesh(core_axis_name="c", subcore_axis_name="s")

    @functools.partial(
        pl.kernel, mesh=mesh, out_type=jax.ShapeDtypeStruct((TOP_K, t, w), table.dtype),
        scratch_types=[pltpu.VMEM((TOP_K, per_k, ch), I32), pltpu.VMEM((2, ch, w), table.dtype),
                       pltpu.SemaphoreType.DMA((2,)), pltpu.SemaphoreType.DMA((2,))],
        name="sc_combine_gather")
    def gather_kernel(table_hbm, dest_hbm, out_hbm, idx_v, rows_v, sem_in, sem_out):
        wid = lax.axis_index("s") * n_cores + lax.axis_index("c")
        first = wid * per_k
        for kk in range(TOP_K):
            pltpu.sync_copy(dest_hbm.at[kk, pl.ds(first, per_k)], idx_v.at[kk])

        def gather(cc, b):
            return pltpu.make_async_copy(table_hbm.at[idx_v.at[cc // per_k, cc % per_k]], rows_v.at[b], sem_in.at[b])

        def store(cc, b):
            return pltpu.make_async_copy(
                rows_v.at[b], out_hbm.at[cc // per_k, pl.ds((first + cc % per_k) * ch, ch)], sem_out.at[b])

        gather(0, 0).start()
        gather(1, 1).start()

        @pl.loop(0, n_chunks, step=2)
        def _(c):
            for b in range(2):
                gather(c + b, b).wait()
                store(c + b, b).start()
            for b in range(2):
                store(c + b, b).wait()

                @pl.when(c + 2 + b < n_chunks)
                def _():
                    gather(c + 2 + b, b).start()

    return gather_kernel(table, dest_km.reshape(TOP_K, t // ch, ch))


def _combine_kernel(x1_ref, yg_ref, route_ref, g_ref, o_ref, *, final_norm):
    tm = x1_ref.shape[0]
    moe_lo = jnp.zeros((tm, HALF), F32)
    moe_hi = jnp.zeros((tm, HALF), F32)
    route_t = jnp.concatenate([route_ref[...], jnp.zeros((LANES - ROUTE_ROWS, tm), F32)], axis=0).T
    for kk in range(TOP_K):
        gate = route_t[:, TOP_K + kk:TOP_K + kk + 1]
        y_lo, y_hi = _unpack_rows(yg_ref[kk])
        moe_lo = moe_lo + gate * y_lo
        moe_hi = moe_hi + gate * y_hi
    acc = x1_ref[...] + jnp.concatenate([moe_lo, moe_hi], axis=1)
    o_ref[...] = _rms(acc) * g_ref[...] if final_norm else acc


def _combine(x1, yg, route, g_final, final_norm, part):
    t = x1.shape[0]
    tm = TM_COMB
    steps = yg.shape[1] // tm
    off = part * steps
    return pl.pallas_call(
        functools.partial(_combine_kernel, final_norm=final_norm),
        grid=(steps,),
        in_specs=[pl.BlockSpec((tm, D_MODEL), lambda i: (off + i, 0)),
                  pl.BlockSpec((TOP_K, tm, HALF), lambda i: (0, i, 0)),
                  pl.BlockSpec((ROUTE_ROWS, tm), lambda i: (0, off + i)),
                  pl.BlockSpec((1, D_MODEL), lambda i: (0, 0))],
        out_specs=pl.BlockSpec((tm, D_MODEL), lambda i: (off + i, 0)),
        out_shape=jax.ShapeDtypeStruct((t, D_MODEL), F32),
        input_output_aliases={0: 0},
        compiler_params=pltpu.CompilerParams(dimension_semantics=("arbitrary",)),
        name="combine",
    )(x1, yg, route, g_final)


def _prep_in_weights(w_in, w_uq, w_ukv):
    w_z = w_in[:, 0:512]
    w_xbc = w_in[:, 512:1536]
    w_dt = w_in[:, 1536:1544]
    w_cq = w_in[:, 1544:1800]
    w_ckv = w_in[:, 1800:1928]
    w_kr = w_in[:, 1928:1960]
    half = MLA_ROPE // 2
    zeros = lambda rows, width: jnp.zeros(rows + (width,), BF16)
    cat = lambda parts: jnp.concatenate([p.astype(BF16) for p in parts], axis=-1)
    d = (D_MODEL,)
    misc = [w_dt, zeros(d, MLA_NOPE - SSD_HEADS), w_kr, w_kr[:, half:], w_kr[:, :half]]
    w1 = cat([w_z, w_xbc, w_cq, w_ckv] + misc)

    wq3 = w_uq.reshape(MLA_Q_RANK, MLA_HEADS, MLA_QK)
    qh = (MLA_Q_RANK, MLA_HEADS)
    main = cat([wq3, zeros(qh, LANES - MLA_QK)])
    swap = cat([zeros(qh, MLA_NOPE), wq3[:, :, MLA_NOPE + half:], wq3[:, :, MLA_NOPE:MLA_NOPE + half],
                zeros(qh, LANES - MLA_QK)])
    wq = jnp.concatenate([main.reshape(MLA_Q_RANK, -1), swap.reshape(MLA_Q_RANK, -1)], axis=1)

    wkv3 = w_ukv.reshape(MLA_KV_RANK, MLA_HEADS, MLA_NOPE + MLA_V)
    kh = (MLA_KV_RANK, MLA_HEADS)
    kpart = cat([wkv3[:, :, :MLA_NOPE], zeros(kh, LANES - MLA_NOPE)])
    vpart = wkv3[:, :, MLA_NOPE:].astype(BF16)
    wkv = jnp.concatenate([kpart.reshape(MLA_KV_RANK, -1), vpart.reshape(MLA_KV_RANK, -1)], axis=1)
    return w1, wq, wkv


def _rope_inv_freq():
    inv_freq = ROPE_THETA ** (-jnp.arange(0, MLA_ROPE, 2, dtype=F32) / MLA_ROPE)
    return inv_freq[:, None]


def _pad_lanes(v, fill=0.0):
    return jnp.full((1, LANES), fill, F32).at[0, :v.shape[0]].set(v)


def kernel(x, positions, norm_mix_g, w_in, conv_w, conv_b, dt_bias, a_log, d_skip, ssd_norm_g, q_norm_g, w_uq, kv_norm_g, w_ukv, w_out, norm_ffn_g, w_router, b_router, w_gate_up, b_gate_up, w_down, b_down, norm_final_g):
    bsz, seqlen, d = x.shape
    t = bsz * seqlen
    depth = w_in.shape[0]
    x2 = x.reshape(t, d)
    pos_rows = positions.reshape(t // TM_PROJ, 1, TM_PROJ).astype(I32)
    invf = _rope_inv_freq()

    for l in range(depth):
        w1, wq, wkv = _prep_in_weights(w_in[l], w_uq[l], w_ukv[l])
        z, xbc, dtm, q, k, v = _inproj(
            x2, pos_rows, norm_mix_g[l][None, :], w1, q_norm_g[l][None, :], wq, kv_norm_g[l][None, :], wkv,
            invf, bsz, seqlen)
        y_ssd = _ssd(z, xbc, dtm, conv_w[l], conv_b[l][None, :], _pad_lanes(dt_bias[l]), _pad_lanes(a_log[l]),
                     jnp.repeat(d_skip[l], SSD_HEAD_DIM)[None, :], ssd_norm_g[l][None, :], bsz, seqlen)
        y_mla = _attention(q, k, v, bsz, seqlen).reshape(t, MLA_WIDTH)

        wr = jnp.zeros((d, LANES), F32).at[:, :N_EXPERTS].set(w_router[l])
        wr_hi = wr.astype(BF16)
        wr_lo = (wr - wr_hi.astype(F32)).astype(BF16)
        x1, h2p, route, cnt = _outproj(x2, y_ssd, y_mla, w_out[l], norm_ffn_g[l][None, :],
                                       jnp.concatenate([wr_hi, wr_lo], axis=1), _pad_lanes(b_router[l]))

        t_part = t // MOE_PARTS
        n_slots = t_part * TOP_K + N_EXPERTS * MOE_STEP
        n_blocks = n_slots // MOE_STEP
        block_start = jnp.arange(n_blocks, dtype=I32) * MOE_STEP
        eids = jnp.arange(N_EXPERTS, dtype=I32)
        plans = []
        for part in range(MOE_PARTS):
            counts = cnt[part, :, 0].astype(I32)
            padded = ((counts + MOE_STEP - 1) // MOE_STEP) * MOE_STEP
            end_pad = jnp.cumsum(padded)
            start_pad = end_pad - padded
            block_e = jnp.minimum(jnp.sum(block_start[:, None] >= end_pad[None, :], axis=1),
                                  N_EXPERTS - 1).astype(I32)
            block_hot = block_e[:, None] == eids[None, :]
            per_block = lambda table, hot=block_hot: jnp.sum(jnp.where(hot, table[None, :], 0), axis=1).astype(I32)
            block_valid = jnp.clip(per_block(counts) - (block_start - per_block(start_pad)), 0, MOE_STEP).astype(I32)
            used = counts > 0
            later_used = jnp.where((eids[None, :] > eids[:, None]) & used[None, :], eids[None, :], N_EXPERTS)
            next_used = jnp.min(later_used, axis=1)
            next_used = jnp.where(next_used < N_EXPERTS, next_used, -1).astype(I32)
            stage_slot = ((jnp.cumsum(used.astype(I32)) - 1) & 1).astype(I32)
            n_used = end_pad[-1] // MOE_STEP
            block_src = jnp.minimum(jnp.arange(n_blocks, dtype=I32), n_used - 1).astype(I32)
            route_part = route[:, part * t_part:(part + 1) * t_part]
            dest_km = _slot_plan(route_part, start_pad.astype(I32))
            xs = _sc_scatter_rows(h2p, dest_km, n_slots, part * t_part)
            plans.append((dest_km, xs, (block_e, block_valid, per_block(next_used), per_block(stage_slot), block_src)))

        x2 = x1
        for part, (dest_km, xs, tables) in enumerate(plans):
            ys = _expert_ffn(*tables, xs, w_gate_up[l], b_gate_up[l][:, None, :], w_down[l], b_down[l][:, None, :])
            yg = _sc_gather_rows(ys, dest_km)
            x2 = _combine(x2, yg, route, norm_final_g[None, :], l == depth - 1, part)
    return x2.reshape(bsz, seqlen, d)
```

```python
import functools

import jax
import jax.numpy as jnp
import numpy as np
from jax import lax
from jax.experimental import pallas as pl
from jax.experimental.pallas import tpu as pltpu
from jax.experimental.pallas import tpu_sc as plsc

F32 = jnp.float32
BF16 = jnp.bfloat16
I32 = jnp.int32

D_MODEL = 1024
EPS = 1e-6
LANES = 128
V7X_VMEM_BYTES = 64 * 1024 * 1024
VMEM_LIMIT_BYTES = V7X_VMEM_BYTES * 7 // 8

SSD_HEADS = 8
SSD_HEAD_DIM = 64
SSD_WIDTH = 512
SSD_STATE = 128
SSD_CONV = 4
SSD_CHUNK = 128
SSD_CONV_DIM = 1024
CONV_PAD = 8
SSD_CPS = 8

MLA_HEADS = 8
MLA_Q_RANK = 256
MLA_KV_RANK = 128
MLA_NOPE = 64
MLA_ROPE = 32
MLA_V = 64
MLA_QK = MLA_NOPE + MLA_ROPE
MLA_WIDTH = 512
ROPE_THETA = 10000.0
LOG2_E = 1.4426950408889634

N_EXPERTS = 32
TOP_K = 4
D_FF = 1024
SWIGLU_LIMIT = 7.0
SWIGLU_ALPHA = 1.702

IN_W = 512 + 1024 + 256 + 128 + 128

TM_PROJ = 1024
TQ = 512
TK = 512
ATT_HPS = 4
VT_ROWS = 80
MOE_BM = 256
MOE_STEP = 1024
TM_COMB = 512
TM_PLAN = 2048
ROUTE_ROWS = 16

NT_DIMS = (((1,), (1,)), ((), ()))


def _rms(x):
    return x * lax.rsqrt(jnp.mean(x * x, axis=-1, keepdims=True) + EPS)


HALF = D_MODEL // 2
HI_MASK = np.int32(-65536)


def _pack_rows(a):
    lo = lax.bitcast_convert_type(a[:, :HALF].astype(BF16).astype(F32), I32)
    hi = lax.bitcast_convert_type(a[:, HALF:].astype(BF16).astype(F32), I32)
    return (hi & HI_MASK) | lax.shift_right_logical(lo, 16)


def _unpack_rows(p):
    lo = lax.bitcast_convert_type(lax.shift_left(p, 16), F32)
    hi = lax.bitcast_convert_type(p & HI_MASK, F32)
    return lo, hi


def _inproj_kernel(x_ref, pos_ref, g_ref, w1_ref, qg_ref, wq_ref, kvg_ref, wkv_ref, invf_ref,
                   z_ref, xbc_ref, dtm_ref, q_ref, k_ref, vt_ref):
    x = x_ref[...]
    h = (_rms(x) * g_ref[...]).astype(BF16)
    p = jnp.dot(h, w1_ref[...], preferred_element_type=F32)
    z_ref[...] = p[:, 0:512]
    xbc_ref[...] = p[:, 512:1536]
    cq = p[:, 1536:1792]
    ckv = p[:, 1792:1920]
    m1 = p[:, 1920:2048]
    dtm_ref[...] = m1

    lane = lax.broadcasted_iota(I32, (1, LANES), 1)
    tm = x.shape[0]
    ang = invf_ref[...] * pos_ref[0].astype(F32)
    cos_c = jnp.cos(ang)
    sin_c = jnp.sin(ang)
    z_lo = jnp.zeros((MLA_NOPE, tm), F32)
    z_hi = jnp.zeros((LANES - MLA_QK, tm), F32)
    cos_t = jnp.concatenate([z_lo, cos_c, cos_c, z_hi], axis=0).T
    sin_t = jnp.concatenate([z_lo, -sin_c, sin_c, z_hi], axis=0).T
    cosq_t = jnp.where(lane < MLA_NOPE, 1.0, cos_t)
    scale = MLA_QK ** -0.5 * LOG2_E

    cqn = (_rms(cq) * qg_ref[...]).astype(BF16)
    qq = jnp.dot(cqn, wq_ref[...], preferred_element_type=F32)
    ckvn = (_rms(ckv) * kvg_ref[...]).astype(BF16)
    kv = jnp.dot(ckvn, wkv_ref[...], preferred_element_type=F32)
    swap_down = LANES - MLA_ROPE
    krot = m1 * cos_t + pltpu.roll(m1, swap_down, axis=1) * sin_t
    for h_i in range(MLA_HEADS):
        lo = h_i * LANES
        qm = qq[:, lo:lo + LANES]
        qs = qq[:, 1024 + lo:1024 + lo + LANES]
        q_ref[0, h_i] = ((qm * cosq_t + qs * sin_t) * scale).astype(BF16)
        k_ref[0, h_i] = (kv[:, lo:lo + LANES] + krot).astype(BF16)
    ones_rows = jnp.ones((VT_ROWS - MLA_V, tm), BF16)
    for pr in range(MLA_HEADS // 2):
        vpt = kv[:, 1024 + pr * LANES:1024 + (pr + 1) * LANES].T.astype(BF16)
        vt_ref[0, 2 * pr] = jnp.concatenate([vpt[0:MLA_V, :], ones_rows], axis=0)
        vt_ref[0, 2 * pr + 1] = jnp.concatenate([vpt[MLA_V:2 * MLA_V, :], ones_rows], axis=0)


def _inproj(x2, pos_rows, g_mix, w1, qg, wq, kvg, wkv, invf, bsz, seqlen):
    t = x2.shape[0]
    tm = TM_PROJ
    per_b = seqlen // tm
    full = lambda shape: pl.BlockSpec(shape, lambda i: (0,) * len(shape))
    head_spec = pl.BlockSpec((1, MLA_HEADS, tm, LANES), lambda i: (i // per_b, 0, i % per_b, 0))
    head_shape = jax.ShapeDtypeStruct((bsz, MLA_HEADS, seqlen, LANES), BF16)
    vt_spec = pl.BlockSpec((1, MLA_HEADS, VT_ROWS, tm), lambda i: (i // per_b, 0, 0, i % per_b))
    vt_shape = jax.ShapeDtypeStruct((bsz, MLA_HEADS, VT_ROWS, seqlen), BF16)
    return pl.pallas_call(
        _inproj_kernel,
        grid=(t // tm,),
        in_specs=[
            pl.BlockSpec((tm, D_MODEL), lambda i: (i, 0)),
            pl.BlockSpec((1, 1, tm), lambda i: (i, 0, 0)),
            full((1, D_MODEL)), full((D_MODEL, IN_W)),
            full((1, MLA_Q_RANK)), full((MLA_Q_RANK, 2048)),
            full((1, MLA_KV_RANK)), full((MLA_KV_RANK, 1536)),
            full((MLA_ROPE // 2, 1)),
        ],
        out_specs=[
            pl.BlockSpec((tm, 512), lambda i: (i, 0)),
            pl.BlockSpec((tm, 1024), lambda i: (i, 0)),
            pl.BlockSpec((tm, LANES), lambda i: (i, 0)),
            head_spec, head_spec, vt_spec,
        ],
        out_shape=[
            jax.ShapeDtypeStruct((t, 512), F32),
            jax.ShapeDtypeStruct((t, 1024), F32),
            jax.ShapeDtypeStruct((t, LANES), F32),
            head_shape, head_shape, vt_shape,
        ],
        compiler_params=pltpu.CompilerParams(
            dimension_semantics=("arbitrary",), vmem_limit_bytes=VMEM_LIMIT_BYTES),
        name="inproj",
    )(x2, pos_rows, g_mix, w1, qg, wq, kvg, wkv, invf)


def _ssd_kernel(z_ref, xbc_ref, dtm_ref, cw_ref, cb_ref, dtb_ref, alog_ref, dsk_ref, ng_ref,
                y_ref, ext_ref, st_ref):
    q = SSD_CHUNK
    rows = SSD_CPS * q

    @pl.when(pl.program_id(1) == 0)
    def _():
        ext_ref[0:CONV_PAD, :] = jnp.zeros((CONV_PAD, SSD_CONV_DIM), F32)
        st_ref[...] = jnp.zeros_like(st_ref)

    ext_ref[CONV_PAD:CONV_PAD + rows, :] = xbc_ref[...]

    lane = lax.broadcasted_iota(I32, (1, LANES), 1)
    row = lax.broadcasted_iota(I32, (q, q), 0)
    col = lax.broadcasted_iota(I32, (q, q), 1)
    tril = row >= col
    tril_b = jnp.where(tril, 1.0, 0.0).astype(BF16)
    spread = jnp.where(
        lax.broadcasted_iota(I32, (LANES, SSD_WIDTH), 0)
        == lax.broadcasted_iota(I32, (LANES, SSD_WIDTH), 1) // SSD_HEAD_DIM, 1.0, 0.0).astype(BF16)
    a_neg = -jnp.exp(alog_ref[...]) * LOG2_E

    def split3(v):
        hi = v.astype(BF16)
        r1 = v - hi.astype(F32)
        mid = r1.astype(BF16)
        return hi, mid, (r1 - mid.astype(F32)).astype(BF16)

    def dot3_right(parts, m):
        return sum(jnp.dot(p, m, preferred_element_type=F32) for p in parts)

    def expand(cols):
        return dot3_right(split3(cols), spread)

    for ci in range(SSD_CPS):
        lo = ci * q
        window = ext_ref[lo:lo + CONV_PAD + q, :]
        conv = cb_ref[...] + cw_ref[SSD_CONV - 1:SSD_CONV, :] * window[CONV_PAD:, :]
        for kk in range(SSD_CONV - 1):
            shifted = pltpu.roll(window, SSD_CONV - 1 - kk, axis=0)
            conv = conv + cw_ref[kk:kk + 1, :] * shifted[CONV_PAD:, :]
        u = conv * jax.nn.sigmoid(conv)
        xs = u[:, 0:512]
        bm = u[:, 512:768]
        cm = u[:, 768:1024]

        xdt = dtm_ref[lo:lo + q, :] + dtb_ref[...]
        dt = jnp.maximum(xdt, 0.0) + jnp.log1p(jnp.exp(-jnp.abs(xdt)))
        adt = jnp.where(lane < SSD_HEADS, dt * a_neg, 0.0)
        cum_col = sum(jnp.dot(tril_b, p, preferred_element_type=F32) for p in split3(adt))
        cum_row = cum_col.T

        dt_e = expand(dt)
        ac_e = expand(cum_col)
        last_e = ac_e[q - 1:q, :]
        xd = xs * dt_e
        w_end = xd * jnp.exp2(last_e - ac_e)
        eac = jnp.exp2(ac_e)
        cdec = jnp.exp2(last_e)

        y_parts = []
        for g in range(2):
            gl = g * 256
            bg = bm[:, g * SSD_STATE:(g + 1) * SSD_STATE]
            cg = cm[:, g * SSD_STATE:(g + 1) * SSD_STATE].astype(BF16)
            scores = lax.dot_general(cg, bg.astype(BF16), NT_DIMS, preferred_element_type=F32)
            bgt = bg.T.astype(BF16)
            sprev = st_ref[g]
            yoff = jnp.dot(cg, sprev.astype(BF16), preferred_element_type=F32)
            st_ref[g] = sprev * cdec[:, gl:gl + 256] + jnp.dot(
                bgt, w_end[:, gl:gl + 256].astype(BF16), preferred_element_type=F32)
            for pr in range(2):
                pl_lo = gl + pr * LANES
                xdp = xd[:, pl_lo:pl_lo + LANES].astype(BF16)
                res = []
                for jj in range(2):
                    h_i = g * 4 + pr * 2 + jj
                    seg = cum_col[:, h_i:h_i + 1] - cum_row[h_i:h_i + 1, :]
                    dec = jnp.exp2(jnp.where(tril, seg, -jnp.inf))
                    res.append(jnp.dot((scores * dec).astype(BF16), xdp, preferred_element_type=F32))
                ydiag = jnp.where(lane < SSD_HEAD_DIM, res[0], res[1])
                y_parts.append(ydiag + yoff[:, pr * LANES:(pr + 1) * LANES] * eac[:, pl_lo:pl_lo + LANES])
        y = jnp.concatenate(y_parts, axis=1) + dsk_ref[...] * xs
        zz = z_ref[lo:lo + q, :]
        y = y * (zz * jax.nn.sigmoid(zz))
        outs = []
        for g in range(2):
            yg = y[:, g * 256:(g + 1) * 256]
            outs.append(_rms(yg))
        y_ref[lo:lo + q, :] = (jnp.concatenate(outs, axis=1) * ng_ref[...]).astype(BF16)

    ext_ref[0:CONV_PAD, :] = ext_ref[rows:rows + CONV_PAD, :]


def _ssd(z, xbc, dtm, cw, cb, dtb, alog, dsk, ng, bsz, seqlen):
    t = z.shape[0]
    q = SSD_CHUNK
    rows = SSD_CPS * q
    nc = seqlen // rows
    full = lambda shape: pl.BlockSpec(shape, lambda b, c: (0,) * len(shape))
    row_spec = lambda width: pl.BlockSpec((rows, width), lambda b, c: (b * nc + c, 0))
    return pl.pallas_call(
        _ssd_kernel,
        grid=(bsz, nc),
        in_specs=[row_spec(512), row_spec(1024), row_spec(LANES),
                  full((SSD_CONV, SSD_CONV_DIM)), full((1, SSD_CONV_DIM)),
                  full((1, LANES)), full((1, LANES)), full((1, SSD_WIDTH)), full((1, SSD_WIDTH))],
        out_specs=row_spec(512),
        out_shape=jax.ShapeDtypeStruct((t, SSD_WIDTH), BF16),
        scratch_shapes=[pltpu.VMEM((rows + CONV_PAD, SSD_CONV_DIM), F32),
                        pltpu.VMEM((2, SSD_STATE, 256), F32)],
        compiler_params=pltpu.CompilerParams(dimension_semantics=("arbitrary", "arbitrary")),
        name="ssd",
    )(z, xbc, dtm, cw, cb, dtb, alog, dsk, ng)


def _attn_kernel(q_ref, k_ref, vt_ref, o_ref, acc_ref, s_ref, bmax_ref):
    i = pl.program_id(2)
    acc_ref[...] = jnp.zeros_like(acc_ref)

    def col_max(st):
        part = st[0:LANES, :]
        for r0 in range(LANES, TK, LANES):
            part = jnp.maximum(part, st[r0:r0 + LANES, :])
        return jnp.broadcast_to(jnp.max(part, axis=0, keepdims=True), (8, TQ))

    def scores(j, slot):
        start = pl.multiple_of(j * TK, TK)
        for hh in range(ATT_HPS):
            kb = k_ref[0, hh, pl.ds(start, TK), :]
            st = lax.dot_general(kb, q_ref[0, hh], NT_DIMS, preferred_element_type=F32)
            s_ref[slot, hh] = st
            bmax_ref[slot, hh] = col_max(st)

    def consume(j, slot, m_all, masked):
        start = pl.multiple_of(j * TK, TK)
        new_m = []
        for hh in range(ATT_HPS):
            vt = vt_ref[0, hh, :, pl.ds(start, TK)]
            st = s_ref[slot, hh]
            if masked:
                key = lax.broadcasted_iota(I32, (TK, TQ), 0)
                qry = lax.broadcasted_iota(I32, (TK, TQ), 1)
                st = jnp.where(key <= qry, st, -jnp.inf)
                block_max = col_max(st)
            else:
                block_max = bmax_ref[slot, hh]
            m_old = m_all[hh]
            m_new = jnp.maximum(m_old, block_max)
            alpha = jnp.exp2(m_old - m_new)
            p = jnp.exp2(st - m_new[0:1, :]).astype(BF16)
            acc_ref[hh] = acc_ref[hh] * alpha[0:1, :] + jnp.dot(vt, p, preferred_element_type=F32)
            new_m.append(m_new)
        return tuple(new_m)

    def pair(p, m_all):
        scores(2 * p + 1, 1)
        m_all = consume(2 * p, 0, m_all, False)
        scores(2 * p + 2, 0)
        return consume(2 * p + 1, 1, m_all, False)

    def odd_tail(_, m_all):
        scores(i, 1)
        return consume(i - 1, 0, m_all, False)

    def quad(g, m_all):
        return pair(2 * g + 1, pair(2 * g, m_all))

    m0 = jnp.full((8, TQ), -jnp.inf, F32)
    scores(0, 0)
    m_all = lax.fori_loop(0, i // 4, quad, (m0,) * ATT_HPS)
    m_all = lax.fori_loop(2 * (i // 4), i // 2, pair, m_all)
    m_all = lax.fori_loop(0, i & 1, odd_tail, m_all)

    @pl.when((i & 1) == 0)
    def _():
        consume(i, 0, m_all, True)

    @pl.when((i & 1) == 1)
    def _():
        consume(i, 1, m_all, True)

    outs = []
    for pr in range(ATT_HPS // 2):
        pair_t = []
        for hh in (2 * pr, 2 * pr + 1):
            a = acc_ref[hh]
            pair_t.append(a[0:MLA_V, :] / a[MLA_V:MLA_V + 1, :])
        outs.append(jnp.concatenate(pair_t, axis=0).T)
    o_ref[0] = jnp.concatenate(outs, axis=1).astype(BF16)


def _attention(q, k, vt, bsz, seqlen):
    nq = seqlen // TQ
    hps = ATT_HPS
    return pl.pallas_call(
        _attn_kernel,
        grid=(bsz, MLA_HEADS // hps, nq),
        in_specs=[pl.BlockSpec((1, hps, TQ, LANES), lambda b, p, i: (b, p, i, 0)),
                  pl.BlockSpec((1, hps, seqlen, LANES), lambda b, p, i: (b, p, 0, 0)),
                  pl.BlockSpec((1, hps, VT_ROWS, seqlen), lambda b, p, i: (b, p, 0, 0))],
        out_specs=pl.BlockSpec((1, TQ, hps * MLA_V), lambda b, p, i: (b, i, p)),
        out_shape=jax.ShapeDtypeStruct((bsz, seqlen, MLA_WIDTH), BF16),
        scratch_shapes=[pltpu.VMEM((hps, VT_ROWS, TQ), F32), pltpu.VMEM((2, hps, TK, TQ), F32),
                        pltpu.VMEM((2, hps, 8, TQ), F32)],
        compiler_params=pltpu.CompilerParams(
            dimension_semantics=("arbitrary", "arbitrary", "arbitrary"),
            vmem_limit_bytes=VMEM_LIMIT_BYTES),
        name="attention",
    )(q, k, vt)


def _outproj_kernel(x_ref, ys_ref, ym_ref, wo_ref, g_ref, wr_ref, br_ref,
                    x1_ref, h2_ref, route_ref, cnt_ref, wo_bf):
    tm = x_ref.shape[0]

    @pl.when(pl.program_id(0) == 0)
    def _():
        cnt_ref[...] = jnp.zeros_like(cnt_ref)
        wo_bf[...] = wo_ref[...].astype(BF16)

    mix = (jnp.dot(ys_ref[...], wo_bf[0:512, :], preferred_element_type=F32)
           + jnp.dot(ym_ref[...], wo_bf[512:1024, :], preferred_element_type=F32))
    x1 = x_ref[...] + mix
    x1_ref[...] = x1
    h2 = _rms(x1) * g_ref[...]
    h2_ref[...] = _pack_rows(h2)

    h_hi = h2.astype(BF16)
    h_lo = (h2 - h_hi.astype(F32)).astype(BF16)
    hh = jnp.dot(h_hi, wr_ref[...], preferred_element_type=F32)
    lh = jnp.dot(h_lo, wr_ref[:, 0:LANES], preferred_element_type=F32)
    logits = hh[:, 0:LANES] + (hh[:, LANES:2 * LANES] + lh) + br_ref[...]
    lt = logits.T[0:N_EXPERTS, :]
    eid = lax.broadcasted_iota(I32, (N_EXPERTS, 1), 0).astype(F32)

    vals, idxs, hots = [], [], []
    for _ in range(TOP_K):
        mx = jnp.max(lt, axis=0, keepdims=True)
        idx = jnp.min(jnp.where(lt == mx, eid, float(N_EXPERTS)), axis=0, keepdims=True)
        hot = eid == idx
        lt = jnp.where(hot, -jnp.inf, lt)
        vals.append(mx)
        idxs.append(idx)
        hots.append(hot)
    exps = [jnp.exp(v - vals[0]) for v in vals]
    denom = exps[0] + exps[1] + exps[2] + exps[3]

    multi_f = jnp.where(hots[0] | hots[1] | hots[2] | hots[3], 1.0, 0.0)
    r = lax.broadcasted_iota(I32, (tm, tm), 0)
    c = lax.broadcasted_iota(I32, (tm, tm), 1)
    earlier = jnp.where(r < c, 1.0, 0.0).astype(BF16)
    before = jnp.dot(multi_f.astype(BF16), earlier, preferred_element_type=F32) + cnt_ref[:, 0:1]
    cnt_ref[...] = cnt_ref[...] + jnp.sum(multi_f, axis=1, keepdims=True)

    ranks = [jnp.sum(jnp.where(hots[kk], before, 0.0), axis=0, keepdims=True) for kk in range(TOP_K)]
    gates = [e / denom for e in exps]
    route_ref[...] = jnp.concatenate(idxs + gates + ranks + [jnp.zeros((ROUTE_ROWS - 3 * TOP_K, tm), F32)], axis=0)


def _outproj(x2, y_ssd, y_mla, wo, g_ffn, wr, br):
    t = x2.shape[0]
    tm = TM_PROJ
    full = lambda shape: pl.BlockSpec(shape, lambda i: (0,) * len(shape))
    rows = lambda width: pl.BlockSpec((tm, width), lambda i: (i, 0))
    return pl.pallas_call(
        _outproj_kernel,
        grid=(t // tm,),
        in_specs=[rows(D_MODEL), rows(512), rows(512), full((1024, D_MODEL)), full((1, D_MODEL)),
                  full((D_MODEL, 2 * LANES)), full((1, LANES))],
        out_specs=[rows(D_MODEL), rows(HALF), pl.BlockSpec((ROUTE_ROWS, tm), lambda i: (0, i)),
                   full((N_EXPERTS, LANES))],
        out_shape=[jax.ShapeDtypeStruct((t, D_MODEL), F32),
                   jax.ShapeDtypeStruct((t, HALF), I32),
                   jax.ShapeDtypeStruct((ROUTE_ROWS, t), F32),
                   jax.ShapeDtypeStruct((N_EXPERTS, LANES), F32)],
        scratch_shapes=[pltpu.VMEM((1024, D_MODEL), BF16)],
        compiler_params=pltpu.CompilerParams(
            dimension_semantics=("arbitrary",), vmem_limit_bytes=VMEM_LIMIT_BYTES),
        name="outproj_router",
    )(x2, y_ssd, y_mla, wo, g_ffn, wr, br)


def _ffn_kernel(be_ref, bv_ref, nx_ref, sl_ref, xb_ref, xs_ref, wgu_hbm, bgu_ref, wd_hbm, bd_ref, ys_ref,
                wgu_st, wd_st, wgu_bf, wd_bf, sem):
    i = pl.program_id(0)
    e = be_ref[i]
    valid = bv_ref[i]
    slot = sl_ref[i]
    first = ((i == 0) | (e != be_ref[jnp.maximum(i - 1, 0)])) & (valid > 0)

    def weight_copies(expert, dst_slot):
        return (pltpu.make_async_copy(wgu_hbm.at[expert], wgu_st.at[dst_slot], sem.at[0, dst_slot]),
                pltpu.make_async_copy(wd_hbm.at[expert], wd_st.at[dst_slot], sem.at[1, dst_slot]))

    @pl.when(i == 0)
    def _():
        for cp in weight_copies(e, slot):
            cp.start()

    @pl.when(first)
    def _():
        for cp in weight_copies(e, slot):
            cp.wait()

        @pl.when(nx_ref[i] >= 0)
        def _():
            for cp in weight_copies(nx_ref[i], 1 - slot):
                cp.start(priority=1)

        wgu_bf[...] = wgu_st[slot].astype(BF16)
        wd_bf[...] = wd_st[slot].astype(BF16)

    def chain(r0, n_rows, masked):
        rows = pl.ds(r0, n_rows)
        xp = xs_ref[rows, :]
        if masked:
            xp = jnp.where(r0 + lax.broadcasted_iota(I32, (n_rows, 1), 0) < valid, xp, 0)
        x_lo, x_hi = _unpack_rows(xp)
        gu = (jnp.dot(x_lo.astype(BF16), wgu_bf[0:HALF, :], preferred_element_type=F32)
              + jnp.dot(x_hi.astype(BF16), wgu_bf[HALF:D_MODEL, :], preferred_element_type=F32)
              + bgu_ref[0])
        gate = jnp.minimum(gu[:, :D_FF], SWIGLU_LIMIT)
        up = jnp.clip(gu[:, D_FF:], -SWIGLU_LIMIT, SWIGLU_LIMIT)
        glu = gate * jax.nn.sigmoid(SWIGLU_ALPHA * gate)
        mid = ((up + 1.0) * glu).astype(BF16)
        ys_ref[rows, :] = _pack_rows(jnp.dot(mid, wd_bf[...], preferred_element_type=F32) + bd_ref[0])

    @pl.when(valid == MOE_STEP)
    def _():
        for r0 in range(0, MOE_STEP, MOE_BM):
            chain(r0, MOE_BM, False)

    @pl.when((valid > 0) & (valid < MOE_STEP))
    def _():
        half_bm = MOE_BM // 2
        for r0 in range(0, MOE_STEP, MOE_BM):
            pl.when(valid > r0 + half_bm)(functools.partial(chain, r0, MOE_BM, True))
            pl.when((valid > r0) & (valid <= r0 + half_bm))(functools.partial(chain, r0, half_bm, True))

            @pl.when(valid <= r0 + half_bm)
            def _():
                ys_ref[pl.ds(r0 + half_bm, half_bm), :] = jnp.zeros((half_bm, HALF), I32)

            @pl.when(valid <= r0)
            def _():
                ys_ref[pl.ds(r0, half_bm), :] = jnp.zeros((half_bm, HALF), I32)


def _expert_ffn(block_e, block_valid, block_next, block_slot, block_src, xs, wgu, bgu, wd, bd):
    n_slots = xs.shape[0]
    bm = MOE_STEP
    grid_spec = pltpu.PrefetchScalarGridSpec(
        num_scalar_prefetch=5,
        grid=(n_slots // bm,),
        in_specs=[
            pl.BlockSpec((bm, HALF), lambda i, be, bv, nx, sl, xb: (xb[i], 0)),
            pl.BlockSpec(memory_space=pl.ANY),
            pl.BlockSpec((1, 1, 2 * D_FF), lambda i, be, bv, nx, sl, xb: (be[i], 0, 0)),
            pl.BlockSpec(memory_space=pl.ANY),
            pl.BlockSpec((1, 1, D_MODEL), lambda i, be, bv, nx, sl, xb: (be[i], 0, 0)),
        ],
        out_specs=pl.BlockSpec((bm, HALF), lambda i, be, bv, nx, sl, xb: (xb[i], 0)),
        scratch_shapes=[pltpu.VMEM((2, D_MODEL, 2 * D_FF), F32), pltpu.VMEM((2, D_FF, D_MODEL), F32),
                        pltpu.VMEM((D_MODEL, 2 * D_FF), BF16), pltpu.VMEM((D_FF, D_MODEL), BF16),
                        pltpu.SemaphoreType.DMA((2, 2))],
    )
    return pl.pallas_call(
        _ffn_kernel,
        grid_spec=grid_spec,
        out_shape=jax.ShapeDtypeStruct((n_slots, HALF), I32),
        input_output_aliases={5: 0},
        compiler_params=pltpu.CompilerParams(
            dimension_semantics=("arbitrary",), vmem_limit_bytes=VMEM_LIMIT_BYTES),
        name="expert_ffn",
    )(block_e, block_valid, block_next, block_slot, block_src, xs, wgu, bgu, wd, bd)


def _plan_kernel(sp_ref, route_ref, dest_ref):
    idx = route_ref[0:TOP_K, :]
    rank = route_ref[2 * TOP_K:3 * TOP_K, :]
    start = jnp.zeros(idx.shape, F32)
    for e_i in range(N_EXPERTS):
        start = jnp.where(idx == float(e_i), sp_ref[e_i].astype(F32), start)
    dest_ref[...] = (start + rank).astype(I32)


def _slot_plan(route, start_pad):
    t = route.shape[1]
    tm = TM_PLAN
    grid_spec = pltpu.PrefetchScalarGridSpec(
        num_scalar_prefetch=1,
        grid=(t // tm,),
        in_specs=[pl.BlockSpec((ROUTE_ROWS, tm), lambda i, sp: (0, i))],
        out_specs=pl.BlockSpec((TOP_K, tm), lambda i, sp: (0, i)),
    )
    return pl.pallas_call(
        _plan_kernel,
        grid_spec=grid_spec,
        out_shape=jax.ShapeDtypeStruct((TOP_K, t), I32),
        compiler_params=pltpu.CompilerParams(dimension_semantics=("arbitrary",)),
        name="slot_plan",
    )(start_pad, route)


SC_SCATTER_CHUNK = 64
SC_GATHER_CHUNK = 64


def _sc_workers():
    info = plsc.get_sparse_core_info()
    return info.num_cores, info.num_cores * info.num_subcores


def _sc_scatter_rows(rows, dest_km, n_out):
    t, w = rows.shape
    ch = SC_SCATTER_CHUNK
    n_cores, n_workers = _sc_workers()
    n_chunks = t // n_workers // ch
    mesh = plsc.VectorSubcoreMesh(core_axis_name="c", subcore_axis_name="s")

    @functools.partial(
        pl.kernel, mesh=mesh, out_type=jax.ShapeDtypeStruct((n_out, w), rows.dtype),
        scratch_types=[pltpu.VMEM((TOP_K, n_chunks, ch), I32), pltpu.VMEM((2, ch, w), rows.dtype),
                       pltpu.SemaphoreType.DMA((2,)), pltpu.SemaphoreType.DMA((2,))],
        name="sc_dispatch_scatter")
    def scatter_kernel(rows_hbm, dest_hbm, out_hbm, idx_v, rows_v, sem_in, sem_out):
        wid = lax.axis_index("s") * n_cores + lax.axis_index("c")
        first = wid * n_chunks
        for kk in range(TOP_K):
            pltpu.sync_copy(dest_hbm.at[kk, pl.ds(first, n_chunks)], idx_v.at[kk])

        def load(cc, b):
            return pltpu.make_async_copy(rows_hbm.at[pl.ds((first + cc) * ch, ch)], rows_v.at[b], sem_in.at[b])

        def scatters(cc, b):
            return [pltpu.make_async_copy(rows_v.at[b], out_hbm.at[idx_v.at[kk, cc]], sem_out.at[b])
                    for kk in range(TOP_K)]

        load(0, 0).start()
        load(1, 1).start()

        @pl.loop(0, n_chunks, step=2)
        def _(c):
            for b in range(2):
                load(c + b, b).wait()
                for cp in scatters(c + b, b):
                    cp.start()
            for b in range(2):
                for cp in scatters(c + b, b):
                    cp.wait()

                @pl.when(c + 2 + b < n_chunks)
                def _():
                    load(c + 2 + b, b).start()

    return scatter_kernel(rows, dest_km.reshape(TOP_K, t // ch, ch))


def _sc_gather_rows(table, dest_km):
    _, w = table.shape
    t = dest_km.shape[1]
    ch = SC_GATHER_CHUNK
    n_cores, n_workers = _sc_workers()
    per_k = t // n_workers // ch
    n_chunks = TOP_K * per_k
    mesh = plsc.VectorSubcoreMesh(core_axis_name="c", subcore_axis_name="s")

    @functools.partial(
        pl.kernel, mesh=mesh, out_type=jax.ShapeDtypeStruct((TOP_K, t, w), table.dtype),
        scratch_types=[pltpu.VMEM((TOP_K, per_k, ch), I32), pltpu.VMEM((2, ch, w), table.dtype),
                       pltpu.SemaphoreType.DMA((2,)), pltpu.SemaphoreType.DMA((2,))],
        name="sc_combine_gather")
    def gather_kernel(table_hbm, dest_hbm, out_hbm, idx_v, rows_v, sem_in, sem_out):
        wid = lax.axis_index("s") * n_cores + lax.axis_index("c")
        first = wid * per_k
        for kk in range(TOP_K):
            pltpu.sync_copy(dest_hbm.at[kk, pl.ds(first, per_k)], idx_v.at[kk])

        def gather(cc, b):
            return pltpu.make_async_copy(table_hbm.at[idx_v.at[cc // per_k, cc % per_k]], rows_v.at[b], sem_in.at[b])

        def store(cc, b):
            return pltpu.make_async_copy(
                rows_v.at[b], out_hbm.at[cc // per_k, pl.ds((first + cc % per_k) * ch, ch)], sem_out.at[b])

        gather(0, 0).start()
        gather(1, 1).start()

        @pl.loop(0, n_chunks, step=2)
        def _(c):
            for b in range(2):
                gather(c + b, b).wait()
                store(c + b, b).start()
            for b in range(2):
                store(c + b, b).wait()

                @pl.when(c + 2 + b < n_chunks)
                def _():
                    gather(c + 2 + b, b).start()

    return gather_kernel(table, dest_km.reshape(TOP_K, t // ch, ch))


def _combine_kernel(x1_ref, yg_ref, route_ref, g_ref, o_ref, *, final_norm):
    tm = x1_ref.shape[0]
    moe_lo = jnp.zeros((tm, HALF), F32)
    moe_hi = jnp.zeros((tm, HALF), F32)
    route_t = jnp.concatenate([route_ref[...], jnp.zeros((LANES - ROUTE_ROWS, tm), F32)], axis=0).T
    for kk in range(TOP_K):
        gate = route_t[:, TOP_K + kk:TOP_K + kk + 1]
        y_lo, y_hi = _unpack_rows(yg_ref[kk])
        moe_lo = moe_lo + gate * y_lo
        moe_hi = moe_hi + gate * y_hi
    acc = x1_ref[...] + jnp.concatenate([moe_lo, moe_hi], axis=1)
    o_ref[...] = _rms(acc) * g_ref[...] if final_norm else acc


def _combine(x1, yg, route, g_final, final_norm):
    t = x1.shape[0]
    tm = TM_COMB
    return pl.pallas_call(
        functools.partial(_combine_kernel, final_norm=final_norm),
        grid=(t // tm,),
        in_specs=[pl.BlockSpec((tm, D_MODEL), lambda i: (i, 0)),
                  pl.BlockSpec((TOP_K, tm, HALF), lambda i: (0, i, 0)),
                  pl.BlockSpec((ROUTE_ROWS, tm), lambda i: (0, i)),
                  pl.BlockSpec((1, D_MODEL), lambda i: (0, 0))],
        out_specs=pl.BlockSpec((tm, D_MODEL), lambda i: (i, 0)),
        out_shape=jax.ShapeDtypeStruct((t, D_MODEL), F32),
        compiler_params=pltpu.CompilerParams(dimension_semantics=("arbitrary",)),
        name="combine",
    )(x1, yg, route, g_final)


def _prep_in_weights(w_in, w_uq, w_ukv):
    w_z = w_in[:, 0:512]
    w_xbc = w_in[:, 512:1536]
    w_dt = w_in[:, 1536:1544]
    w_cq = w_in[:, 1544:1800]
    w_ckv = w_in[:, 1800:1928]
    w_kr = w_in[:, 1928:1960]
    half = MLA_ROPE // 2
    zeros = lambda rows, width: jnp.zeros(rows + (width,), BF16)
    cat = lambda parts: jnp.concatenate([p.astype(BF16) for p in parts], axis=-1)
    d = (D_MODEL,)
    misc = [w_dt, zeros(d, MLA_NOPE - SSD_HEADS), w_kr, w_kr[:, half:], w_kr[:, :half]]
    w1 = cat([w_z, w_xbc, w_cq, w_ckv] + misc)

    wq3 = w_uq.reshape(MLA_Q_RANK, MLA_HEADS, MLA_QK)
    qh = (MLA_Q_RANK, MLA_HEADS)
    main = cat([wq3, zeros(qh, LANES - MLA_QK)])
    swap = cat([zeros(qh, MLA_NOPE), wq3[:, :, MLA_NOPE + half:], wq3[:, :, MLA_NOPE:MLA_NOPE + half],
                zeros(qh, LANES - MLA_QK)])
    wq = jnp.concatenate([main.reshape(MLA_Q_RANK, -1), swap.reshape(MLA_Q_RANK, -1)], axis=1)

    wkv3 = w_ukv.reshape(MLA_KV_RANK, MLA_HEADS, MLA_NOPE + MLA_V)
    kh = (MLA_KV_RANK, MLA_HEADS)
    kpart = cat([wkv3[:, :, :MLA_NOPE], zeros(kh, LANES - MLA_NOPE)])
    vpart = wkv3[:, :, MLA_NOPE:].astype(BF16)
    wkv = jnp.concatenate([kpart.reshape(MLA_KV_RANK, -1), vpart.reshape(MLA_KV_RANK, -1)], axis=1)
    return w1, wq, wkv


def _rope_inv_freq():
    inv_freq = ROPE_THETA ** (-jnp.arange(0, MLA_ROPE, 2, dtype=F32) / MLA_ROPE)
    return inv_freq[:, None]


def _pad_lanes(v, fill=0.0):
    return jnp.full((1, LANES), fill, F32).at[0, :v.shape[0]].set(v)


def kernel(x, positions, norm_mix_g, w_in, conv_w, conv_b, dt_bias, a_log, d_skip, ssd_norm_g, q_norm_g, w_uq, kv_norm_g, w_ukv, w_out, norm_ffn_g, w_router, b_router, w_gate_up, b_gate_up, w_down, b_down, norm_final_g):
    bsz, seqlen, d = x.shape
    t = bsz * seqlen
    depth = w_in.shape[0]
    x2 = x.reshape(t, d)
    pos_rows = positions.reshape(t // TM_PROJ, 1, TM_PROJ).astype(I32)
    invf = _rope_inv_freq()

    for l in range(depth):
        w1, wq, wkv = _prep_in_weights(w_in[l], w_uq[l], w_ukv[l])
        z, xbc, dtm, q, k, v = _inproj(
            x2, pos_rows, norm_mix_g[l][None, :], w1, q_norm_g[l][None, :], wq, kv_norm_g[l][None, :], wkv,
            invf, bsz, seqlen)
        y_ssd = _ssd(z, xbc, dtm, conv_w[l], conv_b[l][None, :], _pad_lanes(dt_bias[l]), _pad_lanes(a_log[l]),
                     jnp.repeat(d_skip[l], SSD_HEAD_DIM)[None, :], ssd_norm_g[l][None, :], bsz, seqlen)
        y_mla = _attention(q, k, v, bsz, seqlen).reshape(t, MLA_WIDTH)

        wr = jnp.zeros((d, LANES), F32).at[:, :N_EXPERTS].set(w_router[l])
        wr_hi = wr.astype(BF16)
        wr_lo = (wr - wr_hi.astype(F32)).astype(BF16)
        x1, h2p, route, cnt = _outproj(x2, y_ssd, y_mla, w_out[l], norm_ffn_g[l][None, :],
                                       jnp.concatenate([wr_hi, wr_lo], axis=1), _pad_lanes(b_router[l]))

        counts = cnt[:, 0].astype(I32)
        padded = ((counts + MOE_STEP - 1) // MOE_STEP) * MOE_STEP
        end_pad = jnp.cumsum(padded)
        start_pad = end_pad - padded
        n_slots = t * TOP_K + N_EXPERTS * MOE_STEP
        n_blocks = n_slots // MOE_STEP
        block_start = jnp.arange(n_blocks, dtype=I32) * MOE_STEP
        block_e = jnp.minimum(jnp.sum(block_start[:, None] >= end_pad[None, :], axis=1), N_EXPERTS - 1).astype(I32)
        eids = jnp.arange(N_EXPERTS, dtype=I32)
        block_hot = block_e[:, None] == eids[None, :]
        per_block = lambda table: jnp.sum(jnp.where(block_hot, table[None, :], 0), axis=1).astype(I32)
        block_valid = jnp.clip(per_block(counts) - (block_start - per_block(start_pad)), 0, MOE_STEP).astype(I32)
        used = counts > 0
        later_used = jnp.where((eids[None, :] > eids[:, None]) & used[None, :], eids[None, :], N_EXPERTS)
        next_used = jnp.min(later_used, axis=1)
        next_used = jnp.where(next_used < N_EXPERTS, next_used, -1).astype(I32)
        stage_slot = ((jnp.cumsum(used.astype(I32)) - 1) & 1).astype(I32)
        dest_km = _slot_plan(route, start_pad.astype(I32))

        xs = _sc_scatter_rows(h2p, dest_km, n_slots)
        n_used = end_pad[-1] // MOE_STEP
        block_src = jnp.minimum(jnp.arange(n_blocks, dtype=I32), n_used - 1).astype(I32)
        ys = _expert_ffn(block_e, block_valid, per_block(next_used), per_block(stage_slot), block_src, xs,
                         w_gate_up[l], b_gate_up[l][:, None, :], w_down[l], b_down[l][:, None, :])
        yg = _sc_gather_rows(ys, dest_km)
        x2 = _combine(x1, yg, route, norm_final_g[None, :], l == depth - 1)
    return x2.reshape(bsz, seqlen, d)
```

```python
import functools

import jax
import jax.numpy as jnp
import numpy as np
from jax import lax
from jax.experimental import pallas as pl
from jax.experimental.pallas import tpu as pltpu
from jax.experimental.pallas import tpu_sc as plsc

F32 = jnp.float32
BF16 = jnp.bfloat16
I32 = jnp.int32

D_MODEL = 1024
EPS = 1e-6
LANES = 128
V7X_VMEM_BYTES = 64 * 1024 * 1024
VMEM_LIMIT_BYTES = V7X_VMEM_BYTES * 7 // 8

SSD_HEADS = 8
SSD_HEAD_DIM = 64
SSD_WIDTH = 512
SSD_STATE = 128
SSD_CONV = 4
SSD_CHUNK = 128
SSD_CONV_DIM = 1024
CONV_PAD = 8
SSD_CPS = 8

MLA_HEADS = 8
MLA_Q_RANK = 256
MLA_KV_RANK = 128
MLA_NOPE = 64
MLA_ROPE = 32
MLA_V = 64
MLA_QK = MLA_NOPE + MLA_ROPE
MLA_WIDTH = 512
ROPE_THETA = 10000.0
LOG2_E = 1.4426950408889634

N_EXPERTS = 32
TOP_K = 4
D_FF = 1024
SWIGLU_LIMIT = 7.0
SWIGLU_ALPHA = 1.702

IN_W = 512 + 1024 + 256 + 128 + 128

TM_PROJ = 1024
TQ = 512
TK = 512
ATT_HPS = 4
VT_ROWS = 80
MOE_BM = 256
MOE_STEP = 1024
TM_COMB = 1024
TM_PLAN = 2048
ROUTE_ROWS = 16

NT_DIMS = (((1,), (1,)), ((), ()))


def _rms(x):
    return x * lax.rsqrt(jnp.mean(x * x, axis=-1, keepdims=True) + EPS)


HALF = D_MODEL // 2
HI_MASK = np.int32(-65536)


def _pack_rows(a):
    lo = lax.bitcast_convert_type(a[:, :HALF].astype(BF16).astype(F32), I32)
    hi = lax.bitcast_convert_type(a[:, HALF:].astype(BF16).astype(F32), I32)
    return (hi & HI_MASK) | lax.shift_right_logical(lo, 16)


def _unpack_rows(p):
    lo = lax.bitcast_convert_type(lax.shift_left(p, 16), F32)
    hi = lax.bitcast_convert_type(p & HI_MASK, F32)
    return lo, hi


def _inproj_kernel(x_ref, pos_ref, g_ref, w1_ref, qg_ref, wq_ref, kvg_ref, wkv_ref, invf_ref,
                   z_ref, xbc_ref, dtm_ref, q_ref, k_ref, vt_ref):
    x = x_ref[...]
    h = (_rms(x) * g_ref[...]).astype(BF16)
    p = jnp.dot(h, w1_ref[...], preferred_element_type=F32)
    z_ref[...] = p[:, 0:512]
    xbc_ref[...] = p[:, 512:1536]
    cq = p[:, 1536:1792]
    ckv = p[:, 1792:1920]
    m1 = p[:, 1920:2048]
    dtm_ref[...] = m1

    lane = lax.broadcasted_iota(I32, (1, LANES), 1)
    tm = x.shape[0]
    ang = invf_ref[...] * pos_ref[0].astype(F32)
    cos_c = jnp.cos(ang)
    sin_c = jnp.sin(ang)
    z_lo = jnp.zeros((MLA_NOPE, tm), F32)
    z_hi = jnp.zeros((LANES - MLA_QK, tm), F32)
    cos_t = jnp.concatenate([z_lo, cos_c, cos_c, z_hi], axis=0).T
    sin_t = jnp.concatenate([z_lo, -sin_c, sin_c, z_hi], axis=0).T
    cosq_t = jnp.where(lane < MLA_NOPE, 1.0, cos_t)
    scale = MLA_QK ** -0.5 * LOG2_E

    cqn = (_rms(cq) * qg_ref[...]).astype(BF16)
    qq = jnp.dot(cqn, wq_ref[...], preferred_element_type=F32)
    ckvn = (_rms(ckv) * kvg_ref[...]).astype(BF16)
    kv = jnp.dot(ckvn, wkv_ref[...], preferred_element_type=F32)
    swap_down = LANES - MLA_ROPE
    krot = m1 * cos_t + pltpu.roll(m1, swap_down, axis=1) * sin_t
    for h_i in range(MLA_HEADS):
        lo = h_i * LANES
        qm = qq[:, lo:lo + LANES]
        qs = qq[:, 1024 + lo:1024 + lo + LANES]
        q_ref[0, h_i] = ((qm * cosq_t + qs * sin_t) * scale).astype(BF16)
        k_ref[0, h_i] = (kv[:, lo:lo + LANES] + krot).astype(BF16)
    ones_rows = jnp.ones((VT_ROWS - MLA_V, tm), BF16)
    for pr in range(MLA_HEADS // 2):
        vpt = kv[:, 1024 + pr * LANES:1024 + (pr + 1) * LANES].T.astype(BF16)
        vt_ref[0, 2 * pr] = jnp.concatenate([vpt[0:MLA_V, :], ones_rows], axis=0)
        vt_ref[0, 2 * pr + 1] = jnp.concatenate([vpt[MLA_V:2 * MLA_V, :], ones_rows], axis=0)


def _inproj(x2, pos_rows, g_mix, w1, qg, wq, kvg, wkv, invf, bsz, seqlen):
    t = x2.shape[0]
    tm = TM_PROJ
    per_b = seqlen // tm
    full = lambda shape: pl.BlockSpec(shape, lambda i: (0,) * len(shape))
    head_spec = pl.BlockSpec((1, MLA_HEADS, tm, LANES), lambda i: (i // per_b, 0, i % per_b, 0))
    head_shape = jax.ShapeDtypeStruct((bsz, MLA_HEADS, seqlen, LANES), BF16)
    vt_spec = pl.BlockSpec((1, MLA_HEADS, VT_ROWS, tm), lambda i: (i // per_b, 0, 0, i % per_b))
    vt_shape = jax.ShapeDtypeStruct((bsz, MLA_HEADS, VT_ROWS, seqlen), BF16)
    return pl.pallas_call(
        _inproj_kernel,
        grid=(t // tm,),
        in_specs=[
            pl.BlockSpec((tm, D_MODEL), lambda i: (i, 0)),
            pl.BlockSpec((1, 1, tm), lambda i: (i, 0, 0)),
            full((1, D_MODEL)), full((D_MODEL, IN_W)),
            full((1, MLA_Q_RANK)), full((MLA_Q_RANK, 2048)),
            full((1, MLA_KV_RANK)), full((MLA_KV_RANK, 1536)),
            full((MLA_ROPE // 2, 1)),
        ],
        out_specs=[
            pl.BlockSpec((tm, 512), lambda i: (i, 0)),
            pl.BlockSpec((tm, 1024), lambda i: (i, 0)),
            pl.BlockSpec((tm, LANES), lambda i: (i, 0)),
            head_spec, head_spec, vt_spec,
        ],
        out_shape=[
            jax.ShapeDtypeStruct((t, 512), F32),
            jax.ShapeDtypeStruct((t, 1024), F32),
            jax.ShapeDtypeStruct((t, LANES), F32),
            head_shape, head_shape, vt_shape,
        ],
        compiler_params=pltpu.CompilerParams(
            dimension_semantics=("arbitrary",), vmem_limit_bytes=VMEM_LIMIT_BYTES),
        name="inproj",
    )(x2, pos_rows, g_mix, w1, qg, wq, kvg, wkv, invf)


def _ssd_kernel(z_ref, xbc_ref, dtm_ref, cw_ref, cb_ref, dtb_ref, alog_ref, dsk_ref, ng_ref,
                y_ref, ext_ref, st_ref):
    q = SSD_CHUNK
    rows = SSD_CPS * q

    @pl.when(pl.program_id(1) == 0)
    def _():
        ext_ref[0:CONV_PAD, :] = jnp.zeros((CONV_PAD, SSD_CONV_DIM), F32)
        st_ref[...] = jnp.zeros_like(st_ref)

    ext_ref[CONV_PAD:CONV_PAD + rows, :] = xbc_ref[...]

    lane = lax.broadcasted_iota(I32, (1, LANES), 1)
    row = lax.broadcasted_iota(I32, (q, q), 0)
    col = lax.broadcasted_iota(I32, (q, q), 1)
    tril = row >= col
    tril_b = jnp.where(tril, 1.0, 0.0).astype(BF16)
    spread = jnp.where(
        lax.broadcasted_iota(I32, (LANES, SSD_WIDTH), 0)
        == lax.broadcasted_iota(I32, (LANES, SSD_WIDTH), 1) // SSD_HEAD_DIM, 1.0, 0.0).astype(BF16)
    a_neg = -jnp.exp(alog_ref[...]) * LOG2_E

    def split3(v):
        hi = v.astype(BF16)
        r1 = v - hi.astype(F32)
        mid = r1.astype(BF16)
        return hi, mid, (r1 - mid.astype(F32)).astype(BF16)

    def dot3_right(parts, m):
        return sum(jnp.dot(p, m, preferred_element_type=F32) for p in parts)

    def expand(cols):
        return dot3_right(split3(cols), spread)

    for ci in range(SSD_CPS):
        lo = ci * q
        window = ext_ref[lo:lo + CONV_PAD + q, :]
        conv = cb_ref[...] + cw_ref[SSD_CONV - 1:SSD_CONV, :] * window[CONV_PAD:, :]
        for kk in range(SSD_CONV - 1):
            shifted = pltpu.roll(window, SSD_CONV - 1 - kk, axis=0)
            conv = conv + cw_ref[kk:kk + 1, :] * shifted[CONV_PAD:, :]
        u = conv * jax.nn.sigmoid(conv)
        xs = u[:, 0:512]
        bm = u[:, 512:768]
        cm = u[:, 768:1024]

        xdt = dtm_ref[lo:lo + q, :] + dtb_ref[...]
        dt = jnp.maximum(xdt, 0.0) + jnp.log1p(jnp.exp(-jnp.abs(xdt)))
        adt = jnp.where(lane < SSD_HEADS, dt * a_neg, 0.0)
        cum_col = sum(jnp.dot(tril_b, p, preferred_element_type=F32) for p in split3(adt))
        cum_row = cum_col.T

        dt_e = expand(dt)
        ac_e = expand(cum_col)
        last_e = ac_e[q - 1:q, :]
        xd = xs * dt_e
        w_end = xd * jnp.exp2(last_e - ac_e)
        eac = jnp.exp2(ac_e)
        cdec = jnp.exp2(last_e)

        y_parts = []
        for g in range(2):
            gl = g * 256
            bg = bm[:, g * SSD_STATE:(g + 1) * SSD_STATE]
            cg = cm[:, g * SSD_STATE:(g + 1) * SSD_STATE].astype(BF16)
            scores = lax.dot_general(cg, bg.astype(BF16), NT_DIMS, preferred_element_type=F32)
            bgt = bg.T.astype(BF16)
            sprev = st_ref[g]
            yoff = jnp.dot(cg, sprev.astype(BF16), preferred_element_type=F32)
            st_ref[g] = sprev * cdec[:, gl:gl + 256] + jnp.dot(
                bgt, w_end[:, gl:gl + 256].astype(BF16), preferred_element_type=F32)
            for pr in range(2):
                pl_lo = gl + pr * LANES
                xdp = xd[:, pl_lo:pl_lo + LANES].astype(BF16)
                res = []
                for jj in range(2):
                    h_i = g * 4 + pr * 2 + jj
                    seg = cum_col[:, h_i:h_i + 1] - cum_row[h_i:h_i + 1, :]
                    dec = jnp.exp2(jnp.where(tril, seg, -jnp.inf))
                    res.append(jnp.dot((scores * dec).astype(BF16), xdp, preferred_element_type=F32))
                ydiag = jnp.where(lane < SSD_HEAD_DIM, res[0], res[1])
                y_parts.append(ydiag + yoff[:, pr * LANES:(pr + 1) * LANES] * eac[:, pl_lo:pl_lo + LANES])
        y = jnp.concatenate(y_parts, axis=1) + dsk_ref[...] * xs
        zz = z_ref[lo:lo + q, :]
        y = y * (zz * jax.nn.sigmoid(zz))
        outs = []
        for g in range(2):
            yg = y[:, g * 256:(g + 1) * 256]
            outs.append(_rms(yg))
        y_ref[lo:lo + q, :] = (jnp.concatenate(outs, axis=1) * ng_ref[...]).astype(BF16)

    ext_ref[0:CONV_PAD, :] = ext_ref[rows:rows + CONV_PAD, :]


def _ssd(z, xbc, dtm, cw, cb, dtb, alog, dsk, ng, bsz, seqlen):
    t = z.shape[0]
    q = SSD_CHUNK
    rows = SSD_CPS * q
    nc = seqlen // rows
    full = lambda shape: pl.BlockSpec(shape, lambda b, c: (0,) * len(shape))
    row_spec = lambda width: pl.BlockSpec((rows, width), lambda b, c: (b * nc + c, 0))
    return pl.pallas_call(
        _ssd_kernel,
        grid=(bsz, nc),
        in_specs=[row_spec(512), row_spec(1024), row_spec(LANES),
                  full((SSD_CONV, SSD_CONV_DIM)), full((1, SSD_CONV_DIM)),
                  full((1, LANES)), full((1, LANES)), full((1, SSD_WIDTH)), full((1, SSD_WIDTH))],
        out_specs=row_spec(512),
        out_shape=jax.ShapeDtypeStruct((t, SSD_WIDTH), BF16),
        scratch_shapes=[pltpu.VMEM((rows + CONV_PAD, SSD_CONV_DIM), F32),
                        pltpu.VMEM((2, SSD_STATE, 256), F32)],
        compiler_params=pltpu.CompilerParams(dimension_semantics=("arbitrary", "arbitrary")),
        name="ssd",
    )(z, xbc, dtm, cw, cb, dtb, alog, dsk, ng)


def _attn_kernel(q_ref, k_ref, vt_ref, o_ref, acc_ref, s_ref, bmax_ref):
    i = pl.program_id(2)
    acc_ref[...] = jnp.zeros_like(acc_ref)

    def col_max(st):
        part = st[0:LANES, :]
        for r0 in range(LANES, TK, LANES):
            part = jnp.maximum(part, st[r0:r0 + LANES, :])
        return jnp.broadcast_to(jnp.max(part, axis=0, keepdims=True), (8, TQ))

    def scores(j, slot):
        start = pl.multiple_of(j * TK, TK)
        for hh in range(ATT_HPS):
            kb = k_ref[0, hh, pl.ds(start, TK), :]
            st = lax.dot_general(kb, q_ref[0, hh], NT_DIMS, preferred_element_type=F32)
            s_ref[slot, hh] = st
            bmax_ref[slot, hh] = col_max(st)

    def consume(j, slot, m_all, masked):
        start = pl.multiple_of(j * TK, TK)
        new_m = []
        for hh in range(ATT_HPS):
            vt = vt_ref[0, hh, :, pl.ds(start, TK)]
            st = s_ref[slot, hh]
            if masked:
                key = lax.broadcasted_iota(I32, (TK, TQ), 0)
                qry = lax.broadcasted_iota(I32, (TK, TQ), 1)
                st = jnp.where(key <= qry, st, -jnp.inf)
                block_max = col_max(st)
            else:
                block_max = bmax_ref[slot, hh]
            m_old = m_all[hh]
            m_new = jnp.maximum(m_old, block_max)
            alpha = jnp.exp2(m_old - m_new)
            p = jnp.exp2(st - m_new[0:1, :]).astype(BF16)
            acc_ref[hh] = acc_ref[hh] * alpha[0:1, :] + jnp.dot(vt, p, preferred_element_type=F32)
            new_m.append(m_new)
        return tuple(new_m)

    def pair(p, m_all):
        scores(2 * p + 1, 1)
        m_all = consume(2 * p, 0, m_all, False)
        scores(2 * p + 2, 0)
        return consume(2 * p + 1, 1, m_all, False)

    def odd_tail(_, m_all):
        scores(i, 1)
        return consume(i - 1, 0, m_all, False)

    def quad(g, m_all):
        return pair(2 * g + 1, pair(2 * g, m_all))

    m0 = jnp.full((8, TQ), -jnp.inf, F32)
    scores(0, 0)
    m_all = lax.fori_loop(0, i // 4, quad, (m0,) * ATT_HPS)
    m_all = lax.fori_loop(2 * (i // 4), i // 2, pair, m_all)
    m_all = lax.fori_loop(0, i & 1, odd_tail, m_all)

    @pl.when((i & 1) == 0)
    def _():
        consume(i, 0, m_all, True)

    @pl.when((i & 1) == 1)
    def _():
        consume(i, 1, m_all, True)

    outs = []
    for pr in range(ATT_HPS // 2):
        pair_t = []
        for hh in (2 * pr, 2 * pr + 1):
            a = acc_ref[hh]
            pair_t.append(a[0:MLA_V, :] / a[MLA_V:MLA_V + 1, :])
        outs.append(jnp.concatenate(pair_t, axis=0).T)
    o_ref[0] = jnp.concatenate(outs, axis=1).astype(BF16)


def _attention(q, k, vt, bsz, seqlen):
    nq = seqlen // TQ
    hps = ATT_HPS
    return pl.pallas_call(
        _attn_kernel,
        grid=(bsz, MLA_HEADS // hps, nq),
        in_specs=[pl.BlockSpec((1, hps, TQ, LANES), lambda b, p, i: (b, p, i, 0)),
                  pl.BlockSpec((1, hps, seqlen, LANES), lambda b, p, i: (b, p, 0, 0)),
                  pl.BlockSpec((1, hps, VT_ROWS, seqlen), lambda b, p, i: (b, p, 0, 0))],
        out_specs=pl.BlockSpec((1, TQ, hps * MLA_V), lambda b, p, i: (b, i, p)),
        out_shape=jax.ShapeDtypeStruct((bsz, seqlen, MLA_WIDTH), BF16),
        scratch_shapes=[pltpu.VMEM((hps, VT_ROWS, TQ), F32), pltpu.VMEM((2, hps, TK, TQ), F32),
                        pltpu.VMEM((2, hps, 8, TQ), F32)],
        compiler_params=pltpu.CompilerParams(
            dimension_semantics=("arbitrary", "arbitrary", "arbitrary"),
            vmem_limit_bytes=VMEM_LIMIT_BYTES),
        name="attention",
    )(q, k, vt)


def _outproj_kernel(x_ref, ys_ref, ym_ref, wo_ref, g_ref, wr_ref, br_ref,
                    x1_ref, h2_ref, route_ref, cnt_ref, wo_bf):
    tm = x_ref.shape[0]

    @pl.when(pl.program_id(0) == 0)
    def _():
        cnt_ref[...] = jnp.zeros_like(cnt_ref)
        wo_bf[...] = wo_ref[...].astype(BF16)

    mix = (jnp.dot(ys_ref[...], wo_bf[0:512, :], preferred_element_type=F32)
           + jnp.dot(ym_ref[...], wo_bf[512:1024, :], preferred_element_type=F32))
    x1 = x_ref[...] + mix
    x1_ref[...] = x1
    h2 = _rms(x1) * g_ref[...]
    h2_ref[...] = _pack_rows(h2)

    h_hi = h2.astype(BF16)
    h_lo = (h2 - h_hi.astype(F32)).astype(BF16)
    hh = jnp.dot(h_hi, wr_ref[...], preferred_element_type=F32)
    lh = jnp.dot(h_lo, wr_ref[:, 0:LANES], preferred_element_type=F32)
    logits = hh[:, 0:LANES] + (hh[:, LANES:2 * LANES] + lh) + br_ref[...]
    lt = logits.T[0:N_EXPERTS, :]
    eid = lax.broadcasted_iota(I32, (N_EXPERTS, 1), 0).astype(F32)

    vals, idxs, hots = [], [], []
    for _ in range(TOP_K):
        mx = jnp.max(lt, axis=0, keepdims=True)
        idx = jnp.min(jnp.where(lt == mx, eid, float(N_EXPERTS)), axis=0, keepdims=True)
        hot = eid == idx
        lt = jnp.where(hot, -jnp.inf, lt)
        vals.append(mx)
        idxs.append(idx)
        hots.append(hot)
    exps = [jnp.exp(v - vals[0]) for v in vals]
    denom = exps[0] + exps[1] + exps[2] + exps[3]

    multi_f = jnp.where(hots[0] | hots[1] | hots[2] | hots[3], 1.0, 0.0)
    r = lax.broadcasted_iota(I32, (tm, tm), 0)
    c = lax.broadcasted_iota(I32, (tm, tm), 1)
    earlier = jnp.where(r < c, 1.0, 0.0).astype(BF16)
    before = jnp.dot(multi_f.astype(BF16), earlier, preferred_element_type=F32) + cnt_ref[:, 0:1]
    cnt_ref[...] = cnt_ref[...] + jnp.sum(multi_f, axis=1, keepdims=True)

    ranks = [jnp.sum(jnp.where(hots[kk], before, 0.0), axis=0, keepdims=True) for kk in range(TOP_K)]
    gates = [e / denom for e in exps]
    route_ref[...] = jnp.concatenate(idxs + gates + ranks + [jnp.zeros((ROUTE_ROWS - 3 * TOP_K, tm), F32)], axis=0)


def _outproj(x2, y_ssd, y_mla, wo, g_ffn, wr, br):
    t = x2.shape[0]
    tm = TM_PROJ
    full = lambda shape: pl.BlockSpec(shape, lambda i: (0,) * len(shape))
    rows = lambda width: pl.BlockSpec((tm, width), lambda i: (i, 0))
    return pl.pallas_call(
        _outproj_kernel,
        grid=(t // tm,),
        in_specs=[rows(D_MODEL), rows(512), rows(512), full((1024, D_MODEL)), full((1, D_MODEL)),
                  full((D_MODEL, 2 * LANES)), full((1, LANES))],
        out_specs=[rows(D_MODEL), rows(HALF), pl.BlockSpec((ROUTE_ROWS, tm), lambda i: (0, i)),
                   full((N_EXPERTS, LANES))],
        out_shape=[jax.ShapeDtypeStruct((t, D_MODEL), F32),
                   jax.ShapeDtypeStruct((t, HALF), I32),
                   jax.ShapeDtypeStruct((ROUTE_ROWS, t), F32),
                   jax.ShapeDtypeStruct((N_EXPERTS, LANES), F32)],
        scratch_shapes=[pltpu.VMEM((1024, D_MODEL), BF16)],
        compiler_params=pltpu.CompilerParams(
            dimension_semantics=("arbitrary",), vmem_limit_bytes=VMEM_LIMIT_BYTES),
        name="outproj_router",
    )(x2, y_ssd, y_mla, wo, g_ffn, wr, br)


def _ffn_kernel(be_ref, bv_ref, nx_ref, sl_ref, xb_ref, xs_ref, wgu_hbm, bgu_ref, wd_hbm, bd_ref, ys_ref,
                wgu_st, wd_st, wgu_bf, wd_bf, sem):
    i = pl.program_id(0)
    e = be_ref[i]
    valid = bv_ref[i]
    slot = sl_ref[i]
    first = ((i == 0) | (e != be_ref[jnp.maximum(i - 1, 0)])) & (valid > 0)

    def weight_copies(expert, dst_slot):
        return (pltpu.make_async_copy(wgu_hbm.at[expert], wgu_st.at[dst_slot], sem.at[0, dst_slot]),
                pltpu.make_async_copy(wd_hbm.at[expert], wd_st.at[dst_slot], sem.at[1, dst_slot]))

    @pl.when(i == 0)
    def _():
        for cp in weight_copies(e, slot):
            cp.start()

    @pl.when(first)
    def _():
        for cp in weight_copies(e, slot):
            cp.wait()

        @pl.when(nx_ref[i] >= 0)
        def _():
            for cp in weight_copies(nx_ref[i], 1 - slot):
                cp.start(priority=1)

        wgu_bf[...] = wgu_st[slot].astype(BF16)
        wd_bf[...] = wd_st[slot].astype(BF16)

    def chain(r0, n_rows, masked):
        rows = pl.ds(r0, n_rows)
        xp = xs_ref[rows, :]
        if masked:
            xp = jnp.where(r0 + lax.broadcasted_iota(I32, (n_rows, 1), 0) < valid, xp, 0)
        x_lo, x_hi = _unpack_rows(xp)
        gu = (jnp.dot(x_lo.astype(BF16), wgu_bf[0:HALF, :], preferred_element_type=F32)
              + jnp.dot(x_hi.astype(BF16), wgu_bf[HALF:D_MODEL, :], preferred_element_type=F32)
              + bgu_ref[0])
        gate = jnp.minimum(gu[:, :D_FF], SWIGLU_LIMIT)
        up = jnp.clip(gu[:, D_FF:], -SWIGLU_LIMIT, SWIGLU_LIMIT)
        glu = gate * jax.nn.sigmoid(SWIGLU_ALPHA * gate)
        mid = ((up + 1.0) * glu).astype(BF16)
        ys_ref[rows, :] = _pack_rows(jnp.dot(mid, wd_bf[...], preferred_element_type=F32) + bd_ref[0])

    @pl.when(valid == MOE_STEP)
    def _():
        for r0 in range(0, MOE_STEP, MOE_BM):
            chain(r0, MOE_BM, False)

    @pl.when((valid > 0) & (valid < MOE_STEP))
    def _():
        half_bm = MOE_BM // 2
        for r0 in range(0, MOE_STEP, MOE_BM):
            pl.when(valid > r0 + half_bm)(functools.partial(chain, r0, MOE_BM, True))
            pl.when((valid > r0) & (valid <= r0 + half_bm))(functools.partial(chain, r0, half_bm, True))

            @pl.when(valid <= r0 + half_bm)
            def _():
                ys_ref[pl.ds(r0 + half_bm, half_bm), :] = jnp.zeros((half_bm, HALF), I32)

            @pl.when(valid <= r0)
            def _():
                ys_ref[pl.ds(r0, half_bm), :] = jnp.zeros((half_bm, HALF), I32)


def _expert_ffn(block_e, block_valid, block_next, block_slot, block_src, xs, wgu, bgu, wd, bd):
    n_slots = xs.shape[0]
    bm = MOE_STEP
    grid_spec = pltpu.PrefetchScalarGridSpec(
        num_scalar_prefetch=5,
        grid=(n_slots // bm,),
        in_specs=[
            pl.BlockSpec((bm, HALF), lambda i, be, bv, nx, sl, xb: (xb[i], 0)),
            pl.BlockSpec(memory_space=pl.ANY),
            pl.BlockSpec((1, 1, 2 * D_FF), lambda i, be, bv, nx, sl, xb: (be[i], 0, 0)),
            pl.BlockSpec(memory_space=pl.ANY),
            pl.BlockSpec((1, 1, D_MODEL), lambda i, be, bv, nx, sl, xb: (be[i], 0, 0)),
        ],
        out_specs=pl.BlockSpec((bm, HALF), lambda i, be, bv, nx, sl, xb: (xb[i], 0)),
        scratch_shapes=[pltpu.VMEM((2, D_MODEL, 2 * D_FF), F32), pltpu.VMEM((2, D_FF, D_MODEL), F32),
                        pltpu.VMEM((D_MODEL, 2 * D_FF), BF16), pltpu.VMEM((D_FF, D_MODEL), BF16),
                        pltpu.SemaphoreType.DMA((2, 2))],
    )
    return pl.pallas_call(
        _ffn_kernel,
        grid_spec=grid_spec,
        out_shape=jax.ShapeDtypeStruct((n_slots, HALF), I32),
        input_output_aliases={5: 0},
        compiler_params=pltpu.CompilerParams(
            dimension_semantics=("arbitrary",), vmem_limit_bytes=VMEM_LIMIT_BYTES),
        name="expert_ffn",
    )(block_e, block_valid, block_next, block_slot, block_src, xs, wgu, bgu, wd, bd)


def _plan_kernel(sp_ref, route_ref, dest_ref):
    idx = route_ref[0:TOP_K, :]
    rank = route_ref[2 * TOP_K:3 * TOP_K, :]
    start = jnp.zeros(idx.shape, F32)
    for e_i in range(N_EXPERTS):
        start = jnp.where(idx == float(e_i), sp_ref[e_i].astype(F32), start)
    dest_ref[...] = (start + rank).astype(I32)


def _slot_plan(route, start_pad):
    t = route.shape[1]
    tm = TM_PLAN
    grid_spec = pltpu.PrefetchScalarGridSpec(
        num_scalar_prefetch=1,
        grid=(t // tm,),
        in_specs=[pl.BlockSpec((ROUTE_ROWS, tm), lambda i, sp: (0, i))],
        out_specs=pl.BlockSpec((TOP_K, tm), lambda i, sp: (0, i)),
    )
    return pl.pallas_call(
        _plan_kernel,
        grid_spec=grid_spec,
        out_shape=jax.ShapeDtypeStruct((TOP_K, t), I32),
        compiler_params=pltpu.CompilerParams(dimension_semantics=("arbitrary",)),
        name="slot_plan",
    )(start_pad, route)


SC_SCATTER_CHUNK = 64
SC_GATHER_CHUNK = 64


def _sc_workers():
    info = plsc.get_sparse_core_info()
    return info.num_cores, info.num_cores * info.num_subcores


def _sc_scatter_rows(rows, dest_km, n_out):
    t, w = rows.shape
    ch = SC_SCATTER_CHUNK
    n_cores, n_workers = _sc_workers()
    n_chunks = t // n_workers // ch
    mesh = plsc.VectorSubcoreMesh(core_axis_name="c", subcore_axis_name="s")

    @functools.partial(
        pl.kernel, mesh=mesh, out_type=jax.ShapeDtypeStruct((n_out, w), rows.dtype),
        scratch_types=[pltpu.VMEM((TOP_K, n_chunks, ch), I32), pltpu.VMEM((2, ch, w), rows.dtype),
                       pltpu.SemaphoreType.DMA((2,)), pltpu.SemaphoreType.DMA((2,))],
        name="sc_dispatch_scatter")
    def scatter_kernel(rows_hbm, dest_hbm, out_hbm, idx_v, rows_v, sem_in, sem_out):
        wid = lax.axis_index("s") * n_cores + lax.axis_index("c")
        first = wid * n_chunks
        for kk in range(TOP_K):
            pltpu.sync_copy(dest_hbm.at[kk, pl.ds(first, n_chunks)], idx_v.at[kk])

        def load(cc, b):
            return pltpu.make_async_copy(rows_hbm.at[pl.ds((first + cc) * ch, ch)], rows_v.at[b], sem_in.at[b])

        def scatters(cc, b):
            return [pltpu.make_async_copy(rows_v.at[b], out_hbm.at[idx_v.at[kk, cc]], sem_out.at[b])
                    for kk in range(TOP_K)]

        load(0, 0).start()
        load(1, 1).start()

        @pl.loop(0, n_chunks, step=2)
        def _(c):
            for b in range(2):
                load(c + b, b).wait()
                for cp in scatters(c + b, b):
                    cp.start()
            for b in range(2):
                for cp in scatters(c + b, b):
                    cp.wait()

                @pl.when(c + 2 + b < n_chunks)
                def _():
                    load(c + 2 + b, b).start()

    return scatter_kernel(rows, dest_km.reshape(TOP_K, t // ch, ch))


def _sc_gather_rows(table, dest_km):
    _, w = table.shape
    t = dest_km.shape[1]
    ch = SC_GATHER_CHUNK
    n_cores, n_workers = _sc_workers()
    per_k = t // n_workers // ch
    n_chunks = TOP_K * per_k
    mesh = plsc.VectorSubcoreMesh(core_axis_name="c", subcore_axis_name="s")

    @functools.partial(
        pl.kernel, mesh=mesh, out_type=jax.ShapeDtypeStruct((TOP_K, t, w), table.dtype),
        scratch_types=[pltpu.VMEM((TOP_K, per_k, ch), I32), pltpu.VMEM((2, ch, w), table.dtype),
                       pltpu.SemaphoreType.DMA((2,)), pltpu.SemaphoreType.DMA((2,))],
        name="sc_combine_gather")
    def gather_kernel(table_hbm, dest_hbm, out_hbm, idx_v, rows_v, sem_in, sem_out):
        wid = lax.axis_index("s") * n_cores + lax.axis_index("c")
        first = wid * per_k
        for kk in range(TOP_K):
            pltpu.sync_copy(dest_hbm.at[kk, pl.ds(first, per_k)], idx_v.at[kk])

        def gather(cc, b):
            return pltpu.make_async_copy(table_hbm.at[idx_v.at[cc // per_k, cc % per_k]], rows_v.at[b], sem_in.at[b])

        def store(cc, b):
            return pltpu.make_async_copy(
                rows_v.at[b], out_hbm.at[cc // per_k, pl.ds((first + cc % per_k) * ch, ch)], sem_out.at[b])

        gather(0, 0).start()
        gather(1, 1).start()

        @pl.loop(0, n_chunks, step=2)
        def _(c):
            for b in range(2):
                gather(c + b, b).wait()
                store(c + b, b).start()
            for b in range(2):
                store(c + b, b).wait()

                @pl.when(c + 2 + b < n_chunks)
                def _():
                    gather(c + 2 + b, b).start()

    return gather_kernel(table, dest_km.reshape(TOP_K, t // ch, ch))


def _combine_kernel(x1_ref, yg_ref, route_ref, g_ref, o_ref, *, final_norm):
    tm = x1_ref.shape[0]
    moe_lo = jnp.zeros((tm, HALF), F32)
    moe_hi = jnp.zeros((tm, HALF), F32)
    route_t = jnp.concatenate([route_ref[...], jnp.zeros((LANES - ROUTE_ROWS, tm), F32)], axis=0).T
    for kk in range(TOP_K):
        gate = route_t[:, TOP_K + kk:TOP_K + kk + 1]
        y_lo, y_hi = _unpack_rows(yg_ref[kk])
        moe_lo = moe_lo + gate * y_lo
        moe_hi = moe_hi + gate * y_hi
    acc = x1_ref[...] + jnp.concatenate([moe_lo, moe_hi], axis=1)
    o_ref[...] = _rms(acc) * g_ref[...] if final_norm else acc


def _combine(x1, yg, route, g_final, final_norm):
    t = x1.shape[0]
    tm = TM_COMB
    return pl.pallas_call(
        functools.partial(_combine_kernel, final_norm=final_norm),
        grid=(t // tm,),
        in_specs=[pl.BlockSpec((tm, D_MODEL), lambda i: (i, 0)),
                  pl.BlockSpec((TOP_K, tm, HALF), lambda i: (0, i, 0)),
                  pl.BlockSpec((ROUTE_ROWS, tm), lambda i: (0, i)),
                  pl.BlockSpec((1, D_MODEL), lambda i: (0, 0))],
        out_specs=pl.BlockSpec((tm, D_MODEL), lambda i: (i, 0)),
        out_shape=jax.ShapeDtypeStruct((t, D_MODEL), F32),
        compiler_params=pltpu.CompilerParams(
            dimension_semantics=("arbitrary",), vmem_limit_bytes=VMEM_LIMIT_BYTES),
        name="combine",
    )(x1, yg, route, g_final)


def _prep_in_weights(w_in, w_uq, w_ukv):
    w_z = w_in[:, 0:512]
    w_xbc = w_in[:, 512:1536]
    w_dt = w_in[:, 1536:1544]
    w_cq = w_in[:, 1544:1800]
    w_ckv = w_in[:, 1800:1928]
    w_kr = w_in[:, 1928:1960]
    half = MLA_ROPE // 2
    zeros = lambda rows, width: jnp.zeros(rows + (width,), BF16)
    cat = lambda parts: jnp.concatenate([p.astype(BF16) for p in parts], axis=-1)
    d = (D_MODEL,)
    misc = [w_dt, zeros(d, MLA_NOPE - SSD_HEADS), w_kr, w_kr[:, half:], w_kr[:, :half]]
    w1 = cat([w_z, w_xbc, w_cq, w_ckv] + misc)

    wq3 = w_uq.reshape(MLA_Q_RANK, MLA_HEADS, MLA_QK)
    qh = (MLA_Q_RANK, MLA_HEADS)
    main = cat([wq3, zeros(qh, LANES - MLA_QK)])
    swap = cat([zeros(qh, MLA_NOPE), wq3[:, :, MLA_NOPE + half:], wq3[:, :, MLA_NOPE:MLA_NOPE + half],
                zeros(qh, LANES - MLA_QK)])
    wq = jnp.concatenate([main.reshape(MLA_Q_RANK, -1), swap.reshape(MLA_Q_RANK, -1)], axis=1)

    wkv3 = w_ukv.reshape(MLA_KV_RANK, MLA_HEADS, MLA_NOPE + MLA_V)
    kh = (MLA_KV_RANK, MLA_HEADS)
    kpart = cat([wkv3[:, :, :MLA_NOPE], zeros(kh, LANES - MLA_NOPE)])
    vpart = wkv3[:, :, MLA_NOPE:].astype(BF16)
    wkv = jnp.concatenate([kpart.reshape(MLA_KV_RANK, -1), vpart.reshape(MLA_KV_RANK, -1)], axis=1)
    return w1, wq, wkv


def _rope_inv_freq():
    inv_freq = ROPE_THETA ** (-jnp.arange(0, MLA_ROPE, 2, dtype=F32) / MLA_ROPE)
    return inv_freq[:, None]


def _pad_lanes(v, fill=0.0):
    return jnp.full((1, LANES), fill, F32).at[0, :v.shape[0]].set(v)


def kernel(x, positions, norm_mix_g, w_in, conv_w, conv_b, dt_bias, a_log, d_skip, ssd_norm_g, q_norm_g, w_uq, kv_norm_g, w_ukv, w_out, norm_ffn_g, w_router, b_router, w_gate_up, b_gate_up, w_down, b_down, norm_final_g):
    bsz, seqlen, d = x.shape
    t = bsz * seqlen
    depth = w_in.shape[0]
    x2 = x.reshape(t, d)
    pos_rows = positions.reshape(t // TM_PROJ, 1, TM_PROJ).astype(I32)
    invf = _rope_inv_freq()

    for l in range(depth):
        w1, wq, wkv = _prep_in_weights(w_in[l], w_uq[l], w_ukv[l])
        z, xbc, dtm, q, k, v = _inproj(
            x2, pos_rows, norm_mix_g[l][None, :], w1, q_norm_g[l][None, :], wq, kv_norm_g[l][None, :], wkv,
            invf, bsz, seqlen)
        y_ssd = _ssd(z, xbc, dtm, conv_w[l], conv_b[l][None, :], _pad_lanes(dt_bias[l]), _pad_lanes(a_log[l]),
                     jnp.repeat(d_skip[l], SSD_HEAD_DIM)[None, :], ssd_norm_g[l][None, :], bsz, seqlen)
        y_mla = _attention(q, k, v, bsz, seqlen).reshape(t, MLA_WIDTH)

        wr = jnp.zeros((d, LANES), F32).at[:, :N_EXPERTS].set(w_router[l])
        wr_hi = wr.astype(BF16)
        wr_lo = (wr - wr_hi.astype(F32)).astype(BF16)
        x1, h2p, route, cnt = _outproj(x2, y_ssd, y_mla, w_out[l], norm_ffn_g[l][None, :],
                                       jnp.concatenate([wr_hi, wr_lo], axis=1), _pad_lanes(b_router[l]))

        counts = cnt[:, 0].astype(I32)
        padded = ((counts + MOE_STEP - 1) // MOE_STEP) * MOE_STEP
        end_pad = jnp.cumsum(padded)
        start_pad = end_pad - padded
        n_slots = t * TOP_K + N_EXPERTS * MOE_STEP
        n_blocks = n_slots // MOE_STEP
        block_start = jnp.arange(n_blocks, dtype=I32) * MOE_STEP
        block_e = jnp.minimum(jnp.sum(block_start[:, None] >= end_pad[None, :], axis=1), N_EXPERTS - 1).astype(I32)
        eids = jnp.arange(N_EXPERTS, dtype=I32)
        block_hot = block_e[:, None] == eids[None, :]
        per_block = lambda table: jnp.sum(jnp.where(block_hot, table[None, :], 0), axis=1).astype(I32)
        block_valid = jnp.clip(per_block(counts) - (block_start - per_block(start_pad)), 0, MOE_STEP).astype(I32)
        used = counts > 0
        later_used = jnp.where((eids[None, :] > eids[:, None]) & used[None, :], eids[None, :], N_EXPERTS)
        next_used = jnp.min(later_used, axis=1)
        next_used = jnp.where(next_used < N_EXPERTS, next_used, -1).astype(I32)
        stage_slot = ((jnp.cumsum(used.astype(I32)) - 1) & 1).astype(I32)
        dest_km = _slot_plan(route, start_pad.astype(I32))

        xs = _sc_scatter_rows(h2p, dest_km, n_slots)
        n_used = end_pad[-1] // MOE_STEP
        block_src = jnp.minimum(jnp.arange(n_blocks, dtype=I32), n_used - 1).astype(I32)
        ys = _expert_ffn(block_e, block_valid, per_block(next_used), per_block(stage_slot), block_src, xs,
                         w_gate_up[l], b_gate_up[l][:, None, :], w_down[l], b_down[l][:, None, :])
        yg = _sc_gather_rows(ys, dest_km)
        x2 = _combine(x1, yg, route, norm_final_g[None, :], l == depth - 1)
    return x2.reshape(bsz, seqlen, d)
```

```python
import functools

import jax
import jax.numpy as jnp
import numpy as np
from jax import lax
from jax.experimental import pallas as pl
from jax.experimental.pallas import tpu as pltpu
from jax.experimental.pallas import tpu_sc as plsc

F32 = jnp.float32
BF16 = jnp.bfloat16
I32 = jnp.int32

D_MODEL = 1024
EPS = 1e-6
LANES = 128
V7X_VMEM_BYTES = 64 * 1024 * 1024
VMEM_LIMIT_BYTES = V7X_VMEM_BYTES * 7 // 8

SSD_HEADS = 8
SSD_HEAD_DIM = 64
SSD_WIDTH = 512
SSD_STATE = 128
SSD_CONV = 4
SSD_CHUNK = 128
SSD_CONV_DIM = 1024
CONV_PAD = 8
SSD_CPS = 8

MLA_HEADS = 8
MLA_Q_RANK = 256
MLA_KV_RANK = 128
MLA_NOPE = 64
MLA_ROPE = 32
MLA_V = 64
MLA_QK = MLA_NOPE + MLA_ROPE
MLA_WIDTH = 512
ROPE_THETA = 10000.0
LOG2_E = 1.4426950408889634

N_EXPERTS = 32
TOP_K = 4
D_FF = 1024
SWIGLU_LIMIT = 7.0
SWIGLU_ALPHA = 1.702

IN_W = 512 + 1024 + 256 + 128 + 128

TM_PROJ = 1024
TQ = 512
TK = 512
ATT_HPS = 4
VT_ROWS = 80
MOE_BM = 256
MOE_STEP = 1024
TM_COMB = 1024
TM_PLAN = 2048
ROUTE_ROWS = 16

NT_DIMS = (((1,), (1,)), ((), ()))


def _rms(x):
    return x * lax.rsqrt(jnp.mean(x * x, axis=-1, keepdims=True) + EPS)


HALF = D_MODEL // 2
HI_MASK = np.int32(-65536)


def _pack_rows(a):
    lo = lax.bitcast_convert_type(a[:, :HALF].astype(BF16).astype(F32), I32)
    hi = lax.bitcast_convert_type(a[:, HALF:].astype(BF16).astype(F32), I32)
    return (hi & HI_MASK) | lax.shift_right_logical(lo, 16)


def _unpack_rows(p):
    lo = lax.bitcast_convert_type(lax.shift_left(p, 16), F32)
    hi = lax.bitcast_convert_type(p & HI_MASK, F32)
    return lo, hi


def _inproj_kernel(x_ref, pos_ref, g_ref, w1_ref, qg_ref, wq_ref, kvg_ref, wkv_ref, invf_ref,
                   z_ref, xbc_ref, dtm_ref, q_ref, k_ref, vt_ref):
    x = x_ref[...]
    h = (_rms(x) * g_ref[...]).astype(BF16)
    p = jnp.dot(h, w1_ref[...], preferred_element_type=F32)
    z_ref[...] = p[:, 0:512]
    xbc_ref[...] = p[:, 512:1536]
    cq = p[:, 1536:1792]
    ckv = p[:, 1792:1920]
    m1 = p[:, 1920:2048]
    dtm_ref[...] = m1

    lane = lax.broadcasted_iota(I32, (1, LANES), 1)
    tm = x.shape[0]
    ang = invf_ref[...] * pos_ref[0].astype(F32)
    cos_c = jnp.cos(ang)
    sin_c = jnp.sin(ang)
    z_lo = jnp.zeros((MLA_NOPE, tm), F32)
    z_hi = jnp.zeros((LANES - MLA_QK, tm), F32)
    cos_t = jnp.concatenate([z_lo, cos_c, cos_c, z_hi], axis=0).T
    sin_t = jnp.concatenate([z_lo, -sin_c, sin_c, z_hi], axis=0).T
    cosq_t = jnp.where(lane < MLA_NOPE, 1.0, cos_t)
    scale = MLA_QK ** -0.5 * LOG2_E

    cqn = (_rms(cq) * qg_ref[...]).astype(BF16)
    qq = jnp.dot(cqn, wq_ref[...], preferred_element_type=F32)
    ckvn = (_rms(ckv) * kvg_ref[...]).astype(BF16)
    kv = jnp.dot(ckvn, wkv_ref[...], preferred_element_type=F32)
    swap_down = LANES - MLA_ROPE
    krot = m1 * cos_t + pltpu.roll(m1, swap_down, axis=1) * sin_t
    for h_i in range(MLA_HEADS):
        lo = h_i * LANES
        qm = qq[:, lo:lo + LANES]
        qs = qq[:, 1024 + lo:1024 + lo + LANES]
        q_ref[0, h_i] = ((qm * cosq_t + qs * sin_t) * scale).astype(BF16)
        k_ref[0, h_i] = (kv[:, lo:lo + LANES] + krot).astype(BF16)
    ones_rows = jnp.ones((VT_ROWS - MLA_V, tm), BF16)
    for pr in range(MLA_HEADS // 2):
        vpt = kv[:, 1024 + pr * LANES:1024 + (pr + 1) * LANES].T.astype(BF16)
        vt_ref[0, 2 * pr] = jnp.concatenate([vpt[0:MLA_V, :], ones_rows], axis=0)
        vt_ref[0, 2 * pr + 1] = jnp.concatenate([vpt[MLA_V:2 * MLA_V, :], ones_rows], axis=0)


def _inproj(x2, pos_rows, g_mix, w1, qg, wq, kvg, wkv, invf, bsz, seqlen):
    t = x2.shape[0]
    tm = TM_PROJ
    per_b = seqlen // tm
    full = lambda shape: pl.BlockSpec(shape, lambda i: (0,) * len(shape))
    head_spec = pl.BlockSpec((1, MLA_HEADS, tm, LANES), lambda i: (i // per_b, 0, i % per_b, 0))
    head_shape = jax.ShapeDtypeStruct((bsz, MLA_HEADS, seqlen, LANES), BF16)
    vt_spec = pl.BlockSpec((1, MLA_HEADS, VT_ROWS, tm), lambda i: (i // per_b, 0, 0, i % per_b))
    vt_shape = jax.ShapeDtypeStruct((bsz, MLA_HEADS, VT_ROWS, seqlen), BF16)
    return pl.pallas_call(
        _inproj_kernel,
        grid=(t // tm,),
        in_specs=[
            pl.BlockSpec((tm, D_MODEL), lambda i: (i, 0)),
            pl.BlockSpec((1, 1, tm), lambda i: (i, 0, 0)),
            full((1, D_MODEL)), full((D_MODEL, IN_W)),
            full((1, MLA_Q_RANK)), full((MLA_Q_RANK, 2048)),
            full((1, MLA_KV_RANK)), full((MLA_KV_RANK, 1536)),
            full((MLA_ROPE // 2, 1)),
        ],
        out_specs=[
            pl.BlockSpec((tm, 512), lambda i: (i, 0)),
            pl.BlockSpec((tm, 1024), lambda i: (i, 0)),
            pl.BlockSpec((tm, LANES), lambda i: (i, 0)),
            head_spec, head_spec, vt_spec,
        ],
        out_shape=[
            jax.ShapeDtypeStruct((t, 512), F32),
            jax.ShapeDtypeStruct((t, 1024), F32),
            jax.ShapeDtypeStruct((t, LANES), F32),
            head_shape, head_shape, vt_shape,
        ],
        compiler_params=pltpu.CompilerParams(
            dimension_semantics=("arbitrary",), vmem_limit_bytes=VMEM_LIMIT_BYTES),
        name="inproj",
    )(x2, pos_rows, g_mix, w1, qg, wq, kvg, wkv, invf)


def _ssd_kernel(z_ref, xbc_ref, dtm_ref, cw_ref, cb_ref, dtb_ref, alog_ref, dsk_ref, ng_ref,
                y_ref, ext_ref, st_ref):
    q = SSD_CHUNK
    rows = SSD_CPS * q

    @pl.when(pl.program_id(1) == 0)
    def _():
        ext_ref[0:CONV_PAD, :] = jnp.zeros((CONV_PAD, SSD_CONV_DIM), F32)
        st_ref[...] = jnp.zeros_like(st_ref)

    ext_ref[CONV_PAD:CONV_PAD + rows, :] = xbc_ref[...]

    lane = lax.broadcasted_iota(I32, (1, LANES), 1)
    row = lax.broadcasted_iota(I32, (q, q), 0)
    col = lax.broadcasted_iota(I32, (q, q), 1)
    tril = row >= col
    tril_b = jnp.where(tril, 1.0, 0.0).astype(BF16)
    spread = jnp.where(
        lax.broadcasted_iota(I32, (LANES, SSD_WIDTH), 0)
        == lax.broadcasted_iota(I32, (LANES, SSD_WIDTH), 1) // SSD_HEAD_DIM, 1.0, 0.0).astype(BF16)
    a_neg = -jnp.exp(alog_ref[...]) * LOG2_E

    def split3(v):
        hi = v.astype(BF16)
        r1 = v - hi.astype(F32)
        mid = r1.astype(BF16)
        return hi, mid, (r1 - mid.astype(F32)).astype(BF16)

    def dot3_right(parts, m):
        return sum(jnp.dot(p, m, preferred_element_type=F32) for p in parts)

    def expand(cols):
        return dot3_right(split3(cols), spread)

    for ci in range(SSD_CPS):
        lo = ci * q
        window = ext_ref[lo:lo + CONV_PAD + q, :]
        conv = cb_ref[...] + cw_ref[SSD_CONV - 1:SSD_CONV, :] * window[CONV_PAD:, :]
        for kk in range(SSD_CONV - 1):
            shifted = pltpu.roll(window, SSD_CONV - 1 - kk, axis=0)
            conv = conv + cw_ref[kk:kk + 1, :] * shifted[CONV_PAD:, :]
        u = conv * jax.nn.sigmoid(conv)
        xs = u[:, 0:512]
        bm = u[:, 512:768]
        cm = u[:, 768:1024]

        xdt = dtm_ref[lo:lo + q, :] + dtb_ref[...]
        dt = jnp.maximum(xdt, 0.0) + jnp.log1p(jnp.exp(-jnp.abs(xdt)))
        adt = jnp.where(lane < SSD_HEADS, dt * a_neg, 0.0)
        cum_col = sum(jnp.dot(tril_b, p, preferred_element_type=F32) for p in split3(adt))
        cum_row = cum_col.T

        dt_e = expand(dt)
        ac_e = expand(cum_col)
        last_e = ac_e[q - 1:q, :]
        xd = xs * dt_e
        w_end = xd * jnp.exp2(last_e - ac_e)
        eac = jnp.exp2(ac_e)
        cdec = jnp.exp2(last_e)

        y_parts = []
        for g in range(2):
            gl = g * 256
            bg = bm[:, g * SSD_STATE:(g + 1) * SSD_STATE]
            cg = cm[:, g * SSD_STATE:(g + 1) * SSD_STATE].astype(BF16)
            scores = lax.dot_general(cg, bg.astype(BF16), NT_DIMS, preferred_element_type=F32)
            bgt = bg.T.astype(BF16)
            sprev = st_ref[g]
            yoff = jnp.dot(cg, sprev.astype(BF16), preferred_element_type=F32)
            st_ref[g] = sprev * cdec[:, gl:gl + 256] + jnp.dot(
                bgt, w_end[:, gl:gl + 256].astype(BF16), preferred_element_type=F32)
            for pr in range(2):
                pl_lo = gl + pr * LANES
                xdp = xd[:, pl_lo:pl_lo + LANES].astype(BF16)
                res = []
                for jj in range(2):
                    h_i = g * 4 + pr * 2 + jj
                    seg = cum_col[:, h_i:h_i + 1] - cum_row[h_i:h_i + 1, :]
                    dec = jnp.exp2(jnp.where(tril, seg, -jnp.inf))
                    res.append(jnp.dot((scores * dec).astype(BF16), xdp, preferred_element_type=F32))
                ydiag = jnp.where(lane < SSD_HEAD_DIM, res[0], res[1])
                y_parts.append(ydiag + yoff[:, pr * LANES:(pr + 1) * LANES] * eac[:, pl_lo:pl_lo + LANES])
        y = jnp.concatenate(y_parts, axis=1) + dsk_ref[...] * xs
        zz = z_ref[lo:lo + q, :]
        y = y * (zz * jax.nn.sigmoid(zz))
        outs = []
        for g in range(2):
            yg = y[:, g * 256:(g + 1) * 256]
            outs.append(_rms(yg))
        y_ref[lo:lo + q, :] = (jnp.concatenate(outs, axis=1) * ng_ref[...]).astype(BF16)

    ext_ref[0:CONV_PAD, :] = ext_ref[rows:rows + CONV_PAD, :]


def _ssd(z, xbc, dtm, cw, cb, dtb, alog, dsk, ng, bsz, seqlen):
    t = z.shape[0]
    q = SSD_CHUNK
    rows = SSD_CPS * q
    nc = seqlen // rows
    full = lambda shape: pl.BlockSpec(shape, lambda b, c: (0,) * len(shape))
    row_spec = lambda width: pl.BlockSpec((rows, width), lambda b, c: (b * nc + c, 0))
    return pl.pallas_call(
        _ssd_kernel,
        grid=(bsz, nc),
        in_specs=[row_spec(512), row_spec(1024), row_spec(LANES),
                  full((SSD_CONV, SSD_CONV_DIM)), full((1, SSD_CONV_DIM)),
                  full((1, LANES)), full((1, LANES)), full((1, SSD_WIDTH)), full((1, SSD_WIDTH))],
        out_specs=row_spec(512),
        out_shape=jax.ShapeDtypeStruct((t, SSD_WIDTH), BF16),
        scratch_shapes=[pltpu.VMEM((rows + CONV_PAD, SSD_CONV_DIM), F32),
                        pltpu.VMEM((2, SSD_STATE, 256), F32)],
        compiler_params=pltpu.CompilerParams(dimension_semantics=("arbitrary", "arbitrary")),
        name="ssd",
    )(z, xbc, dtm, cw, cb, dtb, alog, dsk, ng)


def _attn_kernel(q_ref, k_ref, vt_ref, o_ref, acc_ref, s_ref, bmax_ref):
    i = pl.program_id(2)
    acc_ref[...] = jnp.zeros_like(acc_ref)

    def col_max(st):
        part = st[0:LANES, :]
        for r0 in range(LANES, TK, LANES):
            part = jnp.maximum(part, st[r0:r0 + LANES, :])
        return jnp.broadcast_to(jnp.max(part, axis=0, keepdims=True), (8, TQ))

    def scores(j, slot):
        start = pl.multiple_of(j * TK, TK)
        for hh in range(ATT_HPS):
            kb = k_ref[0, hh, pl.ds(start, TK), :]
            st = lax.dot_general(kb, q_ref[0, hh], NT_DIMS, preferred_element_type=F32)
            s_ref[slot, hh] = st
            bmax_ref[slot, hh] = col_max(st)

    def consume(j, slot, m_all, masked):
        start = pl.multiple_of(j * TK, TK)
        new_m = []
        for hh in range(ATT_HPS):
            vt = vt_ref[0, hh, :, pl.ds(start, TK)]
            st = s_ref[slot, hh]
            if masked:
                key = lax.broadcasted_iota(I32, (TK, TQ), 0)
                qry = lax.broadcasted_iota(I32, (TK, TQ), 1)
                st = jnp.where(key <= qry, st, -jnp.inf)
                block_max = col_max(st)
            else:
                block_max = bmax_ref[slot, hh]
            m_old = m_all[hh]
            m_new = jnp.maximum(m_old, block_max)
            alpha = jnp.exp2(m_old - m_new)
            p = jnp.exp2((st - m_new[0:1, :]).astype(BF16))
            acc_ref[hh] = acc_ref[hh] * alpha[0:1, :] + jnp.dot(vt, p, preferred_element_type=F32)
            new_m.append(m_new)
        return tuple(new_m)

    def pair(p, m_all):
        scores(2 * p + 1, 1)
        m_all = consume(2 * p, 0, m_all, False)
        scores(2 * p + 2, 0)
        return consume(2 * p + 1, 1, m_all, False)

    def odd_tail(_, m_all):
        scores(i, 1)
        return consume(i - 1, 0, m_all, False)

    def quad(g, m_all):
        return pair(2 * g + 1, pair(2 * g, m_all))

    m0 = jnp.full((8, TQ), -jnp.inf, F32)
    scores(0, 0)
    m_all = lax.fori_loop(0, i // 4, quad, (m0,) * ATT_HPS)
    m_all = lax.fori_loop(2 * (i // 4), i // 2, pair, m_all)
    m_all = lax.fori_loop(0, i & 1, odd_tail, m_all)

    @pl.when((i & 1) == 0)
    def _():
        consume(i, 0, m_all, True)

    @pl.when((i & 1) == 1)
    def _():
        consume(i, 1, m_all, True)

    outs = []
    for pr in range(ATT_HPS // 2):
        pair_t = []
        for hh in (2 * pr, 2 * pr + 1):
            a = acc_ref[hh]
            pair_t.append(a[0:MLA_V, :] / a[MLA_V:MLA_V + 1, :])
        outs.append(jnp.concatenate(pair_t, axis=0).T)
    o_ref[0] = jnp.concatenate(outs, axis=1).astype(BF16)


def _attention(q, k, vt, bsz, seqlen):
    nq = seqlen // TQ
    hps = ATT_HPS
    return pl.pallas_call(
        _attn_kernel,
        grid=(bsz, MLA_HEADS // hps, nq),
        in_specs=[pl.BlockSpec((1, hps, TQ, LANES), lambda b, p, i: (b, p, i, 0)),
                  pl.BlockSpec((1, hps, seqlen, LANES), lambda b, p, i: (b, p, 0, 0)),
                  pl.BlockSpec((1, hps, VT_ROWS, seqlen), lambda b, p, i: (b, p, 0, 0))],
        out_specs=pl.BlockSpec((1, TQ, hps * MLA_V), lambda b, p, i: (b, i, p)),
        out_shape=jax.ShapeDtypeStruct((bsz, seqlen, MLA_WIDTH), BF16),
        scratch_shapes=[pltpu.VMEM((hps, VT_ROWS, TQ), F32), pltpu.VMEM((2, hps, TK, TQ), F32),
                        pltpu.VMEM((2, hps, 8, TQ), F32)],
        compiler_params=pltpu.CompilerParams(
            dimension_semantics=("arbitrary", "arbitrary", "arbitrary"),
            vmem_limit_bytes=VMEM_LIMIT_BYTES),
        name="attention",
    )(q, k, vt)


def _outproj_kernel(x_ref, ys_ref, ym_ref, wo_ref, g_ref, wr_ref, br_ref,
                    x1_ref, h2_ref, route_ref, cnt_ref, wo_bf):
    tm = x_ref.shape[0]

    @pl.when(pl.program_id(0) == 0)
    def _():
        cnt_ref[...] = jnp.zeros_like(cnt_ref)
        wo_bf[...] = wo_ref[...].astype(BF16)

    mix = (jnp.dot(ys_ref[...], wo_bf[0:512, :], preferred_element_type=F32)
           + jnp.dot(ym_ref[...], wo_bf[512:1024, :], preferred_element_type=F32))
    x1 = x_ref[...] + mix
    x1_ref[...] = x1
    h2 = _rms(x1) * g_ref[...]
    h2_ref[...] = _pack_rows(h2)

    h_hi = h2.astype(BF16)
    h_lo = (h2 - h_hi.astype(F32)).astype(BF16)
    hh = jnp.dot(h_hi, wr_ref[...], preferred_element_type=F32)
    lh = jnp.dot(h_lo, wr_ref[:, 0:LANES], preferred_element_type=F32)
    logits = hh[:, 0:LANES] + (hh[:, LANES:2 * LANES] + lh) + br_ref[...]
    lt = logits.T[0:N_EXPERTS, :]
    eid = lax.broadcasted_iota(I32, (N_EXPERTS, 1), 0).astype(F32)

    vals, idxs, hots = [], [], []
    for _ in range(TOP_K):
        mx = jnp.max(lt, axis=0, keepdims=True)
        idx = jnp.min(jnp.where(lt == mx, eid, float(N_EXPERTS)), axis=0, keepdims=True)
        hot = eid == idx
        lt = jnp.where(hot, -jnp.inf, lt)
        vals.append(mx)
        idxs.append(idx)
        hots.append(hot)
    exps = [jnp.exp(v - vals[0]) for v in vals]
    denom = exps[0] + exps[1] + exps[2] + exps[3]

    multi_f = jnp.where(hots[0] | hots[1] | hots[2] | hots[3], 1.0, 0.0)
    r = lax.broadcasted_iota(I32, (tm, tm), 0)
    c = lax.broadcasted_iota(I32, (tm, tm), 1)
    earlier = jnp.where(r < c, 1.0, 0.0).astype(BF16)
    before = jnp.dot(multi_f.astype(BF16), earlier, preferred_element_type=F32) + cnt_ref[:, 0:1]
    cnt_ref[...] = cnt_ref[...] + jnp.sum(multi_f, axis=1, keepdims=True)

    ranks = [jnp.sum(jnp.where(hots[kk], before, 0.0), axis=0, keepdims=True) for kk in range(TOP_K)]
    gates = [e / denom for e in exps]
    route_ref[...] = jnp.concatenate(idxs + gates + ranks + [jnp.zeros((ROUTE_ROWS - 3 * TOP_K, tm), F32)], axis=0)


def _outproj(x2, y_ssd, y_mla, wo, g_ffn, wr, br):
    t = x2.shape[0]
    tm = TM_PROJ
    full = lambda shape: pl.BlockSpec(shape, lambda i: (0,) * len(shape))
    rows = lambda width: pl.BlockSpec((tm, width), lambda i: (i, 0))
    return pl.pallas_call(
        _outproj_kernel,
        grid=(t // tm,),
        in_specs=[rows(D_MODEL), rows(512), rows(512), full((1024, D_MODEL)), full((1, D_MODEL)),
                  full((D_MODEL, 2 * LANES)), full((1, LANES))],
        out_specs=[rows(D_MODEL), rows(HALF), pl.BlockSpec((ROUTE_ROWS, tm), lambda i: (0, i)),
                   full((N_EXPERTS, LANES))],
        out_shape=[jax.ShapeDtypeStruct((t, D_MODEL), F32),
                   jax.ShapeDtypeStruct((t, HALF), I32),
                   jax.ShapeDtypeStruct((ROUTE_ROWS, t), F32),
                   jax.ShapeDtypeStruct((N_EXPERTS, LANES), F32)],
        scratch_shapes=[pltpu.VMEM((1024, D_MODEL), BF16)],
        compiler_params=pltpu.CompilerParams(
            dimension_semantics=("arbitrary",), vmem_limit_bytes=VMEM_LIMIT_BYTES),
        name="outproj_router",
    )(x2, y_ssd, y_mla, wo, g_ffn, wr, br)


def _ffn_kernel(be_ref, bv_ref, nx_ref, sl_ref, xb_ref, xs_ref, wgu_hbm, bgu_ref, wd_hbm, bd_ref, ys_ref,
                wgu_st, wd_st, wgu_bf, wd_bf, sem):
    i = pl.program_id(0)
    e = be_ref[i]
    valid = bv_ref[i]
    slot = sl_ref[i]
    first = ((i == 0) | (e != be_ref[jnp.maximum(i - 1, 0)])) & (valid > 0)

    def weight_copies(expert, dst_slot):
        return (pltpu.make_async_copy(wgu_hbm.at[expert], wgu_st.at[dst_slot], sem.at[0, dst_slot]),
                pltpu.make_async_copy(wd_hbm.at[expert], wd_st.at[dst_slot], sem.at[1, dst_slot]))

    @pl.when(i == 0)
    def _():
        for cp in weight_copies(e, slot):
            cp.start()

    @pl.when(first)
    def _():
        for cp in weight_copies(e, slot):
            cp.wait()

        @pl.when(nx_ref[i] >= 0)
        def _():
            for cp in weight_copies(nx_ref[i], 1 - slot):
                cp.start(priority=1)

        wgu_bf[...] = wgu_st[slot].astype(BF16)
        wd_bf[...] = wd_st[slot].astype(BF16)

    def chain(r0, n_rows, masked):
        rows = pl.ds(r0, n_rows)
        xp = xs_ref[rows, :]
        if masked:
            xp = jnp.where(r0 + lax.broadcasted_iota(I32, (n_rows, 1), 0) < valid, xp, 0)
        x_lo, x_hi = _unpack_rows(xp)
        gu = (jnp.dot(x_lo.astype(BF16), wgu_bf[0:HALF, :], preferred_element_type=F32)
              + jnp.dot(x_hi.astype(BF16), wgu_bf[HALF:D_MODEL, :], preferred_element_type=F32)
              + bgu_ref[0])
        gate = jnp.minimum(gu[:, :D_FF], SWIGLU_LIMIT)
        up = jnp.clip(gu[:, D_FF:], -SWIGLU_LIMIT, SWIGLU_LIMIT)
        glu = gate * jax.nn.sigmoid(SWIGLU_ALPHA * gate)
        mid = ((up + 1.0) * glu).astype(BF16)
        ys_ref[rows, :] = _pack_rows(jnp.dot(mid, wd_bf[...], preferred_element_type=F32) + bd_ref[0])

    @pl.when(valid == MOE_STEP)
    def _():
        for r0 in range(0, MOE_STEP, MOE_BM):
            chain(r0, MOE_BM, False)

    @pl.when((valid > 0) & (valid < MOE_STEP))
    def _():
        half_bm = MOE_BM // 2
        for r0 in range(0, MOE_STEP, MOE_BM):
            pl.when(valid > r0 + half_bm)(functools.partial(chain, r0, MOE_BM, True))
            pl.when((valid > r0) & (valid <= r0 + half_bm))(functools.partial(chain, r0, half_bm, True))

            @pl.when(valid <= r0 + half_bm)
            def _():
                ys_ref[pl.ds(r0 + half_bm, half_bm), :] = jnp.zeros((half_bm, HALF), I32)

            @pl.when(valid <= r0)
            def _():
                ys_ref[pl.ds(r0, half_bm), :] = jnp.zeros((half_bm, HALF), I32)


def _expert_ffn(block_e, block_valid, block_next, block_slot, block_src, xs, wgu, bgu, wd, bd):
    n_slots = xs.shape[0]
    bm = MOE_STEP
    grid_spec = pltpu.PrefetchScalarGridSpec(
        num_scalar_prefetch=5,
        grid=(n_slots // bm,),
        in_specs=[
            pl.BlockSpec((bm, HALF), lambda i, be, bv, nx, sl, xb: (xb[i], 0)),
            pl.BlockSpec(memory_space=pl.ANY),
            pl.BlockSpec((1, 1, 2 * D_FF), lambda i, be, bv, nx, sl, xb: (be[i], 0, 0)),
            pl.BlockSpec(memory_space=pl.ANY),
            pl.BlockSpec((1, 1, D_MODEL), lambda i, be, bv, nx, sl, xb: (be[i], 0, 0)),
        ],
        out_specs=pl.BlockSpec((bm, HALF), lambda i, be, bv, nx, sl, xb: (xb[i], 0)),
        scratch_shapes=[pltpu.VMEM((2, D_MODEL, 2 * D_FF), F32), pltpu.VMEM((2, D_FF, D_MODEL), F32),
                        pltpu.VMEM((D_MODEL, 2 * D_FF), BF16), pltpu.VMEM((D_FF, D_MODEL), BF16),
                        pltpu.SemaphoreType.DMA((2, 2))],
    )
    return pl.pallas_call(
        _ffn_kernel,
        grid_spec=grid_spec,
        out_shape=jax.ShapeDtypeStruct((n_slots, HALF), I32),
        input_output_aliases={5: 0},
        compiler_params=pltpu.CompilerParams(
            dimension_semantics=("arbitrary",), vmem_limit_bytes=VMEM_LIMIT_BYTES),
        name="expert_ffn",
    )(block_e, block_valid, block_next, block_slot, block_src, xs, wgu, bgu, wd, bd)


def _plan_kernel(sp_ref, route_ref, dest_ref):
    idx = route_ref[0:TOP_K, :]
    rank = route_ref[2 * TOP_K:3 * TOP_K, :]
    start = jnp.zeros(idx.shape, F32)
    for e_i in range(N_EXPERTS):
        start = jnp.where(idx == float(e_i), sp_ref[e_i].astype(F32), start)
    dest_ref[...] = (start + rank).astype(I32)


def _slot_plan(route, start_pad):
    t = route.shape[1]
    tm = TM_PLAN
    grid_spec = pltpu.PrefetchScalarGridSpec(
        num_scalar_prefetch=1,
        grid=(t // tm,),
        in_specs=[pl.BlockSpec((ROUTE_ROWS, tm), lambda i, sp: (0, i))],
        out_specs=pl.BlockSpec((TOP_K, tm), lambda i, sp: (0, i)),
    )
    return pl.pallas_call(
        _plan_kernel,
        grid_spec=grid_spec,
        out_shape=jax.ShapeDtypeStruct((TOP_K, t), I32),
        compiler_params=pltpu.CompilerParams(dimension_semantics=("arbitrary",)),
        name="slot_plan",
    )(start_pad, route)


SC_SCATTER_CHUNK = 64
SC_GATHER_CHUNK = 64


def _sc_workers():
    info = plsc.get_sparse_core_info()
    return info.num_cores, info.num_cores * info.num_subcores


def _sc_scatter_rows(rows, dest_km, n_out):
    t, w = rows.shape
    ch = SC_SCATTER_CHUNK
    n_cores, n_workers = _sc_workers()
    n_chunks = t // n_workers // ch
    mesh = plsc.VectorSubcoreMesh(core_axis_name="c", subcore_axis_name="s")

    @functools.partial(
        pl.kernel, mesh=mesh, out_type=jax.ShapeDtypeStruct((n_out, w), rows.dtype),
        scratch_types=[pltpu.VMEM((TOP_K, n_chunks, ch), I32), pltpu.VMEM((2, ch, w), rows.dtype),
                       pltpu.SemaphoreType.DMA((2,)), pltpu.SemaphoreType.DMA((2,))],
        name="sc_dispatch_scatter")
    def scatter_kernel(rows_hbm, dest_hbm, out_hbm, idx_v, rows_v, sem_in, sem_out):
        wid = lax.axis_index("s") * n_cores + lax.axis_index("c")
        first = wid * n_chunks
        for kk in range(TOP_K):
            pltpu.sync_copy(dest_hbm.at[kk, pl.ds(first, n_chunks)], idx_v.at[kk])

        def load(cc, b):
            return pltpu.make_async_copy(rows_hbm.at[pl.ds((first + cc) * ch, ch)], rows_v.at[b], sem_in.at[b])

        def scatters(cc, b):
            return [pltpu.make_async_copy(rows_v.at[b], out_hbm.at[idx_v.at[kk, cc]], sem_out.at[b])
                    for kk in range(TOP_K)]

        load(0, 0).start()
        load(1, 1).start()

        @pl.loop(0, n_chunks, step=2)
        def _(c):
            for b in range(2):
                load(c + b, b).wait()
                for cp in scatters(c + b, b):
                    cp.start()
            for b in range(2):
                for cp in scatters(c + b, b):
                    cp.wait()

                @pl.when(c + 2 + b < n_chunks)
                def _():
                    load(c + 2 + b, b).start()

    return scatter_kernel(rows, dest_km.reshape(TOP_K, t // ch, ch))


def _sc_gather_rows(table, dest_km):
    _, w = table.shape
    t = dest_km.shape[1]
    ch = SC_GATHER_CHUNK
    n_cores, n_workers = _sc_workers()
    per_k = t // n_workers // ch
    n_chunks = TOP_K * per_k
    mesh = plsc.VectorSubcoreMesh(core_axis_name="c", subcore_axis_name="s")

    @functools.partial(
        pl.kernel, mesh=mesh, out_type=jax.ShapeDtypeStruct((TOP_K, t, w), table.dtype),
        scratch_types=[pltpu.VMEM((TOP_K, per_k, ch), I32), pltpu.VMEM((2, ch, w), table.dtype),
                       pltpu.SemaphoreType.DMA((2,)), pltpu.SemaphoreType.DMA((2,))],
        name="sc_combine_gather")
    def gather_kernel(table_hbm, dest_hbm, out_hbm, idx_v, rows_v, sem_in, sem_out):
        wid = lax.axis_index("s") * n_cores + lax.axis_index("c")
        first = wid * per_k
        for kk in range(TOP_K):
            pltpu.sync_copy(dest_hbm.at[kk, pl.ds(first, per_k)], idx_v.at[kk])

        def gather(cc, b):
            return pltpu.make_async_copy(table_hbm.at[idx_v.at[cc // per_k, cc % per_k]], rows_v.at[b], sem_in.at[b])

        def store(cc, b):
            return pltpu.make_async_copy(
                rows_v.at[b], out_hbm.at[cc // per_k, pl.ds((first + cc % per_k) * ch, ch)], sem_out.at[b])

        gather(0, 0).start()
        gather(1, 1).start()

        @pl.loop(0, n_chunks, step=2)
        def _(c):
            for b in range(2):
                gather(c + b, b).wait()
                store(c + b, b).start()
            for b in range(2):
                store(c + b, b).wait()

                @pl.when(c + 2 + b < n_chunks)
                def _():
                    gather(c + 2 + b, b).start()

    return gather_kernel(table, dest_km.reshape(TOP_K, t // ch, ch))


def _combine_kernel(x1_ref, yg_ref, route_ref, g_ref, o_ref, *, final_norm):
    tm = x1_ref.shape[0]
    moe_lo = jnp.zeros((tm, HALF), F32)
    moe_hi = jnp.zeros((tm, HALF), F32)
    route_t = jnp.concatenate([route_ref[...], jnp.zeros((LANES - ROUTE_ROWS, tm), F32)], axis=0).T
    for kk in range(TOP_K):
        gate = route_t[:, TOP_K + kk:TOP_K + kk + 1]
        y_lo, y_hi = _unpack_rows(yg_ref[kk])
        moe_lo = moe_lo + gate * y_lo
        moe_hi = moe_hi + gate * y_hi
    acc = x1_ref[...] + jnp.concatenate([moe_lo, moe_hi], axis=1)
    o_ref[...] = _rms(acc) * g_ref[...] if final_norm else acc


def _combine(x1, yg, route, g_final, final_norm):
    t = x1.shape[0]
    tm = TM_COMB
    return pl.pallas_call(
        functools.partial(_combine_kernel, final_norm=final_norm),
        grid=(t // tm,),
        in_specs=[pl.BlockSpec((tm, D_MODEL), lambda i: (i, 0)),
                  pl.BlockSpec((TOP_K, tm, HALF), lambda i: (0, i, 0)),
                  pl.BlockSpec((ROUTE_ROWS, tm), lambda i: (0, i)),
                  pl.BlockSpec((1, D_MODEL), lambda i: (0, 0))],
        out_specs=pl.BlockSpec((tm, D_MODEL), lambda i: (i, 0)),
        out_shape=jax.ShapeDtypeStruct((t, D_MODEL), F32),
        compiler_params=pltpu.CompilerParams(
            dimension_semantics=("arbitrary",), vmem_limit_bytes=VMEM_LIMIT_BYTES),
        name="combine",
    )(x1, yg, route, g_final)


def _prep_in_weights(w_in, w_uq, w_ukv):
    w_z = w_in[:, 0:512]
    w_xbc = w_in[:, 512:1536]
    w_dt = w_in[:, 1536:1544]
    w_cq = w_in[:, 1544:1800]
    w_ckv = w_in[:, 1800:1928]
    w_kr = w_in[:, 1928:1960]
    half = MLA_ROPE // 2
    zeros = lambda rows, width: jnp.zeros(rows + (width,), BF16)
    cat = lambda parts: jnp.concatenate([p.astype(BF16) for p in parts], axis=-1)
    d = (D_MODEL,)
    misc = [w_dt, zeros(d, MLA_NOPE - SSD_HEADS), w_kr, w_kr[:, half:], w_kr[:, :half]]
    w1 = cat([w_z, w_xbc, w_cq, w_ckv] + misc)

    wq3 = w_uq.reshape(MLA_Q_RANK, MLA_HEADS, MLA_QK)
    qh = (MLA_Q_RANK, MLA_HEADS)
    main = cat([wq3, zeros(qh, LANES - MLA_QK)])
    swap = cat([zeros(qh, MLA_NOPE), wq3[:, :, MLA_NOPE + half:], wq3[:, :, MLA_NOPE:MLA_NOPE + half],
                zeros(qh, LANES - MLA_QK)])
    wq = jnp.concatenate([main.reshape(MLA_Q_RANK, -1), swap.reshape(MLA_Q_RANK, -1)], axis=1)

    wkv3 = w_ukv.reshape(MLA_KV_RANK, MLA_HEADS, MLA_NOPE + MLA_V)
    kh = (MLA_KV_RANK, MLA_HEADS)
    kpart = cat([wkv3[:, :, :MLA_NOPE], zeros(kh, LANES - MLA_NOPE)])
    vpart = wkv3[:, :, MLA_NOPE:].astype(BF16)
    wkv = jnp.concatenate([kpart.reshape(MLA_KV_RANK, -1), vpart.reshape(MLA_KV_RANK, -1)], axis=1)
    return w1, wq, wkv


def _rope_inv_freq():
    inv_freq = ROPE_THETA ** (-jnp.arange(0, MLA_ROPE, 2, dtype=F32) / MLA_ROPE)
    return inv_freq[:, None]


def _pad_lanes(v, fill=0.0):
    return jnp.full((1, LANES), fill, F32).at[0, :v.shape[0]].set(v)


def kernel(x, positions, norm_mix_g, w_in, conv_w, conv_b, dt_bias, a_log, d_skip, ssd_norm_g, q_norm_g, w_uq, kv_norm_g, w_ukv, w_out, norm_ffn_g, w_router, b_router, w_gate_up, b_gate_up, w_down, b_down, norm_final_g):
    bsz, seqlen, d = x.shape
    t = bsz * seqlen
    depth = w_in.shape[0]
    x2 = x.reshape(t, d)
    pos_rows = positions.reshape(t // TM_PROJ, 1, TM_PROJ).astype(I32)
    invf = _rope_inv_freq()

    for l in range(depth):
        w1, wq, wkv = _prep_in_weights(w_in[l], w_uq[l], w_ukv[l])
        z, xbc, dtm, q, k, v = _inproj(
            x2, pos_rows, norm_mix_g[l][None, :], w1, q_norm_g[l][None, :], wq, kv_norm_g[l][None, :], wkv,
            invf, bsz, seqlen)
        y_ssd = _ssd(z, xbc, dtm, conv_w[l], conv_b[l][None, :], _pad_lanes(dt_bias[l]), _pad_lanes(a_log[l]),
                     jnp.repeat(d_skip[l], SSD_HEAD_DIM)[None, :], ssd_norm_g[l][None, :], bsz, seqlen)
        y_mla = _attention(q, k, v, bsz, seqlen).reshape(t, MLA_WIDTH)

        wr = jnp.zeros((d, LANES), F32).at[:, :N_EXPERTS].set(w_router[l])
        wr_hi = wr.astype(BF16)
        wr_lo = (wr - wr_hi.astype(F32)).astype(BF16)
        x1, h2p, route, cnt = _outproj(x2, y_ssd, y_mla, w_out[l], norm_ffn_g[l][None, :],
                                       jnp.concatenate([wr_hi, wr_lo], axis=1), _pad_lanes(b_router[l]))

        counts = cnt[:, 0].astype(I32)
        padded = ((counts + MOE_STEP - 1) // MOE_STEP) * MOE_STEP
        end_pad = jnp.cumsum(padded)
        start_pad = end_pad - padded
        n_slots = t * TOP_K + N_EXPERTS * MOE_STEP
        n_blocks = n_slots // MOE_STEP
        block_start = jnp.arange(n_blocks, dtype=I32) * MOE_STEP
        block_e = jnp.minimum(jnp.sum(block_start[:, None] >= end_pad[None, :], axis=1), N_EXPERTS - 1).astype(I32)
        eids = jnp.arange(N_EXPERTS, dtype=I32)
        block_hot = block_e[:, None] == eids[None, :]
        per_block = lambda table: jnp.sum(jnp.where(block_hot, table[None, :], 0), axis=1).astype(I32)
        block_valid = jnp.clip(per_block(counts) - (block_start - per_block(start_pad)), 0, MOE_STEP).astype(I32)
        used = counts > 0
        later_used = jnp.where((eids[None, :] > eids[:, None]) & used[None, :], eids[None, :], N_EXPERTS)
        next_used = jnp.min(later_used, axis=1)
        next_used = jnp.where(next_used < N_EXPERTS, next_used, -1).astype(I32)
        stage_slot = ((jnp.cumsum(used.astype(I32)) - 1) & 1).astype(I32)
        dest_km = _slot_plan(route, start_pad.astype(I32))

        xs = _sc_scatter_rows(h2p, dest_km, n_slots)
        n_used = end_pad[-1] // MOE_STEP
        block_src = jnp.minimum(jnp.arange(n_blocks, dtype=I32), n_used - 1).astype(I32)
        ys = _expert_ffn(block_e, block_valid, per_block(next_used), per_block(stage_slot), block_src, xs,
                         w_gate_up[l], b_gate_up[l][:, None, :], w_down[l], b_down[l][:, None, :])
        yg = _sc_gather_rows(ys, dest_km)
        x2 = _combine(x1, yg, route, norm_final_g[None, :], l == depth - 1)
    return x2.reshape(bsz, seqlen, d)
```
